```python
import math
import jax, jax.numpy as jnp
from jax import lax
import numpy as np

D_MODEL = 1024
BATCH = 8
SEQ = 8192
DEPTH = 2

N_MIXERS = 2
EXPAND = 2
E_WIDTH = EXPAND * D_MODEL
HEAD_DIM = 128
N_SLOTS = E_WIDTH // HEAD_DIM
DILATED_GROUPS = ((128, 1), (512, 4), (2048, 16))
N_GROUPS = 3
ROT_DIM = HEAD_DIM // 4
ROPE_THETA = 500000.0
BLOCK = 128
POOL_WINDOWS = (2, 4, 8, 16)
N_POOL = 4
POOL_CH = E_WIDTH // N_POOL
RMS_EPS = 1e-6
NEG_INF = -1e30
N_ATTN_LAYERS = (DEPTH + 1) // 2
N_POOL_LAYERS = DEPTH // 2

kernel_name = "hybrid_dilated_attn_multiscale_pool"


def rmsnorm(x, g):
    x32 = x.astype(jnp.float32)
    y = x32 * lax.rsqrt(jnp.mean(x32 * x32, axis=-1, keepdims=True) + RMS_EPS)
    return (y * g.astype(jnp.float32)).astype(x.dtype)


def rope_partial(t, cos, sin):
    t32 = t.astype(jnp.float32)
    half = ROT_DIM // 2
    t1 = t32[..., :half]
    t2 = t32[..., half:ROT_DIM]
    out = jnp.concatenate([t1 * cos - t2 * sin, t2 * cos + t1 * sin, t32[..., ROT_DIM:]], axis=-1)
    return out.astype(t.dtype)


def dilated_window_attention(q, k, v, dilation, w_sub):
    B, S, H, Dh = q.shape
    L = S // dilation
    nb = -(-L // BLOCK)
    Lp = nb * BLOCK
    N = B * dilation

    def to_sub(t):
        t = t.reshape(B, L, dilation, H, Dh).transpose(0, 2, 1, 3, 4).reshape(N, L, H, Dh)
        return jnp.pad(t, ((0, 0), (0, Lp - L), (0, 0), (0, 0)))

    def band_keys(t):
        tp = jnp.pad(t, ((0, 0), (BLOCK, 0), (0, 0), (0, 0))).reshape(N, nb + 1, BLOCK, H, Dh)
        return jnp.concatenate([tp[:, :-1], tp[:, 1:]], axis=2)

    qs = to_sub(q).reshape(N, nb, BLOCK, H, Dh)
    kw = band_keys(to_sub(k))
    vw = band_keys(to_sub(v))

    qi = jnp.arange(BLOCK)[:, None]
    kj = jnp.arange(2 * BLOCK)[None, :]
    dist = qi + BLOCK - kj
    band = (dist >= 0) & (dist <= w_sub)
    not_first = (jnp.arange(nb) > 0)[:, None, None]
    valid = band[None] & (not_first | (kj >= BLOCK)[None])

    scale = 1.0 / math.sqrt(Dh)
    s = jnp.einsum('nbqhd,nbkhd->nbhqk', qs, kw).astype(jnp.float32) * scale
    s = jnp.where(valid[None, :, None], s, NEG_INF)
    m = jnp.max(s, axis=-1, keepdims=True)
    p = jnp.exp(s - m)
    den = jnp.sum(p, axis=-1)
    o = jnp.einsum('nbhqk,nbkhd->nbqhd', p, vw.astype(jnp.float32))
    o = o / jnp.transpose(den, (0, 1, 3, 2))[..., None]
    lse = jnp.transpose(m[..., 0] + jnp.log(den), (0, 1, 3, 2))

    o = o.reshape(N, Lp, H, Dh)[:, :L].reshape(B, dilation, L, H, Dh)
    o = o.transpose(0, 2, 1, 3, 4).reshape(B, S, H, Dh)
    lse = lse.reshape(N, Lp, H)[:, :L].reshape(B, dilation, L, H)
    lse = lse.transpose(0, 2, 1, 3).reshape(B, S, H)
    return o, lse


def attention_mixer(xn, cos, sin, w_in, w_out):
    B, S, D = xn.shape
    qkv_cols = N_GROUPS * 3 * E_WIDTH
    w_qkv = w_in[:, :qkv_cols].reshape(D, N_GROUPS, 3, N_SLOTS, HEAD_DIM)
    z = xn @ w_in[:, qkv_cols:]
    outs, lses = [], []
    for g, (window, dil) in enumerate(DILATED_GROUPS):
        qkv = jnp.einsum('bsd,dchk->cbshk', xn, w_qkv[:, g])
        q = rope_partial(qkv[0], cos, sin)
        k = rope_partial(qkv[1], cos, sin)
        o, lse = dilated_window_attention(q, k, qkv[2], dil, window // dil)
        outs.append(o)
        lses.append(lse)
    wts = jax.nn.softmax(jnp.stack(lses, axis=0), axis=0)
    y = jnp.sum(wts[..., None] * jnp.stack(outs, axis=0), axis=0)
    y = y.reshape(B, S, E_WIDTH).astype(xn.dtype) * jax.nn.silu(z)
    return y @ w_out


def pooling_mixer(xn, w_in, w_grp, b_grp, scale, w_out):
    B, S, D = xn.shape
    uz = xn @ w_in
    u, z = uz[..., :E_WIDTH], uz[..., E_WIDTH:]
    ug = u.reshape(B, S, N_POOL, POOL_CH).astype(jnp.float32)
    c = jnp.cumsum(ug, axis=1)
    t = jnp.arange(S)
    parts = []
    for g, w in enumerate(POOL_WINDOWS):
        cg = c[:, :, g]
        lag = jnp.pad(cg[:, :S - w], ((0, 0), (w, 0), (0, 0)))
        cnt = jnp.minimum(t + 1, w).astype(jnp.float32)[None, :, None]
        parts.append((cg - lag) / cnt)
    pooled = (jnp.stack(parts, axis=2) - ug).astype(xn.dtype)
    h = jnp.einsum('bsgc,gcd->bsgd', pooled, w_grp) + b_grp
    h = h.reshape(B, S, E_WIDTH) * scale
    y = h * jax.nn.silu(z)
    return y @ w_out


def _fwd_setup_inputs(seed: int = 0) -> dict:
    key = jax.random.key(seed)
    ks = jax.random.split(key, 12)
    D, E = D_MODEL, E_WIDTH
    f32 = jnp.float32
    x = jax.random.normal(ks[0], (BATCH, SEQ, D), f32)
    offset = jax.random.randint(ks[1], (BATCH, 1), 0, 4096, dtype=jnp.int32)
    positions = (offset + jnp.arange(SEQ, dtype=jnp.int32)[None, :]).astype(jnp.int32)
    norm_pre = 1.0 + 0.1 * jax.random.normal(ks[2], (DEPTH, D), f32)
    norm_post = 1.0 + 0.1 * jax.random.normal(ks[3], (DEPTH, D), f32)
    attn_w_in = jax.random.normal(ks[4], (N_ATTN_LAYERS, D, N_GROUPS * 3 * E + E), f32) * D ** -0.5
    attn_w_out = jax.random.normal(ks[5], (N_ATTN_LAYERS, E, D), f32) * E ** -0.5
    pool_w_in = jax.random.normal(ks[6], (N_POOL_LAYERS, D, 2 * E), f32) * D ** -0.5
    pool_w_grp = jax.random.normal(ks[7], (N_POOL_LAYERS, N_POOL, POOL_CH, POOL_CH), f32) * POOL_CH ** -0.5
    pool_b_grp = 0.01 * jax.random.normal(ks[8], (N_POOL_LAYERS, N_POOL, POOL_CH), f32)
    pool_scale = 1.0 + 0.1 * jax.random.normal(ks[9], (N_POOL_LAYERS, E), f32)
    pool_w_out = jax.random.normal(ks[10], (N_POOL_LAYERS, E, D), f32) * E ** -0.5
    return {"x": x, "positions": positions, "norm_pre": norm_pre, "norm_post": norm_post,
            "attn_w_in": attn_w_in, "attn_w_out": attn_w_out,
            "pool_w_in": pool_w_in, "pool_w_grp": pool_w_grp, "pool_b_grp": pool_b_grp,
            "pool_scale": pool_scale, "pool_w_out": pool_w_out}


def _fwd_reference(x, positions, norm_pre, norm_post, attn_w_in, attn_w_out,
              pool_w_in, pool_w_grp, pool_b_grp, pool_scale, pool_w_out):
    inv_freq = ROPE_THETA ** (-jnp.arange(0, ROT_DIM, 2, dtype=jnp.float32) / ROT_DIM)
    ang = positions.astype(jnp.float32)[..., None] * inv_freq
    cos = jnp.cos(ang)[:, :, None, :]
    sin = jnp.sin(ang)[:, :, None, :]
    h = x
    for i in range(DEPTH):
        xn = rmsnorm(h, norm_pre[i])
        j = i // N_MIXERS
        if i % N_MIXERS == 0:
            y = attention_mixer(xn, cos, sin, attn_w_in[j], attn_w_out[j])
        else:
            y = pooling_mixer(xn, pool_w_in[j], pool_w_grp[j], pool_b_grp[j],
                              pool_scale[j], pool_w_out[j])
        h = h + rmsnorm(y, norm_post[i])
    return h


import jax as _jax
import jax.numpy as _jnp

TWIN_FORMAT = 'train_step'
FWD_PARAMS = ['x', 'positions', 'norm_pre', 'norm_post', 'attn_w_in', 'attn_w_out', 'pool_w_in', 'pool_w_grp', 'pool_b_grp', 'pool_scale', 'pool_w_out']
TWIN_WEIGHTS = ['norm_pre', 'norm_post', 'attn_w_in', 'attn_w_out', 'pool_w_in', 'pool_w_grp', 'pool_b_grp', 'pool_scale', 'pool_w_out']
TWIN_DIFF_INPUT = 'x'
TWIN_INPUTS = ['x', 'positions', 'norm_pre', 'norm_post', 'attn_w_in', 'attn_w_out', 'pool_w_in', 'pool_w_grp', 'pool_b_grp', 'pool_scale', 'pool_w_out', 'loss_target', 'm_norm_pre', 'm_norm_post', 'm_attn_w_in', 'm_attn_w_out', 'm_pool_w_in', 'm_pool_w_grp', 'm_pool_b_grp', 'm_pool_scale', 'm_pool_w_out', 'v_norm_pre', 'v_norm_post', 'v_attn_w_in', 'v_attn_w_out', 'v_pool_w_in', 'v_pool_w_grp', 'v_pool_b_grp', 'v_pool_scale', 'v_pool_w_out']
TWIN_OUTPUTS = ['loss', 'grad_x', 'grad_norm_pre', 'grad_norm_post', 'grad_attn_w_in', 'grad_attn_w_out', 'grad_pool_w_in', 'grad_pool_w_grp', 'grad_pool_b_grp', 'grad_pool_scale', 'grad_pool_w_out', 'delta_norm_pre', 'delta_norm_post', 'delta_attn_w_in', 'delta_attn_w_out', 'delta_pool_w_in', 'delta_pool_w_grp', 'delta_pool_b_grp', 'delta_pool_scale', 'delta_pool_w_out', 'new_m_norm_pre', 'new_m_norm_post', 'new_m_attn_w_in', 'new_m_attn_w_out', 'new_m_pool_w_in', 'new_m_pool_w_grp', 'new_m_pool_b_grp', 'new_m_pool_scale', 'new_m_pool_w_out', 'new_v_norm_pre', 'new_v_norm_post', 'new_v_attn_w_in', 'new_v_attn_w_out', 'new_v_pool_w_in', 'new_v_pool_w_grp', 'new_v_pool_b_grp', 'new_v_pool_scale', 'new_v_pool_w_out']
TWIN_LEAF_KINDS = {'loss': 'loss', 'grad_x': 'grad_x', 'grad_norm_pre': 'grad_w', 'grad_norm_post': 'grad_w', 'grad_attn_w_in': 'grad_w', 'grad_attn_w_out': 'grad_w', 'grad_pool_w_in': 'grad_w', 'grad_pool_w_grp': 'grad_w', 'grad_pool_b_grp': 'grad_w', 'grad_pool_scale': 'grad_w', 'grad_pool_w_out': 'grad_w', 'delta_norm_pre': 'delta_w', 'delta_norm_post': 'delta_w', 'delta_attn_w_in': 'delta_w', 'delta_attn_w_out': 'delta_w', 'delta_pool_w_in': 'delta_w', 'delta_pool_w_grp': 'delta_w', 'delta_pool_b_grp': 'delta_w', 'delta_pool_scale': 'delta_w', 'delta_pool_w_out': 'delta_w', 'new_m_norm_pre': 'new_m', 'new_m_norm_post': 'new_m', 'new_m_attn_w_in': 'new_m', 'new_m_attn_w_out': 'new_m', 'new_m_pool_w_in': 'new_m', 'new_m_pool_w_grp': 'new_m', 'new_m_pool_b_grp': 'new_m', 'new_m_pool_scale': 'new_m', 'new_m_pool_w_out': 'new_m', 'new_v_norm_pre': 'new_v', 'new_v_norm_post': 'new_v', 'new_v_attn_w_in': 'new_v', 'new_v_attn_w_out': 'new_v', 'new_v_pool_w_in': 'new_v', 'new_v_pool_w_grp': 'new_v', 'new_v_pool_b_grp': 'new_v', 'new_v_pool_scale': 'new_v', 'new_v_pool_w_out': 'new_v'}


def _forward(args):
    return _fwd_reference(*[args[k] for k in FWD_PARAMS])


def _output_shape():
    def fwd():
        inp = _fwd_setup_inputs(0)
        return _fwd_reference(*[inp[k] for k in FWD_PARAMS])
    out = _jax.eval_shape(fwd)
    return out.shape, out.dtype

N_MICROBATCH = 1
ADAM_LR = 0.001
ADAM_B1 = 0.9
ADAM_B2 = 0.999
ADAM_EPS = 1e-08
ADAM_WD = 0.01
ADAM_STEP = 10
PER_EXAMPLE_BATCH_AXIS = {'x': 0, 'positions': 0, 'loss_target': 0}
SHARED_INPUTS = []
_WEIGHT_DTYPES = {'norm_pre': _jnp.float32, 'norm_post': _jnp.float32, 'attn_w_in': _jnp.float32, 'attn_w_out': _jnp.float32, 'pool_w_in': _jnp.float32, 'pool_w_grp': _jnp.float32, 'pool_b_grp': _jnp.float32, 'pool_scale': _jnp.float32, 'pool_w_out': _jnp.float32}
MOMENT_SCALE = {'norm_pre': 1.068228e+00, 'norm_post': 6.396784e+01, 'attn_w_in': 2.882687e-01, 'attn_w_out': 6.896083e-01, 'pool_w_in': 4.096024e-01, 'pool_w_grp': 4.959342e-01, 'pool_b_grp': 2.037444e+00, 'pool_scale': 4.863691e-01, 'pool_w_out': 7.255509e-01}


def _to_microbatches(a, axis):
    t = _jnp.moveaxis(a, axis, 0)
    t = t.reshape((N_MICROBATCH, t.shape[0] // N_MICROBATCH) + t.shape[1:])
    return _jnp.moveaxis(t, 1, axis + 1)


def setup_inputs(seed: int = 0) -> dict:
    inp = _fwd_setup_inputs(seed)
    key = _jax.random.fold_in(_jax.random.key(seed), 7919)
    shape, _ = _output_shape()
    out = dict(inp)
    out["loss_target"] = _jax.random.normal(_jax.random.fold_in(key, 0), shape, _jnp.float32)
    for i, name in enumerate(TWIN_WEIGHTS):
        w = inp[name].astype(_jnp.float32)
        if MOMENT_SCALE is None:
            s = _jnp.sqrt(_jnp.mean(_jnp.square(w)) + 1e-30)
        else:
            s = MOMENT_SCALE[name]
        km, kv = _jax.random.split(_jax.random.fold_in(key, i + 1))
        out[name] = w
        out["m_" + name] = s * _jax.random.normal(km, w.shape, _jnp.float32)
        out["v_" + name] = (s * s) * _jax.random.uniform(kv, w.shape, _jnp.float32, 0.5, 1.5)
    if N_MICROBATCH > 1:
        for name, axis in PER_EXAMPLE_BATCH_AXIS.items():
            out[name] = _to_microbatches(out[name], axis)
    return {'x': out['x'], 'positions': out['positions'], 'norm_pre': out['norm_pre'], 'norm_post': out['norm_post'], 'attn_w_in': out['attn_w_in'], 'attn_w_out': out['attn_w_out'], 'pool_w_in': out['pool_w_in'], 'pool_w_grp': out['pool_w_grp'], 'pool_b_grp': out['pool_b_grp'], 'pool_scale': out['pool_scale'], 'pool_w_out': out['pool_w_out'], 'loss_target': out['loss_target'], 'm_norm_pre': out['m_norm_pre'], 'm_norm_post': out['m_norm_post'], 'm_attn_w_in': out['m_attn_w_in'], 'm_attn_w_out': out['m_attn_w_out'], 'm_pool_w_in': out['m_pool_w_in'], 'm_pool_w_grp': out['m_pool_w_grp'], 'm_pool_b_grp': out['m_pool_b_grp'], 'm_pool_scale': out['m_pool_scale'], 'm_pool_w_out': out['m_pool_w_out'], 'v_norm_pre': out['v_norm_pre'], 'v_norm_post': out['v_norm_post'], 'v_attn_w_in': out['v_attn_w_in'], 'v_attn_w_out': out['v_attn_w_out'], 'v_pool_w_in': out['v_pool_w_in'], 'v_pool_w_grp': out['v_pool_w_grp'], 'v_pool_b_grp': out['v_pool_b_grp'], 'v_pool_scale': out['v_pool_scale'], 'v_pool_w_out': out['v_pool_w_out']}


def _loss(weights, diff, rest, loss_target):
    with _jax.named_scope("forward"):
        args = {**rest, TWIN_DIFF_INPUT: diff, **{k: w.astype(_WEIGHT_DTYPES[k]) for k, w in weights.items()}}
        y = _forward(args)
    with _jax.named_scope("loss_head"):
        err = _jnp.square(y.astype(_jnp.float32) - loss_target)
        return 0.5 * _jnp.sum(_jnp.mean(err, axis=-1)) if err.ndim else 0.5 * err


def _adamw(w, g, m, v):
    m = ADAM_B1 * m + (1.0 - ADAM_B1) * g
    v = ADAM_B2 * v + (1.0 - ADAM_B2) * _jnp.square(g)
    m_hat = m / (1.0 - ADAM_B1 ** ADAM_STEP)
    v_hat = v / (1.0 - ADAM_B2 ** ADAM_STEP)
    delta = -ADAM_LR * (m_hat / (_jnp.sqrt(v_hat) + ADAM_EPS) + ADAM_WD * w)
    return delta, m, v


def reference(x, positions, norm_pre, norm_post, attn_w_in, attn_w_out, pool_w_in, pool_w_grp, pool_b_grp, pool_scale, pool_w_out, loss_target, m_norm_pre, m_norm_post, m_attn_w_in, m_attn_w_out, m_pool_w_in, m_pool_w_grp, m_pool_b_grp, m_pool_scale, m_pool_w_out, v_norm_pre, v_norm_post, v_attn_w_in, v_attn_w_out, v_pool_w_in, v_pool_w_grp, v_pool_b_grp, v_pool_scale, v_pool_w_out):
    given = dict(x=x, positions=positions, norm_pre=norm_pre, norm_post=norm_post, attn_w_in=attn_w_in, attn_w_out=attn_w_out, pool_w_in=pool_w_in, pool_w_grp=pool_w_grp, pool_b_grp=pool_b_grp, pool_scale=pool_scale, pool_w_out=pool_w_out, loss_target=loss_target, m_norm_pre=m_norm_pre, m_norm_post=m_norm_post, m_attn_w_in=m_attn_w_in, m_attn_w_out=m_attn_w_out, m_pool_w_in=m_pool_w_in, m_pool_w_grp=m_pool_w_grp, m_pool_b_grp=m_pool_b_grp, m_pool_scale=m_pool_scale, m_pool_w_out=m_pool_w_out, v_norm_pre=v_norm_pre, v_norm_post=v_norm_post, v_attn_w_in=v_attn_w_in, v_attn_w_out=v_attn_w_out, v_pool_w_in=v_pool_w_in, v_pool_w_grp=v_pool_w_grp, v_pool_b_grp=v_pool_b_grp, v_pool_scale=v_pool_scale, v_pool_w_out=v_pool_w_out)
    weights = {n: given[n] for n in TWIN_WEIGHTS}
    shared = {n: given[n] for n in SHARED_INPUTS}
    per_example = {n: given[n] for n in ['x', 'positions']}
    grad_fn = _jax.value_and_grad(_loss, argnums=(0, 1))

    def one_microbatch(ex, loss_target):
        ex = dict(ex)
        diff = ex.pop(TWIN_DIFF_INPUT)
        return grad_fn(weights, diff, {**shared, **ex}, loss_target)

    if N_MICROBATCH == 1:
        loss, (grad_w, grad_x) = one_microbatch(per_example, given["loss_target"])
    else:
        def body(carry, xs):
            loss_sum, grad_sum = carry
            l_k, (gw_k, gx_k) = one_microbatch(xs[0], xs[1])
            with _jax.named_scope("update"):
                return (loss_sum + l_k, _jax.tree.map(_jnp.add, grad_sum, gw_k)), gx_k

        init = (_jnp.zeros((), _jnp.float32), _jax.tree.map(_jnp.zeros_like, weights))
        (loss, grad_w), grad_x = _jax.lax.scan(body, init, (per_example, given["loss_target"]))
    with _jax.named_scope("update"):
        delta_w, new_m, new_v = {}, {}, {}
        for n in TWIN_WEIGHTS:
            delta_w[n], new_m[n], new_v[n] = _adamw(weights[n], grad_w[n], given["m_" + n], given["v_" + n])
    return (loss, grad_x, *[grad_w[n] for n in TWIN_WEIGHTS], *[delta_w[n] for n in TWIN_WEIGHTS],
            *[new_m[n] for n in TWIN_WEIGHTS], *[new_v[n] for n in TWIN_WEIGHTS])
```

```python
import functools
import math

import numpy as np
import jax
import jax.numpy as jnp
from jax import lax
from jax.experimental import pallas as pl
from jax.experimental.pallas import tpu as pltpu

F32 = jnp.float32
BF16 = jnp.bfloat16
SDS = jax.ShapeDtypeStruct

N_DEV = 8
D = 1024
E = 2048
HD = 128
NH = E // HD
DIL = (1, 4, 16)
QB = 128
SEG = 3 * E
W_IN_COLS = 3 * SEG + E
PW = 4 * SEG
W_SHARD = W_IN_COLS // N_DEV
POOL_WIN = (2, 4, 8, 16)
PC = E // 4
EPS = 1e-6
NEG = -1e30
SCALE = 1.0 / math.sqrt(HD)
LR, B1, B2, ADAM_EPS, WD, STEP = 0.001, 0.9, 0.999, 1e-08, 0.01, 10
MIB = 1024 * 1024
ANY = pl.BlockSpec(memory_space=pl.ANY)
MESH = pl.DeviceIdType.MESH


def _cp(sem, mb):
    return pltpu.CompilerParams(dimension_semantics=sem, vmem_limit_bytes=mb * MIB)


def _dot(a, b):
    return jnp.dot(a, b, preferred_element_type=F32)


def _dot_nt(a, b):
    return lax.dot_general(a, b, (((1,), (1,)), ((), ())), preferred_element_type=F32)


def _rms(h):
    return lax.rsqrt(jnp.mean(h * h, axis=-1, keepdims=True) + EPS)


def _row_tile(R, C, budget):
    tr = R
    while tr * C * 4 > budget and tr % 16 == 0:
        tr //= 2
    return tr


def _fold8(t):
    return t.reshape(t.shape[0] // 8, 8, t.shape[1]).sum(axis=0)


def _sigmoid(z):
    return 1.0 / (1.0 + jnp.exp(-z))


def _rope(t, c, s1, s2):
    return t * c + pltpu.roll(t, HD - 16, 1) * s1 + pltpu.roll(t, 16, 1) * s2


def _unrope(t, c, s1, s2):
    return t * c - pltpu.roll(t, HD - 16, 1) * s1 - pltpu.roll(t, 16, 1) * s2


def _mesh_pos():
    return lax.axis_index("x"), lax.axis_index("y"), lax.axis_index("c")


def all_gather(arrs, name):
    n = len(arrs)

    def body(*refs):
        ins, outs = refs[:n], refs[n:2 * n]
        send_sems, recv_sems, local_sems = refs[2 * n:]
        x, y, c = _mesh_pos()
        me, sib = (x, y, c), (x, y, 1 - c)
        chips = [(1 - x, y), (x, 1 - y), (1 - x, 1 - y)]

        def slot(p):
            return 4 * p[0] + 2 * p[1] + p[2]

        def copy(a, k, block, to, src=None):
            dst = outs[a].at[slot(block)]
            return pltpu.make_async_remote_copy(
                src_ref=dst if src is None else src, dst_ref=dst,
                send_sem=send_sems.at[a, k], recv_sem=recv_sems.at[a, k],
                device_id=to, device_id_type=MESH)

        mine = [pltpu.make_async_copy(ins[a], outs[a].at[slot(me)], local_sems.at[a]) for a in range(n)]
        for cp in mine:
            cp.start()
        first = []
        for a in range(n):
            first.append(copy(a, 0, me, sib, src=ins[a]))
            for j, chip in enumerate(chips):
                first.append(copy(a, 1 + j, me, (*chip, c), src=ins[a]))
        for cp in first:
            cp.start()
        passed = []
        for j, chip in enumerate(chips):
            for a in range(n):
                copy(a, 1 + j, (*chip, c), me).wait_recv()
                fw = copy(a, 4 + j, (*chip, c), sib)
                fw.start()
                passed.append(fw)
        for a in range(n):
            copy(a, 0, sib, me).wait_recv()
        for j, chip in enumerate(chips):
            for a in range(n):
                copy(a, 4 + j, (*chip, 1 - c), me).wait_recv()
        for cp in first + passed:
            cp.wait_send()
        for cp in mine:
            cp.wait()

    return pl.pallas_call(
        body, name=name,
        out_shape=[SDS((N_DEV,) + a.shape, a.dtype) for a in arrs],
        in_specs=[ANY] * n, out_specs=[ANY] * n,
        scratch_shapes=[pltpu.SemaphoreType.DMA((n, 7)), pltpu.SemaphoreType.DMA((n, 7)),
                        pltpu.SemaphoreType.DMA((n,))],
    )(*arrs)


def rs_pair(arrs, name):
    n = len(arrs)

    def body(*refs):
        ins, outs = refs[:n], refs[n:2 * n]
        send_sems, recv_sems = refs[2 * n:]
        x, y, c = _mesh_pos()
        cps = []
        for a in range(n):
            for q in range(4):
                cps.append(pltpu.make_async_remote_copy(
                    src_ref=ins[a].at[2 * q + (1 - c)], dst_ref=outs[a].at[q],
                    send_sem=send_sems.at[a, q], recv_sem=recv_sems.at[a, q],
                    device_id=(x, y, 1 - c), device_id_type=MESH))
        for cp in cps:
            cp.start()
        for cp in cps:
            cp.wait()

    return pl.pallas_call(
        body, name=name,
        out_shape=[SDS((4,) + a.shape[1:], a.dtype) for a in arrs],
        in_specs=[ANY] * n, out_specs=[ANY] * n,
        scratch_shapes=[pltpu.SemaphoreType.DMA((n, 4)), pltpu.SemaphoreType.DMA((n, 4))],
    )(*arrs)


def rs_chips(parts, name):
    n = len(parts)

    def body(*refs):
        ins, outs = refs[:n], refs[n:2 * n]
        send_sems, recv_sems, local_sems = refs[2 * n:]
        x, y, c = _mesh_pos()
        mychip = 2 * x + y
        chips = [(1 - x, y), (x, 1 - y), (1 - x, 1 - y)]
        mine = [pltpu.make_async_copy(ins[a].at[mychip], outs[a].at[mychip], local_sems.at[a]) for a in range(n)]
        for cp in mine:
            cp.start()
        cps = []
        for a in range(n):
            for j, chip in enumerate(chips):
                q = 2 * chip[0] + chip[1]
                cps.append(pltpu.make_async_remote_copy(
                    src_ref=ins[a].at[q], dst_ref=outs[a].at[mychip],
                    send_sem=send_sems.at[a, j], recv_sem=recv_sems.at[a, j],
                    device_id=(*chip, c), device_id_type=MESH))
        for cp in cps:
            cp.start()
        for cp in cps:
            cp.wait()
        for cp in mine:
            cp.wait()

    return pl.pallas_call(
        body, name=name,
        out_shape=[SDS(a.shape, a.dtype) for a in parts],
        in_specs=[ANY] * n, out_specs=[ANY] * n,
        scratch_shapes=[pltpu.SemaphoreType.DMA((n, 3)), pltpu.SemaphoreType.DMA((n, 3)),
                        pltpu.SemaphoreType.DMA((n,))],
    )(*parts)


def pair_add(full, sib, cidx, name):
    _, R, C = full.shape
    tr = _row_tile(R, C, 512 * 1024)

    def body(c_ref, a_ref, b_ref, o_ref):
        o_ref[...] = a_ref[...] + b_ref[...]

    return pl.pallas_call(
        body, name=name,
        grid_spec=pltpu.PrefetchScalarGridSpec(
            num_scalar_prefetch=1, grid=(4, R // tr),
            in_specs=[pl.BlockSpec((None, tr, C), lambda q, i, cr: (2 * q + cr[0], i, 0)),
                      pl.BlockSpec((None, tr, C), lambda q, i, cr: (q, i, 0))],
            out_specs=pl.BlockSpec((None, tr, C), lambda q, i, cr: (q, i, 0))),
        out_shape=SDS((4, R, C), F32),
        compiler_params=_cp(("parallel", "parallel"), 32),
    )(cidx, full, sib)


def _adam_math(w, g, m, v):
    m2 = B1 * m + (1.0 - B1) * g
    v2 = B2 * v + (1.0 - B2) * (g * g)
    m_hat = m2 / (1.0 - B1 ** STEP)
    v_hat = v2 / (1.0 - B2 ** STEP)
    delta = -LR * (m_hat / (jnp.sqrt(v_hat) + ADAM_EPS) + WD * w)
    return delta, m2, v2


def adamw_sum(recv, w, m, v, name):
    K, R, C = recv.shape
    tr = _row_tile(R, C, 256 * 1024)

    def body(r_ref, w_ref, m_ref, v_ref, g_ref, d_ref, m2_ref, v2_ref):
        g = r_ref[0]
        for k in range(1, K):
            g = g + r_ref[k]
        delta, m2, v2 = _adam_math(w_ref[...], g, m_ref[...], v_ref[...])
        g_ref[...] = g
        d_ref[...] = delta
        m2_ref[...] = m2
        v2_ref[...] = v2

    tile = pl.BlockSpec((tr, C), lambda i: (i, 0))
    return pl.pallas_call(
        body, name=name, grid=(R // tr,),
        in_specs=[pl.BlockSpec((K, tr, C), lambda i: (0, i, 0)), tile, tile, tile],
        out_specs=[tile] * 4, out_shape=[SDS((R, C), F32)] * 4,
        compiler_params=_cp(("parallel",), 32),
    )(recv, w, m, v)


def adamw_small(g, w, m, v, name):
    def body(g_ref, w_ref, m_ref, v_ref, d_ref, m2_ref, v2_ref):
        delta, m2, v2 = _adam_math(w_ref[...], g_ref[...], m_ref[...], v_ref[...])
        d_ref[...] = delta
        m2_ref[...] = m2
        v2_ref[...] = v2

    return pl.pallas_call(body, name=name, out_shape=[SDS(w.shape, F32)] * 3)(g, w, m, v)


def sum_slots(a, name):
    K = a.shape[0]

    def body(a_ref, o_ref):
        t = a_ref[0]
        for k in range(1, K):
            t = t + a_ref[k]
        o_ref[...] = t

    return pl.pallas_call(body, name=name, out_shape=SDS(a.shape[1:], F32))(a)


def norm_pre0(x, g):
    S = x.shape[0]
    ts = 512

    def body(x_ref, g_ref, o_ref, ot_ref):
        h = x_ref[...]
        xn = h * _rms(h) * g_ref[...]
        o_ref[...] = xn.astype(BF16)
        ot_ref[...] = xn.T.astype(BF16)

    return pl.pallas_call(
        body, name="norm_pre0", grid=(S // ts,),
        in_specs=[pl.BlockSpec((ts, D), lambda i: (i, 0)), pl.BlockSpec((1, D), lambda i: (0, 0))],
        out_specs=[pl.BlockSpec((ts, D), lambda i: (i, 0)), pl.BlockSpec((D, ts), lambda i: (0, i))],
        out_shape=[SDS((S, D), BF16), SDS((D, S), BF16)],
        compiler_params=_cp(("parallel",), 32),
    )(x, g)


def post0_pre1(x, a0, g_post, g_pre):
    S = x.shape[0]
    ts = 512

    def body(x_ref, a_ref, gp_ref, gn_ref, h_ref, o_ref, ot_ref):
        a = a_ref[...]
        h1 = x_ref[...] + a * _rms(a) * gp_ref[...]
        h_ref[...] = h1
        xn = h1 * _rms(h1) * gn_ref[...]
        o_ref[...] = xn.astype(BF16)
        ot_ref[...] = xn.T.astype(BF16)

    row = pl.BlockSpec((ts, D), lambda i: (i, 0))
    vec = pl.BlockSpec((1, D), lambda i: (0, 0))
    return pl.pallas_call(
        body, name="post0_pre1", grid=(S // ts,),
        in_specs=[row, row, vec, vec],
        out_specs=[row, row, pl.BlockSpec((D, ts), lambda i: (0, i))],
        out_shape=[SDS((S, D), F32), SDS((S, D), BF16), SDS((D, S), BF16)],
        compiler_params=_cp(("parallel",), 40),
    )(x, a0, g_post, g_pre)


def post1_loss(h1, a1, target, g_post):
    S = h1.shape[0]
    ts = 512

    def body(h_ref, a_ref, t_ref, g_ref, dh_ref, da_ref, loss_ref, dg_ref):
        @pl.when(pl.program_id(0) == 0)
        def _():
            loss_ref[...] = jnp.zeros_like(loss_ref)
            dg_ref[...] = jnp.zeros_like(dg_ref)

        a = a_ref[...]
        g = g_ref[...]
        rp = _rms(a)
        yhat = a * rp
        e = h_ref[...] + yhat * g - t_ref[...]
        loss_ref[...] += _fold8(e * e)
        dh = e * (1.0 / D)
        dh_ref[...] = dh
        dg_ref[...] += _fold8(dh * yhat)
        dyh = dh * g
        da = rp * (dyh - yhat * jnp.mean(dyh * yhat, axis=-1, keepdims=True))
        da_ref[...] = da.astype(BF16)

    row = pl.BlockSpec((ts, D), lambda i: (i, 0))
    acc = pl.BlockSpec((8, D), lambda i: (0, 0))
    return pl.pallas_call(
        body, name="post1_loss", grid=(S // ts,),
        in_specs=[row, row, row, pl.BlockSpec((1, D), lambda i: (0, 0))],
        out_specs=[row, row, acc, acc],
        out_shape=[SDS((S, D), F32), SDS((S, D), BF16), SDS((8, D), F32), SDS((8, D), F32)],
        compiler_params=_cp(("arbitrary",), 40),
    )(h1, a1, target, g_post)


def mid_bwd(dxn1, dh2, h1, a0, g_pre1, g_post0):
    S = h1.shape[0]
    ts = 512

    def body(dx_ref, dh2_ref, h_ref, a_ref, gn_ref, gp_ref, dh1_ref, da_ref, dgn_ref, dgp_ref):
        @pl.when(pl.program_id(0) == 0)
        def _():
            dgn_ref[...] = jnp.zeros_like(dgn_ref)
            dgp_ref[...] = jnp.zeros_like(dgp_ref)

        h = h_ref[...]
        r1 = _rms(h)
        xhat = h * r1
        dxn = dx_ref[...]
        dgn_ref[...] += _fold8(dxn * xhat)
        dxh = dxn * gn_ref[...]
        dh1 = dh2_ref[...] + r1 * (dxh - xhat * jnp.mean(dxh * xhat, axis=-1, keepdims=True))
        dh1_ref[...] = dh1
        a = a_ref[...]
        rp = _rms(a)
        yhat = a * rp
        dgp_ref[...] += _fold8(dh1 * yhat)
        dyh = dh1 * gp_ref[...]
        da = rp * (dyh - yhat * jnp.mean(dyh * yhat, axis=-1, keepdims=True))
        da_ref[...] = da.astype(BF16)

    row = pl.BlockSpec((ts, D), lambda i: (i, 0))
    vec = pl.BlockSpec((1, D), lambda i: (0, 0))
    acc = pl.BlockSpec((8, D), lambda i: (0, 0))
    return pl.pallas_call(
        body, name="mid_bwd", grid=(S // ts,),
        in_specs=[row, row, row, row, vec, vec],
        out_specs=[row, row, acc, acc],
        out_shape=[SDS((S, D), F32), SDS((S, D), BF16), SDS((8, D), F32), SDS((8, D), F32)],
        compiler_params=_cp(("arbitrary",), 48),
    )(dxn1, dh2, h1, a0, g_pre1, g_post0)


def pre0_bwd(dxn0, dh1, x, g_pre0):
    S = x.shape[0]
    ts = 512

    def body(dx_ref, dh_ref, x_ref, g_ref, gx_ref, dg_ref):
        @pl.when(pl.program_id(0) == 0)
        def _():
            dg_ref[...] = jnp.zeros_like(dg_ref)

        h = x_ref[...]
        r = _rms(h)
        xhat = h * r
        dxn = dx_ref[...]
        dg_ref[...] += _fold8(dxn * xhat)
        dxh = dxn * g_ref[...]
        gx_ref[...] = dh_ref[...] + r * (dxh - xhat * jnp.mean(dxh * xhat, axis=-1, keepdims=True))

    row = pl.BlockSpec((ts, D), lambda i: (i, 0))
    return pl.pallas_call(
        body, name="pre0_bwd", grid=(S // ts,),
        in_specs=[row, row, row, pl.BlockSpec((1, D), lambda i: (0, 0))],
        out_specs=[row, pl.BlockSpec((8, D), lambda i: (0, 0))],
        out_shape=[SDS((S, D), F32), SDS((8, D), F32)],
        compiler_params=_cp(("arbitrary",), 40),
    )(dxn0, dh1, x, g_pre0)


def mm_qkvz(xn, w8, ctab, s1tab, s2tab):
    S = xn.shape[0]
    tm = 1024
    nchunk = W_SHARD // 256

    def body(a_ref, b_ref, c_ref, s1_ref, s2_ref, o_ref):
        j = pl.program_id(0)

        def chunk(cidx, carry):
            col = pl.multiple_of(cidx * 256, 256)
            r = _dot(a_ref[...], b_ref[:, pl.ds(col, 256)])
            for half in range(2):
                t = r[:, half * HD:(half + 1) * HD]
                head = j * (W_SHARD // HD) + cidx * 2 + half
                rot = jnp.logical_and(head < 3 * SEG // HD, (head % (SEG // HD)) < 2 * NH)
                dst = pl.ds(pl.multiple_of(col + half * HD, HD), HD)

                @pl.when(rot)
                def _():
                    o_ref[:, dst] = _rope(t, c_ref[...], s1_ref[...], s2_ref[...]).astype(BF16)

                @pl.when(jnp.logical_not(rot))
                def _():
                    o_ref[:, dst] = t.astype(BF16)
            return carry

        lax.fori_loop(0, nchunk, chunk, 0)

    tab = pl.BlockSpec((tm, HD), lambda j, i: (i, 0))
    return pl.pallas_call(
        body, name="mm_qkvz", grid=(N_DEV, S // tm),
        in_specs=[pl.BlockSpec((tm, D), lambda j, i: (i, 0)),
                  pl.BlockSpec((None, D, W_SHARD), lambda j, i: (j, 0, 0)), tab, tab, tab],
        out_specs=pl.BlockSpec((tm, W_SHARD), lambda j, i: (i, j)),
        out_shape=SDS((S, PW), BF16),
        compiler_params=_cp(("parallel", "parallel"), 48),
    )(xn, w8, ctab, s1tab, s2tab)


def mm_rows(a, b, name, out_dtype, tm=1024):
    M, K = a.shape
    N = b.shape[1]

    def body(a_ref, b_ref, o_ref):
        def chunk(cidx, carry):
            col = pl.ds(pl.multiple_of(cidx * 256, 256), 256)
            o_ref[:, col] = _dot(a_ref[...], b_ref[:, col]).astype(out_dtype)
            return carry

        lax.fori_loop(0, N // 256, chunk, 0)

    return pl.pallas_call(
        body, name=name, grid=(M // tm,),
        in_specs=[pl.BlockSpec((tm, K), lambda i: (i, 0)), pl.BlockSpec((K, N), lambda i: (0, 0))],
        out_specs=pl.BlockSpec((tm, N), lambda i: (i, 0)),
        out_shape=SDS((M, N), out_dtype),
        compiler_params=_cp(("parallel",), 48),
    )(a, b)


def mm_uz(xn, w8):
    S = xn.shape[0]
    tm = 512
    bw = w8.shape[2]

    def body(a_ref, b_ref, o_ref):
        def blk(dv, carry):
            for half in range(bw // 256):
                col = pl.ds(pl.multiple_of(dv * bw + half * 256, 256), 256)
                o_ref[:, col] = _dot(a_ref[...], b_ref[dv, :, half * 256:(half + 1) * 256])
            return carry

        lax.fori_loop(0, N_DEV, blk, 0)

    return pl.pallas_call(
        body, name="mm_uz", grid=(S // tm,),
        in_specs=[pl.BlockSpec((tm, D), lambda i: (i, 0)), pl.BlockSpec((N_DEV, D, bw), lambda i: (0, 0, 0))],
        out_specs=pl.BlockSpec((tm, N_DEV * bw), lambda i: (i, 0)),
        out_shape=SDS((S, N_DEV * bw), F32),
        compiler_params=_cp(("parallel",), 48),
    )(xn, w8)


def mm_acc(a, b, name, *, grid, a_spec, b_spec, o_spec, o_shape, acc_shape, write, vmem=48):
    nk = grid[-1]

    def body(a_ref, b_ref, o_ref, acc_ref):
        k = pl.program_id(len(grid) - 1)

        @pl.when(k == 0)
        def _():
            acc_ref[...] = jnp.zeros_like(acc_ref)

        acc_ref[...] += _dot(a_ref[...], b_ref[...])

        @pl.when(k == nk - 1)
        def _():
            write(o_ref, acc_ref)

    return pl.pallas_call(
        body, name=name, grid=grid, in_specs=[a_spec, b_spec], out_specs=o_spec, out_shape=o_shape,
        scratch_shapes=[pltpu.VMEM(acc_shape, F32)],
        compiler_params=_cp(("parallel",) * (len(grid) - 1) + ("arbitrary",), vmem),
    )(a, b)


def _write_plain(o_ref, acc_ref):
    o_ref[...] = acc_ref[...]


def mm_wgrad_rows(at, b, name):
    M, S = at.shape
    N = b.shape[1]
    tm, tk = 1024, 1024
    return mm_acc(at, b, name, grid=(M // tm, S // tk),
                  a_spec=pl.BlockSpec((tm, tk), lambda i, k: (i, k)),
                  b_spec=pl.BlockSpec((tk, N), lambda i, k: (k, 0)),
                  o_spec=pl.BlockSpec((tm, N), lambda i, k: (i, 0)),
                  o_shape=SDS((M, N), F32), acc_shape=(tm, N), write=_write_plain)


def mm_wgrad_cols(at, b, name, *, ncols, shard, tn):
    M, S = at.shape
    tk = 1024
    per = shard // tn if tn <= shard else 1
    nb = max(1, tn // shard)

    if tn <= shard:
        o_spec = pl.BlockSpec((None, M, tn), lambda t, k: (t // per, 0, t % per))
        write = _write_plain
    else:
        o_spec = pl.BlockSpec((nb, M, shard), lambda t, k: (t, 0, 0))

        def write(o_ref, acc_ref):
            for u in range(nb):
                o_ref[u] = acc_ref[:, u * shard:(u + 1) * shard]

    return mm_acc(at, b, name, grid=(ncols // tn, S // tk),
                  a_spec=pl.BlockSpec((M, tk), lambda t, k: (0, k)),
                  b_spec=pl.BlockSpec((tk, tn), lambda t, k: (k, t)),
                  o_spec=o_spec, o_shape=SDS((N_DEV, M, shard), F32), acc_shape=(M, tn), write=write)


def mm_dwg(pooled_t, dh):
    S = dh.shape[0]
    tk = 2048
    return mm_acc(pooled_t, dh, "mm_dwg", grid=(4, S // tk),
                  a_spec=pl.BlockSpec((PC, tk), lambda g, k: (g, k)),
                  b_spec=pl.BlockSpec((tk, PC), lambda g, k: (k, g)),
                  o_spec=pl.BlockSpec((None, PC, PC), lambda g, k: (g, 0, 0)),
                  o_shape=SDS((4, PC, PC), F32), acc_shape=(PC, PC), write=_write_plain)


def mm_dx_stack(da, w8, name, *, kchunk, tm):
    S = da.shape[0]
    bw = w8.shape[2]
    nck = bw // kchunk

    def body(a_ref, b_ref, o_ref, acc_ref):
        j = pl.program_id(1)

        @pl.when(j == 0)
        def _():
            acc_ref[...] = jnp.zeros_like(acc_ref)

        for u in range(nck):
            ks = slice(u * kchunk, (u + 1) * kchunk)
            acc_ref[...] += _dot_nt(a_ref[:, ks], b_ref[:, ks])

        @pl.when(j == N_DEV - 1)
        def _():
            o_ref[...] = acc_ref[...]

    return pl.pallas_call(
        body, name=name, grid=(S // tm, N_DEV),
        in_specs=[pl.BlockSpec((tm, bw), lambda i, j: (i, j)),
                  pl.BlockSpec((None, D, bw), lambda i, j: (j, 0, 0))],
        out_specs=pl.BlockSpec((tm, D), lambda i, j: (i, 0)),
        out_shape=SDS((S, D), F32),
        scratch_shapes=[pltpu.VMEM((tm, D), F32)],
        compiler_params=_cp(("parallel", "arbitrary"), 48),
    )(da, w8)


def attn_fwd(P, g, d):
    S = P.shape[0]
    L = S // d
    T = min(512, L)
    nq = T // QB
    Pv = P.reshape(L, d * PW)

    def body(qkv_ref, kh_ref, vh_ref, o_ref, lse_ref):
        i = pl.program_id(1)
        row = lax.broadcasted_iota(jnp.int32, (QB, QB), 0)
        col = lax.broadcasted_iota(jnp.int32, (QB, QB), 1)
        cur_ok = col <= row
        prev_ok = col >= row
        first_ok = jnp.logical_and(prev_ok, i > 0)
        lse_ref[...] = jnp.zeros_like(lse_ref)

        def head(h, carry):
            off = pl.multiple_of(h * HD, HD)
            cq, ck, cv = pl.ds(off, HD), pl.ds(E + off, HD), pl.ds(2 * E + off, HD)
            for j in range(nq):
                rows = slice(j * QB, (j + 1) * QB)
                q = qkv_ref[rows, cq]
                kc = qkv_ref[rows, ck]
                vc = qkv_ref[rows, cv]
                if j == 0:
                    kp, vp, pmask = kh_ref[:, cq], vh_ref[:, cq], first_ok
                else:
                    prow = slice((j - 1) * QB, j * QB)
                    kp, vp, pmask = qkv_ref[prow, ck], qkv_ref[prow, cv], prev_ok
                sc = jnp.where(cur_ok, _dot_nt(q, kc) * SCALE, NEG)
                sp = jnp.where(pmask, _dot_nt(q, kp) * SCALE, NEG)
                m = jnp.maximum(jnp.max(sc, axis=1, keepdims=True), jnp.max(sp, axis=1, keepdims=True))
                pc = jnp.exp(sc - m)
                pp = jnp.exp(sp - m)
                den = jnp.sum(pc, axis=1, keepdims=True) + jnp.sum(pp, axis=1, keepdims=True)
                o = (_dot(pc.astype(BF16), vc) + _dot(pp.astype(BF16), vp)) / den
                o_ref[rows, cq] = o.astype(BF16)
                lse_ref[rows, :] = jnp.where(col == h, m + jnp.log(den), lse_ref[rows, :])
            return carry

        lax.fori_loop(0, NH, head, 0)

    cb = lambda r: r * 4 + g
    return pl.pallas_call(
        body, name=f"attn_fwd{g}", grid=(d, L // T),
        in_specs=[pl.BlockSpec((T, SEG), lambda r, i: (i, cb(r))),
                  pl.BlockSpec((QB, E), lambda r, i: (jnp.maximum(i * nq - 1, 0), cb(r) * 3 + 1)),
                  pl.BlockSpec((QB, E), lambda r, i: (jnp.maximum(i * nq - 1, 0), cb(r) * 3 + 2))],
        out_specs=[pl.BlockSpec((T, E), lambda r, i: (i, r)), pl.BlockSpec((T, HD), lambda r, i: (i, r))],
        out_shape=[SDS((L, d * E), BF16), SDS((L, d * HD), F32)],
        compiler_params=_cp(("parallel", "parallel"), 48),
    )(Pv, Pv, Pv)


def combine_fwd(os_, lses, P, ehot):
    S = P.shape[0]
    ts = 256

    def body(o0, o1, o2, l0, l1, l2, z_ref, e_ref, y_ref, ya_ref, yat_ref, lse_ref):
        ls = [l0[...], l1[...], l2[...]]
        m = jnp.maximum(jnp.maximum(ls[0], ls[1]), ls[2])
        es = [jnp.exp(l - m) for l in ls]
        den = es[0] + es[1] + es[2]
        lse_ref[...] = m + jnp.log(den)
        y = None
        for e, o in zip(es, (o0, o1, o2)):
            w = e / den
            hi = w.astype(BF16)
            lo = (w - hi.astype(F32)).astype(BF16)
            wb = _dot(hi, e_ref[...]) + _dot(lo, e_ref[...])
            t = wb * o[...].astype(F32)
            y = t if y is None else y + t
        z = z_ref[...].astype(F32)
        ya = y * (z * _sigmoid(z))
        y_ref[...] = y.astype(BF16)
        ya_ref[...] = ya.astype(BF16)
        yat_ref[...] = ya.T.astype(BF16)

    wide = pl.BlockSpec((ts, E), lambda i: (i, 0))
    nar = pl.BlockSpec((ts, HD), lambda i: (i, 0))
    return pl.pallas_call(
        body, name="combine_fwd", grid=(S // ts,),
        in_specs=[wide, wide, wide, nar, nar, nar,
                  pl.BlockSpec((ts, E), lambda i: (i, PW // E - 3)),
                  pl.BlockSpec((HD, E), lambda i: (0, 0))],
        out_specs=[wide, wide, pl.BlockSpec((E, ts), lambda i: (0, i)), nar],
        out_shape=[SDS((S, E), BF16), SDS((S, E), BF16), SDS((E, S), BF16), SDS((S, HD), F32)],
        compiler_params=_cp(("parallel",), 48),
    )(*os_, *lses, P, ehot)


def mm_dya(da0, w_out, P, y):
    S = da0.shape[0]
    tm = 512

    def body(a_ref, w_ref, z_ref, y_ref, dy_ref, dz_ref):
        def chunk(cidx, carry):
            col = pl.ds(pl.multiple_of(cidx * 256, 256), 256)
            dya = _dot_nt(a_ref[...], w_ref[col, :])
            z = z_ref[:, col].astype(F32)
            sig = _sigmoid(z)
            dy_ref[:, col] = (dya * z * sig).astype(BF16)
            dz_ref[:, col] = (dya * y_ref[:, col].astype(F32) * sig * (1.0 + z * (1.0 - sig))).astype(BF16)
            return carry

        lax.fori_loop(0, E // 256, chunk, 0)

    zcol = PW // E - 3
    return pl.pallas_call(
        body, name="mm_dya", grid=(S // tm,),
        in_specs=[pl.BlockSpec((tm, D), lambda i: (i, 0)), pl.BlockSpec((E, D), lambda i: (0, 0)),
                  pl.BlockSpec((tm, E), lambda i: (i, zcol)), pl.BlockSpec((tm, E), lambda i: (i, 0))],
        out_specs=[pl.BlockSpec((tm, E), lambda i: (i, 0)), pl.BlockSpec((tm, E), lambda i: (i, zcol))],
        out_shape=[SDS((S, E), BF16), SDS((S, PW), BF16)],
        compiler_params=_cp(("parallel",), 48),
    )(da0, w_out, P, y)


def attn_bwd(P, dP, dy, y, lse, tabs, g, d):
    S = P.shape[0]
    L = S // d
    T = min(512, L)
    nq = T // QB
    ni = L // T
    Pv = P.reshape(L, d * PW)
    dPv = dP.reshape(L, d * PW)
    dyv, yv = dy.reshape(L, d * E), y.reshape(L, d * E)
    nar = [t.reshape(L, d * HD) for t in (lse,) + tuple(tabs)]

    def body(qkv_ref, kh_ref, vh_ref, dy_ref, y_ref, lse_ref, c_ref, s1_ref, s2_ref, alias_ref,
             o_ref, dkc_ref, dvc_ref):
        i = pl.program_id(1)
        ii = ni - 1 - i
        row = lax.broadcasted_iota(jnp.int32, (QB, QB), 0)
        col = lax.broadcasted_iota(jnp.int32, (QB, QB), 1)
        cur_ok = col <= row
        prev_ok = col >= row
        first_ok = jnp.logical_and(prev_ok, ii > 0)

        @pl.when(i == 0)
        def _():
            dkc_ref[...] = jnp.zeros_like(dkc_ref)
            dvc_ref[...] = jnp.zeros_like(dvc_ref)

        def head(h, carry):
            off = pl.multiple_of(h * HD, HD)
            cq, ck, cv = pl.ds(off, HD), pl.ds(E + off, HD), pl.ds(2 * E + off, HD)
            pend_dk = dkc_ref[:, cq]
            pend_dv = dvc_ref[:, cq]
            for j in reversed(range(nq)):
                rows = slice(j * QB, (j + 1) * QB)
                q = qkv_ref[rows, cq]
                kc = qkv_ref[rows, ck]
                vc = qkv_ref[rows, cv]
                if j == 0:
                    kp, vp, pmask = kh_ref[:, cq], vh_ref[:, cq], first_ok
                else:
                    prow = slice((j - 1) * QB, j * QB)
                    kp, vp, pmask = qkv_ref[prow, ck], qkv_ref[prow, cv], prev_ok
                dyj = dy_ref[rows, cq]
                lse_h = jnp.sum(jnp.where(col == h, lse_ref[rows, :], 0.0), axis=1, keepdims=True)
                delta = jnp.sum(dyj.astype(F32) * y_ref[rows, cq].astype(F32), axis=1, keepdims=True)
                pc = jnp.exp(jnp.where(cur_ok, _dot_nt(q, kc) * SCALE, NEG) - lse_h)
                pp = jnp.exp(jnp.where(pmask, _dot_nt(q, kp) * SCALE, NEG) - lse_h)
                dsc = pc * (_dot_nt(dyj, vc) - delta) * SCALE
                dsp = pp * (_dot_nt(dyj, vp) - delta) * SCALE
                dq = _dot(dsc.astype(BF16), kc) + _dot(dsp.astype(BF16), kp)
                dk = _dot(dsc.T.astype(BF16), q) + pend_dk
                dv = _dot(pc.T.astype(BF16), dyj) + pend_dv
                pend_dk = _dot(dsp.T.astype(BF16), q)
                pend_dv = _dot(pp.T.astype(BF16), dyj)
                c, s1, s2 = c_ref[rows, :], s1_ref[rows, :], s2_ref[rows, :]
                o_ref[rows, cq] = _unrope(dq, c, s1, s2).astype(BF16)
                o_ref[rows, ck] = _unrope(dk, c, s1, s2).astype(BF16)
                o_ref[rows, cv] = dv.astype(BF16)
            dkc_ref[:, cq] = pend_dk
            dvc_ref[:, cq] = pend_dv
            return carry

        lax.fori_loop(0, NH, head, 0)

    cb = lambda r: r * 4 + g
    rev = lambda i: ni - 1 - i
    halo = lambda i: jnp.maximum(rev(i) * nq - 1, 0)
    wide = pl.BlockSpec((T, E), lambda r, i: (rev(i), r))
    narrow = pl.BlockSpec((T, HD), lambda r, i: (rev(i), r))
    out = pl.pallas_call(
        body, name=f"attn_bwd{g}", grid=(d, ni),
        in_specs=[pl.BlockSpec((T, SEG), lambda r, i: (rev(i), cb(r))),
                  pl.BlockSpec((QB, E), lambda r, i: (halo(i), cb(r) * 3 + 1)),
                  pl.BlockSpec((QB, E), lambda r, i: (halo(i), cb(r) * 3 + 2)),
                  wide, wide, narrow, narrow, narrow, narrow, ANY],
        out_specs=pl.BlockSpec((T, SEG), lambda r, i: (rev(i), cb(r))),
        out_shape=SDS((L, d * PW), BF16),
        scratch_shapes=[pltpu.VMEM((QB, E), F32), pltpu.VMEM((QB, E), F32)],
        input_output_aliases={9: 0},
        compiler_params=_cp(("arbitrary", "arbitrary"), 56),
    )(Pv, Pv, Pv, dyv, yv, *nar, dPv)
    return out.reshape(S, PW)


def _pool_cnt(t0, rows):
    t = (lax.broadcasted_iota(jnp.int32, (rows, E), 0) + t0 + 1).astype(F32)
    ch = lax.broadcasted_iota(jnp.int32, (rows, E), 1)
    w = jnp.where(ch < PC, 2.0, jnp.where(ch < 2 * PC, 4.0, jnp.where(ch < 3 * PC, 8.0, 16.0)))
    return jnp.minimum(t, w)


def _by_group(parts):
    return jnp.concatenate([parts[g][:, g * PC:(g + 1) * PC] for g in range(4)], axis=1)


def pool_fwd(uz):
    S = uz.shape[0]
    ts = 256

    def body(u_ref, h_ref, o_ref, ot_ref):
        i = pl.program_id(0)
        u = u_ref[...]
        halo = jnp.where(i > 0, h_ref[...], 0.0)
        ext = jnp.concatenate([halo, u], axis=0)
        s2 = ext + pltpu.roll(ext, 1, 0)
        s4 = s2 + pltpu.roll(s2, 2, 0)
        s8 = s4 + pltpu.roll(s4, 4, 0)
        s16 = s8 + pltpu.roll(s8, 8, 0)
        win = _by_group([s2, s4, s8, s16])[16:, :]
        pooled = win / _pool_cnt(i * ts, ts) - u
        o_ref[...] = pooled.astype(BF16)
        ot_ref[...] = pooled.T.astype(BF16)

    return pl.pallas_call(
        body, name="pool_fwd", grid=(S // ts,),
        in_specs=[pl.BlockSpec((ts, E), lambda i: (i, 0)),
                  pl.BlockSpec((16, E), lambda i: (jnp.maximum(i * (ts // 16) - 1, 0), 0))],
        out_specs=[pl.BlockSpec((ts, E), lambda i: (i, 0)), pl.BlockSpec((E, ts), lambda i: (0, i))],
        out_shape=[SDS((S, E), BF16), SDS((E, S), BF16)],
        compiler_params=_cp(("parallel",), 48),
    )(uz, uz)


def pool_bwd(dpooled, duz):
    S = dpooled.shape[0]
    ts = 256
    nt = S // ts

    def body(d_ref, h_ref, alias_ref, o_ref):
        i = pl.program_id(0)
        dp = d_ref[...].astype(F32)
        halo = jnp.where(i < nt - 1, h_ref[...].astype(F32), 0.0)
        n = ts + 16
        ext = jnp.concatenate([dp, halo], axis=0) / _pool_cnt(i * ts, n)
        f2 = ext + pltpu.roll(ext, n - 1, 0)
        f4 = f2 + pltpu.roll(f2, n - 2, 0)
        f8 = f4 + pltpu.roll(f4, n - 4, 0)
        f16 = f8 + pltpu.roll(f8, n - 8, 0)
        win = _by_group([f2, f4, f8, f16])[:ts, :]
        o_ref[...] = (win - dp).astype(BF16)

    return pl.pallas_call(
        body, name="pool_bwd", grid=(nt,),
        in_specs=[pl.BlockSpec((ts, E), lambda i: (i, 0)),
                  pl.BlockSpec((16, E), lambda i: (jnp.minimum((i + 1) * (ts // 16), S // 16 - 1), 0)), ANY],
        out_specs=pl.BlockSpec((ts, E), lambda i: (i, 0)),
        out_shape=SDS(duz.shape, BF16),
        input_output_aliases={2: 0},
        compiler_params=_cp(("parallel",), 48),
    )(dpooled, dpooled, duz)


def mm_grp(pooled, wg, b, scale, uz):
    S = pooled.shape[0]
    tm = 512

    def body(p_ref, w_ref, b_ref, s_ref, z_ref, h_ref, y_ref, yt_ref):
        for g in range(4):
            cs = slice(g * PC, (g + 1) * PC)
            h = _dot(p_ref[:, cs], w_ref[g]) + b_ref[:, cs]
            z = z_ref[:, cs]
            yp = h * s_ref[:, cs] * (z * _sigmoid(z))
            h_ref[:, cs] = h.astype(BF16)
            y_ref[:, cs] = yp.astype(BF16)
            yt_ref[cs, :] = yp.T.astype(BF16)

    row = pl.BlockSpec((tm, E), lambda i: (i, 0))
    vec = pl.BlockSpec((1, E), lambda i: (0, 0))
    return pl.pallas_call(
        body, name="mm_grp", grid=(S // tm,),
        in_specs=[row, pl.BlockSpec((4, PC, PC), lambda i: (0, 0, 0)), vec, vec,
                  pl.BlockSpec((tm, E), lambda i: (i, 1))],
        out_specs=[row, row, pl.BlockSpec((E, tm), lambda i: (0, i))],
        out_shape=[SDS((S, E), BF16), SDS((S, E), BF16), SDS((E, S), BF16)],
        compiler_params=_cp(("parallel",), 48),
    )(pooled, wg, b, scale, uz)


def mm_dyp(da1, w_out, uz, h, scale):
    S = da1.shape[0]
    tm = 512

    def body(a_ref, w_ref, z_ref, h_ref, s_ref, dh_ref, dz_ref, dsc_ref, db_ref):
        @pl.when(pl.program_id(0) == 0)
        def _():
            dsc_ref[...] = jnp.zeros_like(dsc_ref)
            db_ref[...] = jnp.zeros_like(db_ref)

        def chunk(cidx, carry):
            col = pl.ds(pl.multiple_of(cidx * 256, 256), 256)
            dyp = _dot_nt(a_ref[...], w_ref[col, :])
            z = z_ref[:, col]
            hh = h_ref[:, col].astype(F32)
            sc = s_ref[:, col]
            sig = _sigmoid(z)
            dhs = dyp * z * sig
            dz_ref[:, col] = (dyp * hh * sc * sig * (1.0 + z * (1.0 - sig))).astype(BF16)
            dh = dhs * sc
            dh_ref[:, col] = dh.astype(BF16)
            dsc_ref[:, col] += _fold8(dhs * hh)
            db_ref[:, col] += _fold8(dh)
            return carry

        lax.fori_loop(0, E // 256, chunk, 0)

    row = pl.BlockSpec((tm, E), lambda i: (i, 0))
    acc = pl.BlockSpec((8, E), lambda i: (0, 0))
    return pl.pallas_call(
        body, name="mm_dyp", grid=(S // tm,),
        in_specs=[pl.BlockSpec((tm, D), lambda i: (i, 0)), pl.BlockSpec((E, D), lambda i: (0, 0)),
                  pl.BlockSpec((tm, E), lambda i: (i, 1)), row, pl.BlockSpec((1, E), lambda i: (0, 0))],
        out_specs=[row, pl.BlockSpec((tm, E), lambda i: (i, 1)), acc, acc],
        out_shape=[SDS((S, E), BF16), SDS((S, 2 * E), BF16), SDS((8, E), F32), SDS((8, E), F32)],
        compiler_params=_cp(("arbitrary",), 48),
    )(da1, w_out, uz, h, scale)


def mm_dpooled(dh, wg):
    S = dh.shape[0]
    tm = 1024

    def body(a_ref, w_ref, o_ref):
        for g in range(4):
            cs = slice(g * PC, (g + 1) * PC)
            o_ref[:, cs] = _dot_nt(a_ref[:, cs], w_ref[g]).astype(BF16)

    row = pl.BlockSpec((tm, E), lambda i: (i, 0))
    return pl.pallas_call(
        body, name="mm_dpooled", grid=(S // tm,),
        in_specs=[row, pl.BlockSpec((4, PC, PC), lambda i: (0, 0, 0))],
        out_specs=row, out_shape=SDS((S, E), BF16),
        compiler_params=_cp(("parallel",), 48),
    )(dh, wg)


def _rope_tables(positions):
    inv_freq = 500000.0 ** (-jnp.arange(0, 32, 2, dtype=F32) / 32)
    ang = positions.astype(F32)[:, None] * inv_freq
    cos, sin = jnp.cos(ang), jnp.sin(ang)
    S = positions.shape[0]
    one = jnp.ones((S, HD - 32), F32)
    zero16 = jnp.zeros((S, 16), F32)
    zero = jnp.zeros((S, HD - 32), F32)
    c = jnp.concatenate([cos, cos, one], axis=1)
    s1 = jnp.concatenate([-sin, zero16, zero], axis=1)
    s2 = jnp.concatenate([zero16, sin, zero], axis=1)
    return c, s1, s2


def kernel(x, positions, norm_pre, norm_post, attn_w_in, attn_w_out, pool_w_in, pool_w_grp, pool_b_grp, pool_scale, pool_w_out, loss_target, m_norm_pre, m_norm_post, m_attn_w_in, m_attn_w_out, m_pool_w_in, m_pool_w_grp, m_pool_b_grp, m_pool_scale, m_pool_w_out, v_norm_pre, v_norm_post, v_attn_w_in, v_attn_w_out, v_pool_w_in, v_pool_w_grp, v_pool_b_grp, v_pool_scale, v_pool_w_out):
    S = x.shape[1]
    xi, yi, ci = _mesh_pos()
    dev = 4 * xi + 2 * yi + ci
    x2 = x[0]
    tgt = loss_target[0]

    small = jnp.concatenate([pool_b_grp[0].reshape(2, HD), pool_scale[0].reshape(2, HD),
                             jnp.zeros((4, HD), F32)], axis=0)
    w_in8, w_out8, wp_in8, wg8, wp_out8, small8 = all_gather(
        [attn_w_in[0].astype(BF16), attn_w_out[0].astype(BF16), pool_w_in[0].astype(BF16),
         pool_w_grp[0].astype(BF16), pool_w_out[0].astype(BF16), small], "gather_weights")
    w_out = w_out8.reshape(E, D)
    wp_out = wp_out8.reshape(E, D)
    wg = wg8.transpose(1, 0, 2, 3).reshape(4, PC, PC)
    b_full = small8[:, 0:2, :].reshape(N_DEV, 4, PC // N_DEV).transpose(1, 0, 2).reshape(1, E)
    scale_full = small8[:, 2:4, :].reshape(1, E)

    tabs = _rope_tables(positions[0])
    ehot = (jnp.arange(E)[None, :] // HD == jnp.arange(HD)[:, None]).astype(BF16)

    xn0, xn0t = norm_pre0(x2, norm_pre[0:1])
    P = mm_qkvz(xn0, w_in8, *tabs)
    os_, lses = [], []
    for g, d in enumerate(DIL):
        o, l = attn_fwd(P, g, d)
        os_.append(o.reshape(S, E))
        lses.append(l.reshape(S, HD))
    yatt, ya, yat, lse = combine_fwd(os_, lses, P, ehot)
    a0 = mm_rows(ya, w_out, "mm_out0", F32)
    h1, xn1, xn1t = post0_pre1(x2, a0, norm_post[0:1], norm_pre[1:2])

    uz = mm_uz(xn1, wp_in8)
    pooled, pooled_t = pool_fwd(uz)
    hgrp, yp, ypt = mm_grp(pooled, wg, b_full, scale_full, uz)
    a1 = mm_rows(yp, wp_out, "mm_out1", F32)
    dh2, da1, loss_rows, dg_post1 = post1_loss(h1, a1, tgt, norm_post[1:2])
    loss = lax.psum(0.5 / D * jnp.sum(loss_rows), ("x", "y", "c"))

    dh, duz, dscale_p, db_p = mm_dyp(da1, wp_out, uz, hgrp, scale_full)
    dpooled = mm_dpooled(dh, wg)
    duz = pool_bwd(dpooled, duz)
    g_wg = mm_dwg(pooled_t, dh)
    g_wp_out = mm_wgrad_rows(ypt, da1, "mm_dwp_out")
    g_wp_in = mm_wgrad_cols(xn1t, duz, "mm_dwp_in", ncols=2 * E, shard=PC, tn=1024)
    dxn1 = mm_dx_stack(duz, wp_in8, "mm_dxn1", kchunk=PC, tm=1024)
    dh1, da0, dg_pre1, dg_post0 = mid_bwd(dxn1, dh2, h1, a0, norm_pre[1:2], norm_post[0:1])

    dy, dP = mm_dya(da0, w_out, P, yatt)
    for g, d in enumerate(DIL):
        dP = attn_bwd(P, dP, dy, yatt, lse, tabs, g, d)
    g_w_out = mm_wgrad_rows(yat, da0, "mm_dw_out")
    g_w_in = mm_wgrad_cols(xn0t, dP, "mm_dw_in", ncols=W_IN_COLS, shard=W_SHARD, tn=W_SHARD // 2)
    dxn0 = mm_dx_stack(dP, w_in8, "mm_dxn0", kchunk=W_SHARD // 2, tm=1024)
    grad_x, dg_pre0 = pre0_bwd(dxn0, dh1, x2, norm_pre[0:1])

    cidx = ci.astype(jnp.int32).reshape(1)
    fulls = [g_w_in, g_w_out.reshape(N_DEV, E // N_DEV, D), g_wp_in,
             g_wg.reshape(4, N_DEV, PC // N_DEV, PC).transpose(1, 0, 2, 3).reshape(N_DEV, 4 * PC // N_DEV, PC),
             g_wp_out.reshape(N_DEV, E // N_DEV, D)]
    sibs = rs_pair(fulls, "rs_pair")
    parts = [pair_add(f, s, cidx, f"pair_add{k}") for k, (f, s) in enumerate(zip(fulls, sibs))]
    recvs = rs_chips(parts, "rs_chips")
    shards = [(attn_w_in, m_attn_w_in, v_attn_w_in), (attn_w_out, m_attn_w_out, v_attn_w_out),
              (pool_w_in, m_pool_w_in, v_pool_w_in), (pool_w_grp, m_pool_w_grp, v_pool_w_grp),
              (pool_w_out, m_pool_w_out, v_pool_w_out)]
    big = []
    for k, (recv, (w, m, v)) in enumerate(zip(recvs, shards)):
        shp = w.shape
        r2 = recv.shape[1:]
        res = adamw_sum(recv, w.reshape(r2), m.reshape(r2), v.reshape(r2), f"adamw{k}")
        big.append([t.reshape(shp) for t in res])

    smalls = jnp.concatenate([dg_pre0.sum(0, keepdims=True), dg_pre1.sum(0, keepdims=True),
                              dg_post0.sum(0, keepdims=True), dg_post1.sum(0, keepdims=True),
                              db_p.sum(0).reshape(2, D), dscale_p.sum(0).reshape(2, D)], axis=0)
    (smalls8,) = all_gather([smalls], "gather_small_grads")
    tot = sum_slots(smalls8, "sum_small_grads")
    g_norm_pre, g_norm_post = tot[0:2], tot[2:4]
    g_b = lax.dynamic_slice_in_dim(tot[4:6].reshape(4, PC), dev * (PC // N_DEV), PC // N_DEV, axis=1)[None]
    g_scale = lax.dynamic_slice_in_dim(tot[6:8].reshape(1, E), dev * (E // N_DEV), E // N_DEV, axis=1)
    sm = [adamw_small(g_norm_pre, norm_pre, m_norm_pre, v_norm_pre, "adamw_norm_pre"),
          adamw_small(g_norm_post, norm_post, m_norm_post, v_norm_post, "adamw_norm_post"),
          adamw_small(g_b, pool_b_grp, m_pool_b_grp, v_pool_b_grp, "adamw_b"),
          adamw_small(g_scale, pool_scale, m_pool_scale, v_pool_scale, "adamw_scale")]

    grads = [g_norm_pre, g_norm_post, big[0][0], big[1][0], big[2][0], big[3][0], g_b, g_scale, big[4][0]]

    def pick(k):
        return [sm[0][k - 1], sm[1][k - 1], big[0][k], big[1][k], big[2][k], big[3][k], sm[2][k - 1], sm[3][k - 1],
                big[4][k]]

    return (loss, grad_x[None], *grads, *pick(1), *pick(2), *pick(3))
```

```python
import functools
import math

import numpy as np
import jax
import jax.numpy as jnp
from jax import lax
from jax.experimental import pallas as pl
from jax.experimental.pallas import tpu as pltpu

F32 = jnp.float32
BF16 = jnp.bfloat16
SDS = jax.ShapeDtypeStruct

N_DEV = 8
D = 1024
E = 2048
HD = 128
NH = E // HD
DIL = (1, 4, 16)
QB = 128
SEG = 3 * E
W_IN_COLS = 3 * SEG + E
W_SHARD = W_IN_COLS // N_DEV
CT = 512
POOL_WIN = (2, 4, 8, 16)
PC = E // 4
EPS = 1e-6
NEG = -1e30
SCALE = 1.0 / math.sqrt(HD)
LR, B1, B2, ADAM_EPS, WD, STEP = 0.001, 0.9, 0.999, 1e-08, 0.01, 10
MIB = 1024 * 1024
ANY = pl.BlockSpec(memory_space=pl.ANY)
MESH = pl.DeviceIdType.MESH


def _cp(sem, mb):
    return pltpu.CompilerParams(dimension_semantics=sem, vmem_limit_bytes=mb * MIB)


def _dot(a, b):
    return jnp.dot(a, b, preferred_element_type=F32)


def _dot_nt(a, b):
    return lax.dot_general(a, b, (((1,), (1,)), ((), ())), preferred_element_type=F32)


def _rms(h):
    return lax.rsqrt(jnp.mean(h * h, axis=-1, keepdims=True) + EPS)


def _row_tile(R, C, budget):
    tr = R
    while tr * C * 4 > budget and tr % 16 == 0:
        tr //= 2
    return tr


def _fold8(t):
    return t.reshape(t.shape[0] // 8, 8, t.shape[1]).sum(axis=0)


def _sigmoid(z):
    return 1.0 / (1.0 + jnp.exp(-z))


LANES = 128


def _scr(rows, C):
    return pltpu.VMEM((C // LANES, rows, LANES), F32)


def _scr_put(scr, val):
    for c in range(scr.shape[0]):
        scr[c] = val[:, c * LANES:(c + 1) * LANES]


def _scr_get(scr):
    return jnp.concatenate([scr[c] for c in range(scr.shape[0])], axis=1)


def _store_perm(dst_ref, scr, d):
    n = dst_ref.shape[1]
    for r in range(d):
        for c in range(scr.shape[0]):
            dst_ref[r, :, c * LANES:(c + 1) * LANES] = scr[c, pl.ds(r, n, stride=d), :].astype(dst_ref.dtype)


def _load_perm(scr, src_ref, d, add=False):
    n = src_ref.shape[1]
    for r in range(d):
        rows = pl.ds(r, n, stride=d)
        for c in range(scr.shape[0]):
            v = src_ref[r, :, c * LANES:(c + 1) * LANES].astype(F32)
            scr[c, rows, :] = scr[c, rows, :] + v if add else v


def _rope(t, c, s1, s2):
    return t * c + pltpu.roll(t, HD - 16, 1) * s1 + pltpu.roll(t, 16, 1) * s2


def _unrope(t, c, s1, s2):
    return t * c - pltpu.roll(t, HD - 16, 1) * s1 - pltpu.roll(t, 16, 1) * s2


def _mesh_pos():
    return lax.axis_index("x"), lax.axis_index("y"), lax.axis_index("c")


def all_gather(arrs, name):
    n = len(arrs)

    def body(*refs):
        ins, outs = refs[:n], refs[n:2 * n]
        send_sems, recv_sems, local_sems = refs[2 * n:]
        x, y, c = _mesh_pos()
        me, sib = (x, y, c), (x, y, 1 - c)
        chips = [(1 - x, y), (x, 1 - y), (1 - x, 1 - y)]

        def slot(p):
            return 4 * p[0] + 2 * p[1] + p[2]

        def copy(a, k, block, to, src=None):
            dst = outs[a].at[slot(block)]
            return pltpu.make_async_remote_copy(
                src_ref=dst if src is None else src, dst_ref=dst,
                send_sem=send_sems.at[a, k], recv_sem=recv_sems.at[a, k],
                device_id=to, device_id_type=MESH)

        mine = [pltpu.make_async_copy(ins[a], outs[a].at[slot(me)], local_sems.at[a]) for a in range(n)]
        for cp in mine:
            cp.start()
        first = []
        for a in range(n):
            first.append(copy(a, 0, me, sib, src=ins[a]))
            for j, chip in enumerate(chips):
                first.append(copy(a, 1 + j, me, (*chip, c), src=ins[a]))
        for cp in first:
            cp.start()
        passed = []
        for j, chip in enumerate(chips):
            for a in range(n):
                copy(a, 1 + j, (*chip, c), me).wait_recv()
                fw = copy(a, 4 + j, (*chip, c), sib)
                fw.start()
                passed.append(fw)
        for a in range(n):
            copy(a, 0, sib, me).wait_recv()
        for j, chip in enumerate(chips):
            for a in range(n):
                copy(a, 4 + j, (*chip, 1 - c), me).wait_recv()
        for cp in first + passed:
            cp.wait_send()
        for cp in mine:
            cp.wait()

    return pl.pallas_call(
        body, name=name,
        out_shape=[SDS((N_DEV,) + a.shape, a.dtype) for a in arrs],
        in_specs=[ANY] * n, out_specs=[ANY] * n,
        scratch_shapes=[pltpu.SemaphoreType.DMA((n, 7)), pltpu.SemaphoreType.DMA((n, 7)),
                        pltpu.SemaphoreType.DMA((n,))],
    )(*arrs)


def rs_pair(arrs, name):
    n = len(arrs)

    def body(*refs):
        ins, outs = refs[:n], refs[n:2 * n]
        send_sems, recv_sems = refs[2 * n:]
        x, y, c = _mesh_pos()
        cps = []
        for a in range(n):
            for q in range(4):
                cps.append(pltpu.make_async_remote_copy(
                    src_ref=ins[a].at[2 * q + (1 - c)], dst_ref=outs[a].at[q],
                    send_sem=send_sems.at[a, q], recv_sem=recv_sems.at[a, q],
                    device_id=(x, y, 1 - c), device_id_type=MESH))
        for cp in cps:
            cp.start()
        for cp in cps:
            cp.wait()

    return pl.pallas_call(
        body, name=name,
        out_shape=[SDS((4,) + a.shape[1:], a.dtype) for a in arrs],
        in_specs=[ANY] * n, out_specs=[ANY] * n,
        scratch_shapes=[pltpu.SemaphoreType.DMA((n, 4)), pltpu.SemaphoreType.DMA((n, 4))],
    )(*arrs)


def rs_chips(parts, name):
    n = len(parts)

    def body(*refs):
        ins, outs = refs[:n], refs[n:2 * n]
        send_sems, recv_sems, local_sems = refs[2 * n:]
        x, y, c = _mesh_pos()
        mychip = 2 * x + y
        chips = [(1 - x, y), (x, 1 - y), (1 - x, 1 - y)]
        mine = [pltpu.make_async_copy(ins[a].at[mychip], outs[a].at[mychip], local_sems.at[a]) for a in range(n)]
        for cp in mine:
            cp.start()
        cps = []
        for a in range(n):
            for j, chip in enumerate(chips):
                q = 2 * chip[0] + chip[1]
                cps.append(pltpu.make_async_remote_copy(
                    src_ref=ins[a].at[q], dst_ref=outs[a].at[mychip],
                    send_sem=send_sems.at[a, j], recv_sem=recv_sems.at[a, j],
                    device_id=(*chip, c), device_id_type=MESH))
        for cp in cps:
            cp.start()
        for cp in cps:
            cp.wait()
        for cp in mine:
            cp.wait()

    return pl.pallas_call(
        body, name=name,
        out_shape=[SDS(a.shape, a.dtype) for a in parts],
        in_specs=[ANY] * n, out_specs=[ANY] * n,
        scratch_shapes=[pltpu.SemaphoreType.DMA((n, 3)), pltpu.SemaphoreType.DMA((n, 3)),
                        pltpu.SemaphoreType.DMA((n,))],
    )(*parts)


def pair_add(full, sib, cidx, name):
    _, R, C = full.shape
    tr = _row_tile(R, C, 512 * 1024)

    def body(c_ref, a_ref, b_ref, o_ref):
        o_ref[...] = a_ref[...] + b_ref[...]

    return pl.pallas_call(
        body, name=name,
        grid_spec=pltpu.PrefetchScalarGridSpec(
            num_scalar_prefetch=1, grid=(4, R // tr),
            in_specs=[pl.BlockSpec((None, tr, C), lambda q, i, cr: (2 * q + cr[0], i, 0)),
                      pl.BlockSpec((None, tr, C), lambda q, i, cr: (q, i, 0))],
            out_specs=pl.BlockSpec((None, tr, C), lambda q, i, cr: (q, i, 0))),
        out_shape=SDS((4, R, C), F32),
        compiler_params=_cp(("parallel", "parallel"), 32),
    )(cidx, full, sib)


def _adam_math(w, g, m, v):
    m2 = B1 * m + (1.0 - B1) * g
    v2 = B2 * v + (1.0 - B2) * (g * g)
    m_hat = m2 / (1.0 - B1 ** STEP)
    v_hat = v2 / (1.0 - B2 ** STEP)
    delta = -LR * (m_hat / (jnp.sqrt(v_hat) + ADAM_EPS) + WD * w)
    return delta, m2, v2


def adamw_sum(recv, w, m, v, name):
    K, R, C = recv.shape
    tr = _row_tile(R, C, 256 * 1024)

    def body(r_ref, w_ref, m_ref, v_ref, g_ref, d_ref, m2_ref, v2_ref):
        g = r_ref[0]
        for k in range(1, K):
            g = g + r_ref[k]
        delta, m2, v2 = _adam_math(w_ref[...], g, m_ref[...], v_ref[...])
        g_ref[...] = g
        d_ref[...] = delta
        m2_ref[...] = m2
        v2_ref[...] = v2

    tile = pl.BlockSpec((tr, C), lambda i: (i, 0))
    return pl.pallas_call(
        body, name=name, grid=(R // tr,),
        in_specs=[pl.BlockSpec((K, tr, C), lambda i: (0, i, 0)), tile, tile, tile],
        out_specs=[tile] * 4, out_shape=[SDS((R, C), F32)] * 4,
        compiler_params=_cp(("parallel",), 32),
    )(recv, w, m, v)


def adamw_small(g, w, m, v, name):
    def body(g_ref, w_ref, m_ref, v_ref, d_ref, m2_ref, v2_ref):
        delta, m2, v2 = _adam_math(w_ref[...], g_ref[...], m_ref[...], v_ref[...])
        d_ref[...] = delta
        m2_ref[...] = m2
        v2_ref[...] = v2

    return pl.pallas_call(body, name=name, out_shape=[SDS(w.shape, F32)] * 3)(g, w, m, v)


def sum_slots(a, name):
    K = a.shape[0]

    def body(a_ref, o_ref):
        t = a_ref[0]
        for k in range(1, K):
            t = t + a_ref[k]
        o_ref[...] = t

    return pl.pallas_call(body, name=name, out_shape=SDS(a.shape[1:], F32))(a)


def norm_pre0(x, g):
    S = x.shape[0]
    ts = 512

    def body(x_ref, g_ref, o_ref, o4_ref, o16_ref, ot_ref, scr):
        h = x_ref[...]
        xn = h * _rms(h) * g_ref[...]
        o_ref[...] = xn.astype(BF16)
        ot_ref[...] = xn.T.astype(BF16)
        _scr_put(scr, xn)
        _store_perm(o4_ref, scr, 4)
        _store_perm(o16_ref, scr, 16)

    return pl.pallas_call(
        body, name="norm_pre0", grid=(S // ts,),
        in_specs=[pl.BlockSpec((ts, D), lambda i: (i, 0)), pl.BlockSpec((1, D), lambda i: (0, 0))],
        out_specs=[pl.BlockSpec((ts, D), lambda i: (i, 0)),
                   pl.BlockSpec((4, ts // 4, D), lambda i: (0, i, 0)),
                   pl.BlockSpec((16, ts // 16, D), lambda i: (0, i, 0)),
                   pl.BlockSpec((D, ts), lambda i: (0, i))],
        out_shape=[SDS((S, D), BF16), SDS((4, S // 4, D), BF16), SDS((16, S // 16, D), BF16), SDS((D, S), BF16)],
        scratch_shapes=[_scr(ts, D)],
        compiler_params=_cp(("parallel",), 32),
    )(x, g)


def transpose_rows(a, name):
    S, C = a.shape
    ts = 512

    def body(a_ref, o_ref):
        o_ref[...] = a_ref[...].astype(F32).T.astype(BF16)

    return pl.pallas_call(
        body, name=name, grid=(S // ts,),
        in_specs=[pl.BlockSpec((ts, C), lambda i: (i, 0))],
        out_specs=pl.BlockSpec((C, ts), lambda i: (0, i)),
        out_shape=SDS((C, S), BF16),
        compiler_params=_cp(("parallel",), 32),
    )(a)


def post0_pre1(x, a0, g_post, g_pre):
    S = x.shape[0]
    ts = 512

    def body(x_ref, a_ref, gp_ref, gn_ref, h_ref, o_ref, ot_ref):
        a = a_ref[...]
        h1 = x_ref[...] + a * _rms(a) * gp_ref[...]
        h_ref[...] = h1
        xn = h1 * _rms(h1) * gn_ref[...]
        o_ref[...] = xn.astype(BF16)
        ot_ref[...] = xn.T.astype(BF16)

    row = pl.BlockSpec((ts, D), lambda i: (i, 0))
    vec = pl.BlockSpec((1, D), lambda i: (0, 0))
    return pl.pallas_call(
        body, name="post0_pre1", grid=(S // ts,),
        in_specs=[row, row, vec, vec],
        out_specs=[row, row, pl.BlockSpec((D, ts), lambda i: (0, i))],
        out_shape=[SDS((S, D), F32), SDS((S, D), BF16), SDS((D, S), BF16)],
        compiler_params=_cp(("parallel",), 40),
    )(x, a0, g_post, g_pre)


def post1_loss(h1, a1, target, g_post):
    S = h1.shape[0]
    ts = 512

    def body(h_ref, a_ref, t_ref, g_ref, dh_ref, da_ref, loss_ref, dg_ref):
        @pl.when(pl.program_id(0) == 0)
        def _():
            loss_ref[...] = jnp.zeros_like(loss_ref)
            dg_ref[...] = jnp.zeros_like(dg_ref)

        a = a_ref[...]
        g = g_ref[...]
        rp = _rms(a)
        yhat = a * rp
        e = h_ref[...] + yhat * g - t_ref[...]
        loss_ref[...] += _fold8(e * e)
        dh = e * (1.0 / D)
        dh_ref[...] = dh
        dg_ref[...] += _fold8(dh * yhat)
        dyh = dh * g
        da = rp * (dyh - yhat * jnp.mean(dyh * yhat, axis=-1, keepdims=True))
        da_ref[...] = da.astype(BF16)

    row = pl.BlockSpec((ts, D), lambda i: (i, 0))
    acc = pl.BlockSpec((8, D), lambda i: (0, 0))
    return pl.pallas_call(
        body, name="post1_loss", grid=(S // ts,),
        in_specs=[row, row, row, pl.BlockSpec((1, D), lambda i: (0, 0))],
        out_specs=[row, row, acc, acc],
        out_shape=[SDS((S, D), F32), SDS((S, D), BF16), SDS((8, D), F32), SDS((8, D), F32)],
        compiler_params=_cp(("arbitrary",), 40),
    )(h1, a1, target, g_post)


def mid_bwd(dxn1, dh2, h1, a0, g_pre1, g_post0):
    S = h1.shape[0]
    ts = 512

    def body(dx_ref, dh2_ref, h_ref, a_ref, gn_ref, gp_ref, dh1_ref, da_ref, dgn_ref, dgp_ref):
        @pl.when(pl.program_id(0) == 0)
        def _():
            dgn_ref[...] = jnp.zeros_like(dgn_ref)
            dgp_ref[...] = jnp.zeros_like(dgp_ref)

        h = h_ref[...]
        r1 = _rms(h)
        xhat = h * r1
        dxn = dx_ref[...]
        dgn_ref[...] += _fold8(dxn * xhat)
        dxh = dxn * gn_ref[...]
        dh1 = dh2_ref[...] + r1 * (dxh - xhat * jnp.mean(dxh * xhat, axis=-1, keepdims=True))
        dh1_ref[...] = dh1
        a = a_ref[...]
        rp = _rms(a)
        yhat = a * rp
        dgp_ref[...] += _fold8(dh1 * yhat)
        dyh = dh1 * gp_ref[...]
        da = rp * (dyh - yhat * jnp.mean(dyh * yhat, axis=-1, keepdims=True))
        da_ref[...] = da.astype(BF16)

    row = pl.BlockSpec((ts, D), lambda i: (i, 0))
    vec = pl.BlockSpec((1, D), lambda i: (0, 0))
    acc = pl.BlockSpec((8, D), lambda i: (0, 0))
    return pl.pallas_call(
        body, name="mid_bwd", grid=(S // ts,),
        in_specs=[row, row, row, row, vec, vec],
        out_specs=[row, row, acc, acc],
        out_shape=[SDS((S, D), F32), SDS((S, D), BF16), SDS((8, D), F32), SDS((8, D), F32)],
        compiler_params=_cp(("arbitrary",), 48),
    )(dxn1, dh2, h1, a0, g_pre1, g_post0)


def pre0_bwd(dx_tok, dx_z, dx4, dx16, dh1, x, g_pre0):
    S = x.shape[0]
    ts = 512

    def body(da_ref, dz_ref, d4_ref, d16_ref, dh_ref, x_ref, g_ref, gx_ref, dg_ref, scr):
        @pl.when(pl.program_id(0) == 0)
        def _():
            dg_ref[...] = jnp.zeros_like(dg_ref)

        _scr_put(scr, da_ref[...] + dz_ref[...])
        _load_perm(scr, d4_ref, 4, add=True)
        _load_perm(scr, d16_ref, 16, add=True)
        h = x_ref[...]
        r = _rms(h)
        xhat = h * r
        dxn = _scr_get(scr)
        dg_ref[...] += _fold8(dxn * xhat)
        dxh = dxn * g_ref[...]
        gx_ref[...] = dh_ref[...] + r * (dxh - xhat * jnp.mean(dxh * xhat, axis=-1, keepdims=True))

    row = pl.BlockSpec((ts, D), lambda i: (i, 0))
    return pl.pallas_call(
        body, name="pre0_bwd", grid=(S // ts,),
        in_specs=[row, row, pl.BlockSpec((4, ts // 4, D), lambda i: (0, i, 0)),
                  pl.BlockSpec((16, ts // 16, D), lambda i: (0, i, 0)), row, row,
                  pl.BlockSpec((1, D), lambda i: (0, 0))],
        out_specs=[row, pl.BlockSpec((8, D), lambda i: (0, 0))],
        out_shape=[SDS((S, D), F32), SDS((8, D), F32)],
        scratch_shapes=[_scr(ts, D)],
        compiler_params=_cp(("arbitrary",), 48),
    )(dx_tok, dx_z, dx4.reshape(4, S // 4, D), dx16.reshape(16, S // 16, D), dh1, x, g_pre0)


def _w_tile(tile0):
    per = W_SHARD // CT
    return lambda t: ((tile0 + t) // per, 0, (tile0 + t) % per)


def mm_in(xn, w8, tile0, ntiles, tabs, name):
    S = xn.shape[0]
    tm = 2048
    wt = _w_tile(tile0)

    def body(a_ref, b_ref, *rest):
        o_ref = rest[-1]
        r = _dot(a_ref[...], b_ref[...])
        if tabs is None:
            o_ref[...] = r.astype(BF16)
            return
        c_ref, s1_ref, s2_ref = rest[:3]
        rot = pl.program_id(1) < 2 * E // CT
        c = jnp.where(rot, c_ref[...], 1.0)
        s1 = jnp.where(rot, s1_ref[...], 0.0)
        s2 = jnp.where(rot, s2_ref[...], 0.0)
        for hh in range(CT // HD):
            cs = slice(hh * HD, (hh + 1) * HD)
            o_ref[:, cs] = _rope(r[:, cs], c, s1, s2).astype(BF16)

    tab = pl.BlockSpec((tm, HD), lambda i, t: (i, 0))
    return pl.pallas_call(
        body, name=name, grid=(S // tm, ntiles),
        in_specs=[pl.BlockSpec((tm, D), lambda i, t: (i, 0)),
                  pl.BlockSpec((None, D, CT), lambda i, t: wt(t))] + ([] if tabs is None else [tab] * 3),
        out_specs=pl.BlockSpec((tm, CT), lambda i, t: (i, t)),
        out_shape=SDS((S, ntiles * CT), BF16),
        compiler_params=_cp(("parallel", "parallel"), 48),
    )(xn, w8, *(() if tabs is None else tabs))


def mm_rows(a, b, name, out_dtype, tm=1024):
    M, K = a.shape
    N = b.shape[1]

    def body(a_ref, b_ref, o_ref):
        def chunk(cidx, carry):
            col = pl.ds(pl.multiple_of(cidx * 256, 256), 256)
            o_ref[:, col] = _dot(a_ref[...], b_ref[:, col]).astype(out_dtype)
            return carry

        lax.fori_loop(0, N // 256, chunk, 0)

    return pl.pallas_call(
        body, name=name, grid=(M // tm,),
        in_specs=[pl.BlockSpec((tm, K), lambda i: (i, 0)), pl.BlockSpec((K, N), lambda i: (0, 0))],
        out_specs=pl.BlockSpec((tm, N), lambda i: (i, 0)),
        out_shape=SDS((M, N), out_dtype),
        compiler_params=_cp(("parallel",), 48),
    )(a, b)


def mm_uz(xn, w8):
    S = xn.shape[0]
    tm = 512
    bw = w8.shape[2]

    def body(a_ref, b_ref, o_ref):
        def blk(dv, carry):
            for half in range(bw // 256):
                col = pl.ds(pl.multiple_of(dv * bw + half * 256, 256), 256)
                o_ref[:, col] = _dot(a_ref[...], b_ref[dv, :, half * 256:(half + 1) * 256])
            return carry

        lax.fori_loop(0, N_DEV, blk, 0)

    return pl.pallas_call(
        body, name="mm_uz", grid=(S // tm,),
        in_specs=[pl.BlockSpec((tm, D), lambda i: (i, 0)), pl.BlockSpec((N_DEV, D, bw), lambda i: (0, 0, 0))],
        out_specs=pl.BlockSpec((tm, N_DEV * bw), lambda i: (i, 0)),
        out_shape=SDS((S, N_DEV * bw), F32),
        compiler_params=_cp(("parallel",), 48),
    )(xn, w8)


def mm_acc(a, b, name, *, grid, a_spec, b_spec, o_spec, o_shape, acc_shape, write, vmem=48):
    nk = grid[-1]

    def body(a_ref, b_ref, o_ref, acc_ref):
        k = pl.program_id(len(grid) - 1)

        @pl.when(k == 0)
        def _():
            acc_ref[...] = jnp.zeros_like(acc_ref)

        acc_ref[...] += _dot(a_ref[...], b_ref[...])

        @pl.when(k == nk - 1)
        def _():
            write(o_ref, acc_ref)

    return pl.pallas_call(
        body, name=name, grid=grid, in_specs=[a_spec, b_spec], out_specs=o_spec, out_shape=o_shape,
        scratch_shapes=[pltpu.VMEM(acc_shape, F32)],
        compiler_params=_cp(("parallel",) * (len(grid) - 1) + ("arbitrary",), vmem),
    )(a, b)


def _write_plain(o_ref, acc_ref):
    o_ref[...] = acc_ref[...]


def mm_wgrad_rows(at, b, name):
    M, S = at.shape
    N = b.shape[1]
    tm, tk = 1024, 1024
    return mm_acc(at, b, name, grid=(M // tm, S // tk),
                  a_spec=pl.BlockSpec((tm, tk), lambda i, k: (i, k)),
                  b_spec=pl.BlockSpec((tk, N), lambda i, k: (k, 0)),
                  o_spec=pl.BlockSpec((tm, N), lambda i, k: (i, 0)),
                  o_shape=SDS((M, N), F32), acc_shape=(tm, N), write=_write_plain)


def mm_wgrad_cols(at, b, name, *, ncols, shard, tn):
    M, S = at.shape
    tk = 1024
    per = shard // tn if tn <= shard else 1
    nb = max(1, tn // shard)

    if tn <= shard:
        o_spec = pl.BlockSpec((None, M, tn), lambda t, k: (t // per, 0, t % per))
        write = _write_plain
    else:
        o_spec = pl.BlockSpec((nb, M, shard), lambda t, k: (t, 0, 0))

        def write(o_ref, acc_ref):
            for u in range(nb):
                o_ref[u] = acc_ref[:, u * shard:(u + 1) * shard]

    return mm_acc(at, b, name, grid=(ncols // tn, S // tk),
                  a_spec=pl.BlockSpec((M, tk), lambda t, k: (0, k)),
                  b_spec=pl.BlockSpec((tk, tn), lambda t, k: (k, t)),
                  o_spec=o_spec, o_shape=SDS((N_DEV, M, shard), F32), acc_shape=(M, tn), write=write)


def mm_dwg(pooled_t, dh):
    S = dh.shape[0]
    tk = 2048
    return mm_acc(pooled_t, dh, "mm_dwg", grid=(4, S // tk),
                  a_spec=pl.BlockSpec((PC, tk), lambda g, k: (g, k)),
                  b_spec=pl.BlockSpec((tk, PC), lambda g, k: (k, g)),
                  o_spec=pl.BlockSpec((None, PC, PC), lambda g, k: (g, 0, 0)),
                  o_shape=SDS((4, PC, PC), F32), acc_shape=(PC, PC), write=_write_plain)


def mm_dx_stack(da, w8, name, *, kchunk, tm):
    S = da.shape[0]
    bw = w8.shape[2]
    nck = bw // kchunk

    def body(a_ref, b_ref, o_ref, acc_ref):
        j = pl.program_id(1)

        @pl.when(j == 0)
        def _():
            acc_ref[...] = jnp.zeros_like(acc_ref)

        for u in range(nck):
            ks = slice(u * kchunk, (u + 1) * kchunk)
            acc_ref[...] += _dot_nt(a_ref[:, ks], b_ref[:, ks])

        @pl.when(j == N_DEV - 1)
        def _():
            o_ref[...] = acc_ref[...]

    return pl.pallas_call(
        body, name=name, grid=(S // tm, N_DEV),
        in_specs=[pl.BlockSpec((tm, bw), lambda i, j: (i, j)),
                  pl.BlockSpec((None, D, bw), lambda i, j: (j, 0, 0))],
        out_specs=pl.BlockSpec((tm, D), lambda i, j: (i, 0)),
        out_shape=SDS((S, D), F32),
        scratch_shapes=[pltpu.VMEM((tm, D), F32)],
        compiler_params=_cp(("parallel", "arbitrary"), 48),
    )(da, w8)


def mm_dw_in_part(at, b, tile0, prev, name):
    M, S = at.shape
    ntiles = b.shape[1] // CT
    tk = 2048
    nk = S // tk
    wt = _w_tile(tile0)

    def body(a_ref, b_ref, *rest):
        o_ref, acc_ref = rest[-2:]
        k = pl.program_id(1)

        @pl.when(k == 0)
        def _():
            acc_ref[...] = jnp.zeros_like(acc_ref)

        acc_ref[...] += _dot(a_ref[...], b_ref[...])

        @pl.when(k == nk - 1)
        def _():
            o_ref[...] = acc_ref[...]

    return pl.pallas_call(
        body, name=name, grid=(ntiles, nk),
        in_specs=[pl.BlockSpec((M, tk), lambda t, k: (0, k)), pl.BlockSpec((tk, CT), lambda t, k: (k, t))]
        + ([] if prev is None else [ANY]),
        out_specs=pl.BlockSpec((None, M, CT), lambda t, k: wt(t)),
        out_shape=SDS((N_DEV, M, W_SHARD), F32),
        scratch_shapes=[pltpu.VMEM((M, CT), F32)],
        input_output_aliases={} if prev is None else {2: 0},
        compiler_params=_cp(("parallel", "arbitrary"), 48),
    )(at, b, *(() if prev is None else (prev,)))


def mm_dx_part(da, w8, tile0, name):
    S = da.shape[0]
    ntiles = da.shape[1] // CT
    tm = 2048
    wt = _w_tile(tile0)

    def body(a_ref, b_ref, o_ref, acc_ref):
        t = pl.program_id(1)

        @pl.when(t == 0)
        def _():
            acc_ref[...] = jnp.zeros_like(acc_ref)

        acc_ref[...] += _dot_nt(a_ref[...], b_ref[...])

        @pl.when(t == ntiles - 1)
        def _():
            o_ref[...] = acc_ref[...]

    return pl.pallas_call(
        body, name=name, grid=(S // tm, ntiles),
        in_specs=[pl.BlockSpec((tm, CT), lambda i, t: (i, t)), pl.BlockSpec((None, D, CT), lambda i, t: wt(t))],
        out_specs=pl.BlockSpec((tm, D), lambda i, t: (i, 0)),
        out_shape=SDS((S, D), F32),
        scratch_shapes=[pltpu.VMEM((tm, D), F32)],
        compiler_params=_cp(("parallel", "arbitrary"), 56),
    )(da, w8)


def _band_masks(not_first):
    row = lax.broadcasted_iota(jnp.int32, (QB, QB), 0)
    col = lax.broadcasted_iota(jnp.int32, (QB, QB), 1)
    cur = jnp.where(col <= row, 0.0, NEG)
    prev = jnp.where(col >= row, 0.0, NEG)
    first = jnp.where(jnp.logical_and(col >= row, not_first), 0.0, NEG)
    return col, cur, prev, first


def attn_fwd(P, g, d):
    S = P.shape[0]
    L = S // d
    T = min(512, L)
    nq = T // QB
    ni = L // T

    def body(qkv_ref, kh_ref, vh_ref, o_ref, lse_ref):
        col, mcur, mprev, mfirst = _band_masks(pl.program_id(1) > 0)
        lse_ref[...] = jnp.zeros_like(lse_ref)

        def head(h, carry):
            off = pl.multiple_of(h * HD, HD)
            cq, ck, cv = pl.ds(off, HD), pl.ds(E + off, HD), pl.ds(2 * E + off, HD)
            for j in range(nq):
                rows = slice(j * QB, (j + 1) * QB)
                q = qkv_ref[rows, cq]
                kc = qkv_ref[rows, ck]
                vc = qkv_ref[rows, cv]
                if j == 0:
                    kp, vp, pmask = kh_ref[:, cq], vh_ref[:, cq], mfirst
                else:
                    prow = slice((j - 1) * QB, j * QB)
                    kp, vp, pmask = qkv_ref[prow, ck], qkv_ref[prow, cv], mprev
                sc = _dot_nt(q, kc) * SCALE + mcur
                sp = _dot_nt(q, kp) * SCALE + pmask
                m = jnp.maximum(jnp.max(sc, axis=1, keepdims=True), jnp.max(sp, axis=1, keepdims=True))
                pc = jnp.exp(sc - m)
                pp = jnp.exp(sp - m)
                den = jnp.sum(pc, axis=1, keepdims=True) + jnp.sum(pp, axis=1, keepdims=True)
                o = (_dot(pc.astype(BF16), vc) + _dot(pp.astype(BF16), vp)) / den
                o_ref[rows, cq] = o.astype(BF16)
                lse_ref[rows, :] = jnp.where(col == h, m + jnp.log(den), lse_ref[rows, :])
            return carry

        lax.fori_loop(0, NH, head, 0, unroll=2)

    halo = lambda r, i: jnp.maximum(r * (L // QB) + i * nq - 1, 0)
    return pl.pallas_call(
        body, name=f"attn_fwd{g}", grid=(d, ni),
        in_specs=[pl.BlockSpec((T, SEG), lambda r, i: (r * ni + i, 0)),
                  pl.BlockSpec((QB, E), lambda r, i: (halo(r, i), 1)),
                  pl.BlockSpec((QB, E), lambda r, i: (halo(r, i), 2))],
        out_specs=[pl.BlockSpec((T, E), lambda r, i: (r * ni + i, 0)),
                   pl.BlockSpec((T, HD), lambda r, i: (r * ni + i, 0))],
        out_shape=[SDS((S, E), BF16), SDS((S, HD), F32)],
        compiler_params=_cp(("parallel", "parallel"), 48),
    )(P, P, P)


def _perm_specs(ts, C):
    return [pl.BlockSpec((ts, C), lambda i: (i, 0)),
            pl.BlockSpec((4, ts // 4, C), lambda i: (0, i, 0)),
            pl.BlockSpec((16, ts // 16, C), lambda i: (0, i, 0))]


def _perm_shapes(S, C, dtype):
    return [SDS((S, C), dtype), SDS((4, S // 4, C), dtype), SDS((16, S // 16, C), dtype)]


def combine_fwd(os_, lses, z, ehot):
    S = z.shape[0]
    ts = 256

    def body(o0, o1, o2, l0, l1, l2, z_ref, e_ref, y0, y1, y2, s0, s1, s2, ya_ref, yat_ref,
             so1, so2, sl1, sl2, sy, sl):
        _load_perm(so1, o1, 4)
        _load_perm(so2, o2, 16)
        _load_perm(sl1, l1, 4)
        _load_perm(sl2, l2, 16)
        ls = [l0[...], sl1[0], sl2[0]]
        m = jnp.maximum(jnp.maximum(ls[0], ls[1]), ls[2])
        es = [jnp.exp(l - m) for l in ls]
        den = es[0] + es[1] + es[2]
        sl[0] = m + jnp.log(den)
        y = None
        for e, o in zip(es, (o0[...].astype(F32), _scr_get(so1), _scr_get(so2))):
            w = e / den
            hi = w.astype(BF16)
            lo = (w - hi.astype(F32)).astype(BF16)
            wb = _dot(hi, e_ref[...]) + _dot(lo, e_ref[...])
            y = wb * o if y is None else y + wb * o
        z = z_ref[...].astype(F32)
        ya = y * (z * _sigmoid(z))
        ya_ref[...] = ya.astype(BF16)
        yat_ref[...] = ya.T.astype(BF16)
        _scr_put(sy, y)
        y0[...] = y.astype(BF16)
        _store_perm(y1, sy, 4)
        _store_perm(y2, sy, 16)
        s0[...] = sl[0]
        _store_perm(s1, sl, 4)
        _store_perm(s2, sl, 16)

    wide = pl.BlockSpec((ts, E), lambda i: (i, 0))
    os3 = [os_[0], os_[1].reshape(4, S // 4, E), os_[2].reshape(16, S // 16, E)]
    ls3 = [lses[0], lses[1].reshape(4, S // 4, HD), lses[2].reshape(16, S // 16, HD)]
    res = pl.pallas_call(
        body, name="combine_fwd", grid=(S // ts,),
        in_specs=_perm_specs(ts, E) + _perm_specs(ts, HD) + [wide, pl.BlockSpec((HD, E), lambda i: (0, 0))],
        out_specs=_perm_specs(ts, E) + _perm_specs(ts, HD) + [wide, pl.BlockSpec((E, ts), lambda i: (0, i))],
        out_shape=_perm_shapes(S, E, BF16) + _perm_shapes(S, HD, F32) + [SDS((S, E), BF16), SDS((E, S), BF16)],
        scratch_shapes=[_scr(ts, E), _scr(ts, E), _scr(ts, HD), _scr(ts, HD), _scr(ts, E), _scr(ts, HD)],
        compiler_params=_cp(("parallel",), 56),
    )(*os3, *ls3, z, ehot)
    ys = [res[0], res[1].reshape(S, E), res[2].reshape(S, E)]
    lse3 = [res[3], res[4].reshape(S, HD), res[5].reshape(S, HD)]
    return ys, lse3, res[6], res[7]


def mm_dya(da0, w_out, z, y):
    S = da0.shape[0]
    tm = 512

    def body(a_ref, w_ref, z_ref, y_ref, dy0, dy1, dy2, dz_ref, scr):
        def chunk(cidx, carry):
            col = pl.ds(pl.multiple_of(cidx * 256, 256), 256)
            dya = _dot_nt(a_ref[...], w_ref[col, :])
            zz = z_ref[:, col].astype(F32)
            sig = _sigmoid(zz)
            dy = dya * zz * sig
            scr[2 * cidx] = dy[:, :LANES]
            scr[2 * cidx + 1] = dy[:, LANES:]
            dy0[:, col] = dy.astype(BF16)
            dz_ref[:, col] = (dya * y_ref[:, col].astype(F32) * sig * (1.0 + zz * (1.0 - sig))).astype(BF16)
            return carry

        lax.fori_loop(0, E // 256, chunk, 0)
        _store_perm(dy1, scr, 4)
        _store_perm(dy2, scr, 16)

    wide = pl.BlockSpec((tm, E), lambda i: (i, 0))
    res = pl.pallas_call(
        body, name="mm_dya", grid=(S // tm,),
        in_specs=[pl.BlockSpec((tm, D), lambda i: (i, 0)), pl.BlockSpec((E, D), lambda i: (0, 0)), wide, wide],
        out_specs=_perm_specs(tm, E) + [wide],
        out_shape=_perm_shapes(S, E, BF16) + [SDS((S, E), BF16)],
        scratch_shapes=[_scr(tm, E)],
        compiler_params=_cp(("parallel",), 48),
    )(da0, w_out, z, y)
    return [res[0], res[1].reshape(S, E), res[2].reshape(S, E)], res[3]


def attn_bwd(P, dy, y, lse, tabs, g, d):
    S = P.shape[0]
    L = S // d
    T = min(512, L)
    nq = T // QB
    ni = L // T

    def body(qkv_ref, kh_ref, vh_ref, dy_ref, y_ref, lse_ref, c_ref, s1_ref, s2_ref,
             o_ref, dkc_ref, dvc_ref):
        i = pl.program_id(1)
        col, mcur, mprev, mfirst = _band_masks(i < ni - 1)

        @pl.when(i == 0)
        def _():
            dkc_ref[...] = jnp.zeros_like(dkc_ref)
            dvc_ref[...] = jnp.zeros_like(dvc_ref)

        def head(h, carry):
            off = pl.multiple_of(h * HD, HD)
            cq, ck, cv = pl.ds(off, HD), pl.ds(E + off, HD), pl.ds(2 * E + off, HD)
            pend_dk = dkc_ref[:, cq]
            pend_dv = dvc_ref[:, cq]
            for j in reversed(range(nq)):
                rows = slice(j * QB, (j + 1) * QB)
                q = qkv_ref[rows, cq]
                kc = qkv_ref[rows, ck]
                vc = qkv_ref[rows, cv]
                if j == 0:
                    kp, vp, pmask = kh_ref[:, cq], vh_ref[:, cq], mfirst
                else:
                    prow = slice((j - 1) * QB, j * QB)
                    kp, vp, pmask = qkv_ref[prow, ck], qkv_ref[prow, cv], mprev
                dyj = dy_ref[rows, cq]
                lse_h = jnp.sum(jnp.where(col == h, lse_ref[rows, :], 0.0), axis=1, keepdims=True)
                delta = jnp.sum(dyj.astype(F32) * y_ref[rows, cq].astype(F32), axis=1, keepdims=True)
                pc = jnp.exp(_dot_nt(q, kc) * SCALE + mcur - lse_h)
                pp = jnp.exp(_dot_nt(q, kp) * SCALE + pmask - lse_h)
                dsc = pc * (_dot_nt(dyj, vc) - delta) * SCALE
                dsp = pp * (_dot_nt(dyj, vp) - delta) * SCALE
                dq = _dot(dsc.astype(BF16), kc) + _dot(dsp.astype(BF16), kp)
                dk = _dot(dsc.T.astype(BF16), q) + pend_dk
                dv = _dot(pc.T.astype(BF16), dyj) + pend_dv
                pend_dk = _dot(dsp.T.astype(BF16), q)
                pend_dv = _dot(pp.T.astype(BF16), dyj)
                c, s1, s2 = c_ref[rows, :], s1_ref[rows, :], s2_ref[rows, :]
                o_ref[rows, cq] = _unrope(dq, c, s1, s2).astype(BF16)
                o_ref[rows, ck] = _unrope(dk, c, s1, s2).astype(BF16)
                o_ref[rows, cv] = dv.astype(BF16)
            dkc_ref[:, cq] = pend_dk
            dvc_ref[:, cq] = pend_dv
            return carry

        lax.fori_loop(0, NH, head, 0, unroll=2)

    blk = lambda r, i: r * ni + ni - 1 - i
    halo = lambda r, i: jnp.maximum(r * (L // QB) + (ni - 1 - i) * nq - 1, 0)
    main = pl.BlockSpec((T, SEG), lambda r, i: (blk(r, i), 0))
    wide = pl.BlockSpec((T, E), lambda r, i: (blk(r, i), 0))
    narrow = pl.BlockSpec((T, HD), lambda r, i: (blk(r, i), 0))
    return pl.pallas_call(
        body, name=f"attn_bwd{g}", grid=(d, ni),
        in_specs=[main, pl.BlockSpec((QB, E), lambda r, i: (halo(r, i), 1)),
                  pl.BlockSpec((QB, E), lambda r, i: (halo(r, i), 2)),
                  wide, wide, narrow, narrow, narrow, narrow],
        out_specs=main, out_shape=SDS((S, SEG), BF16),
        scratch_shapes=[pltpu.VMEM((QB, E), F32), pltpu.VMEM((QB, E), F32)],
        compiler_params=_cp(("arbitrary", "arbitrary"), 56),
    )(P, P, P, dy, y, lse, *tabs)


def _pool_cnt(t0, rows):
    t = (lax.broadcasted_iota(jnp.int32, (rows, E), 0) + t0 + 1).astype(F32)
    ch = lax.broadcasted_iota(jnp.int32, (rows, E), 1)
    w = jnp.where(ch < PC, 2.0, jnp.where(ch < 2 * PC, 4.0, jnp.where(ch < 3 * PC, 8.0, 16.0)))
    return jnp.minimum(t, w)


def _by_group(parts):
    return jnp.concatenate([parts[g][:, g * PC:(g + 1) * PC] for g in range(4)], axis=1)


def pool_fwd(uz):
    S = uz.shape[0]
    ts = 256

    def body(u_ref, h_ref, o_ref, ot_ref):
        i = pl.program_id(0)
        u = u_ref[...]
        halo = jnp.where(i > 0, h_ref[...], 0.0)
        ext = jnp.concatenate([halo, u], axis=0)
        s2 = ext + pltpu.roll(ext, 1, 0)
        s4 = s2 + pltpu.roll(s2, 2, 0)
        s8 = s4 + pltpu.roll(s4, 4, 0)
        s16 = s8 + pltpu.roll(s8, 8, 0)
        win = _by_group([s2, s4, s8, s16])[16:, :]
        pooled = win / _pool_cnt(i * ts, ts) - u
        o_ref[...] = pooled.astype(BF16)
        ot_ref[...] = pooled.T.astype(BF16)

    return pl.pallas_call(
        body, name="pool_fwd", grid=(S // ts,),
        in_specs=[pl.BlockSpec((ts, E), lambda i: (i, 0)),
                  pl.BlockSpec((16, E), lambda i: (jnp.maximum(i * (ts // 16) - 1, 0), 0))],
        out_specs=[pl.BlockSpec((ts, E), lambda i: (i, 0)), pl.BlockSpec((E, ts), lambda i: (0, i))],
        out_shape=[SDS((S, E), BF16), SDS((E, S), BF16)],
        compiler_params=_cp(("parallel",), 48),
    )(uz, uz)


def pool_bwd(dpooled, duz):
    S = dpooled.shape[0]
    ts = 256
    nt = S // ts

    def body(d_ref, h_ref, alias_ref, o_ref):
        i = pl.program_id(0)
        dp = d_ref[...].astype(F32)
        halo = jnp.where(i < nt - 1, h_ref[...].astype(F32), 0.0)
        n = ts + 16
        ext = jnp.concatenate([dp, halo], axis=0) / _pool_cnt(i * ts, n)
        f2 = ext + pltpu.roll(ext, n - 1, 0)
        f4 = f2 + pltpu.roll(f2, n - 2, 0)
        f8 = f4 + pltpu.roll(f4, n - 4, 0)
        f16 = f8 + pltpu.roll(f8, n - 8, 0)
        win = _by_group([f2, f4, f8, f16])[:ts, :]
        o_ref[...] = (win - dp).astype(BF16)

    return pl.pallas_call(
        body, name="pool_bwd", grid=(nt,),
        in_specs=[pl.BlockSpec((ts, E), lambda i: (i, 0)),
                  pl.BlockSpec((16, E), lambda i: (jnp.minimum((i + 1) * (ts // 16), S // 16 - 1), 0)), ANY],
        out_specs=pl.BlockSpec((ts, E), lambda i: (i, 0)),
        out_shape=SDS(duz.shape, BF16),
        input_output_aliases={2: 0},
        compiler_params=_cp(("parallel",), 48),
    )(dpooled, dpooled, duz)


def mm_grp(pooled, wg, b, scale, uz):
    S = pooled.shape[0]
    tm = 512

    def body(p_ref, w_ref, b_ref, s_ref, z_ref, h_ref, y_ref, yt_ref):
        for g in range(4):
            cs = slice(g * PC, (g + 1) * PC)
            h = _dot(p_ref[:, cs], w_ref[g]) + b_ref[:, cs]
            z = z_ref[:, cs]
            yp = h * s_ref[:, cs] * (z * _sigmoid(z))
            h_ref[:, cs] = h.astype(BF16)
            y_ref[:, cs] = yp.astype(BF16)
            yt_ref[cs, :] = yp.T.astype(BF16)

    row = pl.BlockSpec((tm, E), lambda i: (i, 0))
    vec = pl.BlockSpec((1, E), lambda i: (0, 0))
    return pl.pallas_call(
        body, name="mm_grp", grid=(S // tm,),
        in_specs=[row, pl.BlockSpec((4, PC, PC), lambda i: (0, 0, 0)), vec, vec,
                  pl.BlockSpec((tm, E), lambda i: (i, 1))],
        out_specs=[row, row, pl.BlockSpec((E, tm), lambda i: (0, i))],
        out_shape=[SDS((S, E), BF16), SDS((S, E), BF16), SDS((E, S), BF16)],
        compiler_params=_cp(("parallel",), 48),
    )(pooled, wg, b, scale, uz)


def mm_dyp(da1, w_out, uz, h, scale):
    S = da1.shape[0]
    tm = 512

    def body(a_ref, w_ref, z_ref, h_ref, s_ref, dh_ref, dz_ref, dsc_ref, db_ref):
        @pl.when(pl.program_id(0) == 0)
        def _():
            dsc_ref[...] = jnp.zeros_like(dsc_ref)
            db_ref[...] = jnp.zeros_like(db_ref)

        def chunk(cidx, carry):
            col = pl.ds(pl.multiple_of(cidx * 256, 256), 256)
            dyp = _dot_nt(a_ref[...], w_ref[col, :])
            z = z_ref[:, col]
            hh = h_ref[:, col].astype(F32)
            sc = s_ref[:, col]
            sig = _sigmoid(z)
            dhs = dyp * z * sig
            dz_ref[:, col] = (dyp * hh * sc * sig * (1.0 + z * (1.0 - sig))).astype(BF16)
            dh = dhs * sc
            dh_ref[:, col] = dh.astype(BF16)
            dsc_ref[:, col] += _fold8(dhs * hh)
            db_ref[:, col] += _fold8(dh)
            return carry

        lax.fori_loop(0, E // 256, chunk, 0)

    row = pl.BlockSpec((tm, E), lambda i: (i, 0))
    acc = pl.BlockSpec((8, E), lambda i: (0, 0))
    return pl.pallas_call(
        body, name="mm_dyp", grid=(S // tm,),
        in_specs=[pl.BlockSpec((tm, D), lambda i: (i, 0)), pl.BlockSpec((E, D), lambda i: (0, 0)),
                  pl.BlockSpec((tm, E), lambda i: (i, 1)), row, pl.BlockSpec((1, E), lambda i: (0, 0))],
        out_specs=[row, pl.BlockSpec((tm, E), lambda i: (i, 1)), acc, acc],
        out_shape=[SDS((S, E), BF16), SDS((S, 2 * E), BF16), SDS((8, E), F32), SDS((8, E), F32)],
        compiler_params=_cp(("arbitrary",), 48),
    )(da1, w_out, uz, h, scale)


def mm_dpooled(dh, wg):
    S = dh.shape[0]
    tm = 1024

    def body(a_ref, w_ref, o_ref):
        for g in range(4):
            cs = slice(g * PC, (g + 1) * PC)
            o_ref[:, cs] = _dot_nt(a_ref[:, cs], w_ref[g]).astype(BF16)

    row = pl.BlockSpec((tm, E), lambda i: (i, 0))
    return pl.pallas_call(
        body, name="mm_dpooled", grid=(S // tm,),
        in_specs=[row, pl.BlockSpec((4, PC, PC), lambda i: (0, 0, 0))],
        out_specs=row, out_shape=SDS((S, E), BF16),
        compiler_params=_cp(("parallel",), 48),
    )(dh, wg)


def _rope_tables(positions):
    inv_freq = 500000.0 ** (-jnp.arange(0, 32, 2, dtype=F32) / 32)
    ang = positions.astype(F32)[:, None] * inv_freq
    cos, sin = jnp.cos(ang), jnp.sin(ang)
    S = positions.shape[0]
    one = jnp.ones((S, HD - 32), F32)
    zero16 = jnp.zeros((S, 16), F32)
    zero = jnp.zeros((S, HD - 32), F32)
    c = jnp.concatenate([cos, cos, one], axis=1)
    s1 = jnp.concatenate([-sin, zero16, zero], axis=1)
    s2 = jnp.concatenate([zero16, sin, zero], axis=1)
    return c, s1, s2


def kernel(x, positions, norm_pre, norm_post, attn_w_in, attn_w_out, pool_w_in, pool_w_grp, pool_b_grp, pool_scale, pool_w_out, loss_target, m_norm_pre, m_norm_post, m_attn_w_in, m_attn_w_out, m_pool_w_in, m_pool_w_grp, m_pool_b_grp, m_pool_scale, m_pool_w_out, v_norm_pre, v_norm_post, v_attn_w_in, v_attn_w_out, v_pool_w_in, v_pool_w_grp, v_pool_b_grp, v_pool_scale, v_pool_w_out):
    S = x.shape[1]
    xi, yi, ci = _mesh_pos()
    dev = 4 * xi + 2 * yi + ci
    x2 = x[0]
    tgt = loss_target[0]

    small = jnp.concatenate([pool_b_grp[0].reshape(2, HD), pool_scale[0].reshape(2, HD),
                             jnp.zeros((4, HD), F32)], axis=0)
    w_in8, w_out8, wp_in8, wg8, wp_out8, small8 = all_gather(
        [attn_w_in[0].astype(BF16), attn_w_out[0].astype(BF16), pool_w_in[0].astype(BF16),
         pool_w_grp[0].astype(BF16), pool_w_out[0].astype(BF16), small], "gather_weights")
    w_out = w_out8.reshape(E, D)
    wp_out = wp_out8.reshape(E, D)
    wg = wg8.transpose(1, 0, 2, 3).reshape(4, PC, PC)
    b_full = small8[:, 0:2, :].reshape(N_DEV, 4, PC // N_DEV).transpose(1, 0, 2).reshape(1, E)
    scale_full = small8[:, 2:4, :].reshape(1, E)

    pos = positions[0]
    tabs = [_rope_tables(pos.reshape(S // d, d).T.reshape(S)) for d in DIL]
    ehot = (jnp.arange(E)[None, :] // HD == jnp.arange(HD)[:, None]).astype(BF16)
    seg_tiles = SEG // CT

    xn0, xn0_4, xn0_16, xn0t = norm_pre0(x2, norm_pre[0:1])
    xn0s = [xn0, xn0_4.reshape(S, D), xn0_16.reshape(S, D)]
    xn0ts = [xn0t, transpose_rows(xn0s[1], "xn0t_4"), transpose_rows(xn0s[2], "xn0t_16")]
    Ps, os_, lses = [], [], []
    for g, d in enumerate(DIL):
        P = mm_in(xn0s[g], w_in8, g * seg_tiles, seg_tiles, tabs[g], f"mm_qkv{g}")
        o, l = attn_fwd(P, g, d)
        Ps.append(P)
        os_.append(o)
        lses.append(l)
    z0 = mm_in(xn0, w_in8, 3 * seg_tiles, E // CT, None, "mm_z0")
    ys, lse3, ya, yat = combine_fwd(os_, lses, z0, ehot)
    a0 = mm_rows(ya, w_out, "mm_out0", F32)
    h1, xn1, xn1t = post0_pre1(x2, a0, norm_post[0:1], norm_pre[1:2])

    uz = mm_uz(xn1, wp_in8)
    pooled, pooled_t = pool_fwd(uz)
    hgrp, yp, ypt = mm_grp(pooled, wg, b_full, scale_full, uz)
    a1 = mm_rows(yp, wp_out, "mm_out1", F32)
    dh2, da1, loss_rows, dg_post1 = post1_loss(h1, a1, tgt, norm_post[1:2])
    loss = lax.psum(0.5 / D * jnp.sum(loss_rows), ("x", "y", "c"))

    dh, duz, dscale_p, db_p = mm_dyp(da1, wp_out, uz, hgrp, scale_full)
    dpooled = mm_dpooled(dh, wg)
    duz = pool_bwd(dpooled, duz)
    g_wg = mm_dwg(pooled_t, dh)
    g_wp_out = mm_wgrad_rows(ypt, da1, "mm_dwp_out")
    g_wp_in = mm_wgrad_cols(xn1t, duz, "mm_dwp_in", ncols=2 * E, shard=PC, tn=1024)
    dxn1 = mm_dx_stack(duz, wp_in8, "mm_dxn1", kchunk=PC, tm=1024)
    dh1, da0, dg_pre1, dg_post0 = mid_bwd(dxn1, dh2, h1, a0, norm_pre[1:2], norm_post[0:1])

    dys, dz0 = mm_dya(da0, w_out, z0, ys[0])
    g_w_out = mm_wgrad_rows(yat, da0, "mm_dw_out")
    g_w_in = mm_dw_in_part(xn0t, dz0, 3 * seg_tiles, None, "mm_dw_in_z")
    dxs = [mm_dx_part(dz0, w_in8, 3 * seg_tiles, "mm_dxn0_z")]
    for g, d in enumerate(DIL):
        dP = attn_bwd(Ps[g], dys[g], ys[g], lse3[g], tabs[g], g, d)
        g_w_in = mm_dw_in_part(xn0ts[g], dP, g * seg_tiles, g_w_in, f"mm_dw_in{g}")
        dxs.append(mm_dx_part(dP, w_in8, g * seg_tiles, f"mm_dxn0_{g}"))
    grad_x, dg_pre0 = pre0_bwd(dxs[1], dxs[0], dxs[2], dxs[3], dh1, x2, norm_pre[0:1])

    cidx = ci.astype(jnp.int32).reshape(1)
    fulls = [g_w_in, g_w_out.reshape(N_DEV, E // N_DEV, D), g_wp_in,
             g_wg.reshape(4, N_DEV, PC // N_DEV, PC).transpose(1, 0, 2, 3).reshape(N_DEV, 4 * PC // N_DEV, PC),
             g_wp_out.reshape(N_DEV, E // N_DEV, D)]
    sibs = rs_pair(fulls, "rs_pair")
    parts = [pair_add(f, s, cidx, f"pair_add{k}") for k, (f, s) in enumerate(zip(fulls, sibs))]
    recvs = rs_chips(parts, "rs_chips")
    shards = [(attn_w_in, m_attn_w_in, v_attn_w_in), (attn_w_out, m_attn_w_out, v_attn_w_out),
              (pool_w_in, m_pool_w_in, v_pool_w_in), (pool_w_grp, m_pool_w_grp, v_pool_w_grp),
              (pool_w_out, m_pool_w_out, v_pool_w_out)]
    big = []
    for k, (recv, (w, m, v)) in enumerate(zip(recvs, shards)):
        shp = w.shape
        r2 = recv.shape[1:]
        res = adamw_sum(recv, w.reshape(r2), m.reshape(r2), v.reshape(r2), f"adamw{k}")
        big.append([t.reshape(shp) for t in res])

    smalls = jnp.concatenate([dg_pre0.sum(0, keepdims=True), dg_pre1.sum(0, keepdims=True),
                              dg_post0.sum(0, keepdims=True), dg_post1.sum(0, keepdims=True),
                              db_p.sum(0).reshape(2, D), dscale_p.sum(0).reshape(2, D)], axis=0)
    (smalls8,) = all_gather([smalls], "gather_small_grads")
    tot = sum_slots(smalls8, "sum_small_grads")
    g_norm_pre, g_norm_post = tot[0:2], tot[2:4]
    g_b = lax.dynamic_slice_in_dim(tot[4:6].reshape(4, PC), dev * (PC // N_DEV), PC // N_DEV, axis=1)[None]
    g_scale = lax.dynamic_slice_in_dim(tot[6:8].reshape(1, E), dev * (E // N_DEV), E // N_DEV, axis=1)
    sm = [adamw_small(g_norm_pre, norm_pre, m_norm_pre, v_norm_pre, "adamw_norm_pre"),
          adamw_small(g_norm_post, norm_post, m_norm_post, v_norm_post, "adamw_norm_post"),
          adamw_small(g_b, pool_b_grp, m_pool_b_grp, v_pool_b_grp, "adamw_b"),
          adamw_small(g_scale, pool_scale, m_pool_scale, v_pool_scale, "adamw_scale")]

    grads = [g_norm_pre, g_norm_post, big[0][0], big[1][0], big[2][0], big[3][0], g_b, g_scale, big[4][0]]

    def pick(k):
        return [sm[0][k - 1], sm[1][k - 1], big[0][k], big[1][k], big[2][k], big[3][k], sm[2][k - 1], sm[3][k - 1],
                big[4][k]]

    return (loss, grad_x[None], *grads, *pick(1), *pick(2), *pick(3))
```

```python
import functools
import math

import numpy as np
import jax
import jax.numpy as jnp
from jax import lax
from jax.experimental import pallas as pl
from jax.experimental.pallas import tpu as pltpu

F32 = jnp.float32
BF16 = jnp.bfloat16
SDS = jax.ShapeDtypeStruct

N_DEV = 8
D = 1024
E = 2048
HD = 128
NH = E // HD
DIL = (1, 4, 16)
QB = 128
SEG = 3 * E
W_IN_COLS = 3 * SEG + E
W_SHARD = W_IN_COLS // N_DEV
CT = 512
POOL_WIN = (2, 4, 8, 16)
PC = E // 4
EPS = 1e-6
NEG = -1e30
SCALE = 1.0 / math.sqrt(HD)
LR, B1, B2, ADAM_EPS, WD, STEP = 0.001, 0.9, 0.999, 1e-08, 0.01, 10
MIB = 1024 * 1024
ANY = pl.BlockSpec(memory_space=pl.ANY)
MESH = pl.DeviceIdType.MESH


def _cp(sem, mb):
    return pltpu.CompilerParams(dimension_semantics=sem, vmem_limit_bytes=mb * MIB)


def _dot(a, b):
    return jnp.dot(a, b, preferred_element_type=F32)


def _dot_nt(a, b):
    return lax.dot_general(a, b, (((1,), (1,)), ((), ())), preferred_element_type=F32)


def _rms(h):
    return lax.rsqrt(jnp.mean(h * h, axis=-1, keepdims=True) + EPS)


def _row_tile(R, C, budget):
    tr = R
    while tr * C * 4 > budget and tr % 16 == 0:
        tr //= 2
    return tr


def _fold8(t):
    return t.reshape(t.shape[0] // 8, 8, t.shape[1]).sum(axis=0)


def _sigmoid(z):
    return 1.0 / (1.0 + jnp.exp(-z))


LANES = 128


def _scr(rows, C):
    return pltpu.VMEM((C // LANES, rows, LANES), F32)


def _scr_put(scr, val):
    for c in range(scr.shape[0]):
        scr[c] = val[:, c * LANES:(c + 1) * LANES]


def _scr_get(scr):
    return jnp.concatenate([scr[c] for c in range(scr.shape[0])], axis=1)


def _store_perm(dst_ref, scr, d):
    n = dst_ref.shape[1]
    for r in range(d):
        for c in range(scr.shape[0]):
            dst_ref[r, :, c * LANES:(c + 1) * LANES] = scr[c, pl.ds(r, n, stride=d), :].astype(dst_ref.dtype)


def _load_perm(scr, src_ref, d, add=False):
    n = src_ref.shape[1]
    for r in range(d):
        rows = pl.ds(r, n, stride=d)
        for c in range(scr.shape[0]):
            v = src_ref[r, :, c * LANES:(c + 1) * LANES].astype(F32)
            scr[c, rows, :] = scr[c, rows, :] + v if add else v


def _rope(t, c, s1, s2):
    t = t.astype(BF16)
    return t * c + pltpu.roll(t, HD - 16, 1) * s1 + pltpu.roll(t, 16, 1) * s2


def _unrope(t, c, s1, s2):
    t = t.astype(BF16)
    return t * c - pltpu.roll(t, HD - 16, 1) * s1 - pltpu.roll(t, 16, 1) * s2


def _mesh_pos():
    return lax.axis_index("x"), lax.axis_index("y"), lax.axis_index("c")


def all_gather(arrs, name):
    n = len(arrs)

    def body(*refs):
        ins, outs = refs[:n], refs[n:2 * n]
        send_sems, recv_sems, local_sems = refs[2 * n:]
        x, y, c = _mesh_pos()
        me, sib = (x, y, c), (x, y, 1 - c)
        chips = [(1 - x, y), (x, 1 - y), (1 - x, 1 - y)]

        def slot(p):
            return 4 * p[0] + 2 * p[1] + p[2]

        def copy(a, k, block, to, src=None):
            dst = outs[a].at[slot(block)]
            return pltpu.make_async_remote_copy(
                src_ref=dst if src is None else src, dst_ref=dst,
                send_sem=send_sems.at[a, k], recv_sem=recv_sems.at[a, k],
                device_id=to, device_id_type=MESH)

        mine = [pltpu.make_async_copy(ins[a], outs[a].at[slot(me)], local_sems.at[a]) for a in range(n)]
        for cp in mine:
            cp.start()
        first = []
        for a in range(n):
            first.append(copy(a, 0, me, sib, src=ins[a]))
            for j, chip in enumerate(chips):
                first.append(copy(a, 1 + j, me, (*chip, c), src=ins[a]))
        for cp in first:
            cp.start()
        passed = []
        for j, chip in enumerate(chips):
            for a in range(n):
                copy(a, 1 + j, (*chip, c), me).wait_recv()
                fw = copy(a, 4 + j, (*chip, c), sib)
                fw.start()
                passed.append(fw)
        for a in range(n):
            copy(a, 0, sib, me).wait_recv()
        for j, chip in enumerate(chips):
            for a in range(n):
                copy(a, 4 + j, (*chip, 1 - c), me).wait_recv()
        for cp in first + passed:
            cp.wait_send()
        for cp in mine:
            cp.wait()

    return pl.pallas_call(
        body, name=name,
        out_shape=[SDS((N_DEV,) + a.shape, a.dtype) for a in arrs],
        in_specs=[ANY] * n, out_specs=[ANY] * n,
        scratch_shapes=[pltpu.SemaphoreType.DMA((n, 7)), pltpu.SemaphoreType.DMA((n, 7)),
                        pltpu.SemaphoreType.DMA((n,))],
    )(*arrs)


def rs_pair(arrs, name):
    n = len(arrs)

    def body(*refs):
        ins, outs = refs[:n], refs[n:2 * n]
        send_sems, recv_sems = refs[2 * n:]
        x, y, c = _mesh_pos()
        cps = []
        for a in range(n):
            for q in range(4):
                cps.append(pltpu.make_async_remote_copy(
                    src_ref=ins[a].at[2 * q + (1 - c)], dst_ref=outs[a].at[q],
                    send_sem=send_sems.at[a, q], recv_sem=recv_sems.at[a, q],
                    device_id=(x, y, 1 - c), device_id_type=MESH))
        for cp in cps:
            cp.start()
        for cp in cps:
            cp.wait()

    return pl.pallas_call(
        body, name=name,
        out_shape=[SDS((4,) + a.shape[1:], a.dtype) for a in arrs],
        in_specs=[ANY] * n, out_specs=[ANY] * n,
        scratch_shapes=[pltpu.SemaphoreType.DMA((n, 4)), pltpu.SemaphoreType.DMA((n, 4))],
    )(*arrs)


def rs_chips(parts, name):
    n = len(parts)

    def body(*refs):
        ins, outs = refs[:n], refs[n:2 * n]
        send_sems, recv_sems, local_sems = refs[2 * n:]
        x, y, c = _mesh_pos()
        mychip = 2 * x + y
        chips = [(1 - x, y), (x, 1 - y), (1 - x, 1 - y)]
        mine = [pltpu.make_async_copy(ins[a].at[mychip], outs[a].at[mychip], local_sems.at[a]) for a in range(n)]
        for cp in mine:
            cp.start()
        cps = []
        for a in range(n):
            for j, chip in enumerate(chips):
                q = 2 * chip[0] + chip[1]
                cps.append(pltpu.make_async_remote_copy(
                    src_ref=ins[a].at[q], dst_ref=outs[a].at[mychip],
                    send_sem=send_sems.at[a, j], recv_sem=recv_sems.at[a, j],
                    device_id=(*chip, c), device_id_type=MESH))
        for cp in cps:
            cp.start()
        for cp in cps:
            cp.wait()
        for cp in mine:
            cp.wait()

    return pl.pallas_call(
        body, name=name,
        out_shape=[SDS(a.shape, a.dtype) for a in parts],
        in_specs=[ANY] * n, out_specs=[ANY] * n,
        scratch_shapes=[pltpu.SemaphoreType.DMA((n, 3)), pltpu.SemaphoreType.DMA((n, 3)),
                        pltpu.SemaphoreType.DMA((n,))],
    )(*parts)


def pair_add(full, sib, cidx, name):
    _, R, C = full.shape
    tr = _row_tile(R, C, MIB)

    def body(c_ref, a_ref, b_ref, o_ref):
        o_ref[...] = (a_ref[...] + b_ref[...]).astype(BF16)

    return pl.pallas_call(
        body, name=name,
        grid_spec=pltpu.PrefetchScalarGridSpec(
            num_scalar_prefetch=1, grid=(4, R // tr),
            in_specs=[pl.BlockSpec((None, tr, C), lambda q, i, cr: (2 * q + cr[0], i, 0)),
                      pl.BlockSpec((None, tr, C), lambda q, i, cr: (q, i, 0))],
            out_specs=pl.BlockSpec((None, tr, C), lambda q, i, cr: (q, i, 0))),
        out_shape=SDS((4, R, C), BF16),
        compiler_params=_cp(("parallel", "parallel"), 32),
    )(cidx, full, sib)


def _adam_math(w, g, m, v):
    m2 = B1 * m + (1.0 - B1) * g
    v2 = B2 * v + (1.0 - B2) * (g * g)
    m_hat = m2 / (1.0 - B1 ** STEP)
    v_hat = v2 / (1.0 - B2 ** STEP)
    delta = -LR * (m_hat / (jnp.sqrt(v_hat) + ADAM_EPS) + WD * w)
    return delta, m2, v2


def adamw_sum(recv, w, m, v, name):
    K, R, C = recv.shape
    tr = _row_tile(R, C, MIB)

    def body(r_ref, w_ref, m_ref, v_ref, g_ref, d_ref, m2_ref, v2_ref):
        g = r_ref[0].astype(F32)
        for k in range(1, K):
            g = g + r_ref[k].astype(F32)
        delta, m2, v2 = _adam_math(w_ref[...], g, m_ref[...], v_ref[...])
        g_ref[...] = g
        d_ref[...] = delta
        m2_ref[...] = m2
        v2_ref[...] = v2

    tile = pl.BlockSpec((tr, C), lambda i: (i, 0))
    return pl.pallas_call(
        body, name=name, grid=(R // tr,),
        in_specs=[pl.BlockSpec((K, tr, C), lambda i: (0, i, 0)), tile, tile, tile],
        out_specs=[tile] * 4, out_shape=[SDS((R, C), F32)] * 4,
        compiler_params=_cp(("parallel",), 32),
    )(recv, w, m, v)


def adamw_small(g, w, m, v, name):
    def body(g_ref, w_ref, m_ref, v_ref, d_ref, m2_ref, v2_ref):
        delta, m2, v2 = _adam_math(w_ref[...], g_ref[...], m_ref[...], v_ref[...])
        d_ref[...] = delta
        m2_ref[...] = m2
        v2_ref[...] = v2

    return pl.pallas_call(body, name=name, out_shape=[SDS(w.shape, F32)] * 3)(g, w, m, v)


def sum_slots(a, name):
    K = a.shape[0]

    def body(a_ref, o_ref):
        t = a_ref[0]
        for k in range(1, K):
            t = t + a_ref[k]
        o_ref[...] = t

    return pl.pallas_call(body, name=name, out_shape=SDS(a.shape[1:], F32))(a)


def norm_pre0(x, g):
    S = x.shape[0]
    ts = 512

    def body(x_ref, g_ref, o_ref, o4_ref, o16_ref, ot_ref, scr):
        h = x_ref[...]
        xn = h * _rms(h) * g_ref[...]
        o_ref[...] = xn.astype(BF16)
        ot_ref[...] = xn.T.astype(BF16)
        _scr_put(scr, xn)
        _store_perm(o4_ref, scr, 4)
        _store_perm(o16_ref, scr, 16)

    return pl.pallas_call(
        body, name="norm_pre0", grid=(S // ts,),
        in_specs=[pl.BlockSpec((ts, D), lambda i: (i, 0)), pl.BlockSpec((1, D), lambda i: (0, 0))],
        out_specs=[pl.BlockSpec((ts, D), lambda i: (i, 0)),
                   pl.BlockSpec((4, ts // 4, D), lambda i: (0, i, 0)),
                   pl.BlockSpec((16, ts // 16, D), lambda i: (0, i, 0)),
                   pl.BlockSpec((D, ts), lambda i: (0, i))],
        out_shape=[SDS((S, D), BF16), SDS((4, S // 4, D), BF16), SDS((16, S // 16, D), BF16), SDS((D, S), BF16)],
        scratch_shapes=[_scr(ts, D)],
        compiler_params=_cp(("parallel",), 32),
    )(x, g)


def transpose_rows(a, name):
    S, C = a.shape
    ts = 512

    def body(a_ref, o_ref):
        o_ref[...] = a_ref[...].astype(F32).T.astype(BF16)

    return pl.pallas_call(
        body, name=name, grid=(S // ts,),
        in_specs=[pl.BlockSpec((ts, C), lambda i: (i, 0))],
        out_specs=pl.BlockSpec((C, ts), lambda i: (0, i)),
        out_shape=SDS((C, S), BF16),
        compiler_params=_cp(("parallel",), 32),
    )(a)


def post0_pre1(x, a0, g_post, g_pre):
    S = x.shape[0]
    ts = 512

    def body(x_ref, a_ref, gp_ref, gn_ref, h_ref, o_ref, ot_ref):
        a = a_ref[...]
        h1 = x_ref[...] + a * _rms(a) * gp_ref[...]
        h_ref[...] = h1
        xn = h1 * _rms(h1) * gn_ref[...]
        o_ref[...] = xn.astype(BF16)
        ot_ref[...] = xn.T.astype(BF16)

    row = pl.BlockSpec((ts, D), lambda i: (i, 0))
    vec = pl.BlockSpec((1, D), lambda i: (0, 0))
    return pl.pallas_call(
        body, name="post0_pre1", grid=(S // ts,),
        in_specs=[row, row, vec, vec],
        out_specs=[row, row, pl.BlockSpec((D, ts), lambda i: (0, i))],
        out_shape=[SDS((S, D), F32), SDS((S, D), BF16), SDS((D, S), BF16)],
        compiler_params=_cp(("parallel",), 40),
    )(x, a0, g_post, g_pre)


def post1_loss(h1, a1, target, g_post):
    S = h1.shape[0]
    ts = 512

    def body(h_ref, a_ref, t_ref, g_ref, dh_ref, da_ref, loss_ref, dg_ref):
        @pl.when(pl.program_id(0) == 0)
        def _():
            loss_ref[...] = jnp.zeros_like(loss_ref)
            dg_ref[...] = jnp.zeros_like(dg_ref)

        a = a_ref[...]
        g = g_ref[...]
        rp = _rms(a)
        yhat = a * rp
        e = h_ref[...] + yhat * g - t_ref[...]
        loss_ref[...] += _fold8(e * e)
        dh = e * (1.0 / D)
        dh_ref[...] = dh
        dg_ref[...] += _fold8(dh * yhat)
        dyh = dh * g
        da = rp * (dyh - yhat * jnp.mean(dyh * yhat, axis=-1, keepdims=True))
        da_ref[...] = da.astype(BF16)

    row = pl.BlockSpec((ts, D), lambda i: (i, 0))
    acc = pl.BlockSpec((8, D), lambda i: (0, 0))
    return pl.pallas_call(
        body, name="post1_loss", grid=(S // ts,),
        in_specs=[row, row, row, pl.BlockSpec((1, D), lambda i: (0, 0))],
        out_specs=[row, row, acc, acc],
        out_shape=[SDS((S, D), F32), SDS((S, D), BF16), SDS((8, D), F32), SDS((8, D), F32)],
        compiler_params=_cp(("arbitrary",), 40),
    )(h1, a1, target, g_post)


def mid_bwd(dxn1, dh2, h1, a0, g_pre1, g_post0):
    S = h1.shape[0]
    ts = 512

    def body(dx_ref, dh2_ref, h_ref, a_ref, gn_ref, gp_ref, dh1_ref, da_ref, dgn_ref, dgp_ref):
        @pl.when(pl.program_id(0) == 0)
        def _():
            dgn_ref[...] = jnp.zeros_like(dgn_ref)
            dgp_ref[...] = jnp.zeros_like(dgp_ref)

        h = h_ref[...]
        r1 = _rms(h)
        xhat = h * r1
        dxn = dx_ref[...]
        dgn_ref[...] += _fold8(dxn * xhat)
        dxh = dxn * gn_ref[...]
        dh1 = dh2_ref[...] + r1 * (dxh - xhat * jnp.mean(dxh * xhat, axis=-1, keepdims=True))
        dh1_ref[...] = dh1
        a = a_ref[...]
        rp = _rms(a)
        yhat = a * rp
        dgp_ref[...] += _fold8(dh1 * yhat)
        dyh = dh1 * gp_ref[...]
        da = rp * (dyh - yhat * jnp.mean(dyh * yhat, axis=-1, keepdims=True))
        da_ref[...] = da.astype(BF16)

    row = pl.BlockSpec((ts, D), lambda i: (i, 0))
    vec = pl.BlockSpec((1, D), lambda i: (0, 0))
    acc = pl.BlockSpec((8, D), lambda i: (0, 0))
    return pl.pallas_call(
        body, name="mid_bwd", grid=(S // ts,),
        in_specs=[row, row, row, row, vec, vec],
        out_specs=[row, row, acc, acc],
        out_shape=[SDS((S, D), F32), SDS((S, D), BF16), SDS((8, D), F32), SDS((8, D), F32)],
        compiler_params=_cp(("arbitrary",), 48),
    )(dxn1, dh2, h1, a0, g_pre1, g_post0)


def pre0_bwd(dx_tok, dx_z, dx4, dx16, dh1, x, g_pre0):
    S = x.shape[0]
    ts = 512

    def body(da_ref, dz_ref, d4_ref, d16_ref, dh_ref, x_ref, g_ref, gx_ref, dg_ref, scr):
        @pl.when(pl.program_id(0) == 0)
        def _():
            dg_ref[...] = jnp.zeros_like(dg_ref)

        _scr_put(scr, da_ref[...] + dz_ref[...])
        _load_perm(scr, d4_ref, 4, add=True)
        _load_perm(scr, d16_ref, 16, add=True)
        h = x_ref[...]
        r = _rms(h)
        xhat = h * r
        dxn = _scr_get(scr)
        dg_ref[...] += _fold8(dxn * xhat)
        dxh = dxn * g_ref[...]
        gx_ref[...] = dh_ref[...] + r * (dxh - xhat * jnp.mean(dxh * xhat, axis=-1, keepdims=True))

    row = pl.BlockSpec((ts, D), lambda i: (i, 0))
    return pl.pallas_call(
        body, name="pre0_bwd", grid=(S // ts,),
        in_specs=[row, row, pl.BlockSpec((4, ts // 4, D), lambda i: (0, i, 0)),
                  pl.BlockSpec((16, ts // 16, D), lambda i: (0, i, 0)), row, row,
                  pl.BlockSpec((1, D), lambda i: (0, 0))],
        out_specs=[row, pl.BlockSpec((8, D), lambda i: (0, 0))],
        out_shape=[SDS((S, D), F32), SDS((8, D), F32)],
        scratch_shapes=[_scr(ts, D)],
        compiler_params=_cp(("arbitrary",), 48),
    )(dx_tok, dx_z, dx4.reshape(4, S // 4, D), dx16.reshape(16, S // 16, D), dh1, x, g_pre0)


def _w_tile(tile0):
    per = W_SHARD // CT
    return lambda t: ((tile0 + t) // per, 0, (tile0 + t) % per)


def mm_in(xn, w8, tile0, ntiles, tabs, name):
    S = xn.shape[0]
    tm = 2048
    wt = _w_tile(tile0)

    def body(a_ref, b_ref, *rest):
        o_ref = rest[-1]
        rc = 512
        for u in range(tm // rc):
            rows = slice(u * rc, (u + 1) * rc)
            r = _dot(a_ref[rows, :], b_ref[...])
            if tabs is None:
                o_ref[rows, :] = r.astype(BF16)
                continue
            c_ref, s1_ref, s2_ref = rest[:3]
            rot = pl.program_id(1) < 2 * E // CT
            c = jnp.where(rot, c_ref[rows, :], 1.0)
            s1 = jnp.where(rot, s1_ref[rows, :], 0.0)
            s2 = jnp.where(rot, s2_ref[rows, :], 0.0)
            for hh in range(CT // HD):
                cs = slice(hh * HD, (hh + 1) * HD)
                o_ref[rows, cs] = _rope(r[:, cs], c, s1, s2).astype(BF16)

    tab = pl.BlockSpec((tm, HD), lambda i, t: (i, 0))
    return pl.pallas_call(
        body, name=name, grid=(S // tm, ntiles),
        in_specs=[pl.BlockSpec((tm, D), lambda i, t: (i, 0)),
                  pl.BlockSpec((None, D, CT), lambda i, t: wt(t))] + ([] if tabs is None else [tab] * 3),
        out_specs=pl.BlockSpec((tm, CT), lambda i, t: (i, t)),
        out_shape=SDS((S, ntiles * CT), BF16),
        compiler_params=_cp(("parallel", "parallel"), 48),
    )(xn, w8, *(() if tabs is None else tabs))


def mm_rows(a, b, name, out_dtype, tm=1024):
    M, K = a.shape
    N = b.shape[1]

    def body(a_ref, b_ref, o_ref):
        def chunk(cidx, carry):
            col = pl.ds(pl.multiple_of(cidx * 256, 256), 256)
            o_ref[:, col] = _dot(a_ref[...], b_ref[:, col]).astype(out_dtype)
            return carry

        lax.fori_loop(0, N // 256, chunk, 0)

    return pl.pallas_call(
        body, name=name, grid=(M // tm,),
        in_specs=[pl.BlockSpec((tm, K), lambda i: (i, 0)), pl.BlockSpec((K, N), lambda i: (0, 0))],
        out_specs=pl.BlockSpec((tm, N), lambda i: (i, 0)),
        out_shape=SDS((M, N), out_dtype),
        compiler_params=_cp(("parallel",), 48),
    )(a, b)


def mm_uz(xn, w8):
    S = xn.shape[0]
    tm = 512
    bw = w8.shape[2]

    def body(a_ref, b_ref, o_ref):
        def blk(dv, carry):
            for half in range(bw // 256):
                col = pl.ds(pl.multiple_of(dv * bw + half * 256, 256), 256)
                o_ref[:, col] = _dot(a_ref[...], b_ref[dv, :, half * 256:(half + 1) * 256])
            return carry

        lax.fori_loop(0, N_DEV, blk, 0)

    return pl.pallas_call(
        body, name="mm_uz", grid=(S // tm,),
        in_specs=[pl.BlockSpec((tm, D), lambda i: (i, 0)), pl.BlockSpec((N_DEV, D, bw), lambda i: (0, 0, 0))],
        out_specs=pl.BlockSpec((tm, N_DEV * bw), lambda i: (i, 0)),
        out_shape=SDS((S, N_DEV * bw), F32),
        compiler_params=_cp(("parallel",), 48),
    )(xn, w8)


def mm_acc(a, b, name, *, grid, a_spec, b_spec, o_spec, o_shape, acc_shape, write, vmem=48):
    nk = grid[-1]

    def body(a_ref, b_ref, o_ref, acc_ref):
        k = pl.program_id(len(grid) - 1)

        @pl.when(k == 0)
        def _():
            acc_ref[...] = jnp.zeros_like(acc_ref)

        acc_ref[...] += _dot(a_ref[...], b_ref[...])

        @pl.when(k == nk - 1)
        def _():
            write(o_ref, acc_ref)

    return pl.pallas_call(
        body, name=name, grid=grid, in_specs=[a_spec, b_spec], out_specs=o_spec, out_shape=o_shape,
        scratch_shapes=[pltpu.VMEM(acc_shape, F32)],
        compiler_params=_cp(("parallel",) * (len(grid) - 1) + ("arbitrary",), vmem),
    )(a, b)


def _write_plain(o_ref, acc_ref):
    o_ref[...] = acc_ref[...]


def mm_wgrad_rows(at, b, name):
    M, S = at.shape
    N = b.shape[1]
    tm, tk = 1024, 1024
    return mm_acc(at, b, name, grid=(M // tm, S // tk),
                  a_spec=pl.BlockSpec((tm, tk), lambda i, k: (i, k)),
                  b_spec=pl.BlockSpec((tk, N), lambda i, k: (k, 0)),
                  o_spec=pl.BlockSpec((tm, N), lambda i, k: (i, 0)),
                  o_shape=SDS((M, N), F32), acc_shape=(tm, N), write=_write_plain)


def mm_wgrad_cols(at, b, name, *, ncols, shard, tn):
    M, S = at.shape
    tk = 1024
    per = shard // tn if tn <= shard else 1
    nb = max(1, tn // shard)

    if tn <= shard:
        o_spec = pl.BlockSpec((None, M, tn), lambda t, k: (t // per, 0, t % per))
        write = _write_plain
    else:
        o_spec = pl.BlockSpec((nb, M, shard), lambda t, k: (t, 0, 0))

        def write(o_ref, acc_ref):
            for u in range(nb):
                o_ref[u] = acc_ref[:, u * shard:(u + 1) * shard]

    return mm_acc(at, b, name, grid=(ncols // tn, S // tk),
                  a_spec=pl.BlockSpec((M, tk), lambda t, k: (0, k)),
                  b_spec=pl.BlockSpec((tk, tn), lambda t, k: (k, t)),
                  o_spec=o_spec, o_shape=SDS((N_DEV, M, shard), F32), acc_shape=(M, tn), write=write)


def mm_dwg(pooled_t, dh):
    S = dh.shape[0]
    tk = 2048
    return mm_acc(pooled_t, dh, "mm_dwg", grid=(4, S // tk),
                  a_spec=pl.BlockSpec((PC, tk), lambda g, k: (g, k)),
                  b_spec=pl.BlockSpec((tk, PC), lambda g, k: (k, g)),
                  o_spec=pl.BlockSpec((None, PC, PC), lambda g, k: (g, 0, 0)),
                  o_shape=SDS((4, PC, PC), F32), acc_shape=(PC, PC), write=_write_plain)


def mm_dx_stack(da, w8, name, *, kchunk, tm):
    S = da.shape[0]
    bw = w8.shape[2]
    nck = bw // kchunk

    def body(a_ref, b_ref, o_ref, acc_ref):
        j = pl.program_id(1)

        @pl.when(j == 0)
        def _():
            acc_ref[...] = jnp.zeros_like(acc_ref)

        for u in range(nck):
            ks = slice(u * kchunk, (u + 1) * kchunk)
            acc_ref[...] += _dot_nt(a_ref[:, ks], b_ref[:, ks])

        @pl.when(j == N_DEV - 1)
        def _():
            o_ref[...] = acc_ref[...]

    return pl.pallas_call(
        body, name=name, grid=(S // tm, N_DEV),
        in_specs=[pl.BlockSpec((tm, bw), lambda i, j: (i, j)),
                  pl.BlockSpec((None, D, bw), lambda i, j: (j, 0, 0))],
        out_specs=pl.BlockSpec((tm, D), lambda i, j: (i, 0)),
        out_shape=SDS((S, D), F32),
        scratch_shapes=[pltpu.VMEM((tm, D), F32)],
        compiler_params=_cp(("parallel", "arbitrary"), 48),
    )(da, w8)


def mm_dw_in_part(at, b, tile0, prev, name):
    M, S = at.shape
    ntiles = b.shape[1] // CT
    tk = 2048
    nk = S // tk
    wt = _w_tile(tile0)

    def body(a_ref, b_ref, *rest):
        o_ref, acc_ref = rest[-2:]
        k = pl.program_id(1)

        @pl.when(k == 0)
        def _():
            acc_ref[...] = jnp.zeros_like(acc_ref)

        acc_ref[...] += _dot(a_ref[...], b_ref[...])

        @pl.when(k == nk - 1)
        def _():
            o_ref[...] = acc_ref[...]

    return pl.pallas_call(
        body, name=name, grid=(ntiles, nk),
        in_specs=[pl.BlockSpec((M, tk), lambda t, k: (0, k)), pl.BlockSpec((tk, CT), lambda t, k: (k, t))]
        + ([] if prev is None else [ANY]),
        out_specs=pl.BlockSpec((None, M, CT), lambda t, k: wt(t)),
        out_shape=SDS((N_DEV, M, W_SHARD), F32),
        scratch_shapes=[pltpu.VMEM((M, CT), F32)],
        input_output_aliases={} if prev is None else {2: 0},
        compiler_params=_cp(("parallel", "arbitrary"), 48),
    )(at, b, *(() if prev is None else (prev,)))


def mm_dx_part(da, w8, tile0, name):
    S = da.shape[0]
    ntiles = da.shape[1] // CT
    tm = 2048
    wt = _w_tile(tile0)

    def body(a_ref, b_ref, o_ref, acc_ref):
        t = pl.program_id(1)

        @pl.when(t == 0)
        def _():
            acc_ref[...] = jnp.zeros_like(acc_ref)

        acc_ref[...] += _dot_nt(a_ref[...], b_ref[...])

        @pl.when(t == ntiles - 1)
        def _():
            o_ref[...] = acc_ref[...]

    return pl.pallas_call(
        body, name=name, grid=(S // tm, ntiles),
        in_specs=[pl.BlockSpec((tm, CT), lambda i, t: (i, t)), pl.BlockSpec((None, D, CT), lambda i, t: wt(t))],
        out_specs=pl.BlockSpec((tm, D), lambda i, t: (i, 0)),
        out_shape=SDS((S, D), F32),
        scratch_shapes=[pltpu.VMEM((tm, D), F32)],
        compiler_params=_cp(("parallel", "arbitrary"), 56),
    )(da, w8)


def _band_masks(not_first):
    row = lax.broadcasted_iota(jnp.int32, (QB, QB), 0)
    col = lax.broadcasted_iota(jnp.int32, (QB, QB), 1)
    cur = jnp.where(col <= row, 0.0, NEG)
    prev = jnp.where(col >= row, 0.0, NEG)
    first = jnp.where(jnp.logical_and(col >= row, not_first), 0.0, NEG)
    return col, jnp.concatenate([prev, cur], axis=1), jnp.concatenate([first, cur], axis=1)


def _fill_kv(ext, qkv_ref, kh_ref, vh_ref):
    ext[0:QB, 0:E] = kh_ref[...]
    ext[0:QB, E:2 * E] = vh_ref[...]
    ext[QB:, :] = qkv_ref[:, E:3 * E]


def attn_fwd(P, g, d):
    S = P.shape[0]
    L = S // d
    T = min(512, L)
    nq = T // QB
    ni = L // T

    def body(qkv_ref, kh_ref, vh_ref, o_ref, lse_ref, ext):
        col, mask, mask_first = _band_masks(pl.program_id(1) > 0)
        lse_ref[...] = jnp.zeros_like(lse_ref)
        _fill_kv(ext, qkv_ref, kh_ref, vh_ref)

        def head(h, carry):
            off = pl.multiple_of(h * HD, HD)
            cq, cv = pl.ds(off, HD), pl.ds(E + off, HD)
            for j in range(nq):
                rows = slice(j * QB, (j + 1) * QB)
                krows = slice(j * QB, (j + 2) * QB)
                s = _dot_nt(qkv_ref[rows, cq], ext[krows, cq]) * SCALE + (mask_first if j == 0 else mask)
                m = jnp.max(s, axis=1, keepdims=True)
                p = jnp.exp(s - m)
                den = jnp.sum(p, axis=1, keepdims=True)
                o = _dot(p.astype(BF16), ext[krows, cv]) / den
                o_ref[rows, cq] = o.astype(BF16)
                lse_ref[rows, :] = jnp.where(col == h, m + jnp.log(den), lse_ref[rows, :])
            return carry

        lax.fori_loop(0, NH, head, 0, unroll=2)

    halo = lambda r, i: jnp.maximum(r * (L // QB) + i * nq - 1, 0)
    return pl.pallas_call(
        body, name=f"attn_fwd{g}", grid=(d, ni),
        in_specs=[pl.BlockSpec((T, SEG), lambda r, i: (r * ni + i, 0)),
                  pl.BlockSpec((QB, E), lambda r, i: (halo(r, i), 1)),
                  pl.BlockSpec((QB, E), lambda r, i: (halo(r, i), 2))],
        out_specs=[pl.BlockSpec((T, E), lambda r, i: (r * ni + i, 0)),
                   pl.BlockSpec((T, HD), lambda r, i: (r * ni + i, 0))],
        out_shape=[SDS((S, E), BF16), SDS((S, HD), F32)],
        scratch_shapes=[pltpu.VMEM((T + QB, 2 * E), BF16)],
        compiler_params=_cp(("parallel", "parallel"), 48),
    )(P, P, P)


def _perm_specs(ts, C):
    return [pl.BlockSpec((ts, C), lambda i: (i, 0)),
            pl.BlockSpec((4, ts // 4, C), lambda i: (0, i, 0)),
            pl.BlockSpec((16, ts // 16, C), lambda i: (0, i, 0))]


def _perm_shapes(S, C, dtype):
    return [SDS((S, C), dtype), SDS((4, S // 4, C), dtype), SDS((16, S // 16, C), dtype)]


def combine_fwd(os_, lses, z, ehot):
    S = z.shape[0]
    ts = 256

    def body(o0, o1, o2, l0, l1, l2, z_ref, e_ref, y0, y1, y2, s0, s1, s2, ya_ref, yat_ref,
             so1, so2, sl1, sl2, sy, sl):
        _load_perm(so1, o1, 4)
        _load_perm(so2, o2, 16)
        _load_perm(sl1, l1, 4)
        _load_perm(sl2, l2, 16)
        ls = [l0[...], sl1[0], sl2[0]]
        m = jnp.maximum(jnp.maximum(ls[0], ls[1]), ls[2])
        es = [jnp.exp(l - m) for l in ls]
        den = es[0] + es[1] + es[2]
        sl[0] = m + jnp.log(den)
        y = None
        for e, o in zip(es, (o0[...].astype(F32), _scr_get(so1), _scr_get(so2))):
            w = e / den
            hi = w.astype(BF16)
            lo = (w - hi.astype(F32)).astype(BF16)
            wb = _dot(hi, e_ref[...]) + _dot(lo, e_ref[...])
            y = wb * o if y is None else y + wb * o
        z = z_ref[...].astype(F32)
        ya = y * (z * _sigmoid(z))
        ya_ref[...] = ya.astype(BF16)
        yat_ref[...] = ya.T.astype(BF16)
        _scr_put(sy, y)
        y0[...] = y.astype(BF16)
        _store_perm(y1, sy, 4)
        _store_perm(y2, sy, 16)
        s0[...] = sl[0]
        _store_perm(s1, sl, 4)
        _store_perm(s2, sl, 16)

    wide = pl.BlockSpec((ts, E), lambda i: (i, 0))
    os3 = [os_[0], os_[1].reshape(4, S // 4, E), os_[2].reshape(16, S // 16, E)]
    ls3 = [lses[0], lses[1].reshape(4, S // 4, HD), lses[2].reshape(16, S // 16, HD)]
    res = pl.pallas_call(
        body, name="combine_fwd", grid=(S // ts,),
        in_specs=_perm_specs(ts, E) + _perm_specs(ts, HD) + [wide, pl.BlockSpec((HD, E), lambda i: (0, 0))],
        out_specs=_perm_specs(ts, E) + _perm_specs(ts, HD) + [wide, pl.BlockSpec((E, ts), lambda i: (0, i))],
        out_shape=_perm_shapes(S, E, BF16) + _perm_shapes(S, HD, F32) + [SDS((S, E), BF16), SDS((E, S), BF16)],
        scratch_shapes=[_scr(ts, E), _scr(ts, E), _scr(ts, HD), _scr(ts, HD), _scr(ts, E), _scr(ts, HD)],
        compiler_params=_cp(("parallel",), 56),
    )(*os3, *ls3, z, ehot)
    ys = [res[0], res[1].reshape(S, E), res[2].reshape(S, E)]
    lse3 = [res[3], res[4].reshape(S, HD), res[5].reshape(S, HD)]
    return ys, lse3, res[6], res[7]


def mm_dya(da0, w_out, z, y):
    S = da0.shape[0]
    tm = 512

    def body(a_ref, w_ref, z_ref, y_ref, dy0, dy1, dy2, dz_ref, scr):
        def chunk(cidx, carry):
            col = pl.ds(pl.multiple_of(cidx * 256, 256), 256)
            dya = _dot_nt(a_ref[...], w_ref[col, :])
            zz = z_ref[:, col].astype(F32)
            sig = _sigmoid(zz)
            dy = dya * zz * sig
            scr[2 * cidx] = dy[:, :LANES]
            scr[2 * cidx + 1] = dy[:, LANES:]
            dy0[:, col] = dy.astype(BF16)
            dz_ref[:, col] = (dya * y_ref[:, col].astype(F32) * sig * (1.0 + zz * (1.0 - sig))).astype(BF16)
            return carry

        lax.fori_loop(0, E // 256, chunk, 0)
        _store_perm(dy1, scr, 4)
        _store_perm(dy2, scr, 16)

    wide = pl.BlockSpec((tm, E), lambda i: (i, 0))
    res = pl.pallas_call(
        body, name="mm_dya", grid=(S // tm,),
        in_specs=[pl.BlockSpec((tm, D), lambda i: (i, 0)), pl.BlockSpec((E, D), lambda i: (0, 0)), wide, wide],
        out_specs=_perm_specs(tm, E) + [wide],
        out_shape=_perm_shapes(S, E, BF16) + [SDS((S, E), BF16)],
        scratch_shapes=[_scr(tm, E)],
        compiler_params=_cp(("parallel",), 48),
    )(da0, w_out, z, y)
    return [res[0], res[1].reshape(S, E), res[2].reshape(S, E)], res[3]


def attn_bwd(P, dy, y, lse, tabs, g, d):
    S = P.shape[0]
    L = S // d
    T = min(512, L)
    nq = T // QB
    ni = L // T

    def body(qkv_ref, kh_ref, vh_ref, dy_ref, y_ref, lse_ref, c_ref, s1_ref, s2_ref,
             o_ref, dkc_ref, dvc_ref, ext):
        i = pl.program_id(1)
        col, mask, mask_first = _band_masks(i < ni - 1)
        _fill_kv(ext, qkv_ref, kh_ref, vh_ref)

        @pl.when(i == 0)
        def _():
            dkc_ref[...] = jnp.zeros_like(dkc_ref)
            dvc_ref[...] = jnp.zeros_like(dvc_ref)

        def head(h, carry):
            off = pl.multiple_of(h * HD, HD)
            cq, ck, cv = pl.ds(off, HD), pl.ds(E + off, HD), pl.ds(2 * E + off, HD)
            pend_dk = dkc_ref[:, cq]
            pend_dv = dvc_ref[:, cq]
            for j in reversed(range(nq)):
                rows = slice(j * QB, (j + 1) * QB)
                krows = slice(j * QB, (j + 2) * QB)
                q = qkv_ref[rows, cq]
                k2 = ext[krows, cq]
                dyj = dy_ref[rows, cq]
                lse_h = jnp.sum(jnp.where(col == h, lse_ref[rows, :], 0.0), axis=1, keepdims=True)
                delta = jnp.sum(dyj.astype(F32) * y_ref[rows, cq].astype(F32), axis=1, keepdims=True)
                p = jnp.exp(_dot_nt(q, k2) * SCALE + (mask_first if j == 0 else mask) - lse_h)
                ds = p * (_dot_nt(dyj, ext[krows, ck]) - delta) * SCALE
                dq = _dot(ds.astype(BF16), k2)
                dk2 = _dot(ds.T.astype(BF16), q)
                dv2 = _dot(p.T.astype(BF16), dyj)
                dk = dk2[QB:] + pend_dk
                dv = dv2[QB:] + pend_dv
                pend_dk, pend_dv = dk2[:QB], dv2[:QB]
                c, s1, s2 = c_ref[rows, :], s1_ref[rows, :], s2_ref[rows, :]
                o_ref[rows, cq] = _unrope(dq, c, s1, s2).astype(BF16)
                o_ref[rows, ck] = _unrope(dk, c, s1, s2).astype(BF16)
                o_ref[rows, cv] = dv.astype(BF16)
            dkc_ref[:, cq] = pend_dk
            dvc_ref[:, cq] = pend_dv
            return carry

        lax.fori_loop(0, NH, head, 0, unroll=2)

    blk = lambda r, i: r * ni + ni - 1 - i
    halo = lambda r, i: jnp.maximum(r * (L // QB) + (ni - 1 - i) * nq - 1, 0)
    main = pl.BlockSpec((T, SEG), lambda r, i: (blk(r, i), 0))
    wide = pl.BlockSpec((T, E), lambda r, i: (blk(r, i), 0))
    narrow = pl.BlockSpec((T, HD), lambda r, i: (blk(r, i), 0))
    return pl.pallas_call(
        body, name=f"attn_bwd{g}", grid=(d, ni),
        in_specs=[main, pl.BlockSpec((QB, E), lambda r, i: (halo(r, i), 1)),
                  pl.BlockSpec((QB, E), lambda r, i: (halo(r, i), 2)),
                  wide, wide, narrow, narrow, narrow, narrow],
        out_specs=main, out_shape=SDS((S, SEG), BF16),
        scratch_shapes=[pltpu.VMEM((QB, E), F32), pltpu.VMEM((QB, E), F32), pltpu.VMEM((T + QB, 2 * E), BF16)],
        compiler_params=_cp(("arbitrary", "arbitrary"), 56),
    )(P, P, P, dy, y, lse, *tabs)


def _pool_cnt(t0, rows):
    t = (lax.broadcasted_iota(jnp.int32, (rows, E), 0) + t0 + 1).astype(F32)
    ch = lax.broadcasted_iota(jnp.int32, (rows, E), 1)
    w = jnp.where(ch < PC, 2.0, jnp.where(ch < 2 * PC, 4.0, jnp.where(ch < 3 * PC, 8.0, 16.0)))
    return jnp.minimum(t, w)


def _by_group(parts):
    return jnp.concatenate([parts[g][:, g * PC:(g + 1) * PC] for g in range(4)], axis=1)


def pool_fwd(uz):
    S = uz.shape[0]
    ts = 256

    def body(u_ref, h_ref, o_ref, ot_ref):
        i = pl.program_id(0)
        u = u_ref[...]
        halo = jnp.where(i > 0, h_ref[...], 0.0)
        ext = jnp.concatenate([halo, u], axis=0)
        s2 = ext + pltpu.roll(ext, 1, 0)
        s4 = s2 + pltpu.roll(s2, 2, 0)
        s8 = s4 + pltpu.roll(s4, 4, 0)
        s16 = s8 + pltpu.roll(s8, 8, 0)
        win = _by_group([s2, s4, s8, s16])[16:, :]
        pooled = win / _pool_cnt(i * ts, ts) - u
        o_ref[...] = pooled.astype(BF16)
        ot_ref[...] = pooled.T.astype(BF16)

    return pl.pallas_call(
        body, name="pool_fwd", grid=(S // ts,),
        in_specs=[pl.BlockSpec((ts, E), lambda i: (i, 0)),
                  pl.BlockSpec((16, E), lambda i: (jnp.maximum(i * (ts // 16) - 1, 0), 0))],
        out_specs=[pl.BlockSpec((ts, E), lambda i: (i, 0)), pl.BlockSpec((E, ts), lambda i: (0, i))],
        out_shape=[SDS((S, E), BF16), SDS((E, S), BF16)],
        compiler_params=_cp(("parallel",), 48),
    )(uz, uz)


def pool_bwd(dpooled, duz):
    S = dpooled.shape[0]
    ts = 256
    nt = S // ts

    def body(d_ref, h_ref, alias_ref, o_ref):
        i = pl.program_id(0)
        dp = d_ref[...].astype(F32)
        halo = jnp.where(i < nt - 1, h_ref[...].astype(F32), 0.0)
        n = ts + 16
        ext = jnp.concatenate([dp, halo], axis=0) / _pool_cnt(i * ts, n)
        f2 = ext + pltpu.roll(ext, n - 1, 0)
        f4 = f2 + pltpu.roll(f2, n - 2, 0)
        f8 = f4 + pltpu.roll(f4, n - 4, 0)
        f16 = f8 + pltpu.roll(f8, n - 8, 0)
        win = _by_group([f2, f4, f8, f16])[:ts, :]
        o_ref[...] = (win - dp).astype(BF16)

    return pl.pallas_call(
        body, name="pool_bwd", grid=(nt,),
        in_specs=[pl.BlockSpec((ts, E), lambda i: (i, 0)),
                  pl.BlockSpec((16, E), lambda i: (jnp.minimum((i + 1) * (ts // 16), S // 16 - 1), 0)), ANY],
        out_specs=pl.BlockSpec((ts, E), lambda i: (i, 0)),
        out_shape=SDS(duz.shape, BF16),
        input_output_aliases={2: 0},
        compiler_params=_cp(("parallel",), 48),
    )(dpooled, dpooled, duz)


def mm_grp(pooled, wg, b, scale, uz):
    S = pooled.shape[0]
    tm = 512

    def body(p_ref, w_ref, b_ref, s_ref, z_ref, h_ref, y_ref, yt_ref):
        for g in range(4):
            cs = slice(g * PC, (g + 1) * PC)
            h = _dot(p_ref[:, cs], w_ref[g]) + b_ref[:, cs]
            z = z_ref[:, cs]
            yp = h * s_ref[:, cs] * (z * _sigmoid(z))
            h_ref[:, cs] = h.astype(BF16)
            y_ref[:, cs] = yp.astype(BF16)
            yt_ref[cs, :] = yp.T.astype(BF16)

    row = pl.BlockSpec((tm, E), lambda i: (i, 0))
    vec = pl.BlockSpec((1, E), lambda i: (0, 0))
    return pl.pallas_call(
        body, name="mm_grp", grid=(S // tm,),
        in_specs=[row, pl.BlockSpec((4, PC, PC), lambda i: (0, 0, 0)), vec, vec,
                  pl.BlockSpec((tm, E), lambda i: (i, 1))],
        out_specs=[row, row, pl.BlockSpec((E, tm), lambda i: (0, i))],
        out_shape=[SDS((S, E), BF16), SDS((S, E), BF16), SDS((E, S), BF16)],
        compiler_params=_cp(("parallel",), 48),
    )(pooled, wg, b, scale, uz)


def mm_dyp(da1, w_out, uz, h, scale):
    S = da1.shape[0]
    tm = 512

    def body(a_ref, w_ref, z_ref, h_ref, s_ref, dh_ref, dz_ref, dsc_ref, db_ref):
        @pl.when(pl.program_id(0) == 0)
        def _():
            dsc_ref[...] = jnp.zeros_like(dsc_ref)
            db_ref[...] = jnp.zeros_like(db_ref)

        def chunk(cidx, carry):
            col = pl.ds(pl.multiple_of(cidx * 256, 256), 256)
            dyp = _dot_nt(a_ref[...], w_ref[col, :])
            z = z_ref[:, col]
            hh = h_ref[:, col].astype(F32)
            sc = s_ref[:, col]
            sig = _sigmoid(z)
            dhs = dyp * z * sig
            dz_ref[:, col] = (dyp * hh * sc * sig * (1.0 + z * (1.0 - sig))).astype(BF16)
            dh = dhs * sc
            dh_ref[:, col] = dh.astype(BF16)
            dsc_ref[:, col] += _fold8(dhs * hh)
            db_ref[:, col] += _fold8(dh)
            return carry

        lax.fori_loop(0, E // 256, chunk, 0)

    row = pl.BlockSpec((tm, E), lambda i: (i, 0))
    acc = pl.BlockSpec((8, E), lambda i: (0, 0))
    return pl.pallas_call(
        body, name="mm_dyp", grid=(S // tm,),
        in_specs=[pl.BlockSpec((tm, D), lambda i: (i, 0)), pl.BlockSpec((E, D), lambda i: (0, 0)),
                  pl.BlockSpec((tm, E), lambda i: (i, 1)), row, pl.BlockSpec((1, E), lambda i: (0, 0))],
        out_specs=[row, pl.BlockSpec((tm, E), lambda i: (i, 1)), acc, acc],
        out_shape=[SDS((S, E), BF16), SDS((S, 2 * E), BF16), SDS((8, E), F32), SDS((8, E), F32)],
        compiler_params=_cp(("arbitrary",), 48),
    )(da1, w_out, uz, h, scale)


def mm_dpooled(dh, wg):
    S = dh.shape[0]
    tm = 1024

    def body(a_ref, w_ref, o_ref):
        for g in range(4):
            cs = slice(g * PC, (g + 1) * PC)
            o_ref[:, cs] = _dot_nt(a_ref[:, cs], w_ref[g]).astype(BF16)

    row = pl.BlockSpec((tm, E), lambda i: (i, 0))
    return pl.pallas_call(
        body, name="mm_dpooled", grid=(S // tm,),
        in_specs=[row, pl.BlockSpec((4, PC, PC), lambda i: (0, 0, 0))],
        out_specs=row, out_shape=SDS((S, E), BF16),
        compiler_params=_cp(("parallel",), 48),
    )(dh, wg)


def _rope_tables(positions):
    inv_freq = 500000.0 ** (-jnp.arange(0, 32, 2, dtype=F32) / 32)
    ang = positions.astype(F32)[:, None] * inv_freq
    cos, sin = jnp.cos(ang), jnp.sin(ang)
    S = positions.shape[0]
    one = jnp.ones((S, HD - 32), F32)
    zero16 = jnp.zeros((S, 16), F32)
    zero = jnp.zeros((S, HD - 32), F32)
    c = jnp.concatenate([cos, cos, one], axis=1)
    s1 = jnp.concatenate([-sin, zero16, zero], axis=1)
    s2 = jnp.concatenate([zero16, sin, zero], axis=1)
    return c.astype(BF16), s1.astype(BF16), s2.astype(BF16)


def kernel(x, positions, norm_pre, norm_post, attn_w_in, attn_w_out, pool_w_in, pool_w_grp, pool_b_grp, pool_scale, pool_w_out, loss_target, m_norm_pre, m_norm_post, m_attn_w_in, m_attn_w_out, m_pool_w_in, m_pool_w_grp, m_pool_b_grp, m_pool_scale, m_pool_w_out, v_norm_pre, v_norm_post, v_attn_w_in, v_attn_w_out, v_pool_w_in, v_pool_w_grp, v_pool_b_grp, v_pool_scale, v_pool_w_out):
    S = x.shape[1]
    xi, yi, ci = _mesh_pos()
    dev = 4 * xi + 2 * yi + ci
    x2 = x[0]
    tgt = loss_target[0]

    small = jnp.concatenate([pool_b_grp[0].reshape(2, HD), pool_scale[0].reshape(2, HD),
                             jnp.zeros((4, HD), F32)], axis=0)
    w_in8, w_out8, wp_in8, wg8, wp_out8, small8 = all_gather(
        [attn_w_in[0].astype(BF16), attn_w_out[0].astype(BF16), pool_w_in[0].astype(BF16),
         pool_w_grp[0].astype(BF16), pool_w_out[0].astype(BF16), small], "gather_weights")
    w_out = w_out8.reshape(E, D)
    wp_out = wp_out8.reshape(E, D)
    wg = wg8.transpose(1, 0, 2, 3).reshape(4, PC, PC)
    b_full = small8[:, 0:2, :].reshape(N_DEV, 4, PC // N_DEV).transpose(1, 0, 2).reshape(1, E)
    scale_full = small8[:, 2:4, :].reshape(1, E)

    pos = positions[0]
    tabs = [_rope_tables(pos.reshape(S // d, d).T.reshape(S)) for d in DIL]
    ehot = (jnp.arange(E)[None, :] // HD == jnp.arange(HD)[:, None]).astype(BF16)
    seg_tiles = SEG // CT

    xn0, xn0_4, xn0_16, xn0t = norm_pre0(x2, norm_pre[0:1])
    xn0s = [xn0, xn0_4.reshape(S, D), xn0_16.reshape(S, D)]
    xn0ts = [xn0t, transpose_rows(xn0s[1], "xn0t_4"), transpose_rows(xn0s[2], "xn0t_16")]
    Ps, os_, lses = [], [], []
    for g, d in enumerate(DIL):
        P = mm_in(xn0s[g], w_in8, g * seg_tiles, seg_tiles, tabs[g], f"mm_qkv{g}")
        o, l = attn_fwd(P, g, d)
        Ps.append(P)
        os_.append(o)
        lses.append(l)
    z0 = mm_in(xn0, w_in8, 3 * seg_tiles, E // CT, None, "mm_z0")
    ys, lse3, ya, yat = combine_fwd(os_, lses, z0, ehot)
    a0 = mm_rows(ya, w_out, "mm_out0", F32)
    h1, xn1, xn1t = post0_pre1(x2, a0, norm_post[0:1], norm_pre[1:2])

    uz = mm_uz(xn1, wp_in8)
    pooled, pooled_t = pool_fwd(uz)
    hgrp, yp, ypt = mm_grp(pooled, wg, b_full, scale_full, uz)
    a1 = mm_rows(yp, wp_out, "mm_out1", F32)
    dh2, da1, loss_rows, dg_post1 = post1_loss(h1, a1, tgt, norm_post[1:2])
    loss = lax.psum(0.5 / D * jnp.sum(loss_rows), ("x", "y", "c"))

    dh, duz, dscale_p, db_p = mm_dyp(da1, wp_out, uz, hgrp, scale_full)
    dpooled = mm_dpooled(dh, wg)
    duz = pool_bwd(dpooled, duz)
    g_wg = mm_dwg(pooled_t, dh)
    g_wp_out = mm_wgrad_rows(ypt, da1, "mm_dwp_out")
    g_wp_in = mm_wgrad_cols(xn1t, duz, "mm_dwp_in", ncols=2 * E, shard=PC, tn=1024)
    dxn1 = mm_dx_stack(duz, wp_in8, "mm_dxn1", kchunk=PC, tm=1024)
    dh1, da0, dg_pre1, dg_post0 = mid_bwd(dxn1, dh2, h1, a0, norm_pre[1:2], norm_post[0:1])

    dys, dz0 = mm_dya(da0, w_out, z0, ys[0])
    g_w_out = mm_wgrad_rows(yat, da0, "mm_dw_out")
    g_w_in = mm_dw_in_part(xn0t, dz0, 3 * seg_tiles, None, "mm_dw_in_z")
    dxs = [mm_dx_part(dz0, w_in8, 3 * seg_tiles, "mm_dxn0_z")]
    for g, d in enumerate(DIL):
        dP = attn_bwd(Ps[g], dys[g], ys[g], lse3[g], tabs[g], g, d)
        g_w_in = mm_dw_in_part(xn0ts[g], dP, g * seg_tiles, g_w_in, f"mm_dw_in{g}")
        dxs.append(mm_dx_part(dP, w_in8, g * seg_tiles, f"mm_dxn0_{g}"))
    grad_x, dg_pre0 = pre0_bwd(dxs[1], dxs[0], dxs[2], dxs[3], dh1, x2, norm_pre[0:1])

    cidx = ci.astype(jnp.int32).reshape(1)
    fulls = [g_w_in, g_w_out.reshape(N_DEV, E // N_DEV, D), g_wp_in,
             g_wg.reshape(4, N_DEV, PC // N_DEV, PC).transpose(1, 0, 2, 3).reshape(N_DEV, 4 * PC // N_DEV, PC),
             g_wp_out.reshape(N_DEV, E // N_DEV, D)]
    sibs = rs_pair(fulls, "rs_pair")
    parts = [pair_add(f, s, cidx, f"pair_add{k}") for k, (f, s) in enumerate(zip(fulls, sibs))]
    recvs = rs_chips(parts, "rs_chips")
    shards = [(attn_w_in, m_attn_w_in, v_attn_w_in), (attn_w_out, m_attn_w_out, v_attn_w_out),
              (pool_w_in, m_pool_w_in, v_pool_w_in), (pool_w_grp, m_pool_w_grp, v_pool_w_grp),
              (pool_w_out, m_pool_w_out, v_pool_w_out)]
    big = []
    for k, (recv, (w, m, v)) in enumerate(zip(recvs, shards)):
        shp = w.shape
        r2 = recv.shape[1:]
        res = adamw_sum(recv, w.reshape(r2), m.reshape(r2), v.reshape(r2), f"adamw{k}")
        big.append([t.reshape(shp) for t in res])

    smalls = jnp.concatenate([dg_pre0.sum(0, keepdims=True), dg_pre1.sum(0, keepdims=True),
                              dg_post0.sum(0, keepdims=True), dg_post1.sum(0, keepdims=True),
                              db_p.sum(0).reshape(2, D), dscale_p.sum(0).reshape(2, D)], axis=0)
    (smalls8,) = all_gather([smalls], "gather_small_grads")
    tot = sum_slots(smalls8, "sum_small_grads")
    g_norm_pre, g_norm_post = tot[0:2], tot[2:4]
    g_b = lax.dynamic_slice_in_dim(tot[4:6].reshape(4, PC), dev * (PC // N_DEV), PC // N_DEV, axis=1)[None]
    g_scale = lax.dynamic_slice_in_dim(tot[6:8].reshape(1, E), dev * (E // N_DEV), E // N_DEV, axis=1)
    sm = [adamw_small(g_norm_pre, norm_pre, m_norm_pre, v_norm_pre, "adamw_norm_pre"),
          adamw_small(g_norm_post, norm_post, m_norm_post, v_norm_post, "adamw_norm_post"),
          adamw_small(g_b, pool_b_grp, m_pool_b_grp, v_pool_b_grp, "adamw_b"),
          adamw_small(g_scale, pool_scale, m_pool_scale, v_pool_scale, "adamw_scale")]

    grads = [g_norm_pre, g_norm_post, big[0][0], big[1][0], big[2][0], big[3][0], g_b, g_scale, big[4][0]]

    def pick(k):
        return [sm[0][k - 1], sm[1][k - 1], big[0][k], big[1][k], big[2][k], big[3][k], sm[2][k - 1], sm[3][k - 1],
                big[4][k]]

    return (loss, grad_x[None], *grads, *pick(1), *pick(2), *pick(3))
```

```python
import functools
import math

import numpy as np
import jax
import jax.numpy as jnp
from jax import lax
from jax.experimental import pallas as pl
from jax.experimental.pallas import tpu as pltpu

F32 = jnp.float32
BF16 = jnp.bfloat16
SDS = jax.ShapeDtypeStruct

N_DEV = 8
D = 1024
E = 2048
HD = 128
NH = E // HD
DIL = (1, 4, 16)
QB = 128
SEG = 3 * E
W_IN_COLS = 3 * SEG + E
W_SHARD = W_IN_COLS // N_DEV
CT = 512
POOL_WIN = (2, 4, 8, 16)
PC = E // 4
EPS = 1e-6
NEG = -1e30
SCALE = 1.0 / math.sqrt(HD)
LR, B1, B2, ADAM_EPS, WD, STEP = 0.001, 0.9, 0.999, 1e-08, 0.01, 10
MIB = 1024 * 1024
ANY = pl.BlockSpec(memory_space=pl.ANY)
MESH = pl.DeviceIdType.MESH


def _cp(sem, mb):
    return pltpu.CompilerParams(dimension_semantics=sem, vmem_limit_bytes=mb * MIB)


def _dot(a, b):
    return jnp.dot(a, b, preferred_element_type=F32)


def _dot_nt(a, b):
    return lax.dot_general(a, b, (((1,), (1,)), ((), ())), preferred_element_type=F32)


def _rms(h):
    return lax.rsqrt(jnp.mean(h * h, axis=-1, keepdims=True) + EPS)


def _row_tile(R, C, budget):
    tr = R
    while tr * C * 4 > budget and tr % 16 == 0:
        tr //= 2
    return tr


def _fold8(t):
    return t.reshape(t.shape[0] // 8, 8, t.shape[1]).sum(axis=0)


def _sigmoid(z):
    return 1.0 / (1.0 + jnp.exp(-z))


LANES = 128


def _scr(rows, C):
    return pltpu.VMEM((C // LANES, rows, LANES), F32)


def _scr_put(scr, val):
    for c in range(scr.shape[0]):
        scr[c] = val[:, c * LANES:(c + 1) * LANES]


def _scr_get(scr):
    return jnp.concatenate([scr[c] for c in range(scr.shape[0])], axis=1)


def _store_perm(dst_ref, scr, d):
    n = dst_ref.shape[1]
    for r in range(d):
        for c in range(scr.shape[0]):
            dst_ref[r, :, c * LANES:(c + 1) * LANES] = scr[c, pl.ds(r, n, stride=d), :].astype(dst_ref.dtype)


def _load_perm(scr, src_ref, d, add=False):
    n = src_ref.shape[1]
    for r in range(d):
        rows = pl.ds(r, n, stride=d)
        for c in range(scr.shape[0]):
            v = src_ref[r, :, c * LANES:(c + 1) * LANES].astype(F32)
            scr[c, rows, :] = scr[c, rows, :] + v if add else v


def _rope(t, c, s1, s2):
    t = t.astype(BF16)
    return t * c + pltpu.roll(t, HD - 16, 1) * s1 + pltpu.roll(t, 16, 1) * s2


def _unrope(t, c, s1, s2):
    t = t.astype(BF16)
    return t * c - pltpu.roll(t, HD - 16, 1) * s1 - pltpu.roll(t, 16, 1) * s2


def _mesh_pos():
    return lax.axis_index("x"), lax.axis_index("y"), lax.axis_index("c")


def all_gather(arrs, name):
    n = len(arrs)

    def body(*refs):
        ins, outs = refs[:n], refs[n:2 * n]
        send_sems, recv_sems, local_sems = refs[2 * n:]
        x, y, c = _mesh_pos()
        me, sib = (x, y, c), (x, y, 1 - c)
        chips = [(1 - x, y), (x, 1 - y), (1 - x, 1 - y)]

        def slot(p):
            return 4 * p[0] + 2 * p[1] + p[2]

        def copy(a, k, block, to, src=None):
            dst = outs[a].at[slot(block)]
            return pltpu.make_async_remote_copy(
                src_ref=dst if src is None else src, dst_ref=dst,
                send_sem=send_sems.at[a, k], recv_sem=recv_sems.at[a, k],
                device_id=to, device_id_type=MESH)

        mine = [pltpu.make_async_copy(ins[a], outs[a].at[slot(me)], local_sems.at[a]) for a in range(n)]
        for cp in mine:
            cp.start()
        first = []
        for a in range(n):
            first.append(copy(a, 0, me, sib, src=ins[a]))
            for j, chip in enumerate(chips):
                first.append(copy(a, 1 + j, me, (*chip, c), src=ins[a]))
        for cp in first:
            cp.start()
        passed = []
        for j, chip in enumerate(chips):
            for a in range(n):
                copy(a, 1 + j, (*chip, c), me).wait_recv()
                fw = copy(a, 4 + j, (*chip, c), sib)
                fw.start()
                passed.append(fw)
        for a in range(n):
            copy(a, 0, sib, me).wait_recv()
        for j, chip in enumerate(chips):
            for a in range(n):
                copy(a, 4 + j, (*chip, 1 - c), me).wait_recv()
        for cp in first + passed:
            cp.wait_send()
        for cp in mine:
            cp.wait()

    return pl.pallas_call(
        body, name=name,
        out_shape=[SDS((N_DEV,) + a.shape, a.dtype) for a in arrs],
        in_specs=[ANY] * n, out_specs=[ANY] * n,
        scratch_shapes=[pltpu.SemaphoreType.DMA((n, 7)), pltpu.SemaphoreType.DMA((n, 7)),
                        pltpu.SemaphoreType.DMA((n,))],
    )(*arrs)


HBM_SPEC = pl.BlockSpec(memory_space=pltpu.HBM)
SEM_SPEC = pl.BlockSpec(memory_space=pltpu.SEMAPHORE)
EFFECT = pltpu.SideEffectType.DATAFLOW_SIDE_EFFECTING


def _pair_plan():
    def plan(x, y, c):
        return [(2 * q + (1 - c), q, (x, y, 1 - c)) for q in range(4)]
    return plan


def _chips_plan():
    def plan(x, y, c):
        chips = [(1 - x, y), (x, 1 - y), (1 - x, 1 - y)]
        return [(2 * cx + cy, 2 * x + y, (cx, cy, c)) for cx, cy in chips]
    return plan


def _peers_plan():
    def plan(x, y, c):
        out = []
        for k in range(1, N_DEV):
            fx, fy, fc = (k >> 2) & 1, (k >> 1) & 1, k & 1
            px, py, pc = (x + fx) % 2, (y + fy) % 2, (c + fc) % 2
            out.append((None, 4 * x + 2 * y + c, (px, py, pc)))
        return out
    return plan


def _split_copies(plan, srcs, lands, send_sems, recv_sems):
    x, y, c = _mesh_pos()
    cps = []
    for a, (src, land) in enumerate(zip(srcs, lands)):
        steps = plan(x, y, c)
        for k, (si, li, to) in enumerate(steps):
            sem = a * len(steps) + k
            cps.append(pltpu.make_async_remote_copy(
                src_ref=src if si is None else src.at[si], dst_ref=land.at[li],
                send_sem=send_sems.at[sem], recv_sem=recv_sems.at[sem],
                device_id=to, device_id_type=MESH))
    return cps


def split_start(name, srcs, lands, plan, nk):
    n = len(srcs)

    def body(*refs):
        send_sems, recv_sems = refs[2 * n], refs[2 * n + 1]
        token = refs[-1]
        for cp in _split_copies(plan, refs[:n], refs[n:2 * n], send_sems, recv_sems):
            cp.start()
        token[...] = jnp.zeros_like(token)

    ops = [pltpu.with_memory_space_constraint(a, pltpu.HBM) for a in list(srcs) + list(lands)]
    res = pl.pallas_call(
        body, name=name,
        out_shape=(pltpu.SemaphoreType.DMA((n * nk,)), pltpu.SemaphoreType.DMA((n * nk,)),
                   *[pltpu.HBM(a.shape, a.dtype) for a in ops], SDS((8, 128), F32)),
        in_specs=[HBM_SPEC] * (2 * n),
        out_specs=(SEM_SPEC, SEM_SPEC, *[HBM_SPEC] * (2 * n), pl.BlockSpec(memory_space=pltpu.VMEM)),
        input_output_aliases={i: 2 + i for i in range(2 * n)},
        compiler_params=pltpu.CompilerParams(has_side_effects=EFFECT),
    )(*ops)
    return res[:-1], res[-1]


def split_wait(name, flight, plan, after):
    send_sems, recv_sems = flight[0], flight[1]
    bufs = list(flight[2:])
    n = len(bufs) // 2

    def body(*refs):
        for cp in _split_copies(plan, refs[:n], refs[n:2 * n], refs[2 * n], refs[2 * n + 1]):
            cp.wait_send()
            cp.wait_recv()

    res = pl.pallas_call(
        body, name=name,
        out_shape=[pltpu.HBM(a.shape, a.dtype) for a in bufs],
        in_specs=[HBM_SPEC] * (2 * n) + [SEM_SPEC, SEM_SPEC, ANY],
        out_specs=[HBM_SPEC] * (2 * n),
        input_output_aliases={i: i for i in range(2 * n)},
        compiler_params=pltpu.CompilerParams(has_side_effects=EFFECT),
    )(*bufs, send_sems, recv_sems, after)
    return res[:n], res[n:]


def rs_pair(arrs, name):
    n = len(arrs)

    def body(*refs):
        ins, outs = refs[:n], refs[n:2 * n]
        send_sems, recv_sems = refs[2 * n:]
        x, y, c = _mesh_pos()
        cps = []
        for a in range(n):
            for q in range(4):
                cps.append(pltpu.make_async_remote_copy(
                    src_ref=ins[a].at[2 * q + (1 - c)], dst_ref=outs[a].at[q],
                    send_sem=send_sems.at[a, q], recv_sem=recv_sems.at[a, q],
                    device_id=(x, y, 1 - c), device_id_type=MESH))
        for cp in cps:
            cp.start()
        for cp in cps:
            cp.wait()

    return pl.pallas_call(
        body, name=name,
        out_shape=[SDS((4,) + a.shape[1:], a.dtype) for a in arrs],
        in_specs=[ANY] * n, out_specs=[ANY] * n,
        scratch_shapes=[pltpu.SemaphoreType.DMA((n, 4)), pltpu.SemaphoreType.DMA((n, 4))],
    )(*arrs)


def rs_chips(parts, name):
    n = len(parts)

    def body(*refs):
        ins, outs = refs[:n], refs[n:2 * n]
        send_sems, recv_sems, local_sems = refs[2 * n:]
        x, y, c = _mesh_pos()
        mychip = 2 * x + y
        chips = [(1 - x, y), (x, 1 - y), (1 - x, 1 - y)]
        mine = [pltpu.make_async_copy(ins[a].at[mychip], outs[a].at[mychip], local_sems.at[a]) for a in range(n)]
        for cp in mine:
            cp.start()
        cps = []
        for a in range(n):
            for j, chip in enumerate(chips):
                q = 2 * chip[0] + chip[1]
                cps.append(pltpu.make_async_remote_copy(
                    src_ref=ins[a].at[q], dst_ref=outs[a].at[mychip],
                    send_sem=send_sems.at[a, j], recv_sem=recv_sems.at[a, j],
                    device_id=(*chip, c), device_id_type=MESH))
        for cp in cps:
            cp.start()
        for cp in cps:
            cp.wait()
        for cp in mine:
            cp.wait()

    return pl.pallas_call(
        body, name=name,
        out_shape=[SDS(a.shape, a.dtype) for a in parts],
        in_specs=[ANY] * n, out_specs=[ANY] * n,
        scratch_shapes=[pltpu.SemaphoreType.DMA((n, 3)), pltpu.SemaphoreType.DMA((n, 3)),
                        pltpu.SemaphoreType.DMA((n,))],
    )(*parts)


def pair_add(full, sib, cidx, name):
    _, R, C = full.shape
    tr = _row_tile(R, C, MIB)

    def body(c_ref, a_ref, b_ref, o_ref):
        o_ref[...] = (a_ref[...] + b_ref[...]).astype(BF16)

    return pl.pallas_call(
        body, name=name,
        grid_spec=pltpu.PrefetchScalarGridSpec(
            num_scalar_prefetch=1, grid=(4, R // tr),
            in_specs=[pl.BlockSpec((None, tr, C), lambda q, i, cr: (2 * q + cr[0], i, 0)),
                      pl.BlockSpec((None, tr, C), lambda q, i, cr: (q, i, 0))],
            out_specs=pl.BlockSpec((None, tr, C), lambda q, i, cr: (q, i, 0))),
        out_shape=SDS((4, R, C), BF16),
        compiler_params=_cp(("parallel", "parallel"), 32),
    )(cidx, full, sib)


def _adam_math(w, g, m, v):
    m2 = B1 * m + (1.0 - B1) * g
    v2 = B2 * v + (1.0 - B2) * (g * g)
    m_hat = m2 / (1.0 - B1 ** STEP)
    v_hat = v2 / (1.0 - B2 ** STEP)
    delta = -LR * (m_hat / (jnp.sqrt(v_hat) + ADAM_EPS) + WD * w)
    return delta, m2, v2


def adamw_sum(recv, part, chip, w, m, v, name):
    K, R, C = recv.shape
    tr = _row_tile(R, C, MIB)

    def body(chip_ref, r_ref, p_ref, w_ref, m_ref, v_ref, g_ref, d_ref, m2_ref, v2_ref):
        g = r_ref[0].astype(F32)
        for k in range(1, K):
            g = g + r_ref[k].astype(F32)
        g = g + p_ref[...].astype(F32)
        delta, m2, v2 = _adam_math(w_ref[...], g, m_ref[...], v_ref[...])
        g_ref[...] = g
        d_ref[...] = delta
        m2_ref[...] = m2
        v2_ref[...] = v2

    tile = pl.BlockSpec((tr, C), lambda i, cr: (i, 0))
    return pl.pallas_call(
        body, name=name,
        grid_spec=pltpu.PrefetchScalarGridSpec(
            num_scalar_prefetch=1, grid=(R // tr,),
            in_specs=[pl.BlockSpec((K, tr, C), lambda i, cr: (0, i, 0)),
                      pl.BlockSpec((None, tr, C), lambda i, cr: (cr[0], i, 0)), tile, tile, tile],
            out_specs=[tile] * 4),
        out_shape=[SDS((R, C), F32)] * 4,
        compiler_params=_cp(("parallel",), 32),
    )(chip, recv, part, w, m, v)


def adamw_small(g, w, m, v, name):
    def body(g_ref, w_ref, m_ref, v_ref, d_ref, m2_ref, v2_ref):
        delta, m2, v2 = _adam_math(w_ref[...], g_ref[...], m_ref[...], v_ref[...])
        d_ref[...] = delta
        m2_ref[...] = m2
        v2_ref[...] = v2

    return pl.pallas_call(body, name=name, out_shape=[SDS(w.shape, F32)] * 3)(g, w, m, v)


def sum_slots(a, name):
    K = a.shape[0]

    def body(a_ref, o_ref):
        t = a_ref[0]
        for k in range(1, K):
            t = t + a_ref[k]
        o_ref[...] = t

    return pl.pallas_call(body, name=name, out_shape=SDS(a.shape[1:], F32))(a)


def norm_pre0(x, g):
    S = x.shape[0]
    ts = 512

    def body(x_ref, g_ref, o_ref, o4_ref, o16_ref, ot_ref, scr):
        h = x_ref[...]
        xn = h * _rms(h) * g_ref[...]
        o_ref[...] = xn.astype(BF16)
        ot_ref[...] = xn.T.astype(BF16)
        _scr_put(scr, xn)
        _store_perm(o4_ref, scr, 4)
        _store_perm(o16_ref, scr, 16)

    return pl.pallas_call(
        body, name="norm_pre0", grid=(S // ts,),
        in_specs=[pl.BlockSpec((ts, D), lambda i: (i, 0)), pl.BlockSpec((1, D), lambda i: (0, 0))],
        out_specs=[pl.BlockSpec((ts, D), lambda i: (i, 0)),
                   pl.BlockSpec((4, ts // 4, D), lambda i: (0, i, 0)),
                   pl.BlockSpec((16, ts // 16, D), lambda i: (0, i, 0)),
                   pl.BlockSpec((D, ts), lambda i: (0, i))],
        out_shape=[SDS((S, D), BF16), SDS((4, S // 4, D), BF16), SDS((16, S // 16, D), BF16), SDS((D, S), BF16)],
        scratch_shapes=[_scr(ts, D)],
        compiler_params=_cp(("parallel",), 32),
    )(x, g)


def transpose_rows(a, name):
    S, C = a.shape
    ts = 512

    def body(a_ref, o_ref):
        o_ref[...] = a_ref[...].astype(F32).T.astype(BF16)

    return pl.pallas_call(
        body, name=name, grid=(S // ts,),
        in_specs=[pl.BlockSpec((ts, C), lambda i: (i, 0))],
        out_specs=pl.BlockSpec((C, ts), lambda i: (0, i)),
        out_shape=SDS((C, S), BF16),
        compiler_params=_cp(("parallel",), 32),
    )(a)


def post0_pre1(x, a0, g_post, g_pre):
    S = x.shape[0]
    ts = 512

    def body(x_ref, a_ref, gp_ref, gn_ref, h_ref, o_ref, ot_ref):
        a = a_ref[...]
        h1 = x_ref[...] + a * _rms(a) * gp_ref[...]
        h_ref[...] = h1
        xn = h1 * _rms(h1) * gn_ref[...]
        o_ref[...] = xn.astype(BF16)
        ot_ref[...] = xn.T.astype(BF16)

    row = pl.BlockSpec((ts, D), lambda i: (i, 0))
    vec = pl.BlockSpec((1, D), lambda i: (0, 0))
    return pl.pallas_call(
        body, name="post0_pre1", grid=(S // ts,),
        in_specs=[row, row, vec, vec],
        out_specs=[row, row, pl.BlockSpec((D, ts), lambda i: (0, i))],
        out_shape=[SDS((S, D), F32), SDS((S, D), BF16), SDS((D, S), BF16)],
        compiler_params=_cp(("parallel",), 40),
    )(x, a0, g_post, g_pre)


def post1_loss(h1, a1, target, g_post):
    S = h1.shape[0]
    ts = 512

    def body(h_ref, a_ref, t_ref, g_ref, dh_ref, da_ref, loss_ref, dg_ref):
        @pl.when(pl.program_id(0) == 0)
        def _():
            loss_ref[...] = jnp.zeros_like(loss_ref)
            dg_ref[...] = jnp.zeros_like(dg_ref)

        a = a_ref[...]
        g = g_ref[...]
        rp = _rms(a)
        yhat = a * rp
        e = h_ref[...] + yhat * g - t_ref[...]
        loss_ref[...] += _fold8(e * e)
        dh = e * (1.0 / D)
        dh_ref[...] = dh
        dg_ref[...] += _fold8(dh * yhat)
        dyh = dh * g
        da = rp * (dyh - yhat * jnp.mean(dyh * yhat, axis=-1, keepdims=True))
        da_ref[...] = da.astype(BF16)

    row = pl.BlockSpec((ts, D), lambda i: (i, 0))
    acc = pl.BlockSpec((8, D), lambda i: (0, 0))
    return pl.pallas_call(
        body, name="post1_loss", grid=(S // ts,),
        in_specs=[row, row, row, pl.BlockSpec((1, D), lambda i: (0, 0))],
        out_specs=[row, row, acc, acc],
        out_shape=[SDS((S, D), F32), SDS((S, D), BF16), SDS((8, D), F32), SDS((8, D), F32)],
        compiler_params=_cp(("arbitrary",), 40),
    )(h1, a1, target, g_post)


def mid_bwd(dxn1, dh2, h1, a0, g_pre1, g_post0):
    S = h1.shape[0]
    ts = 512

    def body(dx_ref, dh2_ref, h_ref, a_ref, gn_ref, gp_ref, dh1_ref, da_ref, dgn_ref, dgp_ref):
        @pl.when(pl.program_id(0) == 0)
        def _():
            dgn_ref[...] = jnp.zeros_like(dgn_ref)
            dgp_ref[...] = jnp.zeros_like(dgp_ref)

        h = h_ref[...]
        r1 = _rms(h)
        xhat = h * r1
        dxn = dx_ref[...]
        dgn_ref[...] += _fold8(dxn * xhat)
        dxh = dxn * gn_ref[...]
        dh1 = dh2_ref[...] + r1 * (dxh - xhat * jnp.mean(dxh * xhat, axis=-1, keepdims=True))
        dh1_ref[...] = dh1
        a = a_ref[...]
        rp = _rms(a)
        yhat = a * rp
        dgp_ref[...] += _fold8(dh1 * yhat)
        dyh = dh1 * gp_ref[...]
        da = rp * (dyh - yhat * jnp.mean(dyh * yhat, axis=-1, keepdims=True))
        da_ref[...] = da.astype(BF16)

    row = pl.BlockSpec((ts, D), lambda i: (i, 0))
    vec = pl.BlockSpec((1, D), lambda i: (0, 0))
    acc = pl.BlockSpec((8, D), lambda i: (0, 0))
    return pl.pallas_call(
        body, name="mid_bwd", grid=(S // ts,),
        in_specs=[row, row, row, row, vec, vec],
        out_specs=[row, row, acc, acc],
        out_shape=[SDS((S, D), F32), SDS((S, D), BF16), SDS((8, D), F32), SDS((8, D), F32)],
        compiler_params=_cp(("arbitrary",), 48),
    )(dxn1, dh2, h1, a0, g_pre1, g_post0)


def pre0_bwd(dx_tok, dx_z, dx4, dx16, dh1, x, g_pre0):
    S = x.shape[0]
    ts = 512

    def body(da_ref, dz_ref, d4_ref, d16_ref, dh_ref, x_ref, g_ref, gx_ref, dg_ref, scr):
        @pl.when(pl.program_id(0) == 0)
        def _():
            dg_ref[...] = jnp.zeros_like(dg_ref)

        _scr_put(scr, da_ref[...] + dz_ref[...])
        _load_perm(scr, d4_ref, 4, add=True)
        _load_perm(scr, d16_ref, 16, add=True)
        h = x_ref[...]
        r = _rms(h)
        xhat = h * r
        dxn = _scr_get(scr)
        dg_ref[...] += _fold8(dxn * xhat)
        dxh = dxn * g_ref[...]
        gx_ref[...] = dh_ref[...] + r * (dxh - xhat * jnp.mean(dxh * xhat, axis=-1, keepdims=True))

    row = pl.BlockSpec((ts, D), lambda i: (i, 0))
    return pl.pallas_call(
        body, name="pre0_bwd", grid=(S // ts,),
        in_specs=[row, row, pl.BlockSpec((4, ts // 4, D), lambda i: (0, i, 0)),
                  pl.BlockSpec((16, ts // 16, D), lambda i: (0, i, 0)), row, row,
                  pl.BlockSpec((1, D), lambda i: (0, 0))],
        out_specs=[row, pl.BlockSpec((8, D), lambda i: (0, 0))],
        out_shape=[SDS((S, D), F32), SDS((8, D), F32)],
        scratch_shapes=[_scr(ts, D)],
        compiler_params=_cp(("arbitrary",), 48),
    )(dx_tok, dx_z, dx4.reshape(4, S // 4, D), dx16.reshape(16, S // 16, D), dh1, x, g_pre0)


def _w_tile(tile0):
    per = W_SHARD // CT
    return lambda t: ((tile0 + t) // per, 0, (tile0 + t) % per)


def mm_in(xn, w8, tile0, ntiles, tabs, name, after=None):
    S = xn.shape[0]
    tm = 2048
    wt = _w_tile(tile0)

    def body(a_ref, b_ref, *rest):
        o_ref = rest[-1]
        rc = 512
        for u in range(tm // rc):
            rows = slice(u * rc, (u + 1) * rc)
            r = _dot(a_ref[rows, :], b_ref[...])
            if tabs is None:
                o_ref[rows, :] = r.astype(BF16)
                continue
            c_ref, s1_ref, s2_ref = rest[:3]
            rot = pl.program_id(1) < 2 * E // CT
            c = jnp.where(rot, c_ref[rows, :], 1.0)
            s1 = jnp.where(rot, s1_ref[rows, :], 0.0)
            s2 = jnp.where(rot, s2_ref[rows, :], 0.0)
            for hh in range(CT // HD):
                cs = slice(hh * HD, (hh + 1) * HD)
                o_ref[rows, cs] = _rope(r[:, cs], c, s1, s2).astype(BF16)

    tab = pl.BlockSpec((tm, HD), lambda i, t: (i, 0))
    return pl.pallas_call(
        body, name=name, grid=(S // tm, ntiles),
        in_specs=[pl.BlockSpec((tm, D), lambda i, t: (i, 0)),
                  pl.BlockSpec((None, D, CT), lambda i, t: wt(t))] + ([] if tabs is None else [tab] * 3)
        + ([] if after is None else [ANY]),
        out_specs=pl.BlockSpec((tm, CT), lambda i, t: (i, t)),
        out_shape=SDS((S, ntiles * CT), BF16),
        compiler_params=_cp(("parallel", "parallel"), 48),
    )(xn, w8, *(() if tabs is None else tabs), *(() if after is None else (after,)))


def mm_rows(a, b, name, out_dtype, tm=1024):
    M, K = a.shape
    N = b.shape[1]

    def body(a_ref, b_ref, o_ref):
        def chunk(cidx, carry):
            col = pl.ds(pl.multiple_of(cidx * 256, 256), 256)
            o_ref[:, col] = _dot(a_ref[...], b_ref[:, col]).astype(out_dtype)
            return carry

        lax.fori_loop(0, N // 256, chunk, 0)

    return pl.pallas_call(
        body, name=name, grid=(M // tm,),
        in_specs=[pl.BlockSpec((tm, K), lambda i: (i, 0)), pl.BlockSpec((K, N), lambda i: (0, 0))],
        out_specs=pl.BlockSpec((tm, N), lambda i: (i, 0)),
        out_shape=SDS((M, N), out_dtype),
        compiler_params=_cp(("parallel",), 48),
    )(a, b)


def mm_uz(xn, w8):
    S = xn.shape[0]
    tm = 512
    bw = w8.shape[2]

    def body(a_ref, b_ref, o_ref):
        def blk(dv, carry):
            for half in range(bw // 256):
                col = pl.ds(pl.multiple_of(dv * bw + half * 256, 256), 256)
                o_ref[:, col] = _dot(a_ref[...], b_ref[dv, :, half * 256:(half + 1) * 256])
            return carry

        lax.fori_loop(0, N_DEV, blk, 0)

    return pl.pallas_call(
        body, name="mm_uz", grid=(S // tm,),
        in_specs=[pl.BlockSpec((tm, D), lambda i: (i, 0)), pl.BlockSpec((N_DEV, D, bw), lambda i: (0, 0, 0))],
        out_specs=pl.BlockSpec((tm, N_DEV * bw), lambda i: (i, 0)),
        out_shape=SDS((S, N_DEV * bw), F32),
        compiler_params=_cp(("parallel",), 48),
    )(xn, w8)


def mm_acc(a, b, name, *, grid, a_spec, b_spec, o_spec, o_shape, acc_shape, write, vmem=48):
    nk = grid[-1]

    def body(a_ref, b_ref, o_ref, acc_ref):
        k = pl.program_id(len(grid) - 1)

        @pl.when(k == 0)
        def _():
            acc_ref[...] = jnp.zeros_like(acc_ref)

        acc_ref[...] += _dot(a_ref[...], b_ref[...])

        @pl.when(k == nk - 1)
        def _():
            write(o_ref, acc_ref)

    return pl.pallas_call(
        body, name=name, grid=grid, in_specs=[a_spec, b_spec], out_specs=o_spec, out_shape=o_shape,
        scratch_shapes=[pltpu.VMEM(acc_shape, F32)],
        compiler_params=_cp(("parallel",) * (len(grid) - 1) + ("arbitrary",), vmem),
    )(a, b)


def _write_plain(o_ref, acc_ref):
    o_ref[...] = acc_ref[...]


def mm_wgrad_rows(at, b, name):
    M, S = at.shape
    N = b.shape[1]
    tm, tk = 1024, 1024
    return mm_acc(at, b, name, grid=(M // tm, S // tk),
                  a_spec=pl.BlockSpec((tm, tk), lambda i, k: (i, k)),
                  b_spec=pl.BlockSpec((tk, N), lambda i, k: (k, 0)),
                  o_spec=pl.BlockSpec((tm, N), lambda i, k: (i, 0)),
                  o_shape=SDS((M, N), F32), acc_shape=(tm, N), write=_write_plain)


def mm_wgrad_cols(at, b, name, *, ncols, shard, tn):
    M, S = at.shape
    tk = 1024
    per = shard // tn if tn <= shard else 1
    nb = max(1, tn // shard)

    if tn <= shard:
        o_spec = pl.BlockSpec((None, M, tn), lambda t, k: (t // per, 0, t % per))
        write = _write_plain
    else:
        o_spec = pl.BlockSpec((nb, M, shard), lambda t, k: (t, 0, 0))

        def write(o_ref, acc_ref):
            for u in range(nb):
                o_ref[u] = acc_ref[:, u * shard:(u + 1) * shard]

    return mm_acc(at, b, name, grid=(ncols // tn, S // tk),
                  a_spec=pl.BlockSpec((M, tk), lambda t, k: (0, k)),
                  b_spec=pl.BlockSpec((tk, tn), lambda t, k: (k, t)),
                  o_spec=o_spec, o_shape=SDS((N_DEV, M, shard), F32), acc_shape=(M, tn), write=write)


def mm_dwg(pooled_t, dh):
    S = dh.shape[0]
    tk = 2048
    return mm_acc(pooled_t, dh, "mm_dwg", grid=(4, S // tk),
                  a_spec=pl.BlockSpec((PC, tk), lambda g, k: (g, k)),
                  b_spec=pl.BlockSpec((tk, PC), lambda g, k: (k, g)),
                  o_spec=pl.BlockSpec((None, PC, PC), lambda g, k: (g, 0, 0)),
                  o_shape=SDS((4, PC, PC), F32), acc_shape=(PC, PC), write=_write_plain)


def mm_dx_stack(da, w8, name, *, kchunk, tm):
    S = da.shape[0]
    bw = w8.shape[2]
    nck = bw // kchunk

    def body(a_ref, b_ref, o_ref, acc_ref):
        j = pl.program_id(1)

        @pl.when(j == 0)
        def _():
            acc_ref[...] = jnp.zeros_like(acc_ref)

        for u in range(nck):
            ks = slice(u * kchunk, (u + 1) * kchunk)
            acc_ref[...] += _dot_nt(a_ref[:, ks], b_ref[:, ks])

        @pl.when(j == N_DEV - 1)
        def _():
            o_ref[...] = acc_ref[...]

    return pl.pallas_call(
        body, name=name, grid=(S // tm, N_DEV),
        in_specs=[pl.BlockSpec((tm, bw), lambda i, j: (i, j)),
                  pl.BlockSpec((None, D, bw), lambda i, j: (j, 0, 0))],
        out_specs=pl.BlockSpec((tm, D), lambda i, j: (i, 0)),
        out_shape=SDS((S, D), F32),
        scratch_shapes=[pltpu.VMEM((tm, D), F32)],
        compiler_params=_cp(("parallel", "arbitrary"), 48),
    )(da, w8)


def mm_dw_in_part(at, b, tile0, prev, name):
    M, S = at.shape
    ntiles = b.shape[1] // CT
    tk = 2048
    nk = S // tk
    wt = _w_tile(tile0)

    def body(a_ref, b_ref, *rest):
        o_ref, acc_ref = rest[-2:]
        k = pl.program_id(1)

        @pl.when(k == 0)
        def _():
            acc_ref[...] = jnp.zeros_like(acc_ref)

        acc_ref[...] += _dot(a_ref[...], b_ref[...])

        @pl.when(k == nk - 1)
        def _():
            o_ref[...] = acc_ref[...]

    return pl.pallas_call(
        body, name=name, grid=(ntiles, nk),
        in_specs=[pl.BlockSpec((M, tk), lambda t, k: (0, k)), pl.BlockSpec((tk, CT), lambda t, k: (k, t))]
        + ([] if prev is None else [ANY]),
        out_specs=pl.BlockSpec((None, M, CT), lambda t, k: wt(t)),
        out_shape=SDS((N_DEV, M, W_SHARD), F32),
        scratch_shapes=[pltpu.VMEM((M, CT), F32)],
        input_output_aliases={} if prev is None else {2: 0},
        compiler_params=_cp(("parallel", "arbitrary"), 48),
    )(at, b, *(() if prev is None else (prev,)))


def mm_dx_part(da, w8, tile0, name, after=None):
    S = da.shape[0]
    ntiles = da.shape[1] // CT
    tm = 2048
    wt = _w_tile(tile0)

    def body(a_ref, b_ref, *rest):
        o_ref, acc_ref = rest[-2:]
        t = pl.program_id(1)

        @pl.when(t == 0)
        def _():
            acc_ref[...] = jnp.zeros_like(acc_ref)

        acc_ref[...] += _dot_nt(a_ref[...], b_ref[...])

        @pl.when(t == ntiles - 1)
        def _():
            o_ref[...] = acc_ref[...]

    return pl.pallas_call(
        body, name=name, grid=(S // tm, ntiles),
        in_specs=[pl.BlockSpec((tm, CT), lambda i, t: (i, t)), pl.BlockSpec((None, D, CT), lambda i, t: wt(t))]
        + ([] if after is None else [ANY]),
        out_specs=pl.BlockSpec((tm, D), lambda i, t: (i, 0)),
        out_shape=SDS((S, D), F32),
        scratch_shapes=[pltpu.VMEM((tm, D), F32)],
        compiler_params=_cp(("parallel", "arbitrary"), 56),
    )(da, w8, *(() if after is None else (after,)))


def _band_masks(not_first):
    row = lax.broadcasted_iota(jnp.int32, (QB, QB), 0)
    col = lax.broadcasted_iota(jnp.int32, (QB, QB), 1)
    cur = jnp.where(col <= row, 0.0, NEG)
    prev = jnp.where(col >= row, 0.0, NEG)
    first = jnp.where(jnp.logical_and(col >= row, not_first), 0.0, NEG)
    return col, jnp.concatenate([prev, cur], axis=1), jnp.concatenate([first, cur], axis=1)


def _fill_kv(ext, qkv_ref, kh_ref, vh_ref):
    ext[0:QB, 0:E] = kh_ref[...]
    ext[0:QB, E:2 * E] = vh_ref[...]
    ext[QB:, :] = qkv_ref[:, E:3 * E]


def attn_fwd(P, g, d):
    S = P.shape[0]
    L = S // d
    T = min(512, L)
    nq = T // QB
    ni = L // T

    def body(qkv_ref, kh_ref, vh_ref, o_ref, lse_ref, ext):
        col, mask, mask_first = _band_masks(pl.program_id(1) > 0)
        lse_ref[...] = jnp.zeros_like(lse_ref)
        _fill_kv(ext, qkv_ref, kh_ref, vh_ref)

        def head(h, carry):
            off = pl.multiple_of(h * HD, HD)
            cq, cv = pl.ds(off, HD), pl.ds(E + off, HD)
            for j in range(nq):
                rows = slice(j * QB, (j + 1) * QB)
                krows = slice(j * QB, (j + 2) * QB)
                s = _dot_nt(qkv_ref[rows, cq], ext[krows, cq]) * SCALE + (mask_first if j == 0 else mask)
                m = jnp.max(s, axis=1, keepdims=True)
                p = jnp.exp(s - m)
                den = jnp.sum(p, axis=1, keepdims=True)
                o = _dot(p.astype(BF16), ext[krows, cv]) / den
                o_ref[rows, cq] = o.astype(BF16)
                lse_ref[rows, :] = jnp.where(col == h, m + jnp.log(den), lse_ref[rows, :])
            return carry

        lax.fori_loop(0, NH, head, 0, unroll=2)

    halo = lambda r, i: jnp.maximum(r * (L // QB) + i * nq - 1, 0)
    return pl.pallas_call(
        body, name=f"attn_fwd{g}", grid=(d, ni),
        in_specs=[pl.BlockSpec((T, SEG), lambda r, i: (r * ni + i, 0)),
                  pl.BlockSpec((QB, E), lambda r, i: (halo(r, i), 1)),
                  pl.BlockSpec((QB, E), lambda r, i: (halo(r, i), 2))],
        out_specs=[pl.BlockSpec((T, E), lambda r, i: (r * ni + i, 0)),
                   pl.BlockSpec((T, HD), lambda r, i: (r * ni + i, 0))],
        out_shape=[SDS((S, E), BF16), SDS((S, HD), F32)],
        scratch_shapes=[pltpu.VMEM((T + QB, 2 * E), BF16)],
        compiler_params=_cp(("parallel", "parallel"), 48),
    )(P, P, P)


def _perm_specs(ts, C):
    return [pl.BlockSpec((ts, C), lambda i: (i, 0)),
            pl.BlockSpec((4, ts // 4, C), lambda i: (0, i, 0)),
            pl.BlockSpec((16, ts // 16, C), lambda i: (0, i, 0))]


def _perm_shapes(S, C, dtype):
    return [SDS((S, C), dtype), SDS((4, S // 4, C), dtype), SDS((16, S // 16, C), dtype)]


def combine_fwd(os_, lses, z, ehot):
    S = z.shape[0]
    ts = 256

    def body(o0, o1, o2, l0, l1, l2, z_ref, e_ref, y0, y1, y2, s0, s1, s2, ya_ref, yat_ref,
             so1, so2, sl1, sl2, sy, sl):
        _load_perm(so1, o1, 4)
        _load_perm(so2, o2, 16)
        _load_perm(sl1, l1, 4)
        _load_perm(sl2, l2, 16)
        ls = [l0[...], sl1[0], sl2[0]]
        m = jnp.maximum(jnp.maximum(ls[0], ls[1]), ls[2])
        es = [jnp.exp(l - m) for l in ls]
        den = es[0] + es[1] + es[2]
        sl[0] = m + jnp.log(den)
        y = None
        for e, o in zip(es, (o0[...].astype(F32), _scr_get(so1), _scr_get(so2))):
            w = e / den
            hi = w.astype(BF16)
            lo = (w - hi.astype(F32)).astype(BF16)
            wb = _dot(hi, e_ref[...]) + _dot(lo, e_ref[...])
            y = wb * o if y is None else y + wb * o
        z = z_ref[...].astype(F32)
        ya = y * (z * _sigmoid(z))
        ya_ref[...] = ya.astype(BF16)
        yat_ref[...] = ya.T.astype(BF16)
        _scr_put(sy, y)
        y0[...] = y.astype(BF16)
        _store_perm(y1, sy, 4)
        _store_perm(y2, sy, 16)
        s0[...] = sl[0]
        _store_perm(s1, sl, 4)
        _store_perm(s2, sl, 16)

    wide = pl.BlockSpec((ts, E), lambda i: (i, 0))
    os3 = [os_[0], os_[1].reshape(4, S // 4, E), os_[2].reshape(16, S // 16, E)]
    ls3 = [lses[0], lses[1].reshape(4, S // 4, HD), lses[2].reshape(16, S // 16, HD)]
    res = pl.pallas_call(
        body, name="combine_fwd", grid=(S // ts,),
        in_specs=_perm_specs(ts, E) + _perm_specs(ts, HD) + [wide, pl.BlockSpec((HD, E), lambda i: (0, 0))],
        out_specs=_perm_specs(ts, E) + _perm_specs(ts, HD) + [wide, pl.BlockSpec((E, ts), lambda i: (0, i))],
        out_shape=_perm_shapes(S, E, BF16) + _perm_shapes(S, HD, F32) + [SDS((S, E), BF16), SDS((E, S), BF16)],
        scratch_shapes=[_scr(ts, E), _scr(ts, E), _scr(ts, HD), _scr(ts, HD), _scr(ts, E), _scr(ts, HD)],
        compiler_params=_cp(("parallel",), 56),
    )(*os3, *ls3, z, ehot)
    ys = [res[0], res[1].reshape(S, E), res[2].reshape(S, E)]
    lse3 = [res[3], res[4].reshape(S, HD), res[5].reshape(S, HD)]
    return ys, lse3, res[6], res[7]


def mm_dya(da0, w_out, z, y):
    S = da0.shape[0]
    tm = 512

    def body(a_ref, w_ref, z_ref, y_ref, dy0, dy1, dy2, dz_ref, scr):
        def chunk(cidx, carry):
            col = pl.ds(pl.multiple_of(cidx * 256, 256), 256)
            dya = _dot_nt(a_ref[...], w_ref[col, :])
            zz = z_ref[:, col].astype(F32)
            sig = _sigmoid(zz)
            dy = dya * zz * sig
            scr[2 * cidx] = dy[:, :LANES]
            scr[2 * cidx + 1] = dy[:, LANES:]
            dy0[:, col] = dy.astype(BF16)
            dz_ref[:, col] = (dya * y_ref[:, col].astype(F32) * sig * (1.0 + zz * (1.0 - sig))).astype(BF16)
            return carry

        lax.fori_loop(0, E // 256, chunk, 0)
        _store_perm(dy1, scr, 4)
        _store_perm(dy2, scr, 16)

    wide = pl.BlockSpec((tm, E), lambda i: (i, 0))
    res = pl.pallas_call(
        body, name="mm_dya", grid=(S // tm,),
        in_specs=[pl.BlockSpec((tm, D), lambda i: (i, 0)), pl.BlockSpec((E, D), lambda i: (0, 0)), wide, wide],
        out_specs=_perm_specs(tm, E) + [wide],
        out_shape=_perm_shapes(S, E, BF16) + [SDS((S, E), BF16)],
        scratch_shapes=[_scr(tm, E)],
        compiler_params=_cp(("parallel",), 48),
    )(da0, w_out, z, y)
    return [res[0], res[1].reshape(S, E), res[2].reshape(S, E)], res[3]


def attn_bwd(P, dy, y, lse, tabs, g, d):
    S = P.shape[0]
    L = S // d
    T = min(512, L)
    nq = T // QB
    ni = L // T

    def body(qkv_ref, kh_ref, vh_ref, dy_ref, y_ref, lse_ref, c_ref, s1_ref, s2_ref,
             o_ref, dkc_ref, dvc_ref, ext):
        i = pl.program_id(1)
        col, mask, mask_first = _band_masks(i < ni - 1)
        _fill_kv(ext, qkv_ref, kh_ref, vh_ref)

        @pl.when(i == 0)
        def _():
            dkc_ref[...] = jnp.zeros_like(dkc_ref)
            dvc_ref[...] = jnp.zeros_like(dvc_ref)

        def head(h, carry):
            off = pl.multiple_of(h * HD, HD)
            cq, ck, cv = pl.ds(off, HD), pl.ds(E + off, HD), pl.ds(2 * E + off, HD)
            pend_dk = dkc_ref[:, cq]
            pend_dv = dvc_ref[:, cq]
            for j in reversed(range(nq)):
                rows = slice(j * QB, (j + 1) * QB)
                krows = slice(j * QB, (j + 2) * QB)
                q = qkv_ref[rows, cq]
                k2 = ext[krows, cq]
                dyj = dy_ref[rows, cq]
                lse_h = jnp.sum(jnp.where(col == h, lse_ref[rows, :], 0.0), axis=1, keepdims=True)
                delta = jnp.sum(dyj.astype(F32) * y_ref[rows, cq].astype(F32), axis=1, keepdims=True)
                p = jnp.exp(_dot_nt(q, k2) * SCALE + (mask_first if j == 0 else mask) - lse_h)
                ds = p * (_dot_nt(dyj, ext[krows, ck]) - delta) * SCALE
                dq = _dot(ds.astype(BF16), k2)
                dk2 = _dot(ds.astype(BF16).T, q)
                dv2 = _dot(p.astype(BF16).T, dyj)
                dk = dk2[QB:] + pend_dk
                dv = dv2[QB:] + pend_dv
                pend_dk, pend_dv = dk2[:QB], dv2[:QB]
                c, s1, s2 = c_ref[rows, :], s1_ref[rows, :], s2_ref[rows, :]
                o_ref[rows, cq] = _unrope(dq, c, s1, s2).astype(BF16)
                o_ref[rows, ck] = _unrope(dk, c, s1, s2).astype(BF16)
                o_ref[rows, cv] = dv.astype(BF16)
            dkc_ref[:, cq] = pend_dk
            dvc_ref[:, cq] = pend_dv
            return carry

        lax.fori_loop(0, NH, head, 0, unroll=2)

    blk = lambda r, i: r * ni + ni - 1 - i
    halo = lambda r, i: jnp.maximum(r * (L // QB) + (ni - 1 - i) * nq - 1, 0)
    main = pl.BlockSpec((T, SEG), lambda r, i: (blk(r, i), 0))
    wide = pl.BlockSpec((T, E), lambda r, i: (blk(r, i), 0))
    narrow = pl.BlockSpec((T, HD), lambda r, i: (blk(r, i), 0))
    return pl.pallas_call(
        body, name=f"attn_bwd{g}", grid=(d, ni),
        in_specs=[main, pl.BlockSpec((QB, E), lambda r, i: (halo(r, i), 1)),
                  pl.BlockSpec((QB, E), lambda r, i: (halo(r, i), 2)),
                  wide, wide, narrow, narrow, narrow, narrow],
        out_specs=main, out_shape=SDS((S, SEG), BF16),
        scratch_shapes=[pltpu.VMEM((QB, E), F32), pltpu.VMEM((QB, E), F32), pltpu.VMEM((T + QB, 2 * E), BF16)],
        compiler_params=_cp(("arbitrary", "arbitrary"), 56),
    )(P, P, P, dy, y, lse, *tabs)


def _pool_cnt(t0, rows):
    t = (lax.broadcasted_iota(jnp.int32, (rows, E), 0) + t0 + 1).astype(F32)
    ch = lax.broadcasted_iota(jnp.int32, (rows, E), 1)
    w = jnp.where(ch < PC, 2.0, jnp.where(ch < 2 * PC, 4.0, jnp.where(ch < 3 * PC, 8.0, 16.0)))
    return jnp.minimum(t, w)


def _by_group(parts):
    return jnp.concatenate([parts[g][:, g * PC:(g + 1) * PC] for g in range(4)], axis=1)


def pool_fwd(uz):
    S = uz.shape[0]
    ts = 256

    def body(u_ref, h_ref, o_ref, ot_ref):
        i = pl.program_id(0)
        u = u_ref[...]
        halo = jnp.where(i > 0, h_ref[...], 0.0)
        ext = jnp.concatenate([halo, u], axis=0)
        s2 = ext + pltpu.roll(ext, 1, 0)
        s4 = s2 + pltpu.roll(s2, 2, 0)
        s8 = s4 + pltpu.roll(s4, 4, 0)
        s16 = s8 + pltpu.roll(s8, 8, 0)
        win = _by_group([s2, s4, s8, s16])[16:, :]
        pooled = win / _pool_cnt(i * ts, ts) - u
        o_ref[...] = pooled.astype(BF16)
        ot_ref[...] = pooled.T.astype(BF16)

    return pl.pallas_call(
        body, name="pool_fwd", grid=(S // ts,),
        in_specs=[pl.BlockSpec((ts, E), lambda i: (i, 0)),
                  pl.BlockSpec((16, E), lambda i: (jnp.maximum(i * (ts // 16) - 1, 0), 0))],
        out_specs=[pl.BlockSpec((ts, E), lambda i: (i, 0)), pl.BlockSpec((E, ts), lambda i: (0, i))],
        out_shape=[SDS((S, E), BF16), SDS((E, S), BF16)],
        compiler_params=_cp(("parallel",), 48),
    )(uz, uz)


def pool_bwd(dpooled, duz):
    S = dpooled.shape[0]
    ts = 256
    nt = S // ts

    def body(d_ref, h_ref, alias_ref, o_ref):
        i = pl.program_id(0)
        dp = d_ref[...].astype(F32)
        halo = jnp.where(i < nt - 1, h_ref[...].astype(F32), 0.0)
        n = ts + 16
        ext = jnp.concatenate([dp, halo], axis=0) / _pool_cnt(i * ts, n)
        f2 = ext + pltpu.roll(ext, n - 1, 0)
        f4 = f2 + pltpu.roll(f2, n - 2, 0)
        f8 = f4 + pltpu.roll(f4, n - 4, 0)
        f16 = f8 + pltpu.roll(f8, n - 8, 0)
        win = _by_group([f2, f4, f8, f16])[:ts, :]
        o_ref[...] = (win - dp).astype(BF16)

    return pl.pallas_call(
        body, name="pool_bwd", grid=(nt,),
        in_specs=[pl.BlockSpec((ts, E), lambda i: (i, 0)),
                  pl.BlockSpec((16, E), lambda i: (jnp.minimum((i + 1) * (ts // 16), S // 16 - 1), 0)), ANY],
        out_specs=pl.BlockSpec((ts, E), lambda i: (i, 0)),
        out_shape=SDS(duz.shape, BF16),
        input_output_aliases={2: 0},
        compiler_params=_cp(("parallel",), 48),
    )(dpooled, dpooled, duz)


def mm_grp(pooled, wg, b, scale, uz):
    S = pooled.shape[0]
    tm = 512

    def body(p_ref, w_ref, b_ref, s_ref, z_ref, h_ref, y_ref, yt_ref):
        for g in range(4):
            cs = slice(g * PC, (g + 1) * PC)
            h = _dot(p_ref[:, cs], w_ref[g]) + b_ref[:, cs]
            z = z_ref[:, cs]
            yp = h * s_ref[:, cs] * (z * _sigmoid(z))
            h_ref[:, cs] = h.astype(BF16)
            y_ref[:, cs] = yp.astype(BF16)
            yt_ref[cs, :] = yp.T.astype(BF16)

    row = pl.BlockSpec((tm, E), lambda i: (i, 0))
    vec = pl.BlockSpec((1, E), lambda i: (0, 0))
    return pl.pallas_call(
        body, name="mm_grp", grid=(S // tm,),
        in_specs=[row, pl.BlockSpec((4, PC, PC), lambda i: (0, 0, 0)), vec, vec,
                  pl.BlockSpec((tm, E), lambda i: (i, 1))],
        out_specs=[row, row, pl.BlockSpec((E, tm), lambda i: (0, i))],
        out_shape=[SDS((S, E), BF16), SDS((S, E), BF16), SDS((E, S), BF16)],
        compiler_params=_cp(("parallel",), 48),
    )(pooled, wg, b, scale, uz)


def mm_dyp(da1, w_out, uz, h, scale):
    S = da1.shape[0]
    tm = 512

    def body(a_ref, w_ref, z_ref, h_ref, s_ref, dh_ref, dz_ref, dsc_ref, db_ref):
        @pl.when(pl.program_id(0) == 0)
        def _():
            dsc_ref[...] = jnp.zeros_like(dsc_ref)
            db_ref[...] = jnp.zeros_like(db_ref)

        def chunk(cidx, carry):
            col = pl.ds(pl.multiple_of(cidx * 256, 256), 256)
            dyp = _dot_nt(a_ref[...], w_ref[col, :])
            z = z_ref[:, col]
            hh = h_ref[:, col].astype(F32)
            sc = s_ref[:, col]
            sig = _sigmoid(z)
            dhs = dyp * z * sig
            dz_ref[:, col] = (dyp * hh * sc * sig * (1.0 + z * (1.0 - sig))).astype(BF16)
            dh = dhs * sc
            dh_ref[:, col] = dh.astype(BF16)
            dsc_ref[:, col] += _fold8(dhs * hh)
            db_ref[:, col] += _fold8(dh)
            return carry

        lax.fori_loop(0, E // 256, chunk, 0)

    row = pl.BlockSpec((tm, E), lambda i: (i, 0))
    acc = pl.BlockSpec((8, E), lambda i: (0, 0))
    return pl.pallas_call(
        body, name="mm_dyp", grid=(S // tm,),
        in_specs=[pl.BlockSpec((tm, D), lambda i: (i, 0)), pl.BlockSpec((E, D), lambda i: (0, 0)),
                  pl.BlockSpec((tm, E), lambda i: (i, 1)), row, pl.BlockSpec((1, E), lambda i: (0, 0))],
        out_specs=[row, pl.BlockSpec((tm, E), lambda i: (i, 1)), acc, acc],
        out_shape=[SDS((S, E), BF16), SDS((S, 2 * E), BF16), SDS((8, E), F32), SDS((8, E), F32)],
        compiler_params=_cp(("arbitrary",), 48),
    )(da1, w_out, uz, h, scale)


def mm_dpooled(dh, wg):
    S = dh.shape[0]
    tm = 1024

    def body(a_ref, w_ref, o_ref):
        for g in range(4):
            cs = slice(g * PC, (g + 1) * PC)
            o_ref[:, cs] = _dot_nt(a_ref[:, cs], w_ref[g]).astype(BF16)

    row = pl.BlockSpec((tm, E), lambda i: (i, 0))
    return pl.pallas_call(
        body, name="mm_dpooled", grid=(S // tm,),
        in_specs=[row, pl.BlockSpec((4, PC, PC), lambda i: (0, 0, 0))],
        out_specs=row, out_shape=SDS((S, E), BF16),
        compiler_params=_cp(("parallel",), 48),
    )(dh, wg)


def _rope_tables(positions):
    inv_freq = 500000.0 ** (-jnp.arange(0, 32, 2, dtype=F32) / 32)
    ang = positions.astype(F32)[:, None] * inv_freq
    cos, sin = jnp.cos(ang), jnp.sin(ang)
    S = positions.shape[0]
    one = jnp.ones((S, HD - 32), F32)
    zero16 = jnp.zeros((S, 16), F32)
    zero = jnp.zeros((S, HD - 32), F32)
    c = jnp.concatenate([cos, cos, one], axis=1)
    s1 = jnp.concatenate([-sin, zero16, zero], axis=1)
    s2 = jnp.concatenate([zero16, sin, zero], axis=1)
    return c.astype(BF16), s1.astype(BF16), s2.astype(BF16)


def kernel(x, positions, norm_pre, norm_post, attn_w_in, attn_w_out, pool_w_in, pool_w_grp, pool_b_grp, pool_scale, pool_w_out, loss_target, m_norm_pre, m_norm_post, m_attn_w_in, m_attn_w_out, m_pool_w_in, m_pool_w_grp, m_pool_b_grp, m_pool_scale, m_pool_w_out, v_norm_pre, v_norm_post, v_attn_w_in, v_attn_w_out, v_pool_w_in, v_pool_w_grp, v_pool_b_grp, v_pool_scale, v_pool_w_out):
    S = x.shape[1]
    xi, yi, ci = _mesh_pos()
    dev = 4 * xi + 2 * yi + ci
    x2 = x[0]
    tgt = loss_target[0]

    small = jnp.concatenate([pool_b_grp[0].reshape(2, HD), pool_scale[0].reshape(2, HD),
                             jnp.zeros((4, HD), F32)], axis=0)
    (w_in8,) = all_gather([attn_w_in[0].astype(BF16)], "gather_w_in")
    small, w_in8 = lax.optimization_barrier((small, w_in8))
    rest_l = [attn_w_out[0].astype(BF16), pool_w_in[0].astype(BF16), pool_w_grp[0].astype(BF16),
              pool_w_out[0].astype(BF16), small]
    rest_flight, rest_token = split_start(
        "gather_rest_start", rest_l, [lax.empty((N_DEV,) + a.shape, a.dtype) for a in rest_l], _peers_plan(), 7)

    pos = positions[0]
    tabs = [_rope_tables(pos.reshape(S // d, d).T.reshape(S)) for d in DIL]
    ehot = (jnp.arange(E)[None, :] // HD == jnp.arange(HD)[:, None]).astype(BF16)
    seg_tiles = SEG // CT

    xn0, xn0_4, xn0_16, xn0t = norm_pre0(x2, norm_pre[0:1])
    xn0s = [xn0, xn0_4.reshape(S, D), xn0_16.reshape(S, D)]
    xn0ts = [xn0t, transpose_rows(xn0s[1], "xn0t_4"), transpose_rows(xn0s[2], "xn0t_16")]
    Ps, os_, lses = [], [], []
    for g, d in enumerate(DIL):
        P = mm_in(xn0s[g], w_in8, g * seg_tiles, seg_tiles, tabs[g], f"mm_qkv{g}", after=rest_token)
        o, l = attn_fwd(P, g, d)
        Ps.append(P)
        os_.append(o)
        lses.append(l)
    z0 = mm_in(xn0, w_in8, 3 * seg_tiles, E // CT, None, "mm_z0")
    ys, lse3, ya, yat = combine_fwd(os_, lses, z0, ehot)

    rest_l, rest8 = split_wait("gather_rest_wait", rest_flight, _peers_plan(), ya)
    rest8 = [lax.dynamic_update_slice(r8, a[None], (dev,) + (0,) * a.ndim) for r8, a in zip(rest8, rest_l)]
    w_out8, wp_in8, wg8, wp_out8, small8 = rest8
    w_out = w_out8.reshape(E, D)
    wp_out = wp_out8.reshape(E, D)
    wg = wg8.transpose(1, 0, 2, 3).reshape(4, PC, PC)
    b_full = small8[:, 0:2, :].reshape(N_DEV, 4, PC // N_DEV).transpose(1, 0, 2).reshape(1, E)
    scale_full = small8[:, 2:4, :].reshape(1, E)
    a0 = mm_rows(ya, w_out, "mm_out0", F32)
    h1, xn1, xn1t = post0_pre1(x2, a0, norm_post[0:1], norm_pre[1:2])

    uz = mm_uz(xn1, wp_in8)
    pooled, pooled_t = pool_fwd(uz)
    hgrp, yp, ypt = mm_grp(pooled, wg, b_full, scale_full, uz)
    a1 = mm_rows(yp, wp_out, "mm_out1", F32)
    dh2, da1, loss_rows, dg_post1 = post1_loss(h1, a1, tgt, norm_post[1:2])
    loss = lax.psum(0.5 / D * jnp.sum(loss_rows), ("x", "y", "c"))

    dh, duz, dscale_p, db_p = mm_dyp(da1, wp_out, uz, hgrp, scale_full)
    dpooled = mm_dpooled(dh, wg)
    duz = pool_bwd(dpooled, duz)
    g_wg = mm_dwg(pooled_t, dh)
    g_wp_out = mm_wgrad_rows(ypt, da1, "mm_dwp_out")
    g_wp_in = mm_wgrad_cols(xn1t, duz, "mm_dwp_in", ncols=2 * E, shard=PC, tn=1024)
    dxn1 = mm_dx_stack(duz, wp_in8, "mm_dxn1", kchunk=PC, tm=1024)
    dh1, da0, dg_pre1, dg_post0 = mid_bwd(dxn1, dh2, h1, a0, norm_pre[1:2], norm_post[0:1])

    dys, dz0 = mm_dya(da0, w_out, z0, ys[0])
    g_w_out = mm_wgrad_rows(yat, da0, "mm_dw_out")
    g_w_in = mm_dw_in_part(xn0t, dz0, 3 * seg_tiles, None, "mm_dw_in_z")
    dPs = []
    for g, d in enumerate(DIL):
        dP = attn_bwd(Ps[g], dys[g], ys[g], lse3[g], tabs[g], g, d)
        g_w_in = mm_dw_in_part(xn0ts[g], dP, g * seg_tiles, g_w_in, f"mm_dw_in{g}")
        dPs.append(dP)

    cidx = ci.astype(jnp.int32).reshape(1)
    chip = (2 * xi + yi).astype(jnp.int32).reshape(1)
    fulls = [g_w_in, g_w_out.reshape(N_DEV, E // N_DEV, D), g_wp_in,
             g_wg.reshape(4, N_DEV, PC // N_DEV, PC).transpose(1, 0, 2, 3).reshape(N_DEV, 4 * PC // N_DEV, PC),
             g_wp_out.reshape(N_DEV, E // N_DEV, D)]
    pair_flight, pair_token = split_start(
        "rs_pair_start", fulls, [lax.empty((4,) + f.shape[1:], F32) for f in fulls], _pair_plan(), 4)
    dx_z = mm_dx_part(dz0, w_in8, 3 * seg_tiles, "mm_dxn0_z", after=pair_token)
    dx_0 = mm_dx_part(dPs[0], w_in8, 0, "mm_dxn0_0", after=dx_z)
    fulls, sibs = split_wait("rs_pair_wait", pair_flight, _pair_plan(), dx_0)
    parts = [pair_add(f, s, cidx, f"pair_add{k}") for k, (f, s) in enumerate(zip(fulls, sibs))]
    chips_flight, chips_token = split_start(
        "rs_chips_start", parts, [jnp.zeros(p.shape, BF16) for p in parts], _chips_plan(), 3)
    dx_1 = mm_dx_part(dPs[1], w_in8, seg_tiles, "mm_dxn0_1", after=chips_token)
    dx_2 = mm_dx_part(dPs[2], w_in8, 2 * seg_tiles, "mm_dxn0_2", after=dx_1)
    grad_x, dg_pre0 = pre0_bwd(dx_0, dx_z, dx_1, dx_2, dh1, x2, norm_pre[0:1])
    parts, recvs = split_wait("rs_chips_wait", chips_flight, _chips_plan(), grad_x)
    shards = [(attn_w_in, m_attn_w_in, v_attn_w_in), (attn_w_out, m_attn_w_out, v_attn_w_out),
              (pool_w_in, m_pool_w_in, v_pool_w_in), (pool_w_grp, m_pool_w_grp, v_pool_w_grp),
              (pool_w_out, m_pool_w_out, v_pool_w_out)]
    big = []
    for k, (recv, part, (w, m, v)) in enumerate(zip(recvs, parts, shards)):
        shp = w.shape
        r2 = recv.shape[1:]
        res = adamw_sum(recv, part, chip, w.reshape(r2), m.reshape(r2), v.reshape(r2), f"adamw{k}")
        big.append([t.reshape(shp) for t in res])

    smalls = jnp.concatenate([dg_pre0.sum(0, keepdims=True), dg_pre1.sum(0, keepdims=True),
                              dg_post0.sum(0, keepdims=True), dg_post1.sum(0, keepdims=True),
                              db_p.sum(0).reshape(2, D), dscale_p.sum(0).reshape(2, D)], axis=0)
    (smalls8,) = all_gather([smalls], "gather_small_grads")
    tot = sum_slots(smalls8, "sum_small_grads")
    g_norm_pre, g_norm_post = tot[0:2], tot[2:4]
    g_b = lax.dynamic_slice_in_dim(tot[4:6].reshape(4, PC), dev * (PC // N_DEV), PC // N_DEV, axis=1)[None]
    g_scale = lax.dynamic_slice_in_dim(tot[6:8].reshape(1, E), dev * (E // N_DEV), E // N_DEV, axis=1)
    sm = [adamw_small(g_norm_pre, norm_pre, m_norm_pre, v_norm_pre, "adamw_norm_pre"),
          adamw_small(g_norm_post, norm_post, m_norm_post, v_norm_post, "adamw_norm_post"),
          adamw_small(g_b, pool_b_grp, m_pool_b_grp, v_pool_b_grp, "adamw_b"),
          adamw_small(g_scale, pool_scale, m_pool_scale, v_pool_scale, "adamw_scale")]

    grads = [g_norm_pre, g_norm_post, big[0][0], big[1][0], big[2][0], big[3][0], g_b, g_scale, big[4][0]]

    def pick(k):
        return [sm[0][k - 1], sm[1][k - 1], big[0][k], big[1][k], big[2][k], big[3][k], sm[2][k - 1], sm[3][k - 1],
                big[4][k]]

    return (loss, grad_x[None], *grads, *pick(1), *pick(2), *pick(3))
```

```python
import functools
import math

import numpy as np
import jax
import jax.numpy as jnp
from jax import lax
from jax.experimental import pallas as pl
from jax.experimental.pallas import tpu as pltpu

F32 = jnp.float32
BF16 = jnp.bfloat16
SDS = jax.ShapeDtypeStruct

N_DEV = 8
D = 1024
E = 2048
HD = 128
NH = E // HD
DIL = (1, 4, 16)
QB = 128
SEG = 3 * E
W_IN_COLS = 3 * SEG + E
W_SHARD = W_IN_COLS // N_DEV
CT = 512
POOL_WIN = (2, 4, 8, 16)
PC = E // 4
EPS = 1e-6
NEG = -1e30
SCALE = 1.0 / math.sqrt(HD)
LR, B1, B2, ADAM_EPS, WD, STEP = 0.001, 0.9, 0.999, 1e-08, 0.01, 10
MIB = 1024 * 1024
ANY = pl.BlockSpec(memory_space=pl.ANY)
MESH = pl.DeviceIdType.MESH


def _cp(sem, mb):
    return pltpu.CompilerParams(dimension_semantics=sem, vmem_limit_bytes=mb * MIB)


def _dot(a, b):
    return jnp.dot(a, b, preferred_element_type=F32)


def _dot_nt(a, b):
    return lax.dot_general(a, b, (((1,), (1,)), ((), ())), preferred_element_type=F32)


def _rms(h):
    return lax.rsqrt(jnp.mean(h * h, axis=-1, keepdims=True) + EPS)


def _row_tile(R, C, budget):
    tr = R
    while tr * C * 4 > budget and tr % 16 == 0:
        tr //= 2
    return tr


def _fold8(t):
    return t.reshape(t.shape[0] // 8, 8, t.shape[1]).sum(axis=0)


def _sigmoid(z):
    return 1.0 / (1.0 + jnp.exp(-z))


LANES = 128


def _scr(rows, C):
    return pltpu.VMEM((C // LANES, rows, LANES), F32)


def _scr_put(scr, val):
    for c in range(scr.shape[0]):
        scr[c] = val[:, c * LANES:(c + 1) * LANES]


def _scr_get(scr):
    return jnp.concatenate([scr[c] for c in range(scr.shape[0])], axis=1)


def _store_perm(dst_ref, scr, d):
    n = dst_ref.shape[1]
    for r in range(d):
        for c in range(scr.shape[0]):
            dst_ref[r, :, c * LANES:(c + 1) * LANES] = scr[c, pl.ds(r, n, stride=d), :].astype(dst_ref.dtype)


def _load_perm(scr, src_ref, d, add=False):
    n = src_ref.shape[1]
    for r in range(d):
        rows = pl.ds(r, n, stride=d)
        for c in range(scr.shape[0]):
            v = src_ref[r, :, c * LANES:(c + 1) * LANES].astype(F32)
            scr[c, rows, :] = scr[c, rows, :] + v if add else v


def _rope(t, c, s1, s2):
    t = t.astype(BF16)
    return t * c + pltpu.roll(t, HD - 16, 1) * s1 + pltpu.roll(t, 16, 1) * s2


def _unrope(t, c, s1, s2):
    t = t.astype(BF16)
    return t * c - pltpu.roll(t, HD - 16, 1) * s1 - pltpu.roll(t, 16, 1) * s2


def _mesh_pos():
    return lax.axis_index("x"), lax.axis_index("y"), lax.axis_index("c")


def all_gather(arrs, name):
    n = len(arrs)

    def body(*refs):
        ins, outs = refs[:n], refs[n:2 * n]
        send_sems, recv_sems, local_sems = refs[2 * n:]
        x, y, c = _mesh_pos()
        me, sib = (x, y, c), (x, y, 1 - c)
        chips = [(1 - x, y), (x, 1 - y), (1 - x, 1 - y)]

        def slot(p):
            return 4 * p[0] + 2 * p[1] + p[2]

        def copy(a, k, block, to, src=None):
            dst = outs[a].at[slot(block)]
            return pltpu.make_async_remote_copy(
                src_ref=dst if src is None else src, dst_ref=dst,
                send_sem=send_sems.at[a, k], recv_sem=recv_sems.at[a, k],
                device_id=to, device_id_type=MESH)

        mine = [pltpu.make_async_copy(ins[a], outs[a].at[slot(me)], local_sems.at[a]) for a in range(n)]
        for cp in mine:
            cp.start()
        first = []
        for a in range(n):
            first.append(copy(a, 0, me, sib, src=ins[a]))
            for j, chip in enumerate(chips):
                first.append(copy(a, 1 + j, me, (*chip, c), src=ins[a]))
        for cp in first:
            cp.start()
        passed = []
        for j, chip in enumerate(chips):
            for a in range(n):
                copy(a, 1 + j, (*chip, c), me).wait_recv()
                fw = copy(a, 4 + j, (*chip, c), sib)
                fw.start()
                passed.append(fw)
        for a in range(n):
            copy(a, 0, sib, me).wait_recv()
        for j, chip in enumerate(chips):
            for a in range(n):
                copy(a, 4 + j, (*chip, 1 - c), me).wait_recv()
        for cp in first + passed:
            cp.wait_send()
        for cp in mine:
            cp.wait()

    return pl.pallas_call(
        body, name=name,
        out_shape=[SDS((N_DEV,) + a.shape, a.dtype) for a in arrs],
        in_specs=[ANY] * n, out_specs=[ANY] * n,
        scratch_shapes=[pltpu.SemaphoreType.DMA((n, 7)), pltpu.SemaphoreType.DMA((n, 7)),
                        pltpu.SemaphoreType.DMA((n,))],
    )(*arrs)


HBM_SPEC = pl.BlockSpec(memory_space=pltpu.HBM)
SEM_SPEC = pl.BlockSpec(memory_space=pltpu.SEMAPHORE)
EFFECT = pltpu.SideEffectType.DATAFLOW_SIDE_EFFECTING


def _pair_plan():
    def plan(x, y, c):
        return [(2 * q + (1 - c), q, (x, y, 1 - c)) for q in range(4)]
    return plan


def _chips_plan():
    def plan(x, y, c):
        chips = [(1 - x, y), (x, 1 - y), (1 - x, 1 - y)]
        return [(2 * cx + cy, 2 * x + y, (cx, cy, c)) for cx, cy in chips]
    return plan


def _peers_plan():
    def plan(x, y, c):
        out = []
        for k in range(1, N_DEV):
            fx, fy, fc = (k >> 2) & 1, (k >> 1) & 1, k & 1
            px, py, pc = (x + fx) % 2, (y + fy) % 2, (c + fc) % 2
            out.append((None, 4 * x + 2 * y + c, (px, py, pc)))
        return out
    return plan


def _split_copies(plan, srcs, lands, send_sems, recv_sems):
    x, y, c = _mesh_pos()
    cps = []
    for a, (src, land) in enumerate(zip(srcs, lands)):
        steps = plan(x, y, c)
        for k, (si, li, to) in enumerate(steps):
            sem = a * len(steps) + k
            cps.append(pltpu.make_async_remote_copy(
                src_ref=src if si is None else src.at[si], dst_ref=land.at[li],
                send_sem=send_sems.at[sem], recv_sem=recv_sems.at[sem],
                device_id=to, device_id_type=MESH))
    return cps


def split_start(name, srcs, lands, plan, nk):
    n = len(srcs)

    def body(*refs):
        send_sems, recv_sems = refs[2 * n], refs[2 * n + 1]
        token = refs[-1]
        for cp in _split_copies(plan, refs[:n], refs[n:2 * n], send_sems, recv_sems):
            cp.start()
        token[...] = jnp.zeros_like(token)

    ops = [pltpu.with_memory_space_constraint(a, pltpu.HBM) for a in list(srcs) + list(lands)]
    res = pl.pallas_call(
        body, name=name,
        out_shape=(pltpu.SemaphoreType.DMA((n * nk,)), pltpu.SemaphoreType.DMA((n * nk,)),
                   *[pltpu.HBM(a.shape, a.dtype) for a in ops], SDS((8, 128), F32)),
        in_specs=[HBM_SPEC] * (2 * n),
        out_specs=(SEM_SPEC, SEM_SPEC, *[HBM_SPEC] * (2 * n), pl.BlockSpec(memory_space=pltpu.VMEM)),
        input_output_aliases={i: 2 + i for i in range(2 * n)},
        compiler_params=pltpu.CompilerParams(has_side_effects=EFFECT),
    )(*ops)
    return res[:-1], res[-1]


def split_wait(name, flight, plan, after):
    send_sems, recv_sems = flight[0], flight[1]
    bufs = list(flight[2:])
    n = len(bufs) // 2

    def body(*refs):
        for cp in _split_copies(plan, refs[:n], refs[n:2 * n], refs[2 * n], refs[2 * n + 1]):
            cp.wait_send()
            cp.wait_recv()

    res = pl.pallas_call(
        body, name=name,
        out_shape=[pltpu.HBM(a.shape, a.dtype) for a in bufs],
        in_specs=[HBM_SPEC] * (2 * n) + [SEM_SPEC, SEM_SPEC, ANY],
        out_specs=[HBM_SPEC] * (2 * n),
        input_output_aliases={i: i for i in range(2 * n)},
        compiler_params=pltpu.CompilerParams(has_side_effects=EFFECT),
    )(*bufs, send_sems, recv_sems, after)
    return res[:n], res[n:]


def rs_pair(arrs, name):
    n = len(arrs)

    def body(*refs):
        ins, outs = refs[:n], refs[n:2 * n]
        send_sems, recv_sems = refs[2 * n:]
        x, y, c = _mesh_pos()
        cps = []
        for a in range(n):
            for q in range(4):
                cps.append(pltpu.make_async_remote_copy(
                    src_ref=ins[a].at[2 * q + (1 - c)], dst_ref=outs[a].at[q],
                    send_sem=send_sems.at[a, q], recv_sem=recv_sems.at[a, q],
                    device_id=(x, y, 1 - c), device_id_type=MESH))
        for cp in cps:
            cp.start()
        for cp in cps:
            cp.wait()

    return pl.pallas_call(
        body, name=name,
        out_shape=[SDS((4,) + a.shape[1:], a.dtype) for a in arrs],
        in_specs=[ANY] * n, out_specs=[ANY] * n,
        scratch_shapes=[pltpu.SemaphoreType.DMA((n, 4)), pltpu.SemaphoreType.DMA((n, 4))],
    )(*arrs)


def rs_chips(parts, name):
    n = len(parts)

    def body(*refs):
        ins, outs = refs[:n], refs[n:2 * n]
        send_sems, recv_sems, local_sems = refs[2 * n:]
        x, y, c = _mesh_pos()
        mychip = 2 * x + y
        chips = [(1 - x, y), (x, 1 - y), (1 - x, 1 - y)]
        mine = [pltpu.make_async_copy(ins[a].at[mychip], outs[a].at[mychip], local_sems.at[a]) for a in range(n)]
        for cp in mine:
            cp.start()
        cps = []
        for a in range(n):
            for j, chip in enumerate(chips):
                q = 2 * chip[0] + chip[1]
                cps.append(pltpu.make_async_remote_copy(
                    src_ref=ins[a].at[q], dst_ref=outs[a].at[mychip],
                    send_sem=send_sems.at[a, j], recv_sem=recv_sems.at[a, j],
                    device_id=(*chip, c), device_id_type=MESH))
        for cp in cps:
            cp.start()
        for cp in cps:
            cp.wait()
        for cp in mine:
            cp.wait()

    return pl.pallas_call(
        body, name=name,
        out_shape=[SDS(a.shape, a.dtype) for a in parts],
        in_specs=[ANY] * n, out_specs=[ANY] * n,
        scratch_shapes=[pltpu.SemaphoreType.DMA((n, 3)), pltpu.SemaphoreType.DMA((n, 3)),
                        pltpu.SemaphoreType.DMA((n,))],
    )(*parts)


def pair_add(full, sib, cidx, name):
    _, R, C = full.shape
    tr = _row_tile(R, C, MIB)

    def body(c_ref, a_ref, b_ref, o_ref):
        o_ref[...] = (a_ref[...] + b_ref[...]).astype(BF16)

    return pl.pallas_call(
        body, name=name,
        grid_spec=pltpu.PrefetchScalarGridSpec(
            num_scalar_prefetch=1, grid=(4, R // tr),
            in_specs=[pl.BlockSpec((None, tr, C), lambda q, i, cr: (2 * q + cr[0], i, 0)),
                      pl.BlockSpec((None, tr, C), lambda q, i, cr: (q, i, 0))],
            out_specs=pl.BlockSpec((None, tr, C), lambda q, i, cr: (q, i, 0))),
        out_shape=SDS((4, R, C), BF16),
        compiler_params=_cp(("parallel", "parallel"), 32),
    )(cidx, full, sib)


def _adam_math(w, g, m, v):
    m2 = B1 * m + (1.0 - B1) * g
    v2 = B2 * v + (1.0 - B2) * (g * g)
    m_hat = m2 / (1.0 - B1 ** STEP)
    v_hat = v2 / (1.0 - B2 ** STEP)
    delta = -LR * (m_hat / (jnp.sqrt(v_hat) + ADAM_EPS) + WD * w)
    return delta, m2, v2


def adamw_sum(recv, part, chip, w, m, v, name):
    K, R, C = recv.shape
    tr = _row_tile(R, C, MIB)

    def body(chip_ref, r_ref, p_ref, w_ref, m_ref, v_ref, g_ref, d_ref, m2_ref, v2_ref):
        g = r_ref[0].astype(F32)
        for k in range(1, K):
            g = g + r_ref[k].astype(F32)
        g = g + p_ref[...].astype(F32)
        delta, m2, v2 = _adam_math(w_ref[...], g, m_ref[...], v_ref[...])
        g_ref[...] = g
        d_ref[...] = delta
        m2_ref[...] = m2
        v2_ref[...] = v2

    tile = pl.BlockSpec((tr, C), lambda i, cr: (i, 0))
    return pl.pallas_call(
        body, name=name,
        grid_spec=pltpu.PrefetchScalarGridSpec(
            num_scalar_prefetch=1, grid=(R // tr,),
            in_specs=[pl.BlockSpec((K, tr, C), lambda i, cr: (0, i, 0)),
                      pl.BlockSpec((None, tr, C), lambda i, cr: (cr[0], i, 0)), tile, tile, tile],
            out_specs=[tile] * 4),
        out_shape=[SDS((R, C), F32)] * 4,
        compiler_params=_cp(("parallel",), 32),
    )(chip, recv, part, w, m, v)


def adamw_small(g, w, m, v, name):
    def body(g_ref, w_ref, m_ref, v_ref, d_ref, m2_ref, v2_ref):
        delta, m2, v2 = _adam_math(w_ref[...], g_ref[...], m_ref[...], v_ref[...])
        d_ref[...] = delta
        m2_ref[...] = m2
        v2_ref[...] = v2

    return pl.pallas_call(body, name=name, out_shape=[SDS(w.shape, F32)] * 3)(g, w, m, v)


def sum_slots(a, name):
    K = a.shape[0]

    def body(a_ref, o_ref):
        t = a_ref[0]
        for k in range(1, K):
            t = t + a_ref[k]
        o_ref[...] = t

    return pl.pallas_call(body, name=name, out_shape=SDS(a.shape[1:], F32))(a)


def norm_pre0(x, g):
    S = x.shape[0]
    ts = 512

    def body(x_ref, g_ref, o_ref, o4_ref, o16_ref, ot_ref, scr):
        h = x_ref[...]
        xn = h * _rms(h) * g_ref[...]
        o_ref[...] = xn.astype(BF16)
        ot_ref[...] = xn.T.astype(BF16)
        _scr_put(scr, xn)
        _store_perm(o4_ref, scr, 4)
        _store_perm(o16_ref, scr, 16)

    return pl.pallas_call(
        body, name="norm_pre0", grid=(S // ts,),
        in_specs=[pl.BlockSpec((ts, D), lambda i: (i, 0)), pl.BlockSpec((1, D), lambda i: (0, 0))],
        out_specs=[pl.BlockSpec((ts, D), lambda i: (i, 0)),
                   pl.BlockSpec((4, ts // 4, D), lambda i: (0, i, 0)),
                   pl.BlockSpec((16, ts // 16, D), lambda i: (0, i, 0)),
                   pl.BlockSpec((D, ts), lambda i: (0, i))],
        out_shape=[SDS((S, D), BF16), SDS((4, S // 4, D), BF16), SDS((16, S // 16, D), BF16), SDS((D, S), BF16)],
        scratch_shapes=[_scr(ts, D)],
        compiler_params=_cp(("parallel",), 32),
    )(x, g)


def transpose_rows(a, name):
    S, C = a.shape
    ts = 512

    def body(a_ref, o_ref):
        o_ref[...] = a_ref[...].astype(F32).T.astype(BF16)

    return pl.pallas_call(
        body, name=name, grid=(S // ts,),
        in_specs=[pl.BlockSpec((ts, C), lambda i: (i, 0))],
        out_specs=pl.BlockSpec((C, ts), lambda i: (0, i)),
        out_shape=SDS((C, S), BF16),
        compiler_params=_cp(("parallel",), 32),
    )(a)


def post0_pre1(x, a0, g_post, g_pre):
    S = x.shape[0]
    ts = 512

    def body(x_ref, a_ref, gp_ref, gn_ref, h_ref, o_ref, ot_ref):
        a = a_ref[...]
        h1 = x_ref[...] + a * _rms(a) * gp_ref[...]
        h_ref[...] = h1
        xn = h1 * _rms(h1) * gn_ref[...]
        o_ref[...] = xn.astype(BF16)
        ot_ref[...] = xn.T.astype(BF16)

    row = pl.BlockSpec((ts, D), lambda i: (i, 0))
    vec = pl.BlockSpec((1, D), lambda i: (0, 0))
    return pl.pallas_call(
        body, name="post0_pre1", grid=(S // ts,),
        in_specs=[row, row, vec, vec],
        out_specs=[row, row, pl.BlockSpec((D, ts), lambda i: (0, i))],
        out_shape=[SDS((S, D), F32), SDS((S, D), BF16), SDS((D, S), BF16)],
        compiler_params=_cp(("parallel",), 40),
    )(x, a0, g_post, g_pre)


def post1_loss(h1, a1, target, g_post):
    S = h1.shape[0]
    ts = 512

    def body(h_ref, a_ref, t_ref, g_ref, dh_ref, da_ref, loss_ref, dg_ref):
        @pl.when(pl.program_id(0) == 0)
        def _():
            loss_ref[...] = jnp.zeros_like(loss_ref)
            dg_ref[...] = jnp.zeros_like(dg_ref)

        a = a_ref[...]
        g = g_ref[...]
        rp = _rms(a)
        yhat = a * rp
        e = h_ref[...] + yhat * g - t_ref[...]
        loss_ref[...] += _fold8(e * e)
        dh = e * (1.0 / D)
        dh_ref[...] = dh
        dg_ref[...] += _fold8(dh * yhat)
        dyh = dh * g
        da = rp * (dyh - yhat * jnp.mean(dyh * yhat, axis=-1, keepdims=True))
        da_ref[...] = da.astype(BF16)

    row = pl.BlockSpec((ts, D), lambda i: (i, 0))
    acc = pl.BlockSpec((8, D), lambda i: (0, 0))
    return pl.pallas_call(
        body, name="post1_loss", grid=(S // ts,),
        in_specs=[row, row, row, pl.BlockSpec((1, D), lambda i: (0, 0))],
        out_specs=[row, row, acc, acc],
        out_shape=[SDS((S, D), F32), SDS((S, D), BF16), SDS((8, D), F32), SDS((8, D), F32)],
        compiler_params=_cp(("arbitrary",), 40),
    )(h1, a1, target, g_post)


def mid_bwd(dxn1, dh2, h1, a0, g_pre1, g_post0):
    S = h1.shape[0]
    ts = 512

    def body(dx_ref, dh2_ref, h_ref, a_ref, gn_ref, gp_ref, dh1_ref, da_ref, dgn_ref, dgp_ref):
        @pl.when(pl.program_id(0) == 0)
        def _():
            dgn_ref[...] = jnp.zeros_like(dgn_ref)
            dgp_ref[...] = jnp.zeros_like(dgp_ref)

        h = h_ref[...]
        r1 = _rms(h)
        xhat = h * r1
        dxn = dx_ref[...]
        dgn_ref[...] += _fold8(dxn * xhat)
        dxh = dxn * gn_ref[...]
        dh1 = dh2_ref[...] + r1 * (dxh - xhat * jnp.mean(dxh * xhat, axis=-1, keepdims=True))
        dh1_ref[...] = dh1
        a = a_ref[...]
        rp = _rms(a)
        yhat = a * rp
        dgp_ref[...] += _fold8(dh1 * yhat)
        dyh = dh1 * gp_ref[...]
        da = rp * (dyh - yhat * jnp.mean(dyh * yhat, axis=-1, keepdims=True))
        da_ref[...] = da.astype(BF16)

    row = pl.BlockSpec((ts, D), lambda i: (i, 0))
    vec = pl.BlockSpec((1, D), lambda i: (0, 0))
    acc = pl.BlockSpec((8, D), lambda i: (0, 0))
    return pl.pallas_call(
        body, name="mid_bwd", grid=(S // ts,),
        in_specs=[row, row, row, row, vec, vec],
        out_specs=[row, row, acc, acc],
        out_shape=[SDS((S, D), F32), SDS((S, D), BF16), SDS((8, D), F32), SDS((8, D), F32)],
        compiler_params=_cp(("arbitrary",), 48),
    )(dxn1, dh2, h1, a0, g_pre1, g_post0)


def pre0_bwd(dx_tok, dx_z, dx4, dx16, dh1, x, g_pre0):
    S = x.shape[0]
    ts = 512

    def body(da_ref, dz_ref, d4_ref, d16_ref, dh_ref, x_ref, g_ref, gx_ref, dg_ref, scr):
        @pl.when(pl.program_id(0) == 0)
        def _():
            dg_ref[...] = jnp.zeros_like(dg_ref)

        _scr_put(scr, da_ref[...] + dz_ref[...])
        _load_perm(scr, d4_ref, 4, add=True)
        _load_perm(scr, d16_ref, 16, add=True)
        h = x_ref[...]
        r = _rms(h)
        xhat = h * r
        dxn = _scr_get(scr)
        dg_ref[...] += _fold8(dxn * xhat)
        dxh = dxn * g_ref[...]
        gx_ref[...] = dh_ref[...] + r * (dxh - xhat * jnp.mean(dxh * xhat, axis=-1, keepdims=True))

    row = pl.BlockSpec((ts, D), lambda i: (i, 0))
    return pl.pallas_call(
        body, name="pre0_bwd", grid=(S // ts,),
        in_specs=[row, row, pl.BlockSpec((4, ts // 4, D), lambda i: (0, i, 0)),
                  pl.BlockSpec((16, ts // 16, D), lambda i: (0, i, 0)), row, row,
                  pl.BlockSpec((1, D), lambda i: (0, 0))],
        out_specs=[row, pl.BlockSpec((8, D), lambda i: (0, 0))],
        out_shape=[SDS((S, D), F32), SDS((8, D), F32)],
        scratch_shapes=[_scr(ts, D)],
        compiler_params=_cp(("arbitrary",), 48),
    )(dx_tok, dx_z, dx4.reshape(4, S // 4, D), dx16.reshape(16, S // 16, D), dh1, x, g_pre0)


def _w_tile(tile0):
    per = W_SHARD // CT
    return lambda t: ((tile0 + t) // per, 0, (tile0 + t) % per)


def mm_in(xn, w8, tile0, ntiles, tabs, name, after=None):
    S = xn.shape[0]
    tm = 2048
    wt = _w_tile(tile0)

    def body(a_ref, b_ref, *rest):
        o_ref = rest[-1]
        rc = 512
        for u in range(tm // rc):
            rows = slice(u * rc, (u + 1) * rc)
            r = _dot(a_ref[rows, :], b_ref[...])
            if tabs is None:
                o_ref[rows, :] = r.astype(BF16)
                continue
            c_ref, s1_ref, s2_ref = rest[:3]
            rot = pl.program_id(1) < 2 * E // CT
            c = jnp.where(rot, c_ref[rows, :], 1.0)
            s1 = jnp.where(rot, s1_ref[rows, :], 0.0)
            s2 = jnp.where(rot, s2_ref[rows, :], 0.0)
            for hh in range(CT // HD):
                cs = slice(hh * HD, (hh + 1) * HD)
                o_ref[rows, cs] = _rope(r[:, cs], c, s1, s2).astype(BF16)

    tab = pl.BlockSpec((tm, HD), lambda i, t: (i, 0))
    return pl.pallas_call(
        body, name=name, grid=(S // tm, ntiles),
        in_specs=[pl.BlockSpec((tm, D), lambda i, t: (i, 0)),
                  pl.BlockSpec((None, D, CT), lambda i, t: wt(t))] + ([] if tabs is None else [tab] * 3)
        + ([] if after is None else [ANY]),
        out_specs=pl.BlockSpec((tm, CT), lambda i, t: (i, t)),
        out_shape=SDS((S, ntiles * CT), BF16),
        compiler_params=_cp(("parallel", "parallel"), 48),
    )(xn, w8, *(() if tabs is None else tabs), *(() if after is None else (after,)))


def mm_rows(a, b, name, out_dtype, tm=1024):
    M, K = a.shape
    N = b.shape[1]

    def body(a_ref, b_ref, o_ref):
        for cidx in range(N // 256):
            col = slice(cidx * 256, (cidx + 1) * 256)
            o_ref[:, col] = _dot(a_ref[...], b_ref[:, col]).astype(out_dtype)

    return pl.pallas_call(
        body, name=name, grid=(M // tm,),
        in_specs=[pl.BlockSpec((tm, K), lambda i: (i, 0)), pl.BlockSpec((K, N), lambda i: (0, 0))],
        out_specs=pl.BlockSpec((tm, N), lambda i: (i, 0)),
        out_shape=SDS((M, N), out_dtype),
        compiler_params=_cp(("parallel",), 48),
    )(a, b)


def mm_uz(xn, w8):
    S = xn.shape[0]
    tm = 512
    bw = w8.shape[2]

    def body(a_ref, b_ref, o_ref):
        for dv in range(N_DEV):
            o_ref[:, dv * bw:(dv + 1) * bw] = _dot(a_ref[...], b_ref[dv])

    return pl.pallas_call(
        body, name="mm_uz", grid=(S // tm,),
        in_specs=[pl.BlockSpec((tm, D), lambda i: (i, 0)), pl.BlockSpec((N_DEV, D, bw), lambda i: (0, 0, 0))],
        out_specs=pl.BlockSpec((tm, N_DEV * bw), lambda i: (i, 0)),
        out_shape=SDS((S, N_DEV * bw), F32),
        compiler_params=_cp(("parallel",), 48),
    )(xn, w8)


def mm_acc(a, b, name, *, grid, a_spec, b_spec, o_spec, o_shape, acc_shape, write, vmem=48):
    nk = grid[-1]

    def body(a_ref, b_ref, o_ref, acc_ref):
        k = pl.program_id(len(grid) - 1)

        @pl.when(k == 0)
        def _():
            acc_ref[...] = jnp.zeros_like(acc_ref)

        acc_ref[...] += _dot(a_ref[...], b_ref[...])

        @pl.when(k == nk - 1)
        def _():
            write(o_ref, acc_ref)

    return pl.pallas_call(
        body, name=name, grid=grid, in_specs=[a_spec, b_spec], out_specs=o_spec, out_shape=o_shape,
        scratch_shapes=[pltpu.VMEM(acc_shape, F32)],
        compiler_params=_cp(("parallel",) * (len(grid) - 1) + ("arbitrary",), vmem),
    )(a, b)


def _write_plain(o_ref, acc_ref):
    o_ref[...] = acc_ref[...]


def mm_wgrad_rows(at, b, name):
    M, S = at.shape
    N = b.shape[1]
    tm, tk = 1024, 1024
    return mm_acc(at, b, name, grid=(M // tm, S // tk),
                  a_spec=pl.BlockSpec((tm, tk), lambda i, k: (i, k)),
                  b_spec=pl.BlockSpec((tk, N), lambda i, k: (k, 0)),
                  o_spec=pl.BlockSpec((tm, N), lambda i, k: (i, 0)),
                  o_shape=SDS((M, N), F32), acc_shape=(tm, N), write=_write_plain)


def mm_wgrad_cols(at, b, name, *, ncols, shard, tn):
    M, S = at.shape
    tk = 1024
    per = shard // tn if tn <= shard else 1
    nb = max(1, tn // shard)

    if tn <= shard:
        o_spec = pl.BlockSpec((None, M, tn), lambda t, k: (t // per, 0, t % per))
        write = _write_plain
    else:
        o_spec = pl.BlockSpec((nb, M, shard), lambda t, k: (t, 0, 0))

        def write(o_ref, acc_ref):
            for u in range(nb):
                o_ref[u] = acc_ref[:, u * shard:(u + 1) * shard]

    return mm_acc(at, b, name, grid=(ncols // tn, S // tk),
                  a_spec=pl.BlockSpec((M, tk), lambda t, k: (0, k)),
                  b_spec=pl.BlockSpec((tk, tn), lambda t, k: (k, t)),
                  o_spec=o_spec, o_shape=SDS((N_DEV, M, shard), F32), acc_shape=(M, tn), write=write)


def mm_dwg(pooled_t, dh):
    S = dh.shape[0]
    tk = 2048
    return mm_acc(pooled_t, dh, "mm_dwg", grid=(4, S // tk),
                  a_spec=pl.BlockSpec((PC, tk), lambda g, k: (g, k)),
                  b_spec=pl.BlockSpec((tk, PC), lambda g, k: (k, g)),
                  o_spec=pl.BlockSpec((None, PC, PC), lambda g, k: (g, 0, 0)),
                  o_shape=SDS((4, PC, PC), F32), acc_shape=(PC, PC), write=_write_plain)


def mm_dx_full(da, w, name):
    S, K = da.shape
    N = w.shape[0]
    tm = 512

    def body(a_ref, b_ref, o_ref):
        o_ref[...] = _dot_nt(a_ref[...], b_ref[...])

    return pl.pallas_call(
        body, name=name, grid=(S // tm,),
        in_specs=[pl.BlockSpec((tm, K), lambda i: (i, 0)), pl.BlockSpec((N, K), lambda i: (0, 0))],
        out_specs=pl.BlockSpec((tm, N), lambda i: (i, 0)),
        out_shape=SDS((S, N), F32),
        compiler_params=_cp(("parallel",), 48),
    )(da, w)


def mm_dw_in_part(at, b, tile0, prev, name):
    M, S = at.shape
    ntiles = b.shape[1] // CT
    tk = 2048
    nk = S // tk
    wt = _w_tile(tile0)

    def body(a_ref, b_ref, *rest):
        o_ref, acc_ref = rest[-2:]
        k = pl.program_id(1)

        @pl.when(k == 0)
        def _():
            acc_ref[...] = jnp.zeros_like(acc_ref)

        acc_ref[...] += _dot(a_ref[...], b_ref[...])

        @pl.when(k == nk - 1)
        def _():
            o_ref[...] = acc_ref[...]

    return pl.pallas_call(
        body, name=name, grid=(ntiles, nk),
        in_specs=[pl.BlockSpec((M, tk), lambda t, k: (0, k)), pl.BlockSpec((tk, CT), lambda t, k: (k, t))]
        + ([] if prev is None else [ANY]),
        out_specs=pl.BlockSpec((None, M, CT), lambda t, k: wt(t)),
        out_shape=SDS((N_DEV, M, W_SHARD), F32),
        scratch_shapes=[pltpu.VMEM((M, CT), F32)],
        input_output_aliases={} if prev is None else {2: 0},
        compiler_params=_cp(("parallel", "arbitrary"), 48),
    )(at, b, *(() if prev is None else (prev,)))


def mm_dx_part(da, w8, tile0, name, after=None):
    S = da.shape[0]
    ntiles = da.shape[1] // CT
    tm = 2048
    wt = _w_tile(tile0)

    def body(a_ref, b_ref, *rest):
        o_ref, acc_ref = rest[-2:]
        t = pl.program_id(1)

        @pl.when(t == 0)
        def _():
            acc_ref[...] = jnp.zeros_like(acc_ref)

        acc_ref[...] += _dot_nt(a_ref[...], b_ref[...])

        @pl.when(t == ntiles - 1)
        def _():
            o_ref[...] = acc_ref[...]

    return pl.pallas_call(
        body, name=name, grid=(S // tm, ntiles),
        in_specs=[pl.BlockSpec((tm, CT), lambda i, t: (i, t)), pl.BlockSpec((None, D, CT), lambda i, t: wt(t))]
        + ([] if after is None else [ANY]),
        out_specs=pl.BlockSpec((tm, D), lambda i, t: (i, 0)),
        out_shape=SDS((S, D), F32),
        scratch_shapes=[pltpu.VMEM((tm, D), F32)],
        compiler_params=_cp(("parallel", "arbitrary"), 56),
    )(da, w8, *(() if after is None else (after,)))


def _band_masks(not_first):
    row = lax.broadcasted_iota(jnp.int32, (QB, QB), 0)
    col = lax.broadcasted_iota(jnp.int32, (QB, QB), 1)
    cur = jnp.where(col <= row, 0.0, NEG)
    prev = jnp.where(col >= row, 0.0, NEG)
    first = jnp.where(jnp.logical_and(col >= row, not_first), 0.0, NEG)
    return col, jnp.concatenate([prev, cur], axis=1), jnp.concatenate([first, cur], axis=1)


def _fill_kv(ext, qkv_ref, kh_ref, vh_ref):
    ext[0:QB, 0:E] = kh_ref[...]
    ext[0:QB, E:2 * E] = vh_ref[...]
    ext[QB:, :] = qkv_ref[:, E:3 * E]


def attn_fwd(P, g, d):
    S = P.shape[0]
    L = S // d
    T = min(512, L)
    nq = T // QB
    ni = L // T

    def body(qkv_ref, kh_ref, vh_ref, o_ref, lse_ref, ext):
        col, mask, mask_first = _band_masks(pl.program_id(1) > 0)
        lse_ref[...] = jnp.zeros_like(lse_ref)
        _fill_kv(ext, qkv_ref, kh_ref, vh_ref)

        def head(h, carry):
            off = pl.multiple_of(h * HD, HD)
            cq, cv = pl.ds(off, HD), pl.ds(E + off, HD)
            for j in range(nq):
                rows = slice(j * QB, (j + 1) * QB)
                krows = slice(j * QB, (j + 2) * QB)
                s = _dot_nt(qkv_ref[rows, cq], ext[krows, cq]) * SCALE + (mask_first if j == 0 else mask)
                m = jnp.max(s, axis=1, keepdims=True)
                p = jnp.exp(s - m)
                den = jnp.sum(p, axis=1, keepdims=True)
                o = _dot(p.astype(BF16), ext[krows, cv]) / den
                o_ref[rows, cq] = o.astype(BF16)
                lse_ref[rows, :] = jnp.where(col == h, m + jnp.log(den), lse_ref[rows, :])
            return carry

        lax.fori_loop(0, NH, head, 0, unroll=2)

    halo = lambda r, i: jnp.maximum(r * (L // QB) + i * nq - 1, 0)
    return pl.pallas_call(
        body, name=f"attn_fwd{g}", grid=(d, ni),
        in_specs=[pl.BlockSpec((T, SEG), lambda r, i: (r * ni + i, 0)),
                  pl.BlockSpec((QB, E), lambda r, i: (halo(r, i), 1)),
                  pl.BlockSpec((QB, E), lambda r, i: (halo(r, i), 2))],
        out_specs=[pl.BlockSpec((T, E), lambda r, i: (r * ni + i, 0)),
                   pl.BlockSpec((T, HD), lambda r, i: (r * ni + i, 0))],
        out_shape=[SDS((S, E), BF16), SDS((S, HD), F32)],
        scratch_shapes=[pltpu.VMEM((T + QB, 2 * E), BF16)],
        compiler_params=_cp(("parallel", "parallel"), 48),
    )(P, P, P)


def _perm_specs(ts, C):
    return [pl.BlockSpec((ts, C), lambda i: (i, 0)),
            pl.BlockSpec((4, ts // 4, C), lambda i: (0, i, 0)),
            pl.BlockSpec((16, ts // 16, C), lambda i: (0, i, 0))]


def _perm_shapes(S, C, dtype):
    return [SDS((S, C), dtype), SDS((4, S // 4, C), dtype), SDS((16, S // 16, C), dtype)]


def combine_fwd(os_, lses, z, ehot):
    S = z.shape[0]
    ts = 256

    def body(o0, o1, o2, l0, l1, l2, z_ref, e_ref, y0, y1, y2, s0, s1, s2, ya_ref, yat_ref,
             so1, so2, sl1, sl2, sy, sl):
        _load_perm(so1, o1, 4)
        _load_perm(so2, o2, 16)
        _load_perm(sl1, l1, 4)
        _load_perm(sl2, l2, 16)
        ls = [l0[...], sl1[0], sl2[0]]
        m = jnp.maximum(jnp.maximum(ls[0], ls[1]), ls[2])
        es = [jnp.exp(l - m) for l in ls]
        den = es[0] + es[1] + es[2]
        sl[0] = m + jnp.log(den)
        y = None
        for e, o in zip(es, (o0[...].astype(F32), _scr_get(so1), _scr_get(so2))):
            w = e / den
            hi = w.astype(BF16)
            lo = (w - hi.astype(F32)).astype(BF16)
            wb = _dot(hi, e_ref[...]) + _dot(lo, e_ref[...])
            y = wb * o if y is None else y + wb * o
        z = z_ref[...].astype(F32)
        ya = y * (z * _sigmoid(z))
        ya_ref[...] = ya.astype(BF16)
        yat_ref[...] = ya.T.astype(BF16)
        _scr_put(sy, y)
        y0[...] = y.astype(BF16)
        _store_perm(y1, sy, 4)
        _store_perm(y2, sy, 16)
        s0[...] = sl[0]
        _store_perm(s1, sl, 4)
        _store_perm(s2, sl, 16)

    wide = pl.BlockSpec((ts, E), lambda i: (i, 0))
    os3 = [os_[0], os_[1].reshape(4, S // 4, E), os_[2].reshape(16, S // 16, E)]
    ls3 = [lses[0], lses[1].reshape(4, S // 4, HD), lses[2].reshape(16, S // 16, HD)]
    res = pl.pallas_call(
        body, name="combine_fwd", grid=(S // ts,),
        in_specs=_perm_specs(ts, E) + _perm_specs(ts, HD) + [wide, pl.BlockSpec((HD, E), lambda i: (0, 0))],
        out_specs=_perm_specs(ts, E) + _perm_specs(ts, HD) + [wide, pl.BlockSpec((E, ts), lambda i: (0, i))],
        out_shape=_perm_shapes(S, E, BF16) + _perm_shapes(S, HD, F32) + [SDS((S, E), BF16), SDS((E, S), BF16)],
        scratch_shapes=[_scr(ts, E), _scr(ts, E), _scr(ts, HD), _scr(ts, HD), _scr(ts, E), _scr(ts, HD)],
        compiler_params=_cp(("parallel",), 56),
    )(*os3, *ls3, z, ehot)
    ys = [res[0], res[1].reshape(S, E), res[2].reshape(S, E)]
    lse3 = [res[3], res[4].reshape(S, HD), res[5].reshape(S, HD)]
    return ys, lse3, res[6], res[7]


def mm_dya(da0, w_out, z, y):
    S = da0.shape[0]
    tm = 512

    def body(a_ref, w_ref, z_ref, y_ref, dy0, dy1, dy2, dz_ref, scr):
        for cidx in range(E // 256):
            col = slice(cidx * 256, (cidx + 1) * 256)
            dya = _dot_nt(a_ref[...], w_ref[col, :])
            zz = z_ref[:, col].astype(F32)
            sig = _sigmoid(zz)
            dy = dya * zz * sig
            scr[2 * cidx] = dy[:, :LANES]
            scr[2 * cidx + 1] = dy[:, LANES:]
            dy0[:, col] = dy.astype(BF16)
            dz_ref[:, col] = (dya * y_ref[:, col].astype(F32) * sig * (1.0 + zz * (1.0 - sig))).astype(BF16)
        _store_perm(dy1, scr, 4)
        _store_perm(dy2, scr, 16)

    wide = pl.BlockSpec((tm, E), lambda i: (i, 0))
    res = pl.pallas_call(
        body, name="mm_dya", grid=(S // tm,),
        in_specs=[pl.BlockSpec((tm, D), lambda i: (i, 0)), pl.BlockSpec((E, D), lambda i: (0, 0)), wide, wide],
        out_specs=_perm_specs(tm, E) + [wide],
        out_shape=_perm_shapes(S, E, BF16) + [SDS((S, E), BF16)],
        scratch_shapes=[_scr(tm, E)],
        compiler_params=_cp(("parallel",), 48),
    )(da0, w_out, z, y)
    return [res[0], res[1].reshape(S, E), res[2].reshape(S, E)], res[3]


def attn_bwd(P, dy, y, lse, tabs, g, d):
    S = P.shape[0]
    L = S // d
    T = min(512, L)
    nq = T // QB
    ni = L // T

    def body(qkv_ref, kh_ref, vh_ref, dy_ref, y_ref, lse_ref, c_ref, s1_ref, s2_ref,
             o_ref, dkc_ref, dvc_ref, ext):
        i = pl.program_id(1)
        col, mask, mask_first = _band_masks(i < ni - 1)
        _fill_kv(ext, qkv_ref, kh_ref, vh_ref)

        @pl.when(i == 0)
        def _():
            dkc_ref[...] = jnp.zeros_like(dkc_ref)
            dvc_ref[...] = jnp.zeros_like(dvc_ref)

        def head(h, carry):
            off = pl.multiple_of(h * HD, HD)
            cq, ck, cv = pl.ds(off, HD), pl.ds(E + off, HD), pl.ds(2 * E + off, HD)
            pend_dk = dkc_ref[:, cq]
            pend_dv = dvc_ref[:, cq]
            for j in reversed(range(nq)):
                rows = slice(j * QB, (j + 1) * QB)
                krows = slice(j * QB, (j + 2) * QB)
                q = qkv_ref[rows, cq]
                k2 = ext[krows, cq]
                dyj = dy_ref[rows, cq]
                lse_h = jnp.sum(jnp.where(col == h, lse_ref[rows, :], 0.0), axis=1, keepdims=True)
                delta = jnp.sum(dyj.astype(F32) * y_ref[rows, cq].astype(F32), axis=1, keepdims=True)
                p = jnp.exp(_dot_nt(q, k2) * SCALE + (mask_first if j == 0 else mask) - lse_h)
                ds = p * (_dot_nt(dyj, ext[krows, ck]) - delta) * SCALE
                dq = _dot(ds.astype(BF16), k2)
                dk2 = _dot(ds.astype(BF16).T, q)
                dv2 = _dot(p.astype(BF16).T, dyj)
                dk = dk2[QB:] + pend_dk
                dv = dv2[QB:] + pend_dv
                pend_dk, pend_dv = dk2[:QB], dv2[:QB]
                c, s1, s2 = c_ref[rows, :], s1_ref[rows, :], s2_ref[rows, :]
                o_ref[rows, cq] = _unrope(dq, c, s1, s2).astype(BF16)
                o_ref[rows, ck] = _unrope(dk, c, s1, s2).astype(BF16)
                o_ref[rows, cv] = dv.astype(BF16)
            dkc_ref[:, cq] = pend_dk
            dvc_ref[:, cq] = pend_dv
            return carry

        lax.fori_loop(0, NH, head, 0, unroll=2)

    blk = lambda r, i: r * ni + ni - 1 - i
    halo = lambda r, i: jnp.maximum(r * (L // QB) + (ni - 1 - i) * nq - 1, 0)
    main = pl.BlockSpec((T, SEG), lambda r, i: (blk(r, i), 0))
    wide = pl.BlockSpec((T, E), lambda r, i: (blk(r, i), 0))
    narrow = pl.BlockSpec((T, HD), lambda r, i: (blk(r, i), 0))
    return pl.pallas_call(
        body, name=f"attn_bwd{g}", grid=(d, ni),
        in_specs=[main, pl.BlockSpec((QB, E), lambda r, i: (halo(r, i), 1)),
                  pl.BlockSpec((QB, E), lambda r, i: (halo(r, i), 2)),
                  wide, wide, narrow, narrow, narrow, narrow],
        out_specs=main, out_shape=SDS((S, SEG), BF16),
        scratch_shapes=[pltpu.VMEM((QB, E), F32), pltpu.VMEM((QB, E), F32), pltpu.VMEM((T + QB, 2 * E), BF16)],
        compiler_params=_cp(("arbitrary", "arbitrary"), 56),
    )(P, P, P, dy, y, lse, *tabs)


def _pool_cnt(t0, rows):
    t = (lax.broadcasted_iota(jnp.int32, (rows, E), 0) + t0 + 1).astype(F32)
    ch = lax.broadcasted_iota(jnp.int32, (rows, E), 1)
    w = jnp.where(ch < PC, 2.0, jnp.where(ch < 2 * PC, 4.0, jnp.where(ch < 3 * PC, 8.0, 16.0)))
    return jnp.minimum(t, w)


def _by_group(parts):
    return jnp.concatenate([parts[g][:, g * PC:(g + 1) * PC] for g in range(4)], axis=1)


def pool_fwd(uz):
    S = uz.shape[0]
    ts = 256

    def body(u_ref, h_ref, o_ref, ot_ref):
        i = pl.program_id(0)
        u = u_ref[...]
        halo = jnp.where(i > 0, h_ref[...], 0.0)
        ext = jnp.concatenate([halo, u], axis=0)
        s2 = ext + pltpu.roll(ext, 1, 0)
        s4 = s2 + pltpu.roll(s2, 2, 0)
        s8 = s4 + pltpu.roll(s4, 4, 0)
        s16 = s8 + pltpu.roll(s8, 8, 0)
        win = _by_group([s2, s4, s8, s16])[16:, :]
        pooled = win / _pool_cnt(i * ts, ts) - u
        o_ref[...] = pooled.astype(BF16)
        ot_ref[...] = pooled.T.astype(BF16)

    return pl.pallas_call(
        body, name="pool_fwd", grid=(S // ts,),
        in_specs=[pl.BlockSpec((ts, E), lambda i: (i, 0)),
                  pl.BlockSpec((16, E), lambda i: (jnp.maximum(i * (ts // 16) - 1, 0), 0))],
        out_specs=[pl.BlockSpec((ts, E), lambda i: (i, 0)), pl.BlockSpec((E, ts), lambda i: (0, i))],
        out_shape=[SDS((S, E), BF16), SDS((E, S), BF16)],
        compiler_params=_cp(("parallel",), 48),
    )(uz, uz)


def pool_bwd(dpooled, duz):
    S = dpooled.shape[0]
    ts = 256
    nt = S // ts

    def body(d_ref, h_ref, alias_ref, o_ref):
        i = pl.program_id(0)
        dp = d_ref[...].astype(F32)
        halo = jnp.where(i < nt - 1, h_ref[...].astype(F32), 0.0)
        n = ts + 16
        ext = jnp.concatenate([dp, halo], axis=0) / _pool_cnt(i * ts, n)
        f2 = ext + pltpu.roll(ext, n - 1, 0)
        f4 = f2 + pltpu.roll(f2, n - 2, 0)
        f8 = f4 + pltpu.roll(f4, n - 4, 0)
        f16 = f8 + pltpu.roll(f8, n - 8, 0)
        win = _by_group([f2, f4, f8, f16])[:ts, :]
        o_ref[...] = (win - dp).astype(BF16)

    return pl.pallas_call(
        body, name="pool_bwd", grid=(nt,),
        in_specs=[pl.BlockSpec((ts, E), lambda i: (i, 0)),
                  pl.BlockSpec((16, E), lambda i: (jnp.minimum((i + 1) * (ts // 16), S // 16 - 1), 0)), ANY],
        out_specs=pl.BlockSpec((ts, E), lambda i: (i, 0)),
        out_shape=SDS(duz.shape, BF16),
        input_output_aliases={2: 0},
        compiler_params=_cp(("parallel",), 48),
    )(dpooled, dpooled, duz)


def mm_grp(pooled, wg, b, scale, uz):
    S = pooled.shape[0]
    tm = 512

    def body(p_ref, w_ref, b_ref, s_ref, z_ref, h_ref, y_ref, yt_ref):
        for g in range(4):
            cs = slice(g * PC, (g + 1) * PC)
            h = _dot(p_ref[:, cs], w_ref[g]) + b_ref[:, cs]
            z = z_ref[:, cs]
            yp = h * s_ref[:, cs] * (z * _sigmoid(z))
            h_ref[:, cs] = h.astype(BF16)
            y_ref[:, cs] = yp.astype(BF16)
            yt_ref[cs, :] = yp.T.astype(BF16)

    row = pl.BlockSpec((tm, E), lambda i: (i, 0))
    vec = pl.BlockSpec((1, E), lambda i: (0, 0))
    return pl.pallas_call(
        body, name="mm_grp", grid=(S // tm,),
        in_specs=[row, pl.BlockSpec((4, PC, PC), lambda i: (0, 0, 0)), vec, vec,
                  pl.BlockSpec((tm, E), lambda i: (i, 1))],
        out_specs=[row, row, pl.BlockSpec((E, tm), lambda i: (0, i))],
        out_shape=[SDS((S, E), BF16), SDS((S, E), BF16), SDS((E, S), BF16)],
        compiler_params=_cp(("parallel",), 48),
    )(pooled, wg, b, scale, uz)


def mm_dyp(da1, w_out, uz, h, scale):
    S = da1.shape[0]
    tm = 512

    def body(a_ref, w_ref, z_ref, h_ref, s_ref, dh_ref, dz_ref, dsc_ref, db_ref):
        @pl.when(pl.program_id(0) == 0)
        def _():
            dsc_ref[...] = jnp.zeros_like(dsc_ref)
            db_ref[...] = jnp.zeros_like(db_ref)

        for cidx in range(E // 256):
            col = slice(cidx * 256, (cidx + 1) * 256)
            dyp = _dot_nt(a_ref[...], w_ref[col, :])
            z = z_ref[:, col]
            hh = h_ref[:, col].astype(F32)
            sc = s_ref[:, col]
            sig = _sigmoid(z)
            dhs = dyp * z * sig
            dz_ref[:, col] = (dyp * hh * sc * sig * (1.0 + z * (1.0 - sig))).astype(BF16)
            dh = dhs * sc
            dh_ref[:, col] = dh.astype(BF16)
            dsc_ref[:, col] += _fold8(dhs * hh)
            db_ref[:, col] += _fold8(dh)

    row = pl.BlockSpec((tm, E), lambda i: (i, 0))
    acc = pl.BlockSpec((8, E), lambda i: (0, 0))
    return pl.pallas_call(
        body, name="mm_dyp", grid=(S // tm,),
        in_specs=[pl.BlockSpec((tm, D), lambda i: (i, 0)), pl.BlockSpec((E, D), lambda i: (0, 0)),
                  pl.BlockSpec((tm, E), lambda i: (i, 1)), row, pl.BlockSpec((1, E), lambda i: (0, 0))],
        out_specs=[row, pl.BlockSpec((tm, E), lambda i: (i, 1)), acc, acc],
        out_shape=[SDS((S, E), BF16), SDS((S, 2 * E), BF16), SDS((8, E), F32), SDS((8, E), F32)],
        compiler_params=_cp(("arbitrary",), 48),
    )(da1, w_out, uz, h, scale)


def mm_dpooled(dh, wg):
    S = dh.shape[0]
    tm = 1024

    def body(a_ref, w_ref, o_ref):
        for g in range(4):
            cs = slice(g * PC, (g + 1) * PC)
            o_ref[:, cs] = _dot_nt(a_ref[:, cs], w_ref[g]).astype(BF16)

    row = pl.BlockSpec((tm, E), lambda i: (i, 0))
    return pl.pallas_call(
        body, name="mm_dpooled", grid=(S // tm,),
        in_specs=[row, pl.BlockSpec((4, PC, PC), lambda i: (0, 0, 0))],
        out_specs=row, out_shape=SDS((S, E), BF16),
        compiler_params=_cp(("parallel",), 48),
    )(dh, wg)


def _rope_tables(positions):
    inv_freq = 500000.0 ** (-jnp.arange(0, 32, 2, dtype=F32) / 32)
    ang = positions.astype(F32)[:, None] * inv_freq
    cos, sin = jnp.cos(ang), jnp.sin(ang)
    S = positions.shape[0]
    one = jnp.ones((S, HD - 32), F32)
    zero16 = jnp.zeros((S, 16), F32)
    zero = jnp.zeros((S, HD - 32), F32)
    c = jnp.concatenate([cos, cos, one], axis=1)
    s1 = jnp.concatenate([-sin, zero16, zero], axis=1)
    s2 = jnp.concatenate([zero16, sin, zero], axis=1)
    return c.astype(BF16), s1.astype(BF16), s2.astype(BF16)


def kernel(x, positions, norm_pre, norm_post, attn_w_in, attn_w_out, pool_w_in, pool_w_grp, pool_b_grp, pool_scale, pool_w_out, loss_target, m_norm_pre, m_norm_post, m_attn_w_in, m_attn_w_out, m_pool_w_in, m_pool_w_grp, m_pool_b_grp, m_pool_scale, m_pool_w_out, v_norm_pre, v_norm_post, v_attn_w_in, v_attn_w_out, v_pool_w_in, v_pool_w_grp, v_pool_b_grp, v_pool_scale, v_pool_w_out):
    S = x.shape[1]
    xi, yi, ci = _mesh_pos()
    dev = 4 * xi + 2 * yi + ci
    x2 = x[0]
    tgt = loss_target[0]

    small = jnp.concatenate([pool_b_grp[0].reshape(2, HD), pool_scale[0].reshape(2, HD),
                             jnp.zeros((4, HD), F32)], axis=0)
    (w_in8,) = all_gather([attn_w_in[0].astype(BF16)], "gather_w_in")
    small, w_in8 = lax.optimization_barrier((small, w_in8))
    rest_l = [attn_w_out[0].astype(BF16), pool_w_in[0].astype(BF16), pool_w_grp[0].astype(BF16),
              pool_w_out[0].astype(BF16), small]
    rest_flight, rest_token = split_start(
        "gather_rest_start", rest_l, [lax.empty((N_DEV,) + a.shape, a.dtype) for a in rest_l], _peers_plan(), 7)

    pos = positions[0]
    tabs = [_rope_tables(pos.reshape(S // d, d).T.reshape(S)) for d in DIL]
    ehot = (jnp.arange(E)[None, :] // HD == jnp.arange(HD)[:, None]).astype(BF16)
    seg_tiles = SEG // CT

    xn0, xn0_4, xn0_16, xn0t = norm_pre0(x2, norm_pre[0:1])
    xn0s = [xn0, xn0_4.reshape(S, D), xn0_16.reshape(S, D)]
    xn0ts = [xn0t, transpose_rows(xn0s[1], "xn0t_4"), transpose_rows(xn0s[2], "xn0t_16")]
    Ps, os_, lses = [], [], []
    for g, d in enumerate(DIL):
        P = mm_in(xn0s[g], w_in8, g * seg_tiles, seg_tiles, tabs[g], f"mm_qkv{g}", after=rest_token)
        o, l = attn_fwd(P, g, d)
        Ps.append(P)
        os_.append(o)
        lses.append(l)
    z0 = mm_in(xn0, w_in8, 3 * seg_tiles, E // CT, None, "mm_z0")
    ys, lse3, ya, yat = combine_fwd(os_, lses, z0, ehot)

    rest_l, rest8 = split_wait("gather_rest_wait", rest_flight, _peers_plan(), ya)
    rest8 = [lax.dynamic_update_slice(r8, a[None], (dev,) + (0,) * a.ndim) for r8, a in zip(rest8, rest_l)]
    w_out8, wp_in8, wg8, wp_out8, small8 = rest8
    w_out = w_out8.reshape(E, D)
    wp_out = wp_out8.reshape(E, D)
    wp_in = wp_in8.transpose(1, 0, 2).reshape(D, 2 * E)
    wg = wg8.transpose(1, 0, 2, 3).reshape(4, PC, PC)
    b_full = small8[:, 0:2, :].reshape(N_DEV, 4, PC // N_DEV).transpose(1, 0, 2).reshape(1, E)
    scale_full = small8[:, 2:4, :].reshape(1, E)
    a0 = mm_rows(ya, w_out, "mm_out0", F32)
    h1, xn1, xn1t = post0_pre1(x2, a0, norm_post[0:1], norm_pre[1:2])

    uz = mm_rows(xn1, wp_in, "mm_uz", F32, tm=512)
    pooled, pooled_t = pool_fwd(uz)
    hgrp, yp, ypt = mm_grp(pooled, wg, b_full, scale_full, uz)
    a1 = mm_rows(yp, wp_out, "mm_out1", F32)
    dh2, da1, loss_rows, dg_post1 = post1_loss(h1, a1, tgt, norm_post[1:2])
    loss = lax.psum(0.5 / D * jnp.sum(loss_rows), ("x", "y", "c"))

    dh, duz, dscale_p, db_p = mm_dyp(da1, wp_out, uz, hgrp, scale_full)
    dpooled = mm_dpooled(dh, wg)
    duz = pool_bwd(dpooled, duz)
    g_wg = mm_dwg(pooled_t, dh)
    g_wp_out = mm_wgrad_rows(ypt, da1, "mm_dwp_out")
    g_wp_in = mm_wgrad_cols(xn1t, duz, "mm_dwp_in", ncols=2 * E, shard=PC, tn=1024)
    dxn1 = mm_dx_full(duz, wp_in, "mm_dxn1")
    dh1, da0, dg_pre1, dg_post0 = mid_bwd(dxn1, dh2, h1, a0, norm_pre[1:2], norm_post[0:1])

    dys, dz0 = mm_dya(da0, w_out, z0, ys[0])
    g_w_out = mm_wgrad_rows(yat, da0, "mm_dw_out")
    g_w_in = mm_dw_in_part(xn0t, dz0, 3 * seg_tiles, None, "mm_dw_in_z")
    dPs = []
    for g, d in enumerate(DIL):
        dP = attn_bwd(Ps[g], dys[g], ys[g], lse3[g], tabs[g], g, d)
        g_w_in = mm_dw_in_part(xn0ts[g], dP, g * seg_tiles, g_w_in, f"mm_dw_in{g}")
        dPs.append(dP)

    cidx = ci.astype(jnp.int32).reshape(1)
    chip = (2 * xi + yi).astype(jnp.int32).reshape(1)
    fulls = [g_w_in, g_w_out.reshape(N_DEV, E // N_DEV, D), g_wp_in,
             g_wg.reshape(4, N_DEV, PC // N_DEV, PC).transpose(1, 0, 2, 3).reshape(N_DEV, 4 * PC // N_DEV, PC),
             g_wp_out.reshape(N_DEV, E // N_DEV, D)]
    pair_flight, pair_token = split_start(
        "rs_pair_start", fulls, [lax.empty((4,) + f.shape[1:], F32) for f in fulls], _pair_plan(), 4)
    dx_z = mm_dx_part(dz0, w_in8, 3 * seg_tiles, "mm_dxn0_z", after=pair_token)
    dx_0 = mm_dx_part(dPs[0], w_in8, 0, "mm_dxn0_0", after=dx_z)
    fulls, sibs = split_wait("rs_pair_wait", pair_flight, _pair_plan(), dx_0)
    parts = [pair_add(f, s, cidx, f"pair_add{k}") for k, (f, s) in enumerate(zip(fulls, sibs))]
    chips_flight, chips_token = split_start(
        "rs_chips_start", parts, [jnp.zeros(p.shape, BF16) for p in parts], _chips_plan(), 3)
    dx_1 = mm_dx_part(dPs[1], w_in8, seg_tiles, "mm_dxn0_1", after=chips_token)
    dx_2 = mm_dx_part(dPs[2], w_in8, 2 * seg_tiles, "mm_dxn0_2", after=dx_1)
    grad_x, dg_pre0 = pre0_bwd(dx_0, dx_z, dx_1, dx_2, dh1, x2, norm_pre[0:1])
    parts, recvs = split_wait("rs_chips_wait", chips_flight, _chips_plan(), grad_x)
    shards = [(attn_w_in, m_attn_w_in, v_attn_w_in), (attn_w_out, m_attn_w_out, v_attn_w_out),
              (pool_w_in, m_pool_w_in, v_pool_w_in), (pool_w_grp, m_pool_w_grp, v_pool_w_grp),
              (pool_w_out, m_pool_w_out, v_pool_w_out)]
    big = []
    for k, (recv, part, (w, m, v)) in enumerate(zip(recvs, parts, shards)):
        shp = w.shape
        r2 = recv.shape[1:]
        res = adamw_sum(recv, part, chip, w.reshape(r2), m.reshape(r2), v.reshape(r2), f"adamw{k}")
        big.append([t.reshape(shp) for t in res])

    smalls = jnp.concatenate([dg_pre0.sum(0, keepdims=True), dg_pre1.sum(0, keepdims=True),
                              dg_post0.sum(0, keepdims=True), dg_post1.sum(0, keepdims=True),
                              db_p.sum(0).reshape(2, D), dscale_p.sum(0).reshape(2, D)], axis=0)
    (smalls8,) = all_gather([smalls], "gather_small_grads")
    tot = sum_slots(smalls8, "sum_small_grads")
    g_norm_pre, g_norm_post = tot[0:2], tot[2:4]
    g_b = lax.dynamic_slice_in_dim(tot[4:6].reshape(4, PC), dev * (PC // N_DEV), PC // N_DEV, axis=1)[None]
    g_scale = lax.dynamic_slice_in_dim(tot[6:8].reshape(1, E), dev * (E // N_DEV), E // N_DEV, axis=1)
    sm = [adamw_small(g_norm_pre, norm_pre, m_norm_pre, v_norm_pre, "adamw_norm_pre"),
          adamw_small(g_norm_post, norm_post, m_norm_post, v_norm_post, "adamw_norm_post"),
          adamw_small(g_b, pool_b_grp, m_pool_b_grp, v_pool_b_grp, "adamw_b"),
          adamw_small(g_scale, pool_scale, m_pool_scale, v_pool_scale, "adamw_scale")]

    grads = [g_norm_pre, g_norm_post, big[0][0], big[1][0], big[2][0], big[3][0], g_b, g_scale, big[4][0]]

    def pick(k):
        return [sm[0][k - 1], sm[1][k - 1], big[0][k], big[1][k], big[2][k], big[3][k], sm[2][k - 1], sm[3][k - 1],
                big[4][k]]

    return (loss, grad_x[None], *grads, *pick(1), *pick(2), *pick(3))
```

```python
import functools
import math

import numpy as np
import jax
import jax.numpy as jnp
from jax import lax
from jax.experimental import pallas as pl
from jax.experimental.pallas import tpu as pltpu

F32 = jnp.float32
BF16 = jnp.bfloat16
SDS = jax.ShapeDtypeStruct

N_DEV = 8
D = 1024
E = 2048
HD = 128
NH = E // HD
DIL = (1, 4, 16)
QB = 128
SEG = 3 * E
W_IN_COLS = 3 * SEG + E
W_SHARD = W_IN_COLS // N_DEV
CT = 512
POOL_WIN = (2, 4, 8, 16)
PC = E // 4
EPS = 1e-6
NEG = -1e30
SCALE = 1.0 / math.sqrt(HD)
LR, B1, B2, ADAM_EPS, WD, STEP = 0.001, 0.9, 0.999, 1e-08, 0.01, 10
MIB = 1024 * 1024
ANY = pl.BlockSpec(memory_space=pl.ANY)
MESH = pl.DeviceIdType.MESH


def _cp(sem, mb):
    return pltpu.CompilerParams(dimension_semantics=sem, vmem_limit_bytes=mb * MIB)


def _dot(a, b):
    return jnp.dot(a, b, preferred_element_type=F32)


def _dot_nt(a, b):
    return lax.dot_general(a, b, (((1,), (1,)), ((), ())), preferred_element_type=F32)


def _rms(h):
    return lax.rsqrt(jnp.mean(h * h, axis=-1, keepdims=True) + EPS)


def _row_tile(R, C, budget):
    tr = R
    while tr * C * 4 > budget and tr % 16 == 0:
        tr //= 2
    return tr


def _fold8(t):
    return t.reshape(t.shape[0] // 8, 8, t.shape[1]).sum(axis=0)


def _sigmoid(z):
    return 1.0 / (1.0 + jnp.exp(-z))


LANES = 128


def _scr(rows, C):
    return pltpu.VMEM((C // LANES, rows, LANES), F32)


def _scr_put(scr, val):
    for c in range(scr.shape[0]):
        scr[c] = val[:, c * LANES:(c + 1) * LANES]


def _scr_get(scr):
    return jnp.concatenate([scr[c] for c in range(scr.shape[0])], axis=1)


def _store_perm(dst_ref, scr, d):
    n = dst_ref.shape[1]
    for r in range(d):
        for c in range(scr.shape[0]):
            dst_ref[r, :, c * LANES:(c + 1) * LANES] = scr[c, pl.ds(r, n, stride=d), :].astype(dst_ref.dtype)


def _load_perm(scr, src_ref, d, add=False):
    n = src_ref.shape[1]
    for r in range(d):
        rows = pl.ds(r, n, stride=d)
        for c in range(scr.shape[0]):
            v = src_ref[r, :, c * LANES:(c + 1) * LANES].astype(F32)
            scr[c, rows, :] = scr[c, rows, :] + v if add else v


def _rope(t, c, s1, s2):
    t = t.astype(BF16)
    return t * c + pltpu.roll(t, HD - 16, 1) * s1 + pltpu.roll(t, 16, 1) * s2


def _unrope(t, c, s1, s2):
    t = t.astype(BF16)
    return t * c - pltpu.roll(t, HD - 16, 1) * s1 - pltpu.roll(t, 16, 1) * s2


def _mesh_pos():
    return lax.axis_index("x"), lax.axis_index("y"), lax.axis_index("c")


def all_gather(arrs, name):
    n = len(arrs)

    def body(*refs):
        ins, outs = refs[:n], refs[n:2 * n]
        send_sems, recv_sems, local_sems = refs[2 * n:]
        x, y, c = _mesh_pos()
        me, sib = (x, y, c), (x, y, 1 - c)
        chips = [(1 - x, y), (x, 1 - y), (1 - x, 1 - y)]

        def slot(p):
            return 4 * p[0] + 2 * p[1] + p[2]

        def copy(a, k, block, to, src=None):
            dst = outs[a].at[slot(block)]
            return pltpu.make_async_remote_copy(
                src_ref=dst if src is None else src, dst_ref=dst,
                send_sem=send_sems.at[a, k], recv_sem=recv_sems.at[a, k],
                device_id=to, device_id_type=MESH)

        mine = [pltpu.make_async_copy(ins[a], outs[a].at[slot(me)], local_sems.at[a]) for a in range(n)]
        for cp in mine:
            cp.start()
        first = []
        for a in range(n):
            first.append(copy(a, 0, me, sib, src=ins[a]))
            for j, chip in enumerate(chips):
                first.append(copy(a, 1 + j, me, (*chip, c), src=ins[a]))
        for cp in first:
            cp.start()
        passed = []
        for j, chip in enumerate(chips):
            for a in range(n):
                copy(a, 1 + j, (*chip, c), me).wait_recv()
                fw = copy(a, 4 + j, (*chip, c), sib)
                fw.start()
                passed.append(fw)
        for a in range(n):
            copy(a, 0, sib, me).wait_recv()
        for j, chip in enumerate(chips):
            for a in range(n):
                copy(a, 4 + j, (*chip, 1 - c), me).wait_recv()
        for cp in first + passed:
            cp.wait_send()
        for cp in mine:
            cp.wait()

    return pl.pallas_call(
        body, name=name,
        out_shape=[SDS((N_DEV,) + a.shape, a.dtype) for a in arrs],
        in_specs=[ANY] * n, out_specs=[ANY] * n,
        scratch_shapes=[pltpu.SemaphoreType.DMA((n, 7)), pltpu.SemaphoreType.DMA((n, 7)),
                        pltpu.SemaphoreType.DMA((n,))],
    )(*arrs)


HBM_SPEC = pl.BlockSpec(memory_space=pltpu.HBM)
SEM_SPEC = pl.BlockSpec(memory_space=pltpu.SEMAPHORE)
EFFECT = pltpu.SideEffectType.DATAFLOW_SIDE_EFFECTING


def _pair_plan():
    def plan(x, y, c):
        return [(2 * q + (1 - c), q, (x, y, 1 - c)) for q in range(4)]
    return plan


def _chips_plan():
    def plan(x, y, c):
        chips = [(1 - x, y), (x, 1 - y), (1 - x, 1 - y)]
        return [(2 * cx + cy, 2 * x + y, (cx, cy, c)) for cx, cy in chips]
    return plan


def _peers_plan():
    def plan(x, y, c):
        out = []
        for k in range(1, N_DEV):
            fx, fy, fc = (k >> 2) & 1, (k >> 1) & 1, k & 1
            px, py, pc = (x + fx) % 2, (y + fy) % 2, (c + fc) % 2
            out.append((None, 4 * x + 2 * y + c, (px, py, pc)))
        return out
    return plan


def _split_copies(plan, srcs, lands, send_sems, recv_sems):
    x, y, c = _mesh_pos()
    cps = []
    for a, (src, land) in enumerate(zip(srcs, lands)):
        steps = plan(x, y, c)
        for k, (si, li, to) in enumerate(steps):
            sem = a * len(steps) + k
            cps.append(pltpu.make_async_remote_copy(
                src_ref=src if si is None else src.at[si], dst_ref=land.at[li],
                send_sem=send_sems.at[sem], recv_sem=recv_sems.at[sem],
                device_id=to, device_id_type=MESH))
    return cps


def split_start(name, srcs, lands, plan, nk):
    n = len(srcs)

    def body(*refs):
        send_sems, recv_sems = refs[2 * n], refs[2 * n + 1]
        token = refs[-1]
        for cp in _split_copies(plan, refs[:n], refs[n:2 * n], send_sems, recv_sems):
            cp.start()
        token[...] = jnp.zeros_like(token)

    ops = [pltpu.with_memory_space_constraint(a, pltpu.HBM) for a in list(srcs) + list(lands)]
    res = pl.pallas_call(
        body, name=name,
        out_shape=(pltpu.SemaphoreType.DMA((n * nk,)), pltpu.SemaphoreType.DMA((n * nk,)),
                   *[pltpu.HBM(a.shape, a.dtype) for a in ops], SDS((8, 128), F32)),
        in_specs=[HBM_SPEC] * (2 * n),
        out_specs=(SEM_SPEC, SEM_SPEC, *[HBM_SPEC] * (2 * n), pl.BlockSpec(memory_space=pltpu.VMEM)),
        input_output_aliases={i: 2 + i for i in range(2 * n)},
        compiler_params=pltpu.CompilerParams(has_side_effects=EFFECT),
    )(*ops)
    return res[:-1], res[-1]


def split_wait(name, flight, plan, after):
    send_sems, recv_sems = flight[0], flight[1]
    bufs = list(flight[2:])
    n = len(bufs) // 2

    def body(*refs):
        for cp in _split_copies(plan, refs[:n], refs[n:2 * n], refs[2 * n], refs[2 * n + 1]):
            cp.wait_send()
            cp.wait_recv()

    res = pl.pallas_call(
        body, name=name,
        out_shape=[pltpu.HBM(a.shape, a.dtype) for a in bufs],
        in_specs=[HBM_SPEC] * (2 * n) + [SEM_SPEC, SEM_SPEC, ANY],
        out_specs=[HBM_SPEC] * (2 * n),
        input_output_aliases={i: i for i in range(2 * n)},
        compiler_params=pltpu.CompilerParams(has_side_effects=EFFECT),
    )(*bufs, send_sems, recv_sems, after)
    return res[:n], res[n:]


def rs_pair(arrs, name):
    n = len(arrs)

    def body(*refs):
        ins, outs = refs[:n], refs[n:2 * n]
        send_sems, recv_sems = refs[2 * n:]
        x, y, c = _mesh_pos()
        cps = []
        for a in range(n):
            for q in range(4):
                cps.append(pltpu.make_async_remote_copy(
                    src_ref=ins[a].at[2 * q + (1 - c)], dst_ref=outs[a].at[q],
                    send_sem=send_sems.at[a, q], recv_sem=recv_sems.at[a, q],
                    device_id=(x, y, 1 - c), device_id_type=MESH))
        for cp in cps:
            cp.start()
        for cp in cps:
            cp.wait()

    return pl.pallas_call(
        body, name=name,
        out_shape=[SDS((4,) + a.shape[1:], a.dtype) for a in arrs],
        in_specs=[ANY] * n, out_specs=[ANY] * n,
        scratch_shapes=[pltpu.SemaphoreType.DMA((n, 4)), pltpu.SemaphoreType.DMA((n, 4))],
    )(*arrs)


def rs_chips(parts, name):
    n = len(parts)

    def body(*refs):
        ins, outs = refs[:n], refs[n:2 * n]
        send_sems, recv_sems, local_sems = refs[2 * n:]
        x, y, c = _mesh_pos()
        mychip = 2 * x + y
        chips = [(1 - x, y), (x, 1 - y), (1 - x, 1 - y)]
        mine = [pltpu.make_async_copy(ins[a].at[mychip], outs[a].at[mychip], local_sems.at[a]) for a in range(n)]
        for cp in mine:
            cp.start()
        cps = []
        for a in range(n):
            for j, chip in enumerate(chips):
                q = 2 * chip[0] + chip[1]
                cps.append(pltpu.make_async_remote_copy(
                    src_ref=ins[a].at[q], dst_ref=outs[a].at[mychip],
                    send_sem=send_sems.at[a, j], recv_sem=recv_sems.at[a, j],
                    device_id=(*chip, c), device_id_type=MESH))
        for cp in cps:
            cp.start()
        for cp in cps:
            cp.wait()
        for cp in mine:
            cp.wait()

    return pl.pallas_call(
        body, name=name,
        out_shape=[SDS(a.shape, a.dtype) for a in parts],
        in_specs=[ANY] * n, out_specs=[ANY] * n,
        scratch_shapes=[pltpu.SemaphoreType.DMA((n, 3)), pltpu.SemaphoreType.DMA((n, 3)),
                        pltpu.SemaphoreType.DMA((n,))],
    )(*parts)


def pair_add(full, sib, cidx, name):
    _, R, C = full.shape
    tr = _row_tile(R, C, MIB)

    def body(c_ref, a_ref, b_ref, o_ref):
        o_ref[...] = (a_ref[...] + b_ref[...]).astype(BF16)

    return pl.pallas_call(
        body, name=name,
        grid_spec=pltpu.PrefetchScalarGridSpec(
            num_scalar_prefetch=1, grid=(4, R // tr),
            in_specs=[pl.BlockSpec((None, tr, C), lambda q, i, cr: (2 * q + cr[0], i, 0)),
                      pl.BlockSpec((None, tr, C), lambda q, i, cr: (q, i, 0))],
            out_specs=pl.BlockSpec((None, tr, C), lambda q, i, cr: (q, i, 0))),
        out_shape=SDS((4, R, C), BF16),
        compiler_params=_cp(("parallel", "parallel"), 32),
    )(cidx, full, sib)


def _adam_math(w, g, m, v):
    m2 = B1 * m + (1.0 - B1) * g
    v2 = B2 * v + (1.0 - B2) * (g * g)
    m_hat = m2 / (1.0 - B1 ** STEP)
    v_hat = v2 / (1.0 - B2 ** STEP)
    delta = -LR * (m_hat / (jnp.sqrt(v_hat) + ADAM_EPS) + WD * w)
    return delta, m2, v2


def adamw_sum(recv, part, chip, w, m, v, name):
    K, R, C = recv.shape
    tr = _row_tile(R, C, MIB)

    def body(chip_ref, r_ref, p_ref, w_ref, m_ref, v_ref, g_ref, d_ref, m2_ref, v2_ref):
        g = r_ref[0].astype(F32)
        for k in range(1, K):
            g = g + r_ref[k].astype(F32)
        g = g + p_ref[...].astype(F32)
        delta, m2, v2 = _adam_math(w_ref[...], g, m_ref[...], v_ref[...])
        g_ref[...] = g
        d_ref[...] = delta
        m2_ref[...] = m2
        v2_ref[...] = v2

    tile = pl.BlockSpec((tr, C), lambda i, cr: (i, 0))
    return pl.pallas_call(
        body, name=name,
        grid_spec=pltpu.PrefetchScalarGridSpec(
            num_scalar_prefetch=1, grid=(R // tr,),
            in_specs=[pl.BlockSpec((K, tr, C), lambda i, cr: (0, i, 0)),
                      pl.BlockSpec((None, tr, C), lambda i, cr: (cr[0], i, 0)), tile, tile, tile],
            out_specs=[tile] * 4),
        out_shape=[SDS((R, C), F32)] * 4,
        compiler_params=_cp(("parallel",), 32),
    )(chip, recv, part, w, m, v)


def adamw_small(g, w, m, v, name):
    def body(g_ref, w_ref, m_ref, v_ref, d_ref, m2_ref, v2_ref):
        delta, m2, v2 = _adam_math(w_ref[...], g_ref[...], m_ref[...], v_ref[...])
        d_ref[...] = delta
        m2_ref[...] = m2
        v2_ref[...] = v2

    return pl.pallas_call(body, name=name, out_shape=[SDS(w.shape, F32)] * 3)(g, w, m, v)


def sum_slots(a, name):
    K = a.shape[0]

    def body(a_ref, o_ref):
        t = a_ref[0]
        for k in range(1, K):
            t = t + a_ref[k]
        o_ref[...] = t

    return pl.pallas_call(body, name=name, out_shape=SDS(a.shape[1:], F32))(a)


def norm_pre0(x, g):
    S = x.shape[0]
    ts = 512

    def body(x_ref, g_ref, o_ref, o4_ref, o16_ref, ot_ref, scr):
        h = x_ref[...]
        xn = h * _rms(h) * g_ref[...]
        o_ref[...] = xn.astype(BF16)
        ot_ref[...] = xn.T.astype(BF16)
        _scr_put(scr, xn)
        _store_perm(o4_ref, scr, 4)
        _store_perm(o16_ref, scr, 16)

    return pl.pallas_call(
        body, name="norm_pre0", grid=(S // ts,),
        in_specs=[pl.BlockSpec((ts, D), lambda i: (i, 0)), pl.BlockSpec((1, D), lambda i: (0, 0))],
        out_specs=[pl.BlockSpec((ts, D), lambda i: (i, 0)),
                   pl.BlockSpec((4, ts // 4, D), lambda i: (0, i, 0)),
                   pl.BlockSpec((16, ts // 16, D), lambda i: (0, i, 0)),
                   pl.BlockSpec((D, ts), lambda i: (0, i))],
        out_shape=[SDS((S, D), BF16), SDS((4, S // 4, D), BF16), SDS((16, S // 16, D), BF16), SDS((D, S), BF16)],
        scratch_shapes=[_scr(ts, D)],
        compiler_params=_cp(("parallel",), 32),
    )(x, g)


def transpose_rows(a, name):
    S, C = a.shape
    ts = 512

    def body(a_ref, o_ref):
        o_ref[...] = a_ref[...].astype(F32).T.astype(BF16)

    return pl.pallas_call(
        body, name=name, grid=(S // ts,),
        in_specs=[pl.BlockSpec((ts, C), lambda i: (i, 0))],
        out_specs=pl.BlockSpec((C, ts), lambda i: (0, i)),
        out_shape=SDS((C, S), BF16),
        compiler_params=_cp(("parallel",), 32),
    )(a)


def post0_pre1(x, a0, g_post, g_pre):
    S = x.shape[0]
    ts = 512

    def body(x_ref, a_ref, gp_ref, gn_ref, h_ref, o_ref, ot_ref):
        a = a_ref[...]
        h1 = x_ref[...] + a * _rms(a) * gp_ref[...]
        h_ref[...] = h1
        xn = h1 * _rms(h1) * gn_ref[...]
        o_ref[...] = xn.astype(BF16)
        ot_ref[...] = xn.T.astype(BF16)

    row = pl.BlockSpec((ts, D), lambda i: (i, 0))
    vec = pl.BlockSpec((1, D), lambda i: (0, 0))
    return pl.pallas_call(
        body, name="post0_pre1", grid=(S // ts,),
        in_specs=[row, row, vec, vec],
        out_specs=[row, row, pl.BlockSpec((D, ts), lambda i: (0, i))],
        out_shape=[SDS((S, D), F32), SDS((S, D), BF16), SDS((D, S), BF16)],
        compiler_params=_cp(("parallel",), 40),
    )(x, a0, g_post, g_pre)


def post1_loss(h1, a1, target, g_post):
    S = h1.shape[0]
    ts = 512

    def body(h_ref, a_ref, t_ref, g_ref, dh_ref, da_ref, loss_ref, dg_ref):
        @pl.when(pl.program_id(0) == 0)
        def _():
            loss_ref[...] = jnp.zeros_like(loss_ref)
            dg_ref[...] = jnp.zeros_like(dg_ref)

        a = a_ref[...]
        g = g_ref[...]
        rp = _rms(a)
        yhat = a * rp
        e = h_ref[...] + yhat * g - t_ref[...]
        loss_ref[...] += _fold8(e * e)
        dh = e * (1.0 / D)
        dh_ref[...] = dh
        dg_ref[...] += _fold8(dh * yhat)
        dyh = dh * g
        da = rp * (dyh - yhat * jnp.mean(dyh * yhat, axis=-1, keepdims=True))
        da_ref[...] = da.astype(BF16)

    row = pl.BlockSpec((ts, D), lambda i: (i, 0))
    acc = pl.BlockSpec((8, D), lambda i: (0, 0))
    return pl.pallas_call(
        body, name="post1_loss", grid=(S // ts,),
        in_specs=[row, row, row, pl.BlockSpec((1, D), lambda i: (0, 0))],
        out_specs=[row, row, acc, acc],
        out_shape=[SDS((S, D), F32), SDS((S, D), BF16), SDS((8, D), F32), SDS((8, D), F32)],
        compiler_params=_cp(("arbitrary",), 40),
    )(h1, a1, target, g_post)


def mid_bwd(dxn1, dh2, h1, a0, g_pre1, g_post0):
    S = h1.shape[0]
    ts = 512

    def body(dx_ref, dh2_ref, h_ref, a_ref, gn_ref, gp_ref, dh1_ref, da_ref, dgn_ref, dgp_ref):
        @pl.when(pl.program_id(0) == 0)
        def _():
            dgn_ref[...] = jnp.zeros_like(dgn_ref)
            dgp_ref[...] = jnp.zeros_like(dgp_ref)

        h = h_ref[...]
        r1 = _rms(h)
        xhat = h * r1
        dxn = dx_ref[...]
        dgn_ref[...] += _fold8(dxn * xhat)
        dxh = dxn * gn_ref[...]
        dh1 = dh2_ref[...] + r1 * (dxh - xhat * jnp.mean(dxh * xhat, axis=-1, keepdims=True))
        dh1_ref[...] = dh1
        a = a_ref[...]
        rp = _rms(a)
        yhat = a * rp
        dgp_ref[...] += _fold8(dh1 * yhat)
        dyh = dh1 * gp_ref[...]
        da = rp * (dyh - yhat * jnp.mean(dyh * yhat, axis=-1, keepdims=True))
        da_ref[...] = da.astype(BF16)

    row = pl.BlockSpec((ts, D), lambda i: (i, 0))
    vec = pl.BlockSpec((1, D), lambda i: (0, 0))
    acc = pl.BlockSpec((8, D), lambda i: (0, 0))
    return pl.pallas_call(
        body, name="mid_bwd", grid=(S // ts,),
        in_specs=[row, row, row, row, vec, vec],
        out_specs=[row, row, acc, acc],
        out_shape=[SDS((S, D), F32), SDS((S, D), BF16), SDS((8, D), F32), SDS((8, D), F32)],
        compiler_params=_cp(("arbitrary",), 48),
    )(dxn1, dh2, h1, a0, g_pre1, g_post0)


def pre0_bwd(dx_tok, dx_z, dx4, dx16, dh1, x, g_pre0):
    S = x.shape[0]
    ts = 512

    def body(da_ref, dz_ref, d4_ref, d16_ref, dh_ref, x_ref, g_ref, gx_ref, dg_ref, scr):
        @pl.when(pl.program_id(0) == 0)
        def _():
            dg_ref[...] = jnp.zeros_like(dg_ref)

        _scr_put(scr, da_ref[...] + dz_ref[...])
        _load_perm(scr, d4_ref, 4, add=True)
        _load_perm(scr, d16_ref, 16, add=True)
        h = x_ref[...]
        r = _rms(h)
        xhat = h * r
        dxn = _scr_get(scr)
        dg_ref[...] += _fold8(dxn * xhat)
        dxh = dxn * g_ref[...]
        gx_ref[...] = dh_ref[...] + r * (dxh - xhat * jnp.mean(dxh * xhat, axis=-1, keepdims=True))

    row = pl.BlockSpec((ts, D), lambda i: (i, 0))
    return pl.pallas_call(
        body, name="pre0_bwd", grid=(S // ts,),
        in_specs=[row, row, pl.BlockSpec((4, ts // 4, D), lambda i: (0, i, 0)),
                  pl.BlockSpec((16, ts // 16, D), lambda i: (0, i, 0)), row, row,
                  pl.BlockSpec((1, D), lambda i: (0, 0))],
        out_specs=[row, pl.BlockSpec((8, D), lambda i: (0, 0))],
        out_shape=[SDS((S, D), F32), SDS((8, D), F32)],
        scratch_shapes=[_scr(ts, D)],
        compiler_params=_cp(("arbitrary",), 48),
    )(dx_tok, dx_z, dx4.reshape(4, S // 4, D), dx16.reshape(16, S // 16, D), dh1, x, g_pre0)


def _w_tile(tile0):
    per = W_SHARD // CT
    return lambda t: ((tile0 + t) // per, 0, (tile0 + t) % per)


def mm_in(xn, w8, tile0, ntiles, tabs, name, after=None):
    S = xn.shape[0]
    tm = 2048
    wt = _w_tile(tile0)

    def body(a_ref, b_ref, *rest):
        o_ref = rest[-1]
        rc = 512
        for u in range(tm // rc):
            rows = slice(u * rc, (u + 1) * rc)
            r = _dot(a_ref[rows, :], b_ref[...])
            if tabs is None:
                o_ref[rows, :] = r.astype(BF16)
                continue
            c_ref, s1_ref, s2_ref = rest[:3]
            rot = pl.program_id(1) < 2 * E // CT
            c = jnp.where(rot, c_ref[rows, :], 1.0)
            s1 = jnp.where(rot, s1_ref[rows, :], 0.0)
            s2 = jnp.where(rot, s2_ref[rows, :], 0.0)
            for hh in range(CT // HD):
                cs = slice(hh * HD, (hh + 1) * HD)
                o_ref[rows, cs] = _rope(r[:, cs], c, s1, s2).astype(BF16)

    tab = pl.BlockSpec((tm, HD), lambda i, t: (i, 0))
    return pl.pallas_call(
        body, name=name, grid=(S // tm, ntiles),
        in_specs=[pl.BlockSpec((tm, D), lambda i, t: (i, 0)),
                  pl.BlockSpec((None, D, CT), lambda i, t: wt(t))] + ([] if tabs is None else [tab] * 3)
        + ([] if after is None else [ANY]),
        out_specs=pl.BlockSpec((tm, CT), lambda i, t: (i, t)),
        out_shape=SDS((S, ntiles * CT), BF16),
        compiler_params=_cp(("parallel", "parallel"), 48),
    )(xn, w8, *(() if tabs is None else tabs), *(() if after is None else (after,)))


def mm_rows(a, b, name, out_dtype, tm=1024):
    M, K = a.shape
    N = b.shape[1]

    def body(a_ref, b_ref, o_ref):
        for cidx in range(N // 256):
            col = slice(cidx * 256, (cidx + 1) * 256)
            o_ref[:, col] = _dot(a_ref[...], b_ref[:, col]).astype(out_dtype)

    return pl.pallas_call(
        body, name=name, grid=(M // tm,),
        in_specs=[pl.BlockSpec((tm, K), lambda i: (i, 0)), pl.BlockSpec((K, N), lambda i: (0, 0))],
        out_specs=pl.BlockSpec((tm, N), lambda i: (i, 0)),
        out_shape=SDS((M, N), out_dtype),
        compiler_params=_cp(("parallel",), 48),
    )(a, b)


def mm_uz(xn, w8):
    S = xn.shape[0]
    tm = 512
    bw = w8.shape[2]

    def body(a_ref, b_ref, o_ref):
        for dv in range(N_DEV):
            o_ref[:, dv * bw:(dv + 1) * bw] = _dot(a_ref[...], b_ref[dv])

    return pl.pallas_call(
        body, name="mm_uz", grid=(S // tm,),
        in_specs=[pl.BlockSpec((tm, D), lambda i: (i, 0)), pl.BlockSpec((N_DEV, D, bw), lambda i: (0, 0, 0))],
        out_specs=pl.BlockSpec((tm, N_DEV * bw), lambda i: (i, 0)),
        out_shape=SDS((S, N_DEV * bw), F32),
        compiler_params=_cp(("parallel",), 48),
    )(xn, w8)


def mm_acc(a, b, name, *, grid, a_spec, b_spec, o_spec, o_shape, acc_shape, write, vmem=48):
    nk = grid[-1]

    def body(a_ref, b_ref, o_ref, acc_ref):
        k = pl.program_id(len(grid) - 1)

        @pl.when(k == 0)
        def _():
            acc_ref[...] = jnp.zeros_like(acc_ref)

        acc_ref[...] += _dot(a_ref[...], b_ref[...])

        @pl.when(k == nk - 1)
        def _():
            write(o_ref, acc_ref)

    return pl.pallas_call(
        body, name=name, grid=grid, in_specs=[a_spec, b_spec], out_specs=o_spec, out_shape=o_shape,
        scratch_shapes=[pltpu.VMEM(acc_shape, F32)],
        compiler_params=_cp(("parallel",) * (len(grid) - 1) + ("arbitrary",), vmem),
    )(a, b)


def _write_plain(o_ref, acc_ref):
    o_ref[...] = acc_ref[...]


def mm_wgrad_rows(at, b, name):
    M, S = at.shape
    N = b.shape[1]
    tm, tk = 1024, 1024
    return mm_acc(at, b, name, grid=(M // tm, S // tk),
                  a_spec=pl.BlockSpec((tm, tk), lambda i, k: (i, k)),
                  b_spec=pl.BlockSpec((tk, N), lambda i, k: (k, 0)),
                  o_spec=pl.BlockSpec((tm, N), lambda i, k: (i, 0)),
                  o_shape=SDS((M, N), F32), acc_shape=(tm, N), write=_write_plain)


def mm_wgrad_cols(at, b, name, *, ncols, shard, tn):
    M, S = at.shape
    tk = 1024
    per = shard // tn if tn <= shard else 1
    nb = max(1, tn // shard)

    if tn <= shard:
        o_spec = pl.BlockSpec((None, M, tn), lambda t, k: (t // per, 0, t % per))
        write = _write_plain
    else:
        o_spec = pl.BlockSpec((nb, M, shard), lambda t, k: (t, 0, 0))

        def write(o_ref, acc_ref):
            for u in range(nb):
                o_ref[u] = acc_ref[:, u * shard:(u + 1) * shard]

    return mm_acc(at, b, name, grid=(ncols // tn, S // tk),
                  a_spec=pl.BlockSpec((M, tk), lambda t, k: (0, k)),
                  b_spec=pl.BlockSpec((tk, tn), lambda t, k: (k, t)),
                  o_spec=o_spec, o_shape=SDS((N_DEV, M, shard), F32), acc_shape=(M, tn), write=write)


def mm_dwg(pooled_t, dh):
    S = dh.shape[0]
    tk = 2048
    return mm_acc(pooled_t, dh, "mm_dwg", grid=(4, S // tk),
                  a_spec=pl.BlockSpec((PC, tk), lambda g, k: (g, k)),
                  b_spec=pl.BlockSpec((tk, PC), lambda g, k: (k, g)),
                  o_spec=pl.BlockSpec((None, PC, PC), lambda g, k: (g, 0, 0)),
                  o_shape=SDS((4, PC, PC), F32), acc_shape=(PC, PC), write=_write_plain)


def mm_dx_full(da, w, name):
    S, K = da.shape
    N = w.shape[0]
    tm = 512

    def body(a_ref, b_ref, o_ref):
        o_ref[...] = _dot_nt(a_ref[...], b_ref[...])

    return pl.pallas_call(
        body, name=name, grid=(S // tm,),
        in_specs=[pl.BlockSpec((tm, K), lambda i: (i, 0)), pl.BlockSpec((N, K), lambda i: (0, 0))],
        out_specs=pl.BlockSpec((tm, N), lambda i: (i, 0)),
        out_shape=SDS((S, N), F32),
        compiler_params=_cp(("parallel",), 48),
    )(da, w)


def mm_dw_in_part(at, b, tile0, prev, name):
    M, S = at.shape
    ntiles = b.shape[1] // CT
    tk = 2048
    nk = S // tk
    wt = _w_tile(tile0)

    def body(a_ref, b_ref, *rest):
        o_ref, acc_ref = rest[-2:]
        k = pl.program_id(1)

        @pl.when(k == 0)
        def _():
            acc_ref[...] = jnp.zeros_like(acc_ref)

        acc_ref[...] += _dot(a_ref[...], b_ref[...])

        @pl.when(k == nk - 1)
        def _():
            o_ref[...] = acc_ref[...]

    return pl.pallas_call(
        body, name=name, grid=(ntiles, nk),
        in_specs=[pl.BlockSpec((M, tk), lambda t, k: (0, k)), pl.BlockSpec((tk, CT), lambda t, k: (k, t))]
        + ([] if prev is None else [ANY]),
        out_specs=pl.BlockSpec((None, M, CT), lambda t, k: wt(t)),
        out_shape=SDS((N_DEV, M, W_SHARD), F32),
        scratch_shapes=[pltpu.VMEM((M, CT), F32)],
        input_output_aliases={} if prev is None else {2: 0},
        compiler_params=_cp(("parallel", "arbitrary"), 48),
    )(at, b, *(() if prev is None else (prev,)))


def mm_dx_part(da, w8, tile0, name, after=None):
    S = da.shape[0]
    ntiles = da.shape[1] // CT
    tm = 2048
    wt = _w_tile(tile0)

    def body(a_ref, b_ref, *rest):
        o_ref, acc_ref = rest[-2:]
        t = pl.program_id(1)

        @pl.when(t == 0)
        def _():
            acc_ref[...] = jnp.zeros_like(acc_ref)

        acc_ref[...] += _dot_nt(a_ref[...], b_ref[...])

        @pl.when(t == ntiles - 1)
        def _():
            o_ref[...] = acc_ref[...]

    return pl.pallas_call(
        body, name=name, grid=(S // tm, ntiles),
        in_specs=[pl.BlockSpec((tm, CT), lambda i, t: (i, t)), pl.BlockSpec((None, D, CT), lambda i, t: wt(t))]
        + ([] if after is None else [ANY]),
        out_specs=pl.BlockSpec((tm, D), lambda i, t: (i, 0)),
        out_shape=SDS((S, D), F32),
        scratch_shapes=[pltpu.VMEM((tm, D), F32)],
        compiler_params=_cp(("parallel", "arbitrary"), 56),
    )(da, w8, *(() if after is None else (after,)))


HEADS_PER_STEP = 2
AHEAD = 2


def _band_masks(not_first):
    row = lax.broadcasted_iota(jnp.int32, (QB, QB), 0)
    col = lax.broadcasted_iota(jnp.int32, (QB, QB), 1)
    cur = jnp.where(col <= row, 0.0, NEG)
    prev = jnp.where(col >= row, 0.0, NEG)
    first = jnp.where(jnp.logical_and(col >= row, not_first), 0.0, NEG)
    return col, jnp.concatenate([prev, cur], axis=1), jnp.concatenate([first, cur], axis=1)


def _fill_kv(ext, qkv_ref, kh_ref, vh_ref):
    ext[0:QB, 0:E] = kh_ref[...]
    ext[0:QB, E:2 * E] = vh_ref[...]
    ext[QB:, :] = qkv_ref[:, E:3 * E]


def attn_fwd(P, g, d):
    S = P.shape[0]
    L = S // d
    T = min(512, L)
    nq = T // QB
    ni = L // T

    def body(qkv_ref, kh_ref, vh_ref, o_ref, lse_ref, ext):
        col, mask, mask_first = _band_masks(pl.program_id(1) > 0)
        lse_ref[...] = jnp.zeros_like(lse_ref)
        _fill_kv(ext, qkv_ref, kh_ref, vh_ref)

        def heads(hp, carry):
            def front(h, j):
                cq = pl.ds(pl.multiple_of(h * HD, HD), HD)
                rows = slice(j * QB, (j + 1) * QB)
                krows = slice(j * QB, (j + 2) * QB)
                s = _dot_nt(qkv_ref[rows, cq], ext[krows, cq]) * SCALE + (mask_first if j == 0 else mask)
                m = jnp.max(s, axis=1, keepdims=True)
                p = jnp.exp(s - m)
                den = jnp.sum(p, axis=1, keepdims=True)
                lse_ref[rows, :] = jnp.where(col == h, m + jnp.log(den), lse_ref[rows, :])
                return p.astype(BF16), den

            def back(h, j, p, den):
                off = pl.multiple_of(h * HD, HD)
                rows = slice(j * QB, (j + 1) * QB)
                krows = slice(j * QB, (j + 2) * QB)
                o_ref[rows, pl.ds(off, HD)] = (_dot(p, ext[krows, pl.ds(E + off, HD)]) / den).astype(BF16)

            items = [(HEADS_PER_STEP * hp + hh, j) for hh in range(HEADS_PER_STEP) for j in range(nq)]
            queue = [front(*it) for it in items[:AHEAD]]
            for u, it in enumerate(items):
                if u + AHEAD < len(items):
                    queue.append(front(*items[u + AHEAD]))
                back(*it, *queue.pop(0))
            return carry

        lax.fori_loop(0, NH // HEADS_PER_STEP, heads, 0)

    halo = lambda r, i: jnp.maximum(r * (L // QB) + i * nq - 1, 0)
    return pl.pallas_call(
        body, name=f"attn_fwd{g}", grid=(d, ni),
        in_specs=[pl.BlockSpec((T, SEG), lambda r, i: (r * ni + i, 0)),
                  pl.BlockSpec((QB, E), lambda r, i: (halo(r, i), 1)),
                  pl.BlockSpec((QB, E), lambda r, i: (halo(r, i), 2))],
        out_specs=[pl.BlockSpec((T, E), lambda r, i: (r * ni + i, 0)),
                   pl.BlockSpec((T, HD), lambda r, i: (r * ni + i, 0))],
        out_shape=[SDS((S, E), BF16), SDS((S, HD), F32)],
        scratch_shapes=[pltpu.VMEM((T + QB, 2 * E), BF16)],
        compiler_params=_cp(("parallel", "parallel"), 48),
    )(P, P, P)


def _perm_specs(ts, C):
    return [pl.BlockSpec((ts, C), lambda i: (i, 0)),
            pl.BlockSpec((4, ts // 4, C), lambda i: (0, i, 0)),
            pl.BlockSpec((16, ts // 16, C), lambda i: (0, i, 0))]


def _perm_shapes(S, C, dtype):
    return [SDS((S, C), dtype), SDS((4, S // 4, C), dtype), SDS((16, S // 16, C), dtype)]


def combine_fwd(os_, lses, z, ehot):
    S = z.shape[0]
    ts = 256

    def body(o0, o1, o2, l0, l1, l2, z_ref, e_ref, y0, y1, y2, s0, s1, s2, ya_ref, yat_ref,
             so1, so2, sl1, sl2, sy, sl):
        _load_perm(so1, o1, 4)
        _load_perm(so2, o2, 16)
        _load_perm(sl1, l1, 4)
        _load_perm(sl2, l2, 16)
        ls = [l0[...], sl1[0], sl2[0]]
        m = jnp.maximum(jnp.maximum(ls[0], ls[1]), ls[2])
        es = [jnp.exp(l - m) for l in ls]
        den = es[0] + es[1] + es[2]
        sl[0] = m + jnp.log(den)
        y = None
        for e, o in zip(es, (o0[...].astype(F32), _scr_get(so1), _scr_get(so2))):
            w = e / den
            hi = w.astype(BF16)
            lo = (w - hi.astype(F32)).astype(BF16)
            wb = _dot(hi, e_ref[...]) + _dot(lo, e_ref[...])
            y = wb * o if y is None else y + wb * o
        z = z_ref[...].astype(F32)
        ya = y * (z * _sigmoid(z))
        ya_ref[...] = ya.astype(BF16)
        yat_ref[...] = ya.T.astype(BF16)
        _scr_put(sy, y)
        y0[...] = y.astype(BF16)
        _store_perm(y1, sy, 4)
        _store_perm(y2, sy, 16)
        s0[...] = sl[0]
        _store_perm(s1, sl, 4)
        _store_perm(s2, sl, 16)

    wide = pl.BlockSpec((ts, E), lambda i: (i, 0))
    os3 = [os_[0], os_[1].reshape(4, S // 4, E), os_[2].reshape(16, S // 16, E)]
    ls3 = [lses[0], lses[1].reshape(4, S // 4, HD), lses[2].reshape(16, S // 16, HD)]
    res = pl.pallas_call(
        body, name="combine_fwd", grid=(S // ts,),
        in_specs=_perm_specs(ts, E) + _perm_specs(ts, HD) + [wide, pl.BlockSpec((HD, E), lambda i: (0, 0))],
        out_specs=_perm_specs(ts, E) + _perm_specs(ts, HD) + [wide, pl.BlockSpec((E, ts), lambda i: (0, i))],
        out_shape=_perm_shapes(S, E, BF16) + _perm_shapes(S, HD, F32) + [SDS((S, E), BF16), SDS((E, S), BF16)],
        scratch_shapes=[_scr(ts, E), _scr(ts, E), _scr(ts, HD), _scr(ts, HD), _scr(ts, E), _scr(ts, HD)],
        compiler_params=_cp(("parallel",), 56),
    )(*os3, *ls3, z, ehot)
    ys = [res[0], res[1].reshape(S, E), res[2].reshape(S, E)]
    lse3 = [res[3], res[4].reshape(S, HD), res[5].reshape(S, HD)]
    return ys, lse3, res[6], res[7]


def mm_dya(da0, w_out, z, y):
    S = da0.shape[0]
    tm = 512

    def body(a_ref, w_ref, z_ref, y_ref, dy0, dy1, dy2, dz_ref, scr):
        for cidx in range(E // 256):
            col = slice(cidx * 256, (cidx + 1) * 256)
            dya = _dot_nt(a_ref[...], w_ref[col, :])
            zz = z_ref[:, col].astype(F32)
            sig = _sigmoid(zz)
            dy = dya * zz * sig
            scr[2 * cidx] = dy[:, :LANES]
            scr[2 * cidx + 1] = dy[:, LANES:]
            dy0[:, col] = dy.astype(BF16)
            dz_ref[:, col] = (dya * y_ref[:, col].astype(F32) * sig * (1.0 + zz * (1.0 - sig))).astype(BF16)
        _store_perm(dy1, scr, 4)
        _store_perm(dy2, scr, 16)

    wide = pl.BlockSpec((tm, E), lambda i: (i, 0))
    res = pl.pallas_call(
        body, name="mm_dya", grid=(S // tm,),
        in_specs=[pl.BlockSpec((tm, D), lambda i: (i, 0)), pl.BlockSpec((E, D), lambda i: (0, 0)), wide, wide],
        out_specs=_perm_specs(tm, E) + [wide],
        out_shape=_perm_shapes(S, E, BF16) + [SDS((S, E), BF16)],
        scratch_shapes=[_scr(tm, E)],
        compiler_params=_cp(("parallel",), 48),
    )(da0, w_out, z, y)
    return [res[0], res[1].reshape(S, E), res[2].reshape(S, E)], res[3]


def attn_bwd(P, dy, y, lse, tabs, g, d):
    S = P.shape[0]
    L = S // d
    T = min(512, L)
    nq = T // QB
    ni = L // T

    def body(qkv_ref, kh_ref, vh_ref, dy_ref, y_ref, lse_ref, c_ref, s1_ref, s2_ref,
             o_ref, dkc_ref, dvc_ref, ext):
        i = pl.program_id(1)
        col, mask, mask_first = _band_masks(i < ni - 1)
        _fill_kv(ext, qkv_ref, kh_ref, vh_ref)

        @pl.when(i == 0)
        def _():
            dkc_ref[...] = jnp.zeros_like(dkc_ref)
            dvc_ref[...] = jnp.zeros_like(dvc_ref)

        def heads(hp, carry):
            def cols(h):
                off = pl.multiple_of(h * HD, HD)
                return pl.ds(off, HD), pl.ds(E + off, HD), pl.ds(2 * E + off, HD)

            def front(h, j):
                cq, ck, _ = cols(h)
                rows = slice(j * QB, (j + 1) * QB)
                krows = slice(j * QB, (j + 2) * QB)
                dyj = dy_ref[rows, cq]
                lse_h = jnp.sum(jnp.where(col == h, lse_ref[rows, :], 0.0), axis=1, keepdims=True)
                delta = jnp.sum(dyj.astype(F32) * y_ref[rows, cq].astype(F32), axis=1, keepdims=True)
                s = _dot_nt(qkv_ref[rows, cq], ext[krows, cq])
                p = jnp.exp(s * SCALE + (mask_first if j == 0 else mask) - lse_h)
                ds = (p * (_dot_nt(dyj, ext[krows, ck]) - delta) * SCALE).astype(BF16)
                return ds, ds.T, p.astype(BF16).T

            def back(h, j, ds, ds_t, p_t, pend_dk, pend_dv):
                cq, ck, cv = cols(h)
                rows = slice(j * QB, (j + 1) * QB)
                krows = slice(j * QB, (j + 2) * QB)
                dq = _dot(ds, ext[krows, cq])
                dk2 = _dot(ds_t, qkv_ref[rows, cq])
                dv2 = _dot(p_t, dy_ref[rows, cq])
                c, s1, s2 = c_ref[rows, :], s1_ref[rows, :], s2_ref[rows, :]
                o_ref[rows, cq] = _unrope(dq, c, s1, s2).astype(BF16)
                o_ref[rows, ck] = _unrope(dk2[QB:] + pend_dk, c, s1, s2).astype(BF16)
                o_ref[rows, cv] = (dv2[QB:] + pend_dv).astype(BF16)
                return dk2[:QB], dv2[:QB]

            items = [(HEADS_PER_STEP * hp + hh, j) for hh in range(HEADS_PER_STEP) for j in reversed(range(nq))]
            queue = [front(*it) for it in items[:AHEAD]]
            pend = None
            for u, (h, j) in enumerate(items):
                if u + AHEAD < len(items):
                    queue.append(front(*items[u + AHEAD]))
                if j == nq - 1:
                    pend = (dkc_ref[:, cols(h)[0]], dvc_ref[:, cols(h)[0]])
                pend = back(h, j, *queue.pop(0), *pend)
                if j == 0:
                    dkc_ref[:, cols(h)[0]], dvc_ref[:, cols(h)[0]] = pend
            return carry

        lax.fori_loop(0, NH // HEADS_PER_STEP, heads, 0)

    blk = lambda r, i: r * ni + ni - 1 - i
    halo = lambda r, i: jnp.maximum(r * (L // QB) + (ni - 1 - i) * nq - 1, 0)
    main = pl.BlockSpec((T, SEG), lambda r, i: (blk(r, i), 0))
    wide = pl.BlockSpec((T, E), lambda r, i: (blk(r, i), 0))
    narrow = pl.BlockSpec((T, HD), lambda r, i: (blk(r, i), 0))
    return pl.pallas_call(
        body, name=f"attn_bwd{g}", grid=(d, ni),
        in_specs=[main, pl.BlockSpec((QB, E), lambda r, i: (halo(r, i), 1)),
                  pl.BlockSpec((QB, E), lambda r, i: (halo(r, i), 2)),
                  wide, wide, narrow, narrow, narrow, narrow],
        out_specs=main, out_shape=SDS((S, SEG), BF16),
        scratch_shapes=[pltpu.VMEM((QB, E), F32), pltpu.VMEM((QB, E), F32), pltpu.VMEM((T + QB, 2 * E), BF16)],
        compiler_params=_cp(("arbitrary", "arbitrary"), 56),
    )(P, P, P, dy, y, lse, *tabs)


def _pool_cnt(t0, rows):
    t = (lax.broadcasted_iota(jnp.int32, (rows, E), 0) + t0 + 1).astype(F32)
    ch = lax.broadcasted_iota(jnp.int32, (rows, E), 1)
    w = jnp.where(ch < PC, 2.0, jnp.where(ch < 2 * PC, 4.0, jnp.where(ch < 3 * PC, 8.0, 16.0)))
    return jnp.minimum(t, w)


def _by_group(parts):
    return jnp.concatenate([parts[g][:, g * PC:(g + 1) * PC] for g in range(4)], axis=1)


def pool_fwd(uz):
    S = uz.shape[0]
    ts = 256

    def body(u_ref, h_ref, o_ref, ot_ref):
        i = pl.program_id(0)
        u = u_ref[...]
        halo = jnp.where(i > 0, h_ref[...], 0.0)
        ext = jnp.concatenate([halo, u], axis=0)
        s2 = ext + pltpu.roll(ext, 1, 0)
        s4 = s2 + pltpu.roll(s2, 2, 0)
        s8 = s4 + pltpu.roll(s4, 4, 0)
        s16 = s8 + pltpu.roll(s8, 8, 0)
        win = _by_group([s2, s4, s8, s16])[16:, :]
        pooled = win / _pool_cnt(i * ts, ts) - u
        o_ref[...] = pooled.astype(BF16)
        ot_ref[...] = pooled.T.astype(BF16)

    return pl.pallas_call(
        body, name="pool_fwd", grid=(S // ts,),
        in_specs=[pl.BlockSpec((ts, E), lambda i: (i, 0)),
                  pl.BlockSpec((16, E), lambda i: (jnp.maximum(i * (ts // 16) - 1, 0), 0))],
        out_specs=[pl.BlockSpec((ts, E), lambda i: (i, 0)), pl.BlockSpec((E, ts), lambda i: (0, i))],
        out_shape=[SDS((S, E), BF16), SDS((E, S), BF16)],
        compiler_params=_cp(("parallel",), 48),
    )(uz, uz)


def pool_bwd(dpooled, duz):
    S = dpooled.shape[0]
    ts = 256
    nt = S // ts

    def body(d_ref, h_ref, alias_ref, o_ref):
        i = pl.program_id(0)
        dp = d_ref[...].astype(F32)
        halo = jnp.where(i < nt - 1, h_ref[...].astype(F32), 0.0)
        n = ts + 16
        ext = jnp.concatenate([dp, halo], axis=0) / _pool_cnt(i * ts, n)
        f2 = ext + pltpu.roll(ext, n - 1, 0)
        f4 = f2 + pltpu.roll(f2, n - 2, 0)
        f8 = f4 + pltpu.roll(f4, n - 4, 0)
        f16 = f8 + pltpu.roll(f8, n - 8, 0)
        win = _by_group([f2, f4, f8, f16])[:ts, :]
        o_ref[...] = (win - dp).astype(BF16)

    return pl.pallas_call(
        body, name="pool_bwd", grid=(nt,),
        in_specs=[pl.BlockSpec((ts, E), lambda i: (i, 0)),
                  pl.BlockSpec((16, E), lambda i: (jnp.minimum((i + 1) * (ts // 16), S // 16 - 1), 0)), ANY],
        out_specs=pl.BlockSpec((ts, E), lambda i: (i, 0)),
        out_shape=SDS(duz.shape, BF16),
        input_output_aliases={2: 0},
        compiler_params=_cp(("parallel",), 48),
    )(dpooled, dpooled, duz)


def mm_grp(pooled, wg, b, scale, uz):
    S = pooled.shape[0]
    tm = 512

    def body(p_ref, w_ref, b_ref, s_ref, z_ref, h_ref, y_ref, yt_ref):
        for g in range(4):
            cs = slice(g * PC, (g + 1) * PC)
            h = _dot(p_ref[:, cs], w_ref[g]) + b_ref[:, cs]
            z = z_ref[:, cs]
            yp = h * s_ref[:, cs] * (z * _sigmoid(z))
            h_ref[:, cs] = h.astype(BF16)
            y_ref[:, cs] = yp.astype(BF16)
            yt_ref[cs, :] = yp.T.astype(BF16)

    row = pl.BlockSpec((tm, E), lambda i: (i, 0))
    vec = pl.BlockSpec((1, E), lambda i: (0, 0))
    return pl.pallas_call(
        body, name="mm_grp", grid=(S // tm,),
        in_specs=[row, pl.BlockSpec((4, PC, PC), lambda i: (0, 0, 0)), vec, vec,
                  pl.BlockSpec((tm, E), lambda i: (i, 1))],
        out_specs=[row, row, pl.BlockSpec((E, tm), lambda i: (0, i))],
        out_shape=[SDS((S, E), BF16), SDS((S, E), BF16), SDS((E, S), BF16)],
        compiler_params=_cp(("parallel",), 48),
    )(pooled, wg, b, scale, uz)


def mm_dyp(da1, w_out, uz, h, scale):
    S = da1.shape[0]
    tm = 512

    def body(a_ref, w_ref, z_ref, h_ref, s_ref, dh_ref, dz_ref, dsc_ref, db_ref):
        @pl.when(pl.program_id(0) == 0)
        def _():
            dsc_ref[...] = jnp.zeros_like(dsc_ref)
            db_ref[...] = jnp.zeros_like(db_ref)

        for cidx in range(E // 256):
            col = slice(cidx * 256, (cidx + 1) * 256)
            dyp = _dot_nt(a_ref[...], w_ref[col, :])
            z = z_ref[:, col]
            hh = h_ref[:, col].astype(F32)
            sc = s_ref[:, col]
            sig = _sigmoid(z)
            dhs = dyp * z * sig
            dz_ref[:, col] = (dyp * hh * sc * sig * (1.0 + z * (1.0 - sig))).astype(BF16)
            dh = dhs * sc
            dh_ref[:, col] = dh.astype(BF16)
            dsc_ref[:, col] += _fold8(dhs * hh)
            db_ref[:, col] += _fold8(dh)

    row = pl.BlockSpec((tm, E), lambda i: (i, 0))
    acc = pl.BlockSpec((8, E), lambda i: (0, 0))
    return pl.pallas_call(
        body, name="mm_dyp", grid=(S // tm,),
        in_specs=[pl.BlockSpec((tm, D), lambda i: (i, 0)), pl.BlockSpec((E, D), lambda i: (0, 0)),
                  pl.BlockSpec((tm, E), lambda i: (i, 1)), row, pl.BlockSpec((1, E), lambda i: (0, 0))],
        out_specs=[row, pl.BlockSpec((tm, E), lambda i: (i, 1)), acc, acc],
        out_shape=[SDS((S, E), BF16), SDS((S, 2 * E), BF16), SDS((8, E), F32), SDS((8, E), F32)],
        compiler_params=_cp(("arbitrary",), 48),
    )(da1, w_out, uz, h, scale)


def mm_dpooled(dh, wg):
    S = dh.shape[0]
    tm = 1024

    def body(a_ref, w_ref, o_ref):
        for g in range(4):
            cs = slice(g * PC, (g + 1) * PC)
            o_ref[:, cs] = _dot_nt(a_ref[:, cs], w_ref[g]).astype(BF16)

    row = pl.BlockSpec((tm, E), lambda i: (i, 0))
    return pl.pallas_call(
        body, name="mm_dpooled", grid=(S // tm,),
        in_specs=[row, pl.BlockSpec((4, PC, PC), lambda i: (0, 0, 0))],
        out_specs=row, out_shape=SDS((S, E), BF16),
        compiler_params=_cp(("parallel",), 48),
    )(dh, wg)


def _rope_tables(positions):
    inv_freq = 500000.0 ** (-jnp.arange(0, 32, 2, dtype=F32) / 32)
    ang = positions.astype(F32)[:, None] * inv_freq
    cos, sin = jnp.cos(ang), jnp.sin(ang)
    S = positions.shape[0]
    one = jnp.ones((S, HD - 32), F32)
    zero16 = jnp.zeros((S, 16), F32)
    zero = jnp.zeros((S, HD - 32), F32)
    c = jnp.concatenate([cos, cos, one], axis=1)
    s1 = jnp.concatenate([-sin, zero16, zero], axis=1)
    s2 = jnp.concatenate([zero16, sin, zero], axis=1)
    return c.astype(BF16), s1.astype(BF16), s2.astype(BF16)


def kernel(x, positions, norm_pre, norm_post, attn_w_in, attn_w_out, pool_w_in, pool_w_grp, pool_b_grp, pool_scale, pool_w_out, loss_target, m_norm_pre, m_norm_post, m_attn_w_in, m_attn_w_out, m_pool_w_in, m_pool_w_grp, m_pool_b_grp, m_pool_scale, m_pool_w_out, v_norm_pre, v_norm_post, v_attn_w_in, v_attn_w_out, v_pool_w_in, v_pool_w_grp, v_pool_b_grp, v_pool_scale, v_pool_w_out):
    S = x.shape[1]
    xi, yi, ci = _mesh_pos()
    dev = 4 * xi + 2 * yi + ci
    x2 = x[0]
    tgt = loss_target[0]

    small = jnp.concatenate([pool_b_grp[0].reshape(2, HD), pool_scale[0].reshape(2, HD),
                             jnp.zeros((4, HD), F32)], axis=0)
    (w_in8,) = all_gather([attn_w_in[0].astype(BF16)], "gather_w_in")
    small, w_in8 = lax.optimization_barrier((small, w_in8))
    rest_l = [attn_w_out[0].astype(BF16), pool_w_in[0].astype(BF16), pool_w_grp[0].astype(BF16),
              pool_w_out[0].astype(BF16), small]
    rest_flight, rest_token = split_start(
        "gather_rest_start", rest_l, [lax.empty((N_DEV,) + a.shape, a.dtype) for a in rest_l], _peers_plan(), 7)

    pos = positions[0]
    tabs = [_rope_tables(pos.reshape(S // d, d).T.reshape(S)) for d in DIL]
    ehot = (jnp.arange(E)[None, :] // HD == jnp.arange(HD)[:, None]).astype(BF16)
    seg_tiles = SEG // CT

    xn0, xn0_4, xn0_16, xn0t = norm_pre0(x2, norm_pre[0:1])
    xn0s = [xn0, xn0_4.reshape(S, D), xn0_16.reshape(S, D)]
    xn0ts = [xn0t, transpose_rows(xn0s[1], "xn0t_4"), transpose_rows(xn0s[2], "xn0t_16")]
    Ps, os_, lses = [], [], []
    for g, d in enumerate(DIL):
        P = mm_in(xn0s[g], w_in8, g * seg_tiles, seg_tiles, tabs[g], f"mm_qkv{g}", after=rest_token)
        o, l = attn_fwd(P, g, d)
        Ps.append(P)
        os_.append(o)
        lses.append(l)
    z0 = mm_in(xn0, w_in8, 3 * seg_tiles, E // CT, None, "mm_z0")
    ys, lse3, ya, yat = combine_fwd(os_, lses, z0, ehot)

    rest_l, rest8 = split_wait("gather_rest_wait", rest_flight, _peers_plan(), ya)
    rest8 = [lax.dynamic_update_slice(r8, a[None], (dev,) + (0,) * a.ndim) for r8, a in zip(rest8, rest_l)]
    w_out8, wp_in8, wg8, wp_out8, small8 = rest8
    w_out = w_out8.reshape(E, D)
    wp_out = wp_out8.reshape(E, D)
    wp_in = wp_in8.transpose(1, 0, 2).reshape(D, 2 * E)
    wg = wg8.transpose(1, 0, 2, 3).reshape(4, PC, PC)
    b_full = small8[:, 0:2, :].reshape(N_DEV, 4, PC // N_DEV).transpose(1, 0, 2).reshape(1, E)
    scale_full = small8[:, 2:4, :].reshape(1, E)
    a0 = mm_rows(ya, w_out, "mm_out0", F32)
    h1, xn1, xn1t = post0_pre1(x2, a0, norm_post[0:1], norm_pre[1:2])

    uz = mm_rows(xn1, wp_in, "mm_uz", F32, tm=512)
    pooled, pooled_t = pool_fwd(uz)
    hgrp, yp, ypt = mm_grp(pooled, wg, b_full, scale_full, uz)
    a1 = mm_rows(yp, wp_out, "mm_out1", F32)
    dh2, da1, loss_rows, dg_post1 = post1_loss(h1, a1, tgt, norm_post[1:2])
    loss = lax.psum(0.5 / D * jnp.sum(loss_rows), ("x", "y", "c"))

    dh, duz, dscale_p, db_p = mm_dyp(da1, wp_out, uz, hgrp, scale_full)
    dpooled = mm_dpooled(dh, wg)
    duz = pool_bwd(dpooled, duz)
    g_wg = mm_dwg(pooled_t, dh)
    g_wp_out = mm_wgrad_rows(ypt, da1, "mm_dwp_out")
    g_wp_in = mm_wgrad_cols(xn1t, duz, "mm_dwp_in", ncols=2 * E, shard=PC, tn=1024)
    dxn1 = mm_dx_full(duz, wp_in, "mm_dxn1")
    dh1, da0, dg_pre1, dg_post0 = mid_bwd(dxn1, dh2, h1, a0, norm_pre[1:2], norm_post[0:1])

    dys, dz0 = mm_dya(da0, w_out, z0, ys[0])
    g_w_out = mm_wgrad_rows(yat, da0, "mm_dw_out")
    g_w_in = mm_dw_in_part(xn0t, dz0, 3 * seg_tiles, None, "mm_dw_in_z")
    dPs = []
    for g, d in enumerate(DIL):
        dP = attn_bwd(Ps[g], dys[g], ys[g], lse3[g], tabs[g], g, d)
        g_w_in = mm_dw_in_part(xn0ts[g], dP, g * seg_tiles, g_w_in, f"mm_dw_in{g}")
        dPs.append(dP)

    cidx = ci.astype(jnp.int32).reshape(1)
    chip = (2 * xi + yi).astype(jnp.int32).reshape(1)
    fulls = [g_w_in, g_w_out.reshape(N_DEV, E // N_DEV, D), g_wp_in,
             g_wg.reshape(4, N_DEV, PC // N_DEV, PC).transpose(1, 0, 2, 3).reshape(N_DEV, 4 * PC // N_DEV, PC),
             g_wp_out.reshape(N_DEV, E // N_DEV, D)]
    pair_flight, pair_token = split_start(
        "rs_pair_start", fulls, [lax.empty((4,) + f.shape[1:], F32) for f in fulls], _pair_plan(), 4)
    dx_z = mm_dx_part(dz0, w_in8, 3 * seg_tiles, "mm_dxn0_z", after=pair_token)
    dx_0 = mm_dx_part(dPs[0], w_in8, 0, "mm_dxn0_0", after=dx_z)
    fulls, sibs = split_wait("rs_pair_wait", pair_flight, _pair_plan(), dx_0)
    parts = [pair_add(f, s, cidx, f"pair_add{k}") for k, (f, s) in enumerate(zip(fulls, sibs))]
    chips_flight, chips_token = split_start(
        "rs_chips_start", parts, [jnp.zeros(p.shape, BF16) for p in parts], _chips_plan(), 3)
    dx_1 = mm_dx_part(dPs[1], w_in8, seg_tiles, "mm_dxn0_1", after=chips_token)
    dx_2 = mm_dx_part(dPs[2], w_in8, 2 * seg_tiles, "mm_dxn0_2", after=dx_1)
    grad_x, dg_pre0 = pre0_bwd(dx_0, dx_z, dx_1, dx_2, dh1, x2, norm_pre[0:1])
    parts, recvs = split_wait("rs_chips_wait", chips_flight, _chips_plan(), grad_x)
    shards = [(attn_w_in, m_attn_w_in, v_attn_w_in), (attn_w_out, m_attn_w_out, v_attn_w_out),
              (pool_w_in, m_pool_w_in, v_pool_w_in), (pool_w_grp, m_pool_w_grp, v_pool_w_grp),
              (pool_w_out, m_pool_w_out, v_pool_w_out)]
    big = []
    for k, (recv, part, (w, m, v)) in enumerate(zip(recvs, parts, shards)):
        shp = w.shape
        r2 = recv.shape[1:]
        res = adamw_sum(recv, part, chip, w.reshape(r2), m.reshape(r2), v.reshape(r2), f"adamw{k}")
        big.append([t.reshape(shp) for t in res])

    smalls = jnp.concatenate([dg_pre0.sum(0, keepdims=True), dg_pre1.sum(0, keepdims=True),
                              dg_post0.sum(0, keepdims=True), dg_post1.sum(0, keepdims=True),
                              db_p.sum(0).reshape(2, D), dscale_p.sum(0).reshape(2, D)], axis=0)
    (smalls8,) = all_gather([smalls], "gather_small_grads")
    tot = sum_slots(smalls8, "sum_small_grads")
    g_norm_pre, g_norm_post = tot[0:2], tot[2:4]
    g_b = lax.dynamic_slice_in_dim(tot[4:6].reshape(4, PC), dev * (PC // N_DEV), PC // N_DEV, axis=1)[None]
    g_scale = lax.dynamic_slice_in_dim(tot[6:8].reshape(1, E), dev * (E // N_DEV), E // N_DEV, axis=1)
    sm = [adamw_small(g_norm_pre, norm_pre, m_norm_pre, v_norm_pre, "adamw_norm_pre"),
          adamw_small(g_norm_post, norm_post, m_norm_post, v_norm_post, "adamw_norm_post"),
          adamw_small(g_b, pool_b_grp, m_pool_b_grp, v_pool_b_grp, "adamw_b"),
          adamw_small(g_scale, pool_scale, m_pool_scale, v_pool_scale, "adamw_scale")]

    grads = [g_norm_pre, g_norm_post, big[0][0], big[1][0], big[2][0], big[3][0], g_b, g_scale, big[4][0]]

    def pick(k):
        return [sm[0][k - 1], sm[1][k - 1], big[0][k], big[1][k], big[2][k], big[3][k], sm[2][k - 1], sm[3][k - 1],
                big[4][k]]

    return (loss, grad_x[None], *grads, *pick(1), *pick(2), *pick(3))
```

```python
import math

import jax
import jax.numpy as jnp
from jax import lax
from jax.experimental import pallas as pl
from jax.experimental.pallas import tpu as pltpu

F32 = jnp.float32
BF16 = jnp.bfloat16
SDS = jax.ShapeDtypeStruct

N_DEV = 8
D = 1024
E = 2048
HD = 128
NH = E // HD
DIL = (1, 4, 16)
QB = 128
SEG = 3 * E
W_IN_COLS = 3 * SEG + E
W_SHARD = W_IN_COLS // N_DEV
CT = 512
PC = E // 4
EPS = 1e-6
NEG = -1e30
SCALE = 1.0 / math.sqrt(HD)
LR, B1, B2, ADAM_EPS, WD, STEP = 0.001, 0.9, 0.999, 1e-08, 0.01, 10
MIB = 1024 * 1024
ANY = pl.BlockSpec(memory_space=pl.ANY)
MESH = pl.DeviceIdType.MESH


def _cp(sem, mb):
    return pltpu.CompilerParams(dimension_semantics=sem, vmem_limit_bytes=mb * MIB)


def _dot(a, b):
    return jnp.dot(a, b, preferred_element_type=F32)


def _dot_nt(a, b):
    return lax.dot_general(a, b, (((1,), (1,)), ((), ())), preferred_element_type=F32)


def _rms(h):
    return lax.rsqrt(jnp.mean(h * h, axis=-1, keepdims=True) + EPS)


def _row_tile(R, C, budget):
    tr = R
    while tr * C * 4 > budget and tr % 16 == 0:
        tr //= 2
    return tr


def _fold8(t):
    return t.reshape(t.shape[0] // 8, 8, t.shape[1]).sum(axis=0)


def _sigmoid(z):
    return pl.reciprocal(1.0 + jnp.exp(-z), approx=True)


LANES = 128


def _scr(rows, C):
    return pltpu.VMEM((C // LANES, rows, LANES), F32)


def _scr_put(scr, val):
    for c in range(scr.shape[0]):
        scr[c] = val[:, c * LANES:(c + 1) * LANES]


def _scr_get(scr):
    return jnp.concatenate([scr[c] for c in range(scr.shape[0])], axis=1)


def _store_perm(dst_ref, scr, d):
    n = dst_ref.shape[1]
    for r in range(d):
        for c in range(scr.shape[0]):
            dst_ref[r, :, c * LANES:(c + 1) * LANES] = scr[c, pl.ds(r, n, stride=d), :].astype(dst_ref.dtype)


def _load_perm(scr, src_ref, d, add=False):
    n = src_ref.shape[1]
    for r in range(d):
        rows = pl.ds(r, n, stride=d)
        for c in range(scr.shape[0]):
            v = src_ref[r, :, c * LANES:(c + 1) * LANES].astype(F32)
            scr[c, rows, :] = scr[c, rows, :] + v if add else v


def _rope(t, c, s1, s2):
    t = t.astype(BF16)
    return t * c + pltpu.roll(t, HD - 16, 1) * s1 + pltpu.roll(t, 16, 1) * s2


def _unrope(t, c, s1, s2):
    t = t.astype(BF16)
    return t * c - pltpu.roll(t, HD - 16, 1) * s1 - pltpu.roll(t, 16, 1) * s2


def _mesh_pos():
    return lax.axis_index("x"), lax.axis_index("y"), lax.axis_index("c")


def all_gather(arrs, name):
    n = len(arrs)

    def body(*refs):
        ins, outs = refs[:n], refs[n:2 * n]
        send_sems, recv_sems, local_sems = refs[2 * n:]
        x, y, c = _mesh_pos()
        me, sib = (x, y, c), (x, y, 1 - c)
        chips = [(1 - x, y), (x, 1 - y), (1 - x, 1 - y)]

        def slot(p):
            return 4 * p[0] + 2 * p[1] + p[2]

        def copy(a, k, block, to, src=None):
            dst = outs[a].at[slot(block)]
            return pltpu.make_async_remote_copy(
                src_ref=dst if src is None else src, dst_ref=dst,
                send_sem=send_sems.at[a, k], recv_sem=recv_sems.at[a, k],
                device_id=to, device_id_type=MESH)

        mine = [pltpu.make_async_copy(ins[a], outs[a].at[slot(me)], local_sems.at[a]) for a in range(n)]
        for cp in mine:
            cp.start()
        first = []
        for a in range(n):
            first.append(copy(a, 0, me, sib, src=ins[a]))
            for j, chip in enumerate(chips):
                first.append(copy(a, 1 + j, me, (*chip, c), src=ins[a]))
        for cp in first:
            cp.start()
        passed = []
        for j, chip in enumerate(chips):
            for a in range(n):
                copy(a, 1 + j, (*chip, c), me).wait_recv()
                fw = copy(a, 4 + j, (*chip, c), sib)
                fw.start()
                passed.append(fw)
        for a in range(n):
            copy(a, 0, sib, me).wait_recv()
        for j, chip in enumerate(chips):
            for a in range(n):
                copy(a, 4 + j, (*chip, 1 - c), me).wait_recv()
        for cp in first + passed:
            cp.wait_send()
        for cp in mine:
            cp.wait()

    return pl.pallas_call(
        body, name=name,
        out_shape=[SDS((N_DEV,) + a.shape, a.dtype) for a in arrs],
        in_specs=[ANY] * n, out_specs=[ANY] * n,
        scratch_shapes=[pltpu.SemaphoreType.DMA((n, 7)), pltpu.SemaphoreType.DMA((n, 7)),
                        pltpu.SemaphoreType.DMA((n,))],
    )(*arrs)


HBM_SPEC = pl.BlockSpec(memory_space=pltpu.HBM)
SEM_SPEC = pl.BlockSpec(memory_space=pltpu.SEMAPHORE)
EFFECT = pltpu.SideEffectType.DATAFLOW_SIDE_EFFECTING


def _pair_plan():
    def plan(x, y, c):
        return [(2 * q + (1 - c), q, (x, y, 1 - c)) for q in range(4)]
    return plan


def _chips_plan():
    def plan(x, y, c):
        chips = [(1 - x, y), (x, 1 - y), (1 - x, 1 - y)]
        return [(2 * cx + cy, 2 * x + y, (cx, cy, c)) for cx, cy in chips]
    return plan


def _first_hop_plan():
    def plan(x, y, c):
        me = 4 * x + 2 * y + c
        return [(None, me, (x, y, 1 - c))] + [(None, me, (cx, cy, c)) for cx, cy in
                                              [(1 - x, y), (x, 1 - y), (1 - x, 1 - y)]]
    return plan


def _second_hop_plan():
    def plan(x, y, c):
        chips = [(1 - x, y), (x, 1 - y), (1 - x, 1 - y)]
        return [(4 * cx + 2 * cy + c, 4 * cx + 2 * cy + c, (x, y, 1 - c)) for cx, cy in chips]
    return plan


def _peers_plan():
    def plan(x, y, c):
        out = []
        for k in range(1, N_DEV):
            fx, fy, fc = (k >> 2) & 1, (k >> 1) & 1, k & 1
            px, py, pc = (x + fx) % 2, (y + fy) % 2, (c + fc) % 2
            out.append((None, 4 * x + 2 * y + c, (px, py, pc)))
        return out
    return plan


def _split_copies(plan, srcs, lands, send_sems, recv_sems):
    x, y, c = _mesh_pos()
    cps = []
    for a, (src, land) in enumerate(zip(srcs, lands)):
        steps = plan(x, y, c)
        for k, (si, li, to) in enumerate(steps):
            sem = a * len(steps) + k
            cps.append(pltpu.make_async_remote_copy(
                src_ref=src if si is None else src.at[si], dst_ref=land.at[li],
                send_sem=send_sems.at[sem], recv_sem=recv_sems.at[sem],
                device_id=to, device_id_type=MESH))
    return cps


def split_start(name, srcs, lands, plan, nk):
    n = len(srcs)
    ops = list(srcs) + ([] if lands is None else list(lands))
    nb = len(ops)

    def body(*refs):
        token = refs[-1]
        for cp in _split_copies(plan, refs[:n], refs[nb - n:nb], refs[nb], refs[nb + 1]):
            cp.start()
        token[...] = jnp.zeros_like(token)

    ops = [pltpu.with_memory_space_constraint(a, pltpu.HBM) for a in ops]
    res = pl.pallas_call(
        body, name=name,
        out_shape=(pltpu.SemaphoreType.DMA((n * nk,)), pltpu.SemaphoreType.DMA((n * nk,)),
                   *[pltpu.HBM(a.shape, a.dtype) for a in ops], SDS((8, 128), F32)),
        in_specs=[HBM_SPEC] * nb,
        out_specs=(SEM_SPEC, SEM_SPEC, *[HBM_SPEC] * nb, pl.BlockSpec(memory_space=pltpu.VMEM)),
        input_output_aliases={i: 2 + i for i in range(nb)},
        compiler_params=pltpu.CompilerParams(has_side_effects=EFFECT),
    )(*ops)
    return res[:-1], res[-1]


def split_wait(name, flight, plan, after, inplace=False):
    send_sems, recv_sems = flight[0], flight[1]
    bufs = list(flight[2:])
    nb = len(bufs)
    n = nb if inplace else nb // 2

    def body(*refs):
        for cp in _split_copies(plan, refs[:n], refs[nb - n:nb], refs[nb], refs[nb + 1]):
            cp.wait_send()
            cp.wait_recv()

    res = pl.pallas_call(
        body, name=name,
        out_shape=[pltpu.HBM(a.shape, a.dtype) for a in bufs],
        in_specs=[HBM_SPEC] * nb + [SEM_SPEC, SEM_SPEC, ANY],
        out_specs=[HBM_SPEC] * nb,
        input_output_aliases={i: i for i in range(nb)},
        compiler_params=pltpu.CompilerParams(has_side_effects=EFFECT),
    )(*bufs, send_sems, recv_sems, after)
    return res[:n], res[nb - n:]


def pair_add(full, sib, cidx, name):
    _, R, C = full.shape
    tr = _row_tile(R, C, MIB)

    def body(c_ref, a_ref, b_ref, o_ref):
        o_ref[...] = (a_ref[...] + b_ref[...]).astype(BF16)

    return pl.pallas_call(
        body, name=name,
        grid_spec=pltpu.PrefetchScalarGridSpec(
            num_scalar_prefetch=1, grid=(4, R // tr),
            in_specs=[pl.BlockSpec((None, tr, C), lambda q, i, cr: (2 * q + cr[0], i, 0)),
                      pl.BlockSpec((None, tr, C), lambda q, i, cr: (q, i, 0))],
            out_specs=pl.BlockSpec((None, tr, C), lambda q, i, cr: (q, i, 0))),
        out_shape=SDS((4, R, C), BF16),
        compiler_params=_cp(("parallel", "parallel"), 32),
    )(cidx, full, sib)


def _adam_math(w, g, m, v):
    m2 = B1 * m + (1.0 - B1) * g
    v2 = B2 * v + (1.0 - B2) * (g * g)
    m_hat = m2 / (1.0 - B1 ** STEP)
    v_hat = v2 / (1.0 - B2 ** STEP)
    delta = -LR * (m_hat / (jnp.sqrt(v_hat) + ADAM_EPS) + WD * w)
    return delta, m2, v2


def adamw_sum(recv, part, chip, w, m, v, name):
    K, R, C = recv.shape
    tr = _row_tile(R, C, MIB)

    def body(chip_ref, r_ref, p_ref, w_ref, m_ref, v_ref, g_ref, d_ref, m2_ref, v2_ref):
        g = r_ref[0].astype(F32)
        for k in range(1, K):
            g = g + r_ref[k].astype(F32)
        g = g + p_ref[...].astype(F32)
        delta, m2, v2 = _adam_math(w_ref[...], g, m_ref[...], v_ref[...])
        g_ref[...] = g
        d_ref[...] = delta
        m2_ref[...] = m2
        v2_ref[...] = v2

    tile = pl.BlockSpec((tr, C), lambda i, cr: (i, 0))
    return pl.pallas_call(
        body, name=name,
        grid_spec=pltpu.PrefetchScalarGridSpec(
            num_scalar_prefetch=1, grid=(R // tr,),
            in_specs=[pl.BlockSpec((K, tr, C), lambda i, cr: (0, i, 0)),
                      pl.BlockSpec((None, tr, C), lambda i, cr: (cr[0], i, 0)), tile, tile, tile],
            out_specs=[tile] * 4),
        out_shape=[SDS((R, C), F32)] * 4,
        compiler_params=_cp(("parallel",), 32),
    )(chip, recv, part, w, m, v)


def adamw_small(g, w, m, v, name):
    def body(g_ref, w_ref, m_ref, v_ref, d_ref, m2_ref, v2_ref):
        delta, m2, v2 = _adam_math(w_ref[...], g_ref[...], m_ref[...], v_ref[...])
        d_ref[...] = delta
        m2_ref[...] = m2
        v2_ref[...] = v2

    return pl.pallas_call(body, name=name, out_shape=[SDS(w.shape, F32)] * 3)(g, w, m, v)


def sum_slots(a, name):
    K = a.shape[0]

    def body(a_ref, o_ref):
        t = a_ref[0]
        for k in range(1, K):
            t = t + a_ref[k]
        o_ref[...] = t

    return pl.pallas_call(body, name=name, out_shape=SDS(a.shape[1:], F32))(a)


def norm_pre0(x, g, after):
    S = x.shape[0]
    ts = 512

    def body(x_ref, g_ref, after_ref, o_ref, o4_ref, o16_ref, ot_ref, scr):
        h = x_ref[...]
        xn = h * _rms(h) * g_ref[...]
        o_ref[...] = xn.astype(BF16)
        ot_ref[...] = xn.T.astype(BF16)
        _scr_put(scr, xn)
        _store_perm(o4_ref, scr, 4)
        _store_perm(o16_ref, scr, 16)

    return pl.pallas_call(
        body, name="norm_pre0", grid=(S // ts,),
        in_specs=[pl.BlockSpec((ts, D), lambda i: (i, 0)), pl.BlockSpec((1, D), lambda i: (0, 0)), ANY],
        out_specs=[pl.BlockSpec((ts, D), lambda i: (i, 0)),
                   pl.BlockSpec((4, ts // 4, D), lambda i: (0, i, 0)),
                   pl.BlockSpec((16, ts // 16, D), lambda i: (0, i, 0)),
                   pl.BlockSpec((D, ts), lambda i: (0, i))],
        out_shape=[SDS((S, D), BF16), SDS((4, S // 4, D), BF16), SDS((16, S // 16, D), BF16), SDS((D, S), BF16)],
        scratch_shapes=[_scr(ts, D)],
        compiler_params=_cp(("parallel",), 32),
    )(x, g, after)


def transpose_rows(a, name):
    S, C = a.shape
    ts = 512

    def body(a_ref, o_ref):
        o_ref[...] = a_ref[...].astype(F32).T.astype(BF16)

    return pl.pallas_call(
        body, name=name, grid=(S // ts,),
        in_specs=[pl.BlockSpec((ts, C), lambda i: (i, 0))],
        out_specs=pl.BlockSpec((C, ts), lambda i: (0, i)),
        out_shape=SDS((C, S), BF16),
        compiler_params=_cp(("parallel",), 32),
    )(a)


def post0_pre1(x, a0, g_post, g_pre):
    S = x.shape[0]
    ts = 512

    def body(x_ref, a_ref, gp_ref, gn_ref, h_ref, o_ref, ot_ref):
        a = a_ref[...]
        h1 = x_ref[...] + a * _rms(a) * gp_ref[...]
        h_ref[...] = h1
        xn = h1 * _rms(h1) * gn_ref[...]
        o_ref[...] = xn.astype(BF16)
        ot_ref[...] = xn.T.astype(BF16)

    row = pl.BlockSpec((ts, D), lambda i: (i, 0))
    vec = pl.BlockSpec((1, D), lambda i: (0, 0))
    return pl.pallas_call(
        body, name="post0_pre1", grid=(S // ts,),
        in_specs=[row, row, vec, vec],
        out_specs=[row, row, pl.BlockSpec((D, ts), lambda i: (0, i))],
        out_shape=[SDS((S, D), F32), SDS((S, D), BF16), SDS((D, S), BF16)],
        compiler_params=_cp(("parallel",), 40),
    )(x, a0, g_post, g_pre)


def post1_loss(h1, a1, target, g_post):
    S = h1.shape[0]
    ts = 512

    def body(h_ref, a_ref, t_ref, g_ref, dh_ref, da_ref, loss_ref, dg_ref):
        @pl.when(pl.program_id(0) == 0)
        def _():
            loss_ref[...] = jnp.zeros_like(loss_ref)
            dg_ref[...] = jnp.zeros_like(dg_ref)

        a = a_ref[...]
        g = g_ref[...]
        rp = _rms(a)
        yhat = a * rp
        e = h_ref[...] + yhat * g - t_ref[...]
        loss_ref[...] += _fold8(e * e)
        dh = e * (1.0 / D)
        dh_ref[...] = dh
        dg_ref[...] += _fold8(dh * yhat)
        dyh = dh * g
        da = rp * (dyh - yhat * jnp.mean(dyh * yhat, axis=-1, keepdims=True))
        da_ref[...] = da.astype(BF16)

    row = pl.BlockSpec((ts, D), lambda i: (i, 0))
    acc = pl.BlockSpec((8, D), lambda i: (0, 0))
    return pl.pallas_call(
        body, name="post1_loss", grid=(S // ts,),
        in_specs=[row, row, row, pl.BlockSpec((1, D), lambda i: (0, 0))],
        out_specs=[row, row, acc, acc],
        out_shape=[SDS((S, D), F32), SDS((S, D), BF16), SDS((8, D), F32), SDS((8, D), F32)],
        compiler_params=_cp(("arbitrary",), 40),
    )(h1, a1, target, g_post)


def mid_bwd(dxn1, dh2, h1, a0, g_pre1, g_post0):
    S = h1.shape[0]
    ts = 512

    def body(dx_ref, dh2_ref, h_ref, a_ref, gn_ref, gp_ref, dh1_ref, da_ref, dgn_ref, dgp_ref):
        @pl.when(pl.program_id(0) == 0)
        def _():
            dgn_ref[...] = jnp.zeros_like(dgn_ref)
            dgp_ref[...] = jnp.zeros_like(dgp_ref)

        h = h_ref[...]
        r1 = _rms(h)
        xhat = h * r1
        dxn = dx_ref[...]
        dgn_ref[...] += _fold8(dxn * xhat)
        dxh = dxn * gn_ref[...]
        dh1 = dh2_ref[...] + r1 * (dxh - xhat * jnp.mean(dxh * xhat, axis=-1, keepdims=True))
        dh1_ref[...] = dh1
        a = a_ref[...]
        rp = _rms(a)
        yhat = a * rp
        dgp_ref[...] += _fold8(dh1 * yhat)
        dyh = dh1 * gp_ref[...]
        da = rp * (dyh - yhat * jnp.mean(dyh * yhat, axis=-1, keepdims=True))
        da_ref[...] = da.astype(BF16)

    row = pl.BlockSpec((ts, D), lambda i: (i, 0))
    vec = pl.BlockSpec((1, D), lambda i: (0, 0))
    acc = pl.BlockSpec((8, D), lambda i: (0, 0))
    return pl.pallas_call(
        body, name="mid_bwd", grid=(S // ts,),
        in_specs=[row, row, row, row, vec, vec],
        out_specs=[row, row, acc, acc],
        out_shape=[SDS((S, D), F32), SDS((S, D), BF16), SDS((8, D), F32), SDS((8, D), F32)],
        compiler_params=_cp(("arbitrary",), 48),
    )(dxn1, dh2, h1, a0, g_pre1, g_post0)


def pre0_bwd(dx_tok, dx_z, dx4, dx16, dh1, x, g_pre0):
    S = x.shape[0]
    ts = 512

    def body(da_ref, dz_ref, d4_ref, d16_ref, dh_ref, x_ref, g_ref, gx_ref, dg_ref, scr):
        @pl.when(pl.program_id(0) == 0)
        def _():
            dg_ref[...] = jnp.zeros_like(dg_ref)

        _scr_put(scr, da_ref[...] + dz_ref[...])
        _load_perm(scr, d4_ref, 4, add=True)
        _load_perm(scr, d16_ref, 16, add=True)
        h = x_ref[...]
        r = _rms(h)
        xhat = h * r
        dxn = _scr_get(scr)
        dg_ref[...] += _fold8(dxn * xhat)
        dxh = dxn * g_ref[...]
        gx_ref[...] = dh_ref[...] + r * (dxh - xhat * jnp.mean(dxh * xhat, axis=-1, keepdims=True))

    row = pl.BlockSpec((ts, D), lambda i: (i, 0))
    return pl.pallas_call(
        body, name="pre0_bwd", grid=(S // ts,),
        in_specs=[row, row, pl.BlockSpec((4, ts // 4, D), lambda i: (0, i, 0)),
                  pl.BlockSpec((16, ts // 16, D), lambda i: (0, i, 0)), row, row,
                  pl.BlockSpec((1, D), lambda i: (0, 0))],
        out_specs=[row, pl.BlockSpec((8, D), lambda i: (0, 0))],
        out_shape=[SDS((S, D), F32), SDS((8, D), F32)],
        scratch_shapes=[_scr(ts, D)],
        compiler_params=_cp(("arbitrary",), 48),
    )(dx_tok, dx_z, dx4.reshape(4, S // 4, D), dx16.reshape(16, S // 16, D), dh1, x, g_pre0)


def _w_tile(tile0):
    per = W_SHARD // CT
    return lambda t: ((tile0 + t) // per, 0, (tile0 + t) % per)


def mm_in(xn, w8, tile0, ntiles, tabs, name, after=None):
    S = xn.shape[0]
    tm = 2048
    wt = _w_tile(tile0)

    def body(a_ref, b_ref, *rest):
        o_ref = rest[-1]
        rc = 512
        for u in range(tm // rc):
            rows = slice(u * rc, (u + 1) * rc)
            r = _dot(a_ref[rows, :], b_ref[...])
            if tabs is None:
                o_ref[rows, :] = r.astype(BF16)
                continue
            c_ref, s1_ref, s2_ref = rest[:3]
            rot = pl.program_id(1) < 2 * E // CT
            c = jnp.where(rot, c_ref[rows, :], 1.0)
            s1 = jnp.where(rot, s1_ref[rows, :], 0.0)
            s2 = jnp.where(rot, s2_ref[rows, :], 0.0)
            for hh in range(CT // HD):
                cs = slice(hh * HD, (hh + 1) * HD)
                o_ref[rows, cs] = _rope(r[:, cs], c, s1, s2).astype(BF16)

    tab = pl.BlockSpec((tm, HD), lambda i, t: (i, 0))
    return pl.pallas_call(
        body, name=name, grid=(S // tm, ntiles),
        in_specs=[pl.BlockSpec((tm, D), lambda i, t: (i, 0)),
                  pl.BlockSpec((None, D, CT), lambda i, t: wt(t))] + ([] if tabs is None else [tab] * 3)
        + ([] if after is None else [ANY]),
        out_specs=pl.BlockSpec((tm, CT), lambda i, t: (i, t)),
        out_shape=SDS((S, ntiles * CT), BF16),
        compiler_params=_cp(("parallel", "parallel"), 48),
    )(xn, w8, *(() if tabs is None else tabs), *(() if after is None else (after,)))


def mm_rows(a, b, name, out_dtype, tm=1024):
    M, K = a.shape
    N = b.shape[1]

    def body(a_ref, b_ref, o_ref):
        for cidx in range(N // 256):
            col = slice(cidx * 256, (cidx + 1) * 256)
            o_ref[:, col] = _dot(a_ref[...], b_ref[:, col]).astype(out_dtype)

    return pl.pallas_call(
        body, name=name, grid=(M // tm,),
        in_specs=[pl.BlockSpec((tm, K), lambda i: (i, 0)), pl.BlockSpec((K, N), lambda i: (0, 0))],
        out_specs=pl.BlockSpec((tm, N), lambda i: (i, 0)),
        out_shape=SDS((M, N), out_dtype),
        compiler_params=_cp(("parallel",), 48),
    )(a, b)


def mm_acc(a, b, name, *, grid, a_spec, b_spec, o_spec, o_shape, acc_shape, write, vmem=48):
    nk = grid[-1]

    def body(a_ref, b_ref, o_ref, acc_ref):
        k = pl.program_id(len(grid) - 1)

        @pl.when(k == 0)
        def _():
            acc_ref[...] = jnp.zeros_like(acc_ref)

        acc_ref[...] += _dot(a_ref[...], b_ref[...])

        @pl.when(k == nk - 1)
        def _():
            write(o_ref, acc_ref)

    return pl.pallas_call(
        body, name=name, grid=grid, in_specs=[a_spec, b_spec], out_specs=o_spec, out_shape=o_shape,
        scratch_shapes=[pltpu.VMEM(acc_shape, F32)],
        compiler_params=_cp(("parallel",) * (len(grid) - 1) + ("arbitrary",), vmem),
    )(a, b)


def _write_plain(o_ref, acc_ref):
    o_ref[...] = acc_ref[...]


def mm_wgrad_rows(at, b, name):
    M, S = at.shape
    N = b.shape[1]
    tm, tk = 1024, 1024
    return mm_acc(at, b, name, grid=(M // tm, S // tk),
                  a_spec=pl.BlockSpec((tm, tk), lambda i, k: (i, k)),
                  b_spec=pl.BlockSpec((tk, N), lambda i, k: (k, 0)),
                  o_spec=pl.BlockSpec((tm, N), lambda i, k: (i, 0)),
                  o_shape=SDS((M, N), F32), acc_shape=(tm, N), write=_write_plain)


def mm_wgrad_cols(at, b, name, *, ncols, shard, tn):
    M, S = at.shape
    tk = 1024
    per = shard // tn if tn <= shard else 1
    nb = max(1, tn // shard)

    if tn <= shard:
        o_spec = pl.BlockSpec((None, M, tn), lambda t, k: (t // per, 0, t % per))
        write = _write_plain
    else:
        o_spec = pl.BlockSpec((nb, M, shard), lambda t, k: (t, 0, 0))

        def write(o_ref, acc_ref):
            for u in range(nb):
                o_ref[u] = acc_ref[:, u * shard:(u + 1) * shard]

    return mm_acc(at, b, name, grid=(ncols // tn, S // tk),
                  a_spec=pl.BlockSpec((M, tk), lambda t, k: (0, k)),
                  b_spec=pl.BlockSpec((tk, tn), lambda t, k: (k, t)),
                  o_spec=o_spec, o_shape=SDS((N_DEV, M, shard), F32), acc_shape=(M, tn), write=write)


def mm_dwg(pooled_t, dh):
    S = dh.shape[0]
    tk = 2048
    return mm_acc(pooled_t, dh, "mm_dwg", grid=(4, S // tk),
                  a_spec=pl.BlockSpec((PC, tk), lambda g, k: (g, k)),
                  b_spec=pl.BlockSpec((tk, PC), lambda g, k: (k, g)),
                  o_spec=pl.BlockSpec((None, PC, PC), lambda g, k: (g, 0, 0)),
                  o_shape=SDS((4, PC, PC), F32), acc_shape=(PC, PC), write=_write_plain)


def mm_dx_full(da, w, name):
    S, K = da.shape
    N = w.shape[0]
    tm = 512

    def body(a_ref, b_ref, o_ref):
        o_ref[...] = _dot_nt(a_ref[...], b_ref[...])

    return pl.pallas_call(
        body, name=name, grid=(S // tm,),
        in_specs=[pl.BlockSpec((tm, K), lambda i: (i, 0)), pl.BlockSpec((N, K), lambda i: (0, 0))],
        out_specs=pl.BlockSpec((tm, N), lambda i: (i, 0)),
        out_shape=SDS((S, N), F32),
        compiler_params=_cp(("parallel",), 48),
    )(da, w)


def mm_dw_in_part(at, b, tile0, prev, name):
    M, S = at.shape
    ntiles = b.shape[1] // CT
    tk = 2048
    nk = S // tk
    wt = _w_tile(tile0)

    def body(a_ref, b_ref, *rest):
        o_ref, acc_ref = rest[-2:]
        k = pl.program_id(1)

        @pl.when(k == 0)
        def _():
            acc_ref[...] = jnp.zeros_like(acc_ref)

        acc_ref[...] += _dot(a_ref[...], b_ref[...])

        @pl.when(k == nk - 1)
        def _():
            o_ref[...] = acc_ref[...]

    return pl.pallas_call(
        body, name=name, grid=(ntiles, nk),
        in_specs=[pl.BlockSpec((M, tk), lambda t, k: (0, k)), pl.BlockSpec((tk, CT), lambda t, k: (k, t))]
        + ([] if prev is None else [ANY]),
        out_specs=pl.BlockSpec((None, M, CT), lambda t, k: wt(t)),
        out_shape=SDS((N_DEV, M, W_SHARD), F32),
        scratch_shapes=[pltpu.VMEM((M, CT), F32)],
        input_output_aliases={} if prev is None else {2: 0},
        compiler_params=_cp(("parallel", "arbitrary"), 48),
    )(at, b, *(() if prev is None else (prev,)))


def mm_dx_part(da, w8, tile0, name, after=None):
    S = da.shape[0]
    ntiles = da.shape[1] // CT
    tm = 2048
    wt = _w_tile(tile0)

    def body(a_ref, b_ref, *rest):
        o_ref, acc_ref = rest[-2:]
        t = pl.program_id(1)

        @pl.when(t == 0)
        def _():
            acc_ref[...] = jnp.zeros_like(acc_ref)

        acc_ref[...] += _dot_nt(a_ref[...], b_ref[...])

        @pl.when(t == ntiles - 1)
        def _():
            o_ref[...] = acc_ref[...]

    return pl.pallas_call(
        body, name=name, grid=(S // tm, ntiles),
        in_specs=[pl.BlockSpec((tm, CT), lambda i, t: (i, t)), pl.BlockSpec((None, D, CT), lambda i, t: wt(t))]
        + ([] if after is None else [ANY]),
        out_specs=pl.BlockSpec((tm, D), lambda i, t: (i, 0)),
        out_shape=SDS((S, D), F32),
        scratch_shapes=[pltpu.VMEM((tm, D), F32)],
        compiler_params=_cp(("parallel", "arbitrary"), 56),
    )(da, w8, *(() if after is None else (after,)))


HEADS_PER_STEP = 2
AHEAD = 2


def _band_masks(not_first):
    row = lax.broadcasted_iota(jnp.int32, (QB, QB), 0)
    col = lax.broadcasted_iota(jnp.int32, (QB, QB), 1)
    cur = jnp.where(col <= row, 0.0, NEG)
    prev = jnp.where(col >= row, 0.0, NEG)
    first = jnp.where(jnp.logical_and(col >= row, not_first), 0.0, NEG)
    return col, jnp.concatenate([prev, cur], axis=1), jnp.concatenate([first, cur], axis=1)


def _fill_kv(ext, qkv_ref, kh_ref, vh_ref):
    ext[0:QB, 0:E] = kh_ref[...]
    ext[0:QB, E:2 * E] = vh_ref[...]
    ext[QB:, :] = qkv_ref[:, E:3 * E]


def attn_fwd(P, g, d):
    S = P.shape[0]
    L = S // d
    T = min(512, L)
    nq = T // QB
    ni = L // T

    def body(qkv_ref, kh_ref, vh_ref, o_ref, lse_ref, ext):
        col, mask, mask_first = _band_masks(pl.program_id(1) > 0)
        lse_ref[...] = jnp.zeros_like(lse_ref)
        _fill_kv(ext, qkv_ref, kh_ref, vh_ref)

        def heads(hp, carry):
            def front(h, j):
                cq = pl.ds(pl.multiple_of(h * HD, HD), HD)
                rows = slice(j * QB, (j + 1) * QB)
                krows = slice(j * QB, (j + 2) * QB)
                s = _dot_nt(qkv_ref[rows, cq], ext[krows, cq]) * SCALE + (mask_first if j == 0 else mask)
                m = jnp.max(s, axis=1, keepdims=True)
                p = jnp.exp(s - m)
                den = jnp.sum(p, axis=1, keepdims=True)
                lse_ref[rows, :] = jnp.where(col == h, m + jnp.log(den), lse_ref[rows, :])
                return p.astype(BF16), den

            def back(h, j, p, den):
                off = pl.multiple_of(h * HD, HD)
                rows = slice(j * QB, (j + 1) * QB)
                krows = slice(j * QB, (j + 2) * QB)
                o_ref[rows, pl.ds(off, HD)] = (_dot(p, ext[krows, pl.ds(E + off, HD)]) / den).astype(BF16)

            items = [(HEADS_PER_STEP * hp + hh, j) for hh in range(HEADS_PER_STEP) for j in range(nq)]
            queue = [front(*it) for it in items[:AHEAD]]
            for u, it in enumerate(items):
                if u + AHEAD < len(items):
                    queue.append(front(*items[u + AHEAD]))
                back(*it, *queue.pop(0))
            return carry

        lax.fori_loop(0, NH // HEADS_PER_STEP, heads, 0)

    halo = lambda r, i: jnp.maximum(r * (L // QB) + i * nq - 1, 0)
    return pl.pallas_call(
        body, name=f"attn_fwd{g}", grid=(d, ni),
        in_specs=[pl.BlockSpec((T, SEG), lambda r, i: (r * ni + i, 0)),
                  pl.BlockSpec((QB, E), lambda r, i: (halo(r, i), 1)),
                  pl.BlockSpec((QB, E), lambda r, i: (halo(r, i), 2))],
        out_specs=[pl.BlockSpec((T, E), lambda r, i: (r * ni + i, 0)),
                   pl.BlockSpec((T, HD), lambda r, i: (r * ni + i, 0))],
        out_shape=[SDS((S, E), BF16), SDS((S, HD), F32)],
        scratch_shapes=[pltpu.VMEM((T + QB, 2 * E), BF16)],
        compiler_params=_cp(("parallel", "parallel"), 48),
    )(P, P, P)


def _perm_specs(ts, C):
    return [pl.BlockSpec((ts, C), lambda i: (i, 0)),
            pl.BlockSpec((4, ts // 4, C), lambda i: (0, i, 0)),
            pl.BlockSpec((16, ts // 16, C), lambda i: (0, i, 0))]


def _perm_shapes(S, C, dtype):
    return [SDS((S, C), dtype), SDS((4, S // 4, C), dtype), SDS((16, S // 16, C), dtype)]


def combine_fwd(os_, lses, z, ehot):
    S = z.shape[0]
    ts = 256

    def body(o0, o1, o2, l0, l1, l2, z_ref, e_ref, y0, y1, y2, s0, s1, s2, ya_ref, yat_ref,
             so1, so2, sl1, sl2, sy, sl):
        _load_perm(so1, o1, 4)
        _load_perm(so2, o2, 16)
        _load_perm(sl1, l1, 4)
        _load_perm(sl2, l2, 16)
        ls = [l0[...], sl1[0], sl2[0]]
        m = jnp.maximum(jnp.maximum(ls[0], ls[1]), ls[2])
        es = [jnp.exp(l - m) for l in ls]
        den = es[0] + es[1] + es[2]
        sl[0] = m + jnp.log(den)
        y = None
        for e, o in zip(es, (o0[...].astype(F32), _scr_get(so1), _scr_get(so2))):
            w = e / den
            hi = w.astype(BF16)
            lo = (w - hi.astype(F32)).astype(BF16)
            wb = _dot(hi, e_ref[...]) + _dot(lo, e_ref[...])
            y = wb * o if y is None else y + wb * o
        z = z_ref[...].astype(F32)
        ya = y * (z * _sigmoid(z))
        ya_ref[...] = ya.astype(BF16)
        yat_ref[...] = ya.T.astype(BF16)
        _scr_put(sy, y)
        y0[...] = y.astype(BF16)
        _store_perm(y1, sy, 4)
        _store_perm(y2, sy, 16)
        s0[...] = sl[0]
        _store_perm(s1, sl, 4)
        _store_perm(s2, sl, 16)

    wide = pl.BlockSpec((ts, E), lambda i: (i, 0))
    os3 = [os_[0], os_[1].reshape(4, S // 4, E), os_[2].reshape(16, S // 16, E)]
    ls3 = [lses[0], lses[1].reshape(4, S // 4, HD), lses[2].reshape(16, S // 16, HD)]
    res = pl.pallas_call(
        body, name="combine_fwd", grid=(S // ts,),
        in_specs=_perm_specs(ts, E) + _perm_specs(ts, HD) + [wide, pl.BlockSpec((HD, E), lambda i: (0, 0))],
        out_specs=_perm_specs(ts, E) + _perm_specs(ts, HD) + [wide, pl.BlockSpec((E, ts), lambda i: (0, i))],
        out_shape=_perm_shapes(S, E, BF16) + _perm_shapes(S, HD, F32) + [SDS((S, E), BF16), SDS((E, S), BF16)],
        scratch_shapes=[_scr(ts, E), _scr(ts, E), _scr(ts, HD), _scr(ts, HD), _scr(ts, E), _scr(ts, HD)],
        compiler_params=_cp(("parallel",), 56),
    )(*os3, *ls3, z, ehot)
    ys = [res[0], res[1].reshape(S, E), res[2].reshape(S, E)]
    lse3 = [res[3], res[4].reshape(S, HD), res[5].reshape(S, HD)]
    return ys, lse3, res[6], res[7]


def mm_dya(da0, w_out, z, y):
    S = da0.shape[0]
    tm = 512

    def body(a_ref, w_ref, z_ref, y_ref, dy0, dy1, dy2, dz_ref, scr):
        for cidx in range(E // 256):
            col = slice(cidx * 256, (cidx + 1) * 256)
            dya = _dot_nt(a_ref[...], w_ref[col, :])
            zz = z_ref[:, col].astype(F32)
            sig = _sigmoid(zz)
            dy = dya * zz * sig
            scr[2 * cidx] = dy[:, :LANES]
            scr[2 * cidx + 1] = dy[:, LANES:]
            dy0[:, col] = dy.astype(BF16)
            dz_ref[:, col] = (dya * y_ref[:, col].astype(F32) * sig * (1.0 + zz * (1.0 - sig))).astype(BF16)
        _store_perm(dy1, scr, 4)
        _store_perm(dy2, scr, 16)

    wide = pl.BlockSpec((tm, E), lambda i: (i, 0))
    res = pl.pallas_call(
        body, name="mm_dya", grid=(S // tm,),
        in_specs=[pl.BlockSpec((tm, D), lambda i: (i, 0)), pl.BlockSpec((E, D), lambda i: (0, 0)), wide, wide],
        out_specs=_perm_specs(tm, E) + [wide],
        out_shape=_perm_shapes(S, E, BF16) + [SDS((S, E), BF16)],
        scratch_shapes=[_scr(tm, E)],
        compiler_params=_cp(("parallel",), 48),
    )(da0, w_out, z, y)
    return [res[0], res[1].reshape(S, E), res[2].reshape(S, E)], res[3]


def attn_bwd(P, dy, y, lse, tabs, g, d):
    S = P.shape[0]
    L = S // d
    T = min(512, L)
    nq = T // QB
    ni = L // T

    def body(qkv_ref, kh_ref, vh_ref, dy_ref, y_ref, lse_ref, c_ref, s1_ref, s2_ref,
             o_ref, dkc_ref, dvc_ref, ext):
        i = pl.program_id(1)
        col, mask, mask_first = _band_masks(i < ni - 1)
        _fill_kv(ext, qkv_ref, kh_ref, vh_ref)

        @pl.when(i == 0)
        def _():
            dkc_ref[...] = jnp.zeros_like(dkc_ref)
            dvc_ref[...] = jnp.zeros_like(dvc_ref)

        def heads(hp, carry):
            def cols(h):
                off = pl.multiple_of(h * HD, HD)
                return pl.ds(off, HD), pl.ds(E + off, HD), pl.ds(2 * E + off, HD)

            def front(h, j):
                cq, ck, _ = cols(h)
                rows = slice(j * QB, (j + 1) * QB)
                krows = slice(j * QB, (j + 2) * QB)
                dyj = dy_ref[rows, cq]
                lse_h = jnp.sum(jnp.where(col == h, lse_ref[rows, :], 0.0), axis=1, keepdims=True)
                delta = jnp.sum(dyj.astype(F32) * y_ref[rows, cq].astype(F32), axis=1, keepdims=True)
                s = _dot_nt(qkv_ref[rows, cq], ext[krows, cq])
                p = jnp.exp(s * SCALE + (mask_first if j == 0 else mask) - lse_h)
                ds = (p * (_dot_nt(dyj, ext[krows, ck]) - delta) * SCALE).astype(BF16)
                return ds, ds.T, p.astype(BF16).T

            def back(h, j, ds, ds_t, p_t, pend_dk, pend_dv):
                cq, ck, cv = cols(h)
                rows = slice(j * QB, (j + 1) * QB)
                krows = slice(j * QB, (j + 2) * QB)
                dq = _dot(ds, ext[krows, cq])
                dk2 = _dot(ds_t, qkv_ref[rows, cq])
                dv2 = _dot(p_t, dy_ref[rows, cq])
                c, s1, s2 = c_ref[rows, :], s1_ref[rows, :], s2_ref[rows, :]
                o_ref[rows, cq] = _unrope(dq, c, s1, s2).astype(BF16)
                o_ref[rows, ck] = _unrope(dk2[QB:] + pend_dk, c, s1, s2).astype(BF16)
                o_ref[rows, cv] = (dv2[QB:] + pend_dv).astype(BF16)
                return dk2[:QB], dv2[:QB]

            items = [(HEADS_PER_STEP * hp + hh, j) for hh in range(HEADS_PER_STEP) for j in reversed(range(nq))]
            queue = [front(*it) for it in items[:AHEAD]]
            pend = None
            for u, (h, j) in enumerate(items):
                if u + AHEAD < len(items):
                    queue.append(front(*items[u + AHEAD]))
                if j == nq - 1:
                    pend = (dkc_ref[:, cols(h)[0]], dvc_ref[:, cols(h)[0]])
                pend = back(h, j, *queue.pop(0), *pend)
                if j == 0:
                    dkc_ref[:, cols(h)[0]], dvc_ref[:, cols(h)[0]] = pend
            return carry

        lax.fori_loop(0, NH // HEADS_PER_STEP, heads, 0)

    blk = lambda r, i: r * ni + ni - 1 - i
    halo = lambda r, i: jnp.maximum(r * (L // QB) + (ni - 1 - i) * nq - 1, 0)
    main = pl.BlockSpec((T, SEG), lambda r, i: (blk(r, i), 0))
    wide = pl.BlockSpec((T, E), lambda r, i: (blk(r, i), 0))
    narrow = pl.BlockSpec((T, HD), lambda r, i: (blk(r, i), 0))
    return pl.pallas_call(
        body, name=f"attn_bwd{g}", grid=(d, ni),
        in_specs=[main, pl.BlockSpec((QB, E), lambda r, i: (halo(r, i), 1)),
                  pl.BlockSpec((QB, E), lambda r, i: (halo(r, i), 2)),
                  wide, wide, narrow, narrow, narrow, narrow],
        out_specs=main, out_shape=SDS((S, SEG), BF16),
        scratch_shapes=[pltpu.VMEM((QB, E), F32), pltpu.VMEM((QB, E), F32), pltpu.VMEM((T + QB, 2 * E), BF16)],
        compiler_params=_cp(("arbitrary", "arbitrary"), 56),
    )(P, P, P, dy, y, lse, *tabs)


def _pool_cnt(t0, rows):
    t = (lax.broadcasted_iota(jnp.int32, (rows, E), 0) + t0 + 1).astype(F32)
    ch = lax.broadcasted_iota(jnp.int32, (rows, E), 1)
    w = jnp.where(ch < PC, 2.0, jnp.where(ch < 2 * PC, 4.0, jnp.where(ch < 3 * PC, 8.0, 16.0)))
    return jnp.minimum(t, w)


def _by_group(parts):
    return jnp.concatenate([parts[g][:, g * PC:(g + 1) * PC] for g in range(4)], axis=1)


def pool_fwd(uz):
    S = uz.shape[0]
    ts = 256

    def body(u_ref, h_ref, o_ref, ot_ref):
        i = pl.program_id(0)
        u = u_ref[...]
        halo = jnp.where(i > 0, h_ref[...], 0.0)
        ext = jnp.concatenate([halo, u], axis=0)
        s2 = ext + pltpu.roll(ext, 1, 0)
        s4 = s2 + pltpu.roll(s2, 2, 0)
        s8 = s4 + pltpu.roll(s4, 4, 0)
        s16 = s8 + pltpu.roll(s8, 8, 0)
        win = _by_group([s2, s4, s8, s16])[16:, :]
        pooled = win / _pool_cnt(i * ts, ts) - u
        o_ref[...] = pooled.astype(BF16)
        ot_ref[...] = pooled.T.astype(BF16)

    return pl.pallas_call(
        body, name="pool_fwd", grid=(S // ts,),
        in_specs=[pl.BlockSpec((ts, E), lambda i: (i, 0)),
                  pl.BlockSpec((16, E), lambda i: (jnp.maximum(i * (ts // 16) - 1, 0), 0))],
        out_specs=[pl.BlockSpec((ts, E), lambda i: (i, 0)), pl.BlockSpec((E, ts), lambda i: (0, i))],
        out_shape=[SDS((S, E), BF16), SDS((E, S), BF16)],
        compiler_params=_cp(("parallel",), 48),
    )(uz, uz)


def pool_bwd(dpooled, duz):
    S = dpooled.shape[0]
    ts = 256
    nt = S // ts

    def body(d_ref, h_ref, alias_ref, o_ref):
        i = pl.program_id(0)
        dp = d_ref[...].astype(F32)
        halo = jnp.where(i < nt - 1, h_ref[...].astype(F32), 0.0)
        n = ts + 16
        ext = jnp.concatenate([dp, halo], axis=0) / _pool_cnt(i * ts, n)
        f2 = ext + pltpu.roll(ext, n - 1, 0)
        f4 = f2 + pltpu.roll(f2, n - 2, 0)
        f8 = f4 + pltpu.roll(f4, n - 4, 0)
        f16 = f8 + pltpu.roll(f8, n - 8, 0)
        win = _by_group([f2, f4, f8, f16])[:ts, :]
        o_ref[...] = (win - dp).astype(BF16)

    return pl.pallas_call(
        body, name="pool_bwd", grid=(nt,),
        in_specs=[pl.BlockSpec((ts, E), lambda i: (i, 0)),
                  pl.BlockSpec((16, E), lambda i: (jnp.minimum((i + 1) * (ts // 16), S // 16 - 1), 0)), ANY],
        out_specs=pl.BlockSpec((ts, E), lambda i: (i, 0)),
        out_shape=SDS(duz.shape, BF16),
        input_output_aliases={2: 0},
        compiler_params=_cp(("parallel",), 48),
    )(dpooled, dpooled, duz)


def mm_grp(pooled, wg, b, scale, uz):
    S = pooled.shape[0]
    tm = 512

    def body(p_ref, w_ref, b_ref, s_ref, z_ref, h_ref, y_ref, yt_ref):
        for g in range(4):
            cs = slice(g * PC, (g + 1) * PC)
            h = _dot(p_ref[:, cs], w_ref[g]) + b_ref[:, cs]
            z = z_ref[:, cs]
            yp = h * s_ref[:, cs] * (z * _sigmoid(z))
            h_ref[:, cs] = h.astype(BF16)
            y_ref[:, cs] = yp.astype(BF16)
            yt_ref[cs, :] = yp.T.astype(BF16)

    row = pl.BlockSpec((tm, E), lambda i: (i, 0))
    vec = pl.BlockSpec((1, E), lambda i: (0, 0))
    return pl.pallas_call(
        body, name="mm_grp", grid=(S // tm,),
        in_specs=[row, pl.BlockSpec((4, PC, PC), lambda i: (0, 0, 0)), vec, vec,
                  pl.BlockSpec((tm, E), lambda i: (i, 1))],
        out_specs=[row, row, pl.BlockSpec((E, tm), lambda i: (0, i))],
        out_shape=[SDS((S, E), BF16), SDS((S, E), BF16), SDS((E, S), BF16)],
        compiler_params=_cp(("parallel",), 48),
    )(pooled, wg, b, scale, uz)


def mm_dyp(da1, w_out, uz, h, scale):
    S = da1.shape[0]
    tm = 512

    def body(a_ref, w_ref, z_ref, h_ref, s_ref, dh_ref, dz_ref, dsc_ref, db_ref):
        @pl.when(pl.program_id(0) == 0)
        def _():
            dsc_ref[...] = jnp.zeros_like(dsc_ref)
            db_ref[...] = jnp.zeros_like(db_ref)

        for cidx in range(E // 256):
            col = slice(cidx * 256, (cidx + 1) * 256)
            dyp = _dot_nt(a_ref[...], w_ref[col, :])
            z = z_ref[:, col]
            hh = h_ref[:, col].astype(F32)
            sc = s_ref[:, col]
            sig = _sigmoid(z)
            dhs = dyp * z * sig
            dz_ref[:, col] = (dyp * hh * sc * sig * (1.0 + z * (1.0 - sig))).astype(BF16)
            dh = dhs * sc
            dh_ref[:, col] = dh.astype(BF16)
            dsc_ref[:, col] += _fold8(dhs * hh)
            db_ref[:, col] += _fold8(dh)

    row = pl.BlockSpec((tm, E), lambda i: (i, 0))
    acc = pl.BlockSpec((8, E), lambda i: (0, 0))
    return pl.pallas_call(
        body, name="mm_dyp", grid=(S // tm,),
        in_specs=[pl.BlockSpec((tm, D), lambda i: (i, 0)), pl.BlockSpec((E, D), lambda i: (0, 0)),
                  pl.BlockSpec((tm, E), lambda i: (i, 1)), row, pl.BlockSpec((1, E), lambda i: (0, 0))],
        out_specs=[row, pl.BlockSpec((tm, E), lambda i: (i, 1)), acc, acc],
        out_shape=[SDS((S, E), BF16), SDS((S, 2 * E), BF16), SDS((8, E), F32), SDS((8, E), F32)],
        compiler_params=_cp(("arbitrary",), 48),
    )(da1, w_out, uz, h, scale)


def mm_dpooled(dh, wg):
    S = dh.shape[0]
    tm = 1024

    def body(a_ref, w_ref, o_ref):
        for g in range(4):
            cs = slice(g * PC, (g + 1) * PC)
            o_ref[:, cs] = _dot_nt(a_ref[:, cs], w_ref[g]).astype(BF16)

    row = pl.BlockSpec((tm, E), lambda i: (i, 0))
    return pl.pallas_call(
        body, name="mm_dpooled", grid=(S // tm,),
        in_specs=[row, pl.BlockSpec((4, PC, PC), lambda i: (0, 0, 0))],
        out_specs=row, out_shape=SDS((S, E), BF16),
        compiler_params=_cp(("parallel",), 48),
    )(dh, wg)


def _rope_tables(positions):
    inv_freq = 500000.0 ** (-jnp.arange(0, 32, 2, dtype=F32) / 32)
    S = positions.shape[0]
    ang = jnp.repeat(positions.astype(F32).reshape(S // 8, 8), 16, axis=1) * jnp.tile(inv_freq, 8)
    cos, sin = jnp.cos(ang).reshape(S, 16), jnp.sin(ang).reshape(S, 16)
    one = jnp.ones((S, HD - 32), F32)
    zero16 = jnp.zeros((S, 16), F32)
    zero = jnp.zeros((S, HD - 32), F32)
    c = jnp.concatenate([cos, cos, one], axis=1)
    s1 = jnp.concatenate([-sin, zero16, zero], axis=1)
    s2 = jnp.concatenate([zero16, sin, zero], axis=1)
    return c.astype(BF16), s1.astype(BF16), s2.astype(BF16)


def kernel(x, positions, norm_pre, norm_post, attn_w_in, attn_w_out, pool_w_in, pool_w_grp, pool_b_grp, pool_scale, pool_w_out, loss_target, m_norm_pre, m_norm_post, m_attn_w_in, m_attn_w_out, m_pool_w_in, m_pool_w_grp, m_pool_b_grp, m_pool_scale, m_pool_w_out, v_norm_pre, v_norm_post, v_attn_w_in, v_attn_w_out, v_pool_w_in, v_pool_w_grp, v_pool_b_grp, v_pool_scale, v_pool_w_out):
    S = x.shape[1]
    xi, yi, ci = _mesh_pos()
    dev = 4 * xi + 2 * yi + ci
    x2 = x[0]
    tgt = loss_target[0]

    small = jnp.concatenate([pool_b_grp[0].reshape(2, HD), pool_scale[0].reshape(2, HD),
                             jnp.zeros((4, HD), F32)], axis=0)
    w_in_l = attn_w_in[0].astype(BF16)
    hop1, hop1_token = split_start("gather_w_in_start", [w_in_l], [lax.empty((N_DEV,) + w_in_l.shape, BF16)],
                                   _first_hop_plan(), 4)

    pos = positions[0]
    tabs = [_rope_tables(pos.reshape(S // d, d).T.reshape(S)) for d in DIL]
    ehot = (jnp.arange(E)[None, :] // HD == jnp.arange(HD)[:, None]).astype(BF16)
    seg_tiles = SEG // CT

    xn0, xn0_4, xn0_16, xn0t = norm_pre0(x2, norm_pre[0:1], hop1_token)
    xn0s = [xn0, xn0_4.reshape(S, D), xn0_16.reshape(S, D)]
    xn0ts = [xn0t, transpose_rows(xn0s[1], "xn0t_4"), transpose_rows(xn0s[2], "xn0t_16")]

    (w_in_l,), (w_in8,) = split_wait("gather_w_in_wait", hop1, _first_hop_plan(), xn0ts[2])
    hop2, hop2_token = split_start("gather_w_in_fwd_start", [w_in8], None, _second_hop_plan(), 3)
    _, (w_in8,) = split_wait("gather_w_in_fwd_wait", hop2, _second_hop_plan(), hop2_token, inplace=True)
    w_in8 = lax.dynamic_update_slice(w_in8, w_in_l[None], (dev, 0, 0))
    small, w_in8 = lax.optimization_barrier((small, w_in8))
    rest_l = [attn_w_out[0].astype(BF16), pool_w_in[0].astype(BF16), pool_w_grp[0].astype(BF16),
              pool_w_out[0].astype(BF16), small]
    rest_flight, rest_token = split_start(
        "gather_rest_start", rest_l, [lax.empty((N_DEV,) + a.shape, a.dtype) for a in rest_l], _peers_plan(), 7)
    Ps, os_, lses = [], [], []
    for g, d in enumerate(DIL):
        P = mm_in(xn0s[g], w_in8, g * seg_tiles, seg_tiles, tabs[g], f"mm_qkv{g}", after=rest_token)
        o, l = attn_fwd(P, g, d)
        Ps.append(P)
        os_.append(o)
        lses.append(l)
    z0 = mm_in(xn0, w_in8, 3 * seg_tiles, E // CT, None, "mm_z0")
    ys, lse3, ya, yat = combine_fwd(os_, lses, z0, ehot)

    rest_l, rest8 = split_wait("gather_rest_wait", rest_flight, _peers_plan(), ya)
    rest8 = [lax.dynamic_update_slice(r8, a[None], (dev,) + (0,) * a.ndim) for r8, a in zip(rest8, rest_l)]
    w_out8, wp_in8, wg8, wp_out8, small8 = rest8
    w_out = w_out8.reshape(E, D)
    wp_out = wp_out8.reshape(E, D)
    wp_in = wp_in8.transpose(1, 0, 2).reshape(D, 2 * E)
    wg = wg8.transpose(1, 0, 2, 3).reshape(4, PC, PC)
    b_full = small8[:, 0:2, :].reshape(N_DEV, 4, PC // N_DEV).transpose(1, 0, 2).reshape(1, E)
    scale_full = small8[:, 2:4, :].reshape(1, E)
    a0 = mm_rows(ya, w_out, "mm_out0", F32)
    h1, xn1, xn1t = post0_pre1(x2, a0, norm_post[0:1], norm_pre[1:2])

    uz = mm_rows(xn1, wp_in, "mm_uz", F32, tm=512)
    pooled, pooled_t = pool_fwd(uz)
    hgrp, yp, ypt = mm_grp(pooled, wg, b_full, scale_full, uz)
    a1 = mm_rows(yp, wp_out, "mm_out1", F32)
    dh2, da1, loss_rows, dg_post1 = post1_loss(h1, a1, tgt, norm_post[1:2])
    loss = lax.psum(0.5 / D * jnp.sum(loss_rows), ("x", "y", "c"))

    dh, duz, dscale_p, db_p = mm_dyp(da1, wp_out, uz, hgrp, scale_full)
    dpooled = mm_dpooled(dh, wg)
    duz = pool_bwd(dpooled, duz)
    g_wg = mm_dwg(pooled_t, dh)
    g_wp_out = mm_wgrad_rows(ypt, da1, "mm_dwp_out")
    g_wp_in = mm_wgrad_cols(xn1t, duz, "mm_dwp_in", ncols=2 * E, shard=PC, tn=1024)
    dxn1 = mm_dx_full(duz, wp_in, "mm_dxn1")
    dh1, da0, dg_pre1, dg_post0 = mid_bwd(dxn1, dh2, h1, a0, norm_pre[1:2], norm_post[0:1])

    dys, dz0 = mm_dya(da0, w_out, z0, ys[0])
    g_w_out = mm_wgrad_rows(yat, da0, "mm_dw_out")
    g_w_in = mm_dw_in_part(xn0t, dz0, 3 * seg_tiles, None, "mm_dw_in_z")
    dPs = []
    for g, d in enumerate(DIL):
        dP = attn_bwd(Ps[g], dys[g], ys[g], lse3[g], tabs[g], g, d)
        g_w_in = mm_dw_in_part(xn0ts[g], dP, g * seg_tiles, g_w_in, f"mm_dw_in{g}")
        dPs.append(dP)

    cidx = ci.astype(jnp.int32).reshape(1)
    chip = (2 * xi + yi).astype(jnp.int32).reshape(1)
    fulls = [g_w_in, g_w_out.reshape(N_DEV, E // N_DEV, D), g_wp_in,
             g_wg.reshape(4, N_DEV, PC // N_DEV, PC).transpose(1, 0, 2, 3).reshape(N_DEV, 4 * PC // N_DEV, PC),
             g_wp_out.reshape(N_DEV, E // N_DEV, D)]
    pair_flight, pair_token = split_start(
        "rs_pair_start", fulls, [lax.empty((4,) + f.shape[1:], F32) for f in fulls], _pair_plan(), 4)
    dx_z = mm_dx_part(dz0, w_in8, 3 * seg_tiles, "mm_dxn0_z", after=pair_token)
    dx_0 = mm_dx_part(dPs[0], w_in8, 0, "mm_dxn0_0", after=dx_z)
    fulls, sibs = split_wait("rs_pair_wait", pair_flight, _pair_plan(), dx_0)
    parts = [pair_add(f, s, cidx, f"pair_add{k}") for k, (f, s) in enumerate(zip(fulls, sibs))]
    chips_flight, chips_token = split_start(
        "rs_chips_start", parts, [jnp.zeros(p.shape, BF16) for p in parts], _chips_plan(), 3)
    dx_1 = mm_dx_part(dPs[1], w_in8, seg_tiles, "mm_dxn0_1", after=chips_token)
    dx_2 = mm_dx_part(dPs[2], w_in8, 2 * seg_tiles, "mm_dxn0_2", after=dx_1)
    grad_x, dg_pre0 = pre0_bwd(dx_0, dx_z, dx_1, dx_2, dh1, x2, norm_pre[0:1])
    parts, recvs = split_wait("rs_chips_wait", chips_flight, _chips_plan(), grad_x)
    shards = [(attn_w_in, m_attn_w_in, v_attn_w_in), (attn_w_out, m_attn_w_out, v_attn_w_out),
              (pool_w_in, m_pool_w_in, v_pool_w_in), (pool_w_grp, m_pool_w_grp, v_pool_w_grp),
              (pool_w_out, m_pool_w_out, v_pool_w_out)]
    big = []
    for k, (recv, part, (w, m, v)) in enumerate(zip(recvs, parts, shards)):
        shp = w.shape
        r2 = recv.shape[1:]
        res = adamw_sum(recv, part, chip, w.reshape(r2), m.reshape(r2), v.reshape(r2), f"adamw{k}")
        big.append([t.reshape(shp) for t in res])

    smalls = jnp.concatenate([dg_pre0.sum(0, keepdims=True), dg_pre1.sum(0, keepdims=True),
                              dg_post0.sum(0, keepdims=True), dg_post1.sum(0, keepdims=True),
                              db_p.sum(0).reshape(2, D), dscale_p.sum(0).reshape(2, D)], axis=0)
    (smalls8,) = all_gather([smalls], "gather_small_grads")
    tot = sum_slots(smalls8, "sum_small_grads")
    g_norm_pre, g_norm_post = tot[0:2], tot[2:4]
    g_b = lax.dynamic_slice_in_dim(tot[4:6].reshape(4, PC), dev * (PC // N_DEV), PC // N_DEV, axis=1)[None]
    g_scale = lax.dynamic_slice_in_dim(tot[6:8].reshape(1, E), dev * (E // N_DEV), E // N_DEV, axis=1)
    sm = [adamw_small(g_norm_pre, norm_pre, m_norm_pre, v_norm_pre, "adamw_norm_pre"),
          adamw_small(g_norm_post, norm_post, m_norm_post, v_norm_post, "adamw_norm_post"),
          adamw_small(g_b, pool_b_grp, m_pool_b_grp, v_pool_b_grp, "adamw_b"),
          adamw_small(g_scale, pool_scale, m_pool_scale, v_pool_scale, "adamw_scale")]

    grads = [g_norm_pre, g_norm_post, big[0][0], big[1][0], big[2][0], big[3][0], g_b, g_scale, big[4][0]]

    def pick(k):
        return [sm[0][k - 1], sm[1][k - 1], big[0][k], big[1][k], big[2][k], big[3][k], sm[2][k - 1], sm[3][k - 1],
                big[4][k]]

    return (loss, grad_x[None], *grads, *pick(1), *pick(2), *pick(3))
```

```python
import math

import jax
import jax.numpy as jnp
from jax import lax
from jax.experimental import pallas as pl
from jax.experimental.pallas import tpu as pltpu

F32 = jnp.float32
BF16 = jnp.bfloat16
SDS = jax.ShapeDtypeStruct

N_DEV = 8
D = 1024
E = 2048
HD = 128
NH = E // HD
DIL = (1, 4, 16)
QB = 128
SEG = 3 * E
W_IN_COLS = 3 * SEG + E
W_SHARD = W_IN_COLS // N_DEV
CT = 512
PC = E // 4
EPS = 1e-6
NEG = -1e30
SCALE = 1.0 / math.sqrt(HD)
LR, B1, B2, ADAM_EPS, WD, STEP = 0.001, 0.9, 0.999, 1e-08, 0.01, 10
MIB = 1024 * 1024
ANY = pl.BlockSpec(memory_space=pl.ANY)
MESH = pl.DeviceIdType.MESH


def _cp(sem, mb):
    return pltpu.CompilerParams(dimension_semantics=sem, vmem_limit_bytes=mb * MIB)


def _dot(a, b):
    return jnp.dot(a, b, preferred_element_type=F32)


def _dot_nt(a, b):
    return lax.dot_general(a, b, (((1,), (1,)), ((), ())), preferred_element_type=F32)


def _rms(h):
    return lax.rsqrt(jnp.mean(h * h, axis=-1, keepdims=True) + EPS)


def _row_tile(R, C, budget):
    tr = R
    while tr * C * 4 > budget and tr % 16 == 0:
        tr //= 2
    return tr


def _fold8(t):
    return t.reshape(t.shape[0] // 8, 8, t.shape[1]).sum(axis=0)


def _sigmoid(z):
    return pl.reciprocal(1.0 + jnp.exp(-z), approx=True)


LANES = 128


def _scr(rows, C):
    return pltpu.VMEM((C // LANES, rows, LANES), F32)


def _scr_put(scr, val):
    for c in range(scr.shape[0]):
        scr[c] = val[:, c * LANES:(c + 1) * LANES]


def _scr_get(scr):
    return jnp.concatenate([scr[c] for c in range(scr.shape[0])], axis=1)


def _store_perm(dst_ref, scr, d):
    n = dst_ref.shape[1]
    for r in range(d):
        for c in range(scr.shape[0]):
            dst_ref[r, :, c * LANES:(c + 1) * LANES] = scr[c, pl.ds(r, n, stride=d), :].astype(dst_ref.dtype)


def _load_perm(scr, src_ref, d, add=False):
    n = src_ref.shape[1]
    for r in range(d):
        rows = pl.ds(r, n, stride=d)
        for c in range(scr.shape[0]):
            v = src_ref[r, :, c * LANES:(c + 1) * LANES].astype(F32)
            scr[c, rows, :] = scr[c, rows, :] + v if add else v


def _rope(t, c, s1, s2):
    t = t.astype(BF16)
    return t * c + pltpu.roll(t, HD - 16, 1) * s1 + pltpu.roll(t, 16, 1) * s2


def _unrope(t, c, s1, s2):
    t = t.astype(BF16)
    return t * c - pltpu.roll(t, HD - 16, 1) * s1 - pltpu.roll(t, 16, 1) * s2


def _mesh_pos():
    return lax.axis_index("x"), lax.axis_index("y"), lax.axis_index("c")


def all_gather(arrs, name):
    n = len(arrs)

    def body(*refs):
        ins, outs = refs[:n], refs[n:2 * n]
        send_sems, recv_sems, local_sems = refs[2 * n:]
        x, y, c = _mesh_pos()
        me, sib = (x, y, c), (x, y, 1 - c)
        chips = [(1 - x, y), (x, 1 - y), (1 - x, 1 - y)]

        def slot(p):
            return 4 * p[0] + 2 * p[1] + p[2]

        def copy(a, k, block, to, src=None):
            dst = outs[a].at[slot(block)]
            return pltpu.make_async_remote_copy(
                src_ref=dst if src is None else src, dst_ref=dst,
                send_sem=send_sems.at[a, k], recv_sem=recv_sems.at[a, k],
                device_id=to, device_id_type=MESH)

        mine = [pltpu.make_async_copy(ins[a], outs[a].at[slot(me)], local_sems.at[a]) for a in range(n)]
        for cp in mine:
            cp.start()
        first = []
        for a in range(n):
            first.append(copy(a, 0, me, sib, src=ins[a]))
            for j, chip in enumerate(chips):
                first.append(copy(a, 1 + j, me, (*chip, c), src=ins[a]))
        for cp in first:
            cp.start()
        passed = []
        for j, chip in enumerate(chips):
            for a in range(n):
                copy(a, 1 + j, (*chip, c), me).wait_recv()
                fw = copy(a, 4 + j, (*chip, c), sib)
                fw.start()
                passed.append(fw)
        for a in range(n):
            copy(a, 0, sib, me).wait_recv()
        for j, chip in enumerate(chips):
            for a in range(n):
                copy(a, 4 + j, (*chip, 1 - c), me).wait_recv()
        for cp in first + passed:
            cp.wait_send()
        for cp in mine:
            cp.wait()

    return pl.pallas_call(
        body, name=name,
        out_shape=[SDS((N_DEV,) + a.shape, a.dtype) for a in arrs],
        in_specs=[ANY] * n, out_specs=[ANY] * n,
        scratch_shapes=[pltpu.SemaphoreType.DMA((n, 7)), pltpu.SemaphoreType.DMA((n, 7)),
                        pltpu.SemaphoreType.DMA((n,))],
    )(*arrs)


HBM_SPEC = pl.BlockSpec(memory_space=pltpu.HBM)
SEM_SPEC = pl.BlockSpec(memory_space=pltpu.SEMAPHORE)
EFFECT = pltpu.SideEffectType.DATAFLOW_SIDE_EFFECTING


def _pair_plan():
    def plan(x, y, c):
        return [(2 * q + (1 - c), q, (x, y, 1 - c)) for q in range(4)]
    return plan


def _chips_plan():
    def plan(x, y, c):
        chips = [(1 - x, y), (x, 1 - y), (1 - x, 1 - y)]
        return [(2 * cx + cy, 2 * x + y, (cx, cy, c)) for cx, cy in chips]
    return plan


def _hop1_plan():
    def plan(x, y, c):
        me = 4 * x + 2 * y + c
        return [(None, me, (x, y, 1 - c)), (None, me, (1 - x, y, c)), (None, me, (x, 1 - y, c))]
    return plan


def _hop2_plan(rows):
    half = rows // 2

    def plan(x, y, c):
        sx, sy = 4 * (1 - x) + 2 * y + c, 4 * x + 2 * (1 - y) + c
        top, bottom = pl.ds(0, half), pl.ds(half, half)
        return [((sx, top), (sx, top), (x, 1 - y, c)), ((sy, bottom), (sy, bottom), (1 - x, y, c)),
                (sx, sx, (x, y, 1 - c)), (sy, sy, (x, y, 1 - c))]
    return plan


def _hop3_plan():
    def plan(x, y, c):
        sd = 4 * (1 - x) + 2 * (1 - y) + c
        return [(sd, sd, (x, y, 1 - c))]
    return plan


def _peers_plan():
    def plan(x, y, c):
        out = []
        for k in range(1, N_DEV):
            fx, fy, fc = (k >> 2) & 1, (k >> 1) & 1, k & 1
            px, py, pc = (x + fx) % 2, (y + fy) % 2, (c + fc) % 2
            out.append((None, 4 * x + 2 * y + c, (px, py, pc)))
        return out
    return plan


def _split_copies(plan, srcs, lands, send_sems, recv_sems):
    x, y, c = _mesh_pos()
    cps = []
    for a, (src, land) in enumerate(zip(srcs, lands)):
        steps = plan(x, y, c)
        for k, (si, li, to) in enumerate(steps):
            sem = a * len(steps) + k
            cps.append(pltpu.make_async_remote_copy(
                src_ref=src if si is None else src.at[si], dst_ref=land.at[li],
                send_sem=send_sems.at[sem], recv_sem=recv_sems.at[sem],
                device_id=to, device_id_type=MESH))
    return cps


def split_start(name, srcs, lands, plan, nk):
    n = len(srcs)
    ops = list(srcs) + ([] if lands is None else list(lands))
    nb = len(ops)

    def body(*refs):
        token = refs[-1]
        for cp in _split_copies(plan, refs[:n], refs[nb - n:nb], refs[nb], refs[nb + 1]):
            cp.start()
        token[...] = jnp.zeros_like(token)

    ops = [pltpu.with_memory_space_constraint(a, pltpu.HBM) for a in ops]
    res = pl.pallas_call(
        body, name=name,
        out_shape=(pltpu.SemaphoreType.DMA((n * nk,)), pltpu.SemaphoreType.DMA((n * nk,)),
                   *[pltpu.HBM(a.shape, a.dtype) for a in ops], SDS((8, 128), F32)),
        in_specs=[HBM_SPEC] * nb,
        out_specs=(SEM_SPEC, SEM_SPEC, *[HBM_SPEC] * nb, pl.BlockSpec(memory_space=pltpu.VMEM)),
        input_output_aliases={i: 2 + i for i in range(nb)},
        compiler_params=pltpu.CompilerParams(has_side_effects=EFFECT),
    )(*ops)
    return res[:-1], res[-1]


def split_wait(name, flight, plan, after, inplace=False):
    send_sems, recv_sems = flight[0], flight[1]
    bufs = list(flight[2:])
    nb = len(bufs)
    n = nb if inplace else nb // 2

    def body(*refs):
        for cp in _split_copies(plan, refs[:n], refs[nb - n:nb], refs[nb], refs[nb + 1]):
            cp.wait_send()
            cp.wait_recv()

    res = pl.pallas_call(
        body, name=name,
        out_shape=[pltpu.HBM(a.shape, a.dtype) for a in bufs],
        in_specs=[HBM_SPEC] * nb + [SEM_SPEC, SEM_SPEC, ANY],
        out_specs=[HBM_SPEC] * nb,
        input_output_aliases={i: i for i in range(nb)},
        compiler_params=pltpu.CompilerParams(has_side_effects=EFFECT),
    )(*bufs, send_sems, recv_sems, after)
    return res[:n], res[nb - n:]


def pair_add(full, sib, cidx, name):
    _, R, C = full.shape
    tr = _row_tile(R, C, MIB)

    def body(c_ref, a_ref, b_ref, o_ref):
        o_ref[...] = (a_ref[...] + b_ref[...]).astype(BF16)

    return pl.pallas_call(
        body, name=name,
        grid_spec=pltpu.PrefetchScalarGridSpec(
            num_scalar_prefetch=1, grid=(4, R // tr),
            in_specs=[pl.BlockSpec((None, tr, C), lambda q, i, cr: (2 * q + cr[0], i, 0)),
                      pl.BlockSpec((None, tr, C), lambda q, i, cr: (q, i, 0))],
            out_specs=pl.BlockSpec((None, tr, C), lambda q, i, cr: (q, i, 0))),
        out_shape=SDS((4, R, C), BF16),
        compiler_params=_cp(("parallel", "parallel"), 32),
    )(cidx, full, sib)


def _adam_math(w, g, m, v):
    m2 = B1 * m + (1.0 - B1) * g
    v2 = B2 * v + (1.0 - B2) * (g * g)
    m_hat = m2 / (1.0 - B1 ** STEP)
    v_hat = v2 / (1.0 - B2 ** STEP)
    delta = -LR * (m_hat / (jnp.sqrt(v_hat) + ADAM_EPS) + WD * w)
    return delta, m2, v2


def adamw_sum(recv, part, chip, w, m, v, name):
    K, R, C = recv.shape
    tr = _row_tile(R, C, MIB)

    def body(chip_ref, r_ref, p_ref, w_ref, m_ref, v_ref, g_ref, d_ref, m2_ref, v2_ref):
        g = r_ref[0].astype(F32)
        for k in range(1, K):
            g = g + r_ref[k].astype(F32)
        g = g + p_ref[...].astype(F32)
        delta, m2, v2 = _adam_math(w_ref[...], g, m_ref[...], v_ref[...])
        g_ref[...] = g
        d_ref[...] = delta
        m2_ref[...] = m2
        v2_ref[...] = v2

    tile = pl.BlockSpec((tr, C), lambda i, cr: (i, 0))
    return pl.pallas_call(
        body, name=name,
        grid_spec=pltpu.PrefetchScalarGridSpec(
            num_scalar_prefetch=1, grid=(R // tr,),
            in_specs=[pl.BlockSpec((K, tr, C), lambda i, cr: (0, i, 0)),
                      pl.BlockSpec((None, tr, C), lambda i, cr: (cr[0], i, 0)), tile, tile, tile],
            out_specs=[tile] * 4),
        out_shape=[SDS((R, C), F32)] * 4,
        compiler_params=_cp(("parallel",), 32),
    )(chip, recv, part, w, m, v)


def adamw_small(g, w, m, v, name):
    def body(g_ref, w_ref, m_ref, v_ref, d_ref, m2_ref, v2_ref):
        delta, m2, v2 = _adam_math(w_ref[...], g_ref[...], m_ref[...], v_ref[...])
        d_ref[...] = delta
        m2_ref[...] = m2
        v2_ref[...] = v2

    return pl.pallas_call(body, name=name, out_shape=[SDS(w.shape, F32)] * 3)(g, w, m, v)


def sum_slots(a, name):
    K = a.shape[0]

    def body(a_ref, o_ref):
        t = a_ref[0]
        for k in range(1, K):
            t = t + a_ref[k]
        o_ref[...] = t

    return pl.pallas_call(body, name=name, out_shape=SDS(a.shape[1:], F32))(a)


def norm_pre0(x, g, after):
    S = x.shape[0]
    ts = 512

    def body(x_ref, g_ref, after_ref, o_ref, o4_ref, o16_ref, ot_ref, scr):
        h = x_ref[...]
        xn = h * _rms(h) * g_ref[...]
        o_ref[...] = xn.astype(BF16)
        ot_ref[...] = xn.T.astype(BF16)
        _scr_put(scr, xn)
        _store_perm(o4_ref, scr, 4)
        _store_perm(o16_ref, scr, 16)

    return pl.pallas_call(
        body, name="norm_pre0", grid=(S // ts,),
        in_specs=[pl.BlockSpec((ts, D), lambda i: (i, 0)), pl.BlockSpec((1, D), lambda i: (0, 0)), ANY],
        out_specs=[pl.BlockSpec((ts, D), lambda i: (i, 0)),
                   pl.BlockSpec((4, ts // 4, D), lambda i: (0, i, 0)),
                   pl.BlockSpec((16, ts // 16, D), lambda i: (0, i, 0)),
                   pl.BlockSpec((D, ts), lambda i: (0, i))],
        out_shape=[SDS((S, D), BF16), SDS((4, S // 4, D), BF16), SDS((16, S // 16, D), BF16), SDS((D, S), BF16)],
        scratch_shapes=[_scr(ts, D)],
        compiler_params=_cp(("parallel",), 32),
    )(x, g, after)


def transpose_rows(a, name):
    S, C = a.shape
    ts = 512

    def body(a_ref, o_ref):
        o_ref[...] = a_ref[...].astype(F32).T.astype(BF16)

    return pl.pallas_call(
        body, name=name, grid=(S // ts,),
        in_specs=[pl.BlockSpec((ts, C), lambda i: (i, 0))],
        out_specs=pl.BlockSpec((C, ts), lambda i: (0, i)),
        out_shape=SDS((C, S), BF16),
        compiler_params=_cp(("parallel",), 32),
    )(a)


def post0_pre1(x, a0, g_post, g_pre):
    S = x.shape[0]
    ts = 512

    def body(x_ref, a_ref, gp_ref, gn_ref, h_ref, o_ref, ot_ref):
        a = a_ref[...]
        h1 = x_ref[...] + a * _rms(a) * gp_ref[...]
        h_ref[...] = h1
        xn = h1 * _rms(h1) * gn_ref[...]
        o_ref[...] = xn.astype(BF16)
        ot_ref[...] = xn.T.astype(BF16)

    row = pl.BlockSpec((ts, D), lambda i: (i, 0))
    vec = pl.BlockSpec((1, D), lambda i: (0, 0))
    return pl.pallas_call(
        body, name="post0_pre1", grid=(S // ts,),
        in_specs=[row, row, vec, vec],
        out_specs=[row, row, pl.BlockSpec((D, ts), lambda i: (0, i))],
        out_shape=[SDS((S, D), F32), SDS((S, D), BF16), SDS((D, S), BF16)],
        compiler_params=_cp(("parallel",), 40),
    )(x, a0, g_post, g_pre)


def post1_loss(h1, a1, target, g_post):
    S = h1.shape[0]
    ts = 512

    def body(h_ref, a_ref, t_ref, g_ref, dh_ref, da_ref, loss_ref, dg_ref):
        @pl.when(pl.program_id(0) == 0)
        def _():
            loss_ref[...] = jnp.zeros_like(loss_ref)
            dg_ref[...] = jnp.zeros_like(dg_ref)

        a = a_ref[...]
        g = g_ref[...]
        rp = _rms(a)
        yhat = a * rp
        e = h_ref[...] + yhat * g - t_ref[...]
        loss_ref[...] += _fold8(e * e)
        dh = e * (1.0 / D)
        dh_ref[...] = dh
        dg_ref[...] += _fold8(dh * yhat)
        dyh = dh * g
        da = rp * (dyh - yhat * jnp.mean(dyh * yhat, axis=-1, keepdims=True))
        da_ref[...] = da.astype(BF16)

    row = pl.BlockSpec((ts, D), lambda i: (i, 0))
    acc = pl.BlockSpec((8, D), lambda i: (0, 0))
    return pl.pallas_call(
        body, name="post1_loss", grid=(S // ts,),
        in_specs=[row, row, row, pl.BlockSpec((1, D), lambda i: (0, 0))],
        out_specs=[row, row, acc, acc],
        out_shape=[SDS((S, D), F32), SDS((S, D), BF16), SDS((8, D), F32), SDS((8, D), F32)],
        compiler_params=_cp(("arbitrary",), 40),
    )(h1, a1, target, g_post)


def mid_bwd(dxn1, dh2, h1, a0, g_pre1, g_post0):
    S = h1.shape[0]
    ts = 512

    def body(dx_ref, dh2_ref, h_ref, a_ref, gn_ref, gp_ref, dh1_ref, da_ref, dgn_ref, dgp_ref):
        @pl.when(pl.program_id(0) == 0)
        def _():
            dgn_ref[...] = jnp.zeros_like(dgn_ref)
            dgp_ref[...] = jnp.zeros_like(dgp_ref)

        h = h_ref[...]
        r1 = _rms(h)
        xhat = h * r1
        dxn = dx_ref[...]
        dgn_ref[...] += _fold8(dxn * xhat)
        dxh = dxn * gn_ref[...]
        dh1 = dh2_ref[...] + r1 * (dxh - xhat * jnp.mean(dxh * xhat, axis=-1, keepdims=True))
        dh1_ref[...] = dh1
        a = a_ref[...]
        rp = _rms(a)
        yhat = a * rp
        dgp_ref[...] += _fold8(dh1 * yhat)
        dyh = dh1 * gp_ref[...]
        da = rp * (dyh - yhat * jnp.mean(dyh * yhat, axis=-1, keepdims=True))
        da_ref[...] = da.astype(BF16)

    row = pl.BlockSpec((ts, D), lambda i: (i, 0))
    vec = pl.BlockSpec((1, D), lambda i: (0, 0))
    acc = pl.BlockSpec((8, D), lambda i: (0, 0))
    return pl.pallas_call(
        body, name="mid_bwd", grid=(S // ts,),
        in_specs=[row, row, row, row, vec, vec],
        out_specs=[row, row, acc, acc],
        out_shape=[SDS((S, D), F32), SDS((S, D), BF16), SDS((8, D), F32), SDS((8, D), F32)],
        compiler_params=_cp(("arbitrary",), 48),
    )(dxn1, dh2, h1, a0, g_pre1, g_post0)


def pre0_bwd(dx_tok, dx_z, dx4, dx16, dh1, x, g_pre0):
    S = x.shape[0]
    ts = 512

    def body(da_ref, dz_ref, d4_ref, d16_ref, dh_ref, x_ref, g_ref, gx_ref, dg_ref, scr):
        @pl.when(pl.program_id(0) == 0)
        def _():
            dg_ref[...] = jnp.zeros_like(dg_ref)

        _scr_put(scr, da_ref[...] + dz_ref[...])
        _load_perm(scr, d4_ref, 4, add=True)
        _load_perm(scr, d16_ref, 16, add=True)
        h = x_ref[...]
        r = _rms(h)
        xhat = h * r
        dxn = _scr_get(scr)
        dg_ref[...] += _fold8(dxn * xhat)
        dxh = dxn * g_ref[...]
        gx_ref[...] = dh_ref[...] + r * (dxh - xhat * jnp.mean(dxh * xhat, axis=-1, keepdims=True))

    row = pl.BlockSpec((ts, D), lambda i: (i, 0))
    return pl.pallas_call(
        body, name="pre0_bwd", grid=(S // ts,),
        in_specs=[row, row, pl.BlockSpec((4, ts // 4, D), lambda i: (0, i, 0)),
                  pl.BlockSpec((16, ts // 16, D), lambda i: (0, i, 0)), row, row,
                  pl.BlockSpec((1, D), lambda i: (0, 0))],
        out_specs=[row, pl.BlockSpec((8, D), lambda i: (0, 0))],
        out_shape=[SDS((S, D), F32), SDS((8, D), F32)],
        scratch_shapes=[_scr(ts, D)],
        compiler_params=_cp(("arbitrary",), 48),
    )(dx_tok, dx_z, dx4.reshape(4, S // 4, D), dx16.reshape(16, S // 16, D), dh1, x, g_pre0)


def _w_tile(tile0):
    per = W_SHARD // CT
    return lambda t: ((tile0 + t) // per, 0, (tile0 + t) % per)


def mm_in(xn, w8, tile0, ntiles, tabs, name, after=None):
    S = xn.shape[0]
    tm = 2048
    wt = _w_tile(tile0)

    def body(a_ref, b_ref, *rest):
        o_ref = rest[-1]
        rc = 512
        for u in range(tm // rc):
            rows = slice(u * rc, (u + 1) * rc)
            r = _dot(a_ref[rows, :], b_ref[...])
            if tabs is None:
                o_ref[rows, :] = r.astype(BF16)
                continue
            c_ref, s1_ref, s2_ref = rest[:3]
            rot = pl.program_id(1) < 2 * E // CT
            qs = jnp.where(pl.program_id(1) < E // CT, SCALE, 1.0)
            c = jnp.where(rot, (c_ref[rows, :] * qs).astype(BF16), 1.0)
            s1 = jnp.where(rot, (s1_ref[rows, :] * qs).astype(BF16), 0.0)
            s2 = jnp.where(rot, (s2_ref[rows, :] * qs).astype(BF16), 0.0)
            for hh in range(CT // HD):
                cs = slice(hh * HD, (hh + 1) * HD)
                o_ref[rows, cs] = _rope(r[:, cs], c, s1, s2).astype(BF16)

    tab = pl.BlockSpec((tm, HD), lambda i, t: (i, 0))
    return pl.pallas_call(
        body, name=name, grid=(S // tm, ntiles),
        in_specs=[pl.BlockSpec((tm, D), lambda i, t: (i, 0)),
                  pl.BlockSpec((None, D, CT), lambda i, t: wt(t))] + ([] if tabs is None else [tab] * 3)
        + ([] if after is None else [ANY]),
        out_specs=pl.BlockSpec((tm, CT), lambda i, t: (i, t)),
        out_shape=SDS((S, ntiles * CT), BF16),
        compiler_params=_cp(("parallel", "parallel"), 48),
    )(xn, w8, *(() if tabs is None else tabs), *(() if after is None else (after,)))


def mm_rows(a, b, name, out_dtype, tm=1024):
    M, K = a.shape
    N = b.shape[1]

    def body(a_ref, b_ref, o_ref):
        for cidx in range(N // 256):
            col = slice(cidx * 256, (cidx + 1) * 256)
            o_ref[:, col] = _dot(a_ref[...], b_ref[:, col]).astype(out_dtype)

    return pl.pallas_call(
        body, name=name, grid=(M // tm,),
        in_specs=[pl.BlockSpec((tm, K), lambda i: (i, 0)), pl.BlockSpec((K, N), lambda i: (0, 0))],
        out_specs=pl.BlockSpec((tm, N), lambda i: (i, 0)),
        out_shape=SDS((M, N), out_dtype),
        compiler_params=_cp(("parallel",), 48),
    )(a, b)


def mm_acc(a, b, name, *, grid, a_spec, b_spec, o_spec, o_shape, acc_shape, write, vmem=48):
    nk = grid[-1]

    def body(a_ref, b_ref, o_ref, acc_ref):
        k = pl.program_id(len(grid) - 1)

        @pl.when(k == 0)
        def _():
            acc_ref[...] = jnp.zeros_like(acc_ref)

        acc_ref[...] += _dot(a_ref[...], b_ref[...])

        @pl.when(k == nk - 1)
        def _():
            write(o_ref, acc_ref)

    return pl.pallas_call(
        body, name=name, grid=grid, in_specs=[a_spec, b_spec], out_specs=o_spec, out_shape=o_shape,
        scratch_shapes=[pltpu.VMEM(acc_shape, F32)],
        compiler_params=_cp(("parallel",) * (len(grid) - 1) + ("arbitrary",), vmem),
    )(a, b)


def _write_plain(o_ref, acc_ref):
    o_ref[...] = acc_ref[...]


def mm_wgrad_rows(at, b, name):
    M, S = at.shape
    N = b.shape[1]
    tm, tk = 1024, 1024
    return mm_acc(at, b, name, grid=(M // tm, S // tk),
                  a_spec=pl.BlockSpec((tm, tk), lambda i, k: (i, k)),
                  b_spec=pl.BlockSpec((tk, N), lambda i, k: (k, 0)),
                  o_spec=pl.BlockSpec((tm, N), lambda i, k: (i, 0)),
                  o_shape=SDS((M, N), F32), acc_shape=(tm, N), write=_write_plain)


def mm_wgrad_cols(at, b, name, *, shard):
    M, S = at.shape
    tk = 1024
    nb = 2
    tn = nb * shard

    def write(o_ref, acc_ref):
        for u in range(nb):
            o_ref[u] = acc_ref[:, u * shard:(u + 1) * shard]

    return mm_acc(at, b, name, grid=(N_DEV // nb, S // tk),
                  a_spec=pl.BlockSpec((M, tk), lambda t, k: (0, k)),
                  b_spec=pl.BlockSpec((tk, tn), lambda t, k: (k, t)),
                  o_spec=pl.BlockSpec((nb, M, shard), lambda t, k: (t, 0, 0)),
                  o_shape=SDS((N_DEV, M, shard), F32), acc_shape=(M, tn), write=write)


def mm_dwg(pooled_t, dh):
    S = dh.shape[0]
    tk = 2048
    return mm_acc(pooled_t, dh, "mm_dwg", grid=(4, S // tk),
                  a_spec=pl.BlockSpec((PC, tk), lambda g, k: (g, k)),
                  b_spec=pl.BlockSpec((tk, PC), lambda g, k: (k, g)),
                  o_spec=pl.BlockSpec((None, PC, PC), lambda g, k: (g, 0, 0)),
                  o_shape=SDS((4, PC, PC), F32), acc_shape=(PC, PC), write=_write_plain)


def mm_dx_full(da, w, name):
    S, K = da.shape
    N = w.shape[0]
    tm = 512

    def body(a_ref, b_ref, o_ref):
        o_ref[...] = _dot_nt(a_ref[...], b_ref[...])

    return pl.pallas_call(
        body, name=name, grid=(S // tm,),
        in_specs=[pl.BlockSpec((tm, K), lambda i: (i, 0)), pl.BlockSpec((N, K), lambda i: (0, 0))],
        out_specs=pl.BlockSpec((tm, N), lambda i: (i, 0)),
        out_shape=SDS((S, N), F32),
        compiler_params=_cp(("parallel",), 48),
    )(da, w)


def mm_dw_in_part(at, b, tile0, prev, name):
    M, S = at.shape
    ntiles = b.shape[1] // CT
    tk = 2048
    nk = S // tk
    wt = _w_tile(tile0)

    def body(a_ref, b_ref, *rest):
        o_ref, acc_ref = rest[-2:]
        k = pl.program_id(1)

        @pl.when(k == 0)
        def _():
            acc_ref[...] = jnp.zeros_like(acc_ref)

        acc_ref[...] += _dot(a_ref[...], b_ref[...])

        @pl.when(k == nk - 1)
        def _():
            o_ref[...] = acc_ref[...]

    return pl.pallas_call(
        body, name=name, grid=(ntiles, nk),
        in_specs=[pl.BlockSpec((M, tk), lambda t, k: (0, k)), pl.BlockSpec((tk, CT), lambda t, k: (k, t))]
        + ([] if prev is None else [ANY]),
        out_specs=pl.BlockSpec((None, M, CT), lambda t, k: wt(t)),
        out_shape=SDS((N_DEV, M, W_SHARD), F32),
        scratch_shapes=[pltpu.VMEM((M, CT), F32)],
        input_output_aliases={} if prev is None else {2: 0},
        compiler_params=_cp(("parallel", "arbitrary"), 48),
    )(at, b, *(() if prev is None else (prev,)))


def mm_dx_part(da, w8, tile0, name, after=None):
    S = da.shape[0]
    ntiles = da.shape[1] // CT
    tm = 2048
    wt = _w_tile(tile0)

    def body(a_ref, b_ref, *rest):
        o_ref, acc_ref = rest[-2:]
        t = pl.program_id(1)

        @pl.when(t == 0)
        def _():
            acc_ref[...] = jnp.zeros_like(acc_ref)

        acc_ref[...] += _dot_nt(a_ref[...], b_ref[...])

        @pl.when(t == ntiles - 1)
        def _():
            o_ref[...] = acc_ref[...]

    return pl.pallas_call(
        body, name=name, grid=(S // tm, ntiles),
        in_specs=[pl.BlockSpec((tm, CT), lambda i, t: (i, t)), pl.BlockSpec((None, D, CT), lambda i, t: wt(t))]
        + ([] if after is None else [ANY]),
        out_specs=pl.BlockSpec((tm, D), lambda i, t: (i, 0)),
        out_shape=SDS((S, D), F32),
        scratch_shapes=[pltpu.VMEM((tm, D), F32)],
        compiler_params=_cp(("parallel", "arbitrary"), 56),
    )(da, w8, *(() if after is None else (after,)))


HEADS_PER_STEP = 2
AHEAD = 2


def _band_masks(not_first):
    row = lax.broadcasted_iota(jnp.int32, (QB, QB), 0)
    col = lax.broadcasted_iota(jnp.int32, (QB, QB), 1)
    cur = jnp.where(col <= row, 0.0, NEG)
    prev = jnp.where(col >= row, 0.0, NEG)
    first = jnp.where(jnp.logical_and(col >= row, not_first), 0.0, NEG)
    return col, jnp.concatenate([prev, cur], axis=1), jnp.concatenate([first, cur], axis=1)


def _fill_kv(ext, qkv_ref, kh_ref, vh_ref):
    ext[0:QB, 0:E] = kh_ref[...]
    ext[0:QB, E:2 * E] = vh_ref[...]
    ext[QB:, :] = qkv_ref[:, E:3 * E]


def attn_fwd(P, g, d):
    S = P.shape[0]
    L = S // d
    T = min(512, L)
    nq = T // QB
    ni = L // T

    def body(qkv_ref, kh_ref, vh_ref, o_ref, lse_ref, ext):
        col, mask, mask_first = _band_masks(pl.program_id(1) > 0)
        lse_ref[...] = jnp.zeros_like(lse_ref)
        _fill_kv(ext, qkv_ref, kh_ref, vh_ref)

        def heads(hp, carry):
            def front(h, j):
                cq = pl.ds(pl.multiple_of(h * HD, HD), HD)
                rows = slice(j * QB, (j + 1) * QB)
                krows = slice(j * QB, (j + 2) * QB)
                s = _dot_nt(qkv_ref[rows, cq], ext[krows, cq]) + (mask_first if j == 0 else mask)
                m = jnp.max(s, axis=1, keepdims=True)
                p = jnp.exp(s - m)
                den = jnp.sum(p, axis=1, keepdims=True)
                lse_ref[rows, :] = jnp.where(col == h, m + jnp.log(den), lse_ref[rows, :])
                return p.astype(BF16), den

            def back(h, j, p, den):
                off = pl.multiple_of(h * HD, HD)
                rows = slice(j * QB, (j + 1) * QB)
                krows = slice(j * QB, (j + 2) * QB)
                o_ref[rows, pl.ds(off, HD)] = (_dot(p, ext[krows, pl.ds(E + off, HD)]) / den).astype(BF16)

            items = [(HEADS_PER_STEP * hp + hh, j) for hh in range(HEADS_PER_STEP) for j in range(nq)]
            queue = [front(*it) for it in items[:AHEAD]]
            for u, it in enumerate(items):
                if u + AHEAD < len(items):
                    queue.append(front(*items[u + AHEAD]))
                back(*it, *queue.pop(0))
            return carry

        lax.fori_loop(0, NH // HEADS_PER_STEP, heads, 0)

    halo = lambda r, i: jnp.maximum(r * (L // QB) + i * nq - 1, 0)
    return pl.pallas_call(
        body, name=f"attn_fwd{g}", grid=(d, ni),
        in_specs=[pl.BlockSpec((T, SEG), lambda r, i: (r * ni + i, 0)),
                  pl.BlockSpec((QB, E), lambda r, i: (halo(r, i), 1)),
                  pl.BlockSpec((QB, E), lambda r, i: (halo(r, i), 2))],
        out_specs=[pl.BlockSpec((T, E), lambda r, i: (r * ni + i, 0)),
                   pl.BlockSpec((T, HD), lambda r, i: (r * ni + i, 0))],
        out_shape=[SDS((S, E), BF16), SDS((S, HD), F32)],
        scratch_shapes=[pltpu.VMEM((T + QB, 2 * E), BF16)],
        compiler_params=_cp(("parallel", "parallel"), 48),
    )(P, P, P)


def _perm_specs(ts, C):
    return [pl.BlockSpec((ts, C), lambda i: (i, 0)),
            pl.BlockSpec((4, ts // 4, C), lambda i: (0, i, 0)),
            pl.BlockSpec((16, ts // 16, C), lambda i: (0, i, 0))]


def _perm_shapes(S, C, dtype):
    return [SDS((S, C), dtype), SDS((4, S // 4, C), dtype), SDS((16, S // 16, C), dtype)]


def combine_fwd(os_, lses, z, ehot):
    S = z.shape[0]
    ts = 256

    def body(o0, o1, o2, l0, l1, l2, z_ref, e_ref, y0, y1, y2, s0, s1, s2, ya_ref, yat_ref,
             so1, so2, sl1, sl2, sy, sl):
        _load_perm(so1, o1, 4)
        _load_perm(so2, o2, 16)
        _load_perm(sl1, l1, 4)
        _load_perm(sl2, l2, 16)
        ls = [l0[...], sl1[0], sl2[0]]
        m = jnp.maximum(jnp.maximum(ls[0], ls[1]), ls[2])
        es = [jnp.exp(l - m) for l in ls]
        den = es[0] + es[1] + es[2]
        sl[0] = m + jnp.log(den)
        y = None
        for e, o in zip(es, (o0[...].astype(F32), _scr_get(so1), _scr_get(so2))):
            w = e / den
            hi = w.astype(BF16)
            lo = (w - hi.astype(F32)).astype(BF16)
            wb = _dot(hi, e_ref[...]) + _dot(lo, e_ref[...])
            y = wb * o if y is None else y + wb * o
        z = z_ref[...].astype(F32)
        ya = y * (z * _sigmoid(z))
        ya_ref[...] = ya.astype(BF16)
        yat_ref[...] = ya.T.astype(BF16)
        _scr_put(sy, y)
        y0[...] = y.astype(BF16)
        _store_perm(y1, sy, 4)
        _store_perm(y2, sy, 16)
        s0[...] = sl[0]
        _store_perm(s1, sl, 4)
        _store_perm(s2, sl, 16)

    wide = pl.BlockSpec((ts, E), lambda i: (i, 0))
    os3 = [os_[0], os_[1].reshape(4, S // 4, E), os_[2].reshape(16, S // 16, E)]
    ls3 = [lses[0], lses[1].reshape(4, S // 4, HD), lses[2].reshape(16, S // 16, HD)]
    res = pl.pallas_call(
        body, name="combine_fwd", grid=(S // ts,),
        in_specs=_perm_specs(ts, E) + _perm_specs(ts, HD) + [wide, pl.BlockSpec((HD, E), lambda i: (0, 0))],
        out_specs=_perm_specs(ts, E) + _perm_specs(ts, HD) + [wide, pl.BlockSpec((E, ts), lambda i: (0, i))],
        out_shape=_perm_shapes(S, E, BF16) + _perm_shapes(S, HD, F32) + [SDS((S, E), BF16), SDS((E, S), BF16)],
        scratch_shapes=[_scr(ts, E), _scr(ts, E), _scr(ts, HD), _scr(ts, HD), _scr(ts, E), _scr(ts, HD)],
        compiler_params=_cp(("parallel",), 56),
    )(*os3, *ls3, z, ehot)
    ys = [res[0], res[1].reshape(S, E), res[2].reshape(S, E)]
    lse3 = [res[3], res[4].reshape(S, HD), res[5].reshape(S, HD)]
    return ys, lse3, res[6], res[7]


def mm_dya(da0, w_out, z, y):
    S = da0.shape[0]
    tm = 512

    def body(a_ref, w_ref, z_ref, y_ref, dy0, dy1, dy2, dz_ref, scr):
        for cidx in range(E // 256):
            col = slice(cidx * 256, (cidx + 1) * 256)
            dya = _dot_nt(a_ref[...], w_ref[col, :])
            zz = z_ref[:, col].astype(F32)
            sig = _sigmoid(zz)
            dy = dya * zz * sig
            scr[2 * cidx] = dy[:, :LANES]
            scr[2 * cidx + 1] = dy[:, LANES:]
            dy0[:, col] = dy.astype(BF16)
            dz_ref[:, col] = (dya * y_ref[:, col].astype(F32) * sig * (1.0 + zz * (1.0 - sig))).astype(BF16)
        _store_perm(dy1, scr, 4)
        _store_perm(dy2, scr, 16)

    wide = pl.BlockSpec((tm, E), lambda i: (i, 0))
    res = pl.pallas_call(
        body, name="mm_dya", grid=(S // tm,),
        in_specs=[pl.BlockSpec((tm, D), lambda i: (i, 0)), pl.BlockSpec((E, D), lambda i: (0, 0)), wide, wide],
        out_specs=_perm_specs(tm, E) + [wide],
        out_shape=_perm_shapes(S, E, BF16) + [SDS((S, E), BF16)],
        scratch_shapes=[_scr(tm, E)],
        compiler_params=_cp(("parallel",), 48),
    )(da0, w_out, z, y)
    return [res[0], res[1].reshape(S, E), res[2].reshape(S, E)], res[3]


def attn_bwd(P, dy, y, lse, tabs, g, d):
    S = P.shape[0]
    L = S // d
    T = min(512, L)
    nq = T // QB
    ni = L // T

    def body(qkv_ref, kh_ref, vh_ref, dy_ref, y_ref, lse_ref, c_ref, s1_ref, s2_ref,
             o_ref, dkc_ref, dvc_ref, ext):
        i = pl.program_id(1)
        col, mask, mask_first = _band_masks(i < ni - 1)
        _fill_kv(ext, qkv_ref, kh_ref, vh_ref)

        @pl.when(i == 0)
        def _():
            dkc_ref[...] = jnp.zeros_like(dkc_ref)
            dvc_ref[...] = jnp.zeros_like(dvc_ref)

        def heads(hp, carry):
            def cols(h):
                off = pl.multiple_of(h * HD, HD)
                return pl.ds(off, HD), pl.ds(E + off, HD), pl.ds(2 * E + off, HD)

            def front(h, j):
                cq, ck, _ = cols(h)
                rows = slice(j * QB, (j + 1) * QB)
                krows = slice(j * QB, (j + 2) * QB)
                dyj = dy_ref[rows, cq]
                lse_h = jnp.sum(jnp.where(col == h, lse_ref[rows, :], 0.0), axis=1, keepdims=True)
                delta = jnp.sum(dyj.astype(F32) * y_ref[rows, cq].astype(F32), axis=1, keepdims=True)
                s = _dot_nt(qkv_ref[rows, cq], ext[krows, cq])
                p = jnp.exp(s + (mask_first if j == 0 else mask) - lse_h)
                ds = (p * (_dot_nt(dyj, ext[krows, ck]) - delta)).astype(BF16)
                return ds, ds.T, p.astype(BF16).T

            def back(h, j, ds, ds_t, p_t, pend_dk, pend_dv):
                cq, ck, cv = cols(h)
                rows = slice(j * QB, (j + 1) * QB)
                krows = slice(j * QB, (j + 2) * QB)
                dq = _dot(ds, ext[krows, cq]) * SCALE
                dk2 = _dot(ds_t, qkv_ref[rows, cq])
                dv2 = _dot(p_t, dy_ref[rows, cq])
                c, s1, s2 = c_ref[rows, :], s1_ref[rows, :], s2_ref[rows, :]
                o_ref[rows, cq] = _unrope(dq, c, s1, s2).astype(BF16)
                o_ref[rows, ck] = _unrope(dk2[QB:] + pend_dk, c, s1, s2).astype(BF16)
                o_ref[rows, cv] = (dv2[QB:] + pend_dv).astype(BF16)
                return dk2[:QB], dv2[:QB]

            items = [(HEADS_PER_STEP * hp + hh, j) for hh in range(HEADS_PER_STEP) for j in reversed(range(nq))]
            queue = [front(*it) for it in items[:AHEAD]]
            pend = None
            for u, (h, j) in enumerate(items):
                if u + AHEAD < len(items):
                    queue.append(front(*items[u + AHEAD]))
                if j == nq - 1:
                    pend = (dkc_ref[:, cols(h)[0]], dvc_ref[:, cols(h)[0]])
                pend = back(h, j, *queue.pop(0), *pend)
                if j == 0:
                    dkc_ref[:, cols(h)[0]], dvc_ref[:, cols(h)[0]] = pend
            return carry

        lax.fori_loop(0, NH // HEADS_PER_STEP, heads, 0)

    blk = lambda r, i: r * ni + ni - 1 - i
    halo = lambda r, i: jnp.maximum(r * (L // QB) + (ni - 1 - i) * nq - 1, 0)
    main = pl.BlockSpec((T, SEG), lambda r, i: (blk(r, i), 0))
    wide = pl.BlockSpec((T, E), lambda r, i: (blk(r, i), 0))
    narrow = pl.BlockSpec((T, HD), lambda r, i: (blk(r, i), 0))
    return pl.pallas_call(
        body, name=f"attn_bwd{g}", grid=(d, ni),
        in_specs=[main, pl.BlockSpec((QB, E), lambda r, i: (halo(r, i), 1)),
                  pl.BlockSpec((QB, E), lambda r, i: (halo(r, i), 2)),
                  wide, wide, narrow, narrow, narrow, narrow],
        out_specs=main, out_shape=SDS((S, SEG), BF16),
        scratch_shapes=[pltpu.VMEM((QB, E), F32), pltpu.VMEM((QB, E), F32), pltpu.VMEM((T + QB, 2 * E), BF16)],
        compiler_params=_cp(("arbitrary", "arbitrary"), 56),
    )(P, P, P, dy, y, lse, *tabs)


def _pool_cnt(t0, rows):
    t = (lax.broadcasted_iota(jnp.int32, (rows, E), 0) + t0 + 1).astype(F32)
    ch = lax.broadcasted_iota(jnp.int32, (rows, E), 1)
    w = jnp.where(ch < PC, 2.0, jnp.where(ch < 2 * PC, 4.0, jnp.where(ch < 3 * PC, 8.0, 16.0)))
    return jnp.minimum(t, w)


def _by_group(parts):
    return jnp.concatenate([parts[g][:, g * PC:(g + 1) * PC] for g in range(4)], axis=1)


def pool_fwd(uz):
    S = uz.shape[0]
    ts = 256

    def body(u_ref, h_ref, o_ref, ot_ref):
        i = pl.program_id(0)
        u = u_ref[...]
        halo = jnp.where(i > 0, h_ref[...], 0.0)
        ext = jnp.concatenate([halo, u], axis=0)
        s2 = ext + pltpu.roll(ext, 1, 0)
        s4 = s2 + pltpu.roll(s2, 2, 0)
        s8 = s4 + pltpu.roll(s4, 4, 0)
        s16 = s8 + pltpu.roll(s8, 8, 0)
        win = _by_group([s2, s4, s8, s16])[16:, :]
        pooled = win / _pool_cnt(i * ts, ts) - u
        o_ref[...] = pooled.astype(BF16)
        ot_ref[...] = pooled.T.astype(BF16)

    return pl.pallas_call(
        body, name="pool_fwd", grid=(S // ts,),
        in_specs=[pl.BlockSpec((ts, E), lambda i: (i, 0)),
                  pl.BlockSpec((16, E), lambda i: (jnp.maximum(i * (ts // 16) - 1, 0), 0))],
        out_specs=[pl.BlockSpec((ts, E), lambda i: (i, 0)), pl.BlockSpec((E, ts), lambda i: (0, i))],
        out_shape=[SDS((S, E), BF16), SDS((E, S), BF16)],
        compiler_params=_cp(("parallel",), 48),
    )(uz, uz)


def pool_bwd(dpooled, duz):
    S = dpooled.shape[0]
    ts = 256
    nt = S // ts

    def body(d_ref, h_ref, alias_ref, o_ref):
        i = pl.program_id(0)
        dp = d_ref[...].astype(F32)
        halo = jnp.where(i < nt - 1, h_ref[...].astype(F32), 0.0)
        n = ts + 16
        ext = jnp.concatenate([dp, halo], axis=0) / _pool_cnt(i * ts, n)
        f2 = ext + pltpu.roll(ext, n - 1, 0)
        f4 = f2 + pltpu.roll(f2, n - 2, 0)
        f8 = f4 + pltpu.roll(f4, n - 4, 0)
        f16 = f8 + pltpu.roll(f8, n - 8, 0)
        win = _by_group([f2, f4, f8, f16])[:ts, :]
        o_ref[...] = (win - dp).astype(BF16)

    return pl.pallas_call(
        body, name="pool_bwd", grid=(nt,),
        in_specs=[pl.BlockSpec((ts, E), lambda i: (i, 0)),
                  pl.BlockSpec((16, E), lambda i: (jnp.minimum((i + 1) * (ts // 16), S // 16 - 1), 0)), ANY],
        out_specs=pl.BlockSpec((ts, E), lambda i: (i, 0)),
        out_shape=SDS(duz.shape, BF16),
        input_output_aliases={2: 0},
        compiler_params=_cp(("parallel",), 48),
    )(dpooled, dpooled, duz)


def mm_grp(pooled, wg, b, scale, uz):
    S = pooled.shape[0]
    tm = 512

    def body(p_ref, w_ref, b_ref, s_ref, z_ref, h_ref, y_ref, yt_ref):
        for g in range(4):
            cs = slice(g * PC, (g + 1) * PC)
            h = _dot(p_ref[:, cs], w_ref[g]) + b_ref[:, cs]
            z = z_ref[:, cs]
            yp = h * s_ref[:, cs] * (z * _sigmoid(z))
            h_ref[:, cs] = h.astype(BF16)
            y_ref[:, cs] = yp.astype(BF16)
            yt_ref[cs, :] = yp.T.astype(BF16)

    row = pl.BlockSpec((tm, E), lambda i: (i, 0))
    vec = pl.BlockSpec((1, E), lambda i: (0, 0))
    return pl.pallas_call(
        body, name="mm_grp", grid=(S // tm,),
        in_specs=[row, pl.BlockSpec((4, PC, PC), lambda i: (0, 0, 0)), vec, vec,
                  pl.BlockSpec((tm, E), lambda i: (i, 1))],
        out_specs=[row, row, pl.BlockSpec((E, tm), lambda i: (0, i))],
        out_shape=[SDS((S, E), BF16), SDS((S, E), BF16), SDS((E, S), BF16)],
        compiler_params=_cp(("parallel",), 48),
    )(pooled, wg, b, scale, uz)


def mm_dyp(da1, w_out, uz, h, scale):
    S = da1.shape[0]
    tm = 512

    def body(a_ref, w_ref, z_ref, h_ref, s_ref, dh_ref, dz_ref, dsc_ref, db_ref):
        @pl.when(pl.program_id(0) == 0)
        def _():
            dsc_ref[...] = jnp.zeros_like(dsc_ref)
            db_ref[...] = jnp.zeros_like(db_ref)

        for cidx in range(E // 256):
            col = slice(cidx * 256, (cidx + 1) * 256)
            dyp = _dot_nt(a_ref[...], w_ref[col, :])
            z = z_ref[:, col]
            hh = h_ref[:, col].astype(F32)
            sc = s_ref[:, col]
            sig = _sigmoid(z)
            dhs = dyp * z * sig
            dz_ref[:, col] = (dyp * hh * sc * sig * (1.0 + z * (1.0 - sig))).astype(BF16)
            dh = dhs * sc
            dh_ref[:, col] = dh.astype(BF16)
            dsc_ref[:, col] += _fold8(dhs * hh)
            db_ref[:, col] += _fold8(dh)

    row = pl.BlockSpec((tm, E), lambda i: (i, 0))
    acc = pl.BlockSpec((8, E), lambda i: (0, 0))
    return pl.pallas_call(
        body, name="mm_dyp", grid=(S // tm,),
        in_specs=[pl.BlockSpec((tm, D), lambda i: (i, 0)), pl.BlockSpec((E, D), lambda i: (0, 0)),
                  pl.BlockSpec((tm, E), lambda i: (i, 1)), row, pl.BlockSpec((1, E), lambda i: (0, 0))],
        out_specs=[row, pl.BlockSpec((tm, E), lambda i: (i, 1)), acc, acc],
        out_shape=[SDS((S, E), BF16), SDS((S, 2 * E), BF16), SDS((8, E), F32), SDS((8, E), F32)],
        compiler_params=_cp(("arbitrary",), 48),
    )(da1, w_out, uz, h, scale)


def mm_dpooled(dh, wg):
    S = dh.shape[0]
    tm = 1024

    def body(a_ref, w_ref, o_ref):
        for g in range(4):
            cs = slice(g * PC, (g + 1) * PC)
            o_ref[:, cs] = _dot_nt(a_ref[:, cs], w_ref[g]).astype(BF16)

    row = pl.BlockSpec((tm, E), lambda i: (i, 0))
    return pl.pallas_call(
        body, name="mm_dpooled", grid=(S // tm,),
        in_specs=[row, pl.BlockSpec((4, PC, PC), lambda i: (0, 0, 0))],
        out_specs=row, out_shape=SDS((S, E), BF16),
        compiler_params=_cp(("parallel",), 48),
    )(dh, wg)


def _rope_tables(positions):
    inv_freq = 500000.0 ** (-jnp.arange(0, 32, 2, dtype=F32) / 32)
    S = positions.shape[0]
    ang = jnp.repeat(positions.astype(F32).reshape(S // 8, 8), 16, axis=1) * jnp.tile(inv_freq, 8)
    cos, sin = lax.optimization_barrier((jnp.cos(ang), jnp.sin(ang)))
    cos, sin = cos.reshape(S, 16), sin.reshape(S, 16)
    one = jnp.ones((S, HD - 32), F32)
    zero16 = jnp.zeros((S, 16), F32)
    zero = jnp.zeros((S, HD - 32), F32)
    c = jnp.concatenate([cos, cos, one], axis=1)
    s1 = jnp.concatenate([-sin, zero16, zero], axis=1)
    s2 = jnp.concatenate([zero16, sin, zero], axis=1)
    return c.astype(BF16), s1.astype(BF16), s2.astype(BF16)


def kernel(x, positions, norm_pre, norm_post, attn_w_in, attn_w_out, pool_w_in, pool_w_grp, pool_b_grp, pool_scale, pool_w_out, loss_target, m_norm_pre, m_norm_post, m_attn_w_in, m_attn_w_out, m_pool_w_in, m_pool_w_grp, m_pool_b_grp, m_pool_scale, m_pool_w_out, v_norm_pre, v_norm_post, v_attn_w_in, v_attn_w_out, v_pool_w_in, v_pool_w_grp, v_pool_b_grp, v_pool_scale, v_pool_w_out):
    S = x.shape[1]
    xi, yi, ci = _mesh_pos()
    dev = 4 * xi + 2 * yi + ci
    x2 = x[0]
    tgt = loss_target[0]

    small = jnp.concatenate([pool_b_grp[0].reshape(2, HD), pool_scale[0].reshape(2, HD),
                             jnp.zeros((4, HD), F32)], axis=0)
    w_in_l = attn_w_in[0].astype(BF16)
    hop1, hop1_token = split_start("gather_w_in_start", [w_in_l], [lax.empty((N_DEV,) + w_in_l.shape, BF16)],
                                   _hop1_plan(), 3)

    pos = positions[0]
    tabs = [_rope_tables(pos.reshape(S // d, d).T.reshape(S)) for d in DIL]
    ehot = (jnp.arange(E)[None, :] // HD == jnp.arange(HD)[:, None]).astype(BF16)
    seg_tiles = SEG // CT

    xn0, xn0_4, xn0_16, xn0t = norm_pre0(x2, norm_pre[0:1], hop1_token)
    xn0s = [xn0, xn0_4.reshape(S, D), xn0_16.reshape(S, D)]
    xn0ts = [xn0t, transpose_rows(xn0s[1], "xn0t_4"), transpose_rows(xn0s[2], "xn0t_16")]

    (w_in_l,), (w_in8,) = split_wait("gather_w_in_wait", hop1, _hop1_plan(), xn0ts[2])
    hop2, hop2_token = split_start("gather_w_in_hop2_start", [w_in8], None, _hop2_plan(D), 4)
    _, (w_in8,) = split_wait("gather_w_in_hop2_wait", hop2, _hop2_plan(D), hop2_token, inplace=True)
    hop3, hop3_token = split_start("gather_w_in_hop3_start", [w_in8], None, _hop3_plan(), 1)
    _, (w_in8,) = split_wait("gather_w_in_hop3_wait", hop3, _hop3_plan(), hop3_token, inplace=True)
    w_in8 = lax.dynamic_update_slice(w_in8, w_in_l[None], (dev, 0, 0))
    small, w_in8 = lax.optimization_barrier((small, w_in8))
    rest_l = [attn_w_out[0].astype(BF16), pool_w_in[0].astype(BF16), pool_w_grp[0].astype(BF16),
              pool_w_out[0].astype(BF16), small]
    rest_flight, rest_token = split_start(
        "gather_rest_start", rest_l, [lax.empty((N_DEV,) + a.shape, a.dtype) for a in rest_l], _peers_plan(), 7)
    Ps, os_, lses = [], [], []
    for g, d in enumerate(DIL):
        P = mm_in(xn0s[g], w_in8, g * seg_tiles, seg_tiles, tabs[g], f"mm_qkv{g}", after=rest_token)
        o, l = attn_fwd(P, g, d)
        Ps.append(P)
        os_.append(o)
        lses.append(l)
    z0 = mm_in(xn0, w_in8, 3 * seg_tiles, E // CT, None, "mm_z0")
    ys, lse3, ya, yat = combine_fwd(os_, lses, z0, ehot)

    rest_l, rest8 = split_wait("gather_rest_wait", rest_flight, _peers_plan(), ya)
    rest8 = [lax.dynamic_update_slice(r8, a[None], (dev,) + (0,) * a.ndim) for r8, a in zip(rest8, rest_l)]
    w_out8, wp_in8, wg8, wp_out8, small8 = rest8
    w_out = w_out8.reshape(E, D)
    wp_out = wp_out8.reshape(E, D)
    wp_in = wp_in8.transpose(1, 0, 2).reshape(D, 2 * E)
    wg = wg8.transpose(1, 0, 2, 3).reshape(4, PC, PC)
    b_full = small8[:, 0:2, :].reshape(N_DEV, 4, PC // N_DEV).transpose(1, 0, 2).reshape(1, E)
    scale_full = small8[:, 2:4, :].reshape(1, E)
    a0 = mm_rows(ya, w_out, "mm_out0", F32)
    h1, xn1, xn1t = post0_pre1(x2, a0, norm_post[0:1], norm_pre[1:2])

    uz = mm_rows(xn1, wp_in, "mm_uz", F32, tm=512)
    pooled, pooled_t = pool_fwd(uz)
    hgrp, yp, ypt = mm_grp(pooled, wg, b_full, scale_full, uz)
    a1 = mm_rows(yp, wp_out, "mm_out1", F32)
    dh2, da1, loss_rows, dg_post1 = post1_loss(h1, a1, tgt, norm_post[1:2])
    loss = lax.psum(0.5 / D * jnp.sum(loss_rows), ("x", "y", "c"))

    dh, duz, dscale_p, db_p = mm_dyp(da1, wp_out, uz, hgrp, scale_full)
    dpooled = mm_dpooled(dh, wg)
    duz = pool_bwd(dpooled, duz)
    g_wg = mm_dwg(pooled_t, dh)
    g_wp_out = mm_wgrad_rows(ypt, da1, "mm_dwp_out")
    g_wp_in = mm_wgrad_cols(xn1t, duz, "mm_dwp_in", shard=PC)
    dxn1 = mm_dx_full(duz, wp_in, "mm_dxn1")
    dh1, da0, dg_pre1, dg_post0 = mid_bwd(dxn1, dh2, h1, a0, norm_pre[1:2], norm_post[0:1])

    dys, dz0 = mm_dya(da0, w_out, z0, ys[0])
    g_w_out = mm_wgrad_rows(yat, da0, "mm_dw_out")
    g_w_in = mm_dw_in_part(xn0t, dz0, 3 * seg_tiles, None, "mm_dw_in_z")
    dPs = []
    for g, d in enumerate(DIL):
        dP = attn_bwd(Ps[g], dys[g], ys[g], lse3[g], tabs[g], g, d)
        g_w_in = mm_dw_in_part(xn0ts[g], dP, g * seg_tiles, g_w_in, f"mm_dw_in{g}")
        dPs.append(dP)

    cidx = ci.astype(jnp.int32).reshape(1)
    chip = (2 * xi + yi).astype(jnp.int32).reshape(1)
    fulls = [g_w_in, g_w_out.reshape(N_DEV, E // N_DEV, D), g_wp_in,
             g_wg.reshape(4, N_DEV, PC // N_DEV, PC).transpose(1, 0, 2, 3).reshape(N_DEV, 4 * PC // N_DEV, PC),
             g_wp_out.reshape(N_DEV, E // N_DEV, D)]
    pair_flight, pair_token = split_start(
        "rs_pair_start", fulls, [lax.empty((4,) + f.shape[1:], F32) for f in fulls], _pair_plan(), 4)
    dx_z = mm_dx_part(dz0, w_in8, 3 * seg_tiles, "mm_dxn0_z", after=pair_token)
    dx_0 = mm_dx_part(dPs[0], w_in8, 0, "mm_dxn0_0", after=dx_z)
    fulls, sibs = split_wait("rs_pair_wait", pair_flight, _pair_plan(), dx_0)
    parts = [pair_add(f, s, cidx, f"pair_add{k}") for k, (f, s) in enumerate(zip(fulls, sibs))]
    chips_flight, chips_token = split_start(
        "rs_chips_start", parts, [jnp.zeros(p.shape, BF16) for p in parts], _chips_plan(), 3)
    dx_1 = mm_dx_part(dPs[1], w_in8, seg_tiles, "mm_dxn0_1", after=chips_token)
    dx_2 = mm_dx_part(dPs[2], w_in8, 2 * seg_tiles, "mm_dxn0_2", after=dx_1)
    grad_x, dg_pre0 = pre0_bwd(dx_0, dx_z, dx_1, dx_2, dh1, x2, norm_pre[0:1])
    parts, recvs = split_wait("rs_chips_wait", chips_flight, _chips_plan(), grad_x)
    shards = [(attn_w_in, m_attn_w_in, v_attn_w_in), (attn_w_out, m_attn_w_out, v_attn_w_out),
              (pool_w_in, m_pool_w_in, v_pool_w_in), (pool_w_grp, m_pool_w_grp, v_pool_w_grp),
              (pool_w_out, m_pool_w_out, v_pool_w_out)]
    big = []
    for k, (recv, part, (w, m, v)) in enumerate(zip(recvs, parts, shards)):
        shp = w.shape
        r2 = recv.shape[1:]
        res = adamw_sum(recv, part, chip, w.reshape(r2), m.reshape(r2), v.reshape(r2), f"adamw{k}")
        big.append([t.reshape(shp) for t in res])

    smalls = jnp.concatenate([dg_pre0.sum(0, keepdims=True), dg_pre1.sum(0, keepdims=True),
                              dg_post0.sum(0, keepdims=True), dg_post1.sum(0, keepdims=True),
                              db_p.sum(0).reshape(2, D), dscale_p.sum(0).reshape(2, D)], axis=0)
    (smalls8,) = all_gather([smalls], "gather_small_grads")
    tot = sum_slots(smalls8, "sum_small_grads")
    g_norm_pre, g_norm_post = tot[0:2], tot[2:4]
    g_b = lax.dynamic_slice_in_dim(tot[4:6].reshape(4, PC), dev * (PC // N_DEV), PC // N_DEV, axis=1)[None]
    g_scale = lax.dynamic_slice_in_dim(tot[6:8].reshape(1, E), dev * (E // N_DEV), E // N_DEV, axis=1)
    sm = [adamw_small(g_norm_pre, norm_pre, m_norm_pre, v_norm_pre, "adamw_norm_pre"),
          adamw_small(g_norm_post, norm_post, m_norm_post, v_norm_post, "adamw_norm_post"),
          adamw_small(g_b, pool_b_grp, m_pool_b_grp, v_pool_b_grp, "adamw_b"),
          adamw_small(g_scale, pool_scale, m_pool_scale, v_pool_scale, "adamw_scale")]

    grads = [g_norm_pre, g_norm_post, big[0][0], big[1][0], big[2][0], big[3][0], g_b, g_scale, big[4][0]]

    def pick(k):
        return [sm[0][k - 1], sm[1][k - 1], big[0][k], big[1][k], big[2][k], big[3][k], sm[2][k - 1], sm[3][k - 1],
                big[4][k]]

    return (loss, grad_x[None], *grads, *pick(1), *pick(2), *pick(3))
```

```python
import math

import jax
import jax.numpy as jnp
from jax import lax
from jax.experimental import pallas as pl
from jax.experimental.pallas import tpu as pltpu

F32 = jnp.float32
BF16 = jnp.bfloat16
SDS = jax.ShapeDtypeStruct

N_DEV = 8
D = 1024
E = 2048
HD = 128
NH = E // HD
DIL = (1, 4, 16)
QB = 128
SEG = 3 * E
W_IN_COLS = 3 * SEG + E
W_SHARD = W_IN_COLS // N_DEV
CT = 512
PC = E // 4
EPS = 1e-6
NEG = -1e30
SCALE = 1.0 / math.sqrt(HD)
LR, B1, B2, ADAM_EPS, WD, STEP = 0.001, 0.9, 0.999, 1e-08, 0.01, 10
MIB = 1024 * 1024
ANY = pl.BlockSpec(memory_space=pl.ANY)
MESH = pl.DeviceIdType.MESH


def _cp(sem, mb):
    return pltpu.CompilerParams(dimension_semantics=sem, vmem_limit_bytes=mb * MIB)


def _dot(a, b):
    return jnp.dot(a, b, preferred_element_type=F32)


def _dot_nt(a, b):
    return lax.dot_general(a, b, (((1,), (1,)), ((), ())), preferred_element_type=F32)


def _rms(h):
    return lax.rsqrt(jnp.mean(h * h, axis=-1, keepdims=True) + EPS)


def _row_tile(R, C, budget):
    tr = R
    while tr * C * 4 > budget and tr % 16 == 0:
        tr //= 2
    return tr


def _fold8(t):
    return t.reshape(t.shape[0] // 8, 8, t.shape[1]).sum(axis=0)


def _sigmoid(z):
    return pl.reciprocal(1.0 + jnp.exp(-z), approx=True)


LANES = 128


def _scr(rows, C):
    return pltpu.VMEM((C // LANES, rows, LANES), F32)


def _scr_put(scr, val):
    for c in range(scr.shape[0]):
        scr[c] = val[:, c * LANES:(c + 1) * LANES]


def _scr_get(scr):
    return jnp.concatenate([scr[c] for c in range(scr.shape[0])], axis=1)


def _store_perm(dst_ref, scr, d):
    n = dst_ref.shape[1]
    for r in range(d):
        for c in range(scr.shape[0]):
            dst_ref[r, :, c * LANES:(c + 1) * LANES] = scr[c, pl.ds(r, n, stride=d), :].astype(dst_ref.dtype)


def _load_perm(scr, src_ref, d, add=False):
    n = src_ref.shape[1]
    for r in range(d):
        rows = pl.ds(r, n, stride=d)
        for c in range(scr.shape[0]):
            v = src_ref[r, :, c * LANES:(c + 1) * LANES].astype(F32)
            scr[c, rows, :] = scr[c, rows, :] + v if add else v


def _rope(t, c, s1, s2):
    t = t.astype(BF16)
    return t * c + pltpu.roll(t, HD - 16, 1) * s1 + pltpu.roll(t, 16, 1) * s2


def _unrope(t, c, s1, s2):
    t = t.astype(BF16)
    return t * c - pltpu.roll(t, HD - 16, 1) * s1 - pltpu.roll(t, 16, 1) * s2


def _mesh_pos():
    return lax.axis_index("x"), lax.axis_index("y"), lax.axis_index("c")


def all_gather(arrs, name):
    n = len(arrs)

    def body(*refs):
        ins, outs = refs[:n], refs[n:2 * n]
        send_sems, recv_sems, local_sems = refs[2 * n:]
        x, y, c = _mesh_pos()
        me, sib = (x, y, c), (x, y, 1 - c)
        chips = [(1 - x, y), (x, 1 - y), (1 - x, 1 - y)]

        def slot(p):
            return 4 * p[0] + 2 * p[1] + p[2]

        def copy(a, k, block, to, src=None):
            dst = outs[a].at[slot(block)]
            return pltpu.make_async_remote_copy(
                src_ref=dst if src is None else src, dst_ref=dst,
                send_sem=send_sems.at[a, k], recv_sem=recv_sems.at[a, k],
                device_id=to, device_id_type=MESH)

        mine = [pltpu.make_async_copy(ins[a], outs[a].at[slot(me)], local_sems.at[a]) for a in range(n)]
        for cp in mine:
            cp.start()
        first = []
        for a in range(n):
            first.append(copy(a, 0, me, sib, src=ins[a]))
            for j, chip in enumerate(chips):
                first.append(copy(a, 1 + j, me, (*chip, c), src=ins[a]))
        for cp in first:
            cp.start()
        passed = []
        for j, chip in enumerate(chips):
            for a in range(n):
                copy(a, 1 + j, (*chip, c), me).wait_recv()
                fw = copy(a, 4 + j, (*chip, c), sib)
                fw.start()
                passed.append(fw)
        for a in range(n):
            copy(a, 0, sib, me).wait_recv()
        for j, chip in enumerate(chips):
            for a in range(n):
                copy(a, 4 + j, (*chip, 1 - c), me).wait_recv()
        for cp in first + passed:
            cp.wait_send()
        for cp in mine:
            cp.wait()

    return pl.pallas_call(
        body, name=name,
        out_shape=[SDS((N_DEV,) + a.shape, a.dtype) for a in arrs],
        in_specs=[ANY] * n, out_specs=[ANY] * n,
        scratch_shapes=[pltpu.SemaphoreType.DMA((n, 7)), pltpu.SemaphoreType.DMA((n, 7)),
                        pltpu.SemaphoreType.DMA((n,))],
    )(*arrs)


HBM_SPEC = pl.BlockSpec(memory_space=pltpu.HBM)
SEM_SPEC = pl.BlockSpec(memory_space=pltpu.SEMAPHORE)
EFFECT = pltpu.SideEffectType.DATAFLOW_SIDE_EFFECTING


def _pair_plan():
    def plan(x, y, c):
        return [(2 * q + (1 - c), q, (x, y, 1 - c)) for q in range(4)]
    return plan


def _chips_plan():
    def plan(x, y, c):
        chips = [(1 - x, y), (x, 1 - y), (1 - x, 1 - y)]
        return [(2 * cx + cy, 2 * x + y, (cx, cy, c)) for cx, cy in chips]
    return plan


def _hop1_plan():
    def plan(x, y, c):
        me = 4 * x + 2 * y + c
        return [(None, me, (x, y, 1 - c)), (None, me, (1 - x, y, c)), (None, me, (x, 1 - y, c))]
    return plan


def _hop2_plan(rows):
    half = rows // 2

    def plan(x, y, c):
        sx, sy = 4 * (1 - x) + 2 * y + c, 4 * x + 2 * (1 - y) + c
        top, bottom = pl.ds(0, half), pl.ds(half, half)
        return [((sx, top), (sx, top), (x, 1 - y, c)), ((sy, bottom), (sy, bottom), (1 - x, y, c)),
                (sx, sx, (x, y, 1 - c)), (sy, sy, (x, y, 1 - c))]
    return plan


def _hop3_plan():
    def plan(x, y, c):
        sd = 4 * (1 - x) + 2 * (1 - y) + c
        return [(sd, sd, (x, y, 1 - c))]
    return plan


def _peers_plan():
    def plan(x, y, c):
        out = []
        for k in range(1, N_DEV):
            fx, fy, fc = (k >> 2) & 1, (k >> 1) & 1, k & 1
            px, py, pc = (x + fx) % 2, (y + fy) % 2, (c + fc) % 2
            out.append((None, 4 * x + 2 * y + c, (px, py, pc)))
        return out
    return plan


def _split_copies(plan, srcs, lands, send_sems, recv_sems):
    x, y, c = _mesh_pos()
    cps = []
    for a, (src, land) in enumerate(zip(srcs, lands)):
        steps = plan(x, y, c)
        for k, (si, li, to) in enumerate(steps):
            sem = a * len(steps) + k
            cps.append(pltpu.make_async_remote_copy(
                src_ref=src if si is None else src.at[si], dst_ref=land.at[li],
                send_sem=send_sems.at[sem], recv_sem=recv_sems.at[sem],
                device_id=to, device_id_type=MESH))
    return cps


def split_start(name, srcs, lands, plan, nk):
    n = len(srcs)
    ops = list(srcs) + ([] if lands is None else list(lands))
    nb = len(ops)

    def body(*refs):
        token = refs[-1]
        for cp in _split_copies(plan, refs[:n], refs[nb - n:nb], refs[nb], refs[nb + 1]):
            cp.start()
        token[...] = jnp.zeros_like(token)

    ops = [pltpu.with_memory_space_constraint(a, pltpu.HBM) for a in ops]
    res = pl.pallas_call(
        body, name=name,
        out_shape=(pltpu.SemaphoreType.DMA((n * nk,)), pltpu.SemaphoreType.DMA((n * nk,)),
                   *[pltpu.HBM(a.shape, a.dtype) for a in ops], SDS((8, 128), F32)),
        in_specs=[HBM_SPEC] * nb,
        out_specs=(SEM_SPEC, SEM_SPEC, *[HBM_SPEC] * nb, pl.BlockSpec(memory_space=pltpu.VMEM)),
        input_output_aliases={i: 2 + i for i in range(nb)},
        compiler_params=pltpu.CompilerParams(has_side_effects=EFFECT),
    )(*ops)
    return res[:-1], res[-1]


def split_wait(name, flight, plan, after, inplace=False):
    send_sems, recv_sems = flight[0], flight[1]
    bufs = list(flight[2:])
    nb = len(bufs)
    n = nb if inplace else nb // 2

    def body(*refs):
        for cp in _split_copies(plan, refs[:n], refs[nb - n:nb], refs[nb], refs[nb + 1]):
            cp.wait_send()
            cp.wait_recv()

    res = pl.pallas_call(
        body, name=name,
        out_shape=[pltpu.HBM(a.shape, a.dtype) for a in bufs],
        in_specs=[HBM_SPEC] * nb + [SEM_SPEC, SEM_SPEC, ANY],
        out_specs=[HBM_SPEC] * nb,
        input_output_aliases={i: i for i in range(nb)},
        compiler_params=pltpu.CompilerParams(has_side_effects=EFFECT),
    )(*bufs, send_sems, recv_sems, after)
    return res[:n], res[nb - n:]


def pair_add(full, sib, cidx, name):
    _, R, C = full.shape
    tr = _row_tile(R, C, MIB)

    def body(c_ref, a_ref, b_ref, o_ref):
        o_ref[...] = (a_ref[...] + b_ref[...]).astype(BF16)

    return pl.pallas_call(
        body, name=name,
        grid_spec=pltpu.PrefetchScalarGridSpec(
            num_scalar_prefetch=1, grid=(4, R // tr),
            in_specs=[pl.BlockSpec((None, tr, C), lambda q, i, cr: (2 * q + cr[0], i, 0)),
                      pl.BlockSpec((None, tr, C), lambda q, i, cr: (q, i, 0))],
            out_specs=pl.BlockSpec((None, tr, C), lambda q, i, cr: (q, i, 0))),
        out_shape=SDS((4, R, C), BF16),
        compiler_params=_cp(("parallel", "parallel"), 32),
    )(cidx, full, sib)


def _adam_math(w, g, m, v):
    m2 = B1 * m + (1.0 - B1) * g
    v2 = B2 * v + (1.0 - B2) * (g * g)
    m_hat = m2 / (1.0 - B1 ** STEP)
    v_hat = v2 / (1.0 - B2 ** STEP)
    delta = -LR * (m_hat / (jnp.sqrt(v_hat) + ADAM_EPS) + WD * w)
    return delta, m2, v2


def adamw_sum(recv, part, chip, w, m, v, name):
    K, R, C = recv.shape
    tr = _row_tile(R, C, MIB)

    def body(chip_ref, r_ref, p_ref, w_ref, m_ref, v_ref, g_ref, d_ref, m2_ref, v2_ref):
        g = r_ref[0].astype(F32)
        for k in range(1, K):
            g = g + r_ref[k].astype(F32)
        g = g + p_ref[...].astype(F32)
        delta, m2, v2 = _adam_math(w_ref[...], g, m_ref[...], v_ref[...])
        g_ref[...] = g
        d_ref[...] = delta
        m2_ref[...] = m2
        v2_ref[...] = v2

    tile = pl.BlockSpec((tr, C), lambda i, cr: (i, 0))
    return pl.pallas_call(
        body, name=name,
        grid_spec=pltpu.PrefetchScalarGridSpec(
            num_scalar_prefetch=1, grid=(R // tr,),
            in_specs=[pl.BlockSpec((K, tr, C), lambda i, cr: (0, i, 0)),
                      pl.BlockSpec((None, tr, C), lambda i, cr: (cr[0], i, 0)), tile, tile, tile],
            out_specs=[tile] * 4),
        out_shape=[SDS((R, C), F32)] * 4,
        compiler_params=_cp(("parallel",), 32),
    )(chip, recv, part, w, m, v)


def adamw_small(g, w, m, v, name):
    def body(g_ref, w_ref, m_ref, v_ref, d_ref, m2_ref, v2_ref):
        delta, m2, v2 = _adam_math(w_ref[...], g_ref[...], m_ref[...], v_ref[...])
        d_ref[...] = delta
        m2_ref[...] = m2
        v2_ref[...] = v2

    return pl.pallas_call(body, name=name, out_shape=[SDS(w.shape, F32)] * 3)(g, w, m, v)


def sum_slots(a, name):
    K = a.shape[0]

    def body(a_ref, o_ref):
        t = a_ref[0]
        for k in range(1, K):
            t = t + a_ref[k]
        o_ref[...] = t

    return pl.pallas_call(body, name=name, out_shape=SDS(a.shape[1:], F32))(a)


def norm_pre0(x, g, after):
    S = x.shape[0]
    ts = 512

    def body(x_ref, g_ref, after_ref, o_ref, o4_ref, o16_ref, ot_ref, scr):
        h = x_ref[...]
        xn = h * _rms(h) * g_ref[...]
        o_ref[...] = xn.astype(BF16)
        ot_ref[...] = xn.T.astype(BF16)
        _scr_put(scr, xn)
        _store_perm(o4_ref, scr, 4)
        _store_perm(o16_ref, scr, 16)

    return pl.pallas_call(
        body, name="norm_pre0", grid=(S // ts,),
        in_specs=[pl.BlockSpec((ts, D), lambda i: (i, 0)), pl.BlockSpec((1, D), lambda i: (0, 0)), ANY],
        out_specs=[pl.BlockSpec((ts, D), lambda i: (i, 0)),
                   pl.BlockSpec((4, ts // 4, D), lambda i: (0, i, 0)),
                   pl.BlockSpec((16, ts // 16, D), lambda i: (0, i, 0)),
                   pl.BlockSpec((D, ts), lambda i: (0, i))],
        out_shape=[SDS((S, D), BF16), SDS((4, S // 4, D), BF16), SDS((16, S // 16, D), BF16), SDS((D, S), BF16)],
        scratch_shapes=[_scr(ts, D)],
        compiler_params=_cp(("parallel",), 32),
    )(x, g, after)


def transpose_rows(a, name):
    S, C = a.shape
    ts = 512

    def body(a_ref, o_ref):
        o_ref[...] = a_ref[...].astype(F32).T.astype(BF16)

    return pl.pallas_call(
        body, name=name, grid=(S // ts,),
        in_specs=[pl.BlockSpec((ts, C), lambda i: (i, 0))],
        out_specs=pl.BlockSpec((C, ts), lambda i: (0, i)),
        out_shape=SDS((C, S), BF16),
        compiler_params=_cp(("parallel",), 32),
    )(a)


def post0_pre1(x, a0, g_post, g_pre):
    S = x.shape[0]
    ts = 512

    def body(x_ref, a_ref, gp_ref, gn_ref, h_ref, o_ref, ot_ref):
        a = a_ref[...]
        h1 = x_ref[...] + a * _rms(a) * gp_ref[...]
        h_ref[...] = h1
        xn = h1 * _rms(h1) * gn_ref[...]
        o_ref[...] = xn.astype(BF16)
        ot_ref[...] = xn.T.astype(BF16)

    row = pl.BlockSpec((ts, D), lambda i: (i, 0))
    vec = pl.BlockSpec((1, D), lambda i: (0, 0))
    return pl.pallas_call(
        body, name="post0_pre1", grid=(S // ts,),
        in_specs=[row, row, vec, vec],
        out_specs=[row, row, pl.BlockSpec((D, ts), lambda i: (0, i))],
        out_shape=[SDS((S, D), F32), SDS((S, D), BF16), SDS((D, S), BF16)],
        compiler_params=_cp(("parallel",), 40),
    )(x, a0, g_post, g_pre)


def post1_loss(h1, a1, target, g_post):
    S = h1.shape[0]
    ts = 512

    def body(h_ref, a_ref, t_ref, g_ref, dh_ref, da_ref, loss_ref, dg_ref):
        @pl.when(pl.program_id(0) == 0)
        def _():
            loss_ref[...] = jnp.zeros_like(loss_ref)
            dg_ref[...] = jnp.zeros_like(dg_ref)

        a = a_ref[...]
        g = g_ref[...]
        rp = _rms(a)
        yhat = a * rp
        e = h_ref[...] + yhat * g - t_ref[...]
        loss_ref[...] += _fold8(e * e)
        dh = e * (1.0 / D)
        dh_ref[...] = dh
        dg_ref[...] += _fold8(dh * yhat)
        dyh = dh * g
        da = rp * (dyh - yhat * jnp.mean(dyh * yhat, axis=-1, keepdims=True))
        da_ref[...] = da.astype(BF16)

    row = pl.BlockSpec((ts, D), lambda i: (i, 0))
    acc = pl.BlockSpec((8, D), lambda i: (0, 0))
    return pl.pallas_call(
        body, name="post1_loss", grid=(S // ts,),
        in_specs=[row, row, row, pl.BlockSpec((1, D), lambda i: (0, 0))],
        out_specs=[row, row, acc, acc],
        out_shape=[SDS((S, D), F32), SDS((S, D), BF16), SDS((8, D), F32), SDS((8, D), F32)],
        compiler_params=_cp(("arbitrary",), 40),
    )(h1, a1, target, g_post)


def mid_bwd(dxn1, dh2, h1, a0, g_pre1, g_post0):
    S = h1.shape[0]
    ts = 512

    def body(dx_ref, dh2_ref, h_ref, a_ref, gn_ref, gp_ref, dh1_ref, da_ref, dgn_ref, dgp_ref):
        @pl.when(pl.program_id(0) == 0)
        def _():
            dgn_ref[...] = jnp.zeros_like(dgn_ref)
            dgp_ref[...] = jnp.zeros_like(dgp_ref)

        h = h_ref[...]
        r1 = _rms(h)
        xhat = h * r1
        dxn = dx_ref[...]
        dgn_ref[...] += _fold8(dxn * xhat)
        dxh = dxn * gn_ref[...]
        dh1 = dh2_ref[...] + r1 * (dxh - xhat * jnp.mean(dxh * xhat, axis=-1, keepdims=True))
        dh1_ref[...] = dh1
        a = a_ref[...]
        rp = _rms(a)
        yhat = a * rp
        dgp_ref[...] += _fold8(dh1 * yhat)
        dyh = dh1 * gp_ref[...]
        da = rp * (dyh - yhat * jnp.mean(dyh * yhat, axis=-1, keepdims=True))
        da_ref[...] = da.astype(BF16)

    row = pl.BlockSpec((ts, D), lambda i: (i, 0))
    vec = pl.BlockSpec((1, D), lambda i: (0, 0))
    acc = pl.BlockSpec((8, D), lambda i: (0, 0))
    return pl.pallas_call(
        body, name="mid_bwd", grid=(S // ts,),
        in_specs=[row, row, row, row, vec, vec],
        out_specs=[row, row, acc, acc],
        out_shape=[SDS((S, D), F32), SDS((S, D), BF16), SDS((8, D), F32), SDS((8, D), F32)],
        compiler_params=_cp(("arbitrary",), 48),
    )(dxn1, dh2, h1, a0, g_pre1, g_post0)


def pre0_bwd(dx_tok, dx_z, dx4, dx16, dh1, x, g_pre0):
    S = x.shape[0]
    ts = 512

    def body(da_ref, dz_ref, d4_ref, d16_ref, dh_ref, x_ref, g_ref, gx_ref, dg_ref, scr):
        @pl.when(pl.program_id(0) == 0)
        def _():
            dg_ref[...] = jnp.zeros_like(dg_ref)

        _scr_put(scr, da_ref[...] + dz_ref[...])
        _load_perm(scr, d4_ref, 4, add=True)
        _load_perm(scr, d16_ref, 16, add=True)
        h = x_ref[...]
        r = _rms(h)
        xhat = h * r
        dxn = _scr_get(scr)
        dg_ref[...] += _fold8(dxn * xhat)
        dxh = dxn * g_ref[...]
        gx_ref[...] = dh_ref[...] + r * (dxh - xhat * jnp.mean(dxh * xhat, axis=-1, keepdims=True))

    row = pl.BlockSpec((ts, D), lambda i: (i, 0))
    return pl.pallas_call(
        body, name="pre0_bwd", grid=(S // ts,),
        in_specs=[row, row, pl.BlockSpec((4, ts // 4, D), lambda i: (0, i, 0)),
                  pl.BlockSpec((16, ts // 16, D), lambda i: (0, i, 0)), row, row,
                  pl.BlockSpec((1, D), lambda i: (0, 0))],
        out_specs=[row, pl.BlockSpec((8, D), lambda i: (0, 0))],
        out_shape=[SDS((S, D), F32), SDS((8, D), F32)],
        scratch_shapes=[_scr(ts, D)],
        compiler_params=_cp(("arbitrary",), 48),
    )(dx_tok, dx_z, dx4.reshape(4, S // 4, D), dx16.reshape(16, S // 16, D), dh1, x, g_pre0)


def _w_tile(tile0):
    per = W_SHARD // CT
    return lambda t: ((tile0 + t) // per, 0, (tile0 + t) % per)


def mm_in(xn, w8, tile0, ntiles, tabs, name, after=None):
    S = xn.shape[0]
    tm = 2048
    wt = _w_tile(tile0)

    def body(a_ref, b_ref, *rest):
        o_ref = rest[-1]
        rc = 512
        for u in range(tm // rc):
            rows = slice(u * rc, (u + 1) * rc)
            r = _dot(a_ref[rows, :], b_ref[...])
            if tabs is None:
                o_ref[rows, :] = r.astype(BF16)
                continue
            c_ref, s1_ref, s2_ref = rest[:3]
            rot = pl.program_id(1) < 2 * E // CT
            qs = jnp.where(pl.program_id(1) < E // CT, SCALE, 1.0)
            c = jnp.where(rot, (c_ref[rows, :] * qs).astype(BF16), 1.0)
            s1 = jnp.where(rot, (s1_ref[rows, :] * qs).astype(BF16), 0.0)
            s2 = jnp.where(rot, (s2_ref[rows, :] * qs).astype(BF16), 0.0)
            for hh in range(CT // HD):
                cs = slice(hh * HD, (hh + 1) * HD)
                o_ref[rows, cs] = _rope(r[:, cs], c, s1, s2).astype(BF16)

    tab = pl.BlockSpec((tm, HD), lambda i, t: (i, 0))
    return pl.pallas_call(
        body, name=name, grid=(S // tm, ntiles),
        in_specs=[pl.BlockSpec((tm, D), lambda i, t: (i, 0)),
                  pl.BlockSpec((None, D, CT), lambda i, t: wt(t))] + ([] if tabs is None else [tab] * 3)
        + ([] if after is None else [ANY]),
        out_specs=pl.BlockSpec((tm, CT), lambda i, t: (i, t)),
        out_shape=SDS((S, ntiles * CT), BF16),
        compiler_params=_cp(("parallel", "parallel"), 48),
    )(xn, w8, *(() if tabs is None else tabs), *(() if after is None else (after,)))


def mm_rows(a, b, name, out_dtype, tm=1024):
    M, K = a.shape
    N = b.shape[1]

    def body(a_ref, b_ref, o_ref):
        for cidx in range(N // 256):
            col = slice(cidx * 256, (cidx + 1) * 256)
            o_ref[:, col] = _dot(a_ref[...], b_ref[:, col]).astype(out_dtype)

    return pl.pallas_call(
        body, name=name, grid=(M // tm,),
        in_specs=[pl.BlockSpec((tm, K), lambda i: (i, 0)), pl.BlockSpec((K, N), lambda i: (0, 0))],
        out_specs=pl.BlockSpec((tm, N), lambda i: (i, 0)),
        out_shape=SDS((M, N), out_dtype),
        compiler_params=_cp(("parallel",), 48),
    )(a, b)


def mm_acc(a, b, name, *, grid, a_spec, b_spec, o_spec, o_shape, acc_shape, write, vmem=48):
    nk = grid[-1]

    def body(a_ref, b_ref, o_ref, acc_ref):
        k = pl.program_id(len(grid) - 1)

        @pl.when(k == 0)
        def _():
            acc_ref[...] = jnp.zeros_like(acc_ref)

        acc_ref[...] += _dot(a_ref[...], b_ref[...])

        @pl.when(k == nk - 1)
        def _():
            write(o_ref, acc_ref)

    return pl.pallas_call(
        body, name=name, grid=grid, in_specs=[a_spec, b_spec], out_specs=o_spec, out_shape=o_shape,
        scratch_shapes=[pltpu.VMEM(acc_shape, F32)],
        compiler_params=_cp(("parallel",) * (len(grid) - 1) + ("arbitrary",), vmem),
    )(a, b)


def _write_plain(o_ref, acc_ref):
    o_ref[...] = acc_ref[...]


def mm_wgrad_rows(at, b, name):
    M, S = at.shape
    N = b.shape[1]
    tm, tk = 1024, 1024
    return mm_acc(at, b, name, grid=(M // tm, S // tk),
                  a_spec=pl.BlockSpec((tm, tk), lambda i, k: (i, k)),
                  b_spec=pl.BlockSpec((tk, N), lambda i, k: (k, 0)),
                  o_spec=pl.BlockSpec((tm, N), lambda i, k: (i, 0)),
                  o_shape=SDS((M, N), F32), acc_shape=(tm, N), write=_write_plain)


def mm_wgrad_cols(at, b, name, *, shard):
    M, S = at.shape
    tk = 1024
    nb = 2
    tn = nb * shard

    def write(o_ref, acc_ref):
        for u in range(nb):
            o_ref[u] = acc_ref[:, u * shard:(u + 1) * shard]

    return mm_acc(at, b, name, grid=(N_DEV // nb, S // tk),
                  a_spec=pl.BlockSpec((M, tk), lambda t, k: (0, k)),
                  b_spec=pl.BlockSpec((tk, tn), lambda t, k: (k, t)),
                  o_spec=pl.BlockSpec((nb, M, shard), lambda t, k: (t, 0, 0)),
                  o_shape=SDS((N_DEV, M, shard), F32), acc_shape=(M, tn), write=write)


def mm_dwg(pooled_t, dh):
    S = dh.shape[0]
    tk = 2048
    return mm_acc(pooled_t, dh, "mm_dwg", grid=(4, S // tk),
                  a_spec=pl.BlockSpec((PC, tk), lambda g, k: (g, k)),
                  b_spec=pl.BlockSpec((tk, PC), lambda g, k: (k, g)),
                  o_spec=pl.BlockSpec((None, PC, PC), lambda g, k: (g, 0, 0)),
                  o_shape=SDS((4, PC, PC), F32), acc_shape=(PC, PC), write=_write_plain)


def mm_dx_full(da, w, name):
    S, K = da.shape
    N = w.shape[0]
    tm = 512

    def body(a_ref, b_ref, o_ref):
        o_ref[...] = _dot_nt(a_ref[...], b_ref[...])

    return pl.pallas_call(
        body, name=name, grid=(S // tm,),
        in_specs=[pl.BlockSpec((tm, K), lambda i: (i, 0)), pl.BlockSpec((N, K), lambda i: (0, 0))],
        out_specs=pl.BlockSpec((tm, N), lambda i: (i, 0)),
        out_shape=SDS((S, N), F32),
        compiler_params=_cp(("parallel",), 48),
    )(da, w)


def mm_dw_in_part(at, b, tile0, prev, name):
    M, S = at.shape
    ntiles = b.shape[1] // CT
    tk = 2048
    nk = S // tk
    per = W_SHARD // CT

    def body(a_ref, b_ref, *rest):
        o_ref, acc_ref, sems = rest[-3:]
        k, t = pl.program_id(0), pl.program_id(1)

        @pl.when(k == 0)
        def _():
            acc_ref[t] = _dot(a_ref[...], b_ref[...])

        @pl.when(k > 0)
        def _():
            acc_ref[t] += _dot(a_ref[...], b_ref[...])

        def out_copy(u):
            tile = tile0 + u
            off = (tile % per) * CT
            if not isinstance(off, int):
                off = pl.multiple_of(off, CT)
            return pltpu.make_async_copy(acc_ref.at[u], o_ref.at[tile // per, :, pl.ds(off, CT)], sems.at[u])

        @pl.when(k == nk - 1)
        def _():
            out_copy(t).start()

        @pl.when(jnp.logical_and(k == nk - 1, t == ntiles - 1))
        def _():
            for u in range(ntiles):
                out_copy(u).wait()

    return pl.pallas_call(
        body, name=name, grid=(nk, ntiles),
        in_specs=[pl.BlockSpec((M, tk), lambda k, t: (0, k)), pl.BlockSpec((tk, CT), lambda k, t: (k, t))]
        + ([] if prev is None else [ANY]),
        out_specs=ANY,
        out_shape=SDS((N_DEV, M, W_SHARD), F32),
        scratch_shapes=[pltpu.VMEM((ntiles, M, CT), F32), pltpu.SemaphoreType.DMA((ntiles,))],
        input_output_aliases={} if prev is None else {2: 0},
        compiler_params=_cp(("arbitrary", "arbitrary"), 48),
    )(at, b, *(() if prev is None else (prev,)))


def mm_dx_part(da, w8, tile0, name, after=None):
    S = da.shape[0]
    ntiles = da.shape[1] // CT
    tm = 2048
    wt = _w_tile(tile0)

    def body(a_ref, b_ref, *rest):
        o_ref, acc_ref = rest[-2:]
        t = pl.program_id(1)

        @pl.when(t == 0)
        def _():
            acc_ref[...] = jnp.zeros_like(acc_ref)

        acc_ref[...] += _dot_nt(a_ref[...], b_ref[...])

        @pl.when(t == ntiles - 1)
        def _():
            o_ref[...] = acc_ref[...]

    return pl.pallas_call(
        body, name=name, grid=(S // tm, ntiles),
        in_specs=[pl.BlockSpec((tm, CT), lambda i, t: (i, t)), pl.BlockSpec((None, D, CT), lambda i, t: wt(t))]
        + ([] if after is None else [ANY]),
        out_specs=pl.BlockSpec((tm, D), lambda i, t: (i, 0)),
        out_shape=SDS((S, D), F32),
        scratch_shapes=[pltpu.VMEM((tm, D), F32)],
        compiler_params=_cp(("parallel", "arbitrary"), 56),
    )(da, w8, *(() if after is None else (after,)))


HEADS_FWD = 4
HEADS_BWD = 2
AHEAD = 2


def _band_masks(not_first):
    row = lax.broadcasted_iota(jnp.int32, (QB, QB), 0)
    col = lax.broadcasted_iota(jnp.int32, (QB, QB), 1)
    cur = jnp.where(col <= row, 0.0, NEG)
    prev = jnp.where(col >= row, 0.0, NEG)
    first = jnp.where(jnp.logical_and(col >= row, not_first), 0.0, NEG)
    return col, jnp.concatenate([prev, cur], axis=1), jnp.concatenate([first, cur], axis=1)


def _fill_kv(ext, qkv_ref, kh_ref, vh_ref):
    ext[0:QB, 0:E] = kh_ref[...]
    ext[0:QB, E:2 * E] = vh_ref[...]
    ext[QB:, :] = qkv_ref[:, E:3 * E]


def attn_fwd(P, g, d):
    S = P.shape[0]
    L = S // d
    T = min(512, L)
    nq = T // QB
    ni = L // T

    def body(qkv_ref, kh_ref, vh_ref, o_ref, lse_ref, ext):
        col, mask, mask_first = _band_masks(pl.program_id(1) > 0)
        lse_ref[...] = jnp.zeros_like(lse_ref)
        _fill_kv(ext, qkv_ref, kh_ref, vh_ref)

        def heads(hp, carry):
            def front(h, j):
                cq = pl.ds(pl.multiple_of(h * HD, HD), HD)
                rows = slice(j * QB, (j + 1) * QB)
                krows = slice(j * QB, (j + 2) * QB)
                s = _dot_nt(qkv_ref[rows, cq], ext[krows, cq]) + (mask_first if j == 0 else mask)
                m = jnp.max(s, axis=1, keepdims=True)
                p = jnp.exp(s - m)
                den = jnp.sum(p, axis=1, keepdims=True)
                lse_ref[rows, :] = jnp.where(col == h, m + jnp.log(den), lse_ref[rows, :])
                return p.astype(BF16), den

            def back(h, j, p, den):
                off = pl.multiple_of(h * HD, HD)
                rows = slice(j * QB, (j + 1) * QB)
                krows = slice(j * QB, (j + 2) * QB)
                o_ref[rows, pl.ds(off, HD)] = (_dot(p, ext[krows, pl.ds(E + off, HD)]) / den).astype(BF16)

            items = [(HEADS_FWD * hp + hh, j) for hh in range(HEADS_FWD) for j in range(nq)]
            queue = [front(*it) for it in items[:AHEAD]]
            for u, it in enumerate(items):
                if u + AHEAD < len(items):
                    queue.append(front(*items[u + AHEAD]))
                back(*it, *queue.pop(0))
            return carry

        lax.fori_loop(0, NH // HEADS_FWD, heads, 0)

    halo = lambda r, i: jnp.maximum(r * (L // QB) + i * nq - 1, 0)
    return pl.pallas_call(
        body, name=f"attn_fwd{g}", grid=(d, ni),
        in_specs=[pl.BlockSpec((T, SEG), lambda r, i: (r * ni + i, 0)),
                  pl.BlockSpec((QB, E), lambda r, i: (halo(r, i), 1)),
                  pl.BlockSpec((QB, E), lambda r, i: (halo(r, i), 2))],
        out_specs=[pl.BlockSpec((T, E), lambda r, i: (r * ni + i, 0)),
                   pl.BlockSpec((T, HD), lambda r, i: (r * ni + i, 0))],
        out_shape=[SDS((S, E), BF16), SDS((S, HD), F32)],
        scratch_shapes=[pltpu.VMEM((T + QB, 2 * E), BF16)],
        compiler_params=_cp(("parallel", "parallel"), 48),
    )(P, P, P)


def _perm_specs(ts, C):
    return [pl.BlockSpec((ts, C), lambda i: (i, 0)),
            pl.BlockSpec((4, ts // 4, C), lambda i: (0, i, 0)),
            pl.BlockSpec((16, ts // 16, C), lambda i: (0, i, 0))]


def _perm_shapes(S, C, dtype):
    return [SDS((S, C), dtype), SDS((4, S // 4, C), dtype), SDS((16, S // 16, C), dtype)]


def combine_fwd(os_, lses, z, ehot):
    S = z.shape[0]
    ts = 256

    def body(o0, o1, o2, l0, l1, l2, z_ref, e_ref, y0, y1, y2, s0, s1, s2, ya_ref, yat_ref,
             so1, so2, sl1, sl2, sy, sl):
        _load_perm(so1, o1, 4)
        _load_perm(so2, o2, 16)
        _load_perm(sl1, l1, 4)
        _load_perm(sl2, l2, 16)
        ls = [l0[...], sl1[0], sl2[0]]
        m = jnp.maximum(jnp.maximum(ls[0], ls[1]), ls[2])
        es = [jnp.exp(l - m) for l in ls]
        den = es[0] + es[1] + es[2]
        sl[0] = m + jnp.log(den)
        y = None
        for e, o in zip(es, (o0[...].astype(F32), _scr_get(so1), _scr_get(so2))):
            w = e / den
            hi = w.astype(BF16)
            lo = (w - hi.astype(F32)).astype(BF16)
            wb = _dot(hi, e_ref[...]) + _dot(lo, e_ref[...])
            y = wb * o if y is None else y + wb * o
        z = z_ref[...].astype(F32)
        ya = y * (z * _sigmoid(z))
        ya_ref[...] = ya.astype(BF16)
        yat_ref[...] = ya.T.astype(BF16)
        _scr_put(sy, y)
        y0[...] = y.astype(BF16)
        _store_perm(y1, sy, 4)
        _store_perm(y2, sy, 16)
        s0[...] = sl[0]
        _store_perm(s1, sl, 4)
        _store_perm(s2, sl, 16)

    wide = pl.BlockSpec((ts, E), lambda i: (i, 0))
    os3 = [os_[0], os_[1].reshape(4, S // 4, E), os_[2].reshape(16, S // 16, E)]
    ls3 = [lses[0], lses[1].reshape(4, S // 4, HD), lses[2].reshape(16, S // 16, HD)]
    res = pl.pallas_call(
        body, name="combine_fwd", grid=(S // ts,),
        in_specs=_perm_specs(ts, E) + _perm_specs(ts, HD) + [wide, pl.BlockSpec((HD, E), lambda i: (0, 0))],
        out_specs=_perm_specs(ts, E) + _perm_specs(ts, HD) + [wide, pl.BlockSpec((E, ts), lambda i: (0, i))],
        out_shape=_perm_shapes(S, E, BF16) + _perm_shapes(S, HD, F32) + [SDS((S, E), BF16), SDS((E, S), BF16)],
        scratch_shapes=[_scr(ts, E), _scr(ts, E), _scr(ts, HD), _scr(ts, HD), _scr(ts, E), _scr(ts, HD)],
        compiler_params=_cp(("parallel",), 56),
    )(*os3, *ls3, z, ehot)
    ys = [res[0], res[1].reshape(S, E), res[2].reshape(S, E)]
    lse3 = [res[3], res[4].reshape(S, HD), res[5].reshape(S, HD)]
    return ys, lse3, res[6], res[7]


def mm_dya(da0, w_out, z, y):
    S = da0.shape[0]
    tm = 512

    def body(a_ref, w_ref, z_ref, y_ref, dy0, dy1, dy2, dz_ref, scr):
        for cidx in range(E // 256):
            col = slice(cidx * 256, (cidx + 1) * 256)
            dya = _dot_nt(a_ref[...], w_ref[col, :])
            zz = z_ref[:, col].astype(F32)
            sig = _sigmoid(zz)
            dy = dya * zz * sig
            scr[2 * cidx] = dy[:, :LANES]
            scr[2 * cidx + 1] = dy[:, LANES:]
            dy0[:, col] = dy.astype(BF16)
            dz_ref[:, col] = (dya * y_ref[:, col].astype(F32) * sig * (1.0 + zz * (1.0 - sig))).astype(BF16)
        _store_perm(dy1, scr, 4)
        _store_perm(dy2, scr, 16)

    wide = pl.BlockSpec((tm, E), lambda i: (i, 0))
    res = pl.pallas_call(
        body, name="mm_dya", grid=(S // tm,),
        in_specs=[pl.BlockSpec((tm, D), lambda i: (i, 0)), pl.BlockSpec((E, D), lambda i: (0, 0)), wide, wide],
        out_specs=_perm_specs(tm, E) + [wide],
        out_shape=_perm_shapes(S, E, BF16) + [SDS((S, E), BF16)],
        scratch_shapes=[_scr(tm, E)],
        compiler_params=_cp(("parallel",), 48),
    )(da0, w_out, z, y)
    return [res[0], res[1].reshape(S, E), res[2].reshape(S, E)], res[3]


def attn_bwd(P, dy, y, lse, tabs, g, d):
    S = P.shape[0]
    L = S // d
    T = min(512, L)
    nq = T // QB
    ni = L // T

    def body(qkv_ref, kh_ref, vh_ref, dy_ref, y_ref, lse_ref, c_ref, s1_ref, s2_ref,
             o_ref, dkc_ref, dvc_ref, ext):
        i = pl.program_id(1)
        col, mask, mask_first = _band_masks(i < ni - 1)
        _fill_kv(ext, qkv_ref, kh_ref, vh_ref)

        @pl.when(i == 0)
        def _():
            dkc_ref[...] = jnp.zeros_like(dkc_ref)
            dvc_ref[...] = jnp.zeros_like(dvc_ref)

        def heads(hp, carry):
            def cols(h):
                off = pl.multiple_of(h * HD, HD)
                return pl.ds(off, HD), pl.ds(E + off, HD), pl.ds(2 * E + off, HD)

            def front(h, j):
                cq, ck, _ = cols(h)
                rows = slice(j * QB, (j + 1) * QB)
                krows = slice(j * QB, (j + 2) * QB)
                dyj = dy_ref[rows, cq]
                lse_h = jnp.sum(jnp.where(col == h, lse_ref[rows, :], 0.0), axis=1, keepdims=True)
                delta = jnp.sum(dyj.astype(F32) * y_ref[rows, cq].astype(F32), axis=1, keepdims=True)
                s = _dot_nt(qkv_ref[rows, cq], ext[krows, cq])
                p = jnp.exp(s + (mask_first if j == 0 else mask) - lse_h)
                ds = (p * (_dot_nt(dyj, ext[krows, ck]) - delta)).astype(BF16)
                return ds, ds.T, p.astype(BF16).T

            def back(h, j, ds, ds_t, p_t, pend_dk, pend_dv):
                cq, ck, cv = cols(h)
                rows = slice(j * QB, (j + 1) * QB)
                krows = slice(j * QB, (j + 2) * QB)
                dq = _dot(ds, ext[krows, cq]) * SCALE
                dk2 = _dot(ds_t, qkv_ref[rows, cq])
                dv2 = _dot(p_t, dy_ref[rows, cq])
                c, s1, s2 = c_ref[rows, :], s1_ref[rows, :], s2_ref[rows, :]
                o_ref[rows, cq] = _unrope(dq, c, s1, s2).astype(BF16)
                o_ref[rows, ck] = _unrope(dk2[QB:] + pend_dk, c, s1, s2).astype(BF16)
                o_ref[rows, cv] = (dv2[QB:] + pend_dv).astype(BF16)
                return dk2[:QB], dv2[:QB]

            items = [(HEADS_BWD * hp + hh, j) for hh in range(HEADS_BWD) for j in reversed(range(nq))]
            queue = [front(*it) for it in items[:AHEAD]]
            pend = None
            for u, (h, j) in enumerate(items):
                if u + AHEAD < len(items):
                    queue.append(front(*items[u + AHEAD]))
                if j == nq - 1:
                    pend = (dkc_ref[:, cols(h)[0]], dvc_ref[:, cols(h)[0]])
                pend = back(h, j, *queue.pop(0), *pend)
                if j == 0:
                    dkc_ref[:, cols(h)[0]], dvc_ref[:, cols(h)[0]] = pend
            return carry

        lax.fori_loop(0, NH // HEADS_BWD, heads, 0)

    blk = lambda r, i: r * ni + ni - 1 - i
    halo = lambda r, i: jnp.maximum(r * (L // QB) + (ni - 1 - i) * nq - 1, 0)
    main = pl.BlockSpec((T, SEG), lambda r, i: (blk(r, i), 0))
    wide = pl.BlockSpec((T, E), lambda r, i: (blk(r, i), 0))
    narrow = pl.BlockSpec((T, HD), lambda r, i: (blk(r, i), 0))
    return pl.pallas_call(
        body, name=f"attn_bwd{g}", grid=(d, ni),
        in_specs=[main, pl.BlockSpec((QB, E), lambda r, i: (halo(r, i), 1)),
                  pl.BlockSpec((QB, E), lambda r, i: (halo(r, i), 2)),
                  wide, wide, narrow, narrow, narrow, narrow],
        out_specs=main, out_shape=SDS((S, SEG), BF16),
        scratch_shapes=[pltpu.VMEM((QB, E), F32), pltpu.VMEM((QB, E), F32), pltpu.VMEM((T + QB, 2 * E), BF16)],
        compiler_params=_cp(("arbitrary", "arbitrary"), 56),
    )(P, P, P, dy, y, lse, *tabs)


def _pool_cnt(t0, rows):
    t = (lax.broadcasted_iota(jnp.int32, (rows, E), 0) + t0 + 1).astype(F32)
    ch = lax.broadcasted_iota(jnp.int32, (rows, E), 1)
    w = jnp.where(ch < PC, 2.0, jnp.where(ch < 2 * PC, 4.0, jnp.where(ch < 3 * PC, 8.0, 16.0)))
    return jnp.minimum(t, w)


def _by_group(parts):
    return jnp.concatenate([parts[g][:, g * PC:(g + 1) * PC] for g in range(4)], axis=1)


def pool_fwd(uz):
    S = uz.shape[0]
    ts = 256

    def body(u_ref, h_ref, o_ref, ot_ref):
        i = pl.program_id(0)
        u = u_ref[...]
        halo = jnp.where(i > 0, h_ref[...], 0.0)
        ext = jnp.concatenate([halo, u], axis=0)
        s2 = ext + pltpu.roll(ext, 1, 0)
        s4 = s2 + pltpu.roll(s2, 2, 0)
        s8 = s4 + pltpu.roll(s4, 4, 0)
        s16 = s8 + pltpu.roll(s8, 8, 0)
        win = _by_group([s2, s4, s8, s16])[16:, :]
        pooled = win / _pool_cnt(i * ts, ts) - u
        o_ref[...] = pooled.astype(BF16)
        ot_ref[...] = pooled.T.astype(BF16)

    return pl.pallas_call(
        body, name="pool_fwd", grid=(S // ts,),
        in_specs=[pl.BlockSpec((ts, E), lambda i: (i, 0)),
                  pl.BlockSpec((16, E), lambda i: (jnp.maximum(i * (ts // 16) - 1, 0), 0))],
        out_specs=[pl.BlockSpec((ts, E), lambda i: (i, 0)), pl.BlockSpec((E, ts), lambda i: (0, i))],
        out_shape=[SDS((S, E), BF16), SDS((E, S), BF16)],
        compiler_params=_cp(("parallel",), 48),
    )(uz, uz)


def pool_bwd(dpooled, duz):
    S = dpooled.shape[0]
    ts = 256
    nt = S // ts

    def body(d_ref, h_ref, alias_ref, o_ref):
        i = pl.program_id(0)
        dp = d_ref[...].astype(F32)
        halo = jnp.where(i < nt - 1, h_ref[...].astype(F32), 0.0)
        n = ts + 16
        ext = jnp.concatenate([dp, halo], axis=0) / _pool_cnt(i * ts, n)
        f2 = ext + pltpu.roll(ext, n - 1, 0)
        f4 = f2 + pltpu.roll(f2, n - 2, 0)
        f8 = f4 + pltpu.roll(f4, n - 4, 0)
        f16 = f8 + pltpu.roll(f8, n - 8, 0)
        win = _by_group([f2, f4, f8, f16])[:ts, :]
        o_ref[...] = (win - dp).astype(BF16)

    return pl.pallas_call(
        body, name="pool_bwd", grid=(nt,),
        in_specs=[pl.BlockSpec((ts, E), lambda i: (i, 0)),
                  pl.BlockSpec((16, E), lambda i: (jnp.minimum((i + 1) * (ts // 16), S // 16 - 1), 0)), ANY],
        out_specs=pl.BlockSpec((ts, E), lambda i: (i, 0)),
        out_shape=SDS(duz.shape, BF16),
        input_output_aliases={2: 0},
        compiler_params=_cp(("parallel",), 48),
    )(dpooled, dpooled, duz)


def mm_grp(pooled, wg, b, scale, uz):
    S = pooled.shape[0]
    tm = 512

    def body(p_ref, w_ref, b_ref, s_ref, z_ref, h_ref, y_ref, yt_ref):
        for g in range(4):
            cs = slice(g * PC, (g + 1) * PC)
            h = _dot(p_ref[:, cs], w_ref[g]) + b_ref[:, cs]
            z = z_ref[:, cs]
            yp = h * s_ref[:, cs] * (z * _sigmoid(z))
            h_ref[:, cs] = h.astype(BF16)
            y_ref[:, cs] = yp.astype(BF16)
            yt_ref[cs, :] = yp.T.astype(BF16)

    row = pl.BlockSpec((tm, E), lambda i: (i, 0))
    vec = pl.BlockSpec((1, E), lambda i: (0, 0))
    return pl.pallas_call(
        body, name="mm_grp", grid=(S // tm,),
        in_specs=[row, pl.BlockSpec((4, PC, PC), lambda i: (0, 0, 0)), vec, vec,
                  pl.BlockSpec((tm, E), lambda i: (i, 1))],
        out_specs=[row, row, pl.BlockSpec((E, tm), lambda i: (0, i))],
        out_shape=[SDS((S, E), BF16), SDS((S, E), BF16), SDS((E, S), BF16)],
        compiler_params=_cp(("parallel",), 48),
    )(pooled, wg, b, scale, uz)


def mm_dyp(da1, w_out, uz, h, scale):
    S = da1.shape[0]
    tm = 512

    def body(a_ref, w_ref, z_ref, h_ref, s_ref, dh_ref, dz_ref, dsc_ref, db_ref):
        @pl.when(pl.program_id(0) == 0)
        def _():
            dsc_ref[...] = jnp.zeros_like(dsc_ref)
            db_ref[...] = jnp.zeros_like(db_ref)

        for cidx in range(E // 256):
            col = slice(cidx * 256, (cidx + 1) * 256)
            dyp = _dot_nt(a_ref[...], w_ref[col, :])
            z = z_ref[:, col]
            hh = h_ref[:, col].astype(F32)
            sc = s_ref[:, col]
            sig = _sigmoid(z)
            dhs = dyp * z * sig
            dz_ref[:, col] = (dyp * hh * sc * sig * (1.0 + z * (1.0 - sig))).astype(BF16)
            dh = dhs * sc
            dh_ref[:, col] = dh.astype(BF16)
            dsc_ref[:, col] += _fold8(dhs * hh)
            db_ref[:, col] += _fold8(dh)

    row = pl.BlockSpec((tm, E), lambda i: (i, 0))
    acc = pl.BlockSpec((8, E), lambda i: (0, 0))
    return pl.pallas_call(
        body, name="mm_dyp", grid=(S // tm,),
        in_specs=[pl.BlockSpec((tm, D), lambda i: (i, 0)), pl.BlockSpec((E, D), lambda i: (0, 0)),
                  pl.BlockSpec((tm, E), lambda i: (i, 1)), row, pl.BlockSpec((1, E), lambda i: (0, 0))],
        out_specs=[row, pl.BlockSpec((tm, E), lambda i: (i, 1)), acc, acc],
        out_shape=[SDS((S, E), BF16), SDS((S, 2 * E), BF16), SDS((8, E), F32), SDS((8, E), F32)],
        compiler_params=_cp(("arbitrary",), 48),
    )(da1, w_out, uz, h, scale)


def mm_dpooled(dh, wg):
    S = dh.shape[0]
    tm = 1024

    def body(a_ref, w_ref, o_ref):
        for g in range(4):
            cs = slice(g * PC, (g + 1) * PC)
            o_ref[:, cs] = _dot_nt(a_ref[:, cs], w_ref[g]).astype(BF16)

    row = pl.BlockSpec((tm, E), lambda i: (i, 0))
    return pl.pallas_call(
        body, name="mm_dpooled", grid=(S // tm,),
        in_specs=[row, pl.BlockSpec((4, PC, PC), lambda i: (0, 0, 0))],
        out_specs=row, out_shape=SDS((S, E), BF16),
        compiler_params=_cp(("parallel",), 48),
    )(dh, wg)


def _rope_tables(positions):
    inv_freq = 500000.0 ** (-jnp.arange(0, 32, 2, dtype=F32) / 32)
    S = positions.shape[0]
    ang = jnp.repeat(positions.astype(F32).reshape(S // 8, 8), 16, axis=1) * jnp.tile(inv_freq, 8)
    cos, sin = lax.optimization_barrier((jnp.cos(ang), jnp.sin(ang)))
    cos, sin = cos.reshape(S, 16), sin.reshape(S, 16)
    one = jnp.ones((S, HD - 32), F32)
    zero16 = jnp.zeros((S, 16), F32)
    zero = jnp.zeros((S, HD - 32), F32)
    c = jnp.concatenate([cos, cos, one], axis=1)
    s1 = jnp.concatenate([-sin, zero16, zero], axis=1)
    s2 = jnp.concatenate([zero16, sin, zero], axis=1)
    return c.astype(BF16), s1.astype(BF16), s2.astype(BF16)


def kernel(x, positions, norm_pre, norm_post, attn_w_in, attn_w_out, pool_w_in, pool_w_grp, pool_b_grp, pool_scale, pool_w_out, loss_target, m_norm_pre, m_norm_post, m_attn_w_in, m_attn_w_out, m_pool_w_in, m_pool_w_grp, m_pool_b_grp, m_pool_scale, m_pool_w_out, v_norm_pre, v_norm_post, v_attn_w_in, v_attn_w_out, v_pool_w_in, v_pool_w_grp, v_pool_b_grp, v_pool_scale, v_pool_w_out):
    S = x.shape[1]
    xi, yi, ci = _mesh_pos()
    dev = 4 * xi + 2 * yi + ci
    x2 = x[0]
    tgt = loss_target[0]

    small = jnp.concatenate([pool_b_grp[0].reshape(2, HD), pool_scale[0].reshape(2, HD),
                             jnp.zeros((4, HD), F32)], axis=0)
    w_in_l = attn_w_in[0].astype(BF16)
    hop1, hop1_token = split_start("gather_w_in_start", [w_in_l], [lax.empty((N_DEV,) + w_in_l.shape, BF16)],
                                   _hop1_plan(), 3)

    pos = positions[0]
    tabs = [_rope_tables(pos.reshape(S // d, d).T.reshape(S)) for d in DIL]
    ehot = (jnp.arange(E)[None, :] // HD == jnp.arange(HD)[:, None]).astype(BF16)
    seg_tiles = SEG // CT

    xn0, xn0_4, xn0_16, xn0t = norm_pre0(x2, norm_pre[0:1], hop1_token)
    xn0s = [xn0, xn0_4.reshape(S, D), xn0_16.reshape(S, D)]
    xn0ts = [xn0t, transpose_rows(xn0s[1], "xn0t_4"), transpose_rows(xn0s[2], "xn0t_16")]

    (w_in_l,), (w_in8,) = split_wait("gather_w_in_wait", hop1, _hop1_plan(), xn0ts[2])
    hop2, hop2_token = split_start("gather_w_in_hop2_start", [w_in8], None, _hop2_plan(D), 4)
    _, (w_in8,) = split_wait("gather_w_in_hop2_wait", hop2, _hop2_plan(D), hop2_token, inplace=True)
    hop3, hop3_token = split_start("gather_w_in_hop3_start", [w_in8], None, _hop3_plan(), 1)
    _, (w_in8,) = split_wait("gather_w_in_hop3_wait", hop3, _hop3_plan(), hop3_token, inplace=True)
    w_in8 = lax.dynamic_update_slice(w_in8, w_in_l[None], (dev, 0, 0))
    small, w_in8 = lax.optimization_barrier((small, w_in8))
    rest_l = [attn_w_out[0].astype(BF16), pool_w_in[0].astype(BF16), pool_w_grp[0].astype(BF16),
              pool_w_out[0].astype(BF16), small]
    rest_flight, rest_token = split_start(
        "gather_rest_start", rest_l, [lax.empty((N_DEV,) + a.shape, a.dtype) for a in rest_l], _peers_plan(), 7)
    Ps, os_, lses = [], [], []
    for g, d in enumerate(DIL):
        P = mm_in(xn0s[g], w_in8, g * seg_tiles, seg_tiles, tabs[g], f"mm_qkv{g}", after=rest_token)
        o, l = attn_fwd(P, g, d)
        Ps.append(P)
        os_.append(o)
        lses.append(l)
    z0 = mm_in(xn0, w_in8, 3 * seg_tiles, E // CT, None, "mm_z0")
    ys, lse3, ya, yat = combine_fwd(os_, lses, z0, ehot)

    rest_l, rest8 = split_wait("gather_rest_wait", rest_flight, _peers_plan(), ya)
    rest8 = [lax.dynamic_update_slice(r8, a[None], (dev,) + (0,) * a.ndim) for r8, a in zip(rest8, rest_l)]
    w_out8, wp_in8, wg8, wp_out8, small8 = rest8
    w_out = w_out8.reshape(E, D)
    wp_out = wp_out8.reshape(E, D)
    wp_in = wp_in8.transpose(1, 0, 2).reshape(D, 2 * E)
    wg = wg8.transpose(1, 0, 2, 3).reshape(4, PC, PC)
    b_full = small8[:, 0:2, :].reshape(N_DEV, 4, PC // N_DEV).transpose(1, 0, 2).reshape(1, E)
    scale_full = small8[:, 2:4, :].reshape(1, E)
    a0 = mm_rows(ya, w_out, "mm_out0", F32)
    h1, xn1, xn1t = post0_pre1(x2, a0, norm_post[0:1], norm_pre[1:2])

    uz = mm_rows(xn1, wp_in, "mm_uz", F32, tm=512)
    pooled, pooled_t = pool_fwd(uz)
    hgrp, yp, ypt = mm_grp(pooled, wg, b_full, scale_full, uz)
    a1 = mm_rows(yp, wp_out, "mm_out1", F32)
    dh2, da1, loss_rows, dg_post1 = post1_loss(h1, a1, tgt, norm_post[1:2])
    loss = lax.psum(0.5 / D * jnp.sum(loss_rows), ("x", "y", "c"))

    dh, duz, dscale_p, db_p = mm_dyp(da1, wp_out, uz, hgrp, scale_full)
    dpooled = mm_dpooled(dh, wg)
    duz = pool_bwd(dpooled, duz)
    g_wg = mm_dwg(pooled_t, dh)
    g_wp_out = mm_wgrad_rows(ypt, da1, "mm_dwp_out")
    g_wp_in = mm_wgrad_cols(xn1t, duz, "mm_dwp_in", shard=PC)
    dxn1 = mm_dx_full(duz, wp_in, "mm_dxn1")
    dh1, da0, dg_pre1, dg_post0 = mid_bwd(dxn1, dh2, h1, a0, norm_pre[1:2], norm_post[0:1])

    dys, dz0 = mm_dya(da0, w_out, z0, ys[0])
    g_w_out = mm_wgrad_rows(yat, da0, "mm_dw_out")
    g_w_in = mm_dw_in_part(xn0t, dz0, 3 * seg_tiles, None, "mm_dw_in_z")
    dPs = []
    for g, d in enumerate(DIL):
        dP = attn_bwd(Ps[g], dys[g], ys[g], lse3[g], tabs[g], g, d)
        g_w_in = mm_dw_in_part(xn0ts[g], dP, g * seg_tiles, g_w_in, f"mm_dw_in{g}")
        dPs.append(dP)

    cidx = ci.astype(jnp.int32).reshape(1)
    chip = (2 * xi + yi).astype(jnp.int32).reshape(1)
    fulls = [g_w_in, g_w_out.reshape(N_DEV, E // N_DEV, D), g_wp_in,
             g_wg.reshape(4, N_DEV, PC // N_DEV, PC).transpose(1, 0, 2, 3).reshape(N_DEV, 4 * PC // N_DEV, PC),
             g_wp_out.reshape(N_DEV, E // N_DEV, D)]
    pair_flight, pair_token = split_start(
        "rs_pair_start", fulls, [lax.empty((4,) + f.shape[1:], F32) for f in fulls], _pair_plan(), 4)
    dx_z = mm_dx_part(dz0, w_in8, 3 * seg_tiles, "mm_dxn0_z", after=pair_token)
    dx_0 = mm_dx_part(dPs[0], w_in8, 0, "mm_dxn0_0", after=dx_z)
    fulls, sibs = split_wait("rs_pair_wait", pair_flight, _pair_plan(), dx_0)
    parts = [pair_add(f, s, cidx, f"pair_add{k}") for k, (f, s) in enumerate(zip(fulls, sibs))]
    chips_flight, chips_token = split_start(
        "rs_chips_start", parts, [jnp.zeros(p.shape, BF16) for p in parts], _chips_plan(), 3)
    dx_1 = mm_dx_part(dPs[1], w_in8, seg_tiles, "mm_dxn0_1", after=chips_token)
    dx_2 = mm_dx_part(dPs[2], w_in8, 2 * seg_tiles, "mm_dxn0_2", after=dx_1)
    grad_x, dg_pre0 = pre0_bwd(dx_0, dx_z, dx_1, dx_2, dh1, x2, norm_pre[0:1])
    parts, recvs = split_wait("rs_chips_wait", chips_flight, _chips_plan(), grad_x)
    shards = [(attn_w_in, m_attn_w_in, v_attn_w_in), (attn_w_out, m_attn_w_out, v_attn_w_out),
              (pool_w_in, m_pool_w_in, v_pool_w_in), (pool_w_grp, m_pool_w_grp, v_pool_w_grp),
              (pool_w_out, m_pool_w_out, v_pool_w_out)]
    big = []
    for k, (recv, part, (w, m, v)) in enumerate(zip(recvs, parts, shards)):
        shp = w.shape
        r2 = recv.shape[1:]
        res = adamw_sum(recv, part, chip, w.reshape(r2), m.reshape(r2), v.reshape(r2), f"adamw{k}")
        big.append([t.reshape(shp) for t in res])

    smalls = jnp.concatenate([dg_pre0.sum(0, keepdims=True), dg_pre1.sum(0, keepdims=True),
                              dg_post0.sum(0, keepdims=True), dg_post1.sum(0, keepdims=True),
                              db_p.sum(0).reshape(2, D), dscale_p.sum(0).reshape(2, D)], axis=0)
    (smalls8,) = all_gather([smalls], "gather_small_grads")
    tot = sum_slots(smalls8, "sum_small_grads")
    g_norm_pre, g_norm_post = tot[0:2], tot[2:4]
    g_b = lax.dynamic_slice_in_dim(tot[4:6].reshape(4, PC), dev * (PC // N_DEV), PC // N_DEV, axis=1)[None]
    g_scale = lax.dynamic_slice_in_dim(tot[6:8].reshape(1, E), dev * (E // N_DEV), E // N_DEV, axis=1)
    sm = [adamw_small(g_norm_pre, norm_pre, m_norm_pre, v_norm_pre, "adamw_norm_pre"),
          adamw_small(g_norm_post, norm_post, m_norm_post, v_norm_post, "adamw_norm_post"),
          adamw_small(g_b, pool_b_grp, m_pool_b_grp, v_pool_b_grp, "adamw_b"),
          adamw_small(g_scale, pool_scale, m_pool_scale, v_pool_scale, "adamw_scale")]

    grads = [g_norm_pre, g_norm_post, big[0][0], big[1][0], big[2][0], big[3][0], g_b, g_scale, big[4][0]]

    def pick(k):
        return [sm[0][k - 1], sm[1][k - 1], big[0][k], big[1][k], big[2][k], big[3][k], sm[2][k - 1], sm[3][k - 1],
                big[4][k]]

    return (loss, grad_x[None], *grads, *pick(1), *pick(2), *pick(3))
```

```python
import math

import jax
import jax.numpy as jnp
from jax import lax
from jax.experimental import pallas as pl
from jax.experimental.pallas import tpu as pltpu

F32 = jnp.float32
BF16 = jnp.bfloat16
SDS = jax.ShapeDtypeStruct

N_DEV = 8
D = 1024
E = 2048
HD = 128
NH = E // HD
DIL = (1, 4, 16)
QB = 128
SEG = 3 * E
W_IN_COLS = 3 * SEG + E
W_SHARD = W_IN_COLS // N_DEV
CT = 512
PC = E // 4
EPS = 1e-6
NEG = -1e30
SCALE = 1.0 / math.sqrt(HD)
LR, B1, B2, ADAM_EPS, WD, STEP = 0.001, 0.9, 0.999, 1e-08, 0.01, 10
MIB = 1024 * 1024
ANY = pl.BlockSpec(memory_space=pl.ANY)
MESH = pl.DeviceIdType.MESH


def _cp(sem, mb):
    return pltpu.CompilerParams(dimension_semantics=sem, vmem_limit_bytes=mb * MIB)


def _dot(a, b):
    return jnp.dot(a, b, preferred_element_type=F32)


def _dot_nt(a, b):
    return lax.dot_general(a, b, (((1,), (1,)), ((), ())), preferred_element_type=F32)


def _rms(h):
    return lax.rsqrt(jnp.mean(h * h, axis=-1, keepdims=True) + EPS)


def _row_tile(R, C, budget):
    tr = R
    while tr * C * 4 > budget and tr % 16 == 0:
        tr //= 2
    return tr


def _fold8(t):
    return t.reshape(t.shape[0] // 8, 8, t.shape[1]).sum(axis=0)


def _sigmoid(z):
    return pl.reciprocal(1.0 + jnp.exp(-z), approx=True)


LANES = 128


def _scr(rows, C):
    return pltpu.VMEM((C // LANES, rows, LANES), F32)


def _scr_put(scr, val):
    for c in range(scr.shape[0]):
        scr[c] = val[:, c * LANES:(c + 1) * LANES]


def _scr_get(scr):
    return jnp.concatenate([scr[c] for c in range(scr.shape[0])], axis=1)


def _store_perm(dst_ref, scr, d):
    n = dst_ref.shape[1]
    for r in range(d):
        for c in range(scr.shape[0]):
            dst_ref[r, :, c * LANES:(c + 1) * LANES] = scr[c, pl.ds(r, n, stride=d), :].astype(dst_ref.dtype)


def _load_perm(scr, src_ref, d, add=False):
    n = src_ref.shape[1]
    for r in range(d):
        rows = pl.ds(r, n, stride=d)
        for c in range(scr.shape[0]):
            v = src_ref[r, :, c * LANES:(c + 1) * LANES].astype(F32)
            scr[c, rows, :] = scr[c, rows, :] + v if add else v


def _rope(t, c, s1, s2):
    t = t.astype(BF16)
    return t * c + pltpu.roll(t, HD - 16, 1) * s1 + pltpu.roll(t, 16, 1) * s2


def _unrope(t, c, s1, s2):
    t = t.astype(BF16)
    return t * c - pltpu.roll(t, HD - 16, 1) * s1 - pltpu.roll(t, 16, 1) * s2


def _mesh_pos():
    return lax.axis_index("x"), lax.axis_index("y"), lax.axis_index("c")


def all_gather(arrs, name):
    n = len(arrs)

    def body(*refs):
        ins, outs = refs[:n], refs[n:2 * n]
        send_sems, recv_sems, local_sems = refs[2 * n:]
        x, y, c = _mesh_pos()
        me, sib = (x, y, c), (x, y, 1 - c)
        chips = [(1 - x, y), (x, 1 - y), (1 - x, 1 - y)]

        def slot(p):
            return 4 * p[0] + 2 * p[1] + p[2]

        def copy(a, k, block, to, src=None):
            dst = outs[a].at[slot(block)]
            return pltpu.make_async_remote_copy(
                src_ref=dst if src is None else src, dst_ref=dst,
                send_sem=send_sems.at[a, k], recv_sem=recv_sems.at[a, k],
                device_id=to, device_id_type=MESH)

        mine = [pltpu.make_async_copy(ins[a], outs[a].at[slot(me)], local_sems.at[a]) for a in range(n)]
        for cp in mine:
            cp.start()
        first = []
        for a in range(n):
            first.append(copy(a, 0, me, sib, src=ins[a]))
            for j, chip in enumerate(chips):
                first.append(copy(a, 1 + j, me, (*chip, c), src=ins[a]))
        for cp in first:
            cp.start()
        passed = []
        for j, chip in enumerate(chips):
            for a in range(n):
                copy(a, 1 + j, (*chip, c), me).wait_recv()
                fw = copy(a, 4 + j, (*chip, c), sib)
                fw.start()
                passed.append(fw)
        for a in range(n):
            copy(a, 0, sib, me).wait_recv()
        for j, chip in enumerate(chips):
            for a in range(n):
                copy(a, 4 + j, (*chip, 1 - c), me).wait_recv()
        for cp in first + passed:
            cp.wait_send()
        for cp in mine:
            cp.wait()

    return pl.pallas_call(
        body, name=name,
        out_shape=[SDS((N_DEV,) + a.shape, a.dtype) for a in arrs],
        in_specs=[ANY] * n, out_specs=[ANY] * n,
        scratch_shapes=[pltpu.SemaphoreType.DMA((n, 7)), pltpu.SemaphoreType.DMA((n, 7)),
                        pltpu.SemaphoreType.DMA((n,))],
    )(*arrs)


HBM_SPEC = pl.BlockSpec(memory_space=pltpu.HBM)
SEM_SPEC = pl.BlockSpec(memory_space=pltpu.SEMAPHORE)
EFFECT = pltpu.SideEffectType.DATAFLOW_SIDE_EFFECTING


def _pair_plan():
    def plan(x, y, c):
        return [(2 * q + (1 - c), q, (x, y, 1 - c)) for q in range(4)]
    return plan


def _chips_plan():
    def plan(x, y, c):
        chips = [(1 - x, y), (x, 1 - y), (1 - x, 1 - y)]
        return [(2 * cx + cy, 2 * x + y, (cx, cy, c)) for cx, cy in chips]
    return plan


def _hop1_plan():
    def plan(x, y, c):
        me = 4 * x + 2 * y + c
        return [(None, me, (x, y, 1 - c)), (None, me, (1 - x, y, c)), (None, me, (x, 1 - y, c))]
    return plan


def _hop2_plan(rows):
    half = rows // 2

    def plan(x, y, c):
        sx, sy = 4 * (1 - x) + 2 * y + c, 4 * x + 2 * (1 - y) + c
        top, bottom = pl.ds(0, half), pl.ds(half, half)
        return [((sx, top), (sx, top), (x, 1 - y, c)), ((sy, bottom), (sy, bottom), (1 - x, y, c)),
                (sx, sx, (x, y, 1 - c)), (sy, sy, (x, y, 1 - c))]
    return plan


def _hop3_plan():
    def plan(x, y, c):
        sd = 4 * (1 - x) + 2 * (1 - y) + c
        return [(sd, sd, (x, y, 1 - c))]
    return plan


def _peers_plan():
    def plan(x, y, c):
        out = []
        for k in range(1, N_DEV):
            fx, fy, fc = (k >> 2) & 1, (k >> 1) & 1, k & 1
            px, py, pc = (x + fx) % 2, (y + fy) % 2, (c + fc) % 2
            out.append((None, 4 * x + 2 * y + c, (px, py, pc)))
        return out
    return plan


def _split_copies(plan, srcs, lands, send_sems, recv_sems):
    x, y, c = _mesh_pos()
    cps = []
    for a, (src, land) in enumerate(zip(srcs, lands)):
        steps = plan(x, y, c)
        for k, (si, li, to) in enumerate(steps):
            sem = a * len(steps) + k
            cps.append(pltpu.make_async_remote_copy(
                src_ref=src if si is None else src.at[si], dst_ref=land.at[li],
                send_sem=send_sems.at[sem], recv_sem=recv_sems.at[sem],
                device_id=to, device_id_type=MESH))
    return cps


def split_start(name, srcs, lands, plan, nk):
    n = len(srcs)
    ops = list(srcs) + ([] if lands is None else list(lands))
    nb = len(ops)

    def body(*refs):
        token = refs[-1]
        for cp in _split_copies(plan, refs[:n], refs[nb - n:nb], refs[nb], refs[nb + 1]):
            cp.start()
        token[...] = jnp.zeros_like(token)

    ops = [pltpu.with_memory_space_constraint(a, pltpu.HBM) for a in ops]
    res = pl.pallas_call(
        body, name=name,
        out_shape=(pltpu.SemaphoreType.DMA((n * nk,)), pltpu.SemaphoreType.DMA((n * nk,)),
                   *[pltpu.HBM(a.shape, a.dtype) for a in ops], SDS((8, 128), F32)),
        in_specs=[HBM_SPEC] * nb,
        out_specs=(SEM_SPEC, SEM_SPEC, *[HBM_SPEC] * nb, pl.BlockSpec(memory_space=pltpu.VMEM)),
        input_output_aliases={i: 2 + i for i in range(nb)},
        compiler_params=pltpu.CompilerParams(has_side_effects=EFFECT),
    )(*ops)
    return res[:-1], res[-1]


def split_wait(name, flight, plan, after, inplace=False):
    send_sems, recv_sems = flight[0], flight[1]
    bufs = list(flight[2:])
    nb = len(bufs)
    n = nb if inplace else nb // 2

    def body(*refs):
        for cp in _split_copies(plan, refs[:n], refs[nb - n:nb], refs[nb], refs[nb + 1]):
            cp.wait_send()
            cp.wait_recv()

    res = pl.pallas_call(
        body, name=name,
        out_shape=[pltpu.HBM(a.shape, a.dtype) for a in bufs],
        in_specs=[HBM_SPEC] * nb + [SEM_SPEC, SEM_SPEC, ANY],
        out_specs=[HBM_SPEC] * nb,
        input_output_aliases={i: i for i in range(nb)},
        compiler_params=pltpu.CompilerParams(has_side_effects=EFFECT),
    )(*bufs, send_sems, recv_sems, after)
    return res[:n], res[nb - n:]


def pair_add(full, sib, cidx, name):
    _, R, C = full.shape
    tr = _row_tile(R, C, MIB)

    def body(c_ref, a_ref, b_ref, o_ref):
        o_ref[...] = (a_ref[...] + b_ref[...]).astype(BF16)

    return pl.pallas_call(
        body, name=name,
        grid_spec=pltpu.PrefetchScalarGridSpec(
            num_scalar_prefetch=1, grid=(4, R // tr),
            in_specs=[pl.BlockSpec((None, tr, C), lambda q, i, cr: (2 * q + cr[0], i, 0)),
                      pl.BlockSpec((None, tr, C), lambda q, i, cr: (q, i, 0))],
            out_specs=pl.BlockSpec((None, tr, C), lambda q, i, cr: (q, i, 0))),
        out_shape=SDS((4, R, C), BF16),
        compiler_params=_cp(("parallel", "parallel"), 32),
    )(cidx, full, sib)


def _adam_math(w, g, m, v):
    m2 = B1 * m + (1.0 - B1) * g
    v2 = B2 * v + (1.0 - B2) * (g * g)
    m_hat = m2 / (1.0 - B1 ** STEP)
    v_hat = v2 / (1.0 - B2 ** STEP)
    delta = -LR * (m_hat / (jnp.sqrt(v_hat) + ADAM_EPS) + WD * w)
    return delta, m2, v2


def adamw_sum(recv, part, chip, w, m, v, name):
    K, R, C = recv.shape
    tr = _row_tile(R, C, MIB)

    def body(chip_ref, r_ref, p_ref, w_ref, m_ref, v_ref, g_ref, d_ref, m2_ref, v2_ref):
        g = r_ref[0].astype(F32)
        for k in range(1, K):
            g = g + r_ref[k].astype(F32)
        g = g + p_ref[...].astype(F32)
        delta, m2, v2 = _adam_math(w_ref[...], g, m_ref[...], v_ref[...])
        g_ref[...] = g
        d_ref[...] = delta
        m2_ref[...] = m2
        v2_ref[...] = v2

    tile = pl.BlockSpec((tr, C), lambda i, cr: (i, 0))
    return pl.pallas_call(
        body, name=name,
        grid_spec=pltpu.PrefetchScalarGridSpec(
            num_scalar_prefetch=1, grid=(R // tr,),
            in_specs=[pl.BlockSpec((K, tr, C), lambda i, cr: (0, i, 0)),
                      pl.BlockSpec((None, tr, C), lambda i, cr: (cr[0], i, 0)), tile, tile, tile],
            out_specs=[tile] * 4),
        out_shape=[SDS((R, C), F32)] * 4,
        compiler_params=_cp(("parallel",), 32),
    )(chip, recv, part, w, m, v)


def adamw_small(g, w, m, v, name):
    def body(g_ref, w_ref, m_ref, v_ref, d_ref, m2_ref, v2_ref):
        delta, m2, v2 = _adam_math(w_ref[...], g_ref[...], m_ref[...], v_ref[...])
        d_ref[...] = delta
        m2_ref[...] = m2
        v2_ref[...] = v2

    return pl.pallas_call(body, name=name, out_shape=[SDS(w.shape, F32)] * 3)(g, w, m, v)


def sum_slots(a, name):
    K = a.shape[0]

    def body(a_ref, o_ref):
        t = a_ref[0]
        for k in range(1, K):
            t = t + a_ref[k]
        o_ref[...] = t

    return pl.pallas_call(body, name=name, out_shape=SDS(a.shape[1:], F32))(a)


def norm_pre0(x, g, after):
    S = x.shape[0]
    ts = 512

    def body(x_ref, g_ref, after_ref, o_ref, o4_ref, o16_ref, ot_ref, scr):
        h = x_ref[...]
        xn = h * _rms(h) * g_ref[...]
        o_ref[...] = xn.astype(BF16)
        ot_ref[...] = xn.T.astype(BF16)
        _scr_put(scr, xn)
        _store_perm(o4_ref, scr, 4)
        _store_perm(o16_ref, scr, 16)

    return pl.pallas_call(
        body, name="norm_pre0", grid=(S // ts,),
        in_specs=[pl.BlockSpec((ts, D), lambda i: (i, 0)), pl.BlockSpec((1, D), lambda i: (0, 0)), ANY],
        out_specs=[pl.BlockSpec((ts, D), lambda i: (i, 0)),
                   pl.BlockSpec((4, ts // 4, D), lambda i: (0, i, 0)),
                   pl.BlockSpec((16, ts // 16, D), lambda i: (0, i, 0)),
                   pl.BlockSpec((D, ts), lambda i: (0, i))],
        out_shape=[SDS((S, D), BF16), SDS((4, S // 4, D), BF16), SDS((16, S // 16, D), BF16), SDS((D, S), BF16)],
        scratch_shapes=[_scr(ts, D)],
        compiler_params=_cp(("parallel",), 32),
    )(x, g, after)


def transpose_rows(a, name):
    S, C = a.shape
    ts = 512

    def body(a_ref, o_ref):
        o_ref[...] = a_ref[...].astype(F32).T.astype(BF16)

    return pl.pallas_call(
        body, name=name, grid=(S // ts,),
        in_specs=[pl.BlockSpec((ts, C), lambda i: (i, 0))],
        out_specs=pl.BlockSpec((C, ts), lambda i: (0, i)),
        out_shape=SDS((C, S), BF16),
        compiler_params=_cp(("parallel",), 32),
    )(a)


def post0_pre1(x, a0, g_post, g_pre):
    S = x.shape[0]
    ts = 512

    def body(x_ref, a_ref, gp_ref, gn_ref, h_ref, o_ref, ot_ref):
        a = a_ref[...]
        h1 = x_ref[...] + a * _rms(a) * gp_ref[...]
        h_ref[...] = h1
        xn = h1 * _rms(h1) * gn_ref[...]
        o_ref[...] = xn.astype(BF16)
        ot_ref[...] = xn.T.astype(BF16)

    row = pl.BlockSpec((ts, D), lambda i: (i, 0))
    vec = pl.BlockSpec((1, D), lambda i: (0, 0))
    return pl.pallas_call(
        body, name="post0_pre1", grid=(S // ts,),
        in_specs=[row, row, vec, vec],
        out_specs=[row, row, pl.BlockSpec((D, ts), lambda i: (0, i))],
        out_shape=[SDS((S, D), F32), SDS((S, D), BF16), SDS((D, S), BF16)],
        compiler_params=_cp(("parallel",), 40),
    )(x, a0, g_post, g_pre)


def post1_loss(h1, a1, target, g_post):
    S = h1.shape[0]
    ts = 512

    def body(h_ref, a_ref, t_ref, g_ref, dh_ref, da_ref, loss_ref, dg_ref):
        @pl.when(pl.program_id(0) == 0)
        def _():
            loss_ref[...] = jnp.zeros_like(loss_ref)
            dg_ref[...] = jnp.zeros_like(dg_ref)

        a = a_ref[...]
        g = g_ref[...]
        rp = _rms(a)
        yhat = a * rp
        e = h_ref[...] + yhat * g - t_ref[...]
        loss_ref[...] += _fold8(e * e)
        dh = e * (1.0 / D)
        dh_ref[...] = dh
        dg_ref[...] += _fold8(dh * yhat)
        dyh = dh * g
        da = rp * (dyh - yhat * jnp.mean(dyh * yhat, axis=-1, keepdims=True))
        da_ref[...] = da.astype(BF16)

    row = pl.BlockSpec((ts, D), lambda i: (i, 0))
    acc = pl.BlockSpec((8, D), lambda i: (0, 0))
    return pl.pallas_call(
        body, name="post1_loss", grid=(S // ts,),
        in_specs=[row, row, row, pl.BlockSpec((1, D), lambda i: (0, 0))],
        out_specs=[row, row, acc, acc],
        out_shape=[SDS((S, D), F32), SDS((S, D), BF16), SDS((8, D), F32), SDS((8, D), F32)],
        compiler_params=_cp(("arbitrary",), 40),
    )(h1, a1, target, g_post)


def mid_bwd(dxn1, dh2, h1, a0, g_pre1, g_post0):
    S = h1.shape[0]
    ts = 512

    def body(dx_ref, dh2_ref, h_ref, a_ref, gn_ref, gp_ref, dh1_ref, da_ref, dgn_ref, dgp_ref):
        @pl.when(pl.program_id(0) == 0)
        def _():
            dgn_ref[...] = jnp.zeros_like(dgn_ref)
            dgp_ref[...] = jnp.zeros_like(dgp_ref)

        h = h_ref[...]
        r1 = _rms(h)
        xhat = h * r1
        dxn = dx_ref[...]
        dgn_ref[...] += _fold8(dxn * xhat)
        dxh = dxn * gn_ref[...]
        dh1 = dh2_ref[...] + r1 * (dxh - xhat * jnp.mean(dxh * xhat, axis=-1, keepdims=True))
        dh1_ref[...] = dh1
        a = a_ref[...]
        rp = _rms(a)
        yhat = a * rp
        dgp_ref[...] += _fold8(dh1 * yhat)
        dyh = dh1 * gp_ref[...]
        da = rp * (dyh - yhat * jnp.mean(dyh * yhat, axis=-1, keepdims=True))
        da_ref[...] = da.astype(BF16)

    row = pl.BlockSpec((ts, D), lambda i: (i, 0))
    vec = pl.BlockSpec((1, D), lambda i: (0, 0))
    acc = pl.BlockSpec((8, D), lambda i: (0, 0))
    return pl.pallas_call(
        body, name="mid_bwd", grid=(S // ts,),
        in_specs=[row, row, row, row, vec, vec],
        out_specs=[row, row, acc, acc],
        out_shape=[SDS((S, D), F32), SDS((S, D), BF16), SDS((8, D), F32), SDS((8, D), F32)],
        compiler_params=_cp(("arbitrary",), 48),
    )(dxn1, dh2, h1, a0, g_pre1, g_post0)


def pre0_bwd(dx_tok, dx_z, dx4, dx16, dh1, x, g_pre0):
    S = x.shape[0]
    ts = 512

    def body(da_ref, dz_ref, d4_ref, d16_ref, dh_ref, x_ref, g_ref, gx_ref, dg_ref, scr):
        @pl.when(pl.program_id(0) == 0)
        def _():
            dg_ref[...] = jnp.zeros_like(dg_ref)

        _scr_put(scr, da_ref[...] + dz_ref[...])
        _load_perm(scr, d4_ref, 4, add=True)
        _load_perm(scr, d16_ref, 16, add=True)
        h = x_ref[...]
        r = _rms(h)
        xhat = h * r
        dxn = _scr_get(scr)
        dg_ref[...] += _fold8(dxn * xhat)
        dxh = dxn * g_ref[...]
        gx_ref[...] = dh_ref[...] + r * (dxh - xhat * jnp.mean(dxh * xhat, axis=-1, keepdims=True))

    row = pl.BlockSpec((ts, D), lambda i: (i, 0))
    return pl.pallas_call(
        body, name="pre0_bwd", grid=(S // ts,),
        in_specs=[row, row, pl.BlockSpec((4, ts // 4, D), lambda i: (0, i, 0)),
                  pl.BlockSpec((16, ts // 16, D), lambda i: (0, i, 0)), row, row,
                  pl.BlockSpec((1, D), lambda i: (0, 0))],
        out_specs=[row, pl.BlockSpec((8, D), lambda i: (0, 0))],
        out_shape=[SDS((S, D), F32), SDS((8, D), F32)],
        scratch_shapes=[_scr(ts, D)],
        compiler_params=_cp(("arbitrary",), 48),
    )(dx_tok, dx_z, dx4.reshape(4, S // 4, D), dx16.reshape(16, S // 16, D), dh1, x, g_pre0)


def _w_tile(tile0):
    per = W_SHARD // CT
    return lambda t: ((tile0 + t) // per, 0, (tile0 + t) % per)


def mm_in(xn, w8, tile0, ntiles, tabs, name, after=None):
    S = xn.shape[0]
    tm = 2048
    wt = _w_tile(tile0)

    def body(a_ref, b_ref, *rest):
        o_ref = rest[-1]
        rc = 512
        for u in range(tm // rc):
            rows = slice(u * rc, (u + 1) * rc)
            r = _dot(a_ref[rows, :], b_ref[...])
            if tabs is None:
                o_ref[rows, :] = r.astype(BF16)
                continue
            c_ref, s1_ref, s2_ref = rest[:3]
            rot = pl.program_id(1) < 2 * E // CT
            qs = jnp.where(pl.program_id(1) < E // CT, SCALE, 1.0)
            c = jnp.where(rot, (c_ref[rows, :] * qs).astype(BF16), 1.0)
            s1 = jnp.where(rot, (s1_ref[rows, :] * qs).astype(BF16), 0.0)
            s2 = jnp.where(rot, (s2_ref[rows, :] * qs).astype(BF16), 0.0)
            for hh in range(CT // HD):
                cs = slice(hh * HD, (hh + 1) * HD)
                o_ref[rows, cs] = _rope(r[:, cs], c, s1, s2).astype(BF16)

    tab = pl.BlockSpec((tm, HD), lambda i, t: (i, 0))
    return pl.pallas_call(
        body, name=name, grid=(S // tm, ntiles),
        in_specs=[pl.BlockSpec((tm, D), lambda i, t: (i, 0)),
                  pl.BlockSpec((None, D, CT), lambda i, t: wt(t))] + ([] if tabs is None else [tab] * 3)
        + ([] if after is None else [ANY]),
        out_specs=pl.BlockSpec((tm, CT), lambda i, t: (i, t)),
        out_shape=SDS((S, ntiles * CT), BF16),
        compiler_params=_cp(("parallel", "parallel"), 48),
    )(xn, w8, *(() if tabs is None else tabs), *(() if after is None else (after,)))


def mm_rows(a, b, name, out_dtype, tm=1024):
    M, K = a.shape
    N = b.shape[1]

    def body(a_ref, b_ref, o_ref):
        for cidx in range(N // 256):
            col = slice(cidx * 256, (cidx + 1) * 256)
            o_ref[:, col] = _dot(a_ref[...], b_ref[:, col]).astype(out_dtype)

    return pl.pallas_call(
        body, name=name, grid=(M // tm,),
        in_specs=[pl.BlockSpec((tm, K), lambda i: (i, 0)), pl.BlockSpec((K, N), lambda i: (0, 0))],
        out_specs=pl.BlockSpec((tm, N), lambda i: (i, 0)),
        out_shape=SDS((M, N), out_dtype),
        compiler_params=_cp(("parallel",), 48),
    )(a, b)


def mm_acc(a, b, name, *, grid, a_spec, b_spec, o_spec, o_shape, acc_shape, write, vmem=48):
    nk = grid[-1]

    def body(a_ref, b_ref, o_ref, acc_ref):
        k = pl.program_id(len(grid) - 1)

        @pl.when(k == 0)
        def _():
            acc_ref[...] = jnp.zeros_like(acc_ref)

        acc_ref[...] += _dot(a_ref[...], b_ref[...])

        @pl.when(k == nk - 1)
        def _():
            write(o_ref, acc_ref)

    return pl.pallas_call(
        body, name=name, grid=grid, in_specs=[a_spec, b_spec], out_specs=o_spec, out_shape=o_shape,
        scratch_shapes=[pltpu.VMEM(acc_shape, F32)],
        compiler_params=_cp(("parallel",) * (len(grid) - 1) + ("arbitrary",), vmem),
    )(a, b)


def _write_plain(o_ref, acc_ref):
    o_ref[...] = acc_ref[...]


def mm_wgrad_rows(at, b, name):
    M, S = at.shape
    N = b.shape[1]
    tm, tk = 1024, 1024
    return mm_acc(at, b, name, grid=(M // tm, S // tk),
                  a_spec=pl.BlockSpec((tm, tk), lambda i, k: (i, k)),
                  b_spec=pl.BlockSpec((tk, N), lambda i, k: (k, 0)),
                  o_spec=pl.BlockSpec((tm, N), lambda i, k: (i, 0)),
                  o_shape=SDS((M, N), F32), acc_shape=(tm, N), write=_write_plain)


def mm_wgrad_cols(at, b, name, *, shard):
    M, S = at.shape
    tk = 1024
    nb = 2
    tn = nb * shard

    def write(o_ref, acc_ref):
        for u in range(nb):
            o_ref[u] = acc_ref[:, u * shard:(u + 1) * shard]

    return mm_acc(at, b, name, grid=(N_DEV // nb, S // tk),
                  a_spec=pl.BlockSpec((M, tk), lambda t, k: (0, k)),
                  b_spec=pl.BlockSpec((tk, tn), lambda t, k: (k, t)),
                  o_spec=pl.BlockSpec((nb, M, shard), lambda t, k: (t, 0, 0)),
                  o_shape=SDS((N_DEV, M, shard), F32), acc_shape=(M, tn), write=write)


def mm_dwg(pooled_t, dh):
    S = dh.shape[0]
    tk = 2048
    return mm_acc(pooled_t, dh, "mm_dwg", grid=(4, S // tk),
                  a_spec=pl.BlockSpec((PC, tk), lambda g, k: (g, k)),
                  b_spec=pl.BlockSpec((tk, PC), lambda g, k: (k, g)),
                  o_spec=pl.BlockSpec((None, PC, PC), lambda g, k: (g, 0, 0)),
                  o_shape=SDS((4, PC, PC), F32), acc_shape=(PC, PC), write=_write_plain)


def mm_dx_full(da, w, name):
    S, K = da.shape
    N = w.shape[0]
    tm = 512

    def body(a_ref, b_ref, o_ref):
        o_ref[...] = _dot_nt(a_ref[...], b_ref[...])

    return pl.pallas_call(
        body, name=name, grid=(S // tm,),
        in_specs=[pl.BlockSpec((tm, K), lambda i: (i, 0)), pl.BlockSpec((N, K), lambda i: (0, 0))],
        out_specs=pl.BlockSpec((tm, N), lambda i: (i, 0)),
        out_shape=SDS((S, N), F32),
        compiler_params=_cp(("parallel",), 48),
    )(da, w)


def mm_dw_in_part(at, b, tile0, prev, name):
    M, S = at.shape
    ntiles = b.shape[1] // CT
    tk = 2048
    nk = S // tk
    per = W_SHARD // CT

    def body(a_ref, b_ref, *rest):
        o_ref, acc_ref, sems = rest[-3:]
        k, t = pl.program_id(0), pl.program_id(1)

        @pl.when(k == 0)
        def _():
            acc_ref[t] = _dot(a_ref[...], b_ref[...])

        @pl.when(k > 0)
        def _():
            acc_ref[t] += _dot(a_ref[...], b_ref[...])

        def out_copy(u):
            tile = tile0 + u
            off = (tile % per) * CT
            if not isinstance(off, int):
                off = pl.multiple_of(off, CT)
            return pltpu.make_async_copy(acc_ref.at[u], o_ref.at[tile // per, :, pl.ds(off, CT)], sems.at[u])

        @pl.when(k == nk - 1)
        def _():
            out_copy(t).start()

        @pl.when(jnp.logical_and(k == nk - 1, t == ntiles - 1))
        def _():
            for u in range(ntiles):
                out_copy(u).wait()

    return pl.pallas_call(
        body, name=name, grid=(nk, ntiles),
        in_specs=[pl.BlockSpec((M, tk), lambda k, t: (0, k)), pl.BlockSpec((tk, CT), lambda k, t: (k, t))]
        + ([] if prev is None else [ANY]),
        out_specs=ANY,
        out_shape=SDS((N_DEV, M, W_SHARD), F32),
        scratch_shapes=[pltpu.VMEM((ntiles, M, CT), F32), pltpu.SemaphoreType.DMA((ntiles,))],
        input_output_aliases={} if prev is None else {2: 0},
        compiler_params=_cp(("arbitrary", "arbitrary"), 48),
    )(at, b, *(() if prev is None else (prev,)))


def mm_dx_part(da, w8, tile0, name, after=None):
    S = da.shape[0]
    npairs = da.shape[1] // (2 * CT)
    tm = 2048
    wt = _w_tile(tile0)

    def body(a_ref, b0_ref, b1_ref, *rest):
        o_ref, acc_ref = rest[-2:]
        t = pl.program_id(1)
        part = _dot_nt(a_ref[:, :CT], b0_ref[...]) + _dot_nt(a_ref[:, CT:], b1_ref[...])

        @pl.when(t == 0)
        def _():
            acc_ref[...] = part

        @pl.when(t > 0)
        def _():
            acc_ref[...] += part

        @pl.when(t == npairs - 1)
        def _():
            o_ref[...] = acc_ref[...]

    return pl.pallas_call(
        body, name=name, grid=(S // tm, npairs),
        in_specs=[pl.BlockSpec((tm, 2 * CT), lambda i, t: (i, t)),
                  pl.BlockSpec((None, D, CT), lambda i, t: wt(2 * t)),
                  pl.BlockSpec((None, D, CT), lambda i, t: wt(2 * t + 1))]
        + ([] if after is None else [ANY]),
        out_specs=pl.BlockSpec((tm, D), lambda i, t: (i, 0)),
        out_shape=SDS((S, D), F32),
        scratch_shapes=[pltpu.VMEM((tm, D), F32)],
        compiler_params=_cp(("parallel", "arbitrary"), 56),
    )(da, w8, w8, *(() if after is None else (after,)))


HEADS_FWD = 4
HEADS_BWD = 2
AHEAD = 2


def _band_masks(not_first):
    row = lax.broadcasted_iota(jnp.int32, (QB, QB), 0)
    col = lax.broadcasted_iota(jnp.int32, (QB, QB), 1)
    cur = jnp.where(col <= row, 0.0, NEG)
    prev = jnp.where(col >= row, 0.0, NEG)
    first = jnp.where(jnp.logical_and(col >= row, not_first), 0.0, NEG)
    return col, jnp.concatenate([prev, cur], axis=1), jnp.concatenate([first, cur], axis=1)


def _fill_kv(ext, qkv_ref, kh_ref, vh_ref):
    ext[0:QB, 0:E] = kh_ref[...]
    ext[0:QB, E:2 * E] = vh_ref[...]
    ext[QB:, :] = qkv_ref[:, E:3 * E]


def attn_fwd(P, g, d):
    S = P.shape[0]
    L = S // d
    T = min(512, L)
    nq = T // QB
    ni = L // T

    def body(qkv_ref, kh_ref, vh_ref, o_ref, lse_ref, ext):
        col, mask, mask_first = _band_masks(pl.program_id(1) > 0)
        lse_ref[...] = jnp.zeros_like(lse_ref)
        _fill_kv(ext, qkv_ref, kh_ref, vh_ref)

        def heads(hp, carry):
            def front(h, j):
                cq = pl.ds(pl.multiple_of(h * HD, HD), HD)
                rows = slice(j * QB, (j + 1) * QB)
                krows = slice(j * QB, (j + 2) * QB)
                s = _dot_nt(qkv_ref[rows, cq], ext[krows, cq]) + (mask_first if j == 0 else mask)
                m = jnp.max(s, axis=1, keepdims=True)
                p = jnp.exp(s - m)
                den = jnp.sum(p, axis=1, keepdims=True)
                lse_ref[rows, :] = jnp.where(col == h, m + jnp.log(den), lse_ref[rows, :])
                return p.astype(BF16), den

            def back(h, j, p, den):
                off = pl.multiple_of(h * HD, HD)
                rows = slice(j * QB, (j + 1) * QB)
                krows = slice(j * QB, (j + 2) * QB)
                o_ref[rows, pl.ds(off, HD)] = (_dot(p, ext[krows, pl.ds(E + off, HD)]) / den).astype(BF16)

            items = [(HEADS_FWD * hp + hh, j) for hh in range(HEADS_FWD) for j in range(nq)]
            queue = [front(*it) for it in items[:AHEAD]]
            for u, it in enumerate(items):
                if u + AHEAD < len(items):
                    queue.append(front(*items[u + AHEAD]))
                back(*it, *queue.pop(0))
            return carry

        lax.fori_loop(0, NH // HEADS_FWD, heads, 0)

    halo = lambda r, i: jnp.maximum(r * (L // QB) + i * nq - 1, 0)
    return pl.pallas_call(
        body, name=f"attn_fwd{g}", grid=(d, ni),
        in_specs=[pl.BlockSpec((T, SEG), lambda r, i: (r * ni + i, 0)),
                  pl.BlockSpec((QB, E), lambda r, i: (halo(r, i), 1)),
                  pl.BlockSpec((QB, E), lambda r, i: (halo(r, i), 2))],
        out_specs=[pl.BlockSpec((T, E), lambda r, i: (r * ni + i, 0)),
                   pl.BlockSpec((T, HD), lambda r, i: (r * ni + i, 0))],
        out_shape=[SDS((S, E), BF16), SDS((S, HD), F32)],
        scratch_shapes=[pltpu.VMEM((T + QB, 2 * E), BF16)],
        compiler_params=_cp(("parallel", "parallel"), 48),
    )(P, P, P)


def _perm_specs(ts, C):
    return [pl.BlockSpec((ts, C), lambda i: (i, 0)),
            pl.BlockSpec((4, ts // 4, C), lambda i: (0, i, 0)),
            pl.BlockSpec((16, ts // 16, C), lambda i: (0, i, 0))]


def _perm_shapes(S, C, dtype):
    return [SDS((S, C), dtype), SDS((4, S // 4, C), dtype), SDS((16, S // 16, C), dtype)]


def combine_fwd(os_, lses, z, ehot):
    S = z.shape[0]
    ts = 256

    def body(o0, o1, o2, l0, l1, l2, z_ref, e_ref, y0, y1, y2, s0, s1, s2, ya_ref, yat_ref,
             so1, so2, sl1, sl2, sy, sl):
        _load_perm(so1, o1, 4)
        _load_perm(so2, o2, 16)
        _load_perm(sl1, l1, 4)
        _load_perm(sl2, l2, 16)
        ls = [l0[...], sl1[0], sl2[0]]
        m = jnp.maximum(jnp.maximum(ls[0], ls[1]), ls[2])
        es = [jnp.exp(l - m) for l in ls]
        den = es[0] + es[1] + es[2]
        sl[0] = m + jnp.log(den)
        y = None
        for e, o in zip(es, (o0[...].astype(F32), _scr_get(so1), _scr_get(so2))):
            w = e / den
            hi = w.astype(BF16)
            lo = (w - hi.astype(F32)).astype(BF16)
            wb = _dot(hi, e_ref[...]) + _dot(lo, e_ref[...])
            y = wb * o if y is None else y + wb * o
        z = z_ref[...].astype(F32)
        ya = y * (z * _sigmoid(z))
        ya_ref[...] = ya.astype(BF16)
        yat_ref[...] = ya.T.astype(BF16)
        _scr_put(sy, y)
        y0[...] = y.astype(BF16)
        _store_perm(y1, sy, 4)
        _store_perm(y2, sy, 16)
        s0[...] = sl[0]
        _store_perm(s1, sl, 4)
        _store_perm(s2, sl, 16)

    wide = pl.BlockSpec((ts, E), lambda i: (i, 0))
    os3 = [os_[0], os_[1].reshape(4, S // 4, E), os_[2].reshape(16, S // 16, E)]
    ls3 = [lses[0], lses[1].reshape(4, S // 4, HD), lses[2].reshape(16, S // 16, HD)]
    res = pl.pallas_call(
        body, name="combine_fwd", grid=(S // ts,),
        in_specs=_perm_specs(ts, E) + _perm_specs(ts, HD) + [wide, pl.BlockSpec((HD, E), lambda i: (0, 0))],
        out_specs=_perm_specs(ts, E) + _perm_specs(ts, HD) + [wide, pl.BlockSpec((E, ts), lambda i: (0, i))],
        out_shape=_perm_shapes(S, E, BF16) + _perm_shapes(S, HD, F32) + [SDS((S, E), BF16), SDS((E, S), BF16)],
        scratch_shapes=[_scr(ts, E), _scr(ts, E), _scr(ts, HD), _scr(ts, HD), _scr(ts, E), _scr(ts, HD)],
        compiler_params=_cp(("parallel",), 56),
    )(*os3, *ls3, z, ehot)
    ys = [res[0], res[1].reshape(S, E), res[2].reshape(S, E)]
    lse3 = [res[3], res[4].reshape(S, HD), res[5].reshape(S, HD)]
    return ys, lse3, res[6], res[7]


def mm_dya(da0, w_out, z, y):
    S = da0.shape[0]
    tm = 512

    def body(a_ref, w_ref, z_ref, y_ref, dy0, dy1, dy2, dz_ref, scr):
        for cidx in range(E // 256):
            col = slice(cidx * 256, (cidx + 1) * 256)
            dya = _dot_nt(a_ref[...], w_ref[col, :])
            zz = z_ref[:, col].astype(F32)
            sig = _sigmoid(zz)
            dy = dya * zz * sig
            scr[2 * cidx] = dy[:, :LANES]
            scr[2 * cidx + 1] = dy[:, LANES:]
            dy0[:, col] = dy.astype(BF16)
            dz_ref[:, col] = (dya * y_ref[:, col].astype(F32) * sig * (1.0 + zz * (1.0 - sig))).astype(BF16)
        _store_perm(dy1, scr, 4)
        _store_perm(dy2, scr, 16)

    wide = pl.BlockSpec((tm, E), lambda i: (i, 0))
    res = pl.pallas_call(
        body, name="mm_dya", grid=(S // tm,),
        in_specs=[pl.BlockSpec((tm, D), lambda i: (i, 0)), pl.BlockSpec((E, D), lambda i: (0, 0)), wide, wide],
        out_specs=_perm_specs(tm, E) + [wide],
        out_shape=_perm_shapes(S, E, BF16) + [SDS((S, E), BF16)],
        scratch_shapes=[_scr(tm, E)],
        compiler_params=_cp(("parallel",), 48),
    )(da0, w_out, z, y)
    return [res[0], res[1].reshape(S, E), res[2].reshape(S, E)], res[3]


def attn_bwd(P, dy, y, lse, tabs, g, d):
    S = P.shape[0]
    L = S // d
    T = min(512, L)
    nq = T // QB
    ni = L // T

    def body(qkv_ref, kh_ref, vh_ref, dy_ref, y_ref, lse_ref, c_ref, s1_ref, s2_ref,
             o_ref, dkc_ref, dvc_ref, ext):
        i = pl.program_id(1)
        _, mask, mask_first = _band_masks(i < ni - 1)
        _fill_kv(ext, qkv_ref, kh_ref, vh_ref)
        row_id2 = lax.broadcasted_iota(jnp.int32, (2 * QB, QB), 0)
        ones = jnp.ones((QB, QB), BF16)
        lse_hl = []
        for j in range(nq):
            t = lse_ref[j * QB:(j + 1) * QB, :]
            hi = t.astype(BF16)
            lse_hl.append(jnp.concatenate([hi, (t - hi.astype(F32)).astype(BF16)], axis=1))

        @pl.when(i == 0)
        def _():
            dkc_ref[...] = jnp.zeros_like(dkc_ref)
            dvc_ref[...] = jnp.zeros_like(dvc_ref)

        def heads(hp, carry):
            def cols(h):
                off = pl.multiple_of(h * HD, HD)
                return pl.ds(off, HD), pl.ds(E + off, HD), pl.ds(2 * E + off, HD)

            def front(h, j):
                cq, ck, _ = cols(h)
                rows = slice(j * QB, (j + 1) * QB)
                krows = slice(j * QB, (j + 2) * QB)
                dyj = dy_ref[rows, cq]
                sel = jnp.logical_or(row_id2 == h, row_id2 == h + QB).astype(BF16)
                lse_b = _dot(lse_hl[j], sel)
                delta_b = _dot((dyj.astype(F32) * y_ref[rows, cq].astype(F32)).astype(BF16), ones)
                s = _dot_nt(qkv_ref[rows, cq], ext[krows, cq])
                p = jnp.exp(s + (mask_first if j == 0 else mask) - jnp.concatenate([lse_b, lse_b], axis=1))
                ds = (p * (_dot_nt(dyj, ext[krows, ck]) - jnp.concatenate([delta_b, delta_b], axis=1))).astype(BF16)
                return ds, jnp.concatenate([ds, p.astype(BF16)], axis=0).T

            def back(h, j, ds, dsp_t, pend_dk, pend_dv):
                cq, ck, cv = cols(h)
                rows = slice(j * QB, (j + 1) * QB)
                krows = slice(j * QB, (j + 2) * QB)
                dq = _dot(ds, ext[krows, cq]) * SCALE
                zero = jnp.zeros((QB, HD), BF16)
                bd = jnp.concatenate([jnp.concatenate([qkv_ref[rows, cq], zero], axis=1),
                                      jnp.concatenate([zero, dy_ref[rows, cq]], axis=1)], axis=0)
                dkv = _dot(dsp_t, bd)
                dk2, dv2 = dkv[:, :HD], dkv[:, HD:]
                c, s1, s2 = c_ref[rows, :], s1_ref[rows, :], s2_ref[rows, :]
                o_ref[rows, cq] = _unrope(dq, c, s1, s2).astype(BF16)
                o_ref[rows, ck] = _unrope(dk2[QB:] + pend_dk, c, s1, s2).astype(BF16)
                o_ref[rows, cv] = (dv2[QB:] + pend_dv).astype(BF16)
                return dk2[:QB], dv2[:QB]

            items = [(HEADS_BWD * hp + hh, j) for hh in range(HEADS_BWD) for j in reversed(range(nq))]
            queue = [front(*it) for it in items[:AHEAD]]
            pend = None
            for u, (h, j) in enumerate(items):
                if u + AHEAD < len(items):
                    queue.append(front(*items[u + AHEAD]))
                if j == nq - 1:
                    pend = (dkc_ref[:, cols(h)[0]], dvc_ref[:, cols(h)[0]])
                pend = back(h, j, *queue.pop(0), *pend)
                if j == 0:
                    dkc_ref[:, cols(h)[0]], dvc_ref[:, cols(h)[0]] = pend
            return carry

        lax.fori_loop(0, NH // HEADS_BWD, heads, 0)

    blk = lambda r, i: r * ni + ni - 1 - i
    halo = lambda r, i: jnp.maximum(r * (L // QB) + (ni - 1 - i) * nq - 1, 0)
    main = pl.BlockSpec((T, SEG), lambda r, i: (blk(r, i), 0))
    wide = pl.BlockSpec((T, E), lambda r, i: (blk(r, i), 0))
    narrow = pl.BlockSpec((T, HD), lambda r, i: (blk(r, i), 0))
    return pl.pallas_call(
        body, name=f"attn_bwd{g}", grid=(d, ni),
        in_specs=[main, pl.BlockSpec((QB, E), lambda r, i: (halo(r, i), 1)),
                  pl.BlockSpec((QB, E), lambda r, i: (halo(r, i), 2)),
                  wide, wide, narrow, narrow, narrow, narrow],
        out_specs=main, out_shape=SDS((S, SEG), BF16),
        scratch_shapes=[pltpu.VMEM((QB, E), F32), pltpu.VMEM((QB, E), F32), pltpu.VMEM((T + QB, 2 * E), BF16)],
        compiler_params=_cp(("arbitrary", "arbitrary"), 56),
    )(P, P, P, dy, y, lse, *tabs)


def _pool_cnt(t0, rows):
    t = (lax.broadcasted_iota(jnp.int32, (rows, E), 0) + t0 + 1).astype(F32)
    ch = lax.broadcasted_iota(jnp.int32, (rows, E), 1)
    w = jnp.where(ch < PC, 2.0, jnp.where(ch < 2 * PC, 4.0, jnp.where(ch < 3 * PC, 8.0, 16.0)))
    return jnp.minimum(t, w)


def _by_group(parts):
    return jnp.concatenate([parts[g][:, g * PC:(g + 1) * PC] for g in range(4)], axis=1)


def pool_fwd(uz):
    S = uz.shape[0]
    ts = 256

    def body(u_ref, h_ref, o_ref, ot_ref):
        i = pl.program_id(0)
        u = u_ref[...]
        halo = jnp.where(i > 0, h_ref[...], 0.0)
        ext = jnp.concatenate([halo, u], axis=0)
        s2 = ext + pltpu.roll(ext, 1, 0)
        s4 = s2 + pltpu.roll(s2, 2, 0)
        s8 = s4 + pltpu.roll(s4, 4, 0)
        s16 = s8 + pltpu.roll(s8, 8, 0)
        win = _by_group([s2, s4, s8, s16])[16:, :]
        pooled = win / _pool_cnt(i * ts, ts) - u
        o_ref[...] = pooled.astype(BF16)
        ot_ref[...] = pooled.T.astype(BF16)

    return pl.pallas_call(
        body, name="pool_fwd", grid=(S // ts,),
        in_specs=[pl.BlockSpec((ts, E), lambda i: (i, 0)),
                  pl.BlockSpec((16, E), lambda i: (jnp.maximum(i * (ts // 16) - 1, 0), 0))],
        out_specs=[pl.BlockSpec((ts, E), lambda i: (i, 0)), pl.BlockSpec((E, ts), lambda i: (0, i))],
        out_shape=[SDS((S, E), BF16), SDS((E, S), BF16)],
        compiler_params=_cp(("parallel",), 48),
    )(uz, uz)


def pool_bwd(dpooled, duz):
    S = dpooled.shape[0]
    ts = 256
    nt = S // ts

    def body(d_ref, h_ref, alias_ref, o_ref):
        i = pl.program_id(0)
        dp = d_ref[...].astype(F32)
        halo = jnp.where(i < nt - 1, h_ref[...].astype(F32), 0.0)
        n = ts + 16
        ext = jnp.concatenate([dp, halo], axis=0) / _pool_cnt(i * ts, n)
        f2 = ext + pltpu.roll(ext, n - 1, 0)
        f4 = f2 + pltpu.roll(f2, n - 2, 0)
        f8 = f4 + pltpu.roll(f4, n - 4, 0)
        f16 = f8 + pltpu.roll(f8, n - 8, 0)
        win = _by_group([f2, f4, f8, f16])[:ts, :]
        o_ref[...] = (win - dp).astype(BF16)

    return pl.pallas_call(
        body, name="pool_bwd", grid=(nt,),
        in_specs=[pl.BlockSpec((ts, E), lambda i: (i, 0)),
                  pl.BlockSpec((16, E), lambda i: (jnp.minimum((i + 1) * (ts // 16), S // 16 - 1), 0)), ANY],
        out_specs=pl.BlockSpec((ts, E), lambda i: (i, 0)),
        out_shape=SDS(duz.shape, BF16),
        input_output_aliases={2: 0},
        compiler_params=_cp(("parallel",), 48),
    )(dpooled, dpooled, duz)


def mm_grp(pooled, wg, b, scale, uz):
    S = pooled.shape[0]
    tm = 512

    def body(p_ref, w_ref, b_ref, s_ref, z_ref, h_ref, y_ref, yt_ref):
        for g in range(4):
            cs = slice(g * PC, (g + 1) * PC)
            h = _dot(p_ref[:, cs], w_ref[g]) + b_ref[:, cs]
            z = z_ref[:, cs]
            yp = h * s_ref[:, cs] * (z * _sigmoid(z))
            h_ref[:, cs] = h.astype(BF16)
            y_ref[:, cs] = yp.astype(BF16)
            yt_ref[cs, :] = yp.T.astype(BF16)

    row = pl.BlockSpec((tm, E), lambda i: (i, 0))
    vec = pl.BlockSpec((1, E), lambda i: (0, 0))
    return pl.pallas_call(
        body, name="mm_grp", grid=(S // tm,),
        in_specs=[row, pl.BlockSpec((4, PC, PC), lambda i: (0, 0, 0)), vec, vec,
                  pl.BlockSpec((tm, E), lambda i: (i, 1))],
        out_specs=[row, row, pl.BlockSpec((E, tm), lambda i: (0, i))],
        out_shape=[SDS((S, E), BF16), SDS((S, E), BF16), SDS((E, S), BF16)],
        compiler_params=_cp(("parallel",), 48),
    )(pooled, wg, b, scale, uz)


def mm_dyp(da1, w_out, uz, h, scale):
    S = da1.shape[0]
    tm = 512

    def body(a_ref, w_ref, z_ref, h_ref, s_ref, dh_ref, dz_ref, dsc_ref, db_ref):
        @pl.when(pl.program_id(0) == 0)
        def _():
            dsc_ref[...] = jnp.zeros_like(dsc_ref)
            db_ref[...] = jnp.zeros_like(db_ref)

        for cidx in range(E // 256):
            col = slice(cidx * 256, (cidx + 1) * 256)
            dyp = _dot_nt(a_ref[...], w_ref[col, :])
            z = z_ref[:, col]
            hh = h_ref[:, col].astype(F32)
            sc = s_ref[:, col]
            sig = _sigmoid(z)
            dhs = dyp * z * sig
            dz_ref[:, col] = (dyp * hh * sc * sig * (1.0 + z * (1.0 - sig))).astype(BF16)
            dh = dhs * sc
            dh_ref[:, col] = dh.astype(BF16)
            dsc_ref[:, col] += _fold8(dhs * hh)
            db_ref[:, col] += _fold8(dh)

    row = pl.BlockSpec((tm, E), lambda i: (i, 0))
    acc = pl.BlockSpec((8, E), lambda i: (0, 0))
    return pl.pallas_call(
        body, name="mm_dyp", grid=(S // tm,),
        in_specs=[pl.BlockSpec((tm, D), lambda i: (i, 0)), pl.BlockSpec((E, D), lambda i: (0, 0)),
                  pl.BlockSpec((tm, E), lambda i: (i, 1)), row, pl.BlockSpec((1, E), lambda i: (0, 0))],
        out_specs=[row, pl.BlockSpec((tm, E), lambda i: (i, 1)), acc, acc],
        out_shape=[SDS((S, E), BF16), SDS((S, 2 * E), BF16), SDS((8, E), F32), SDS((8, E), F32)],
        compiler_params=_cp(("arbitrary",), 48),
    )(da1, w_out, uz, h, scale)


def mm_dpooled(dh, wg):
    S = dh.shape[0]
    tm = 1024

    def body(a_ref, w_ref, o_ref):
        for g in range(4):
            cs = slice(g * PC, (g + 1) * PC)
            o_ref[:, cs] = _dot_nt(a_ref[:, cs], w_ref[g]).astype(BF16)

    row = pl.BlockSpec((tm, E), lambda i: (i, 0))
    return pl.pallas_call(
        body, name="mm_dpooled", grid=(S // tm,),
        in_specs=[row, pl.BlockSpec((4, PC, PC), lambda i: (0, 0, 0))],
        out_specs=row, out_shape=SDS((S, E), BF16),
        compiler_params=_cp(("parallel",), 48),
    )(dh, wg)


def _rope_tables(positions):
    inv_freq = 500000.0 ** (-jnp.arange(0, 32, 2, dtype=F32) / 32)
    S = positions.shape[0]
    ang = jnp.repeat(positions.astype(F32).reshape(S // 8, 8), 16, axis=1) * jnp.tile(inv_freq, 8)
    cos, sin = lax.optimization_barrier((jnp.cos(ang), jnp.sin(ang)))
    cos, sin = cos.reshape(S, 16), sin.reshape(S, 16)
    one = jnp.ones((S, HD - 32), F32)
    zero16 = jnp.zeros((S, 16), F32)
    zero = jnp.zeros((S, HD - 32), F32)
    c = jnp.concatenate([cos, cos, one], axis=1)
    s1 = jnp.concatenate([-sin, zero16, zero], axis=1)
    s2 = jnp.concatenate([zero16, sin, zero], axis=1)
    return c.astype(BF16), s1.astype(BF16), s2.astype(BF16)


def kernel(x, positions, norm_pre, norm_post, attn_w_in, attn_w_out, pool_w_in, pool_w_grp, pool_b_grp, pool_scale, pool_w_out, loss_target, m_norm_pre, m_norm_post, m_attn_w_in, m_attn_w_out, m_pool_w_in, m_pool_w_grp, m_pool_b_grp, m_pool_scale, m_pool_w_out, v_norm_pre, v_norm_post, v_attn_w_in, v_attn_w_out, v_pool_w_in, v_pool_w_grp, v_pool_b_grp, v_pool_scale, v_pool_w_out):
    S = x.shape[1]
    xi, yi, ci = _mesh_pos()
    dev = 4 * xi + 2 * yi + ci
    x2 = x[0]
    tgt = loss_target[0]

    small = jnp.concatenate([pool_b_grp[0].reshape(2, HD), pool_scale[0].reshape(2, HD),
                             jnp.zeros((4, HD), F32)], axis=0)
    w_in_l = attn_w_in[0].astype(BF16)
    hop1, hop1_token = split_start("gather_w_in_start", [w_in_l], [lax.empty((N_DEV,) + w_in_l.shape, BF16)],
                                   _hop1_plan(), 3)

    pos = positions[0]
    tabs = [_rope_tables(pos.reshape(S // d, d).T.reshape(S)) for d in DIL]
    ehot = (jnp.arange(E)[None, :] // HD == jnp.arange(HD)[:, None]).astype(BF16)
    seg_tiles = SEG // CT

    xn0, xn0_4, xn0_16, xn0t = norm_pre0(x2, norm_pre[0:1], hop1_token)
    xn0s = [xn0, xn0_4.reshape(S, D), xn0_16.reshape(S, D)]
    xn0ts = [xn0t, transpose_rows(xn0s[1], "xn0t_4"), transpose_rows(xn0s[2], "xn0t_16")]

    (w_in_l,), (w_in8,) = split_wait("gather_w_in_wait", hop1, _hop1_plan(), xn0ts[2])
    hop2, hop2_token = split_start("gather_w_in_hop2_start", [w_in8], None, _hop2_plan(D), 4)
    _, (w_in8,) = split_wait("gather_w_in_hop2_wait", hop2, _hop2_plan(D), hop2_token, inplace=True)
    hop3, hop3_token = split_start("gather_w_in_hop3_start", [w_in8], None, _hop3_plan(), 1)
    _, (w_in8,) = split_wait("gather_w_in_hop3_wait", hop3, _hop3_plan(), hop3_token, inplace=True)
    w_in8 = lax.dynamic_update_slice(w_in8, w_in_l[None], (dev, 0, 0))
    small, w_in8 = lax.optimization_barrier((small, w_in8))
    rest_l = [attn_w_out[0].astype(BF16), pool_w_in[0].astype(BF16), pool_w_grp[0].astype(BF16),
              pool_w_out[0].astype(BF16), small]
    rest_flight, rest_token = split_start(
        "gather_rest_start", rest_l, [lax.empty((N_DEV,) + a.shape, a.dtype) for a in rest_l], _peers_plan(), 7)
    Ps, os_, lses = [], [], []
    for g, d in enumerate(DIL):
        P = mm_in(xn0s[g], w_in8, g * seg_tiles, seg_tiles, tabs[g], f"mm_qkv{g}", after=rest_token)
        o, l = attn_fwd(P, g, d)
        Ps.append(P)
        os_.append(o)
        lses.append(l)
    z0 = mm_in(xn0, w_in8, 3 * seg_tiles, E // CT, None, "mm_z0")
    ys, lse3, ya, yat = combine_fwd(os_, lses, z0, ehot)

    rest_l, rest8 = split_wait("gather_rest_wait", rest_flight, _peers_plan(), ya)
    rest8 = [lax.dynamic_update_slice(r8, a[None], (dev,) + (0,) * a.ndim) for r8, a in zip(rest8, rest_l)]
    w_out8, wp_in8, wg8, wp_out8, small8 = rest8
    w_out = w_out8.reshape(E, D)
    wp_out = wp_out8.reshape(E, D)
    wp_in = wp_in8.transpose(1, 0, 2).reshape(D, 2 * E)
    wg = wg8.transpose(1, 0, 2, 3).reshape(4, PC, PC)
    b_full = small8[:, 0:2, :].reshape(N_DEV, 4, PC // N_DEV).transpose(1, 0, 2).reshape(1, E)
    scale_full = small8[:, 2:4, :].reshape(1, E)
    a0 = mm_rows(ya, w_out, "mm_out0", F32)
    h1, xn1, xn1t = post0_pre1(x2, a0, norm_post[0:1], norm_pre[1:2])

    uz = mm_rows(xn1, wp_in, "mm_uz", F32, tm=512)
    pooled, pooled_t = pool_fwd(uz)
    hgrp, yp, ypt = mm_grp(pooled, wg, b_full, scale_full, uz)
    a1 = mm_rows(yp, wp_out, "mm_out1", F32)
    dh2, da1, loss_rows, dg_post1 = post1_loss(h1, a1, tgt, norm_post[1:2])
    loss = lax.psum(0.5 / D * jnp.sum(loss_rows), ("x", "y", "c"))

    dh, duz, dscale_p, db_p = mm_dyp(da1, wp_out, uz, hgrp, scale_full)
    dpooled = mm_dpooled(dh, wg)
    duz = pool_bwd(dpooled, duz)
    g_wg = mm_dwg(pooled_t, dh)
    g_wp_out = mm_wgrad_rows(ypt, da1, "mm_dwp_out")
    g_wp_in = mm_wgrad_cols(xn1t, duz, "mm_dwp_in", shard=PC)
    dxn1 = mm_dx_full(duz, wp_in, "mm_dxn1")
    dh1, da0, dg_pre1, dg_post0 = mid_bwd(dxn1, dh2, h1, a0, norm_pre[1:2], norm_post[0:1])

    dys, dz0 = mm_dya(da0, w_out, z0, ys[0])
    g_w_out = mm_wgrad_rows(yat, da0, "mm_dw_out")
    g_w_in = mm_dw_in_part(xn0t, dz0, 3 * seg_tiles, None, "mm_dw_in_z")
    dPs = []
    for g, d in enumerate(DIL):
        dP = attn_bwd(Ps[g], dys[g], ys[g], lse3[g], tabs[g], g, d)
        g_w_in = mm_dw_in_part(xn0ts[g], dP, g * seg_tiles, g_w_in, f"mm_dw_in{g}")
        dPs.append(dP)

    cidx = ci.astype(jnp.int32).reshape(1)
    chip = (2 * xi + yi).astype(jnp.int32).reshape(1)
    fulls = [g_w_in, g_w_out.reshape(N_DEV, E // N_DEV, D), g_wp_in,
             g_wg.reshape(4, N_DEV, PC // N_DEV, PC).transpose(1, 0, 2, 3).reshape(N_DEV, 4 * PC // N_DEV, PC),
             g_wp_out.reshape(N_DEV, E // N_DEV, D)]
    pair_flight, pair_token = split_start(
        "rs_pair_start", fulls, [lax.empty((4,) + f.shape[1:], F32) for f in fulls], _pair_plan(), 4)
    dx_z = mm_dx_part(dz0, w_in8, 3 * seg_tiles, "mm_dxn0_z", after=pair_token)
    dx_0 = mm_dx_part(dPs[0], w_in8, 0, "mm_dxn0_0", after=dx_z)
    fulls, sibs = split_wait("rs_pair_wait", pair_flight, _pair_plan(), dx_0)
    parts = [pair_add(f, s, cidx, f"pair_add{k}") for k, (f, s) in enumerate(zip(fulls, sibs))]
    chips_flight, chips_token = split_start(
        "rs_chips_start", parts, [jnp.zeros(p.shape, BF16) for p in parts], _chips_plan(), 3)
    dx_1 = mm_dx_part(dPs[1], w_in8, seg_tiles, "mm_dxn0_1", after=chips_token)
    dx_2 = mm_dx_part(dPs[2], w_in8, 2 * seg_tiles, "mm_dxn0_2", after=dx_1)
    grad_x, dg_pre0 = pre0_bwd(dx_0, dx_z, dx_1, dx_2, dh1, x2, norm_pre[0:1])
    parts, recvs = split_wait("rs_chips_wait", chips_flight, _chips_plan(), grad_x)
    shards = [(attn_w_in, m_attn_w_in, v_attn_w_in), (attn_w_out, m_attn_w_out, v_attn_w_out),
              (pool_w_in, m_pool_w_in, v_pool_w_in), (pool_w_grp, m_pool_w_grp, v_pool_w_grp),
              (pool_w_out, m_pool_w_out, v_pool_w_out)]
    big = []
    for k, (recv, part, (w, m, v)) in enumerate(zip(recvs, parts, shards)):
        shp = w.shape
        r2 = recv.shape[1:]
        res = adamw_sum(recv, part, chip, w.reshape(r2), m.reshape(r2), v.reshape(r2), f"adamw{k}")
        big.append([t.reshape(shp) for t in res])

    smalls = jnp.concatenate([dg_pre0.sum(0, keepdims=True), dg_pre1.sum(0, keepdims=True),
                              dg_post0.sum(0, keepdims=True), dg_post1.sum(0, keepdims=True),
                              db_p.sum(0).reshape(2, D), dscale_p.sum(0).reshape(2, D)], axis=0)
    (smalls8,) = all_gather([smalls], "gather_small_grads")
    tot = sum_slots(smalls8, "sum_small_grads")
    g_norm_pre, g_norm_post = tot[0:2], tot[2:4]
    g_b = lax.dynamic_slice_in_dim(tot[4:6].reshape(4, PC), dev * (PC // N_DEV), PC // N_DEV, axis=1)[None]
    g_scale = lax.dynamic_slice_in_dim(tot[6:8].reshape(1, E), dev * (E // N_DEV), E // N_DEV, axis=1)
    sm = [adamw_small(g_norm_pre, norm_pre, m_norm_pre, v_norm_pre, "adamw_norm_pre"),
          adamw_small(g_norm_post, norm_post, m_norm_post, v_norm_post, "adamw_norm_post"),
          adamw_small(g_b, pool_b_grp, m_pool_b_grp, v_pool_b_grp, "adamw_b"),
          adamw_small(g_scale, pool_scale, m_pool_scale, v_pool_scale, "adamw_scale")]

    grads = [g_norm_pre, g_norm_post, big[0][0], big[1][0], big[2][0], big[3][0], g_b, g_scale, big[4][0]]

    def pick(k):
        return [sm[0][k - 1], sm[1][k - 1], big[0][k], big[1][k], big[2][k], big[3][k], sm[2][k - 1], sm[3][k - 1],
                big[4][k]]

    return (loss, grad_x[None], *grads, *pick(1), *pick(2), *pick(3))
```

```python
import math

import jax
import jax.numpy as jnp
from jax import lax
from jax.experimental import pallas as pl
from jax.experimental.pallas import tpu as pltpu

F32 = jnp.float32
BF16 = jnp.bfloat16
SDS = jax.ShapeDtypeStruct

N_DEV = 8
D = 1024
E = 2048
HD = 128
NH = E // HD
DIL = (1, 4, 16)
QB = 128
SEG = 3 * E
W_IN_COLS = 3 * SEG + E
W_SHARD = W_IN_COLS // N_DEV
CT = 512
PC = E // 4
EPS = 1e-6
NEG = -1e30
SCALE = 1.0 / math.sqrt(HD)
LR, B1, B2, ADAM_EPS, WD, STEP = 0.001, 0.9, 0.999, 1e-08, 0.01, 10
MIB = 1024 * 1024
ANY = pl.BlockSpec(memory_space=pl.ANY)
MESH = pl.DeviceIdType.MESH


def _cp(sem, mb):
    return pltpu.CompilerParams(dimension_semantics=sem, vmem_limit_bytes=mb * MIB)


def _dot(a, b):
    return jnp.dot(a, b, preferred_element_type=F32)


def _dot_nt(a, b):
    return lax.dot_general(a, b, (((1,), (1,)), ((), ())), preferred_element_type=F32)


def _rms(h):
    return lax.rsqrt(jnp.mean(h * h, axis=-1, keepdims=True) + EPS)


def _row_tile(R, C, budget):
    tr = R
    while tr * C * 4 > budget and tr % 16 == 0:
        tr //= 2
    return tr


def _fold8(t):
    return t.reshape(t.shape[0] // 8, 8, t.shape[1]).sum(axis=0)


def _sigmoid(z):
    return pl.reciprocal(1.0 + jnp.exp(-z), approx=True)


LANES = 128


def _scr(rows, C):
    return pltpu.VMEM((C // LANES, rows, LANES), F32)


def _scr_put(scr, val):
    for c in range(scr.shape[0]):
        scr[c] = val[:, c * LANES:(c + 1) * LANES]


def _scr_get(scr):
    return jnp.concatenate([scr[c] for c in range(scr.shape[0])], axis=1)


def _store_perm(dst_ref, scr, d):
    n = dst_ref.shape[1]
    for r in range(d):
        for c in range(scr.shape[0]):
            dst_ref[r, :, c * LANES:(c + 1) * LANES] = scr[c, pl.ds(r, n, stride=d), :].astype(dst_ref.dtype)


def _load_perm(scr, src_ref, d, add=False):
    n = src_ref.shape[1]
    for r in range(d):
        rows = pl.ds(r, n, stride=d)
        for c in range(scr.shape[0]):
            v = src_ref[r, :, c * LANES:(c + 1) * LANES].astype(F32)
            scr[c, rows, :] = scr[c, rows, :] + v if add else v


def _rope(t, c, s1, s2):
    t = t.astype(BF16)
    return t * c + pltpu.roll(t, HD - 16, 1) * s1 + pltpu.roll(t, 16, 1) * s2


def _unrope(t, c, s1, s2):
    t = t.astype(BF16)
    return t * c - pltpu.roll(t, HD - 16, 1) * s1 - pltpu.roll(t, 16, 1) * s2


def _mesh_pos():
    return lax.axis_index("x"), lax.axis_index("y"), lax.axis_index("c")


def all_gather(arrs, name):
    n = len(arrs)

    def body(*refs):
        ins, outs = refs[:n], refs[n:2 * n]
        send_sems, recv_sems, local_sems = refs[2 * n:]
        x, y, c = _mesh_pos()
        me, sib = (x, y, c), (x, y, 1 - c)
        chips = [(1 - x, y), (x, 1 - y), (1 - x, 1 - y)]

        def slot(p):
            return 4 * p[0] + 2 * p[1] + p[2]

        def copy(a, k, block, to, src=None):
            dst = outs[a].at[slot(block)]
            return pltpu.make_async_remote_copy(
                src_ref=dst if src is None else src, dst_ref=dst,
                send_sem=send_sems.at[a, k], recv_sem=recv_sems.at[a, k],
                device_id=to, device_id_type=MESH)

        mine = [pltpu.make_async_copy(ins[a], outs[a].at[slot(me)], local_sems.at[a]) for a in range(n)]
        for cp in mine:
            cp.start()
        first = []
        for a in range(n):
            first.append(copy(a, 0, me, sib, src=ins[a]))
            for j, chip in enumerate(chips):
                first.append(copy(a, 1 + j, me, (*chip, c), src=ins[a]))
        for cp in first:
            cp.start()
        passed = []
        for j, chip in enumerate(chips):
            for a in range(n):
                copy(a, 1 + j, (*chip, c), me).wait_recv()
                fw = copy(a, 4 + j, (*chip, c), sib)
                fw.start()
                passed.append(fw)
        for a in range(n):
            copy(a, 0, sib, me).wait_recv()
        for j, chip in enumerate(chips):
            for a in range(n):
                copy(a, 4 + j, (*chip, 1 - c), me).wait_recv()
        for cp in first + passed:
            cp.wait_send()
        for cp in mine:
            cp.wait()

    return pl.pallas_call(
        body, name=name,
        out_shape=[SDS((N_DEV,) + a.shape, a.dtype) for a in arrs],
        in_specs=[ANY] * n, out_specs=[ANY] * n,
        scratch_shapes=[pltpu.SemaphoreType.DMA((n, 7)), pltpu.SemaphoreType.DMA((n, 7)),
                        pltpu.SemaphoreType.DMA((n,))],
    )(*arrs)


HBM_SPEC = pl.BlockSpec(memory_space=pltpu.HBM)
SEM_SPEC = pl.BlockSpec(memory_space=pltpu.SEMAPHORE)
EFFECT = pltpu.SideEffectType.DATAFLOW_SIDE_EFFECTING


def _pair_plan():
    def plan(x, y, c):
        return [(2 * q + (1 - c), q, (x, y, 1 - c)) for q in range(4)]
    return plan


def _chips_plan():
    def plan(x, y, c):
        chips = [(1 - x, y), (x, 1 - y), (1 - x, 1 - y)]
        return [(2 * cx + cy, 2 * x + y, (cx, cy, c)) for cx, cy in chips]
    return plan


def _hop1_plan():
    def plan(x, y, c):
        me = 4 * x + 2 * y + c
        return [(None, me, (x, y, 1 - c)), (None, me, (1 - x, y, c)), (None, me, (x, 1 - y, c))]
    return plan


def _hop2_plan(rows):
    half = rows // 2

    def plan(x, y, c):
        sx, sy = 4 * (1 - x) + 2 * y + c, 4 * x + 2 * (1 - y) + c
        top, bottom = pl.ds(0, half), pl.ds(half, half)
        return [((sx, top), (sx, top), (x, 1 - y, c)), ((sy, bottom), (sy, bottom), (1 - x, y, c)),
                (sx, sx, (x, y, 1 - c)), (sy, sy, (x, y, 1 - c))]
    return plan


def _hop3_plan():
    def plan(x, y, c):
        sd = 4 * (1 - x) + 2 * (1 - y) + c
        return [(sd, sd, (x, y, 1 - c))]
    return plan


def _peers_plan():
    def plan(x, y, c):
        out = []
        for k in range(1, N_DEV):
            fx, fy, fc = (k >> 2) & 1, (k >> 1) & 1, k & 1
            px, py, pc = (x + fx) % 2, (y + fy) % 2, (c + fc) % 2
            out.append((None, 4 * x + 2 * y + c, (px, py, pc)))
        return out
    return plan


def _split_copies(plan, srcs, lands, send_sems, recv_sems):
    x, y, c = _mesh_pos()
    cps = []
    for a, (src, land) in enumerate(zip(srcs, lands)):
        steps = plan(x, y, c)
        for k, (si, li, to) in enumerate(steps):
            sem = a * len(steps) + k
            cps.append(pltpu.make_async_remote_copy(
                src_ref=src if si is None else src.at[si], dst_ref=land.at[li],
                send_sem=send_sems.at[sem], recv_sem=recv_sems.at[sem],
                device_id=to, device_id_type=MESH))
    return cps


def split_start(name, srcs, lands, plan, nk):
    n = len(srcs)
    ops = list(srcs) + ([] if lands is None else list(lands))
    nb = len(ops)

    def body(*refs):
        token = refs[-1]
        for cp in _split_copies(plan, refs[:n], refs[nb - n:nb], refs[nb], refs[nb + 1]):
            cp.start()
        token[...] = jnp.zeros_like(token)

    ops = [pltpu.with_memory_space_constraint(a, pltpu.HBM) for a in ops]
    res = pl.pallas_call(
        body, name=name,
        out_shape=(pltpu.SemaphoreType.DMA((n * nk,)), pltpu.SemaphoreType.DMA((n * nk,)),
                   *[pltpu.HBM(a.shape, a.dtype) for a in ops], SDS((8, 128), F32)),
        in_specs=[HBM_SPEC] * nb,
        out_specs=(SEM_SPEC, SEM_SPEC, *[HBM_SPEC] * nb, pl.BlockSpec(memory_space=pltpu.VMEM)),
        input_output_aliases={i: 2 + i for i in range(nb)},
        compiler_params=pltpu.CompilerParams(has_side_effects=EFFECT),
    )(*ops)
    return res[:-1], res[-1]


def split_wait(name, flight, plan, after, inplace=False):
    send_sems, recv_sems = flight[0], flight[1]
    bufs = list(flight[2:])
    nb = len(bufs)
    n = nb if inplace else nb // 2

    def body(*refs):
        for cp in _split_copies(plan, refs[:n], refs[nb - n:nb], refs[nb], refs[nb + 1]):
            cp.wait_send()
            cp.wait_recv()

    res = pl.pallas_call(
        body, name=name,
        out_shape=[pltpu.HBM(a.shape, a.dtype) for a in bufs],
        in_specs=[HBM_SPEC] * nb + [SEM_SPEC, SEM_SPEC, ANY],
        out_specs=[HBM_SPEC] * nb,
        input_output_aliases={i: i for i in range(nb)},
        compiler_params=pltpu.CompilerParams(has_side_effects=EFFECT),
    )(*bufs, send_sems, recv_sems, after)
    return res[:n], res[nb - n:]


def pair_add(full, sib, cidx, name):
    _, R, C = full.shape
    tr = _row_tile(R, C, MIB)

    def body(c_ref, a_ref, b_ref, o_ref):
        o_ref[...] = (a_ref[...] + b_ref[...]).astype(BF16)

    return pl.pallas_call(
        body, name=name,
        grid_spec=pltpu.PrefetchScalarGridSpec(
            num_scalar_prefetch=1, grid=(4, R // tr),
            in_specs=[pl.BlockSpec((None, tr, C), lambda q, i, cr: (2 * q + cr[0], i, 0)),
                      pl.BlockSpec((None, tr, C), lambda q, i, cr: (q, i, 0))],
            out_specs=pl.BlockSpec((None, tr, C), lambda q, i, cr: (q, i, 0))),
        out_shape=SDS((4, R, C), BF16),
        compiler_params=_cp(("parallel", "parallel"), 32),
    )(cidx, full, sib)


def _adam_math(w, g, m, v):
    m2 = B1 * m + (1.0 - B1) * g
    v2 = B2 * v + (1.0 - B2) * (g * g)
    m_hat = m2 / (1.0 - B1 ** STEP)
    v_hat = v2 / (1.0 - B2 ** STEP)
    delta = -LR * (m_hat / (jnp.sqrt(v_hat) + ADAM_EPS) + WD * w)
    return delta, m2, v2


def adamw_sum(recv, part, chip, w, m, v, name):
    K, R, C = recv.shape
    tr = _row_tile(R, C, MIB)

    def body(chip_ref, r_ref, p_ref, w_ref, m_ref, v_ref, g_ref, d_ref, m2_ref, v2_ref):
        g = r_ref[0].astype(F32)
        for k in range(1, K):
            g = g + r_ref[k].astype(F32)
        g = g + p_ref[...].astype(F32)
        delta, m2, v2 = _adam_math(w_ref[...], g, m_ref[...], v_ref[...])
        g_ref[...] = g
        d_ref[...] = delta
        m2_ref[...] = m2
        v2_ref[...] = v2

    tile = pl.BlockSpec((tr, C), lambda i, cr: (i, 0))
    return pl.pallas_call(
        body, name=name,
        grid_spec=pltpu.PrefetchScalarGridSpec(
            num_scalar_prefetch=1, grid=(R // tr,),
            in_specs=[pl.BlockSpec((K, tr, C), lambda i, cr: (0, i, 0)),
                      pl.BlockSpec((None, tr, C), lambda i, cr: (cr[0], i, 0)), tile, tile, tile],
            out_specs=[tile] * 4),
        out_shape=[SDS((R, C), F32)] * 4,
        compiler_params=_cp(("parallel",), 32),
    )(chip, recv, part, w, m, v)


def adamw_small(g, w, m, v, name):
    def body(g_ref, w_ref, m_ref, v_ref, d_ref, m2_ref, v2_ref):
        delta, m2, v2 = _adam_math(w_ref[...], g_ref[...], m_ref[...], v_ref[...])
        d_ref[...] = delta
        m2_ref[...] = m2
        v2_ref[...] = v2

    return pl.pallas_call(body, name=name, out_shape=[SDS(w.shape, F32)] * 3)(g, w, m, v)


def sum_slots(a, name):
    K = a.shape[0]

    def body(a_ref, o_ref):
        t = a_ref[0]
        for k in range(1, K):
            t = t + a_ref[k]
        o_ref[...] = t

    return pl.pallas_call(body, name=name, out_shape=SDS(a.shape[1:], F32))(a)


def norm_pre0(x, g, after):
    S = x.shape[0]
    ts = 512

    def body(x_ref, g_ref, after_ref, o_ref, o4_ref, o16_ref, ot_ref, scr):
        h = x_ref[...]
        xn = h * _rms(h) * g_ref[...]
        o_ref[...] = xn.astype(BF16)
        ot_ref[...] = xn.T.astype(BF16)
        _scr_put(scr, xn)
        _store_perm(o4_ref, scr, 4)
        _store_perm(o16_ref, scr, 16)

    return pl.pallas_call(
        body, name="norm_pre0", grid=(S // ts,),
        in_specs=[pl.BlockSpec((ts, D), lambda i: (i, 0)), pl.BlockSpec((1, D), lambda i: (0, 0)), ANY],
        out_specs=[pl.BlockSpec((ts, D), lambda i: (i, 0)),
                   pl.BlockSpec((4, ts // 4, D), lambda i: (0, i, 0)),
                   pl.BlockSpec((16, ts // 16, D), lambda i: (0, i, 0)),
                   pl.BlockSpec((D, ts), lambda i: (0, i))],
        out_shape=[SDS((S, D), BF16), SDS((4, S // 4, D), BF16), SDS((16, S // 16, D), BF16), SDS((D, S), BF16)],
        scratch_shapes=[_scr(ts, D)],
        compiler_params=_cp(("parallel",), 32),
    )(x, g, after)


def transpose_rows(a, name):
    S, C = a.shape
    ts = 512

    def body(a_ref, o_ref):
        o_ref[...] = a_ref[...].astype(F32).T.astype(BF16)

    return pl.pallas_call(
        body, name=name, grid=(S // ts,),
        in_specs=[pl.BlockSpec((ts, C), lambda i: (i, 0))],
        out_specs=pl.BlockSpec((C, ts), lambda i: (0, i)),
        out_shape=SDS((C, S), BF16),
        compiler_params=_cp(("parallel",), 32),
    )(a)


def post0_pre1(x, a0, g_post, g_pre):
    S = x.shape[0]
    ts = 512

    def body(x_ref, a_ref, gp_ref, gn_ref, h_ref, o_ref, ot_ref):
        a = a_ref[...]
        h1 = x_ref[...] + a * _rms(a) * gp_ref[...]
        h_ref[...] = h1
        xn = h1 * _rms(h1) * gn_ref[...]
        o_ref[...] = xn.astype(BF16)
        ot_ref[...] = xn.T.astype(BF16)

    row = pl.BlockSpec((ts, D), lambda i: (i, 0))
    vec = pl.BlockSpec((1, D), lambda i: (0, 0))
    return pl.pallas_call(
        body, name="post0_pre1", grid=(S // ts,),
        in_specs=[row, row, vec, vec],
        out_specs=[row, row, pl.BlockSpec((D, ts), lambda i: (0, i))],
        out_shape=[SDS((S, D), F32), SDS((S, D), BF16), SDS((D, S), BF16)],
        compiler_params=_cp(("parallel",), 40),
    )(x, a0, g_post, g_pre)


def post1_loss(h1, a1, target, g_post):
    S = h1.shape[0]
    ts = 512

    def body(h_ref, a_ref, t_ref, g_ref, dh_ref, da_ref, loss_ref, dg_ref):
        @pl.when(pl.program_id(0) == 0)
        def _():
            loss_ref[...] = jnp.zeros_like(loss_ref)
            dg_ref[...] = jnp.zeros_like(dg_ref)

        a = a_ref[...]
        g = g_ref[...]
        rp = _rms(a)
        yhat = a * rp
        e = h_ref[...] + yhat * g - t_ref[...]
        loss_ref[...] += _fold8(e * e)
        dh = e * (1.0 / D)
        dh_ref[...] = dh
        dg_ref[...] += _fold8(dh * yhat)
        dyh = dh * g
        da = rp * (dyh - yhat * jnp.mean(dyh * yhat, axis=-1, keepdims=True))
        da_ref[...] = da.astype(BF16)

    row = pl.BlockSpec((ts, D), lambda i: (i, 0))
    acc = pl.BlockSpec((8, D), lambda i: (0, 0))
    return pl.pallas_call(
        body, name="post1_loss", grid=(S // ts,),
        in_specs=[row, row, row, pl.BlockSpec((1, D), lambda i: (0, 0))],
        out_specs=[row, row, acc, acc],
        out_shape=[SDS((S, D), F32), SDS((S, D), BF16), SDS((8, D), F32), SDS((8, D), F32)],
        compiler_params=_cp(("arbitrary",), 40),
    )(h1, a1, target, g_post)


def mid_bwd(dxn1, dh2, h1, a0, g_pre1, g_post0):
    S = h1.shape[0]
    ts = 512

    def body(dx_ref, dh2_ref, h_ref, a_ref, gn_ref, gp_ref, dh1_ref, da_ref, dgn_ref, dgp_ref):
        @pl.when(pl.program_id(0) == 0)
        def _():
            dgn_ref[...] = jnp.zeros_like(dgn_ref)
            dgp_ref[...] = jnp.zeros_like(dgp_ref)

        h = h_ref[...]
        r1 = _rms(h)
        xhat = h * r1
        dxn = dx_ref[...]
        dgn_ref[...] += _fold8(dxn * xhat)
        dxh = dxn * gn_ref[...]
        dh1 = dh2_ref[...] + r1 * (dxh - xhat * jnp.mean(dxh * xhat, axis=-1, keepdims=True))
        dh1_ref[...] = dh1
        a = a_ref[...]
        rp = _rms(a)
        yhat = a * rp
        dgp_ref[...] += _fold8(dh1 * yhat)
        dyh = dh1 * gp_ref[...]
        da = rp * (dyh - yhat * jnp.mean(dyh * yhat, axis=-1, keepdims=True))
        da_ref[...] = da.astype(BF16)

    row = pl.BlockSpec((ts, D), lambda i: (i, 0))
    vec = pl.BlockSpec((1, D), lambda i: (0, 0))
    acc = pl.BlockSpec((8, D), lambda i: (0, 0))
    return pl.pallas_call(
        body, name="mid_bwd", grid=(S // ts,),
        in_specs=[row, row, row, row, vec, vec],
        out_specs=[row, row, acc, acc],
        out_shape=[SDS((S, D), F32), SDS((S, D), BF16), SDS((8, D), F32), SDS((8, D), F32)],
        compiler_params=_cp(("arbitrary",), 48),
    )(dxn1, dh2, h1, a0, g_pre1, g_post0)


def pre0_bwd(dx_tok, dx_z, dx4, dx16, dh1, x, g_pre0):
    S = x.shape[0]
    ts = 512

    def body(da_ref, dz_ref, d4_ref, d16_ref, dh_ref, x_ref, g_ref, gx_ref, dg_ref, scr):
        @pl.when(pl.program_id(0) == 0)
        def _():
            dg_ref[...] = jnp.zeros_like(dg_ref)

        _scr_put(scr, da_ref[...] + dz_ref[...])
        _load_perm(scr, d4_ref, 4, add=True)
        _load_perm(scr, d16_ref, 16, add=True)
        h = x_ref[...]
        r = _rms(h)
        xhat = h * r
        dxn = _scr_get(scr)
        dg_ref[...] += _fold8(dxn * xhat)
        dxh = dxn * g_ref[...]
        gx_ref[...] = dh_ref[...] + r * (dxh - xhat * jnp.mean(dxh * xhat, axis=-1, keepdims=True))

    row = pl.BlockSpec((ts, D), lambda i: (i, 0))
    return pl.pallas_call(
        body, name="pre0_bwd", grid=(S // ts,),
        in_specs=[row, row, pl.BlockSpec((4, ts // 4, D), lambda i: (0, i, 0)),
                  pl.BlockSpec((16, ts // 16, D), lambda i: (0, i, 0)), row, row,
                  pl.BlockSpec((1, D), lambda i: (0, 0))],
        out_specs=[row, pl.BlockSpec((8, D), lambda i: (0, 0))],
        out_shape=[SDS((S, D), F32), SDS((8, D), F32)],
        scratch_shapes=[_scr(ts, D)],
        compiler_params=_cp(("arbitrary",), 48),
    )(dx_tok, dx_z, dx4.reshape(4, S // 4, D), dx16.reshape(16, S // 16, D), dh1, x, g_pre0)


def _w_tile(tile0):
    per = W_SHARD // CT
    return lambda t: ((tile0 + t) // per, 0, (tile0 + t) % per)


def mm_in(xn, w8, tile0, ntiles, tabs, name, after=None):
    S = xn.shape[0]
    tm = 2048
    wt = _w_tile(tile0)

    def body(a_ref, b_ref, *rest):
        o_ref = rest[-1]
        rc = 512
        for u in range(tm // rc):
            rows = slice(u * rc, (u + 1) * rc)
            r = _dot(a_ref[rows, :], b_ref[...])
            if tabs is None:
                o_ref[rows, :] = r.astype(BF16)
                continue
            c_ref, s1_ref, s2_ref = rest[:3]
            rot = pl.program_id(1) < 2 * E // CT
            qs = jnp.where(pl.program_id(1) < E // CT, SCALE, 1.0)
            c = jnp.where(rot, (c_ref[rows, :] * qs).astype(BF16), 1.0)
            s1 = jnp.where(rot, (s1_ref[rows, :] * qs).astype(BF16), 0.0)
            s2 = jnp.where(rot, (s2_ref[rows, :] * qs).astype(BF16), 0.0)
            for hh in range(CT // HD):
                cs = slice(hh * HD, (hh + 1) * HD)
                o_ref[rows, cs] = _rope(r[:, cs], c, s1, s2).astype(BF16)

    tab = pl.BlockSpec((tm, HD), lambda i, t: (i, 0))
    return pl.pallas_call(
        body, name=name, grid=(S // tm, ntiles),
        in_specs=[pl.BlockSpec((tm, D), lambda i, t: (i, 0)),
                  pl.BlockSpec((None, D, CT), lambda i, t: wt(t))] + ([] if tabs is None else [tab] * 3)
        + ([] if after is None else [ANY]),
        out_specs=pl.BlockSpec((tm, CT), lambda i, t: (i, t)),
        out_shape=SDS((S, ntiles * CT), BF16),
        compiler_params=_cp(("parallel", "parallel"), 48),
    )(xn, w8, *(() if tabs is None else tabs), *(() if after is None else (after,)))


def mm_rows(a, b, name, out_dtype, tm=1024):
    M, K = a.shape
    N = b.shape[1]

    def body(a_ref, b_ref, o_ref):
        for cidx in range(N // 256):
            col = slice(cidx * 256, (cidx + 1) * 256)
            o_ref[:, col] = _dot(a_ref[...], b_ref[:, col]).astype(out_dtype)

    return pl.pallas_call(
        body, name=name, grid=(M // tm,),
        in_specs=[pl.BlockSpec((tm, K), lambda i: (i, 0)), pl.BlockSpec((K, N), lambda i: (0, 0))],
        out_specs=pl.BlockSpec((tm, N), lambda i: (i, 0)),
        out_shape=SDS((M, N), out_dtype),
        compiler_params=_cp(("parallel",), 48),
    )(a, b)


def mm_acc(a, b, name, *, grid, a_spec, b_spec, o_spec, o_shape, acc_shape, write, vmem=48):
    nk = grid[-1]

    def body(a_ref, b_ref, o_ref, acc_ref):
        k = pl.program_id(len(grid) - 1)

        @pl.when(k == 0)
        def _():
            acc_ref[...] = jnp.zeros_like(acc_ref)

        acc_ref[...] += _dot(a_ref[...], b_ref[...])

        @pl.when(k == nk - 1)
        def _():
            write(o_ref, acc_ref)

    return pl.pallas_call(
        body, name=name, grid=grid, in_specs=[a_spec, b_spec], out_specs=o_spec, out_shape=o_shape,
        scratch_shapes=[pltpu.VMEM(acc_shape, F32)],
        compiler_params=_cp(("parallel",) * (len(grid) - 1) + ("arbitrary",), vmem),
    )(a, b)


def _write_plain(o_ref, acc_ref):
    o_ref[...] = acc_ref[...]


def mm_wgrad_rows(at, b, name):
    M, S = at.shape
    N = b.shape[1]
    tm, tk = 1024, 1024
    return mm_acc(at, b, name, grid=(M // tm, S // tk),
                  a_spec=pl.BlockSpec((tm, tk), lambda i, k: (i, k)),
                  b_spec=pl.BlockSpec((tk, N), lambda i, k: (k, 0)),
                  o_spec=pl.BlockSpec((tm, N), lambda i, k: (i, 0)),
                  o_shape=SDS((M, N), F32), acc_shape=(tm, N), write=_write_plain)


def mm_wgrad_cols(at, b, name, *, shard):
    M, S = at.shape
    tk = 1024
    nb = 2
    tn = nb * shard

    def write(o_ref, acc_ref):
        for u in range(nb):
            o_ref[u] = acc_ref[:, u * shard:(u + 1) * shard]

    return mm_acc(at, b, name, grid=(N_DEV // nb, S // tk),
                  a_spec=pl.BlockSpec((M, tk), lambda t, k: (0, k)),
                  b_spec=pl.BlockSpec((tk, tn), lambda t, k: (k, t)),
                  o_spec=pl.BlockSpec((nb, M, shard), lambda t, k: (t, 0, 0)),
                  o_shape=SDS((N_DEV, M, shard), F32), acc_shape=(M, tn), write=write)


def mm_dwg(pooled_t, dh):
    S = dh.shape[0]
    tk = 2048
    return mm_acc(pooled_t, dh, "mm_dwg", grid=(4, S // tk),
                  a_spec=pl.BlockSpec((PC, tk), lambda g, k: (g, k)),
                  b_spec=pl.BlockSpec((tk, PC), lambda g, k: (k, g)),
                  o_spec=pl.BlockSpec((None, PC, PC), lambda g, k: (g, 0, 0)),
                  o_shape=SDS((4, PC, PC), F32), acc_shape=(PC, PC), write=_write_plain)


def mm_dx_full(da, w, name):
    S, K = da.shape
    N = w.shape[0]
    tm = 512

    def body(a_ref, b_ref, o_ref):
        o_ref[...] = _dot_nt(a_ref[...], b_ref[...])

    return pl.pallas_call(
        body, name=name, grid=(S // tm,),
        in_specs=[pl.BlockSpec((tm, K), lambda i: (i, 0)), pl.BlockSpec((N, K), lambda i: (0, 0))],
        out_specs=pl.BlockSpec((tm, N), lambda i: (i, 0)),
        out_shape=SDS((S, N), F32),
        compiler_params=_cp(("parallel",), 48),
    )(da, w)


def mm_dw_in_part(at, b, tile0, prev, name):
    M, S = at.shape
    ntiles = b.shape[1] // CT
    tk = 2048
    nk = S // tk
    per = W_SHARD // CT

    def body(a_ref, b_ref, *rest):
        o_ref, acc_ref, sems = rest[-3:]
        k, t = pl.program_id(0), pl.program_id(1)

        @pl.when(k == 0)
        def _():
            acc_ref[t] = _dot(a_ref[...], b_ref[...])

        @pl.when(k > 0)
        def _():
            acc_ref[t] += _dot(a_ref[...], b_ref[...])

        def out_copy(u):
            tile = tile0 + u
            off = (tile % per) * CT
            if not isinstance(off, int):
                off = pl.multiple_of(off, CT)
            return pltpu.make_async_copy(acc_ref.at[u], o_ref.at[tile // per, :, pl.ds(off, CT)], sems.at[u])

        @pl.when(k == nk - 1)
        def _():
            out_copy(t).start()

        @pl.when(jnp.logical_and(k == nk - 1, t == ntiles - 1))
        def _():
            for u in range(ntiles):
                out_copy(u).wait()

    return pl.pallas_call(
        body, name=name, grid=(nk, ntiles),
        in_specs=[pl.BlockSpec((M, tk), lambda k, t: (0, k)), pl.BlockSpec((tk, CT), lambda k, t: (k, t))]
        + ([] if prev is None else [ANY]),
        out_specs=ANY,
        out_shape=SDS((N_DEV, M, W_SHARD), F32),
        scratch_shapes=[pltpu.VMEM((ntiles, M, CT), F32), pltpu.SemaphoreType.DMA((ntiles,))],
        input_output_aliases={} if prev is None else {2: 0},
        compiler_params=_cp(("arbitrary", "arbitrary"), 48),
    )(at, b, *(() if prev is None else (prev,)))


def mm_dx_part(da, w8, tile0, name, after=None):
    S, K = da.shape
    ntiles = K // CT
    tm = 512
    per = W_SHARD // CT

    def body(a_ref, w_ref, *rest):
        o_ref, wcat, sems = rest[-3:]

        @pl.when(pl.program_id(0) == 0)
        def _():
            cps = [pltpu.make_async_copy(
                w_ref.at[(tile0 + u) // per, :, pl.ds(((tile0 + u) % per) * CT, CT)],
                wcat.at[:, pl.ds(u * CT, CT)], sems.at[u]) for u in range(ntiles)]
            for cp in cps:
                cp.start()
            for cp in cps:
                cp.wait()

        o_ref[...] = _dot_nt(a_ref[...], wcat[...])

    return pl.pallas_call(
        body, name=name, grid=(S // tm,),
        in_specs=[pl.BlockSpec((tm, K), lambda i: (i, 0)), ANY] + ([] if after is None else [ANY]),
        out_specs=pl.BlockSpec((tm, D), lambda i: (i, 0)),
        out_shape=SDS((S, D), F32),
        scratch_shapes=[pltpu.VMEM((D, K), BF16), pltpu.SemaphoreType.DMA((ntiles,))],
        compiler_params=_cp(("arbitrary",), 48),
    )(da, w8, *(() if after is None else (after,)))


HEADS_FWD = 4
HEADS_BWD = 2
AHEAD = 2


def _band_masks(not_first):
    row = lax.broadcasted_iota(jnp.int32, (QB, QB), 0)
    col = lax.broadcasted_iota(jnp.int32, (QB, QB), 1)
    cur = jnp.where(col <= row, 0.0, NEG)
    prev = jnp.where(col >= row, 0.0, NEG)
    first = jnp.where(jnp.logical_and(col >= row, not_first), 0.0, NEG)
    return col, jnp.concatenate([prev, cur], axis=1), jnp.concatenate([first, cur], axis=1)


def _fill_kv(ext, qkv_ref, kh_ref, vh_ref):
    ext[0:QB, 0:E] = kh_ref[...]
    ext[0:QB, E:2 * E] = vh_ref[...]
    ext[QB:, :] = qkv_ref[:, E:3 * E]


def attn_fwd(P, g, d):
    S = P.shape[0]
    L = S // d
    T = min(512, L)
    nq = T // QB
    ni = L // T

    def body(qkv_ref, kh_ref, vh_ref, o_ref, lse_ref, ext):
        col, mask, mask_first = _band_masks(pl.program_id(1) > 0)
        lse_ref[...] = jnp.zeros_like(lse_ref)
        _fill_kv(ext, qkv_ref, kh_ref, vh_ref)

        def heads(hp, carry):
            def front(h, j):
                cq = pl.ds(pl.multiple_of(h * HD, HD), HD)
                rows = slice(j * QB, (j + 1) * QB)
                krows = slice(j * QB, (j + 2) * QB)
                s = _dot_nt(qkv_ref[rows, cq], ext[krows, cq]) + (mask_first if j == 0 else mask)
                m = jnp.max(s, axis=1, keepdims=True)
                p = jnp.exp(s - m)
                den = jnp.sum(p, axis=1, keepdims=True)
                lse_ref[rows, :] = jnp.where(col == h, m + jnp.log(den), lse_ref[rows, :])
                return p.astype(BF16), den

            def back(h, j, p, den):
                off = pl.multiple_of(h * HD, HD)
                rows = slice(j * QB, (j + 1) * QB)
                krows = slice(j * QB, (j + 2) * QB)
                o_ref[rows, pl.ds(off, HD)] = (_dot(p, ext[krows, pl.ds(E + off, HD)]) / den).astype(BF16)

            items = [(HEADS_FWD * hp + hh, j) for hh in range(HEADS_FWD) for j in range(nq)]
            queue = [front(*it) for it in items[:AHEAD]]
            for u, it in enumerate(items):
                if u + AHEAD < len(items):
                    queue.append(front(*items[u + AHEAD]))
                back(*it, *queue.pop(0))
            return carry

        lax.fori_loop(0, NH // HEADS_FWD, heads, 0)

    halo = lambda r, i: jnp.maximum(r * (L // QB) + i * nq - 1, 0)
    return pl.pallas_call(
        body, name=f"attn_fwd{g}", grid=(d, ni),
        in_specs=[pl.BlockSpec((T, SEG), lambda r, i: (r * ni + i, 0)),
                  pl.BlockSpec((QB, E), lambda r, i: (halo(r, i), 1)),
                  pl.BlockSpec((QB, E), lambda r, i: (halo(r, i), 2))],
        out_specs=[pl.BlockSpec((T, E), lambda r, i: (r * ni + i, 0)),
                   pl.BlockSpec((T, HD), lambda r, i: (r * ni + i, 0))],
        out_shape=[SDS((S, E), BF16), SDS((S, HD), F32)],
        scratch_shapes=[pltpu.VMEM((T + QB, 2 * E), BF16)],
        compiler_params=_cp(("parallel", "parallel"), 48),
    )(P, P, P)


def _perm_specs(ts, C):
    return [pl.BlockSpec((ts, C), lambda i: (i, 0)),
            pl.BlockSpec((4, ts // 4, C), lambda i: (0, i, 0)),
            pl.BlockSpec((16, ts // 16, C), lambda i: (0, i, 0))]


def _perm_shapes(S, C, dtype):
    return [SDS((S, C), dtype), SDS((4, S // 4, C), dtype), SDS((16, S // 16, C), dtype)]


def combine_fwd(os_, lses, z, ehot):
    S = z.shape[0]
    ts = 256

    def body(o0, o1, o2, l0, l1, l2, z_ref, e_ref, y0, y1, y2, s0, s1, s2, ya_ref, yat_ref,
             so1, so2, sl1, sl2, sy, sl):
        _load_perm(so1, o1, 4)
        _load_perm(so2, o2, 16)
        _load_perm(sl1, l1, 4)
        _load_perm(sl2, l2, 16)
        ls = [l0[...], sl1[0], sl2[0]]
        m = jnp.maximum(jnp.maximum(ls[0], ls[1]), ls[2])
        es = [jnp.exp(l - m) for l in ls]
        den = es[0] + es[1] + es[2]
        sl[0] = m + jnp.log(den)
        y = None
        for e, o in zip(es, (o0[...].astype(F32), _scr_get(so1), _scr_get(so2))):
            w = e / den
            hi = w.astype(BF16)
            lo = (w - hi.astype(F32)).astype(BF16)
            wb = _dot(hi, e_ref[...]) + _dot(lo, e_ref[...])
            y = wb * o if y is None else y + wb * o
        z = z_ref[...].astype(F32)
        ya = y * (z * _sigmoid(z))
        ya_ref[...] = ya.astype(BF16)
        yat_ref[...] = ya.T.astype(BF16)
        _scr_put(sy, y)
        y0[...] = y.astype(BF16)
        _store_perm(y1, sy, 4)
        _store_perm(y2, sy, 16)
        s0[...] = sl[0]
        _store_perm(s1, sl, 4)
        _store_perm(s2, sl, 16)

    wide = pl.BlockSpec((ts, E), lambda i: (i, 0))
    os3 = [os_[0], os_[1].reshape(4, S // 4, E), os_[2].reshape(16, S // 16, E)]
    ls3 = [lses[0], lses[1].reshape(4, S // 4, HD), lses[2].reshape(16, S // 16, HD)]
    res = pl.pallas_call(
        body, name="combine_fwd", grid=(S // ts,),
        in_specs=_perm_specs(ts, E) + _perm_specs(ts, HD) + [wide, pl.BlockSpec((HD, E), lambda i: (0, 0))],
        out_specs=_perm_specs(ts, E) + _perm_specs(ts, HD) + [wide, pl.BlockSpec((E, ts), lambda i: (0, i))],
        out_shape=_perm_shapes(S, E, BF16) + _perm_shapes(S, HD, F32) + [SDS((S, E), BF16), SDS((E, S), BF16)],
        scratch_shapes=[_scr(ts, E), _scr(ts, E), _scr(ts, HD), _scr(ts, HD), _scr(ts, E), _scr(ts, HD)],
        compiler_params=_cp(("parallel",), 56),
    )(*os3, *ls3, z, ehot)
    ys = [res[0], res[1].reshape(S, E), res[2].reshape(S, E)]
    lse3 = [res[3], res[4].reshape(S, HD), res[5].reshape(S, HD)]
    return ys, lse3, res[6], res[7]


def mm_dya(da0, w_out, z, y):
    S = da0.shape[0]
    tm = 512

    def body(a_ref, w_ref, z_ref, y_ref, dy0, dy1, dy2, dz_ref, scr):
        for cidx in range(E // 256):
            col = slice(cidx * 256, (cidx + 1) * 256)
            dya = _dot_nt(a_ref[...], w_ref[col, :])
            zz = z_ref[:, col].astype(F32)
            sig = _sigmoid(zz)
            dy = dya * zz * sig
            scr[2 * cidx] = dy[:, :LANES]
            scr[2 * cidx + 1] = dy[:, LANES:]
            dy0[:, col] = dy.astype(BF16)
            dz_ref[:, col] = (dya * y_ref[:, col].astype(F32) * sig * (1.0 + zz * (1.0 - sig))).astype(BF16)
        _store_perm(dy1, scr, 4)
        _store_perm(dy2, scr, 16)

    wide = pl.BlockSpec((tm, E), lambda i: (i, 0))
    res = pl.pallas_call(
        body, name="mm_dya", grid=(S // tm,),
        in_specs=[pl.BlockSpec((tm, D), lambda i: (i, 0)), pl.BlockSpec((E, D), lambda i: (0, 0)), wide, wide],
        out_specs=_perm_specs(tm, E) + [wide],
        out_shape=_perm_shapes(S, E, BF16) + [SDS((S, E), BF16)],
        scratch_shapes=[_scr(tm, E)],
        compiler_params=_cp(("parallel",), 48),
    )(da0, w_out, z, y)
    return [res[0], res[1].reshape(S, E), res[2].reshape(S, E)], res[3]


def attn_bwd(P, dy, y, lse, tabs, g, d):
    S = P.shape[0]
    L = S // d
    T = min(512, L)
    nq = T // QB
    ni = L // T

    def body(qkv_ref, kh_ref, vh_ref, dy_ref, y_ref, lse_ref, c_ref, s1_ref, s2_ref,
             o_ref, dkc_ref, dvc_ref, ext):
        i = pl.program_id(1)
        _, mask, mask_first = _band_masks(i < ni - 1)
        _fill_kv(ext, qkv_ref, kh_ref, vh_ref)
        row_id2 = lax.broadcasted_iota(jnp.int32, (2 * QB, QB), 0)
        ones = jnp.ones((QB, QB), BF16)
        lse_hl = []
        for j in range(nq):
            t = lse_ref[j * QB:(j + 1) * QB, :]
            hi = t.astype(BF16)
            lse_hl.append(jnp.concatenate([hi, (t - hi.astype(F32)).astype(BF16)], axis=1))

        @pl.when(i == 0)
        def _():
            dkc_ref[...] = jnp.zeros_like(dkc_ref)
            dvc_ref[...] = jnp.zeros_like(dvc_ref)

        def heads(hp, carry):
            def cols(h):
                off = pl.multiple_of(h * HD, HD)
                return pl.ds(off, HD), pl.ds(E + off, HD), pl.ds(2 * E + off, HD)

            def front(h, j):
                cq, ck, _ = cols(h)
                rows = slice(j * QB, (j + 1) * QB)
                krows = slice(j * QB, (j + 2) * QB)
                dyj = dy_ref[rows, cq]
                sel = jnp.logical_or(row_id2 == h, row_id2 == h + QB).astype(BF16)
                lse_b = _dot(lse_hl[j], sel)
                delta_b = _dot((dyj.astype(F32) * y_ref[rows, cq].astype(F32)).astype(BF16), ones)
                s = _dot_nt(qkv_ref[rows, cq], ext[krows, cq])
                p = jnp.exp(s + (mask_first if j == 0 else mask) - jnp.concatenate([lse_b, lse_b], axis=1))
                ds = (p * (_dot_nt(dyj, ext[krows, ck]) - jnp.concatenate([delta_b, delta_b], axis=1))).astype(BF16)
                return ds, jnp.concatenate([ds, p.astype(BF16)], axis=0).T

            def back(h, j, ds, dsp_t, pend_dk, pend_dv):
                cq, ck, cv = cols(h)
                rows = slice(j * QB, (j + 1) * QB)
                krows = slice(j * QB, (j + 2) * QB)
                dq = _dot(ds, ext[krows, cq]) * SCALE
                zero = jnp.zeros((QB, HD), BF16)
                bd = jnp.concatenate([jnp.concatenate([qkv_ref[rows, cq], zero], axis=1),
                                      jnp.concatenate([zero, dy_ref[rows, cq]], axis=1)], axis=0)
                dkv = _dot(dsp_t, bd)
                dk2, dv2 = dkv[:, :HD], dkv[:, HD:]
                c, s1, s2 = c_ref[rows, :], s1_ref[rows, :], s2_ref[rows, :]
                o_ref[rows, cq] = _unrope(dq, c, s1, s2).astype(BF16)
                o_ref[rows, ck] = _unrope(dk2[QB:] + pend_dk, c, s1, s2).astype(BF16)
                o_ref[rows, cv] = (dv2[QB:] + pend_dv).astype(BF16)
                return dk2[:QB], dv2[:QB]

            items = [(HEADS_BWD * hp + hh, j) for hh in range(HEADS_BWD) for j in reversed(range(nq))]
            queue = [front(*it) for it in items[:AHEAD]]
            pend = None
            for u, (h, j) in enumerate(items):
                if u + AHEAD < len(items):
                    queue.append(front(*items[u + AHEAD]))
                if j == nq - 1:
                    pend = (dkc_ref[:, cols(h)[0]], dvc_ref[:, cols(h)[0]])
                pend = back(h, j, *queue.pop(0), *pend)
                if j == 0:
                    dkc_ref[:, cols(h)[0]], dvc_ref[:, cols(h)[0]] = pend
            return carry

        lax.fori_loop(0, NH // HEADS_BWD, heads, 0)

    blk = lambda r, i: r * ni + ni - 1 - i
    halo = lambda r, i: jnp.maximum(r * (L // QB) + (ni - 1 - i) * nq - 1, 0)
    main = pl.BlockSpec((T, SEG), lambda r, i: (blk(r, i), 0))
    wide = pl.BlockSpec((T, E), lambda r, i: (blk(r, i), 0))
    narrow = pl.BlockSpec((T, HD), lambda r, i: (blk(r, i), 0))
    return pl.pallas_call(
        body, name=f"attn_bwd{g}", grid=(d, ni),
        in_specs=[main, pl.BlockSpec((QB, E), lambda r, i: (halo(r, i), 1)),
                  pl.BlockSpec((QB, E), lambda r, i: (halo(r, i), 2)),
                  wide, wide, narrow, narrow, narrow, narrow],
        out_specs=main, out_shape=SDS((S, SEG), BF16),
        scratch_shapes=[pltpu.VMEM((QB, E), F32), pltpu.VMEM((QB, E), F32), pltpu.VMEM((T + QB, 2 * E), BF16)],
        compiler_params=_cp(("arbitrary", "arbitrary"), 56),
    )(P, P, P, dy, y, lse, *tabs)


def _pool_cnt(t0, rows):
    t = (lax.broadcasted_iota(jnp.int32, (rows, E), 0) + t0 + 1).astype(F32)
    ch = lax.broadcasted_iota(jnp.int32, (rows, E), 1)
    w = jnp.where(ch < PC, 2.0, jnp.where(ch < 2 * PC, 4.0, jnp.where(ch < 3 * PC, 8.0, 16.0)))
    return jnp.minimum(t, w)


def _by_group(parts):
    return jnp.concatenate([parts[g][:, g * PC:(g + 1) * PC] for g in range(4)], axis=1)


def pool_fwd(uz):
    S = uz.shape[0]
    ts = 256

    def body(u_ref, h_ref, o_ref, ot_ref):
        i = pl.program_id(0)
        u = u_ref[...]
        halo = jnp.where(i > 0, h_ref[...], 0.0)
        ext = jnp.concatenate([halo, u], axis=0)
        s2 = ext + pltpu.roll(ext, 1, 0)
        s4 = s2 + pltpu.roll(s2, 2, 0)
        s8 = s4 + pltpu.roll(s4, 4, 0)
        s16 = s8 + pltpu.roll(s8, 8, 0)
        win = _by_group([s2, s4, s8, s16])[16:, :]
        pooled = win / _pool_cnt(i * ts, ts) - u
        o_ref[...] = pooled.astype(BF16)
        ot_ref[...] = pooled.T.astype(BF16)

    return pl.pallas_call(
        body, name="pool_fwd", grid=(S // ts,),
        in_specs=[pl.BlockSpec((ts, E), lambda i: (i, 0)),
                  pl.BlockSpec((16, E), lambda i: (jnp.maximum(i * (ts // 16) - 1, 0), 0))],
        out_specs=[pl.BlockSpec((ts, E), lambda i: (i, 0)), pl.BlockSpec((E, ts), lambda i: (0, i))],
        out_shape=[SDS((S, E), BF16), SDS((E, S), BF16)],
        compiler_params=_cp(("parallel",), 48),
    )(uz, uz)


def pool_bwd(dpooled, duz):
    S = dpooled.shape[0]
    ts = 256
    nt = S // ts

    def body(d_ref, h_ref, alias_ref, o_ref):
        i = pl.program_id(0)
        dp = d_ref[...].astype(F32)
        halo = jnp.where(i < nt - 1, h_ref[...].astype(F32), 0.0)
        n = ts + 16
        ext = jnp.concatenate([dp, halo], axis=0) / _pool_cnt(i * ts, n)
        f2 = ext + pltpu.roll(ext, n - 1, 0)
        f4 = f2 + pltpu.roll(f2, n - 2, 0)
        f8 = f4 + pltpu.roll(f4, n - 4, 0)
        f16 = f8 + pltpu.roll(f8, n - 8, 0)
        win = _by_group([f2, f4, f8, f16])[:ts, :]
        o_ref[...] = (win - dp).astype(BF16)

    return pl.pallas_call(
        body, name="pool_bwd", grid=(nt,),
        in_specs=[pl.BlockSpec((ts, E), lambda i: (i, 0)),
                  pl.BlockSpec((16, E), lambda i: (jnp.minimum((i + 1) * (ts // 16), S // 16 - 1), 0)), ANY],
        out_specs=pl.BlockSpec((ts, E), lambda i: (i, 0)),
        out_shape=SDS(duz.shape, BF16),
        input_output_aliases={2: 0},
        compiler_params=_cp(("parallel",), 48),
    )(dpooled, dpooled, duz)


def mm_grp(pooled, wg, b, scale, uz):
    S = pooled.shape[0]
    tm = 512

    def body(p_ref, w_ref, b_ref, s_ref, z_ref, h_ref, y_ref, yt_ref):
        for g in range(4):
            cs = slice(g * PC, (g + 1) * PC)
            h = _dot(p_ref[:, cs], w_ref[g]) + b_ref[:, cs]
            z = z_ref[:, cs]
            yp = h * s_ref[:, cs] * (z * _sigmoid(z))
            h_ref[:, cs] = h.astype(BF16)
            y_ref[:, cs] = yp.astype(BF16)
            yt_ref[cs, :] = yp.T.astype(BF16)

    row = pl.BlockSpec((tm, E), lambda i: (i, 0))
    vec = pl.BlockSpec((1, E), lambda i: (0, 0))
    return pl.pallas_call(
        body, name="mm_grp", grid=(S // tm,),
        in_specs=[row, pl.BlockSpec((4, PC, PC), lambda i: (0, 0, 0)), vec, vec,
                  pl.BlockSpec((tm, E), lambda i: (i, 1))],
        out_specs=[row, row, pl.BlockSpec((E, tm), lambda i: (0, i))],
        out_shape=[SDS((S, E), BF16), SDS((S, E), BF16), SDS((E, S), BF16)],
        compiler_params=_cp(("parallel",), 48),
    )(pooled, wg, b, scale, uz)


def mm_dyp(da1, w_out, uz, h, scale):
    S = da1.shape[0]
    tm = 512

    def body(a_ref, w_ref, z_ref, h_ref, s_ref, dh_ref, dz_ref, dsc_ref, db_ref):
        @pl.when(pl.program_id(0) == 0)
        def _():
            dsc_ref[...] = jnp.zeros_like(dsc_ref)
            db_ref[...] = jnp.zeros_like(db_ref)

        for cidx in range(E // 256):
            col = slice(cidx * 256, (cidx + 1) * 256)
            dyp = _dot_nt(a_ref[...], w_ref[col, :])
            z = z_ref[:, col]
            hh = h_ref[:, col].astype(F32)
            sc = s_ref[:, col]
            sig = _sigmoid(z)
            dhs = dyp * z * sig
            dz_ref[:, col] = (dyp * hh * sc * sig * (1.0 + z * (1.0 - sig))).astype(BF16)
            dh = dhs * sc
            dh_ref[:, col] = dh.astype(BF16)
            dsc_ref[:, col] += _fold8(dhs * hh)
            db_ref[:, col] += _fold8(dh)

    row = pl.BlockSpec((tm, E), lambda i: (i, 0))
    acc = pl.BlockSpec((8, E), lambda i: (0, 0))
    return pl.pallas_call(
        body, name="mm_dyp", grid=(S // tm,),
        in_specs=[pl.BlockSpec((tm, D), lambda i: (i, 0)), pl.BlockSpec((E, D), lambda i: (0, 0)),
                  pl.BlockSpec((tm, E), lambda i: (i, 1)), row, pl.BlockSpec((1, E), lambda i: (0, 0))],
        out_specs=[row, pl.BlockSpec((tm, E), lambda i: (i, 1)), acc, acc],
        out_shape=[SDS((S, E), BF16), SDS((S, 2 * E), BF16), SDS((8, E), F32), SDS((8, E), F32)],
        compiler_params=_cp(("arbitrary",), 48),
    )(da1, w_out, uz, h, scale)


def mm_dpooled(dh, wg):
    S = dh.shape[0]
    tm = 1024

    def body(a_ref, w_ref, o_ref):
        for g in range(4):
            cs = slice(g * PC, (g + 1) * PC)
            o_ref[:, cs] = _dot_nt(a_ref[:, cs], w_ref[g]).astype(BF16)

    row = pl.BlockSpec((tm, E), lambda i: (i, 0))
    return pl.pallas_call(
        body, name="mm_dpooled", grid=(S // tm,),
        in_specs=[row, pl.BlockSpec((4, PC, PC), lambda i: (0, 0, 0))],
        out_specs=row, out_shape=SDS((S, E), BF16),
        compiler_params=_cp(("parallel",), 48),
    )(dh, wg)


def _rope_tables(positions):
    inv_freq = 500000.0 ** (-jnp.arange(0, 32, 2, dtype=F32) / 32)
    S = positions.shape[0]
    ang = jnp.repeat(positions.astype(F32).reshape(S // 8, 8), 16, axis=1) * jnp.tile(inv_freq, 8)
    cos, sin = lax.optimization_barrier((jnp.cos(ang), jnp.sin(ang)))
    cos, sin = cos.reshape(S, 16), sin.reshape(S, 16)
    one = jnp.ones((S, HD - 32), F32)
    zero16 = jnp.zeros((S, 16), F32)
    zero = jnp.zeros((S, HD - 32), F32)
    c = jnp.concatenate([cos, cos, one], axis=1)
    s1 = jnp.concatenate([-sin, zero16, zero], axis=1)
    s2 = jnp.concatenate([zero16, sin, zero], axis=1)
    return c.astype(BF16), s1.astype(BF16), s2.astype(BF16)


def kernel(x, positions, norm_pre, norm_post, attn_w_in, attn_w_out, pool_w_in, pool_w_grp, pool_b_grp, pool_scale, pool_w_out, loss_target, m_norm_pre, m_norm_post, m_attn_w_in, m_attn_w_out, m_pool_w_in, m_pool_w_grp, m_pool_b_grp, m_pool_scale, m_pool_w_out, v_norm_pre, v_norm_post, v_attn_w_in, v_attn_w_out, v_pool_w_in, v_pool_w_grp, v_pool_b_grp, v_pool_scale, v_pool_w_out):
    S = x.shape[1]
    xi, yi, ci = _mesh_pos()
    dev = 4 * xi + 2 * yi + ci
    x2 = x[0]
    tgt = loss_target[0]

    small = jnp.concatenate([pool_b_grp[0].reshape(2, HD), pool_scale[0].reshape(2, HD),
                             jnp.zeros((4, HD), F32)], axis=0)
    w_in_l = attn_w_in[0].astype(BF16)
    hop1, hop1_token = split_start("gather_w_in_start", [w_in_l], [lax.empty((N_DEV,) + w_in_l.shape, BF16)],
                                   _hop1_plan(), 3)

    pos = positions[0]
    tabs = [_rope_tables(pos.reshape(S // d, d).T.reshape(S)) for d in DIL]
    ehot = (jnp.arange(E)[None, :] // HD == jnp.arange(HD)[:, None]).astype(BF16)
    seg_tiles = SEG // CT

    xn0, xn0_4, xn0_16, xn0t = norm_pre0(x2, norm_pre[0:1], hop1_token)
    xn0s = [xn0, xn0_4.reshape(S, D), xn0_16.reshape(S, D)]
    xn0ts = [xn0t, transpose_rows(xn0s[1], "xn0t_4"), transpose_rows(xn0s[2], "xn0t_16")]

    (w_in_l,), (w_in8,) = split_wait("gather_w_in_wait", hop1, _hop1_plan(), xn0ts[2])
    hop2, hop2_token = split_start("gather_w_in_hop2_start", [w_in8], None, _hop2_plan(D), 4)
    _, (w_in8,) = split_wait("gather_w_in_hop2_wait", hop2, _hop2_plan(D), hop2_token, inplace=True)
    hop3, hop3_token = split_start("gather_w_in_hop3_start", [w_in8], None, _hop3_plan(), 1)
    _, (w_in8,) = split_wait("gather_w_in_hop3_wait", hop3, _hop3_plan(), hop3_token, inplace=True)
    w_in8 = lax.dynamic_update_slice(w_in8, w_in_l[None], (dev, 0, 0))
    small, w_in8 = lax.optimization_barrier((small, w_in8))
    rest_l = [attn_w_out[0].astype(BF16), pool_w_in[0].astype(BF16), pool_w_grp[0].astype(BF16),
              pool_w_out[0].astype(BF16), small]
    rest_flight, rest_token = split_start(
        "gather_rest_start", rest_l, [lax.empty((N_DEV,) + a.shape, a.dtype) for a in rest_l], _peers_plan(), 7)
    Ps, os_, lses = [], [], []
    for g, d in enumerate(DIL):
        P = mm_in(xn0s[g], w_in8, g * seg_tiles, seg_tiles, tabs[g], f"mm_qkv{g}", after=rest_token)
        o, l = attn_fwd(P, g, d)
        Ps.append(P)
        os_.append(o)
        lses.append(l)
    z0 = mm_in(xn0, w_in8, 3 * seg_tiles, E // CT, None, "mm_z0")
    ys, lse3, ya, yat = combine_fwd(os_, lses, z0, ehot)

    rest_l, rest8 = split_wait("gather_rest_wait", rest_flight, _peers_plan(), ya)
    rest8 = [lax.dynamic_update_slice(r8, a[None], (dev,) + (0,) * a.ndim) for r8, a in zip(rest8, rest_l)]
    w_out8, wp_in8, wg8, wp_out8, small8 = rest8
    w_out = w_out8.reshape(E, D)
    wp_out = wp_out8.reshape(E, D)
    wp_in = wp_in8.transpose(1, 0, 2).reshape(D, 2 * E)
    wg = wg8.transpose(1, 0, 2, 3).reshape(4, PC, PC)
    b_full = small8[:, 0:2, :].reshape(N_DEV, 4, PC // N_DEV).transpose(1, 0, 2).reshape(1, E)
    scale_full = small8[:, 2:4, :].reshape(1, E)
    a0 = mm_rows(ya, w_out, "mm_out0", F32)
    h1, xn1, xn1t = post0_pre1(x2, a0, norm_post[0:1], norm_pre[1:2])

    uz = mm_rows(xn1, wp_in, "mm_uz", F32, tm=512)
    pooled, pooled_t = pool_fwd(uz)
    hgrp, yp, ypt = mm_grp(pooled, wg, b_full, scale_full, uz)
    a1 = mm_rows(yp, wp_out, "mm_out1", F32)
    dh2, da1, loss_rows, dg_post1 = post1_loss(h1, a1, tgt, norm_post[1:2])
    loss = lax.psum(0.5 / D * jnp.sum(loss_rows), ("x", "y", "c"))

    dh, duz, dscale_p, db_p = mm_dyp(da1, wp_out, uz, hgrp, scale_full)
    dpooled = mm_dpooled(dh, wg)
    duz = pool_bwd(dpooled, duz)
    g_wg = mm_dwg(pooled_t, dh)
    g_wp_out = mm_wgrad_rows(ypt, da1, "mm_dwp_out")
    g_wp_in = mm_wgrad_cols(xn1t, duz, "mm_dwp_in", shard=PC)
    dxn1 = mm_dx_full(duz, wp_in, "mm_dxn1")
    dh1, da0, dg_pre1, dg_post0 = mid_bwd(dxn1, dh2, h1, a0, norm_pre[1:2], norm_post[0:1])

    dys, dz0 = mm_dya(da0, w_out, z0, ys[0])
    g_w_out = mm_wgrad_rows(yat, da0, "mm_dw_out")
    g_w_in = mm_dw_in_part(xn0t, dz0, 3 * seg_tiles, None, "mm_dw_in_z")
    dPs = []
    for g, d in enumerate(DIL):
        dP = attn_bwd(Ps[g], dys[g], ys[g], lse3[g], tabs[g], g, d)
        g_w_in = mm_dw_in_part(xn0ts[g], dP, g * seg_tiles, g_w_in, f"mm_dw_in{g}")
        dPs.append(dP)

    cidx = ci.astype(jnp.int32).reshape(1)
    chip = (2 * xi + yi).astype(jnp.int32).reshape(1)
    fulls = [g_w_in, g_w_out.reshape(N_DEV, E // N_DEV, D), g_wp_in,
             g_wg.reshape(4, N_DEV, PC // N_DEV, PC).transpose(1, 0, 2, 3).reshape(N_DEV, 4 * PC // N_DEV, PC),
             g_wp_out.reshape(N_DEV, E // N_DEV, D)]
    pair_flight, pair_token = split_start(
        "rs_pair_start", fulls, [lax.empty((4,) + f.shape[1:], F32) for f in fulls], _pair_plan(), 4)
    dx_z = mm_dx_part(dz0, w_in8, 3 * seg_tiles, "mm_dxn0_z", after=pair_token)
    dx_0 = mm_dx_part(dPs[0], w_in8, 0, "mm_dxn0_0", after=dx_z)
    fulls, sibs = split_wait("rs_pair_wait", pair_flight, _pair_plan(), dx_0)
    parts = [pair_add(f, s, cidx, f"pair_add{k}") for k, (f, s) in enumerate(zip(fulls, sibs))]
    chips_flight, chips_token = split_start(
        "rs_chips_start", parts, [jnp.zeros(p.shape, BF16) for p in parts], _chips_plan(), 3)
    dx_1 = mm_dx_part(dPs[1], w_in8, seg_tiles, "mm_dxn0_1", after=chips_token)
    dx_2 = mm_dx_part(dPs[2], w_in8, 2 * seg_tiles, "mm_dxn0_2", after=dx_1)
    grad_x, dg_pre0 = pre0_bwd(dx_0, dx_z, dx_1, dx_2, dh1, x2, norm_pre[0:1])
    parts, recvs = split_wait("rs_chips_wait", chips_flight, _chips_plan(), grad_x)
    shards = [(attn_w_in, m_attn_w_in, v_attn_w_in), (attn_w_out, m_attn_w_out, v_attn_w_out),
              (pool_w_in, m_pool_w_in, v_pool_w_in), (pool_w_grp, m_pool_w_grp, v_pool_w_grp),
              (pool_w_out, m_pool_w_out, v_pool_w_out)]
    big = []
    for k, (recv, part, (w, m, v)) in enumerate(zip(recvs, parts, shards)):
        shp = w.shape
        r2 = recv.shape[1:]
        res = adamw_sum(recv, part, chip, w.reshape(r2), m.reshape(r2), v.reshape(r2), f"adamw{k}")
        big.append([t.reshape(shp) for t in res])

    smalls = jnp.concatenate([dg_pre0.sum(0, keepdims=True), dg_pre1.sum(0, keepdims=True),
                              dg_post0.sum(0, keepdims=True), dg_post1.sum(0, keepdims=True),
                              db_p.sum(0).reshape(2, D), dscale_p.sum(0).reshape(2, D)], axis=0)
    (smalls8,) = all_gather([smalls], "gather_small_grads")
    tot = sum_slots(smalls8, "sum_small_grads")
    g_norm_pre, g_norm_post = tot[0:2], tot[2:4]
    g_b = lax.dynamic_slice_in_dim(tot[4:6].reshape(4, PC), dev * (PC // N_DEV), PC // N_DEV, axis=1)[None]
    g_scale = lax.dynamic_slice_in_dim(tot[6:8].reshape(1, E), dev * (E // N_DEV), E // N_DEV, axis=1)
    sm = [adamw_small(g_norm_pre, norm_pre, m_norm_pre, v_norm_pre, "adamw_norm_pre"),
          adamw_small(g_norm_post, norm_post, m_norm_post, v_norm_post, "adamw_norm_post"),
          adamw_small(g_b, pool_b_grp, m_pool_b_grp, v_pool_b_grp, "adamw_b"),
          adamw_small(g_scale, pool_scale, m_pool_scale, v_pool_scale, "adamw_scale")]

    grads = [g_norm_pre, g_norm_post, big[0][0], big[1][0], big[2][0], big[3][0], g_b, g_scale, big[4][0]]

    def pick(k):
        return [sm[0][k - 1], sm[1][k - 1], big[0][k], big[1][k], big[2][k], big[3][k], sm[2][k - 1], sm[3][k - 1],
                big[4][k]]

    return (loss, grad_x[None], *grads, *pick(1), *pick(2), *pick(3))
```

```python
import math

import jax
import jax.numpy as jnp
from jax import lax
from jax.experimental import pallas as pl
from jax.experimental.pallas import tpu as pltpu

F32 = jnp.float32
BF16 = jnp.bfloat16
SDS = jax.ShapeDtypeStruct

N_DEV = 8
D = 1024
E = 2048
HD = 128
NH = E // HD
DIL = (1, 4, 16)
QB = 128
SEG = 3 * E
W_IN_COLS = 3 * SEG + E
W_SHARD = W_IN_COLS // N_DEV
CT = 512
PC = E // 4
EPS = 1e-6
NEG = -1e30
SCALE = 1.0 / math.sqrt(HD)
LR, B1, B2, ADAM_EPS, WD, STEP = 0.001, 0.9, 0.999, 1e-08, 0.01, 10
MIB = 1024 * 1024
ANY = pl.BlockSpec(memory_space=pl.ANY)
MESH = pl.DeviceIdType.MESH


def _cp(sem, mb):
    return pltpu.CompilerParams(dimension_semantics=sem, vmem_limit_bytes=mb * MIB)


def _dot(a, b):
    return jnp.dot(a, b, preferred_element_type=F32)


def _dot_nt(a, b):
    return lax.dot_general(a, b, (((1,), (1,)), ((), ())), preferred_element_type=F32)


def _rms(h):
    return lax.rsqrt(jnp.mean(h * h, axis=-1, keepdims=True) + EPS)


def _row_tile(R, C, budget):
    tr = R
    while tr * C * 4 > budget and tr % 16 == 0:
        tr //= 2
    return tr


def _fold8(t):
    return t.reshape(t.shape[0] // 8, 8, t.shape[1]).sum(axis=0)


def _sigmoid(z):
    return pl.reciprocal(1.0 + jnp.exp(-z), approx=True)


LANES = 128


def _scr(rows, C):
    return pltpu.VMEM((C // LANES, rows, LANES), F32)


def _scr_put(scr, val):
    for c in range(scr.shape[0]):
        scr[c] = val[:, c * LANES:(c + 1) * LANES]


def _scr_get(scr):
    return jnp.concatenate([scr[c] for c in range(scr.shape[0])], axis=1)


def _store_perm(dst_ref, scr, d):
    n = dst_ref.shape[1]
    for r in range(d):
        for c in range(scr.shape[0]):
            dst_ref[r, :, c * LANES:(c + 1) * LANES] = scr[c, pl.ds(r, n, stride=d), :].astype(dst_ref.dtype)


def _load_perm(scr, src_ref, d, add=False):
    n = src_ref.shape[1]
    for r in range(d):
        rows = pl.ds(r, n, stride=d)
        for c in range(scr.shape[0]):
            v = src_ref[r, :, c * LANES:(c + 1) * LANES].astype(F32)
            scr[c, rows, :] = scr[c, rows, :] + v if add else v


def _rope(t, c, s1, s2):
    t = t.astype(BF16)
    return t * c + pltpu.roll(t, HD - 16, 1) * s1 + pltpu.roll(t, 16, 1) * s2


def _unrope(t, c, s1, s2):
    t = t.astype(BF16)
    return t * c - pltpu.roll(t, HD - 16, 1) * s1 - pltpu.roll(t, 16, 1) * s2


def _mesh_pos():
    return lax.axis_index("x"), lax.axis_index("y"), lax.axis_index("c")


def all_gather(arrs, name):
    n = len(arrs)

    def body(*refs):
        ins, outs = refs[:n], refs[n:2 * n]
        send_sems, recv_sems, local_sems = refs[2 * n:]
        x, y, c = _mesh_pos()
        me, sib = (x, y, c), (x, y, 1 - c)
        chips = [(1 - x, y), (x, 1 - y), (1 - x, 1 - y)]

        def slot(p):
            return 4 * p[0] + 2 * p[1] + p[2]

        def copy(a, k, block, to, src=None):
            dst = outs[a].at[slot(block)]
            return pltpu.make_async_remote_copy(
                src_ref=dst if src is None else src, dst_ref=dst,
                send_sem=send_sems.at[a, k], recv_sem=recv_sems.at[a, k],
                device_id=to, device_id_type=MESH)

        mine = [pltpu.make_async_copy(ins[a], outs[a].at[slot(me)], local_sems.at[a]) for a in range(n)]
        for cp in mine:
            cp.start()
        first = []
        for a in range(n):
            first.append(copy(a, 0, me, sib, src=ins[a]))
            for j, chip in enumerate(chips):
                first.append(copy(a, 1 + j, me, (*chip, c), src=ins[a]))
        for cp in first:
            cp.start()
        passed = []
        for j, chip in enumerate(chips):
            for a in range(n):
                copy(a, 1 + j, (*chip, c), me).wait_recv()
                fw = copy(a, 4 + j, (*chip, c), sib)
                fw.start()
                passed.append(fw)
        for a in range(n):
            copy(a, 0, sib, me).wait_recv()
        for j, chip in enumerate(chips):
            for a in range(n):
                copy(a, 4 + j, (*chip, 1 - c), me).wait_recv()
        for cp in first + passed:
            cp.wait_send()
        for cp in mine:
            cp.wait()

    return pl.pallas_call(
        body, name=name,
        out_shape=[SDS((N_DEV,) + a.shape, a.dtype) for a in arrs],
        in_specs=[ANY] * n, out_specs=[ANY] * n,
        scratch_shapes=[pltpu.SemaphoreType.DMA((n, 7)), pltpu.SemaphoreType.DMA((n, 7)),
                        pltpu.SemaphoreType.DMA((n,))],
    )(*arrs)


HBM_SPEC = pl.BlockSpec(memory_space=pltpu.HBM)
SEM_SPEC = pl.BlockSpec(memory_space=pltpu.SEMAPHORE)
EFFECT = pltpu.SideEffectType.DATAFLOW_SIDE_EFFECTING


def _pair_plan():
    def plan(x, y, c):
        return [(2 * q + (1 - c), q, (x, y, 1 - c)) for q in range(4)]
    return plan


def _chips_plan():
    def plan(x, y, c):
        chips = [(1 - x, y), (x, 1 - y), (1 - x, 1 - y)]
        return [(2 * cx + cy, 2 * x + y, (cx, cy, c)) for cx, cy in chips]
    return plan


def _hop1_plan():
    def plan(x, y, c):
        me = 4 * x + 2 * y + c
        return [(None, me, (x, y, 1 - c)), (None, me, (1 - x, y, c)), (None, me, (x, 1 - y, c))]
    return plan


def _hop2_plan(rows):
    half = rows // 2

    def plan(x, y, c):
        sx, sy = 4 * (1 - x) + 2 * y + c, 4 * x + 2 * (1 - y) + c
        top, bottom = pl.ds(0, half), pl.ds(half, half)
        return [((sx, top), (sx, top), (x, 1 - y, c)), ((sy, bottom), (sy, bottom), (1 - x, y, c)),
                (sx, sx, (x, y, 1 - c)), (sy, sy, (x, y, 1 - c))]
    return plan


def _hop3_plan():
    def plan(x, y, c):
        sd = 4 * (1 - x) + 2 * (1 - y) + c
        return [(sd, sd, (x, y, 1 - c))]
    return plan


def _peers_plan():
    def plan(x, y, c):
        out = []
        for k in range(1, N_DEV):
            fx, fy, fc = (k >> 2) & 1, (k >> 1) & 1, k & 1
            px, py, pc = (x + fx) % 2, (y + fy) % 2, (c + fc) % 2
            out.append((None, 4 * x + 2 * y + c, (px, py, pc)))
        return out
    return plan


def _split_copies(plan, srcs, lands, send_sems, recv_sems):
    x, y, c = _mesh_pos()
    cps = []
    for a, (src, land) in enumerate(zip(srcs, lands)):
        steps = plan(x, y, c)
        for k, (si, li, to) in enumerate(steps):
            sem = a * len(steps) + k
            cps.append(pltpu.make_async_remote_copy(
                src_ref=src if si is None else src.at[si], dst_ref=land.at[li],
                send_sem=send_sems.at[sem], recv_sem=recv_sems.at[sem],
                device_id=to, device_id_type=MESH))
    return cps


def split_start(name, srcs, lands, plan, nk):
    n = len(srcs)
    ops = list(srcs) + ([] if lands is None else list(lands))
    nb = len(ops)

    def body(*refs):
        token = refs[-1]
        for cp in _split_copies(plan, refs[:n], refs[nb - n:nb], refs[nb], refs[nb + 1]):
            cp.start()
        token[...] = jnp.zeros_like(token)

    ops = [pltpu.with_memory_space_constraint(a, pltpu.HBM) for a in ops]
    res = pl.pallas_call(
        body, name=name,
        out_shape=(pltpu.SemaphoreType.DMA((n * nk,)), pltpu.SemaphoreType.DMA((n * nk,)),
                   *[pltpu.HBM(a.shape, a.dtype) for a in ops], SDS((8, 128), F32)),
        in_specs=[HBM_SPEC] * nb,
        out_specs=(SEM_SPEC, SEM_SPEC, *[HBM_SPEC] * nb, pl.BlockSpec(memory_space=pltpu.VMEM)),
        input_output_aliases={i: 2 + i for i in range(nb)},
        compiler_params=pltpu.CompilerParams(has_side_effects=EFFECT),
    )(*ops)
    return res[:-1], res[-1]


def split_wait(name, flight, plan, after, inplace=False):
    send_sems, recv_sems = flight[0], flight[1]
    bufs = list(flight[2:])
    nb = len(bufs)
    n = nb if inplace else nb // 2

    def body(*refs):
        for cp in _split_copies(plan, refs[:n], refs[nb - n:nb], refs[nb], refs[nb + 1]):
            cp.wait_send()
            cp.wait_recv()

    res = pl.pallas_call(
        body, name=name,
        out_shape=[pltpu.HBM(a.shape, a.dtype) for a in bufs],
        in_specs=[HBM_SPEC] * nb + [SEM_SPEC, SEM_SPEC, ANY],
        out_specs=[HBM_SPEC] * nb,
        input_output_aliases={i: i for i in range(nb)},
        compiler_params=pltpu.CompilerParams(has_side_effects=EFFECT),
    )(*bufs, send_sems, recv_sems, after)
    return res[:n], res[nb - n:]


def pair_add(full, sib, cidx, name):
    _, R, C = full.shape
    tr = _row_tile(R, C, MIB)

    def body(c_ref, a_ref, b_ref, o_ref):
        o_ref[...] = (a_ref[...] + b_ref[...]).astype(BF16)

    return pl.pallas_call(
        body, name=name,
        grid_spec=pltpu.PrefetchScalarGridSpec(
            num_scalar_prefetch=1, grid=(4, R // tr),
            in_specs=[pl.BlockSpec((None, tr, C), lambda q, i, cr: (2 * q + cr[0], i, 0)),
                      pl.BlockSpec((None, tr, C), lambda q, i, cr: (q, i, 0))],
            out_specs=pl.BlockSpec((None, tr, C), lambda q, i, cr: (q, i, 0))),
        out_shape=SDS((4, R, C), BF16),
        compiler_params=_cp(("parallel", "parallel"), 32),
    )(cidx, full, sib)


def _adam_math(w, g, m, v):
    m2 = B1 * m + (1.0 - B1) * g
    v2 = B2 * v + (1.0 - B2) * (g * g)
    m_hat = m2 / (1.0 - B1 ** STEP)
    v_hat = v2 / (1.0 - B2 ** STEP)
    delta = -LR * (m_hat / (jnp.sqrt(v_hat) + ADAM_EPS) + WD * w)
    return delta, m2, v2


def adamw_sum(recv, part, chip, w, m, v, name):
    K, R, C = recv.shape
    tr = _row_tile(R, C, MIB)

    def body(chip_ref, r_ref, p_ref, w_ref, m_ref, v_ref, g_ref, d_ref, m2_ref, v2_ref):
        g = r_ref[0].astype(F32)
        for k in range(1, K):
            g = g + r_ref[k].astype(F32)
        g = g + p_ref[...].astype(F32)
        delta, m2, v2 = _adam_math(w_ref[...], g, m_ref[...], v_ref[...])
        g_ref[...] = g
        d_ref[...] = delta
        m2_ref[...] = m2
        v2_ref[...] = v2

    tile = pl.BlockSpec((tr, C), lambda i, cr: (i, 0))
    return pl.pallas_call(
        body, name=name,
        grid_spec=pltpu.PrefetchScalarGridSpec(
            num_scalar_prefetch=1, grid=(R // tr,),
            in_specs=[pl.BlockSpec((K, tr, C), lambda i, cr: (0, i, 0)),
                      pl.BlockSpec((None, tr, C), lambda i, cr: (cr[0], i, 0)), tile, tile, tile],
            out_specs=[tile] * 4),
        out_shape=[SDS((R, C), F32)] * 4,
        compiler_params=_cp(("parallel",), 32),
    )(chip, recv, part, w, m, v)


def adamw_small(g, w, m, v, name):
    def body(g_ref, w_ref, m_ref, v_ref, d_ref, m2_ref, v2_ref):
        delta, m2, v2 = _adam_math(w_ref[...], g_ref[...], m_ref[...], v_ref[...])
        d_ref[...] = delta
        m2_ref[...] = m2
        v2_ref[...] = v2

    return pl.pallas_call(body, name=name, out_shape=[SDS(w.shape, F32)] * 3)(g, w, m, v)


def sum_slots(a, name):
    K = a.shape[0]

    def body(a_ref, o_ref):
        t = a_ref[0]
        for k in range(1, K):
            t = t + a_ref[k]
        o_ref[...] = t

    return pl.pallas_call(body, name=name, out_shape=SDS(a.shape[1:], F32))(a)


def norm_pre0(x, g, after):
    S = x.shape[0]
    ts = 512

    def body(x_ref, g_ref, after_ref, o_ref, o4_ref, o16_ref, ot_ref, scr):
        h = x_ref[...]
        xn = h * _rms(h) * g_ref[...]
        o_ref[...] = xn.astype(BF16)
        ot_ref[...] = xn.T.astype(BF16)
        _scr_put(scr, xn)
        _store_perm(o4_ref, scr, 4)
        _store_perm(o16_ref, scr, 16)

    return pl.pallas_call(
        body, name="norm_pre0", grid=(S // ts,),
        in_specs=[pl.BlockSpec((ts, D), lambda i: (i, 0)), pl.BlockSpec((1, D), lambda i: (0, 0)), ANY],
        out_specs=[pl.BlockSpec((ts, D), lambda i: (i, 0)),
                   pl.BlockSpec((4, ts // 4, D), lambda i: (0, i, 0)),
                   pl.BlockSpec((16, ts // 16, D), lambda i: (0, i, 0)),
                   pl.BlockSpec((D, ts), lambda i: (0, i))],
        out_shape=[SDS((S, D), BF16), SDS((4, S // 4, D), BF16), SDS((16, S // 16, D), BF16), SDS((D, S), BF16)],
        scratch_shapes=[_scr(ts, D)],
        compiler_params=_cp(("parallel",), 32),
    )(x, g, after)


def transpose_rows(a, name):
    S, C = a.shape
    ts = 512

    def body(a_ref, o_ref):
        o_ref[...] = a_ref[...].astype(F32).T.astype(BF16)

    return pl.pallas_call(
        body, name=name, grid=(S // ts,),
        in_specs=[pl.BlockSpec((ts, C), lambda i: (i, 0))],
        out_specs=pl.BlockSpec((C, ts), lambda i: (0, i)),
        out_shape=SDS((C, S), BF16),
        compiler_params=_cp(("parallel",), 32),
    )(a)


def post0_pre1(x, a0, g_post, g_pre):
    S = x.shape[0]
    ts = 512

    def body(x_ref, a_ref, gp_ref, gn_ref, h_ref, o_ref, ot_ref):
        a = a_ref[...]
        h1 = x_ref[...] + a * _rms(a) * gp_ref[...]
        h_ref[...] = h1
        xn = h1 * _rms(h1) * gn_ref[...]
        o_ref[...] = xn.astype(BF16)
        ot_ref[...] = xn.T.astype(BF16)

    row = pl.BlockSpec((ts, D), lambda i: (i, 0))
    vec = pl.BlockSpec((1, D), lambda i: (0, 0))
    return pl.pallas_call(
        body, name="post0_pre1", grid=(S // ts,),
        in_specs=[row, row, vec, vec],
        out_specs=[row, row, pl.BlockSpec((D, ts), lambda i: (0, i))],
        out_shape=[SDS((S, D), F32), SDS((S, D), BF16), SDS((D, S), BF16)],
        compiler_params=_cp(("parallel",), 40),
    )(x, a0, g_post, g_pre)


def post1_loss(h1, a1, target, g_post):
    S = h1.shape[0]
    ts = 512

    def body(h_ref, a_ref, t_ref, g_ref, dh_ref, da_ref, loss_ref, dg_ref):
        @pl.when(pl.program_id(0) == 0)
        def _():
            loss_ref[...] = jnp.zeros_like(loss_ref)
            dg_ref[...] = jnp.zeros_like(dg_ref)

        a = a_ref[...]
        g = g_ref[...]
        rp = _rms(a)
        yhat = a * rp
        e = h_ref[...] + yhat * g - t_ref[...]
        loss_ref[...] += _fold8(e * e)
        dh = e * (1.0 / D)
        dh_ref[...] = dh
        dg_ref[...] += _fold8(dh * yhat)
        dyh = dh * g
        da = rp * (dyh - yhat * jnp.mean(dyh * yhat, axis=-1, keepdims=True))
        da_ref[...] = da.astype(BF16)

    row = pl.BlockSpec((ts, D), lambda i: (i, 0))
    acc = pl.BlockSpec((8, D), lambda i: (0, 0))
    return pl.pallas_call(
        body, name="post1_loss", grid=(S // ts,),
        in_specs=[row, row, row, pl.BlockSpec((1, D), lambda i: (0, 0))],
        out_specs=[row, row, acc, acc],
        out_shape=[SDS((S, D), F32), SDS((S, D), BF16), SDS((8, D), F32), SDS((8, D), F32)],
        compiler_params=_cp(("arbitrary",), 40),
    )(h1, a1, target, g_post)


def mid_bwd(dxn1, dh2, h1, a0, g_pre1, g_post0):
    S = h1.shape[0]
    ts = 512

    def body(dx_ref, dh2_ref, h_ref, a_ref, gn_ref, gp_ref, dh1_ref, da_ref, dgn_ref, dgp_ref):
        @pl.when(pl.program_id(0) == 0)
        def _():
            dgn_ref[...] = jnp.zeros_like(dgn_ref)
            dgp_ref[...] = jnp.zeros_like(dgp_ref)

        h = h_ref[...]
        r1 = _rms(h)
        xhat = h * r1
        dxn = dx_ref[...]
        dgn_ref[...] += _fold8(dxn * xhat)
        dxh = dxn * gn_ref[...]
        dh1 = dh2_ref[...] + r1 * (dxh - xhat * jnp.mean(dxh * xhat, axis=-1, keepdims=True))
        dh1_ref[...] = dh1
        a = a_ref[...]
        rp = _rms(a)
        yhat = a * rp
        dgp_ref[...] += _fold8(dh1 * yhat)
        dyh = dh1 * gp_ref[...]
        da = rp * (dyh - yhat * jnp.mean(dyh * yhat, axis=-1, keepdims=True))
        da_ref[...] = da.astype(BF16)

    row = pl.BlockSpec((ts, D), lambda i: (i, 0))
    vec = pl.BlockSpec((1, D), lambda i: (0, 0))
    acc = pl.BlockSpec((8, D), lambda i: (0, 0))
    return pl.pallas_call(
        body, name="mid_bwd", grid=(S // ts,),
        in_specs=[row, row, row, row, vec, vec],
        out_specs=[row, row, acc, acc],
        out_shape=[SDS((S, D), F32), SDS((S, D), BF16), SDS((8, D), F32), SDS((8, D), F32)],
        compiler_params=_cp(("arbitrary",), 48),
    )(dxn1, dh2, h1, a0, g_pre1, g_post0)


def pre0_bwd(dx_tok, dx_z, dx4, dx16, dh1, x, g_pre0):
    S = x.shape[0]
    ts = 512

    def body(da_ref, dz_ref, d4_ref, d16_ref, dh_ref, x_ref, g_ref, gx_ref, dg_ref, scr):
        @pl.when(pl.program_id(0) == 0)
        def _():
            dg_ref[...] = jnp.zeros_like(dg_ref)

        _scr_put(scr, da_ref[...].astype(F32) + dz_ref[...].astype(F32))
        _load_perm(scr, d4_ref, 4, add=True)
        _load_perm(scr, d16_ref, 16, add=True)
        h = x_ref[...]
        r = _rms(h)
        xhat = h * r
        dxn = _scr_get(scr)
        dg_ref[...] += _fold8(dxn * xhat)
        dxh = dxn * g_ref[...]
        gx_ref[...] = dh_ref[...] + r * (dxh - xhat * jnp.mean(dxh * xhat, axis=-1, keepdims=True))

    row = pl.BlockSpec((ts, D), lambda i: (i, 0))
    return pl.pallas_call(
        body, name="pre0_bwd", grid=(S // ts,),
        in_specs=[row, row, pl.BlockSpec((4, ts // 4, D), lambda i: (0, i, 0)),
                  pl.BlockSpec((16, ts // 16, D), lambda i: (0, i, 0)), row, row,
                  pl.BlockSpec((1, D), lambda i: (0, 0))],
        out_specs=[row, pl.BlockSpec((8, D), lambda i: (0, 0))],
        out_shape=[SDS((S, D), F32), SDS((8, D), F32)],
        scratch_shapes=[_scr(ts, D)],
        compiler_params=_cp(("arbitrary",), 48),
    )(dx_tok, dx_z, dx4.reshape(4, S // 4, D), dx16.reshape(16, S // 16, D), dh1, x, g_pre0)


def _w_tile(tile0):
    per = W_SHARD // CT
    return lambda t: ((tile0 + t) // per, 0, (tile0 + t) % per)


def mm_in(xn, w8, tile0, ntiles, tabs, name, after=None):
    S = xn.shape[0]
    tm = 2048
    wt = _w_tile(tile0)

    def body(a_ref, b_ref, *rest):
        o_ref = rest[-1]
        rc = 512
        for u in range(tm // rc):
            rows = slice(u * rc, (u + 1) * rc)
            r = _dot(a_ref[rows, :], b_ref[...])
            if tabs is None:
                o_ref[rows, :] = r.astype(BF16)
                continue
            c_ref, s1_ref, s2_ref = rest[:3]
            rot = pl.program_id(1) < 2 * E // CT
            qs = jnp.where(pl.program_id(1) < E // CT, SCALE, 1.0)
            c = jnp.where(rot, (c_ref[rows, :] * qs).astype(BF16), 1.0)
            s1 = jnp.where(rot, (s1_ref[rows, :] * qs).astype(BF16), 0.0)
            s2 = jnp.where(rot, (s2_ref[rows, :] * qs).astype(BF16), 0.0)
            for hh in range(CT // HD):
                cs = slice(hh * HD, (hh + 1) * HD)
                o_ref[rows, cs] = _rope(r[:, cs], c, s1, s2).astype(BF16)

    tab = pl.BlockSpec((tm, HD), lambda i, t: (i, 0))
    return pl.pallas_call(
        body, name=name, grid=(S // tm, ntiles),
        in_specs=[pl.BlockSpec((tm, D), lambda i, t: (i, 0)),
                  pl.BlockSpec((None, D, CT), lambda i, t: wt(t))] + ([] if tabs is None else [tab] * 3)
        + ([] if after is None else [ANY]),
        out_specs=pl.BlockSpec((tm, CT), lambda i, t: (i, t)),
        out_shape=SDS((S, ntiles * CT), BF16),
        compiler_params=_cp(("parallel", "parallel"), 48),
    )(xn, w8, *(() if tabs is None else tabs), *(() if after is None else (after,)))


def mm_rows(a, b, name, out_dtype, tm=1024):
    M, K = a.shape
    N = b.shape[1]

    def body(a_ref, b_ref, o_ref):
        for cidx in range(N // 256):
            col = slice(cidx * 256, (cidx + 1) * 256)
            o_ref[:, col] = _dot(a_ref[...], b_ref[:, col]).astype(out_dtype)

    return pl.pallas_call(
        body, name=name, grid=(M // tm,),
        in_specs=[pl.BlockSpec((tm, K), lambda i: (i, 0)), pl.BlockSpec((K, N), lambda i: (0, 0))],
        out_specs=pl.BlockSpec((tm, N), lambda i: (i, 0)),
        out_shape=SDS((M, N), out_dtype),
        compiler_params=_cp(("parallel",), 48),
    )(a, b)


def mm_acc(a, b, name, *, grid, a_spec, b_spec, o_spec, o_shape, acc_shape, write, vmem=48):
    nk = grid[-1]

    def body(a_ref, b_ref, o_ref, acc_ref):
        k = pl.program_id(len(grid) - 1)

        @pl.when(k == 0)
        def _():
            acc_ref[...] = jnp.zeros_like(acc_ref)

        acc_ref[...] += _dot(a_ref[...], b_ref[...])

        @pl.when(k == nk - 1)
        def _():
            write(o_ref, acc_ref)

    return pl.pallas_call(
        body, name=name, grid=grid, in_specs=[a_spec, b_spec], out_specs=o_spec, out_shape=o_shape,
        scratch_shapes=[pltpu.VMEM(acc_shape, F32)],
        compiler_params=_cp(("parallel",) * (len(grid) - 1) + ("arbitrary",), vmem),
    )(a, b)


def _write_plain(o_ref, acc_ref):
    o_ref[...] = acc_ref[...]


def mm_wgrad_rows(at, b, name):
    M, S = at.shape
    N = b.shape[1]
    tm, tk = 1024, 1024
    return mm_acc(at, b, name, grid=(M // tm, S // tk),
                  a_spec=pl.BlockSpec((tm, tk), lambda i, k: (i, k)),
                  b_spec=pl.BlockSpec((tk, N), lambda i, k: (k, 0)),
                  o_spec=pl.BlockSpec((tm, N), lambda i, k: (i, 0)),
                  o_shape=SDS((M, N), F32), acc_shape=(tm, N), write=_write_plain)


def mm_wgrad_cols(at, b, name, *, shard):
    M, S = at.shape
    tk = 1024
    nb = 2
    tn = nb * shard

    def write(o_ref, acc_ref):
        for u in range(nb):
            o_ref[u] = acc_ref[:, u * shard:(u + 1) * shard]

    return mm_acc(at, b, name, grid=(N_DEV // nb, S // tk),
                  a_spec=pl.BlockSpec((M, tk), lambda t, k: (0, k)),
                  b_spec=pl.BlockSpec((tk, tn), lambda t, k: (k, t)),
                  o_spec=pl.BlockSpec((nb, M, shard), lambda t, k: (t, 0, 0)),
                  o_shape=SDS((N_DEV, M, shard), F32), acc_shape=(M, tn), write=write)


def mm_dwg(pooled_t, dh):
    S = dh.shape[0]
    tk = 2048
    return mm_acc(pooled_t, dh, "mm_dwg", grid=(4, S // tk),
                  a_spec=pl.BlockSpec((PC, tk), lambda g, k: (g, k)),
                  b_spec=pl.BlockSpec((tk, PC), lambda g, k: (k, g)),
                  o_spec=pl.BlockSpec((None, PC, PC), lambda g, k: (g, 0, 0)),
                  o_shape=SDS((4, PC, PC), F32), acc_shape=(PC, PC), write=_write_plain)


def mm_dx_full(da, w, name):
    S, K = da.shape
    N = w.shape[0]
    tm = 512

    def body(a_ref, b_ref, o_ref):
        o_ref[...] = _dot_nt(a_ref[...], b_ref[...])

    return pl.pallas_call(
        body, name=name, grid=(S // tm,),
        in_specs=[pl.BlockSpec((tm, K), lambda i: (i, 0)), pl.BlockSpec((N, K), lambda i: (0, 0))],
        out_specs=pl.BlockSpec((tm, N), lambda i: (i, 0)),
        out_shape=SDS((S, N), F32),
        compiler_params=_cp(("parallel",), 48),
    )(da, w)


def mm_dw_in_part(at, b, tile0, prev, name):
    M, S = at.shape
    ntiles = b.shape[1] // CT
    tk = 2048
    nk = S // tk
    per = W_SHARD // CT

    def body(a_ref, b_ref, *rest):
        o_ref, acc_ref, sems = rest[-3:]
        k, t = pl.program_id(0), pl.program_id(1)

        @pl.when(k == 0)
        def _():
            acc_ref[t] = _dot(a_ref[...], b_ref[...])

        @pl.when(k > 0)
        def _():
            acc_ref[t] += _dot(a_ref[...], b_ref[...])

        def out_copy(u):
            tile = tile0 + u
            off = (tile % per) * CT
            if not isinstance(off, int):
                off = pl.multiple_of(off, CT)
            return pltpu.make_async_copy(acc_ref.at[u], o_ref.at[tile // per, :, pl.ds(off, CT)], sems.at[u])

        @pl.when(k == nk - 1)
        def _():
            out_copy(t).start()

        @pl.when(jnp.logical_and(k == nk - 1, t == ntiles - 1))
        def _():
            for u in range(ntiles):
                out_copy(u).wait()

    return pl.pallas_call(
        body, name=name, grid=(nk, ntiles),
        in_specs=[pl.BlockSpec((M, tk), lambda k, t: (0, k)), pl.BlockSpec((tk, CT), lambda k, t: (k, t))]
        + ([] if prev is None else [ANY]),
        out_specs=ANY,
        out_shape=SDS((N_DEV, M, W_SHARD), F32),
        scratch_shapes=[pltpu.VMEM((ntiles, M, CT), F32), pltpu.SemaphoreType.DMA((ntiles,))],
        input_output_aliases={} if prev is None else {2: 0},
        compiler_params=_cp(("arbitrary", "arbitrary"), 48),
    )(at, b, *(() if prev is None else (prev,)))


def mm_dx_part(da, w8, tile0, name, after=None):
    S, K = da.shape
    ntiles = K // CT
    tm = 512
    per = W_SHARD // CT

    def body(a_ref, w_ref, *rest):
        o_ref, wcat, sems = rest[-3:]

        @pl.when(pl.program_id(0) == 0)
        def _():
            cps = [pltpu.make_async_copy(
                w_ref.at[(tile0 + u) // per, :, pl.ds(((tile0 + u) % per) * CT, CT)],
                wcat.at[:, pl.ds(u * CT, CT)], sems.at[u]) for u in range(ntiles)]
            for cp in cps:
                cp.start()
            for cp in cps:
                cp.wait()

        o_ref[...] = _dot_nt(a_ref[...], wcat[...]).astype(BF16)

    return pl.pallas_call(
        body, name=name, grid=(S // tm,),
        in_specs=[pl.BlockSpec((tm, K), lambda i: (i, 0)), ANY] + ([] if after is None else [ANY]),
        out_specs=pl.BlockSpec((tm, D), lambda i: (i, 0)),
        out_shape=SDS((S, D), BF16),
        scratch_shapes=[pltpu.VMEM((D, K), BF16), pltpu.SemaphoreType.DMA((ntiles,))],
        compiler_params=_cp(("arbitrary",), 48),
    )(da, w8, *(() if after is None else (after,)))


HEADS_FWD = 4
HEADS_BWD = 2
AHEAD = 2


def _band_masks(not_first):
    row = lax.broadcasted_iota(jnp.int32, (QB, QB), 0)
    col = lax.broadcasted_iota(jnp.int32, (QB, QB), 1)
    cur = jnp.where(col <= row, 0.0, NEG)
    prev = jnp.where(col >= row, 0.0, NEG)
    first = jnp.where(jnp.logical_and(col >= row, not_first), 0.0, NEG)
    return col, jnp.concatenate([prev, cur], axis=1), jnp.concatenate([first, cur], axis=1)


def _fill_kv(ext, qkv_ref, kh_ref, vh_ref):
    ext[0:QB, 0:E] = kh_ref[...]
    ext[0:QB, E:2 * E] = vh_ref[...]
    ext[QB:, :] = qkv_ref[:, E:3 * E]


def attn_fwd(P, g, d):
    S = P.shape[0]
    L = S // d
    T = min(512, L)
    nq = T // QB
    ni = L // T

    def body(qkv_ref, kh_ref, vh_ref, o_ref, lse_ref, ext):
        col, mask, mask_first = _band_masks(pl.program_id(1) > 0)
        lse_ref[...] = jnp.zeros_like(lse_ref)
        _fill_kv(ext, qkv_ref, kh_ref, vh_ref)

        def heads(hp, carry):
            def front(h, j):
                cq = pl.ds(pl.multiple_of(h * HD, HD), HD)
                rows = slice(j * QB, (j + 1) * QB)
                krows = slice(j * QB, (j + 2) * QB)
                s = _dot_nt(qkv_ref[rows, cq], ext[krows, cq]) + (mask_first if j == 0 else mask)
                m = jnp.max(s, axis=1, keepdims=True)
                p = jnp.exp(s - m)
                den = jnp.sum(p, axis=1, keepdims=True)
                lse_ref[rows, :] = jnp.where(col == h, m + jnp.log(den), lse_ref[rows, :])
                return p.astype(BF16), den

            def back(h, j, p, den):
                off = pl.multiple_of(h * HD, HD)
                rows = slice(j * QB, (j + 1) * QB)
                krows = slice(j * QB, (j + 2) * QB)
                o_ref[rows, pl.ds(off, HD)] = (_dot(p, ext[krows, pl.ds(E + off, HD)]) / den).astype(BF16)

            items = [(HEADS_FWD * hp + hh, j) for hh in range(HEADS_FWD) for j in range(nq)]
            queue = [front(*it) for it in items[:AHEAD]]
            for u, it in enumerate(items):
                if u + AHEAD < len(items):
                    queue.append(front(*items[u + AHEAD]))
                back(*it, *queue.pop(0))
            return carry

        lax.fori_loop(0, NH // HEADS_FWD, heads, 0)

    halo = lambda r, i: jnp.maximum(r * (L // QB) + i * nq - 1, 0)
    return pl.pallas_call(
        body, name=f"attn_fwd{g}", grid=(d, ni),
        in_specs=[pl.BlockSpec((T, SEG), lambda r, i: (r * ni + i, 0)),
                  pl.BlockSpec((QB, E), lambda r, i: (halo(r, i), 1)),
                  pl.BlockSpec((QB, E), lambda r, i: (halo(r, i), 2))],
        out_specs=[pl.BlockSpec((T, E), lambda r, i: (r * ni + i, 0)),
                   pl.BlockSpec((T, HD), lambda r, i: (r * ni + i, 0))],
        out_shape=[SDS((S, E), BF16), SDS((S, HD), F32)],
        scratch_shapes=[pltpu.VMEM((T + QB, 2 * E), BF16)],
        compiler_params=_cp(("parallel", "parallel"), 48),
    )(P, P, P)


def _perm_specs(ts, C):
    return [pl.BlockSpec((ts, C), lambda i: (i, 0)),
            pl.BlockSpec((4, ts // 4, C), lambda i: (0, i, 0)),
            pl.BlockSpec((16, ts // 16, C), lambda i: (0, i, 0))]


def _perm_shapes(S, C, dtype):
    return [SDS((S, C), dtype), SDS((4, S // 4, C), dtype), SDS((16, S // 16, C), dtype)]


def combine_fwd(os_, lses, z, ehot):
    S = z.shape[0]
    ts = 256

    def body(o0, o1, o2, l0, l1, l2, z_ref, e_ref, y0, y1, y2, s0, s1, s2, ya_ref, yat_ref,
             so1, so2, sl1, sl2, sy, sl):
        _load_perm(so1, o1, 4)
        _load_perm(so2, o2, 16)
        _load_perm(sl1, l1, 4)
        _load_perm(sl2, l2, 16)
        ls = [l0[...], sl1[0], sl2[0]]
        m = jnp.maximum(jnp.maximum(ls[0], ls[1]), ls[2])
        es = [jnp.exp(l - m) for l in ls]
        den = es[0] + es[1] + es[2]
        sl[0] = m + jnp.log(den)
        y = None
        for e, o in zip(es, (o0[...].astype(F32), _scr_get(so1), _scr_get(so2))):
            w = e / den
            hi = w.astype(BF16)
            lo = (w - hi.astype(F32)).astype(BF16)
            wb = _dot(hi, e_ref[...]) + _dot(lo, e_ref[...])
            y = wb * o if y is None else y + wb * o
        z = z_ref[...].astype(F32)
        ya = y * (z * _sigmoid(z))
        ya_ref[...] = ya.astype(BF16)
        yat_ref[...] = ya.T.astype(BF16)
        _scr_put(sy, y)
        y0[...] = y.astype(BF16)
        _store_perm(y1, sy, 4)
        _store_perm(y2, sy, 16)
        s0[...] = sl[0]
        _store_perm(s1, sl, 4)
        _store_perm(s2, sl, 16)

    wide = pl.BlockSpec((ts, E), lambda i: (i, 0))
    os3 = [os_[0], os_[1].reshape(4, S // 4, E), os_[2].reshape(16, S // 16, E)]
    ls3 = [lses[0], lses[1].reshape(4, S // 4, HD), lses[2].reshape(16, S // 16, HD)]
    res = pl.pallas_call(
        body, name="combine_fwd", grid=(S // ts,),
        in_specs=_perm_specs(ts, E) + _perm_specs(ts, HD) + [wide, pl.BlockSpec((HD, E), lambda i: (0, 0))],
        out_specs=_perm_specs(ts, E) + _perm_specs(ts, HD) + [wide, pl.BlockSpec((E, ts), lambda i: (0, i))],
        out_shape=_perm_shapes(S, E, BF16) + _perm_shapes(S, HD, F32) + [SDS((S, E), BF16), SDS((E, S), BF16)],
        scratch_shapes=[_scr(ts, E), _scr(ts, E), _scr(ts, HD), _scr(ts, HD), _scr(ts, E), _scr(ts, HD)],
        compiler_params=_cp(("parallel",), 56),
    )(*os3, *ls3, z, ehot)
    ys = [res[0], res[1].reshape(S, E), res[2].reshape(S, E)]
    lse3 = [res[3], res[4].reshape(S, HD), res[5].reshape(S, HD)]
    return ys, lse3, res[6], res[7]


def mm_dya(da0, w_out, z, y):
    S = da0.shape[0]
    tm = 512

    def body(a_ref, w_ref, z_ref, y_ref, dy0, dy1, dy2, dz_ref, scr):
        for cidx in range(E // 256):
            col = slice(cidx * 256, (cidx + 1) * 256)
            dya = _dot_nt(a_ref[...], w_ref[col, :])
            zz = z_ref[:, col].astype(F32)
            sig = _sigmoid(zz)
            dy = dya * zz * sig
            scr[2 * cidx] = dy[:, :LANES]
            scr[2 * cidx + 1] = dy[:, LANES:]
            dy0[:, col] = dy.astype(BF16)
            dz_ref[:, col] = (dya * y_ref[:, col].astype(F32) * sig * (1.0 + zz * (1.0 - sig))).astype(BF16)
        _store_perm(dy1, scr, 4)
        _store_perm(dy2, scr, 16)

    wide = pl.BlockSpec((tm, E), lambda i: (i, 0))
    res = pl.pallas_call(
        body, name="mm_dya", grid=(S // tm,),
        in_specs=[pl.BlockSpec((tm, D), lambda i: (i, 0)), pl.BlockSpec((E, D), lambda i: (0, 0)), wide, wide],
        out_specs=_perm_specs(tm, E) + [wide],
        out_shape=_perm_shapes(S, E, BF16) + [SDS((S, E), BF16)],
        scratch_shapes=[_scr(tm, E)],
        compiler_params=_cp(("parallel",), 48),
    )(da0, w_out, z, y)
    return [res[0], res[1].reshape(S, E), res[2].reshape(S, E)], res[3]


def attn_bwd(P, dy, y, lse, tabs, g, d):
    S = P.shape[0]
    L = S // d
    T = min(512, L)
    nq = T // QB
    ni = L // T

    def body(qkv_ref, kh_ref, vh_ref, dy_ref, y_ref, lse_ref, c_ref, s1_ref, s2_ref,
             o_ref, dkc_ref, dvc_ref, ext):
        i = pl.program_id(1)
        _, mask, mask_first = _band_masks(i < ni - 1)
        _fill_kv(ext, qkv_ref, kh_ref, vh_ref)
        row_id2 = lax.broadcasted_iota(jnp.int32, (2 * QB, QB), 0)
        ones = jnp.ones((QB, QB), BF16)
        lse_hl = []
        for j in range(nq):
            t = lse_ref[j * QB:(j + 1) * QB, :]
            hi = t.astype(BF16)
            lse_hl.append(jnp.concatenate([hi, (t - hi.astype(F32)).astype(BF16)], axis=1))

        @pl.when(i == 0)
        def _():
            dkc_ref[...] = jnp.zeros_like(dkc_ref)
            dvc_ref[...] = jnp.zeros_like(dvc_ref)

        def heads(hp, carry):
            def cols(h):
                off = pl.multiple_of(h * HD, HD)
                return pl.ds(off, HD), pl.ds(E + off, HD), pl.ds(2 * E + off, HD)

            def front(h, j):
                cq, ck, _ = cols(h)
                rows = slice(j * QB, (j + 1) * QB)
                krows = slice(j * QB, (j + 2) * QB)
                dyj = dy_ref[rows, cq]
                sel = jnp.logical_or(row_id2 == h, row_id2 == h + QB).astype(BF16)
                lse_b = _dot(lse_hl[j], sel)
                delta_b = _dot((dyj.astype(F32) * y_ref[rows, cq].astype(F32)).astype(BF16), ones)
                s = _dot_nt(qkv_ref[rows, cq], ext[krows, cq])
                p = jnp.exp(s + (mask_first if j == 0 else mask) - jnp.concatenate([lse_b, lse_b], axis=1))
                ds = (p * (_dot_nt(dyj, ext[krows, ck]) - jnp.concatenate([delta_b, delta_b], axis=1))).astype(BF16)
                return ds, jnp.concatenate([ds, p.astype(BF16)], axis=0).T

            def back(h, j, ds, dsp_t, pend_dk, pend_dv):
                cq, ck, cv = cols(h)
                rows = slice(j * QB, (j + 1) * QB)
                krows = slice(j * QB, (j + 2) * QB)
                dq = _dot(ds, ext[krows, cq]) * SCALE
                zero = jnp.zeros((QB, HD), BF16)
                bd = jnp.concatenate([jnp.concatenate([qkv_ref[rows, cq], zero], axis=1),
                                      jnp.concatenate([zero, dy_ref[rows, cq]], axis=1)], axis=0)
                dkv = _dot(dsp_t, bd)
                dk2, dv2 = dkv[:, :HD], dkv[:, HD:]
                c, s1, s2 = c_ref[rows, :], s1_ref[rows, :], s2_ref[rows, :]
                o_ref[rows, cq] = _unrope(dq, c, s1, s2).astype(BF16)
                o_ref[rows, ck] = _unrope(dk2[QB:] + pend_dk, c, s1, s2).astype(BF16)
                o_ref[rows, cv] = (dv2[QB:] + pend_dv).astype(BF16)
                return dk2[:QB], dv2[:QB]

            items = [(HEADS_BWD * hp + hh, j) for hh in range(HEADS_BWD) for j in reversed(range(nq))]
            queue = [front(*it) for it in items[:AHEAD]]
            pend = None
            for u, (h, j) in enumerate(items):
                if u + AHEAD < len(items):
                    queue.append(front(*items[u + AHEAD]))
                if j == nq - 1:
                    pend = (dkc_ref[:, cols(h)[0]], dvc_ref[:, cols(h)[0]])
                pend = back(h, j, *queue.pop(0), *pend)
                if j == 0:
                    dkc_ref[:, cols(h)[0]], dvc_ref[:, cols(h)[0]] = pend
            return carry

        lax.fori_loop(0, NH // HEADS_BWD, heads, 0)

    blk = lambda r, i: r * ni + ni - 1 - i
    halo = lambda r, i: jnp.maximum(r * (L // QB) + (ni - 1 - i) * nq - 1, 0)
    main = pl.BlockSpec((T, SEG), lambda r, i: (blk(r, i), 0))
    wide = pl.BlockSpec((T, E), lambda r, i: (blk(r, i), 0))
    narrow = pl.BlockSpec((T, HD), lambda r, i: (blk(r, i), 0))
    return pl.pallas_call(
        body, name=f"attn_bwd{g}", grid=(d, ni),
        in_specs=[main, pl.BlockSpec((QB, E), lambda r, i: (halo(r, i), 1)),
                  pl.BlockSpec((QB, E), lambda r, i: (halo(r, i), 2)),
                  wide, wide, narrow, narrow, narrow, narrow],
        out_specs=main, out_shape=SDS((S, SEG), BF16),
        scratch_shapes=[pltpu.VMEM((QB, E), F32), pltpu.VMEM((QB, E), F32), pltpu.VMEM((T + QB, 2 * E), BF16)],
        compiler_params=_cp(("arbitrary", "arbitrary"), 56),
    )(P, P, P, dy, y, lse, *tabs)


def _pool_cnt(t0, rows):
    t = (lax.broadcasted_iota(jnp.int32, (rows, E), 0) + t0 + 1).astype(F32)
    ch = lax.broadcasted_iota(jnp.int32, (rows, E), 1)
    w = jnp.where(ch < PC, 2.0, jnp.where(ch < 2 * PC, 4.0, jnp.where(ch < 3 * PC, 8.0, 16.0)))
    return jnp.minimum(t, w)


def _by_group(parts):
    return jnp.concatenate([parts[g][:, g * PC:(g + 1) * PC] for g in range(4)], axis=1)


def pool_fwd(u):
    S = u.shape[0]
    ts = 256

    def body(u_ref, h_ref, o_ref, ot_ref):
        i = pl.program_id(0)
        u = u_ref[...]
        halo = jnp.where(i > 0, h_ref[...], 0.0)
        ext = jnp.concatenate([halo, u], axis=0)
        s2 = ext + pltpu.roll(ext, 1, 0)
        s4 = s2 + pltpu.roll(s2, 2, 0)
        s8 = s4 + pltpu.roll(s4, 4, 0)
        s16 = s8 + pltpu.roll(s8, 8, 0)
        win = _by_group([s2, s4, s8, s16])[16:, :]
        pooled = win / _pool_cnt(i * ts, ts) - u
        o_ref[...] = pooled.astype(BF16)
        ot_ref[...] = pooled.T.astype(BF16)

    return pl.pallas_call(
        body, name="pool_fwd", grid=(S // ts,),
        in_specs=[pl.BlockSpec((ts, E), lambda i: (i, 0)),
                  pl.BlockSpec((16, E), lambda i: (jnp.maximum(i * (ts // 16) - 1, 0), 0))],
        out_specs=[pl.BlockSpec((ts, E), lambda i: (i, 0)), pl.BlockSpec((E, ts), lambda i: (0, i))],
        out_shape=[SDS((S, E), BF16), SDS((E, S), BF16)],
        compiler_params=_cp(("parallel",), 48),
    )(u, u)


def pool_bwd(dpooled, duz):
    S = dpooled.shape[0]
    ts = 256
    nt = S // ts

    def body(d_ref, h_ref, alias_ref, o_ref):
        i = pl.program_id(0)
        dp = d_ref[...].astype(F32)
        halo = jnp.where(i < nt - 1, h_ref[...].astype(F32), 0.0)
        n = ts + 16
        ext = jnp.concatenate([dp, halo], axis=0) / _pool_cnt(i * ts, n)
        f2 = ext + pltpu.roll(ext, n - 1, 0)
        f4 = f2 + pltpu.roll(f2, n - 2, 0)
        f8 = f4 + pltpu.roll(f4, n - 4, 0)
        f16 = f8 + pltpu.roll(f8, n - 8, 0)
        win = _by_group([f2, f4, f8, f16])[:ts, :]
        o_ref[...] = (win - dp).astype(BF16)

    return pl.pallas_call(
        body, name="pool_bwd", grid=(nt,),
        in_specs=[pl.BlockSpec((ts, E), lambda i: (i, 0)),
                  pl.BlockSpec((16, E), lambda i: (jnp.minimum((i + 1) * (ts // 16), S // 16 - 1), 0)), ANY],
        out_specs=pl.BlockSpec((ts, E), lambda i: (i, 0)),
        out_shape=SDS(duz.shape, BF16),
        input_output_aliases={2: 0},
        compiler_params=_cp(("parallel",), 48),
    )(dpooled, dpooled, duz)


def mm_grp(pooled, wg, b, scale, z):
    S = pooled.shape[0]
    tm = 512

    def body(p_ref, w_ref, b_ref, s_ref, z_ref, h_ref, y_ref, yt_ref):
        for g in range(4):
            cs = slice(g * PC, (g + 1) * PC)
            h = _dot(p_ref[:, cs], w_ref[g]) + b_ref[:, cs]
            z = z_ref[:, cs].astype(F32)
            yp = h * s_ref[:, cs] * (z * _sigmoid(z))
            h_ref[:, cs] = h.astype(BF16)
            y_ref[:, cs] = yp.astype(BF16)
            yt_ref[cs, :] = yp.T.astype(BF16)

    row = pl.BlockSpec((tm, E), lambda i: (i, 0))
    vec = pl.BlockSpec((1, E), lambda i: (0, 0))
    return pl.pallas_call(
        body, name="mm_grp", grid=(S // tm,),
        in_specs=[row, pl.BlockSpec((4, PC, PC), lambda i: (0, 0, 0)), vec, vec, row],
        out_specs=[row, row, pl.BlockSpec((E, tm), lambda i: (0, i))],
        out_shape=[SDS((S, E), BF16), SDS((S, E), BF16), SDS((E, S), BF16)],
        compiler_params=_cp(("parallel",), 48),
    )(pooled, wg, b, scale, z)


def mm_dyp(da1, w_out, z, h, scale):
    S = da1.shape[0]
    tm = 512

    def body(a_ref, w_ref, z_ref, h_ref, s_ref, dh_ref, dz_ref, dsc_ref, db_ref):
        @pl.when(pl.program_id(0) == 0)
        def _():
            dsc_ref[...] = jnp.zeros_like(dsc_ref)
            db_ref[...] = jnp.zeros_like(db_ref)

        for cidx in range(E // 256):
            col = slice(cidx * 256, (cidx + 1) * 256)
            dyp = _dot_nt(a_ref[...], w_ref[col, :])
            z = z_ref[:, col].astype(F32)
            hh = h_ref[:, col].astype(F32)
            sc = s_ref[:, col]
            sig = _sigmoid(z)
            dhs = dyp * z * sig
            dz_ref[:, col] = (dyp * hh * sc * sig * (1.0 + z * (1.0 - sig))).astype(BF16)
            dh = dhs * sc
            dh_ref[:, col] = dh.astype(BF16)
            dsc_ref[:, col] += _fold8(dhs * hh)
            db_ref[:, col] += _fold8(dh)

    row = pl.BlockSpec((tm, E), lambda i: (i, 0))
    acc = pl.BlockSpec((8, E), lambda i: (0, 0))
    return pl.pallas_call(
        body, name="mm_dyp", grid=(S // tm,),
        in_specs=[pl.BlockSpec((tm, D), lambda i: (i, 0)), pl.BlockSpec((E, D), lambda i: (0, 0)),
                  row, row, pl.BlockSpec((1, E), lambda i: (0, 0))],
        out_specs=[row, pl.BlockSpec((tm, E), lambda i: (i, 1)), acc, acc],
        out_shape=[SDS((S, E), BF16), SDS((S, 2 * E), BF16), SDS((8, E), F32), SDS((8, E), F32)],
        compiler_params=_cp(("arbitrary",), 48),
    )(da1, w_out, z, h, scale)


def mm_dpooled(dh, wg):
    S = dh.shape[0]
    tm = 1024

    def body(a_ref, w_ref, o_ref):
        for g in range(4):
            cs = slice(g * PC, (g + 1) * PC)
            o_ref[:, cs] = _dot_nt(a_ref[:, cs], w_ref[g]).astype(BF16)

    row = pl.BlockSpec((tm, E), lambda i: (i, 0))
    return pl.pallas_call(
        body, name="mm_dpooled", grid=(S // tm,),
        in_specs=[row, pl.BlockSpec((4, PC, PC), lambda i: (0, 0, 0))],
        out_specs=row, out_shape=SDS((S, E), BF16),
        compiler_params=_cp(("parallel",), 48),
    )(dh, wg)


def _rope_tables(positions):
    inv_freq = 500000.0 ** (-jnp.arange(0, 32, 2, dtype=F32) / 32)
    S = positions.shape[0]
    ang = jnp.repeat(positions.astype(F32).reshape(S // 8, 8), 16, axis=1) * jnp.tile(inv_freq, 8)
    cos, sin = lax.optimization_barrier((jnp.cos(ang), jnp.sin(ang)))
    cos, sin = cos.reshape(S, 16), sin.reshape(S, 16)
    one = jnp.ones((S, HD - 32), F32)
    zero16 = jnp.zeros((S, 16), F32)
    zero = jnp.zeros((S, HD - 32), F32)
    c = jnp.concatenate([cos, cos, one], axis=1)
    s1 = jnp.concatenate([-sin, zero16, zero], axis=1)
    s2 = jnp.concatenate([zero16, sin, zero], axis=1)
    return c.astype(BF16), s1.astype(BF16), s2.astype(BF16)


def kernel(x, positions, norm_pre, norm_post, attn_w_in, attn_w_out, pool_w_in, pool_w_grp, pool_b_grp, pool_scale, pool_w_out, loss_target, m_norm_pre, m_norm_post, m_attn_w_in, m_attn_w_out, m_pool_w_in, m_pool_w_grp, m_pool_b_grp, m_pool_scale, m_pool_w_out, v_norm_pre, v_norm_post, v_attn_w_in, v_attn_w_out, v_pool_w_in, v_pool_w_grp, v_pool_b_grp, v_pool_scale, v_pool_w_out):
    S = x.shape[1]
    xi, yi, ci = _mesh_pos()
    dev = 4 * xi + 2 * yi + ci
    x2 = x[0]
    tgt = loss_target[0]

    small = jnp.concatenate([pool_b_grp[0].reshape(2, HD), pool_scale[0].reshape(2, HD),
                             jnp.zeros((4, HD), F32)], axis=0)
    w_in_l = attn_w_in[0].astype(BF16)
    hop1, hop1_token = split_start("gather_w_in_start", [w_in_l], [lax.empty((N_DEV,) + w_in_l.shape, BF16)],
                                   _hop1_plan(), 3)

    pos = positions[0]
    tabs = [_rope_tables(pos.reshape(S // d, d).T.reshape(S)) for d in DIL]
    ehot = (jnp.arange(E)[None, :] // HD == jnp.arange(HD)[:, None]).astype(BF16)
    seg_tiles = SEG // CT

    xn0, xn0_4, xn0_16, xn0t = norm_pre0(x2, norm_pre[0:1], hop1_token)
    xn0s = [xn0, xn0_4.reshape(S, D), xn0_16.reshape(S, D)]
    xn0ts = [xn0t, transpose_rows(xn0s[1], "xn0t_4"), transpose_rows(xn0s[2], "xn0t_16")]

    (w_in_l,), (w_in8,) = split_wait("gather_w_in_wait", hop1, _hop1_plan(), xn0ts[2])
    hop2, hop2_token = split_start("gather_w_in_hop2_start", [w_in8], None, _hop2_plan(D), 4)
    _, (w_in8,) = split_wait("gather_w_in_hop2_wait", hop2, _hop2_plan(D), hop2_token, inplace=True)
    hop3, hop3_token = split_start("gather_w_in_hop3_start", [w_in8], None, _hop3_plan(), 1)
    _, (w_in8,) = split_wait("gather_w_in_hop3_wait", hop3, _hop3_plan(), hop3_token, inplace=True)
    w_in8 = lax.dynamic_update_slice(w_in8, w_in_l[None], (dev, 0, 0))
    small, w_in8 = lax.optimization_barrier((small, w_in8))
    rest_l = [attn_w_out[0].astype(BF16), pool_w_in[0].astype(BF16), pool_w_grp[0].astype(BF16),
              pool_w_out[0].astype(BF16), small]
    rest_flight, rest_token = split_start(
        "gather_rest_start", rest_l, [lax.empty((N_DEV,) + a.shape, a.dtype) for a in rest_l], _peers_plan(), 7)
    Ps, os_, lses = [], [], []
    for g, d in enumerate(DIL):
        P = mm_in(xn0s[g], w_in8, g * seg_tiles, seg_tiles, tabs[g], f"mm_qkv{g}", after=rest_token)
        o, l = attn_fwd(P, g, d)
        Ps.append(P)
        os_.append(o)
        lses.append(l)
    z0 = mm_in(xn0, w_in8, 3 * seg_tiles, E // CT, None, "mm_z0")
    ys, lse3, ya, yat = combine_fwd(os_, lses, z0, ehot)

    rest_l, rest8 = split_wait("gather_rest_wait", rest_flight, _peers_plan(), ya)
    rest8 = [lax.dynamic_update_slice(r8, a[None], (dev,) + (0,) * a.ndim) for r8, a in zip(rest8, rest_l)]
    w_out8, wp_in8, wg8, wp_out8, small8 = rest8
    w_out = w_out8.reshape(E, D)
    wp_out = wp_out8.reshape(E, D)
    wp_in = wp_in8.transpose(1, 0, 2).reshape(D, 2 * E)
    wg = wg8.transpose(1, 0, 2, 3).reshape(4, PC, PC)
    b_full = small8[:, 0:2, :].reshape(N_DEV, 4, PC // N_DEV).transpose(1, 0, 2).reshape(1, E)
    scale_full = small8[:, 2:4, :].reshape(1, E)
    a0 = mm_rows(ya, w_out, "mm_out0", F32)
    h1, xn1, xn1t = post0_pre1(x2, a0, norm_post[0:1], norm_pre[1:2])

    half = N_DEV // 2
    u1 = mm_rows(xn1, wp_in8[:half].transpose(1, 0, 2).reshape(D, E), "mm_u1", F32)
    z1 = mm_rows(xn1, wp_in8[half:].transpose(1, 0, 2).reshape(D, E), "mm_z1", BF16)
    pooled, pooled_t = pool_fwd(u1)
    hgrp, yp, ypt = mm_grp(pooled, wg, b_full, scale_full, z1)
    a1 = mm_rows(yp, wp_out, "mm_out1", F32)
    dh2, da1, loss_rows, dg_post1 = post1_loss(h1, a1, tgt, norm_post[1:2])
    loss = lax.psum(0.5 / D * jnp.sum(loss_rows), ("x", "y", "c"))

    dh, duz, dscale_p, db_p = mm_dyp(da1, wp_out, z1, hgrp, scale_full)
    dpooled = mm_dpooled(dh, wg)
    duz = pool_bwd(dpooled, duz)
    g_wg = mm_dwg(pooled_t, dh)
    g_wp_out = mm_wgrad_rows(ypt, da1, "mm_dwp_out")
    g_wp_in = mm_wgrad_cols(xn1t, duz, "mm_dwp_in", shard=PC)
    dxn1 = mm_dx_full(duz, wp_in, "mm_dxn1")
    dh1, da0, dg_pre1, dg_post0 = mid_bwd(dxn1, dh2, h1, a0, norm_pre[1:2], norm_post[0:1])

    dys, dz0 = mm_dya(da0, w_out, z0, ys[0])
    g_w_out = mm_wgrad_rows(yat, da0, "mm_dw_out")
    g_w_in = mm_dw_in_part(xn0t, dz0, 3 * seg_tiles, None, "mm_dw_in_z")
    dPs = []
    for g, d in enumerate(DIL):
        dP = attn_bwd(Ps[g], dys[g], ys[g], lse3[g], tabs[g], g, d)
        g_w_in = mm_dw_in_part(xn0ts[g], dP, g * seg_tiles, g_w_in, f"mm_dw_in{g}")
        dPs.append(dP)

    cidx = ci.astype(jnp.int32).reshape(1)
    chip = (2 * xi + yi).astype(jnp.int32).reshape(1)
    fulls = [g_w_in, g_w_out.reshape(N_DEV, E // N_DEV, D), g_wp_in,
             g_wg.reshape(4, N_DEV, PC // N_DEV, PC).transpose(1, 0, 2, 3).reshape(N_DEV, 4 * PC // N_DEV, PC),
             g_wp_out.reshape(N_DEV, E // N_DEV, D)]
    pair_flight, pair_token = split_start(
        "rs_pair_start", fulls, [lax.empty((4,) + f.shape[1:], F32) for f in fulls], _pair_plan(), 4)
    dx_z = mm_dx_part(dz0, w_in8, 3 * seg_tiles, "mm_dxn0_z", after=pair_token)
    dx_0 = mm_dx_part(dPs[0], w_in8, 0, "mm_dxn0_0", after=dx_z)
    fulls, sibs = split_wait("rs_pair_wait", pair_flight, _pair_plan(), dx_0)
    parts = [pair_add(f, s, cidx, f"pair_add{k}") for k, (f, s) in enumerate(zip(fulls, sibs))]
    chips_flight, chips_token = split_start(
        "rs_chips_start", parts, [jnp.zeros(p.shape, BF16) for p in parts], _chips_plan(), 3)
    dx_1 = mm_dx_part(dPs[1], w_in8, seg_tiles, "mm_dxn0_1", after=chips_token)
    dx_2 = mm_dx_part(dPs[2], w_in8, 2 * seg_tiles, "mm_dxn0_2", after=dx_1)
    grad_x, dg_pre0 = pre0_bwd(dx_0, dx_z, dx_1, dx_2, dh1, x2, norm_pre[0:1])
    parts, recvs = split_wait("rs_chips_wait", chips_flight, _chips_plan(), grad_x)
    shards = [(attn_w_in, m_attn_w_in, v_attn_w_in), (attn_w_out, m_attn_w_out, v_attn_w_out),
              (pool_w_in, m_pool_w_in, v_pool_w_in), (pool_w_grp, m_pool_w_grp, v_pool_w_grp),
              (pool_w_out, m_pool_w_out, v_pool_w_out)]
    big = []
    for k, (recv, part, (w, m, v)) in enumerate(zip(recvs, parts, shards)):
        shp = w.shape
        r2 = recv.shape[1:]
        res = adamw_sum(recv, part, chip, w.reshape(r2), m.reshape(r2), v.reshape(r2), f"adamw{k}")
        big.append([t.reshape(shp) for t in res])

    smalls = jnp.concatenate([dg_pre0.sum(0, keepdims=True), dg_pre1.sum(0, keepdims=True),
                              dg_post0.sum(0, keepdims=True), dg_post1.sum(0, keepdims=True),
                              db_p.sum(0).reshape(2, D), dscale_p.sum(0).reshape(2, D)], axis=0)
    (smalls8,) = all_gather([smalls], "gather_small_grads")
    tot = sum_slots(smalls8, "sum_small_grads")
    g_norm_pre, g_norm_post = tot[0:2], tot[2:4]
    g_b = lax.dynamic_slice_in_dim(tot[4:6].reshape(4, PC), dev * (PC // N_DEV), PC // N_DEV, axis=1)[None]
    g_scale = lax.dynamic_slice_in_dim(tot[6:8].reshape(1, E), dev * (E // N_DEV), E // N_DEV, axis=1)
    sm = [adamw_small(g_norm_pre, norm_pre, m_norm_pre, v_norm_pre, "adamw_norm_pre"),
          adamw_small(g_norm_post, norm_post, m_norm_post, v_norm_post, "adamw_norm_post"),
          adamw_small(g_b, pool_b_grp, m_pool_b_grp, v_pool_b_grp, "adamw_b"),
          adamw_small(g_scale, pool_scale, m_pool_scale, v_pool_scale, "adamw_scale")]

    grads = [g_norm_pre, g_norm_post, big[0][0], big[1][0], big[2][0], big[3][0], g_b, g_scale, big[4][0]]

    def pick(k):
        return [sm[0][k - 1], sm[1][k - 1], big[0][k], big[1][k], big[2][k], big[3][k], sm[2][k - 1], sm[3][k - 1],
                big[4][k]]

    return (loss, grad_x[None], *grads, *pick(1), *pick(2), *pick(3))
```

```python
import math

import jax
import jax.numpy as jnp
from jax import lax
from jax.experimental import pallas as pl
from jax.experimental.pallas import tpu as pltpu

F32 = jnp.float32
BF16 = jnp.bfloat16
SDS = jax.ShapeDtypeStruct

N_DEV = 8
D = 1024
E = 2048
HD = 128
NH = E // HD
DIL = (1, 4, 16)
QB = 128
SEG = 3 * E
W_IN_COLS = 3 * SEG + E
W_SHARD = W_IN_COLS // N_DEV
CT = 512
PC = E // 4
EPS = 1e-6
NEG = -1e30
SCALE = 1.0 / math.sqrt(HD)
LR, B1, B2, ADAM_EPS, WD, STEP = 0.001, 0.9, 0.999, 1e-08, 0.01, 10
MIB = 1024 * 1024
ANY = pl.BlockSpec(memory_space=pl.ANY)
MESH = pl.DeviceIdType.MESH


def _cp(sem, mb):
    return pltpu.CompilerParams(dimension_semantics=sem, vmem_limit_bytes=mb * MIB)


def _dot(a, b):
    return jnp.dot(a, b, preferred_element_type=F32)


def _dot_nt(a, b):
    return lax.dot_general(a, b, (((1,), (1,)), ((), ())), preferred_element_type=F32)


def _rms(h):
    return lax.rsqrt(jnp.mean(h * h, axis=-1, keepdims=True) + EPS)


def _row_tile(R, C, budget):
    tr = R
    while tr * C * 4 > budget and tr % 16 == 0:
        tr //= 2
    return tr


def _fold8(t):
    return t.reshape(t.shape[0] // 8, 8, t.shape[1]).sum(axis=0)


def _sigmoid(z):
    return pl.reciprocal(1.0 + jnp.exp(-z), approx=True)


LANES = 128


def _scr(rows, C):
    return pltpu.VMEM((C // LANES, rows, LANES), F32)


def _scr_put(scr, val):
    for c in range(scr.shape[0]):
        scr[c] = val[:, c * LANES:(c + 1) * LANES]


def _scr_get(scr):
    return jnp.concatenate([scr[c] for c in range(scr.shape[0])], axis=1)


def _store_perm(dst_ref, scr, d):
    n = dst_ref.shape[1]
    for r in range(d):
        for c in range(scr.shape[0]):
            dst_ref[r, :, c * LANES:(c + 1) * LANES] = scr[c, pl.ds(r, n, stride=d), :].astype(dst_ref.dtype)


def _load_perm(scr, src_ref, d, add=False):
    n = src_ref.shape[1]
    for r in range(d):
        rows = pl.ds(r, n, stride=d)
        for c in range(scr.shape[0]):
            v = src_ref[r, :, c * LANES:(c + 1) * LANES].astype(F32)
            scr[c, rows, :] = scr[c, rows, :] + v if add else v


def _rope(t, c, s1, s2):
    t = t.astype(BF16)
    return t * c + pltpu.roll(t, HD - 16, 1) * s1 + pltpu.roll(t, 16, 1) * s2


def _unrope(t, c, s1, s2):
    t = t.astype(BF16)
    return t * c - pltpu.roll(t, HD - 16, 1) * s1 - pltpu.roll(t, 16, 1) * s2


def _mesh_pos():
    return lax.axis_index("x"), lax.axis_index("y"), lax.axis_index("c")


def all_gather(arrs, name):
    n = len(arrs)

    def body(*refs):
        ins, outs = refs[:n], refs[n:2 * n]
        send_sems, recv_sems, local_sems = refs[2 * n:]
        x, y, c = _mesh_pos()
        me, sib = (x, y, c), (x, y, 1 - c)
        chips = [(1 - x, y), (x, 1 - y), (1 - x, 1 - y)]

        def slot(p):
            return 4 * p[0] + 2 * p[1] + p[2]

        def copy(a, k, block, to, src=None):
            dst = outs[a].at[slot(block)]
            return pltpu.make_async_remote_copy(
                src_ref=dst if src is None else src, dst_ref=dst,
                send_sem=send_sems.at[a, k], recv_sem=recv_sems.at[a, k],
                device_id=to, device_id_type=MESH)

        mine = [pltpu.make_async_copy(ins[a], outs[a].at[slot(me)], local_sems.at[a]) for a in range(n)]
        for cp in mine:
            cp.start()
        first = []
        for a in range(n):
            first.append(copy(a, 0, me, sib, src=ins[a]))
            for j, chip in enumerate(chips):
                first.append(copy(a, 1 + j, me, (*chip, c), src=ins[a]))
        for cp in first:
            cp.start()
        passed = []
        for j, chip in enumerate(chips):
            for a in range(n):
                copy(a, 1 + j, (*chip, c), me).wait_recv()
                fw = copy(a, 4 + j, (*chip, c), sib)
                fw.start()
                passed.append(fw)
        for a in range(n):
            copy(a, 0, sib, me).wait_recv()
        for j, chip in enumerate(chips):
            for a in range(n):
                copy(a, 4 + j, (*chip, 1 - c), me).wait_recv()
        for cp in first + passed:
            cp.wait_send()
        for cp in mine:
            cp.wait()

    return pl.pallas_call(
        body, name=name,
        out_shape=[SDS((N_DEV,) + a.shape, a.dtype) for a in arrs],
        in_specs=[ANY] * n, out_specs=[ANY] * n,
        scratch_shapes=[pltpu.SemaphoreType.DMA((n, 7)), pltpu.SemaphoreType.DMA((n, 7)),
                        pltpu.SemaphoreType.DMA((n,))],
    )(*arrs)


HBM_SPEC = pl.BlockSpec(memory_space=pltpu.HBM)
SEM_SPEC = pl.BlockSpec(memory_space=pltpu.SEMAPHORE)
EFFECT = pltpu.SideEffectType.DATAFLOW_SIDE_EFFECTING


def _pair_plan():
    def plan(x, y, c):
        return [(2 * q + (1 - c), q, (x, y, 1 - c)) for q in range(4)]
    return plan


def _chips_plan():
    def plan(x, y, c):
        chips = [(1 - x, y), (x, 1 - y), (1 - x, 1 - y)]
        return [(2 * cx + cy, 2 * x + y, (cx, cy, c)) for cx, cy in chips]
    return plan


def _hop1_plan():
    def plan(x, y, c):
        me = 4 * x + 2 * y + c
        return [(None, me, (x, y, 1 - c)), (None, me, (1 - x, y, c)), (None, me, (x, 1 - y, c))]
    return plan


def _hop2_plan(rows):
    half = rows // 2

    def plan(x, y, c):
        sx, sy = 4 * (1 - x) + 2 * y + c, 4 * x + 2 * (1 - y) + c
        top, bottom = pl.ds(0, half), pl.ds(half, half)
        return [((sx, top), (sx, top), (x, 1 - y, c)), ((sy, bottom), (sy, bottom), (1 - x, y, c)),
                (sx, sx, (x, y, 1 - c)), (sy, sy, (x, y, 1 - c))]
    return plan


def _hop3_plan():
    def plan(x, y, c):
        sd = 4 * (1 - x) + 2 * (1 - y) + c
        return [(sd, sd, (x, y, 1 - c))]
    return plan


def _peers_plan():
    def plan(x, y, c):
        out = []
        for k in range(1, N_DEV):
            fx, fy, fc = (k >> 2) & 1, (k >> 1) & 1, k & 1
            px, py, pc = (x + fx) % 2, (y + fy) % 2, (c + fc) % 2
            out.append((None, 4 * x + 2 * y + c, (px, py, pc)))
        return out
    return plan


def _split_copies(plan, srcs, lands, send_sems, recv_sems):
    x, y, c = _mesh_pos()
    cps = []
    for a, (src, land) in enumerate(zip(srcs, lands)):
        steps = plan(x, y, c)
        for k, (si, li, to) in enumerate(steps):
            sem = a * len(steps) + k
            cps.append(pltpu.make_async_remote_copy(
                src_ref=src if si is None else src.at[si], dst_ref=land.at[li],
                send_sem=send_sems.at[sem], recv_sem=recv_sems.at[sem],
                device_id=to, device_id_type=MESH))
    return cps


def split_start(name, srcs, lands, plan, nk):
    n = len(srcs)
    ops = list(srcs) + ([] if lands is None else list(lands))
    nb = len(ops)

    def body(*refs):
        token = refs[-1]
        for cp in _split_copies(plan, refs[:n], refs[nb - n:nb], refs[nb], refs[nb + 1]):
            cp.start()
        token[...] = jnp.zeros_like(token)

    ops = [pltpu.with_memory_space_constraint(a, pltpu.HBM) for a in ops]
    res = pl.pallas_call(
        body, name=name,
        out_shape=(pltpu.SemaphoreType.DMA((n * nk,)), pltpu.SemaphoreType.DMA((n * nk,)),
                   *[pltpu.HBM(a.shape, a.dtype) for a in ops], SDS((8, 128), F32)),
        in_specs=[HBM_SPEC] * nb,
        out_specs=(SEM_SPEC, SEM_SPEC, *[HBM_SPEC] * nb, pl.BlockSpec(memory_space=pltpu.VMEM)),
        input_output_aliases={i: 2 + i for i in range(nb)},
        compiler_params=pltpu.CompilerParams(has_side_effects=EFFECT),
    )(*ops)
    return res[:-1], res[-1]


def split_wait(name, flight, plan, after, inplace=False):
    send_sems, recv_sems = flight[0], flight[1]
    bufs = list(flight[2:])
    nb = len(bufs)
    n = nb if inplace else nb // 2

    def body(*refs):
        for cp in _split_copies(plan, refs[:n], refs[nb - n:nb], refs[nb], refs[nb + 1]):
            cp.wait_send()
            cp.wait_recv()

    res = pl.pallas_call(
        body, name=name,
        out_shape=[pltpu.HBM(a.shape, a.dtype) for a in bufs],
        in_specs=[HBM_SPEC] * nb + [SEM_SPEC, SEM_SPEC, ANY],
        out_specs=[HBM_SPEC] * nb,
        input_output_aliases={i: i for i in range(nb)},
        compiler_params=pltpu.CompilerParams(has_side_effects=EFFECT),
    )(*bufs, send_sems, recv_sems, after)
    return res[:n], res[nb - n:]


def pair_add(full, sib, cidx, name):
    _, R, C = full.shape
    tr = _row_tile(R, C, MIB)

    def body(c_ref, a_ref, b_ref, o_ref):
        o_ref[...] = (a_ref[...] + b_ref[...]).astype(BF16)

    return pl.pallas_call(
        body, name=name,
        grid_spec=pltpu.PrefetchScalarGridSpec(
            num_scalar_prefetch=1, grid=(4, R // tr),
            in_specs=[pl.BlockSpec((None, tr, C), lambda q, i, cr: (2 * q + cr[0], i, 0)),
                      pl.BlockSpec((None, tr, C), lambda q, i, cr: (q, i, 0))],
            out_specs=pl.BlockSpec((None, tr, C), lambda q, i, cr: (q, i, 0))),
        out_shape=SDS((4, R, C), BF16),
        compiler_params=_cp(("parallel", "parallel"), 32),
    )(cidx, full, sib)


def _adam_math(w, g, m, v):
    m2 = B1 * m + (1.0 - B1) * g
    v2 = B2 * v + (1.0 - B2) * (g * g)
    m_hat = m2 / (1.0 - B1 ** STEP)
    v_hat = v2 / (1.0 - B2 ** STEP)
    delta = -LR * (m_hat / (jnp.sqrt(v_hat) + ADAM_EPS) + WD * w)
    return delta, m2, v2


def adamw_sum(recv, part, chip, w, m, v, name):
    K, R, C = recv.shape
    tr = _row_tile(R, C, MIB)

    def body(chip_ref, r_ref, p_ref, w_ref, m_ref, v_ref, g_ref, d_ref, m2_ref, v2_ref):
        g = r_ref[0].astype(F32)
        for k in range(1, K):
            g = g + r_ref[k].astype(F32)
        g = g + p_ref[...].astype(F32)
        delta, m2, v2 = _adam_math(w_ref[...], g, m_ref[...], v_ref[...])
        g_ref[...] = g
        d_ref[...] = delta
        m2_ref[...] = m2
        v2_ref[...] = v2

    tile = pl.BlockSpec((tr, C), lambda i, cr: (i, 0))
    return pl.pallas_call(
        body, name=name,
        grid_spec=pltpu.PrefetchScalarGridSpec(
            num_scalar_prefetch=1, grid=(R // tr,),
            in_specs=[pl.BlockSpec((K, tr, C), lambda i, cr: (0, i, 0)),
                      pl.BlockSpec((None, tr, C), lambda i, cr: (cr[0], i, 0)), tile, tile, tile],
            out_specs=[tile] * 4),
        out_shape=[SDS((R, C), F32)] * 4,
        compiler_params=_cp(("parallel",), 32),
    )(chip, recv, part, w, m, v)


def adamw_small(g, w, m, v, name):
    def body(g_ref, w_ref, m_ref, v_ref, d_ref, m2_ref, v2_ref):
        delta, m2, v2 = _adam_math(w_ref[...], g_ref[...], m_ref[...], v_ref[...])
        d_ref[...] = delta
        m2_ref[...] = m2
        v2_ref[...] = v2

    return pl.pallas_call(body, name=name, out_shape=[SDS(w.shape, F32)] * 3)(g, w, m, v)


def sum_slots(a, name):
    K = a.shape[0]

    def body(a_ref, o_ref):
        t = a_ref[0]
        for k in range(1, K):
            t = t + a_ref[k]
        o_ref[...] = t

    return pl.pallas_call(body, name=name, out_shape=SDS(a.shape[1:], F32))(a)


def norm_pre0(x, g, after):
    S = x.shape[0]
    ts = 512

    def body(x_ref, g_ref, after_ref, o_ref, o4_ref, o16_ref, ot_ref, scr):
        h = x_ref[...]
        xn = h * _rms(h) * g_ref[...]
        o_ref[...] = xn.astype(BF16)
        ot_ref[...] = xn.T.astype(BF16)
        _scr_put(scr, xn)
        _store_perm(o4_ref, scr, 4)
        _store_perm(o16_ref, scr, 16)

    return pl.pallas_call(
        body, name="norm_pre0", grid=(S // ts,),
        in_specs=[pl.BlockSpec((ts, D), lambda i: (i, 0)), pl.BlockSpec((1, D), lambda i: (0, 0)), ANY],
        out_specs=[pl.BlockSpec((ts, D), lambda i: (i, 0)),
                   pl.BlockSpec((4, ts // 4, D), lambda i: (0, i, 0)),
                   pl.BlockSpec((16, ts // 16, D), lambda i: (0, i, 0)),
                   pl.BlockSpec((D, ts), lambda i: (0, i))],
        out_shape=[SDS((S, D), BF16), SDS((4, S // 4, D), BF16), SDS((16, S // 16, D), BF16), SDS((D, S), BF16)],
        scratch_shapes=[_scr(ts, D)],
        compiler_params=_cp(("parallel",), 32),
    )(x, g, after)


def transpose_rows(a, name):
    S, C = a.shape
    ts = 512

    def body(a_ref, o_ref):
        o_ref[...] = a_ref[...].astype(F32).T.astype(BF16)

    return pl.pallas_call(
        body, name=name, grid=(S // ts,),
        in_specs=[pl.BlockSpec((ts, C), lambda i: (i, 0))],
        out_specs=pl.BlockSpec((C, ts), lambda i: (0, i)),
        out_shape=SDS((C, S), BF16),
        compiler_params=_cp(("parallel",), 32),
    )(a)


def post0_pre1(x, a0, g_post, g_pre):
    S = x.shape[0]
    ts = 512

    def body(x_ref, a_ref, gp_ref, gn_ref, h_ref, o_ref, ot_ref):
        a = a_ref[...]
        h1 = x_ref[...] + a * _rms(a) * gp_ref[...]
        h_ref[...] = h1
        xn = h1 * _rms(h1) * gn_ref[...]
        o_ref[...] = xn.astype(BF16)
        ot_ref[...] = xn.T.astype(BF16)

    row = pl.BlockSpec((ts, D), lambda i: (i, 0))
    vec = pl.BlockSpec((1, D), lambda i: (0, 0))
    return pl.pallas_call(
        body, name="post0_pre1", grid=(S // ts,),
        in_specs=[row, row, vec, vec],
        out_specs=[row, row, pl.BlockSpec((D, ts), lambda i: (0, i))],
        out_shape=[SDS((S, D), F32), SDS((S, D), BF16), SDS((D, S), BF16)],
        compiler_params=_cp(("parallel",), 40),
    )(x, a0, g_post, g_pre)


def post1_loss(h1, a1, target, g_post):
    S = h1.shape[0]
    ts = 512

    def body(h_ref, a_ref, t_ref, g_ref, dh_ref, da_ref, loss_ref, dg_ref):
        @pl.when(pl.program_id(0) == 0)
        def _():
            loss_ref[...] = jnp.zeros_like(loss_ref)
            dg_ref[...] = jnp.zeros_like(dg_ref)

        a = a_ref[...]
        g = g_ref[...]
        rp = _rms(a)
        yhat = a * rp
        e = h_ref[...] + yhat * g - t_ref[...]
        loss_ref[...] += _fold8(e * e)
        dh = e * (1.0 / D)
        dh_ref[...] = dh
        dg_ref[...] += _fold8(dh * yhat)
        dyh = dh * g
        da = rp * (dyh - yhat * jnp.mean(dyh * yhat, axis=-1, keepdims=True))
        da_ref[...] = da.astype(BF16)

    row = pl.BlockSpec((ts, D), lambda i: (i, 0))
    acc = pl.BlockSpec((8, D), lambda i: (0, 0))
    return pl.pallas_call(
        body, name="post1_loss", grid=(S // ts,),
        in_specs=[row, row, row, pl.BlockSpec((1, D), lambda i: (0, 0))],
        out_specs=[row, row, acc, acc],
        out_shape=[SDS((S, D), F32), SDS((S, D), BF16), SDS((8, D), F32), SDS((8, D), F32)],
        compiler_params=_cp(("arbitrary",), 40),
    )(h1, a1, target, g_post)


def mid_bwd(dxn1, dh2, h1, a0, g_pre1, g_post0):
    S = h1.shape[0]
    ts = 512

    def body(dx_ref, dh2_ref, h_ref, a_ref, gn_ref, gp_ref, dh1_ref, da_ref, dgn_ref, dgp_ref):
        @pl.when(pl.program_id(0) == 0)
        def _():
            dgn_ref[...] = jnp.zeros_like(dgn_ref)
            dgp_ref[...] = jnp.zeros_like(dgp_ref)

        h = h_ref[...]
        r1 = _rms(h)
        xhat = h * r1
        dxn = dx_ref[...]
        dgn_ref[...] += _fold8(dxn * xhat)
        dxh = dxn * gn_ref[...]
        dh1 = dh2_ref[...] + r1 * (dxh - xhat * jnp.mean(dxh * xhat, axis=-1, keepdims=True))
        dh1_ref[...] = dh1
        a = a_ref[...]
        rp = _rms(a)
        yhat = a * rp
        dgp_ref[...] += _fold8(dh1 * yhat)
        dyh = dh1 * gp_ref[...]
        da = rp * (dyh - yhat * jnp.mean(dyh * yhat, axis=-1, keepdims=True))
        da_ref[...] = da.astype(BF16)

    row = pl.BlockSpec((ts, D), lambda i: (i, 0))
    vec = pl.BlockSpec((1, D), lambda i: (0, 0))
    acc = pl.BlockSpec((8, D), lambda i: (0, 0))
    return pl.pallas_call(
        body, name="mid_bwd", grid=(S // ts,),
        in_specs=[row, row, row, row, vec, vec],
        out_specs=[row, row, acc, acc],
        out_shape=[SDS((S, D), F32), SDS((S, D), BF16), SDS((8, D), F32), SDS((8, D), F32)],
        compiler_params=_cp(("arbitrary",), 48),
    )(dxn1, dh2, h1, a0, g_pre1, g_post0)


def pre0_bwd(dx_tok, dx_z, dx4, dx16, dh1, x, g_pre0):
    S = x.shape[0]
    ts = 512

    def body(da_ref, dz_ref, d4_ref, d16_ref, dh_ref, x_ref, g_ref, gx_ref, dg_ref, scr):
        @pl.when(pl.program_id(0) == 0)
        def _():
            dg_ref[...] = jnp.zeros_like(dg_ref)

        _scr_put(scr, da_ref[...] + dz_ref[...])
        _load_perm(scr, d4_ref, 4, add=True)
        _load_perm(scr, d16_ref, 16, add=True)
        h = x_ref[...]
        r = _rms(h)
        xhat = h * r
        dxn = _scr_get(scr)
        dg_ref[...] += _fold8(dxn * xhat)
        dxh = dxn * g_ref[...]
        gx_ref[...] = dh_ref[...] + r * (dxh - xhat * jnp.mean(dxh * xhat, axis=-1, keepdims=True))

    row = pl.BlockSpec((ts, D), lambda i: (i, 0))
    return pl.pallas_call(
        body, name="pre0_bwd", grid=(S // ts,),
        in_specs=[row, row, pl.BlockSpec((4, ts // 4, D), lambda i: (0, i, 0)),
                  pl.BlockSpec((16, ts // 16, D), lambda i: (0, i, 0)), row, row,
                  pl.BlockSpec((1, D), lambda i: (0, 0))],
        out_specs=[row, pl.BlockSpec((8, D), lambda i: (0, 0))],
        out_shape=[SDS((S, D), F32), SDS((8, D), F32)],
        scratch_shapes=[_scr(ts, D)],
        compiler_params=_cp(("arbitrary",), 48),
    )(dx_tok, dx_z, dx4.reshape(4, S // 4, D), dx16.reshape(16, S // 16, D), dh1, x, g_pre0)


def mm_in(xn, w8, tile0, ntiles, tabs, name, after=None):
    S = xn.shape[0]
    tm = 512
    per = W_SHARD // CT
    N = ntiles * CT

    def body(a_ref, w_ref, *rest):
        o_ref, wcat, sems = rest[-3:]

        @pl.when(pl.program_id(0) == 0)
        def _():
            cps = [pltpu.make_async_copy(
                w_ref.at[(tile0 + u) // per, :, pl.ds(((tile0 + u) % per) * CT, CT)],
                wcat.at[:, pl.ds(u * CT, CT)], sems.at[u]) for u in range(ntiles)]
            for cp in cps:
                cp.start()
            for cp in cps:
                cp.wait()

        if tabs is not None:
            k_tabs = tuple(t_ref[...] for t_ref in rest[:3])
            q_tabs = tuple((t * SCALE).astype(BF16) for t in k_tabs)
        for t in range(ntiles):
            r = _dot(a_ref[...], wcat[:, t * CT:(t + 1) * CT])
            if tabs is None or t >= 2 * E // CT:
                o_ref[:, t * CT:(t + 1) * CT] = r.astype(BF16)
                continue
            c, s1, s2 = q_tabs if t < E // CT else k_tabs
            for hh in range(CT // HD):
                o_ref[:, t * CT + hh * HD:t * CT + (hh + 1) * HD] = _rope(
                    r[:, hh * HD:(hh + 1) * HD], c, s1, s2).astype(BF16)

    tab = pl.BlockSpec((tm, HD), lambda i: (i, 0))
    return pl.pallas_call(
        body, name=name, grid=(S // tm,),
        in_specs=[pl.BlockSpec((tm, D), lambda i: (i, 0)), ANY] + ([] if tabs is None else [tab] * 3)
        + ([] if after is None else [ANY]),
        out_specs=pl.BlockSpec((tm, N), lambda i: (i, 0)),
        out_shape=SDS((S, N), BF16),
        scratch_shapes=[pltpu.VMEM((D, N), BF16), pltpu.SemaphoreType.DMA((ntiles,))],
        compiler_params=_cp(("arbitrary",), 48),
    )(xn, w8, *(() if tabs is None else tabs), *(() if after is None else (after,)))


def mm_rows(a, b, name, out_dtype, tm=1024):
    M, K = a.shape
    N = b.shape[1]

    def body(a_ref, b_ref, o_ref):
        for cidx in range(N // 256):
            col = slice(cidx * 256, (cidx + 1) * 256)
            o_ref[:, col] = _dot(a_ref[...], b_ref[:, col]).astype(out_dtype)

    return pl.pallas_call(
        body, name=name, grid=(M // tm,),
        in_specs=[pl.BlockSpec((tm, K), lambda i: (i, 0)), pl.BlockSpec((K, N), lambda i: (0, 0))],
        out_specs=pl.BlockSpec((tm, N), lambda i: (i, 0)),
        out_shape=SDS((M, N), out_dtype),
        compiler_params=_cp(("parallel",), 48),
    )(a, b)


def mm_acc(a, b, name, *, grid, a_spec, b_spec, o_spec, o_shape, acc_shape, write, vmem=48):
    nk = grid[-1]

    def body(a_ref, b_ref, o_ref, acc_ref):
        k = pl.program_id(len(grid) - 1)

        @pl.when(k == 0)
        def _():
            acc_ref[...] = jnp.zeros_like(acc_ref)

        acc_ref[...] += _dot(a_ref[...], b_ref[...])

        @pl.when(k == nk - 1)
        def _():
            write(o_ref, acc_ref)

    return pl.pallas_call(
        body, name=name, grid=grid, in_specs=[a_spec, b_spec], out_specs=o_spec, out_shape=o_shape,
        scratch_shapes=[pltpu.VMEM(acc_shape, F32)],
        compiler_params=_cp(("parallel",) * (len(grid) - 1) + ("arbitrary",), vmem),
    )(a, b)


def _write_plain(o_ref, acc_ref):
    o_ref[...] = acc_ref[...]


def mm_wgrad_rows(at, b, name):
    M, S = at.shape
    N = b.shape[1]
    tm, tk = 1024, 1024
    return mm_acc(at, b, name, grid=(M // tm, S // tk),
                  a_spec=pl.BlockSpec((tm, tk), lambda i, k: (i, k)),
                  b_spec=pl.BlockSpec((tk, N), lambda i, k: (k, 0)),
                  o_spec=pl.BlockSpec((tm, N), lambda i, k: (i, 0)),
                  o_shape=SDS((M, N), F32), acc_shape=(tm, N), write=_write_plain)


def mm_wgrad_cols(at, b, name, *, shard):
    M, S = at.shape
    tk = 1024
    nb = 2
    tn = nb * shard

    def write(o_ref, acc_ref):
        for u in range(nb):
            o_ref[u] = acc_ref[:, u * shard:(u + 1) * shard]

    return mm_acc(at, b, name, grid=(N_DEV // nb, S // tk),
                  a_spec=pl.BlockSpec((M, tk), lambda t, k: (0, k)),
                  b_spec=pl.BlockSpec((tk, tn), lambda t, k: (k, t)),
                  o_spec=pl.BlockSpec((nb, M, shard), lambda t, k: (t, 0, 0)),
                  o_shape=SDS((N_DEV, M, shard), F32), acc_shape=(M, tn), write=write)


def mm_dwg(pooled_t, dh):
    S = dh.shape[0]
    tk = 2048
    return mm_acc(pooled_t, dh, "mm_dwg", grid=(4, S // tk),
                  a_spec=pl.BlockSpec((PC, tk), lambda g, k: (g, k)),
                  b_spec=pl.BlockSpec((tk, PC), lambda g, k: (k, g)),
                  o_spec=pl.BlockSpec((None, PC, PC), lambda g, k: (g, 0, 0)),
                  o_shape=SDS((4, PC, PC), F32), acc_shape=(PC, PC), write=_write_plain)


def mm_dx_full(da, w, name):
    S, K = da.shape
    N = w.shape[0]
    tm = 512

    def body(a_ref, b_ref, o_ref):
        o_ref[...] = _dot_nt(a_ref[...], b_ref[...])

    return pl.pallas_call(
        body, name=name, grid=(S // tm,),
        in_specs=[pl.BlockSpec((tm, K), lambda i: (i, 0)), pl.BlockSpec((N, K), lambda i: (0, 0))],
        out_specs=pl.BlockSpec((tm, N), lambda i: (i, 0)),
        out_shape=SDS((S, N), F32),
        compiler_params=_cp(("parallel",), 48),
    )(da, w)


def mm_dw_in_part(at, b, tile0, prev, name):
    M, S = at.shape
    ntiles = b.shape[1] // CT
    tk = 2048
    nk = S // tk
    per = W_SHARD // CT

    def body(a_ref, b_ref, *rest):
        o_ref, acc_ref, sems = rest[-3:]
        k, t = pl.program_id(0), pl.program_id(1)

        @pl.when(k == 0)
        def _():
            acc_ref[t] = _dot(a_ref[...], b_ref[...])

        @pl.when(k > 0)
        def _():
            acc_ref[t] += _dot(a_ref[...], b_ref[...])

        def out_copy(u):
            tile = tile0 + u
            off = (tile % per) * CT
            if not isinstance(off, int):
                off = pl.multiple_of(off, CT)
            return pltpu.make_async_copy(acc_ref.at[u], o_ref.at[tile // per, :, pl.ds(off, CT)], sems.at[u])

        @pl.when(k == nk - 1)
        def _():
            out_copy(t).start()

        @pl.when(jnp.logical_and(k == nk - 1, t == ntiles - 1))
        def _():
            for u in range(ntiles):
                out_copy(u).wait()

    return pl.pallas_call(
        body, name=name, grid=(nk, ntiles),
        in_specs=[pl.BlockSpec((M, tk), lambda k, t: (0, k)), pl.BlockSpec((tk, CT), lambda k, t: (k, t))]
        + ([] if prev is None else [ANY]),
        out_specs=ANY,
        out_shape=SDS((N_DEV, M, W_SHARD), F32),
        scratch_shapes=[pltpu.VMEM((ntiles, M, CT), F32), pltpu.SemaphoreType.DMA((ntiles,))],
        input_output_aliases={} if prev is None else {2: 0},
        compiler_params=_cp(("arbitrary", "arbitrary"), 48),
    )(at, b, *(() if prev is None else (prev,)))


def mm_dx_part(da, w8, tile0, name, after=None):
    S, K = da.shape
    ntiles = K // CT
    tm = 512
    per = W_SHARD // CT

    def body(a_ref, w_ref, *rest):
        o_ref, wcat, sems = rest[-3:]

        @pl.when(pl.program_id(0) == 0)
        def _():
            cps = [pltpu.make_async_copy(
                w_ref.at[(tile0 + u) // per, :, pl.ds(((tile0 + u) % per) * CT, CT)],
                wcat.at[:, pl.ds(u * CT, CT)], sems.at[u]) for u in range(ntiles)]
            for cp in cps:
                cp.start()
            for cp in cps:
                cp.wait()

        o_ref[...] = _dot_nt(a_ref[...], wcat[...])

    return pl.pallas_call(
        body, name=name, grid=(S // tm,),
        in_specs=[pl.BlockSpec((tm, K), lambda i: (i, 0)), ANY] + ([] if after is None else [ANY]),
        out_specs=pl.BlockSpec((tm, D), lambda i: (i, 0)),
        out_shape=SDS((S, D), F32),
        scratch_shapes=[pltpu.VMEM((D, K), BF16), pltpu.SemaphoreType.DMA((ntiles,))],
        compiler_params=_cp(("arbitrary",), 48),
    )(da, w8, *(() if after is None else (after,)))


HEADS_FWD = 4
HEADS_BWD = 2
AHEAD = 2


def _band_masks(not_first):
    row = lax.broadcasted_iota(jnp.int32, (QB, QB), 0)
    col = lax.broadcasted_iota(jnp.int32, (QB, QB), 1)
    cur = jnp.where(col <= row, 0.0, NEG)
    prev = jnp.where(col >= row, 0.0, NEG)
    first = jnp.where(jnp.logical_and(col >= row, not_first), 0.0, NEG)
    return col, jnp.concatenate([prev, cur], axis=1), jnp.concatenate([first, cur], axis=1)


def _fill_kv(ext, qkv_ref, kh_ref, vh_ref):
    ext[0:QB, 0:E] = kh_ref[...]
    ext[0:QB, E:2 * E] = vh_ref[...]
    ext[QB:, :] = qkv_ref[:, E:3 * E]


def attn_fwd(P, g, d):
    S = P.shape[0]
    L = S // d
    T = min(512, L)
    nq = T // QB
    ni = L // T

    def body(qkv_ref, kh_ref, vh_ref, o_ref, lse_ref, ext):
        col, mask, mask_first = _band_masks(pl.program_id(1) > 0)
        lse_ref[...] = jnp.zeros_like(lse_ref)
        _fill_kv(ext, qkv_ref, kh_ref, vh_ref)

        def heads(hp, carry):
            def front(h, j):
                cq = pl.ds(pl.multiple_of(h * HD, HD), HD)
                rows = slice(j * QB, (j + 1) * QB)
                krows = slice(j * QB, (j + 2) * QB)
                s = _dot_nt(qkv_ref[rows, cq], ext[krows, cq]) + (mask_first if j == 0 else mask)
                m = jnp.max(s, axis=1, keepdims=True)
                p = jnp.exp(s - m)
                den = jnp.sum(p, axis=1, keepdims=True)
                lse_ref[rows, :] = jnp.where(col == h, m + jnp.log(den), lse_ref[rows, :])
                return p.astype(BF16), den

            def back(h, j, p, den):
                off = pl.multiple_of(h * HD, HD)
                rows = slice(j * QB, (j + 1) * QB)
                krows = slice(j * QB, (j + 2) * QB)
                o_ref[rows, pl.ds(off, HD)] = (_dot(p, ext[krows, pl.ds(E + off, HD)]) / den).astype(BF16)

            items = [(HEADS_FWD * hp + hh, j) for hh in range(HEADS_FWD) for j in range(nq)]
            queue = [front(*it) for it in items[:AHEAD]]
            for u, it in enumerate(items):
                if u + AHEAD < len(items):
                    queue.append(front(*items[u + AHEAD]))
                back(*it, *queue.pop(0))
            return carry

        lax.fori_loop(0, NH // HEADS_FWD, heads, 0)

    halo = lambda r, i: jnp.maximum(r * (L // QB) + i * nq - 1, 0)
    return pl.pallas_call(
        body, name=f"attn_fwd{g}", grid=(d, ni),
        in_specs=[pl.BlockSpec((T, SEG), lambda r, i: (r * ni + i, 0)),
                  pl.BlockSpec((QB, E), lambda r, i: (halo(r, i), 1)),
                  pl.BlockSpec((QB, E), lambda r, i: (halo(r, i), 2))],
        out_specs=[pl.BlockSpec((T, E), lambda r, i: (r * ni + i, 0)),
                   pl.BlockSpec((T, HD), lambda r, i: (r * ni + i, 0))],
        out_shape=[SDS((S, E), BF16), SDS((S, HD), F32)],
        scratch_shapes=[pltpu.VMEM((T + QB, 2 * E), BF16)],
        compiler_params=_cp(("parallel", "parallel"), 48),
    )(P, P, P)


def _perm_specs(ts, C):
    return [pl.BlockSpec((ts, C), lambda i: (i, 0)),
            pl.BlockSpec((4, ts // 4, C), lambda i: (0, i, 0)),
            pl.BlockSpec((16, ts // 16, C), lambda i: (0, i, 0))]


def _perm_shapes(S, C, dtype):
    return [SDS((S, C), dtype), SDS((4, S // 4, C), dtype), SDS((16, S // 16, C), dtype)]


def combine_fwd(os_, lses, z, ehot):
    S = z.shape[0]
    ts = 256

    def body(o0, o1, o2, l0, l1, l2, z_ref, e_ref, y0, y1, y2, s0, s1, s2, ya_ref, yat_ref,
             so1, so2, sl1, sl2, sy, sl):
        _load_perm(so1, o1, 4)
        _load_perm(so2, o2, 16)
        _load_perm(sl1, l1, 4)
        _load_perm(sl2, l2, 16)
        ls = [l0[...], sl1[0], sl2[0]]
        m = jnp.maximum(jnp.maximum(ls[0], ls[1]), ls[2])
        es = [jnp.exp(l - m) for l in ls]
        den = es[0] + es[1] + es[2]
        sl[0] = m + jnp.log(den)
        y = None
        for e, o in zip(es, (o0[...].astype(F32), _scr_get(so1), _scr_get(so2))):
            w = e / den
            hi = w.astype(BF16)
            lo = (w - hi.astype(F32)).astype(BF16)
            wb = _dot(hi, e_ref[...]) + _dot(lo, e_ref[...])
            y = wb * o if y is None else y + wb * o
        z = z_ref[...].astype(F32)
        ya = y * (z * _sigmoid(z))
        ya_ref[...] = ya.astype(BF16)
        yat_ref[...] = ya.T.astype(BF16)
        _scr_put(sy, y)
        y0[...] = y.astype(BF16)
        _store_perm(y1, sy, 4)
        _store_perm(y2, sy, 16)
        s0[...] = sl[0]
        _store_perm(s1, sl, 4)
        _store_perm(s2, sl, 16)

    wide = pl.BlockSpec((ts, E), lambda i: (i, 0))
    os3 = [os_[0], os_[1].reshape(4, S // 4, E), os_[2].reshape(16, S // 16, E)]
    ls3 = [lses[0], lses[1].reshape(4, S // 4, HD), lses[2].reshape(16, S // 16, HD)]
    res = pl.pallas_call(
        body, name="combine_fwd", grid=(S // ts,),
        in_specs=_perm_specs(ts, E) + _perm_specs(ts, HD) + [wide, pl.BlockSpec((HD, E), lambda i: (0, 0))],
        out_specs=_perm_specs(ts, E) + _perm_specs(ts, HD) + [wide, pl.BlockSpec((E, ts), lambda i: (0, i))],
        out_shape=_perm_shapes(S, E, BF16) + _perm_shapes(S, HD, F32) + [SDS((S, E), BF16), SDS((E, S), BF16)],
        scratch_shapes=[_scr(ts, E), _scr(ts, E), _scr(ts, HD), _scr(ts, HD), _scr(ts, E), _scr(ts, HD)],
        compiler_params=_cp(("parallel",), 56),
    )(*os3, *ls3, z, ehot)
    ys = [res[0], res[1].reshape(S, E), res[2].reshape(S, E)]
    lse3 = [res[3], res[4].reshape(S, HD), res[5].reshape(S, HD)]
    return ys, lse3, res[6], res[7]


def mm_dya(da0, w_out, z, y):
    S = da0.shape[0]
    tm = 512

    def body(a_ref, w_ref, z_ref, y_ref, dy0, dy1, dy2, dz_ref, scr):
        for cidx in range(E // 256):
            col = slice(cidx * 256, (cidx + 1) * 256)
            dya = _dot_nt(a_ref[...], w_ref[col, :])
            zz = z_ref[:, col].astype(F32)
            sig = _sigmoid(zz)
            dy = dya * zz * sig
            scr[2 * cidx] = dy[:, :LANES]
            scr[2 * cidx + 1] = dy[:, LANES:]
            dy0[:, col] = dy.astype(BF16)
            dz_ref[:, col] = (dya * y_ref[:, col].astype(F32) * sig * (1.0 + zz * (1.0 - sig))).astype(BF16)
        _store_perm(dy1, scr, 4)
        _store_perm(dy2, scr, 16)

    wide = pl.BlockSpec((tm, E), lambda i: (i, 0))
    res = pl.pallas_call(
        body, name="mm_dya", grid=(S // tm,),
        in_specs=[pl.BlockSpec((tm, D), lambda i: (i, 0)), pl.BlockSpec((E, D), lambda i: (0, 0)), wide, wide],
        out_specs=_perm_specs(tm, E) + [wide],
        out_shape=_perm_shapes(S, E, BF16) + [SDS((S, E), BF16)],
        scratch_shapes=[_scr(tm, E)],
        compiler_params=_cp(("parallel",), 48),
    )(da0, w_out, z, y)
    return [res[0], res[1].reshape(S, E), res[2].reshape(S, E)], res[3]


def attn_bwd(P, dy, y, lse, tabs, g, d):
    S = P.shape[0]
    L = S // d
    T = min(512, L)
    nq = T // QB
    ni = L // T

    def body(qkv_ref, kh_ref, vh_ref, dy_ref, y_ref, lse_ref, c_ref, s1_ref, s2_ref,
             o_ref, dkc_ref, dvc_ref, ext):
        i = pl.program_id(1)
        _, mask, mask_first = _band_masks(i < ni - 1)
        _fill_kv(ext, qkv_ref, kh_ref, vh_ref)
        row_id2 = lax.broadcasted_iota(jnp.int32, (2 * QB, QB), 0)
        ones = jnp.ones((QB, QB), BF16)
        lse_hl = []
        for j in range(nq):
            t = lse_ref[j * QB:(j + 1) * QB, :]
            hi = t.astype(BF16)
            lse_hl.append(jnp.concatenate([hi, (t - hi.astype(F32)).astype(BF16)], axis=1))

        @pl.when(i == 0)
        def _():
            dkc_ref[...] = jnp.zeros_like(dkc_ref)
            dvc_ref[...] = jnp.zeros_like(dvc_ref)

        def heads(hp, carry):
            def cols(h):
                off = pl.multiple_of(h * HD, HD)
                return pl.ds(off, HD), pl.ds(E + off, HD), pl.ds(2 * E + off, HD)

            def front(h, j):
                cq, ck, _ = cols(h)
                rows = slice(j * QB, (j + 1) * QB)
                krows = slice(j * QB, (j + 2) * QB)
                dyj = dy_ref[rows, cq]
                sel = jnp.logical_or(row_id2 == h, row_id2 == h + QB).astype(BF16)
                lse_b = _dot(lse_hl[j], sel)
                delta_b = _dot((dyj.astype(F32) * y_ref[rows, cq].astype(F32)).astype(BF16), ones)
                s = _dot_nt(qkv_ref[rows, cq], ext[krows, cq])
                p = jnp.exp(s + (mask_first if j == 0 else mask) - jnp.concatenate([lse_b, lse_b], axis=1))
                ds = (p * (_dot_nt(dyj, ext[krows, ck]) - jnp.concatenate([delta_b, delta_b], axis=1))).astype(BF16)
                return ds, jnp.concatenate([ds, p.astype(BF16)], axis=0).T

            def back(h, j, ds, dsp_t, pend_dk, pend_dv):
                cq, ck, cv = cols(h)
                rows = slice(j * QB, (j + 1) * QB)
                krows = slice(j * QB, (j + 2) * QB)
                dq = _dot(ds, ext[krows, cq]) * SCALE
                zero = jnp.zeros((QB, HD), BF16)
                bd = jnp.concatenate([jnp.concatenate([qkv_ref[rows, cq], zero], axis=1),
                                      jnp.concatenate([zero, dy_ref[rows, cq]], axis=1)], axis=0)
                dkv = _dot(dsp_t, bd)
                dk2, dv2 = dkv[:, :HD], dkv[:, HD:]
                c, s1, s2 = c_ref[rows, :], s1_ref[rows, :], s2_ref[rows, :]
                o_ref[rows, cq] = _unrope(dq, c, s1, s2).astype(BF16)
                o_ref[rows, ck] = _unrope(dk2[QB:] + pend_dk, c, s1, s2).astype(BF16)
                o_ref[rows, cv] = (dv2[QB:] + pend_dv).astype(BF16)
                return dk2[:QB], dv2[:QB]

            items = [(HEADS_BWD * hp + hh, j) for hh in range(HEADS_BWD) for j in reversed(range(nq))]
            queue = [front(*it) for it in items[:AHEAD]]
            pend = None
            for u, (h, j) in enumerate(items):
                if u + AHEAD < len(items):
                    queue.append(front(*items[u + AHEAD]))
                if j == nq - 1:
                    pend = (dkc_ref[:, cols(h)[0]], dvc_ref[:, cols(h)[0]])
                pend = back(h, j, *queue.pop(0), *pend)
                if j == 0:
                    dkc_ref[:, cols(h)[0]], dvc_ref[:, cols(h)[0]] = pend
            return carry

        lax.fori_loop(0, NH // HEADS_BWD, heads, 0)

    blk = lambda r, i: r * ni + ni - 1 - i
    halo = lambda r, i: jnp.maximum(r * (L // QB) + (ni - 1 - i) * nq - 1, 0)
    main = pl.BlockSpec((T, SEG), lambda r, i: (blk(r, i), 0))
    wide = pl.BlockSpec((T, E), lambda r, i: (blk(r, i), 0))
    narrow = pl.BlockSpec((T, HD), lambda r, i: (blk(r, i), 0))
    return pl.pallas_call(
        body, name=f"attn_bwd{g}", grid=(d, ni),
        in_specs=[main, pl.BlockSpec((QB, E), lambda r, i: (halo(r, i), 1)),
                  pl.BlockSpec((QB, E), lambda r, i: (halo(r, i), 2)),
                  wide, wide, narrow, narrow, narrow, narrow],
        out_specs=main, out_shape=SDS((S, SEG), BF16),
        scratch_shapes=[pltpu.VMEM((QB, E), F32), pltpu.VMEM((QB, E), F32), pltpu.VMEM((T + QB, 2 * E), BF16)],
        compiler_params=_cp(("arbitrary", "arbitrary"), 56),
    )(P, P, P, dy, y, lse, *tabs)


def _pool_cnt(t0, rows):
    t = (lax.broadcasted_iota(jnp.int32, (rows, E), 0) + t0 + 1).astype(F32)
    ch = lax.broadcasted_iota(jnp.int32, (rows, E), 1)
    w = jnp.where(ch < PC, 2.0, jnp.where(ch < 2 * PC, 4.0, jnp.where(ch < 3 * PC, 8.0, 16.0)))
    return jnp.minimum(t, w)


def _by_group(parts):
    return jnp.concatenate([parts[g][:, g * PC:(g + 1) * PC] for g in range(4)], axis=1)


def pool_fwd(uz):
    S = uz.shape[0]
    ts = 256

    def body(u_ref, h_ref, o_ref, ot_ref):
        i = pl.program_id(0)
        u = u_ref[...]
        halo = jnp.where(i > 0, h_ref[...], 0.0)
        ext = jnp.concatenate([halo, u], axis=0)
        s2 = ext + pltpu.roll(ext, 1, 0)
        s4 = s2 + pltpu.roll(s2, 2, 0)
        s8 = s4 + pltpu.roll(s4, 4, 0)
        s16 = s8 + pltpu.roll(s8, 8, 0)
        win = _by_group([s2, s4, s8, s16])[16:, :]
        pooled = win / _pool_cnt(i * ts, ts) - u
        o_ref[...] = pooled.astype(BF16)
        ot_ref[...] = pooled.T.astype(BF16)

    return pl.pallas_call(
        body, name="pool_fwd", grid=(S // ts,),
        in_specs=[pl.BlockSpec((ts, E), lambda i: (i, 0)),
                  pl.BlockSpec((16, E), lambda i: (jnp.maximum(i * (ts // 16) - 1, 0), 0))],
        out_specs=[pl.BlockSpec((ts, E), lambda i: (i, 0)), pl.BlockSpec((E, ts), lambda i: (0, i))],
        out_shape=[SDS((S, E), BF16), SDS((E, S), BF16)],
        compiler_params=_cp(("parallel",), 48),
    )(uz, uz)


def pool_bwd(dpooled, duz):
    S = dpooled.shape[0]
    ts = 256
    nt = S // ts

    def body(d_ref, h_ref, alias_ref, o_ref):
        i = pl.program_id(0)
        dp = d_ref[...].astype(F32)
        halo = jnp.where(i < nt - 1, h_ref[...].astype(F32), 0.0)
        n = ts + 16
        ext = jnp.concatenate([dp, halo], axis=0) / _pool_cnt(i * ts, n)
        f2 = ext + pltpu.roll(ext, n - 1, 0)
        f4 = f2 + pltpu.roll(f2, n - 2, 0)
        f8 = f4 + pltpu.roll(f4, n - 4, 0)
        f16 = f8 + pltpu.roll(f8, n - 8, 0)
        win = _by_group([f2, f4, f8, f16])[:ts, :]
        o_ref[...] = (win - dp).astype(BF16)

    return pl.pallas_call(
        body, name="pool_bwd", grid=(nt,),
        in_specs=[pl.BlockSpec((ts, E), lambda i: (i, 0)),
                  pl.BlockSpec((16, E), lambda i: (jnp.minimum((i + 1) * (ts // 16), S // 16 - 1), 0)), ANY],
        out_specs=pl.BlockSpec((ts, E), lambda i: (i, 0)),
        out_shape=SDS(duz.shape, BF16),
        input_output_aliases={2: 0},
        compiler_params=_cp(("parallel",), 48),
    )(dpooled, dpooled, duz)


def mm_grp(pooled, wg, b, scale, uz):
    S = pooled.shape[0]
    tm = 512

    def body(p_ref, w_ref, b_ref, s_ref, z_ref, h_ref, y_ref, yt_ref):
        for g in range(4):
            cs = slice(g * PC, (g + 1) * PC)
            h = _dot(p_ref[:, cs], w_ref[g]) + b_ref[:, cs]
            z = z_ref[:, cs]
            yp = h * s_ref[:, cs] * (z * _sigmoid(z))
            h_ref[:, cs] = h.astype(BF16)
            y_ref[:, cs] = yp.astype(BF16)
            yt_ref[cs, :] = yp.T.astype(BF16)

    row = pl.BlockSpec((tm, E), lambda i: (i, 0))
    vec = pl.BlockSpec((1, E), lambda i: (0, 0))
    return pl.pallas_call(
        body, name="mm_grp", grid=(S // tm,),
        in_specs=[row, pl.BlockSpec((4, PC, PC), lambda i: (0, 0, 0)), vec, vec,
                  pl.BlockSpec((tm, E), lambda i: (i, 1))],
        out_specs=[row, row, pl.BlockSpec((E, tm), lambda i: (0, i))],
        out_shape=[SDS((S, E), BF16), SDS((S, E), BF16), SDS((E, S), BF16)],
        compiler_params=_cp(("parallel",), 48),
    )(pooled, wg, b, scale, uz)


def mm_dyp(da1, w_out, uz, h, scale):
    S = da1.shape[0]
    tm = 512

    def body(a_ref, w_ref, z_ref, h_ref, s_ref, dh_ref, dz_ref, dsc_ref, db_ref):
        @pl.when(pl.program_id(0) == 0)
        def _():
            dsc_ref[...] = jnp.zeros_like(dsc_ref)
            db_ref[...] = jnp.zeros_like(db_ref)

        for cidx in range(E // 256):
            col = slice(cidx * 256, (cidx + 1) * 256)
            dyp = _dot_nt(a_ref[...], w_ref[col, :])
            z = z_ref[:, col]
            hh = h_ref[:, col].astype(F32)
            sc = s_ref[:, col]
            sig = _sigmoid(z)
            dhs = dyp * z * sig
            dz_ref[:, col] = (dyp * hh * sc * sig * (1.0 + z * (1.0 - sig))).astype(BF16)
            dh = dhs * sc
            dh_ref[:, col] = dh.astype(BF16)
            dsc_ref[:, col] += _fold8(dhs * hh)
            db_ref[:, col] += _fold8(dh)

    row = pl.BlockSpec((tm, E), lambda i: (i, 0))
    acc = pl.BlockSpec((8, E), lambda i: (0, 0))
    return pl.pallas_call(
        body, name="mm_dyp", grid=(S // tm,),
        in_specs=[pl.BlockSpec((tm, D), lambda i: (i, 0)), pl.BlockSpec((E, D), lambda i: (0, 0)),
                  pl.BlockSpec((tm, E), lambda i: (i, 1)), row, pl.BlockSpec((1, E), lambda i: (0, 0))],
        out_specs=[row, pl.BlockSpec((tm, E), lambda i: (i, 1)), acc, acc],
        out_shape=[SDS((S, E), BF16), SDS((S, 2 * E), BF16), SDS((8, E), F32), SDS((8, E), F32)],
        compiler_params=_cp(("arbitrary",), 48),
    )(da1, w_out, uz, h, scale)


def mm_dpooled(dh, wg):
    S = dh.shape[0]
    tm = 1024

    def body(a_ref, w_ref, o_ref):
        for g in range(4):
            cs = slice(g * PC, (g + 1) * PC)
            o_ref[:, cs] = _dot_nt(a_ref[:, cs], w_ref[g]).astype(BF16)

    row = pl.BlockSpec((tm, E), lambda i: (i, 0))
    return pl.pallas_call(
        body, name="mm_dpooled", grid=(S // tm,),
        in_specs=[row, pl.BlockSpec((4, PC, PC), lambda i: (0, 0, 0))],
        out_specs=row, out_shape=SDS((S, E), BF16),
        compiler_params=_cp(("parallel",), 48),
    )(dh, wg)


def _rope_tables(positions):
    inv_freq = 500000.0 ** (-jnp.arange(0, 32, 2, dtype=F32) / 32)
    S = positions.shape[0]
    ang = jnp.repeat(positions.astype(F32).reshape(S // 8, 8), 16, axis=1) * jnp.tile(inv_freq, 8)
    cos, sin = lax.optimization_barrier((jnp.cos(ang), jnp.sin(ang)))
    cos, sin = cos.reshape(S, 16), sin.reshape(S, 16)
    one = jnp.ones((S, HD - 32), F32)
    zero16 = jnp.zeros((S, 16), F32)
    zero = jnp.zeros((S, HD - 32), F32)
    c = jnp.concatenate([cos, cos, one], axis=1)
    s1 = jnp.concatenate([-sin, zero16, zero], axis=1)
    s2 = jnp.concatenate([zero16, sin, zero], axis=1)
    return c.astype(BF16), s1.astype(BF16), s2.astype(BF16)


def kernel(x, positions, norm_pre, norm_post, attn_w_in, attn_w_out, pool_w_in, pool_w_grp, pool_b_grp, pool_scale, pool_w_out, loss_target, m_norm_pre, m_norm_post, m_attn_w_in, m_attn_w_out, m_pool_w_in, m_pool_w_grp, m_pool_b_grp, m_pool_scale, m_pool_w_out, v_norm_pre, v_norm_post, v_attn_w_in, v_attn_w_out, v_pool_w_in, v_pool_w_grp, v_pool_b_grp, v_pool_scale, v_pool_w_out):
    S = x.shape[1]
    xi, yi, ci = _mesh_pos()
    dev = 4 * xi + 2 * yi + ci
    x2 = x[0]
    tgt = loss_target[0]

    small = jnp.concatenate([pool_b_grp[0].reshape(2, HD), pool_scale[0].reshape(2, HD),
                             jnp.zeros((4, HD), F32)], axis=0)
    w_in_l = attn_w_in[0].astype(BF16)
    hop1, hop1_token = split_start("gather_w_in_start", [w_in_l], [lax.empty((N_DEV,) + w_in_l.shape, BF16)],
                                   _hop1_plan(), 3)

    pos = positions[0]
    tabs = [_rope_tables(pos.reshape(S // d, d).T.reshape(S)) for d in DIL]
    ehot = (jnp.arange(E)[None, :] // HD == jnp.arange(HD)[:, None]).astype(BF16)
    seg_tiles = SEG // CT

    xn0, xn0_4, xn0_16, xn0t = norm_pre0(x2, norm_pre[0:1], hop1_token)
    xn0s = [xn0, xn0_4.reshape(S, D), xn0_16.reshape(S, D)]
    xn0ts = [xn0t, transpose_rows(xn0s[1], "xn0t_4"), transpose_rows(xn0s[2], "xn0t_16")]

    (w_in_l,), (w_in8,) = split_wait("gather_w_in_wait", hop1, _hop1_plan(), xn0ts[2])
    hop2, hop2_token = split_start("gather_w_in_hop2_start", [w_in8], None, _hop2_plan(D), 4)
    _, (w_in8,) = split_wait("gather_w_in_hop2_wait", hop2, _hop2_plan(D), hop2_token, inplace=True)
    hop3, hop3_token = split_start("gather_w_in_hop3_start", [w_in8], None, _hop3_plan(), 1)
    _, (w_in8,) = split_wait("gather_w_in_hop3_wait", hop3, _hop3_plan(), hop3_token, inplace=True)
    w_in8 = lax.dynamic_update_slice(w_in8, w_in_l[None], (dev, 0, 0))
    small, w_in8 = lax.optimization_barrier((small, w_in8))
    rest_l = [attn_w_out[0].astype(BF16), pool_w_in[0].astype(BF16), pool_w_grp[0].astype(BF16),
              pool_w_out[0].astype(BF16), small]
    rest_flight, rest_token = split_start(
        "gather_rest_start", rest_l, [lax.empty((N_DEV,) + a.shape, a.dtype) for a in rest_l], _peers_plan(), 7)
    Ps, os_, lses = [], [], []
    for g, d in enumerate(DIL):
        P = mm_in(xn0s[g], w_in8, g * seg_tiles, seg_tiles, tabs[g], f"mm_qkv{g}", after=rest_token)
        o, l = attn_fwd(P, g, d)
        Ps.append(P)
        os_.append(o)
        lses.append(l)
    z0 = mm_in(xn0, w_in8, 3 * seg_tiles, E // CT, None, "mm_z0")
    ys, lse3, ya, yat = combine_fwd(os_, lses, z0, ehot)

    rest_l, rest8 = split_wait("gather_rest_wait", rest_flight, _peers_plan(), ya)
    rest8 = [lax.dynamic_update_slice(r8, a[None], (dev,) + (0,) * a.ndim) for r8, a in zip(rest8, rest_l)]
    w_out8, wp_in8, wg8, wp_out8, small8 = rest8
    w_out = w_out8.reshape(E, D)
    wp_out = wp_out8.reshape(E, D)
    wp_in = wp_in8.transpose(1, 0, 2).reshape(D, 2 * E)
    wg = wg8.transpose(1, 0, 2, 3).reshape(4, PC, PC)
    b_full = small8[:, 0:2, :].reshape(N_DEV, 4, PC // N_DEV).transpose(1, 0, 2).reshape(1, E)
    scale_full = small8[:, 2:4, :].reshape(1, E)
    a0 = mm_rows(ya, w_out, "mm_out0", F32)
    h1, xn1, xn1t = post0_pre1(x2, a0, norm_post[0:1], norm_pre[1:2])

    uz = mm_rows(xn1, wp_in, "mm_uz", F32, tm=512)
    pooled, pooled_t = pool_fwd(uz)
    hgrp, yp, ypt = mm_grp(pooled, wg, b_full, scale_full, uz)
    a1 = mm_rows(yp, wp_out, "mm_out1", F32)
    dh2, da1, loss_rows, dg_post1 = post1_loss(h1, a1, tgt, norm_post[1:2])
    loss = lax.psum(0.5 / D * jnp.sum(loss_rows), ("x", "y", "c"))

    dh, duz, dscale_p, db_p = mm_dyp(da1, wp_out, uz, hgrp, scale_full)
    dpooled = mm_dpooled(dh, wg)
    duz = pool_bwd(dpooled, duz)
    g_wg = mm_dwg(pooled_t, dh)
    g_wp_out = mm_wgrad_rows(ypt, da1, "mm_dwp_out")
    g_wp_in = mm_wgrad_cols(xn1t, duz, "mm_dwp_in", shard=PC)
    dxn1 = mm_dx_full(duz, wp_in, "mm_dxn1")
    dh1, da0, dg_pre1, dg_post0 = mid_bwd(dxn1, dh2, h1, a0, norm_pre[1:2], norm_post[0:1])

    dys, dz0 = mm_dya(da0, w_out, z0, ys[0])
    g_w_out = mm_wgrad_rows(yat, da0, "mm_dw_out")
    g_w_in = mm_dw_in_part(xn0t, dz0, 3 * seg_tiles, None, "mm_dw_in_z")
    dPs = []
    for g, d in enumerate(DIL):
        dP = attn_bwd(Ps[g], dys[g], ys[g], lse3[g], tabs[g], g, d)
        g_w_in = mm_dw_in_part(xn0ts[g], dP, g * seg_tiles, g_w_in, f"mm_dw_in{g}")
        dPs.append(dP)

    cidx = ci.astype(jnp.int32).reshape(1)
    chip = (2 * xi + yi).astype(jnp.int32).reshape(1)
    fulls = [g_w_in, g_w_out.reshape(N_DEV, E // N_DEV, D), g_wp_in,
             g_wg.reshape(4, N_DEV, PC // N_DEV, PC).transpose(1, 0, 2, 3).reshape(N_DEV, 4 * PC // N_DEV, PC),
             g_wp_out.reshape(N_DEV, E // N_DEV, D)]
    pair_flight, pair_token = split_start(
        "rs_pair_start", fulls, [lax.empty((4,) + f.shape[1:], F32) for f in fulls], _pair_plan(), 4)
    dx_z = mm_dx_part(dz0, w_in8, 3 * seg_tiles, "mm_dxn0_z", after=pair_token)
    dx_0 = mm_dx_part(dPs[0], w_in8, 0, "mm_dxn0_0", after=dx_z)
    fulls, sibs = split_wait("rs_pair_wait", pair_flight, _pair_plan(), dx_0)
    parts = [pair_add(f, s, cidx, f"pair_add{k}") for k, (f, s) in enumerate(zip(fulls, sibs))]
    chips_flight, chips_token = split_start(
        "rs_chips_start", parts, [jnp.zeros(p.shape, BF16) for p in parts], _chips_plan(), 3)
    dx_1 = mm_dx_part(dPs[1], w_in8, seg_tiles, "mm_dxn0_1", after=chips_token)
    dx_2 = mm_dx_part(dPs[2], w_in8, 2 * seg_tiles, "mm_dxn0_2", after=dx_1)
    grad_x, dg_pre0 = pre0_bwd(dx_0, dx_z, dx_1, dx_2, dh1, x2, norm_pre[0:1])
    parts, recvs = split_wait("rs_chips_wait", chips_flight, _chips_plan(), grad_x)
    shards = [(attn_w_in, m_attn_w_in, v_attn_w_in), (attn_w_out, m_attn_w_out, v_attn_w_out),
              (pool_w_in, m_pool_w_in, v_pool_w_in), (pool_w_grp, m_pool_w_grp, v_pool_w_grp),
              (pool_w_out, m_pool_w_out, v_pool_w_out)]
    big = []
    for k, (recv, part, (w, m, v)) in enumerate(zip(recvs, parts, shards)):
        shp = w.shape
        r2 = recv.shape[1:]
        res = adamw_sum(recv, part, chip, w.reshape(r2), m.reshape(r2), v.reshape(r2), f"adamw{k}")
        big.append([t.reshape(shp) for t in res])

    smalls = jnp.concatenate([dg_pre0.sum(0, keepdims=True), dg_pre1.sum(0, keepdims=True),
                              dg_post0.sum(0, keepdims=True), dg_post1.sum(0, keepdims=True),
                              db_p.sum(0).reshape(2, D), dscale_p.sum(0).reshape(2, D)], axis=0)
    (smalls8,) = all_gather([smalls], "gather_small_grads")
    tot = sum_slots(smalls8, "sum_small_grads")
    g_norm_pre, g_norm_post = tot[0:2], tot[2:4]
    g_b = lax.dynamic_slice_in_dim(tot[4:6].reshape(4, PC), dev * (PC // N_DEV), PC // N_DEV, axis=1)[None]
    g_scale = lax.dynamic_slice_in_dim(tot[6:8].reshape(1, E), dev * (E // N_DEV), E // N_DEV, axis=1)
    sm = [adamw_small(g_norm_pre, norm_pre, m_norm_pre, v_norm_pre, "adamw_norm_pre"),
          adamw_small(g_norm_post, norm_post, m_norm_post, v_norm_post, "adamw_norm_post"),
          adamw_small(g_b, pool_b_grp, m_pool_b_grp, v_pool_b_grp, "adamw_b"),
          adamw_small(g_scale, pool_scale, m_pool_scale, v_pool_scale, "adamw_scale")]

    grads = [g_norm_pre, g_norm_post, big[0][0], big[1][0], big[2][0], big[3][0], g_b, g_scale, big[4][0]]

    def pick(k):
        return [sm[0][k - 1], sm[1][k - 1], big[0][k], big[1][k], big[2][k], big[3][k], sm[2][k - 1], sm[3][k - 1],
                big[4][k]]

    return (loss, grad_x[None], *grads, *pick(1), *pick(2), *pick(3))
```

```python
import math

import jax
import jax.numpy as jnp
from jax import lax
from jax.experimental import pallas as pl
from jax.experimental.pallas import tpu as pltpu

F32 = jnp.float32
BF16 = jnp.bfloat16
SDS = jax.ShapeDtypeStruct

N_DEV = 8
D = 1024
E = 2048
HD = 128
NH = E // HD
DIL = (1, 4, 16)
QB = 128
SEG = 3 * E
W_IN_COLS = 3 * SEG + E
W_SHARD = W_IN_COLS // N_DEV
CT = 512
PC = E // 4
EPS = 1e-6
NEG = -1e30
SCALE = 1.0 / math.sqrt(HD)
LR, B1, B2, ADAM_EPS, WD, STEP = 0.001, 0.9, 0.999, 1e-08, 0.01, 10
MIB = 1024 * 1024
ANY = pl.BlockSpec(memory_space=pl.ANY)
MESH = pl.DeviceIdType.MESH


def _cp(sem, mb):
    return pltpu.CompilerParams(dimension_semantics=sem, vmem_limit_bytes=mb * MIB)


def _dot(a, b):
    return jnp.dot(a, b, preferred_element_type=F32)


def _dot_nt(a, b):
    return lax.dot_general(a, b, (((1,), (1,)), ((), ())), preferred_element_type=F32)


def _rms(h):
    return lax.rsqrt(jnp.mean(h * h, axis=-1, keepdims=True) + EPS)


def _row_tile(R, C, budget):
    tr = R
    while tr * C * 4 > budget and tr % 16 == 0:
        tr //= 2
    return tr


def _fold8(t):
    return t.reshape(t.shape[0] // 8, 8, t.shape[1]).sum(axis=0)


def _sigmoid(z):
    return pl.reciprocal(1.0 + jnp.exp(-z), approx=True)


LANES = 128


def _scr(rows, C):
    return pltpu.VMEM((C // LANES, rows, LANES), F32)


def _scr_put(scr, val):
    for c in range(scr.shape[0]):
        scr[c] = val[:, c * LANES:(c + 1) * LANES]


def _scr_get(scr):
    return jnp.concatenate([scr[c] for c in range(scr.shape[0])], axis=1)


def _store_perm(dst_ref, scr, d):
    n = dst_ref.shape[1]
    for r in range(d):
        for c in range(scr.shape[0]):
            dst_ref[r, :, c * LANES:(c + 1) * LANES] = scr[c, pl.ds(r, n, stride=d), :].astype(dst_ref.dtype)


def _load_perm(scr, src_ref, d, add=False):
    n = src_ref.shape[1]
    for r in range(d):
        rows = pl.ds(r, n, stride=d)
        for c in range(scr.shape[0]):
            v = src_ref[r, :, c * LANES:(c + 1) * LANES].astype(F32)
            scr[c, rows, :] = scr[c, rows, :] + v if add else v


def _rope(t, c, s1, s2):
    t = t.astype(BF16)
    return t * c + pltpu.roll(t, HD - 16, 1) * s1 + pltpu.roll(t, 16, 1) * s2


def _unrope(t, c, s1, s2):
    t = t.astype(BF16)
    return t * c - pltpu.roll(t, HD - 16, 1) * s1 - pltpu.roll(t, 16, 1) * s2


def _mesh_pos():
    return lax.axis_index("x"), lax.axis_index("y"), lax.axis_index("c")


def all_gather(arrs, name):
    n = len(arrs)

    def body(*refs):
        ins, outs = refs[:n], refs[n:2 * n]
        send_sems, recv_sems, local_sems = refs[2 * n:]
        x, y, c = _mesh_pos()
        me, sib = (x, y, c), (x, y, 1 - c)
        chips = [(1 - x, y), (x, 1 - y), (1 - x, 1 - y)]

        def slot(p):
            return 4 * p[0] + 2 * p[1] + p[2]

        def copy(a, k, block, to, src=None):
            dst = outs[a].at[slot(block)]
            return pltpu.make_async_remote_copy(
                src_ref=dst if src is None else src, dst_ref=dst,
                send_sem=send_sems.at[a, k], recv_sem=recv_sems.at[a, k],
                device_id=to, device_id_type=MESH)

        mine = [pltpu.make_async_copy(ins[a], outs[a].at[slot(me)], local_sems.at[a]) for a in range(n)]
        for cp in mine:
            cp.start()
        first = []
        for a in range(n):
            first.append(copy(a, 0, me, sib, src=ins[a]))
            for j, chip in enumerate(chips):
                first.append(copy(a, 1 + j, me, (*chip, c), src=ins[a]))
        for cp in first:
            cp.start()
        passed = []
        for j, chip in enumerate(chips):
            for a in range(n):
                copy(a, 1 + j, (*chip, c), me).wait_recv()
                fw = copy(a, 4 + j, (*chip, c), sib)
                fw.start()
                passed.append(fw)
        for a in range(n):
            copy(a, 0, sib, me).wait_recv()
        for j, chip in enumerate(chips):
            for a in range(n):
                copy(a, 4 + j, (*chip, 1 - c), me).wait_recv()
        for cp in first + passed:
            cp.wait_send()
        for cp in mine:
            cp.wait()

    return pl.pallas_call(
        body, name=name,
        out_shape=[SDS((N_DEV,) + a.shape, a.dtype) for a in arrs],
        in_specs=[ANY] * n, out_specs=[ANY] * n,
        scratch_shapes=[pltpu.SemaphoreType.DMA((n, 7)), pltpu.SemaphoreType.DMA((n, 7)),
                        pltpu.SemaphoreType.DMA((n,))],
    )(*arrs)


HBM_SPEC = pl.BlockSpec(memory_space=pltpu.HBM)
SEM_SPEC = pl.BlockSpec(memory_space=pltpu.SEMAPHORE)
EFFECT = pltpu.SideEffectType.DATAFLOW_SIDE_EFFECTING


def _pair_plan():
    def plan(x, y, c):
        return [(2 * q + (1 - c), q, (x, y, 1 - c)) for q in range(4)]
    return plan


def _chips_plan():
    def plan(x, y, c):
        chips = [(1 - x, y), (x, 1 - y), (1 - x, 1 - y)]
        return [(2 * cx + cy, 2 * x + y, (cx, cy, c)) for cx, cy in chips]
    return plan


def _hop1_plan():
    def plan(x, y, c):
        me = 4 * x + 2 * y + c
        return [(None, me, (x, y, 1 - c)), (None, me, (1 - x, y, c)), (None, me, (x, 1 - y, c))]
    return plan


def _hop2_plan(rows):
    half = rows // 2

    def plan(x, y, c):
        sx, sy = 4 * (1 - x) + 2 * y + c, 4 * x + 2 * (1 - y) + c
        top, bottom = pl.ds(0, half), pl.ds(half, half)
        return [((sx, top), (sx, top), (x, 1 - y, c)), ((sy, bottom), (sy, bottom), (1 - x, y, c)),
                (sx, sx, (x, y, 1 - c)), (sy, sy, (x, y, 1 - c))]
    return plan


def _hop3_plan():
    def plan(x, y, c):
        sd = 4 * (1 - x) + 2 * (1 - y) + c
        return [(sd, sd, (x, y, 1 - c))]
    return plan


def _peers_plan():
    def plan(x, y, c):
        out = []
        for k in range(1, N_DEV):
            fx, fy, fc = (k >> 2) & 1, (k >> 1) & 1, k & 1
            px, py, pc = (x + fx) % 2, (y + fy) % 2, (c + fc) % 2
            out.append((None, 4 * x + 2 * y + c, (px, py, pc)))
        return out
    return plan


def _split_copies(plan, srcs, lands, send_sems, recv_sems):
    x, y, c = _mesh_pos()
    cps = []
    for a, (src, land) in enumerate(zip(srcs, lands)):
        steps = plan(x, y, c)
        for k, (si, li, to) in enumerate(steps):
            sem = a * len(steps) + k
            cps.append(pltpu.make_async_remote_copy(
                src_ref=src if si is None else src.at[si], dst_ref=land.at[li],
                send_sem=send_sems.at[sem], recv_sem=recv_sems.at[sem],
                device_id=to, device_id_type=MESH))
    return cps


def split_start(name, srcs, lands, plan, nk):
    n = len(srcs)
    ops = list(srcs) + ([] if lands is None else list(lands))
    nb = len(ops)

    def body(*refs):
        token = refs[-1]
        for cp in _split_copies(plan, refs[:n], refs[nb - n:nb], refs[nb], refs[nb + 1]):
            cp.start()
        token[...] = jnp.zeros_like(token)

    ops = [pltpu.with_memory_space_constraint(a, pltpu.HBM) for a in ops]
    res = pl.pallas_call(
        body, name=name,
        out_shape=(pltpu.SemaphoreType.DMA((n * nk,)), pltpu.SemaphoreType.DMA((n * nk,)),
                   *[pltpu.HBM(a.shape, a.dtype) for a in ops], SDS((8, 128), F32)),
        in_specs=[HBM_SPEC] * nb,
        out_specs=(SEM_SPEC, SEM_SPEC, *[HBM_SPEC] * nb, pl.BlockSpec(memory_space=pltpu.VMEM)),
        input_output_aliases={i: 2 + i for i in range(nb)},
        compiler_params=pltpu.CompilerParams(has_side_effects=EFFECT),
    )(*ops)
    return res[:-1], res[-1]


def split_wait(name, flight, plan, after, inplace=False):
    send_sems, recv_sems = flight[0], flight[1]
    bufs = list(flight[2:])
    nb = len(bufs)
    n = nb if inplace else nb // 2

    def body(*refs):
        for cp in _split_copies(plan, refs[:n], refs[nb - n:nb], refs[nb], refs[nb + 1]):
            cp.wait_send()
            cp.wait_recv()

    res = pl.pallas_call(
        body, name=name,
        out_shape=[pltpu.HBM(a.shape, a.dtype) for a in bufs],
        in_specs=[HBM_SPEC] * nb + [SEM_SPEC, SEM_SPEC, ANY],
        out_specs=[HBM_SPEC] * nb,
        input_output_aliases={i: i for i in range(nb)},
        compiler_params=pltpu.CompilerParams(has_side_effects=EFFECT),
    )(*bufs, send_sems, recv_sems, after)
    return res[:n], res[nb - n:]


def pair_add(full, sib, cidx, name):
    _, R, C = full.shape
    tr = _row_tile(R, C, MIB)

    def body(c_ref, a_ref, b_ref, o_ref):
        o_ref[...] = (a_ref[...] + b_ref[...]).astype(BF16)

    return pl.pallas_call(
        body, name=name,
        grid_spec=pltpu.PrefetchScalarGridSpec(
            num_scalar_prefetch=1, grid=(4, R // tr),
            in_specs=[pl.BlockSpec((None, tr, C), lambda q, i, cr: (2 * q + cr[0], i, 0)),
                      pl.BlockSpec((None, tr, C), lambda q, i, cr: (q, i, 0))],
            out_specs=pl.BlockSpec((None, tr, C), lambda q, i, cr: (q, i, 0))),
        out_shape=SDS((4, R, C), BF16),
        compiler_params=_cp(("parallel", "parallel"), 32),
    )(cidx, full, sib)


def _adam_math(w, g, m, v):
    m2 = B1 * m + (1.0 - B1) * g
    v2 = B2 * v + (1.0 - B2) * (g * g)
    m_hat = m2 / (1.0 - B1 ** STEP)
    v_hat = v2 / (1.0 - B2 ** STEP)
    delta = -LR * (m_hat / (jnp.sqrt(v_hat) + ADAM_EPS) + WD * w)
    return delta, m2, v2


def adamw_sum(recv, part, chip, w, m, v, name):
    K, R, C = recv.shape
    tr = _row_tile(R, C, MIB)

    def body(chip_ref, r_ref, p_ref, w_ref, m_ref, v_ref, g_ref, d_ref, m2_ref, v2_ref):
        g = r_ref[0].astype(F32)
        for k in range(1, K):
            g = g + r_ref[k].astype(F32)
        g = g + p_ref[...].astype(F32)
        delta, m2, v2 = _adam_math(w_ref[...], g, m_ref[...], v_ref[...])
        g_ref[...] = g
        d_ref[...] = delta
        m2_ref[...] = m2
        v2_ref[...] = v2

    tile = pl.BlockSpec((tr, C), lambda i, cr: (i, 0))
    return pl.pallas_call(
        body, name=name,
        grid_spec=pltpu.PrefetchScalarGridSpec(
            num_scalar_prefetch=1, grid=(R // tr,),
            in_specs=[pl.BlockSpec((K, tr, C), lambda i, cr: (0, i, 0)),
                      pl.BlockSpec((None, tr, C), lambda i, cr: (cr[0], i, 0)), tile, tile, tile],
            out_specs=[tile] * 4),
        out_shape=[SDS((R, C), F32)] * 4,
        compiler_params=_cp(("parallel",), 32),
    )(chip, recv, part, w, m, v)


def adamw_small(g, w, m, v, name):
    def body(g_ref, w_ref, m_ref, v_ref, d_ref, m2_ref, v2_ref):
        delta, m2, v2 = _adam_math(w_ref[...], g_ref[...], m_ref[...], v_ref[...])
        d_ref[...] = delta
        m2_ref[...] = m2
        v2_ref[...] = v2

    return pl.pallas_call(body, name=name, out_shape=[SDS(w.shape, F32)] * 3)(g, w, m, v)


def sum_slots(a, name):
    K = a.shape[0]

    def body(a_ref, o_ref):
        t = a_ref[0]
        for k in range(1, K):
            t = t + a_ref[k]
        o_ref[...] = t

    return pl.pallas_call(body, name=name, out_shape=SDS(a.shape[1:], F32))(a)


def norm_pre0(x, g, after):
    S = x.shape[0]
    ts = 512

    def body(x_ref, g_ref, after_ref, o_ref, o4_ref, o16_ref, ot_ref, scr):
        h = x_ref[...]
        xn = h * _rms(h) * g_ref[...]
        o_ref[...] = xn.astype(BF16)
        ot_ref[...] = xn.T.astype(BF16)
        _scr_put(scr, xn)
        _store_perm(o4_ref, scr, 4)
        _store_perm(o16_ref, scr, 16)

    return pl.pallas_call(
        body, name="norm_pre0", grid=(S // ts,),
        in_specs=[pl.BlockSpec((ts, D), lambda i: (i, 0)), pl.BlockSpec((1, D), lambda i: (0, 0)), ANY],
        out_specs=[pl.BlockSpec((ts, D), lambda i: (i, 0)),
                   pl.BlockSpec((4, ts // 4, D), lambda i: (0, i, 0)),
                   pl.BlockSpec((16, ts // 16, D), lambda i: (0, i, 0)),
                   pl.BlockSpec((D, ts), lambda i: (0, i))],
        out_shape=[SDS((S, D), BF16), SDS((4, S // 4, D), BF16), SDS((16, S // 16, D), BF16), SDS((D, S), BF16)],
        scratch_shapes=[_scr(ts, D)],
        compiler_params=_cp(("parallel",), 32),
    )(x, g, after)


def transpose_rows(a, name):
    S, C = a.shape
    ts = 512

    def body(a_ref, o_ref):
        o_ref[...] = a_ref[...].astype(F32).T.astype(BF16)

    return pl.pallas_call(
        body, name=name, grid=(S // ts,),
        in_specs=[pl.BlockSpec((ts, C), lambda i: (i, 0))],
        out_specs=pl.BlockSpec((C, ts), lambda i: (0, i)),
        out_shape=SDS((C, S), BF16),
        compiler_params=_cp(("parallel",), 32),
    )(a)


def post0_pre1(x, a0, g_post, g_pre):
    S = x.shape[0]
    ts = 512

    def body(x_ref, a_ref, gp_ref, gn_ref, h_ref, o_ref, ot_ref):
        a = a_ref[...]
        h1 = x_ref[...] + a * _rms(a) * gp_ref[...]
        h_ref[...] = h1
        xn = h1 * _rms(h1) * gn_ref[...]
        o_ref[...] = xn.astype(BF16)
        ot_ref[...] = xn.T.astype(BF16)

    row = pl.BlockSpec((ts, D), lambda i: (i, 0))
    vec = pl.BlockSpec((1, D), lambda i: (0, 0))
    return pl.pallas_call(
        body, name="post0_pre1", grid=(S // ts,),
        in_specs=[row, row, vec, vec],
        out_specs=[row, row, pl.BlockSpec((D, ts), lambda i: (0, i))],
        out_shape=[SDS((S, D), F32), SDS((S, D), BF16), SDS((D, S), BF16)],
        compiler_params=_cp(("parallel",), 40),
    )(x, a0, g_post, g_pre)


def post1_loss(h1, a1, target, g_post):
    S = h1.shape[0]
    ts = 512

    def body(h_ref, a_ref, t_ref, g_ref, dh_ref, da_ref, loss_ref, dg_ref):
        @pl.when(pl.program_id(0) == 0)
        def _():
            loss_ref[...] = jnp.zeros_like(loss_ref)
            dg_ref[...] = jnp.zeros_like(dg_ref)

        a = a_ref[...]
        g = g_ref[...]
        rp = _rms(a)
        yhat = a * rp
        e = h_ref[...] + yhat * g - t_ref[...]
        loss_ref[...] += _fold8(e * e)
        dh = e * (1.0 / D)
        dh_ref[...] = dh
        dg_ref[...] += _fold8(dh * yhat)
        dyh = dh * g
        da = rp * (dyh - yhat * jnp.mean(dyh * yhat, axis=-1, keepdims=True))
        da_ref[...] = da.astype(BF16)

    row = pl.BlockSpec((ts, D), lambda i: (i, 0))
    acc = pl.BlockSpec((8, D), lambda i: (0, 0))
    return pl.pallas_call(
        body, name="post1_loss", grid=(S // ts,),
        in_specs=[row, row, row, pl.BlockSpec((1, D), lambda i: (0, 0))],
        out_specs=[row, row, acc, acc],
        out_shape=[SDS((S, D), F32), SDS((S, D), BF16), SDS((8, D), F32), SDS((8, D), F32)],
        compiler_params=_cp(("arbitrary",), 40),
    )(h1, a1, target, g_post)


def mid_bwd(dxn1, dh2, h1, a0, g_pre1, g_post0):
    S = h1.shape[0]
    ts = 512

    def body(dx_ref, dh2_ref, h_ref, a_ref, gn_ref, gp_ref, dh1_ref, da_ref, dgn_ref, dgp_ref):
        @pl.when(pl.program_id(0) == 0)
        def _():
            dgn_ref[...] = jnp.zeros_like(dgn_ref)
            dgp_ref[...] = jnp.zeros_like(dgp_ref)

        h = h_ref[...]
        r1 = _rms(h)
        xhat = h * r1
        dxn = dx_ref[...]
        dgn_ref[...] += _fold8(dxn * xhat)
        dxh = dxn * gn_ref[...]
        dh1 = dh2_ref[...] + r1 * (dxh - xhat * jnp.mean(dxh * xhat, axis=-1, keepdims=True))
        dh1_ref[...] = dh1
        a = a_ref[...]
        rp = _rms(a)
        yhat = a * rp
        dgp_ref[...] += _fold8(dh1 * yhat)
        dyh = dh1 * gp_ref[...]
        da = rp * (dyh - yhat * jnp.mean(dyh * yhat, axis=-1, keepdims=True))
        da_ref[...] = da.astype(BF16)

    row = pl.BlockSpec((ts, D), lambda i: (i, 0))
    vec = pl.BlockSpec((1, D), lambda i: (0, 0))
    acc = pl.BlockSpec((8, D), lambda i: (0, 0))
    return pl.pallas_call(
        body, name="mid_bwd", grid=(S // ts,),
        in_specs=[row, row, row, row, vec, vec],
        out_specs=[row, row, acc, acc],
        out_shape=[SDS((S, D), F32), SDS((S, D), BF16), SDS((8, D), F32), SDS((8, D), F32)],
        compiler_params=_cp(("arbitrary",), 48),
    )(dxn1, dh2, h1, a0, g_pre1, g_post0)


def pre0_bwd(dx_tok, dx_z, dx4, dx16, dh1, x, g_pre0):
    S = x.shape[0]
    ts = 512

    def body(da_ref, dz_ref, d4_ref, d16_ref, dh_ref, x_ref, g_ref, gx_ref, dg_ref, scr):
        @pl.when(pl.program_id(0) == 0)
        def _():
            dg_ref[...] = jnp.zeros_like(dg_ref)

        _scr_put(scr, da_ref[...] + dz_ref[...])
        _load_perm(scr, d4_ref, 4, add=True)
        _load_perm(scr, d16_ref, 16, add=True)
        h = x_ref[...]
        r = _rms(h)
        xhat = h * r
        dxn = _scr_get(scr)
        dg_ref[...] += _fold8(dxn * xhat)
        dxh = dxn * g_ref[...]
        gx_ref[...] = dh_ref[...] + r * (dxh - xhat * jnp.mean(dxh * xhat, axis=-1, keepdims=True))

    row = pl.BlockSpec((ts, D), lambda i: (i, 0))
    return pl.pallas_call(
        body, name="pre0_bwd", grid=(S // ts,),
        in_specs=[row, row, pl.BlockSpec((4, ts // 4, D), lambda i: (0, i, 0)),
                  pl.BlockSpec((16, ts // 16, D), lambda i: (0, i, 0)), row, row,
                  pl.BlockSpec((1, D), lambda i: (0, 0))],
        out_specs=[row, pl.BlockSpec((8, D), lambda i: (0, 0))],
        out_shape=[SDS((S, D), F32), SDS((8, D), F32)],
        scratch_shapes=[_scr(ts, D)],
        compiler_params=_cp(("arbitrary",), 48),
    )(dx_tok, dx_z, dx4.reshape(4, S // 4, D), dx16.reshape(16, S // 16, D), dh1, x, g_pre0)


def mm_in(xn, w8, tile0, ntiles, tabs, name, after=None):
    S = xn.shape[0]
    tm = 512
    per = W_SHARD // CT
    N = ntiles * CT

    def body(a_ref, w_ref, *rest):
        o_ref, wcat, sems = rest[-3:]

        @pl.when(pl.program_id(0) == 0)
        def _():
            cps = [pltpu.make_async_copy(
                w_ref.at[(tile0 + u) // per, :, pl.ds(((tile0 + u) % per) * CT, CT)],
                wcat.at[:, pl.ds(u * CT, CT)], sems.at[u]) for u in range(ntiles)]
            for cp in cps:
                cp.start()
            for cp in cps:
                cp.wait()

        if tabs is not None:
            k_tabs = tuple(t_ref[...] for t_ref in rest[:3])
            q_tabs = tuple((t * SCALE).astype(BF16) for t in k_tabs)
        for t in range(ntiles):
            r = _dot(a_ref[...], wcat[:, t * CT:(t + 1) * CT])
            if tabs is None or t >= 2 * E // CT:
                o_ref[:, t * CT:(t + 1) * CT] = r.astype(BF16)
                continue
            c, s1, s2 = q_tabs if t < E // CT else k_tabs
            for hh in range(CT // HD):
                o_ref[:, t * CT + hh * HD:t * CT + (hh + 1) * HD] = _rope(
                    r[:, hh * HD:(hh + 1) * HD], c, s1, s2).astype(BF16)

    tab = pl.BlockSpec((tm, HD), lambda i: (i, 0))
    return pl.pallas_call(
        body, name=name, grid=(S // tm,),
        in_specs=[pl.BlockSpec((tm, D), lambda i: (i, 0)), ANY] + ([] if tabs is None else [tab] * 3)
        + ([] if after is None else [ANY]),
        out_specs=pl.BlockSpec((tm, N), lambda i: (i, 0)),
        out_shape=SDS((S, N), BF16),
        scratch_shapes=[pltpu.VMEM((D, N), BF16), pltpu.SemaphoreType.DMA((ntiles,))],
        compiler_params=_cp(("arbitrary",), 48),
    )(xn, w8, *(() if tabs is None else tabs), *(() if after is None else (after,)))


def mm_rows(a, b, name, out_dtype, tm=1024):
    M, K = a.shape
    N = b.shape[1]

    def body(a_ref, b_ref, o_ref):
        for cidx in range(N // 256):
            col = slice(cidx * 256, (cidx + 1) * 256)
            o_ref[:, col] = _dot(a_ref[...], b_ref[:, col]).astype(out_dtype)

    return pl.pallas_call(
        body, name=name, grid=(M // tm,),
        in_specs=[pl.BlockSpec((tm, K), lambda i: (i, 0)), pl.BlockSpec((K, N), lambda i: (0, 0))],
        out_specs=pl.BlockSpec((tm, N), lambda i: (i, 0)),
        out_shape=SDS((M, N), out_dtype),
        compiler_params=_cp(("parallel",), 48),
    )(a, b)


def mm_acc(a, b, name, *, grid, a_spec, b_spec, o_spec, o_shape, acc_shape, write, vmem=48):
    nk = grid[-1]

    def body(a_ref, b_ref, o_ref, acc_ref):
        k = pl.program_id(len(grid) - 1)

        @pl.when(k == 0)
        def _():
            acc_ref[...] = jnp.zeros_like(acc_ref)

        acc_ref[...] += _dot(a_ref[...], b_ref[...])

        @pl.when(k == nk - 1)
        def _():
            write(o_ref, acc_ref)

    return pl.pallas_call(
        body, name=name, grid=grid, in_specs=[a_spec, b_spec], out_specs=o_spec, out_shape=o_shape,
        scratch_shapes=[pltpu.VMEM(acc_shape, F32)],
        compiler_params=_cp(("parallel",) * (len(grid) - 1) + ("arbitrary",), vmem),
    )(a, b)


def _write_plain(o_ref, acc_ref):
    o_ref[...] = acc_ref[...]


def mm_wgrad_rows(at, b, name):
    M, S = at.shape
    N = b.shape[1]
    tm, tk = 1024, 1024
    return mm_acc(at, b, name, grid=(M // tm, S // tk),
                  a_spec=pl.BlockSpec((tm, tk), lambda i, k: (i, k)),
                  b_spec=pl.BlockSpec((tk, N), lambda i, k: (k, 0)),
                  o_spec=pl.BlockSpec((tm, N), lambda i, k: (i, 0)),
                  o_shape=SDS((M, N), F32), acc_shape=(tm, N), write=_write_plain)


def mm_dwg(pooled_t, dh):
    S = dh.shape[0]
    tk = 2048
    return mm_acc(pooled_t, dh, "mm_dwg", grid=(4, S // tk),
                  a_spec=pl.BlockSpec((PC, tk), lambda g, k: (g, k)),
                  b_spec=pl.BlockSpec((tk, PC), lambda g, k: (k, g)),
                  o_spec=pl.BlockSpec((None, PC, PC), lambda g, k: (g, 0, 0)),
                  o_shape=SDS((4, PC, PC), F32), acc_shape=(PC, PC), write=_write_plain)


def mm_dx_full(da, w, name):
    S, K = da.shape
    N = w.shape[0]
    tm = 512

    def body(a_ref, b_ref, o_ref):
        o_ref[...] = _dot_nt(a_ref[...], b_ref[...])

    return pl.pallas_call(
        body, name=name, grid=(S // tm,),
        in_specs=[pl.BlockSpec((tm, K), lambda i: (i, 0)), pl.BlockSpec((N, K), lambda i: (0, 0))],
        out_specs=pl.BlockSpec((tm, N), lambda i: (i, 0)),
        out_shape=SDS((S, N), F32),
        compiler_params=_cp(("parallel",), 48),
    )(da, w)


def mm_dw_in_part(at, b, tile0, prev, name, shard=W_SHARD):
    M, S = at.shape
    ntiles = b.shape[1] // CT
    tk = 2048
    nk = S // tk
    per = shard // CT

    def body(a_ref, b_ref, *rest):
        o_ref, acc_ref, sems = rest[-3:]
        k, t = pl.program_id(0), pl.program_id(1)

        @pl.when(k == 0)
        def _():
            acc_ref[t] = _dot(a_ref[...], b_ref[...])

        @pl.when(k > 0)
        def _():
            acc_ref[t] += _dot(a_ref[...], b_ref[...])

        def out_copy(u):
            tile = tile0 + u
            off = (tile % per) * CT
            if not isinstance(off, int):
                off = pl.multiple_of(off, CT)
            return pltpu.make_async_copy(acc_ref.at[u], o_ref.at[tile // per, :, pl.ds(off, CT)], sems.at[u])

        @pl.when(k == nk - 1)
        def _():
            out_copy(t).start()

        @pl.when(jnp.logical_and(k == nk - 1, t == ntiles - 1))
        def _():
            for u in range(ntiles):
                out_copy(u).wait()

    return pl.pallas_call(
        body, name=name, grid=(nk, ntiles),
        in_specs=[pl.BlockSpec((M, tk), lambda k, t: (0, k)), pl.BlockSpec((tk, CT), lambda k, t: (k, t))]
        + ([] if prev is None else [ANY]),
        out_specs=ANY,
        out_shape=SDS((N_DEV, M, shard), F32),
        scratch_shapes=[pltpu.VMEM((ntiles, M, CT), F32), pltpu.SemaphoreType.DMA((ntiles,))],
        input_output_aliases={} if prev is None else {2: 0},
        compiler_params=_cp(("arbitrary", "arbitrary"), 48),
    )(at, b, *(() if prev is None else (prev,)))


def mm_dx_part(da, w8, tile0, name, after=None):
    S, K = da.shape
    ntiles = K // CT
    tm = 512
    per = W_SHARD // CT

    def body(a_ref, w_ref, *rest):
        o_ref, wcat, sems = rest[-3:]

        @pl.when(pl.program_id(0) == 0)
        def _():
            cps = [pltpu.make_async_copy(
                w_ref.at[(tile0 + u) // per, :, pl.ds(((tile0 + u) % per) * CT, CT)],
                wcat.at[:, pl.ds(u * CT, CT)], sems.at[u]) for u in range(ntiles)]
            for cp in cps:
                cp.start()
            for cp in cps:
                cp.wait()

        o_ref[...] = _dot_nt(a_ref[...], wcat[...])

    return pl.pallas_call(
        body, name=name, grid=(S // tm,),
        in_specs=[pl.BlockSpec((tm, K), lambda i: (i, 0)), ANY] + ([] if after is None else [ANY]),
        out_specs=pl.BlockSpec((tm, D), lambda i: (i, 0)),
        out_shape=SDS((S, D), F32),
        scratch_shapes=[pltpu.VMEM((D, K), BF16), pltpu.SemaphoreType.DMA((ntiles,))],
        compiler_params=_cp(("arbitrary",), 48),
    )(da, w8, *(() if after is None else (after,)))


HEADS_FWD = 4
HEADS_BWD = 2
AHEAD = 2


def _band_masks(not_first):
    row = lax.broadcasted_iota(jnp.int32, (QB, QB), 0)
    col = lax.broadcasted_iota(jnp.int32, (QB, QB), 1)
    cur = jnp.where(col <= row, 0.0, NEG)
    prev = jnp.where(col >= row, 0.0, NEG)
    first = jnp.where(jnp.logical_and(col >= row, not_first), 0.0, NEG)
    return col, jnp.concatenate([prev, cur], axis=1), jnp.concatenate([first, cur], axis=1)


def _fill_kv(ext, qkv_ref, kh_ref, vh_ref):
    ext[0:QB, 0:E] = kh_ref[...]
    ext[0:QB, E:2 * E] = vh_ref[...]
    ext[QB:, :] = qkv_ref[:, E:3 * E]


def attn_fwd(P, g, d):
    S = P.shape[0]
    L = S // d
    T = min(512, L)
    nq = T // QB
    ni = L // T

    def body(qkv_ref, kh_ref, vh_ref, o_ref, lse_ref, ext):
        col, mask, mask_first = _band_masks(pl.program_id(1) > 0)
        lse_ref[...] = jnp.zeros_like(lse_ref)
        _fill_kv(ext, qkv_ref, kh_ref, vh_ref)

        def heads(hp, carry):
            def front(h, j):
                cq = pl.ds(pl.multiple_of(h * HD, HD), HD)
                rows = slice(j * QB, (j + 1) * QB)
                krows = slice(j * QB, (j + 2) * QB)
                s = _dot_nt(qkv_ref[rows, cq], ext[krows, cq]) + (mask_first if j == 0 else mask)
                m = jnp.max(s, axis=1, keepdims=True)
                p = jnp.exp(s - m)
                den = jnp.sum(p, axis=1, keepdims=True)
                lse_ref[rows, :] = jnp.where(col == h, m + jnp.log(den), lse_ref[rows, :])
                return p.astype(BF16), den

            def back(h, j, p, den):
                off = pl.multiple_of(h * HD, HD)
                rows = slice(j * QB, (j + 1) * QB)
                krows = slice(j * QB, (j + 2) * QB)
                o_ref[rows, pl.ds(off, HD)] = (_dot(p, ext[krows, pl.ds(E + off, HD)]) / den).astype(BF16)

            items = [(HEADS_FWD * hp + hh, j) for hh in range(HEADS_FWD) for j in range(nq)]
            queue = [front(*it) for it in items[:AHEAD]]
            for u, it in enumerate(items):
                if u + AHEAD < len(items):
                    queue.append(front(*items[u + AHEAD]))
                back(*it, *queue.pop(0))
            return carry

        lax.fori_loop(0, NH // HEADS_FWD, heads, 0)

    halo = lambda r, i: jnp.maximum(r * (L // QB) + i * nq - 1, 0)
    return pl.pallas_call(
        body, name=f"attn_fwd{g}", grid=(d, ni),
        in_specs=[pl.BlockSpec((T, SEG), lambda r, i: (r * ni + i, 0)),
                  pl.BlockSpec((QB, E), lambda r, i: (halo(r, i), 1)),
                  pl.BlockSpec((QB, E), lambda r, i: (halo(r, i), 2))],
        out_specs=[pl.BlockSpec((T, E), lambda r, i: (r * ni + i, 0)),
                   pl.BlockSpec((T, HD), lambda r, i: (r * ni + i, 0))],
        out_shape=[SDS((S, E), BF16), SDS((S, HD), F32)],
        scratch_shapes=[pltpu.VMEM((T + QB, 2 * E), BF16)],
        compiler_params=_cp(("parallel", "parallel"), 48),
    )(P, P, P)


def _perm_specs(ts, C):
    return [pl.BlockSpec((ts, C), lambda i: (i, 0)),
            pl.BlockSpec((4, ts // 4, C), lambda i: (0, i, 0)),
            pl.BlockSpec((16, ts // 16, C), lambda i: (0, i, 0))]


def _perm_shapes(S, C, dtype):
    return [SDS((S, C), dtype), SDS((4, S // 4, C), dtype), SDS((16, S // 16, C), dtype)]


def combine_fwd(os_, lses, z, ehot):
    S = z.shape[0]
    ts = 256

    def body(o0, o1, o2, l0, l1, l2, z_ref, e_ref, y0, y1, y2, s0, s1, s2, ya_ref, yat_ref,
             so1, so2, sl1, sl2, sy, sl):
        _load_perm(so1, o1, 4)
        _load_perm(so2, o2, 16)
        _load_perm(sl1, l1, 4)
        _load_perm(sl2, l2, 16)
        ls = [l0[...], sl1[0], sl2[0]]
        m = jnp.maximum(jnp.maximum(ls[0], ls[1]), ls[2])
        es = [jnp.exp(l - m) for l in ls]
        den = es[0] + es[1] + es[2]
        sl[0] = m + jnp.log(den)
        y = None
        for e, o in zip(es, (o0[...].astype(F32), _scr_get(so1), _scr_get(so2))):
            w = e / den
            hi = w.astype(BF16)
            lo = (w - hi.astype(F32)).astype(BF16)
            wb = _dot(hi, e_ref[...]) + _dot(lo, e_ref[...])
            y = wb * o if y is None else y + wb * o
        z = z_ref[...].astype(F32)
        ya = y * (z * _sigmoid(z))
        ya_ref[...] = ya.astype(BF16)
        yat_ref[...] = ya.T.astype(BF16)
        _scr_put(sy, y)
        y0[...] = y.astype(BF16)
        _store_perm(y1, sy, 4)
        _store_perm(y2, sy, 16)
        s0[...] = sl[0]
        _store_perm(s1, sl, 4)
        _store_perm(s2, sl, 16)

    wide = pl.BlockSpec((ts, E), lambda i: (i, 0))
    os3 = [os_[0], os_[1].reshape(4, S // 4, E), os_[2].reshape(16, S // 16, E)]
    ls3 = [lses[0], lses[1].reshape(4, S // 4, HD), lses[2].reshape(16, S // 16, HD)]
    res = pl.pallas_call(
        body, name="combine_fwd", grid=(S // ts,),
        in_specs=_perm_specs(ts, E) + _perm_specs(ts, HD) + [wide, pl.BlockSpec((HD, E), lambda i: (0, 0))],
        out_specs=_perm_specs(ts, E) + _perm_specs(ts, HD) + [wide, pl.BlockSpec((E, ts), lambda i: (0, i))],
        out_shape=_perm_shapes(S, E, BF16) + _perm_shapes(S, HD, F32) + [SDS((S, E), BF16), SDS((E, S), BF16)],
        scratch_shapes=[_scr(ts, E), _scr(ts, E), _scr(ts, HD), _scr(ts, HD), _scr(ts, E), _scr(ts, HD)],
        compiler_params=_cp(("parallel",), 56),
    )(*os3, *ls3, z, ehot)
    ys = [res[0], res[1].reshape(S, E), res[2].reshape(S, E)]
    lse3 = [res[3], res[4].reshape(S, HD), res[5].reshape(S, HD)]
    return ys, lse3, res[6], res[7]


def mm_dya(da0, w_out, z, y):
    S = da0.shape[0]
    tm = 512

    def body(a_ref, w_ref, z_ref, y_ref, dy0, dy1, dy2, dz_ref, scr):
        for cidx in range(E // 256):
            col = slice(cidx * 256, (cidx + 1) * 256)
            dya = _dot_nt(a_ref[...], w_ref[col, :])
            zz = z_ref[:, col].astype(F32)
            sig = _sigmoid(zz)
            dy = dya * zz * sig
            scr[2 * cidx] = dy[:, :LANES]
            scr[2 * cidx + 1] = dy[:, LANES:]
            dy0[:, col] = dy.astype(BF16)
            dz_ref[:, col] = (dya * y_ref[:, col].astype(F32) * sig * (1.0 + zz * (1.0 - sig))).astype(BF16)
        _store_perm(dy1, scr, 4)
        _store_perm(dy2, scr, 16)

    wide = pl.BlockSpec((tm, E), lambda i: (i, 0))
    res = pl.pallas_call(
        body, name="mm_dya", grid=(S // tm,),
        in_specs=[pl.BlockSpec((tm, D), lambda i: (i, 0)), pl.BlockSpec((E, D), lambda i: (0, 0)), wide, wide],
        out_specs=_perm_specs(tm, E) + [wide],
        out_shape=_perm_shapes(S, E, BF16) + [SDS((S, E), BF16)],
        scratch_shapes=[_scr(tm, E)],
        compiler_params=_cp(("parallel",), 48),
    )(da0, w_out, z, y)
    return [res[0], res[1].reshape(S, E), res[2].reshape(S, E)], res[3]


def attn_bwd(P, dy, y, lse, tabs, g, d):
    S = P.shape[0]
    L = S // d
    T = min(512, L)
    nq = T // QB
    ni = L // T

    def body(qkv_ref, kh_ref, vh_ref, dy_ref, y_ref, lse_ref, c_ref, s1_ref, s2_ref,
             o_ref, dkc_ref, dvc_ref, ext):
        i = pl.program_id(1)
        _, mask, mask_first = _band_masks(i < ni - 1)
        _fill_kv(ext, qkv_ref, kh_ref, vh_ref)
        row_id2 = lax.broadcasted_iota(jnp.int32, (2 * QB, QB), 0)
        ones = jnp.ones((QB, QB), BF16)
        lse_hl = []
        for j in range(nq):
            t = lse_ref[j * QB:(j + 1) * QB, :]
            hi = t.astype(BF16)
            lse_hl.append(jnp.concatenate([hi, (t - hi.astype(F32)).astype(BF16)], axis=1))

        @pl.when(i == 0)
        def _():
            dkc_ref[...] = jnp.zeros_like(dkc_ref)
            dvc_ref[...] = jnp.zeros_like(dvc_ref)

        def heads(hp, carry):
            def cols(h):
                off = pl.multiple_of(h * HD, HD)
                return pl.ds(off, HD), pl.ds(E + off, HD), pl.ds(2 * E + off, HD)

            def front(h, j):
                cq, ck, _ = cols(h)
                rows = slice(j * QB, (j + 1) * QB)
                krows = slice(j * QB, (j + 2) * QB)
                dyj = dy_ref[rows, cq]
                sel = jnp.logical_or(row_id2 == h, row_id2 == h + QB).astype(BF16)
                lse_b = _dot(lse_hl[j], sel)
                delta_b = _dot((dyj.astype(F32) * y_ref[rows, cq].astype(F32)).astype(BF16), ones)
                s = _dot_nt(qkv_ref[rows, cq], ext[krows, cq])
                p = jnp.exp(s + (mask_first if j == 0 else mask) - jnp.concatenate([lse_b, lse_b], axis=1))
                ds = (p * (_dot_nt(dyj, ext[krows, ck]) - jnp.concatenate([delta_b, delta_b], axis=1))).astype(BF16)
                return ds, jnp.concatenate([ds, p.astype(BF16)], axis=0).T

            def back(h, j, ds, dsp_t, pend_dk, pend_dv):
                cq, ck, cv = cols(h)
                rows = slice(j * QB, (j + 1) * QB)
                krows = slice(j * QB, (j + 2) * QB)
                dq = _dot(ds, ext[krows, cq]) * SCALE
                zero = jnp.zeros((QB, HD), BF16)
                bd = jnp.concatenate([jnp.concatenate([qkv_ref[rows, cq], zero], axis=1),
                                      jnp.concatenate([zero, dy_ref[rows, cq]], axis=1)], axis=0)
                dkv = _dot(dsp_t, bd)
                dk2, dv2 = dkv[:, :HD], dkv[:, HD:]
                c, s1, s2 = c_ref[rows, :], s1_ref[rows, :], s2_ref[rows, :]
                o_ref[rows, cq] = _unrope(dq, c, s1, s2).astype(BF16)
                o_ref[rows, ck] = _unrope(dk2[QB:] + pend_dk, c, s1, s2).astype(BF16)
                o_ref[rows, cv] = (dv2[QB:] + pend_dv).astype(BF16)
                return dk2[:QB], dv2[:QB]

            items = [(HEADS_BWD * hp + hh, j) for hh in range(HEADS_BWD) for j in reversed(range(nq))]
            queue = [front(*it) for it in items[:AHEAD]]
            pend = None
            for u, (h, j) in enumerate(items):
                if u + AHEAD < len(items):
                    queue.append(front(*items[u + AHEAD]))
                if j == nq - 1:
                    pend = (dkc_ref[:, cols(h)[0]], dvc_ref[:, cols(h)[0]])
                pend = back(h, j, *queue.pop(0), *pend)
                if j == 0:
                    dkc_ref[:, cols(h)[0]], dvc_ref[:, cols(h)[0]] = pend
            return carry

        lax.fori_loop(0, NH // HEADS_BWD, heads, 0)

    blk = lambda r, i: r * ni + ni - 1 - i
    halo = lambda r, i: jnp.maximum(r * (L // QB) + (ni - 1 - i) * nq - 1, 0)
    main = pl.BlockSpec((T, SEG), lambda r, i: (blk(r, i), 0))
    wide = pl.BlockSpec((T, E), lambda r, i: (blk(r, i), 0))
    narrow = pl.BlockSpec((T, HD), lambda r, i: (blk(r, i), 0))
    return pl.pallas_call(
        body, name=f"attn_bwd{g}", grid=(d, ni),
        in_specs=[main, pl.BlockSpec((QB, E), lambda r, i: (halo(r, i), 1)),
                  pl.BlockSpec((QB, E), lambda r, i: (halo(r, i), 2)),
                  wide, wide, narrow, narrow, narrow, narrow],
        out_specs=main, out_shape=SDS((S, SEG), BF16),
        scratch_shapes=[pltpu.VMEM((QB, E), F32), pltpu.VMEM((QB, E), F32), pltpu.VMEM((T + QB, 2 * E), BF16)],
        compiler_params=_cp(("arbitrary", "arbitrary"), 56),
    )(P, P, P, dy, y, lse, *tabs)


def _pool_cnt(t0, rows):
    t = (lax.broadcasted_iota(jnp.int32, (rows, E), 0) + t0 + 1).astype(F32)
    ch = lax.broadcasted_iota(jnp.int32, (rows, E), 1)
    w = jnp.where(ch < PC, 2.0, jnp.where(ch < 2 * PC, 4.0, jnp.where(ch < 3 * PC, 8.0, 16.0)))
    return jnp.minimum(t, w)


def _by_group(parts):
    return jnp.concatenate([parts[g][:, g * PC:(g + 1) * PC] for g in range(4)], axis=1)


def pool_fwd(uz):
    S = uz.shape[0]
    ts = 256

    def body(u_ref, h_ref, o_ref, ot_ref):
        i = pl.program_id(0)
        u = u_ref[...]
        halo = jnp.where(i > 0, h_ref[...], 0.0)
        ext = jnp.concatenate([halo, u], axis=0)
        s2 = ext + pltpu.roll(ext, 1, 0)
        s4 = s2 + pltpu.roll(s2, 2, 0)
        s8 = s4 + pltpu.roll(s4, 4, 0)
        s16 = s8 + pltpu.roll(s8, 8, 0)
        win = _by_group([s2, s4, s8, s16])[16:, :]
        pooled = win / _pool_cnt(i * ts, ts) - u
        o_ref[...] = pooled.astype(BF16)
        ot_ref[...] = pooled.T.astype(BF16)

    return pl.pallas_call(
        body, name="pool_fwd", grid=(S // ts,),
        in_specs=[pl.BlockSpec((ts, E), lambda i: (i, 0)),
                  pl.BlockSpec((16, E), lambda i: (jnp.maximum(i * (ts // 16) - 1, 0), 0))],
        out_specs=[pl.BlockSpec((ts, E), lambda i: (i, 0)), pl.BlockSpec((E, ts), lambda i: (0, i))],
        out_shape=[SDS((S, E), BF16), SDS((E, S), BF16)],
        compiler_params=_cp(("parallel",), 48),
    )(uz, uz)


def pool_bwd(dpooled, duz):
    S = dpooled.shape[0]
    ts = 256
    nt = S // ts

    def body(d_ref, h_ref, alias_ref, o_ref):
        i = pl.program_id(0)
        dp = d_ref[...].astype(F32)
        halo = jnp.where(i < nt - 1, h_ref[...].astype(F32), 0.0)
        n = ts + 16
        ext = jnp.concatenate([dp, halo], axis=0) / _pool_cnt(i * ts, n)
        f2 = ext + pltpu.roll(ext, n - 1, 0)
        f4 = f2 + pltpu.roll(f2, n - 2, 0)
        f8 = f4 + pltpu.roll(f4, n - 4, 0)
        f16 = f8 + pltpu.roll(f8, n - 8, 0)
        win = _by_group([f2, f4, f8, f16])[:ts, :]
        o_ref[...] = (win - dp).astype(BF16)

    return pl.pallas_call(
        body, name="pool_bwd", grid=(nt,),
        in_specs=[pl.BlockSpec((ts, E), lambda i: (i, 0)),
                  pl.BlockSpec((16, E), lambda i: (jnp.minimum((i + 1) * (ts // 16), S // 16 - 1), 0)), ANY],
        out_specs=pl.BlockSpec((ts, E), lambda i: (i, 0)),
        out_shape=SDS(duz.shape, BF16),
        input_output_aliases={2: 0},
        compiler_params=_cp(("parallel",), 48),
    )(dpooled, dpooled, duz)


def mm_grp(pooled, wg, b, scale, uz):
    S = pooled.shape[0]
    tm = 512

    def body(p_ref, w_ref, b_ref, s_ref, z_ref, h_ref, y_ref, yt_ref):
        for g in range(4):
            cs = slice(g * PC, (g + 1) * PC)
            h = _dot(p_ref[:, cs], w_ref[g]) + b_ref[:, cs]
            z = z_ref[:, cs]
            yp = h * s_ref[:, cs] * (z * _sigmoid(z))
            h_ref[:, cs] = h.astype(BF16)
            y_ref[:, cs] = yp.astype(BF16)
            yt_ref[cs, :] = yp.T.astype(BF16)

    row = pl.BlockSpec((tm, E), lambda i: (i, 0))
    vec = pl.BlockSpec((1, E), lambda i: (0, 0))
    return pl.pallas_call(
        body, name="mm_grp", grid=(S // tm,),
        in_specs=[row, pl.BlockSpec((4, PC, PC), lambda i: (0, 0, 0)), vec, vec,
                  pl.BlockSpec((tm, E), lambda i: (i, 1))],
        out_specs=[row, row, pl.BlockSpec((E, tm), lambda i: (0, i))],
        out_shape=[SDS((S, E), BF16), SDS((S, E), BF16), SDS((E, S), BF16)],
        compiler_params=_cp(("parallel",), 48),
    )(pooled, wg, b, scale, uz)


def mm_dyp(da1, w_out, uz, h, scale):
    S = da1.shape[0]
    tm = 512

    def body(a_ref, w_ref, z_ref, h_ref, s_ref, dh_ref, dz_ref, dsc_ref, db_ref):
        @pl.when(pl.program_id(0) == 0)
        def _():
            dsc_ref[...] = jnp.zeros_like(dsc_ref)
            db_ref[...] = jnp.zeros_like(db_ref)

        for cidx in range(E // 256):
            col = slice(cidx * 256, (cidx + 1) * 256)
            dyp = _dot_nt(a_ref[...], w_ref[col, :])
            z = z_ref[:, col]
            hh = h_ref[:, col].astype(F32)
            sc = s_ref[:, col]
            sig = _sigmoid(z)
            dhs = dyp * z * sig
            dz_ref[:, col] = (dyp * hh * sc * sig * (1.0 + z * (1.0 - sig))).astype(BF16)
            dh = dhs * sc
            dh_ref[:, col] = dh.astype(BF16)
            dsc_ref[:, col] += _fold8(dhs * hh)
            db_ref[:, col] += _fold8(dh)

    row = pl.BlockSpec((tm, E), lambda i: (i, 0))
    acc = pl.BlockSpec((8, E), lambda i: (0, 0))
    return pl.pallas_call(
        body, name="mm_dyp", grid=(S // tm,),
        in_specs=[pl.BlockSpec((tm, D), lambda i: (i, 0)), pl.BlockSpec((E, D), lambda i: (0, 0)),
                  pl.BlockSpec((tm, E), lambda i: (i, 1)), row, pl.BlockSpec((1, E), lambda i: (0, 0))],
        out_specs=[row, pl.BlockSpec((tm, E), lambda i: (i, 1)), acc, acc],
        out_shape=[SDS((S, E), BF16), SDS((S, 2 * E), BF16), SDS((8, E), F32), SDS((8, E), F32)],
        compiler_params=_cp(("arbitrary",), 48),
    )(da1, w_out, uz, h, scale)


def mm_dpooled(dh, wg):
    S = dh.shape[0]
    tm = 1024

    def body(a_ref, w_ref, o_ref):
        for g in range(4):
            cs = slice(g * PC, (g + 1) * PC)
            o_ref[:, cs] = _dot_nt(a_ref[:, cs], w_ref[g]).astype(BF16)

    row = pl.BlockSpec((tm, E), lambda i: (i, 0))
    return pl.pallas_call(
        body, name="mm_dpooled", grid=(S // tm,),
        in_specs=[row, pl.BlockSpec((4, PC, PC), lambda i: (0, 0, 0))],
        out_specs=row, out_shape=SDS((S, E), BF16),
        compiler_params=_cp(("parallel",), 48),
    )(dh, wg)


def _rope_tables(positions):
    inv_freq = 500000.0 ** (-jnp.arange(0, 32, 2, dtype=F32) / 32)
    S = positions.shape[0]
    ang = jnp.repeat(positions.astype(F32).reshape(S // 8, 8), 16, axis=1) * jnp.tile(inv_freq, 8)
    cos, sin = lax.optimization_barrier((jnp.cos(ang), jnp.sin(ang)))
    cos, sin = cos.reshape(S, 16), sin.reshape(S, 16)
    one = jnp.ones((S, HD - 32), F32)
    zero16 = jnp.zeros((S, 16), F32)
    zero = jnp.zeros((S, HD - 32), F32)
    c = jnp.concatenate([cos, cos, one], axis=1)
    s1 = jnp.concatenate([-sin, zero16, zero], axis=1)
    s2 = jnp.concatenate([zero16, sin, zero], axis=1)
    return c.astype(BF16), s1.astype(BF16), s2.astype(BF16)


def kernel(x, positions, norm_pre, norm_post, attn_w_in, attn_w_out, pool_w_in, pool_w_grp, pool_b_grp, pool_scale, pool_w_out, loss_target, m_norm_pre, m_norm_post, m_attn_w_in, m_attn_w_out, m_pool_w_in, m_pool_w_grp, m_pool_b_grp, m_pool_scale, m_pool_w_out, v_norm_pre, v_norm_post, v_attn_w_in, v_attn_w_out, v_pool_w_in, v_pool_w_grp, v_pool_b_grp, v_pool_scale, v_pool_w_out):
    S = x.shape[1]
    xi, yi, ci = _mesh_pos()
    dev = 4 * xi + 2 * yi + ci
    x2 = x[0]
    tgt = loss_target[0]

    small = jnp.concatenate([pool_b_grp[0].reshape(2, HD), pool_scale[0].reshape(2, HD),
                             jnp.zeros((4, HD), F32)], axis=0)
    w_in_l = attn_w_in[0].astype(BF16)
    hop1, hop1_token = split_start("gather_w_in_start", [w_in_l], [lax.empty((N_DEV,) + w_in_l.shape, BF16)],
                                   _hop1_plan(), 3)

    pos = positions[0]
    tabs = [_rope_tables(pos.reshape(S // d, d).T.reshape(S)) for d in DIL]
    ehot = (jnp.arange(E)[None, :] // HD == jnp.arange(HD)[:, None]).astype(BF16)
    seg_tiles = SEG // CT

    xn0, xn0_4, xn0_16, xn0t = norm_pre0(x2, norm_pre[0:1], hop1_token)
    xn0s = [xn0, xn0_4.reshape(S, D), xn0_16.reshape(S, D)]
    xn0ts = [xn0t, transpose_rows(xn0s[1], "xn0t_4"), transpose_rows(xn0s[2], "xn0t_16")]

    (w_in_l,), (w_in8,) = split_wait("gather_w_in_wait", hop1, _hop1_plan(), xn0ts[2])
    hop2, hop2_token = split_start("gather_w_in_hop2_start", [w_in8], None, _hop2_plan(D), 4)
    _, (w_in8,) = split_wait("gather_w_in_hop2_wait", hop2, _hop2_plan(D), hop2_token, inplace=True)
    hop3, hop3_token = split_start("gather_w_in_hop3_start", [w_in8], None, _hop3_plan(), 1)
    _, (w_in8,) = split_wait("gather_w_in_hop3_wait", hop3, _hop3_plan(), hop3_token, inplace=True)
    w_in8 = lax.dynamic_update_slice(w_in8, w_in_l[None], (dev, 0, 0))
    small, w_in8 = lax.optimization_barrier((small, w_in8))
    rest_l = [attn_w_out[0].astype(BF16), pool_w_in[0].astype(BF16), pool_w_grp[0].astype(BF16),
              pool_w_out[0].astype(BF16), small]
    rest_flight, rest_token = split_start(
        "gather_rest_start", rest_l, [lax.empty((N_DEV,) + a.shape, a.dtype) for a in rest_l], _peers_plan(), 7)
    Ps, os_, lses = [], [], []
    for g, d in enumerate(DIL):
        P = mm_in(xn0s[g], w_in8, g * seg_tiles, seg_tiles, tabs[g], f"mm_qkv{g}", after=rest_token)
        o, l = attn_fwd(P, g, d)
        Ps.append(P)
        os_.append(o)
        lses.append(l)
    z0 = mm_in(xn0, w_in8, 3 * seg_tiles, E // CT, None, "mm_z0")
    ys, lse3, ya, yat = combine_fwd(os_, lses, z0, ehot)

    rest_l, rest8 = split_wait("gather_rest_wait", rest_flight, _peers_plan(), ya)
    rest8 = [lax.dynamic_update_slice(r8, a[None], (dev,) + (0,) * a.ndim) for r8, a in zip(rest8, rest_l)]
    w_out8, wp_in8, wg8, wp_out8, small8 = rest8
    w_out = w_out8.reshape(E, D)
    wp_out = wp_out8.reshape(E, D)
    wp_in = wp_in8.transpose(1, 0, 2).reshape(D, 2 * E)
    wg = wg8.transpose(1, 0, 2, 3).reshape(4, PC, PC)
    b_full = small8[:, 0:2, :].reshape(N_DEV, 4, PC // N_DEV).transpose(1, 0, 2).reshape(1, E)
    scale_full = small8[:, 2:4, :].reshape(1, E)
    a0 = mm_rows(ya, w_out, "mm_out0", F32)
    h1, xn1, xn1t = post0_pre1(x2, a0, norm_post[0:1], norm_pre[1:2])

    uz = mm_rows(xn1, wp_in, "mm_uz", F32, tm=512)
    pooled, pooled_t = pool_fwd(uz)
    hgrp, yp, ypt = mm_grp(pooled, wg, b_full, scale_full, uz)
    a1 = mm_rows(yp, wp_out, "mm_out1", F32)
    dh2, da1, loss_rows, dg_post1 = post1_loss(h1, a1, tgt, norm_post[1:2])
    loss = lax.psum(0.5 / D * jnp.sum(loss_rows), ("x", "y", "c"))

    dh, duz, dscale_p, db_p = mm_dyp(da1, wp_out, uz, hgrp, scale_full)
    dpooled = mm_dpooled(dh, wg)
    duz = pool_bwd(dpooled, duz)
    g_wg = mm_dwg(pooled_t, dh)
    g_wp_out = mm_wgrad_rows(ypt, da1, "mm_dwp_out")
    g_wp_in = mm_dw_in_part(xn1t, duz, 0, None, "mm_dwp_in", shard=PC)
    dxn1 = mm_dx_full(duz, wp_in, "mm_dxn1")
    dh1, da0, dg_pre1, dg_post0 = mid_bwd(dxn1, dh2, h1, a0, norm_pre[1:2], norm_post[0:1])

    dys, dz0 = mm_dya(da0, w_out, z0, ys[0])
    g_w_out = mm_wgrad_rows(yat, da0, "mm_dw_out")
    g_w_in = mm_dw_in_part(xn0t, dz0, 3 * seg_tiles, None, "mm_dw_in_z")
    dPs = []
    for g, d in enumerate(DIL):
        dP = attn_bwd(Ps[g], dys[g], ys[g], lse3[g], tabs[g], g, d)
        g_w_in = mm_dw_in_part(xn0ts[g], dP, g * seg_tiles, g_w_in, f"mm_dw_in{g}")
        dPs.append(dP)

    cidx = ci.astype(jnp.int32).reshape(1)
    chip = (2 * xi + yi).astype(jnp.int32).reshape(1)
    fulls = [g_w_in, g_w_out.reshape(N_DEV, E // N_DEV, D), g_wp_in,
             g_wg.reshape(4, N_DEV, PC // N_DEV, PC).transpose(1, 0, 2, 3).reshape(N_DEV, 4 * PC // N_DEV, PC),
             g_wp_out.reshape(N_DEV, E // N_DEV, D)]
    pair_flight, pair_token = split_start(
        "rs_pair_start", fulls, [lax.empty((4,) + f.shape[1:], F32) for f in fulls], _pair_plan(), 4)
    dx_z = mm_dx_part(dz0, w_in8, 3 * seg_tiles, "mm_dxn0_z", after=pair_token)
    dx_0 = mm_dx_part(dPs[0], w_in8, 0, "mm_dxn0_0", after=dx_z)
    fulls, sibs = split_wait("rs_pair_wait", pair_flight, _pair_plan(), dx_0)
    parts = [pair_add(f, s, cidx, f"pair_add{k}") for k, (f, s) in enumerate(zip(fulls, sibs))]
    chips_flight, chips_token = split_start(
        "rs_chips_start", parts, [jnp.zeros(p.shape, BF16) for p in parts], _chips_plan(), 3)
    dx_1 = mm_dx_part(dPs[1], w_in8, seg_tiles, "mm_dxn0_1", after=chips_token)
    dx_2 = mm_dx_part(dPs[2], w_in8, 2 * seg_tiles, "mm_dxn0_2", after=dx_1)
    grad_x, dg_pre0 = pre0_bwd(dx_0, dx_z, dx_1, dx_2, dh1, x2, norm_pre[0:1])
    parts, recvs = split_wait("rs_chips_wait", chips_flight, _chips_plan(), grad_x)
    shards = [(attn_w_in, m_attn_w_in, v_attn_w_in), (attn_w_out, m_attn_w_out, v_attn_w_out),
              (pool_w_in, m_pool_w_in, v_pool_w_in), (pool_w_grp, m_pool_w_grp, v_pool_w_grp),
              (pool_w_out, m_pool_w_out, v_pool_w_out)]
    big = []
    for k, (recv, part, (w, m, v)) in enumerate(zip(recvs, parts, shards)):
        shp = w.shape
        r2 = recv.shape[1:]
        res = adamw_sum(recv, part, chip, w.reshape(r2), m.reshape(r2), v.reshape(r2), f"adamw{k}")
        big.append([t.reshape(shp) for t in res])

    smalls = jnp.concatenate([dg_pre0.sum(0, keepdims=True), dg_pre1.sum(0, keepdims=True),
                              dg_post0.sum(0, keepdims=True), dg_post1.sum(0, keepdims=True),
                              db_p.sum(0).reshape(2, D), dscale_p.sum(0).reshape(2, D)], axis=0)
    (smalls8,) = all_gather([smalls], "gather_small_grads")
    tot = sum_slots(smalls8, "sum_small_grads")
    g_norm_pre, g_norm_post = tot[0:2], tot[2:4]
    g_b = lax.dynamic_slice_in_dim(tot[4:6].reshape(4, PC), dev * (PC // N_DEV), PC // N_DEV, axis=1)[None]
    g_scale = lax.dynamic_slice_in_dim(tot[6:8].reshape(1, E), dev * (E // N_DEV), E // N_DEV, axis=1)
    sm = [adamw_small(g_norm_pre, norm_pre, m_norm_pre, v_norm_pre, "adamw_norm_pre"),
          adamw_small(g_norm_post, norm_post, m_norm_post, v_norm_post, "adamw_norm_post"),
          adamw_small(g_b, pool_b_grp, m_pool_b_grp, v_pool_b_grp, "adamw_b"),
          adamw_small(g_scale, pool_scale, m_pool_scale, v_pool_scale, "adamw_scale")]

    grads = [g_norm_pre, g_norm_post, big[0][0], big[1][0], big[2][0], big[3][0], g_b, g_scale, big[4][0]]

    def pick(k):
        return [sm[0][k - 1], sm[1][k - 1], big[0][k], big[1][k], big[2][k], big[3][k], sm[2][k - 1], sm[3][k - 1],
                big[4][k]]

    return (loss, grad_x[None], *grads, *pick(1), *pick(2), *pick(3))
```

```python
import math

import jax
import jax.numpy as jnp
from jax import lax
from jax.experimental import pallas as pl
from jax.experimental.pallas import tpu as pltpu

F32 = jnp.float32
BF16 = jnp.bfloat16
SDS = jax.ShapeDtypeStruct

N_DEV = 8
D = 1024
E = 2048
HD = 128
NH = E // HD
DIL = (1, 4, 16)
QB = 128
SEG = 3 * E
W_IN_COLS = 3 * SEG + E
W_SHARD = W_IN_COLS // N_DEV
CT = 512
PC = E // 4
EPS = 1e-6
NEG = -1e30
SCALE = 1.0 / math.sqrt(HD)
LR, B1, B2, ADAM_EPS, WD, STEP = 0.001, 0.9, 0.999, 1e-08, 0.01, 10
MIB = 1024 * 1024
ANY = pl.BlockSpec(memory_space=pl.ANY)
MESH = pl.DeviceIdType.MESH


def _cp(sem, mb):
    return pltpu.CompilerParams(dimension_semantics=sem, vmem_limit_bytes=mb * MIB)


def _dot(a, b):
    return jnp.dot(a, b, preferred_element_type=F32)


def _dot_nt(a, b):
    return lax.dot_general(a, b, (((1,), (1,)), ((), ())), preferred_element_type=F32)


def _rms(h):
    return lax.rsqrt(jnp.mean(h * h, axis=-1, keepdims=True) + EPS)


def _row_tile(R, C, budget):
    tr = R
    while tr * C * 4 > budget and tr % 16 == 0:
        tr //= 2
    return tr


def _fold8(t):
    return t.reshape(t.shape[0] // 8, 8, t.shape[1]).sum(axis=0)


def _sigmoid(z):
    return pl.reciprocal(1.0 + jnp.exp(-z), approx=True)


LANES = 128


def _scr(rows, C):
    return pltpu.VMEM((C // LANES, rows, LANES), F32)


def _scr_put(scr, val):
    for c in range(scr.shape[0]):
        scr[c] = val[:, c * LANES:(c + 1) * LANES]


def _scr_get(scr):
    return jnp.concatenate([scr[c] for c in range(scr.shape[0])], axis=1)


def _store_perm(dst_ref, scr, d):
    n = dst_ref.shape[1]
    for r in range(d):
        for c in range(scr.shape[0]):
            dst_ref[r, :, c * LANES:(c + 1) * LANES] = scr[c, pl.ds(r, n, stride=d), :].astype(dst_ref.dtype)


def _load_perm(scr, src_ref, d, add=False):
    n = src_ref.shape[1]
    for r in range(d):
        rows = pl.ds(r, n, stride=d)
        for c in range(scr.shape[0]):
            v = src_ref[r, :, c * LANES:(c + 1) * LANES].astype(F32)
            scr[c, rows, :] = scr[c, rows, :] + v if add else v


def _rope(t, c, s1, s2):
    t = t.astype(BF16)
    return t * c + pltpu.roll(t, HD - 16, 1) * s1 + pltpu.roll(t, 16, 1) * s2


def _unrope(t, c, s1, s2):
    t = t.astype(BF16)
    return t * c - pltpu.roll(t, HD - 16, 1) * s1 - pltpu.roll(t, 16, 1) * s2


def _mesh_pos():
    return lax.axis_index("x"), lax.axis_index("y"), lax.axis_index("c")


def all_gather(arrs, name):
    n = len(arrs)

    def body(*refs):
        ins, outs = refs[:n], refs[n:2 * n]
        send_sems, recv_sems, local_sems = refs[2 * n:]
        x, y, c = _mesh_pos()
        me, sib = (x, y, c), (x, y, 1 - c)
        chips = [(1 - x, y), (x, 1 - y), (1 - x, 1 - y)]

        def slot(p):
            return 4 * p[0] + 2 * p[1] + p[2]

        def copy(a, k, block, to, src=None):
            dst = outs[a].at[slot(block)]
            return pltpu.make_async_remote_copy(
                src_ref=dst if src is None else src, dst_ref=dst,
                send_sem=send_sems.at[a, k], recv_sem=recv_sems.at[a, k],
                device_id=to, device_id_type=MESH)

        mine = [pltpu.make_async_copy(ins[a], outs[a].at[slot(me)], local_sems.at[a]) for a in range(n)]
        for cp in mine:
            cp.start()
        first = []
        for a in range(n):
            first.append(copy(a, 0, me, sib, src=ins[a]))
            for j, chip in enumerate(chips):
                first.append(copy(a, 1 + j, me, (*chip, c), src=ins[a]))
        for cp in first:
            cp.start()
        passed = []
        for j, chip in enumerate(chips):
            for a in range(n):
                copy(a, 1 + j, (*chip, c), me).wait_recv()
                fw = copy(a, 4 + j, (*chip, c), sib)
                fw.start()
                passed.append(fw)
        for a in range(n):
            copy(a, 0, sib, me).wait_recv()
        for j, chip in enumerate(chips):
            for a in range(n):
                copy(a, 4 + j, (*chip, 1 - c), me).wait_recv()
        for cp in first + passed:
            cp.wait_send()
        for cp in mine:
            cp.wait()

    return pl.pallas_call(
        body, name=name,
        out_shape=[SDS((N_DEV,) + a.shape, a.dtype) for a in arrs],
        in_specs=[ANY] * n, out_specs=[ANY] * n,
        scratch_shapes=[pltpu.SemaphoreType.DMA((n, 7)), pltpu.SemaphoreType.DMA((n, 7)),
                        pltpu.SemaphoreType.DMA((n,))],
    )(*arrs)


HBM_SPEC = pl.BlockSpec(memory_space=pltpu.HBM)
SEM_SPEC = pl.BlockSpec(memory_space=pltpu.SEMAPHORE)
EFFECT = pltpu.SideEffectType.DATAFLOW_SIDE_EFFECTING


def _pair_plan():
    def plan(x, y, c):
        return [(2 * q + (1 - c), q, (x, y, 1 - c)) for q in range(4)]
    return plan


def _chips_plan():
    def plan(x, y, c):
        chips = [(1 - x, y), (x, 1 - y), (1 - x, 1 - y)]
        return [(2 * cx + cy, 2 * x + y, (cx, cy, c)) for cx, cy in chips]
    return plan


def _hop1_plan():
    def plan(x, y, c):
        me = 4 * x + 2 * y + c
        return [(None, me, (x, y, 1 - c)), (None, me, (1 - x, y, c)), (None, me, (x, 1 - y, c))]
    return plan


def _hop2_plan(rows):
    half = rows // 2

    def plan(x, y, c):
        sx, sy = 4 * (1 - x) + 2 * y + c, 4 * x + 2 * (1 - y) + c
        top, bottom = pl.ds(0, half), pl.ds(half, half)
        return [((sx, top), (sx, top), (x, 1 - y, c)), ((sy, bottom), (sy, bottom), (1 - x, y, c)),
                (sx, sx, (x, y, 1 - c)), (sy, sy, (x, y, 1 - c))]
    return plan


def _hop3_plan():
    def plan(x, y, c):
        sd = 4 * (1 - x) + 2 * (1 - y) + c
        return [(sd, sd, (x, y, 1 - c))]
    return plan


def _peers_plan():
    def plan(x, y, c):
        out = []
        for k in range(1, N_DEV):
            fx, fy, fc = (k >> 2) & 1, (k >> 1) & 1, k & 1
            px, py, pc = (x + fx) % 2, (y + fy) % 2, (c + fc) % 2
            out.append((None, 4 * x + 2 * y + c, (px, py, pc)))
        return out
    return plan


def _split_copies(plan, srcs, lands, send_sems, recv_sems):
    x, y, c = _mesh_pos()
    cps = []
    for a, (src, land) in enumerate(zip(srcs, lands)):
        steps = plan(x, y, c)
        for k, (si, li, to) in enumerate(steps):
            sem = a * len(steps) + k
            cps.append(pltpu.make_async_remote_copy(
                src_ref=src if si is None else src.at[si], dst_ref=land.at[li],
                send_sem=send_sems.at[sem], recv_sem=recv_sems.at[sem],
                device_id=to, device_id_type=MESH))
    return cps


def split_start(name, srcs, lands, plan, nk):
    n = len(srcs)
    ops = list(srcs) + ([] if lands is None else list(lands))
    nb = len(ops)

    def body(*refs):
        token = refs[-1]
        for cp in _split_copies(plan, refs[:n], refs[nb - n:nb], refs[nb], refs[nb + 1]):
            cp.start()
        token[...] = jnp.zeros_like(token)

    ops = [pltpu.with_memory_space_constraint(a, pltpu.HBM) for a in ops]
    res = pl.pallas_call(
        body, name=name,
        out_shape=(pltpu.SemaphoreType.DMA((n * nk,)), pltpu.SemaphoreType.DMA((n * nk,)),
                   *[pltpu.HBM(a.shape, a.dtype) for a in ops], SDS((8, 128), F32)),
        in_specs=[HBM_SPEC] * nb,
        out_specs=(SEM_SPEC, SEM_SPEC, *[HBM_SPEC] * nb, pl.BlockSpec(memory_space=pltpu.VMEM)),
        input_output_aliases={i: 2 + i for i in range(nb)},
        compiler_params=pltpu.CompilerParams(has_side_effects=EFFECT),
    )(*ops)
    return res[:-1], res[-1]


def split_wait(name, flight, plan, after, inplace=False):
    send_sems, recv_sems = flight[0], flight[1]
    bufs = list(flight[2:])
    nb = len(bufs)
    n = nb if inplace else nb // 2

    def body(*refs):
        for cp in _split_copies(plan, refs[:n], refs[nb - n:nb], refs[nb], refs[nb + 1]):
            cp.wait_send()
            cp.wait_recv()

    res = pl.pallas_call(
        body, name=name,
        out_shape=[pltpu.HBM(a.shape, a.dtype) for a in bufs],
        in_specs=[HBM_SPEC] * nb + [SEM_SPEC, SEM_SPEC, ANY],
        out_specs=[HBM_SPEC] * nb,
        input_output_aliases={i: i for i in range(nb)},
        compiler_params=pltpu.CompilerParams(has_side_effects=EFFECT),
    )(*bufs, send_sems, recv_sems, after)
    return res[:n], res[nb - n:]


def pair_add(full, sib, cidx, name):
    _, R, C = full.shape
    tr = _row_tile(R, C, MIB)

    def body(c_ref, a_ref, b_ref, o_ref):
        o_ref[...] = (a_ref[...] + b_ref[...]).astype(BF16)

    return pl.pallas_call(
        body, name=name,
        grid_spec=pltpu.PrefetchScalarGridSpec(
            num_scalar_prefetch=1, grid=(4, R // tr),
            in_specs=[pl.BlockSpec((None, tr, C), lambda q, i, cr: (2 * q + cr[0], i, 0)),
                      pl.BlockSpec((None, tr, C), lambda q, i, cr: (q, i, 0))],
            out_specs=pl.BlockSpec((None, tr, C), lambda q, i, cr: (q, i, 0))),
        out_shape=SDS((4, R, C), BF16),
        compiler_params=_cp(("parallel", "parallel"), 32),
    )(cidx, full, sib)


def _adam_math(w, g, m, v):
    m2 = B1 * m + (1.0 - B1) * g
    v2 = B2 * v + (1.0 - B2) * (g * g)
    m_hat = m2 / (1.0 - B1 ** STEP)
    v_hat = v2 / (1.0 - B2 ** STEP)
    delta = -LR * (m_hat / (jnp.sqrt(v_hat) + ADAM_EPS) + WD * w)
    return delta, m2, v2


def adamw_sum(recv, part, chip, w, m, v, name):
    K, R, C = recv.shape
    tr = _row_tile(R, C, MIB)

    def body(chip_ref, r_ref, p_ref, w_ref, m_ref, v_ref, g_ref, d_ref, m2_ref, v2_ref):
        g = r_ref[0].astype(F32)
        for k in range(1, K):
            g = g + r_ref[k].astype(F32)
        g = g + p_ref[...].astype(F32)
        delta, m2, v2 = _adam_math(w_ref[...], g, m_ref[...], v_ref[...])
        g_ref[...] = g
        d_ref[...] = delta
        m2_ref[...] = m2
        v2_ref[...] = v2

    tile = pl.BlockSpec((tr, C), lambda i, cr: (i, 0))
    return pl.pallas_call(
        body, name=name,
        grid_spec=pltpu.PrefetchScalarGridSpec(
            num_scalar_prefetch=1, grid=(R // tr,),
            in_specs=[pl.BlockSpec((K, tr, C), lambda i, cr: (0, i, 0)),
                      pl.BlockSpec((None, tr, C), lambda i, cr: (cr[0], i, 0)), tile, tile, tile],
            out_specs=[tile] * 4),
        out_shape=[SDS((R, C), F32)] * 4,
        compiler_params=_cp(("parallel",), 32),
    )(chip, recv, part, w, m, v)


def adamw_small(g, w, m, v, name):
    def body(g_ref, w_ref, m_ref, v_ref, d_ref, m2_ref, v2_ref):
        delta, m2, v2 = _adam_math(w_ref[...], g_ref[...], m_ref[...], v_ref[...])
        d_ref[...] = delta
        m2_ref[...] = m2
        v2_ref[...] = v2

    return pl.pallas_call(body, name=name, out_shape=[SDS(w.shape, F32)] * 3)(g, w, m, v)


def sum_slots(a, name):
    K = a.shape[0]

    def body(a_ref, o_ref):
        t = a_ref[0]
        for k in range(1, K):
            t = t + a_ref[k]
        o_ref[...] = t

    return pl.pallas_call(body, name=name, out_shape=SDS(a.shape[1:], F32))(a)


def norm_pre0(x, g, after):
    S = x.shape[0]
    ts = 512

    def body(x_ref, g_ref, after_ref, o_ref, o4_ref, o16_ref, ot_ref, scr):
        h = x_ref[...]
        xn = h * _rms(h) * g_ref[...]
        o_ref[...] = xn.astype(BF16)
        ot_ref[...] = xn.T.astype(BF16)
        _scr_put(scr, xn)
        _store_perm(o4_ref, scr, 4)
        _store_perm(o16_ref, scr, 16)

    return pl.pallas_call(
        body, name="norm_pre0", grid=(S // ts,),
        in_specs=[pl.BlockSpec((ts, D), lambda i: (i, 0)), pl.BlockSpec((1, D), lambda i: (0, 0)), ANY],
        out_specs=[pl.BlockSpec((ts, D), lambda i: (i, 0)),
                   pl.BlockSpec((4, ts // 4, D), lambda i: (0, i, 0)),
                   pl.BlockSpec((16, ts // 16, D), lambda i: (0, i, 0)),
                   pl.BlockSpec((D, ts), lambda i: (0, i))],
        out_shape=[SDS((S, D), BF16), SDS((4, S // 4, D), BF16), SDS((16, S // 16, D), BF16), SDS((D, S), BF16)],
        scratch_shapes=[_scr(ts, D)],
        compiler_params=_cp(("parallel",), 32),
    )(x, g, after)


def transpose_rows(a, name):
    S, C = a.shape
    ts = 512

    def body(a_ref, o_ref):
        o_ref[...] = a_ref[...].astype(F32).T.astype(BF16)

    return pl.pallas_call(
        body, name=name, grid=(S // ts,),
        in_specs=[pl.BlockSpec((ts, C), lambda i: (i, 0))],
        out_specs=pl.BlockSpec((C, ts), lambda i: (0, i)),
        out_shape=SDS((C, S), BF16),
        compiler_params=_cp(("parallel",), 32),
    )(a)


def post0_pre1(x, a0, g_post, g_pre):
    S = x.shape[0]
    ts = 512

    def body(x_ref, a_ref, gp_ref, gn_ref, h_ref, o_ref, ot_ref):
        a = a_ref[...]
        h1 = x_ref[...] + a * _rms(a) * gp_ref[...]
        h_ref[...] = h1
        xn = h1 * _rms(h1) * gn_ref[...]
        o_ref[...] = xn.astype(BF16)
        ot_ref[...] = xn.T.astype(BF16)

    row = pl.BlockSpec((ts, D), lambda i: (i, 0))
    vec = pl.BlockSpec((1, D), lambda i: (0, 0))
    return pl.pallas_call(
        body, name="post0_pre1", grid=(S // ts,),
        in_specs=[row, row, vec, vec],
        out_specs=[row, row, pl.BlockSpec((D, ts), lambda i: (0, i))],
        out_shape=[SDS((S, D), F32), SDS((S, D), BF16), SDS((D, S), BF16)],
        compiler_params=_cp(("parallel",), 40),
    )(x, a0, g_post, g_pre)


def post1_loss(h1, a1, target, g_post):
    S = h1.shape[0]
    ts = 512

    def body(h_ref, a_ref, t_ref, g_ref, dh_ref, da_ref, loss_ref, dg_ref):
        @pl.when(pl.program_id(0) == 0)
        def _():
            loss_ref[...] = jnp.zeros_like(loss_ref)
            dg_ref[...] = jnp.zeros_like(dg_ref)

        a = a_ref[...]
        g = g_ref[...]
        rp = _rms(a)
        yhat = a * rp
        e = h_ref[...] + yhat * g - t_ref[...]
        loss_ref[...] += _fold8(e * e)
        dh = e * (1.0 / D)
        dh_ref[...] = dh
        dg_ref[...] += _fold8(dh * yhat)
        dyh = dh * g
        da = rp * (dyh - yhat * jnp.mean(dyh * yhat, axis=-1, keepdims=True))
        da_ref[...] = da.astype(BF16)

    row = pl.BlockSpec((ts, D), lambda i: (i, 0))
    acc = pl.BlockSpec((8, D), lambda i: (0, 0))
    return pl.pallas_call(
        body, name="post1_loss", grid=(S // ts,),
        in_specs=[row, row, row, pl.BlockSpec((1, D), lambda i: (0, 0))],
        out_specs=[row, row, acc, acc],
        out_shape=[SDS((S, D), F32), SDS((S, D), BF16), SDS((8, D), F32), SDS((8, D), F32)],
        compiler_params=_cp(("arbitrary",), 40),
    )(h1, a1, target, g_post)


def mid_bwd(dxn1, dh2, h1, a0, g_pre1, g_post0):
    S = h1.shape[0]
    ts = 512

    def body(dx_ref, dh2_ref, h_ref, a_ref, gn_ref, gp_ref, dh1_ref, da_ref, dgn_ref, dgp_ref):
        @pl.when(pl.program_id(0) == 0)
        def _():
            dgn_ref[...] = jnp.zeros_like(dgn_ref)
            dgp_ref[...] = jnp.zeros_like(dgp_ref)

        h = h_ref[...]
        r1 = _rms(h)
        xhat = h * r1
        dxn = dx_ref[...]
        dgn_ref[...] += _fold8(dxn * xhat)
        dxh = dxn * gn_ref[...]
        dh1 = dh2_ref[...] + r1 * (dxh - xhat * jnp.mean(dxh * xhat, axis=-1, keepdims=True))
        dh1_ref[...] = dh1
        a = a_ref[...]
        rp = _rms(a)
        yhat = a * rp
        dgp_ref[...] += _fold8(dh1 * yhat)
        dyh = dh1 * gp_ref[...]
        da = rp * (dyh - yhat * jnp.mean(dyh * yhat, axis=-1, keepdims=True))
        da_ref[...] = da.astype(BF16)

    row = pl.BlockSpec((ts, D), lambda i: (i, 0))
    vec = pl.BlockSpec((1, D), lambda i: (0, 0))
    acc = pl.BlockSpec((8, D), lambda i: (0, 0))
    return pl.pallas_call(
        body, name="mid_bwd", grid=(S // ts,),
        in_specs=[row, row, row, row, vec, vec],
        out_specs=[row, row, acc, acc],
        out_shape=[SDS((S, D), F32), SDS((S, D), BF16), SDS((8, D), F32), SDS((8, D), F32)],
        compiler_params=_cp(("arbitrary",), 48),
    )(dxn1, dh2, h1, a0, g_pre1, g_post0)


def pre0_bwd(dx_tok, dx_z, dx4, dx16, dh1, x, g_pre0):
    S = x.shape[0]
    ts = 512

    def body(da_ref, dz_ref, d4_ref, d16_ref, dh_ref, x_ref, g_ref, gx_ref, dg_ref, scr):
        @pl.when(pl.program_id(0) == 0)
        def _():
            dg_ref[...] = jnp.zeros_like(dg_ref)

        _scr_put(scr, da_ref[...] + dz_ref[...])
        _load_perm(scr, d4_ref, 4, add=True)
        _load_perm(scr, d16_ref, 16, add=True)
        h = x_ref[...]
        r = _rms(h)
        xhat = h * r
        dxn = _scr_get(scr)
        dg_ref[...] += _fold8(dxn * xhat)
        dxh = dxn * g_ref[...]
        gx_ref[...] = dh_ref[...] + r * (dxh - xhat * jnp.mean(dxh * xhat, axis=-1, keepdims=True))

    row = pl.BlockSpec((ts, D), lambda i: (i, 0))
    return pl.pallas_call(
        body, name="pre0_bwd", grid=(S // ts,),
        in_specs=[row, row, pl.BlockSpec((4, ts // 4, D), lambda i: (0, i, 0)),
                  pl.BlockSpec((16, ts // 16, D), lambda i: (0, i, 0)), row, row,
                  pl.BlockSpec((1, D), lambda i: (0, 0))],
        out_specs=[row, pl.BlockSpec((8, D), lambda i: (0, 0))],
        out_shape=[SDS((S, D), F32), SDS((8, D), F32)],
        scratch_shapes=[_scr(ts, D)],
        compiler_params=_cp(("arbitrary",), 48),
    )(dx_tok, dx_z, dx4.reshape(4, S // 4, D), dx16.reshape(16, S // 16, D), dh1, x, g_pre0)


def mm_in(xn, w8, tile0, ntiles, tabs, name, after=None):
    S = xn.shape[0]
    tm = 512
    per = W_SHARD // CT
    N = ntiles * CT

    def body(a_ref, w_ref, *rest):
        o_ref, wcat, sems = rest[-3:]

        @pl.when(pl.program_id(0) == 0)
        def _():
            cps = [pltpu.make_async_copy(
                w_ref.at[(tile0 + u) // per, :, pl.ds(((tile0 + u) % per) * CT, CT)],
                wcat.at[:, pl.ds(u * CT, CT)], sems.at[u]) for u in range(ntiles)]
            for cp in cps:
                cp.start()
            for cp in cps:
                cp.wait()

        if tabs is not None:
            k_tabs = tuple(t_ref[...] for t_ref in rest[:3])
            q_tabs = tuple((t * SCALE).astype(BF16) for t in k_tabs)
        for t in range(ntiles):
            r = _dot(a_ref[...], wcat[:, t * CT:(t + 1) * CT])
            if tabs is None or t >= 2 * E // CT:
                o_ref[:, t * CT:(t + 1) * CT] = r.astype(BF16)
                continue
            c, s1, s2 = q_tabs if t < E // CT else k_tabs
            for hh in range(CT // HD):
                o_ref[:, t * CT + hh * HD:t * CT + (hh + 1) * HD] = _rope(
                    r[:, hh * HD:(hh + 1) * HD], c, s1, s2).astype(BF16)

    tab = pl.BlockSpec((tm, HD), lambda i: (i, 0))
    return pl.pallas_call(
        body, name=name, grid=(S // tm,),
        in_specs=[pl.BlockSpec((tm, D), lambda i: (i, 0)), ANY] + ([] if tabs is None else [tab] * 3)
        + ([] if after is None else [ANY]),
        out_specs=pl.BlockSpec((tm, N), lambda i: (i, 0)),
        out_shape=SDS((S, N), BF16),
        scratch_shapes=[pltpu.VMEM((D, N), BF16), pltpu.SemaphoreType.DMA((ntiles,))],
        compiler_params=_cp(("arbitrary",), 48),
    )(xn, w8, *(() if tabs is None else tabs), *(() if after is None else (after,)))


def mm_rows(a, b, name, out_dtype, tm=1024):
    M, K = a.shape
    N = b.shape[1]

    def body(a_ref, b_ref, o_ref):
        for cidx in range(N // 256):
            col = slice(cidx * 256, (cidx + 1) * 256)
            o_ref[:, col] = _dot(a_ref[...], b_ref[:, col]).astype(out_dtype)

    return pl.pallas_call(
        body, name=name, grid=(M // tm,),
        in_specs=[pl.BlockSpec((tm, K), lambda i: (i, 0)), pl.BlockSpec((K, N), lambda i: (0, 0))],
        out_specs=pl.BlockSpec((tm, N), lambda i: (i, 0)),
        out_shape=SDS((M, N), out_dtype),
        compiler_params=_cp(("parallel",), 48),
    )(a, b)


def mm_acc(a, b, name, *, grid, a_spec, b_spec, o_spec, o_shape, acc_shape, write, vmem=48):
    nk = grid[-1]

    def body(a_ref, b_ref, o_ref, acc_ref):
        k = pl.program_id(len(grid) - 1)

        @pl.when(k == 0)
        def _():
            acc_ref[...] = jnp.zeros_like(acc_ref)

        acc_ref[...] += _dot(a_ref[...], b_ref[...])

        @pl.when(k == nk - 1)
        def _():
            write(o_ref, acc_ref)

    return pl.pallas_call(
        body, name=name, grid=grid, in_specs=[a_spec, b_spec], out_specs=o_spec, out_shape=o_shape,
        scratch_shapes=[pltpu.VMEM(acc_shape, F32)],
        compiler_params=_cp(("parallel",) * (len(grid) - 1) + ("arbitrary",), vmem),
    )(a, b)


def _write_plain(o_ref, acc_ref):
    o_ref[...] = acc_ref[...]


def mm_wgrad_rows(at, b, name):
    M, S = at.shape
    N = b.shape[1]
    tm, tk = 1024, 1024
    return mm_acc(at, b, name, grid=(M // tm, S // tk),
                  a_spec=pl.BlockSpec((tm, tk), lambda i, k: (i, k)),
                  b_spec=pl.BlockSpec((tk, N), lambda i, k: (k, 0)),
                  o_spec=pl.BlockSpec((tm, N), lambda i, k: (i, 0)),
                  o_shape=SDS((M, N), F32), acc_shape=(tm, N), write=_write_plain)


def mm_wgrad_cols(at, b, name, *, shard):
    M, S = at.shape
    tk = 1024
    nb = 2
    tn = nb * shard

    def write(o_ref, acc_ref):
        for u in range(nb):
            o_ref[u] = acc_ref[:, u * shard:(u + 1) * shard]

    return mm_acc(at, b, name, grid=(N_DEV // nb, S // tk),
                  a_spec=pl.BlockSpec((M, tk), lambda t, k: (0, k)),
                  b_spec=pl.BlockSpec((tk, tn), lambda t, k: (k, t)),
                  o_spec=pl.BlockSpec((nb, M, shard), lambda t, k: (t, 0, 0)),
                  o_shape=SDS((N_DEV, M, shard), F32), acc_shape=(M, tn), write=write)


def mm_dwg(pooled_t, dh):
    S = dh.shape[0]
    tk = 2048
    return mm_acc(pooled_t, dh, "mm_dwg", grid=(4, S // tk),
                  a_spec=pl.BlockSpec((PC, tk), lambda g, k: (g, k)),
                  b_spec=pl.BlockSpec((tk, PC), lambda g, k: (k, g)),
                  o_spec=pl.BlockSpec((None, PC, PC), lambda g, k: (g, 0, 0)),
                  o_shape=SDS((4, PC, PC), F32), acc_shape=(PC, PC), write=_write_plain)


def mm_dx_full(da, w, name):
    S, K = da.shape
    N = w.shape[0]
    tm = 512

    def body(a_ref, b_ref, o_ref):
        o_ref[...] = _dot_nt(a_ref[...], b_ref[...])

    return pl.pallas_call(
        body, name=name, grid=(S // tm,),
        in_specs=[pl.BlockSpec((tm, K), lambda i: (i, 0)), pl.BlockSpec((N, K), lambda i: (0, 0))],
        out_specs=pl.BlockSpec((tm, N), lambda i: (i, 0)),
        out_shape=SDS((S, N), F32),
        compiler_params=_cp(("parallel",), 48),
    )(da, w)


NBUF = 3


def mm_dw_in_part(at, b, tile0, prev, name):
    M, S = at.shape
    ntiles = b.shape[1] // CT
    tk = 2048
    nk = S // tk
    per = W_SHARD // CT

    def body(a_ref, b_ref, *rest):
        o_ref, acc_ref, sems, bbuf, bsems = rest[-5:]
        k, t = pl.program_id(0), pl.program_id(1)
        n = k * ntiles + t

        def b_copy(m):
            r0, c0 = (m // ntiles) * tk, (m % ntiles) * CT
            if not isinstance(m, int):
                r0, c0 = pl.multiple_of(r0, tk), pl.multiple_of(c0, CT)
            return pltpu.make_async_copy(b_ref.at[pl.ds(r0, tk), pl.ds(c0, CT)],
                                         bbuf.at[m % NBUF], bsems.at[m % NBUF])

        @pl.when(n == 0)
        def _():
            for m in range(NBUF - 1):
                b_copy(m).start()

        @pl.when(n + NBUF - 1 < nk * ntiles)
        def _():
            b_copy(n + NBUF - 1).start()

        b_copy(n).wait()
        bt = bbuf[n % NBUF]

        @pl.when(k == 0)
        def _():
            acc_ref[t] = _dot(a_ref[...], bt)

        @pl.when(k > 0)
        def _():
            acc_ref[t] += _dot(a_ref[...], bt)

        def out_copy(u):
            tile = tile0 + u
            off = (tile % per) * CT
            if not isinstance(off, int):
                off = pl.multiple_of(off, CT)
            return pltpu.make_async_copy(acc_ref.at[u], o_ref.at[tile // per, :, pl.ds(off, CT)], sems.at[u])

        @pl.when(k == nk - 1)
        def _():
            out_copy(t).start()

        @pl.when(jnp.logical_and(k == nk - 1, t == ntiles - 1))
        def _():
            for u in range(ntiles):
                out_copy(u).wait()

    return pl.pallas_call(
        body, name=name, grid=(nk, ntiles),
        in_specs=[pl.BlockSpec((M, tk), lambda k, t: (0, k)), ANY] + ([] if prev is None else [ANY]),
        out_specs=ANY,
        out_shape=SDS((N_DEV, M, W_SHARD), F32),
        scratch_shapes=[pltpu.VMEM((ntiles, M, CT), F32), pltpu.SemaphoreType.DMA((ntiles,)),
                        pltpu.VMEM((NBUF, tk, CT), BF16), pltpu.SemaphoreType.DMA((NBUF,))],
        input_output_aliases={} if prev is None else {2: 0},
        compiler_params=_cp(("arbitrary", "arbitrary"), 48),
    )(at, b, *(() if prev is None else (prev,)))


def mm_dx_part(da, w8, tile0, name, after=None):
    S, K = da.shape
    ntiles = K // CT
    tm = 512
    per = W_SHARD // CT

    def body(a_ref, w_ref, *rest):
        o_ref, wcat, sems = rest[-3:]

        @pl.when(pl.program_id(0) == 0)
        def _():
            cps = [pltpu.make_async_copy(
                w_ref.at[(tile0 + u) // per, :, pl.ds(((tile0 + u) % per) * CT, CT)],
                wcat.at[:, pl.ds(u * CT, CT)], sems.at[u]) for u in range(ntiles)]
            for cp in cps:
                cp.start()
            for cp in cps:
                cp.wait()

        o_ref[...] = _dot_nt(a_ref[...], wcat[...])

    return pl.pallas_call(
        body, name=name, grid=(S // tm,),
        in_specs=[pl.BlockSpec((tm, K), lambda i: (i, 0)), ANY] + ([] if after is None else [ANY]),
        out_specs=pl.BlockSpec((tm, D), lambda i: (i, 0)),
        out_shape=SDS((S, D), F32),
        scratch_shapes=[pltpu.VMEM((D, K), BF16), pltpu.SemaphoreType.DMA((ntiles,))],
        compiler_params=_cp(("arbitrary",), 48),
    )(da, w8, *(() if after is None else (after,)))


HEADS_FWD = 4
HEADS_BWD = 2
AHEAD = 2


def _band_masks(not_first):
    row = lax.broadcasted_iota(jnp.int32, (QB, QB), 0)
    col = lax.broadcasted_iota(jnp.int32, (QB, QB), 1)
    cur = jnp.where(col <= row, 0.0, NEG)
    prev = jnp.where(col >= row, 0.0, NEG)
    first = jnp.where(jnp.logical_and(col >= row, not_first), 0.0, NEG)
    return col, jnp.concatenate([prev, cur], axis=1), jnp.concatenate([first, cur], axis=1)


def _fill_kv(ext, qkv_ref, kh_ref, vh_ref):
    ext[0:QB, 0:E] = kh_ref[...]
    ext[0:QB, E:2 * E] = vh_ref[...]
    ext[QB:, :] = qkv_ref[:, E:3 * E]


def attn_fwd(P, g, d):
    S = P.shape[0]
    L = S // d
    T = min(512, L)
    nq = T // QB
    ni = L // T

    def body(qkv_ref, kh_ref, vh_ref, o_ref, lse_ref, ext):
        col, mask, mask_first = _band_masks(pl.program_id(1) > 0)
        lse_ref[...] = jnp.zeros_like(lse_ref)
        _fill_kv(ext, qkv_ref, kh_ref, vh_ref)

        def heads(hp, carry):
            def front(h, j):
                cq = pl.ds(pl.multiple_of(h * HD, HD), HD)
                rows = slice(j * QB, (j + 1) * QB)
                krows = slice(j * QB, (j + 2) * QB)
                s = _dot_nt(qkv_ref[rows, cq], ext[krows, cq]) + (mask_first if j == 0 else mask)
                m = jnp.max(s, axis=1, keepdims=True)
                p = jnp.exp(s - m)
                den = jnp.sum(p, axis=1, keepdims=True)
                lse_ref[rows, :] = jnp.where(col == h, m + jnp.log(den), lse_ref[rows, :])
                return p.astype(BF16), den

            def back(h, j, p, den):
                off = pl.multiple_of(h * HD, HD)
                rows = slice(j * QB, (j + 1) * QB)
                krows = slice(j * QB, (j + 2) * QB)
                o_ref[rows, pl.ds(off, HD)] = (_dot(p, ext[krows, pl.ds(E + off, HD)]) / den).astype(BF16)

            items = [(HEADS_FWD * hp + hh, j) for hh in range(HEADS_FWD) for j in range(nq)]
            queue = [front(*it) for it in items[:AHEAD]]
            for u, it in enumerate(items):
                if u + AHEAD < len(items):
                    queue.append(front(*items[u + AHEAD]))
                back(*it, *queue.pop(0))
            return carry

        lax.fori_loop(0, NH // HEADS_FWD, heads, 0)

    halo = lambda r, i: jnp.maximum(r * (L // QB) + i * nq - 1, 0)
    return pl.pallas_call(
        body, name=f"attn_fwd{g}", grid=(d, ni),
        in_specs=[pl.BlockSpec((T, SEG), lambda r, i: (r * ni + i, 0)),
                  pl.BlockSpec((QB, E), lambda r, i: (halo(r, i), 1)),
                  pl.BlockSpec((QB, E), lambda r, i: (halo(r, i), 2))],
        out_specs=[pl.BlockSpec((T, E), lambda r, i: (r * ni + i, 0)),
                   pl.BlockSpec((T, HD), lambda r, i: (r * ni + i, 0))],
        out_shape=[SDS((S, E), BF16), SDS((S, HD), F32)],
        scratch_shapes=[pltpu.VMEM((T + QB, 2 * E), BF16)],
        compiler_params=_cp(("parallel", "parallel"), 48),
    )(P, P, P)


def _perm_specs(ts, C):
    return [pl.BlockSpec((ts, C), lambda i: (i, 0)),
            pl.BlockSpec((4, ts // 4, C), lambda i: (0, i, 0)),
            pl.BlockSpec((16, ts // 16, C), lambda i: (0, i, 0))]


def _perm_shapes(S, C, dtype):
    return [SDS((S, C), dtype), SDS((4, S // 4, C), dtype), SDS((16, S // 16, C), dtype)]


def combine_fwd(os_, lses, z, ehot):
    S = z.shape[0]
    ts = 256

    def body(o0, o1, o2, l0, l1, l2, z_ref, e_ref, y0, y1, y2, s0, s1, s2, ya_ref, yat_ref,
             so1, so2, sl1, sl2, sy, sl):
        _load_perm(so1, o1, 4)
        _load_perm(so2, o2, 16)
        _load_perm(sl1, l1, 4)
        _load_perm(sl2, l2, 16)
        ls = [l0[...], sl1[0], sl2[0]]
        m = jnp.maximum(jnp.maximum(ls[0], ls[1]), ls[2])
        es = [jnp.exp(l - m) for l in ls]
        den = es[0] + es[1] + es[2]
        sl[0] = m + jnp.log(den)
        y = None
        for e, o in zip(es, (o0[...].astype(F32), _scr_get(so1), _scr_get(so2))):
            w = e / den
            hi = w.astype(BF16)
            lo = (w - hi.astype(F32)).astype(BF16)
            wb = _dot(hi, e_ref[...]) + _dot(lo, e_ref[...])
            y = wb * o if y is None else y + wb * o
        z = z_ref[...].astype(F32)
        ya = y * (z * _sigmoid(z))
        ya_ref[...] = ya.astype(BF16)
        yat_ref[...] = ya.T.astype(BF16)
        _scr_put(sy, y)
        y0[...] = y.astype(BF16)
        _store_perm(y1, sy, 4)
        _store_perm(y2, sy, 16)
        s0[...] = sl[0]
        _store_perm(s1, sl, 4)
        _store_perm(s2, sl, 16)

    wide = pl.BlockSpec((ts, E), lambda i: (i, 0))
    os3 = [os_[0], os_[1].reshape(4, S // 4, E), os_[2].reshape(16, S // 16, E)]
    ls3 = [lses[0], lses[1].reshape(4, S // 4, HD), lses[2].reshape(16, S // 16, HD)]
    res = pl.pallas_call(
        body, name="combine_fwd", grid=(S // ts,),
        in_specs=_perm_specs(ts, E) + _perm_specs(ts, HD) + [wide, pl.BlockSpec((HD, E), lambda i: (0, 0))],
        out_specs=_perm_specs(ts, E) + _perm_specs(ts, HD) + [wide, pl.BlockSpec((E, ts), lambda i: (0, i))],
        out_shape=_perm_shapes(S, E, BF16) + _perm_shapes(S, HD, F32) + [SDS((S, E), BF16), SDS((E, S), BF16)],
        scratch_shapes=[_scr(ts, E), _scr(ts, E), _scr(ts, HD), _scr(ts, HD), _scr(ts, E), _scr(ts, HD)],
        compiler_params=_cp(("parallel",), 56),
    )(*os3, *ls3, z, ehot)
    ys = [res[0], res[1].reshape(S, E), res[2].reshape(S, E)]
    lse3 = [res[3], res[4].reshape(S, HD), res[5].reshape(S, HD)]
    return ys, lse3, res[6], res[7]


def mm_dya(da0, w_out, z, y):
    S = da0.shape[0]
    tm = 512

    def body(a_ref, w_ref, z_ref, y_ref, dy0, dy1, dy2, dz_ref, scr):
        for cidx in range(E // 256):
            col = slice(cidx * 256, (cidx + 1) * 256)
            dya = _dot_nt(a_ref[...], w_ref[col, :])
            zz = z_ref[:, col].astype(F32)
            sig = _sigmoid(zz)
            dy = dya * zz * sig
            scr[2 * cidx] = dy[:, :LANES]
            scr[2 * cidx + 1] = dy[:, LANES:]
            dy0[:, col] = dy.astype(BF16)
            dz_ref[:, col] = (dya * y_ref[:, col].astype(F32) * sig * (1.0 + zz * (1.0 - sig))).astype(BF16)
        _store_perm(dy1, scr, 4)
        _store_perm(dy2, scr, 16)

    wide = pl.BlockSpec((tm, E), lambda i: (i, 0))
    res = pl.pallas_call(
        body, name="mm_dya", grid=(S // tm,),
        in_specs=[pl.BlockSpec((tm, D), lambda i: (i, 0)), pl.BlockSpec((E, D), lambda i: (0, 0)), wide, wide],
        out_specs=_perm_specs(tm, E) + [wide],
        out_shape=_perm_shapes(S, E, BF16) + [SDS((S, E), BF16)],
        scratch_shapes=[_scr(tm, E)],
        compiler_params=_cp(("parallel",), 48),
    )(da0, w_out, z, y)
    return [res[0], res[1].reshape(S, E), res[2].reshape(S, E)], res[3]


def attn_bwd(P, dy, y, lse, tabs, g, d):
    S = P.shape[0]
    L = S // d
    T = min(512, L)
    nq = T // QB
    ni = L // T

    def body(qkv_ref, kh_ref, vh_ref, dy_ref, y_ref, lse_ref, c_ref, s1_ref, s2_ref,
             o_ref, dkc_ref, dvc_ref, ext):
        i = pl.program_id(1)
        _, mask, mask_first = _band_masks(i < ni - 1)
        _fill_kv(ext, qkv_ref, kh_ref, vh_ref)
        row_id2 = lax.broadcasted_iota(jnp.int32, (2 * QB, QB), 0)
        ones = jnp.ones((QB, QB), BF16)
        lse_hl = []
        for j in range(nq):
            t = lse_ref[j * QB:(j + 1) * QB, :]
            hi = t.astype(BF16)
            lse_hl.append(jnp.concatenate([hi, (t - hi.astype(F32)).astype(BF16)], axis=1))

        @pl.when(i == 0)
        def _():
            dkc_ref[...] = jnp.zeros_like(dkc_ref)
            dvc_ref[...] = jnp.zeros_like(dvc_ref)

        def heads(hp, carry):
            def cols(h):
                off = pl.multiple_of(h * HD, HD)
                return pl.ds(off, HD), pl.ds(E + off, HD), pl.ds(2 * E + off, HD)

            def front(h, j):
                cq, ck, _ = cols(h)
                rows = slice(j * QB, (j + 1) * QB)
                krows = slice(j * QB, (j + 2) * QB)
                dyj = dy_ref[rows, cq]
                sel = jnp.logical_or(row_id2 == h, row_id2 == h + QB).astype(BF16)
                lse_b = _dot(lse_hl[j], sel)
                delta_b = _dot((dyj.astype(F32) * y_ref[rows, cq].astype(F32)).astype(BF16), ones)
                s = _dot_nt(qkv_ref[rows, cq], ext[krows, cq])
                p = jnp.exp(s + (mask_first if j == 0 else mask) - jnp.concatenate([lse_b, lse_b], axis=1))
                ds = (p * (_dot_nt(dyj, ext[krows, ck]) - jnp.concatenate([delta_b, delta_b], axis=1))).astype(BF16)
                return ds, jnp.concatenate([ds, p.astype(BF16)], axis=0).T

            def back(h, j, ds, dsp_t, pend_dk, pend_dv):
                cq, ck, cv = cols(h)
                rows = slice(j * QB, (j + 1) * QB)
                krows = slice(j * QB, (j + 2) * QB)
                dq = _dot(ds, ext[krows, cq]) * SCALE
                zero = jnp.zeros((QB, HD), BF16)
                bd = jnp.concatenate([jnp.concatenate([qkv_ref[rows, cq], zero], axis=1),
                                      jnp.concatenate([zero, dy_ref[rows, cq]], axis=1)], axis=0)
                dkv = _dot(dsp_t, bd)
                dk2, dv2 = dkv[:, :HD], dkv[:, HD:]
                c, s1, s2 = c_ref[rows, :], s1_ref[rows, :], s2_ref[rows, :]
                o_ref[rows, cq] = _unrope(dq, c, s1, s2).astype(BF16)
                o_ref[rows, ck] = _unrope(dk2[QB:] + pend_dk, c, s1, s2).astype(BF16)
                o_ref[rows, cv] = (dv2[QB:] + pend_dv).astype(BF16)
                return dk2[:QB], dv2[:QB]

            items = [(HEADS_BWD * hp + hh, j) for hh in range(HEADS_BWD) for j in reversed(range(nq))]
            queue = [front(*it) for it in items[:AHEAD]]
            pend = None
            for u, (h, j) in enumerate(items):
                if u + AHEAD < len(items):
                    queue.append(front(*items[u + AHEAD]))
                if j == nq - 1:
                    pend = (dkc_ref[:, cols(h)[0]], dvc_ref[:, cols(h)[0]])
                pend = back(h, j, *queue.pop(0), *pend)
                if j == 0:
                    dkc_ref[:, cols(h)[0]], dvc_ref[:, cols(h)[0]] = pend
            return carry

        lax.fori_loop(0, NH // HEADS_BWD, heads, 0)

    blk = lambda r, i: r * ni + ni - 1 - i
    halo = lambda r, i: jnp.maximum(r * (L // QB) + (ni - 1 - i) * nq - 1, 0)
    main = pl.BlockSpec((T, SEG), lambda r, i: (blk(r, i), 0))
    wide = pl.BlockSpec((T, E), lambda r, i: (blk(r, i), 0))
    narrow = pl.BlockSpec((T, HD), lambda r, i: (blk(r, i), 0))
    return pl.pallas_call(
        body, name=f"attn_bwd{g}", grid=(d, ni),
        in_specs=[main, pl.BlockSpec((QB, E), lambda r, i: (halo(r, i), 1)),
                  pl.BlockSpec((QB, E), lambda r, i: (halo(r, i), 2)),
                  wide, wide, narrow, narrow, narrow, narrow],
        out_specs=main, out_shape=SDS((S, SEG), BF16),
        scratch_shapes=[pltpu.VMEM((QB, E), F32), pltpu.VMEM((QB, E), F32), pltpu.VMEM((T + QB, 2 * E), BF16)],
        compiler_params=_cp(("arbitrary", "arbitrary"), 56),
    )(P, P, P, dy, y, lse, *tabs)


def _pool_cnt(t0, rows):
    t = (lax.broadcasted_iota(jnp.int32, (rows, E), 0) + t0 + 1).astype(F32)
    ch = lax.broadcasted_iota(jnp.int32, (rows, E), 1)
    w = jnp.where(ch < PC, 2.0, jnp.where(ch < 2 * PC, 4.0, jnp.where(ch < 3 * PC, 8.0, 16.0)))
    return jnp.minimum(t, w)


def _by_group(parts):
    return jnp.concatenate([parts[g][:, g * PC:(g + 1) * PC] for g in range(4)], axis=1)


def pool_fwd(uz):
    S = uz.shape[0]
    ts = 256

    def body(u_ref, h_ref, o_ref, ot_ref):
        i = pl.program_id(0)
        u = u_ref[...]
        halo = jnp.where(i > 0, h_ref[...], 0.0)
        ext = jnp.concatenate([halo, u], axis=0)
        s2 = ext + pltpu.roll(ext, 1, 0)
        s4 = s2 + pltpu.roll(s2, 2, 0)
        s8 = s4 + pltpu.roll(s4, 4, 0)
        s16 = s8 + pltpu.roll(s8, 8, 0)
        win = _by_group([s2, s4, s8, s16])[16:, :]
        pooled = win / _pool_cnt(i * ts, ts) - u
        o_ref[...] = pooled.astype(BF16)
        ot_ref[...] = pooled.T.astype(BF16)

    return pl.pallas_call(
        body, name="pool_fwd", grid=(S // ts,),
        in_specs=[pl.BlockSpec((ts, E), lambda i: (i, 0)),
                  pl.BlockSpec((16, E), lambda i: (jnp.maximum(i * (ts // 16) - 1, 0), 0))],
        out_specs=[pl.BlockSpec((ts, E), lambda i: (i, 0)), pl.BlockSpec((E, ts), lambda i: (0, i))],
        out_shape=[SDS((S, E), BF16), SDS((E, S), BF16)],
        compiler_params=_cp(("parallel",), 48),
    )(uz, uz)


def pool_bwd(dpooled, duz):
    S = dpooled.shape[0]
    ts = 256
    nt = S // ts

    def body(d_ref, h_ref, alias_ref, o_ref):
        i = pl.program_id(0)
        dp = d_ref[...].astype(F32)
        halo = jnp.where(i < nt - 1, h_ref[...].astype(F32), 0.0)
        n = ts + 16
        ext = jnp.concatenate([dp, halo], axis=0) / _pool_cnt(i * ts, n)
        f2 = ext + pltpu.roll(ext, n - 1, 0)
        f4 = f2 + pltpu.roll(f2, n - 2, 0)
        f8 = f4 + pltpu.roll(f4, n - 4, 0)
        f16 = f8 + pltpu.roll(f8, n - 8, 0)
        win = _by_group([f2, f4, f8, f16])[:ts, :]
        o_ref[...] = (win - dp).astype(BF16)

    return pl.pallas_call(
        body, name="pool_bwd", grid=(nt,),
        in_specs=[pl.BlockSpec((ts, E), lambda i: (i, 0)),
                  pl.BlockSpec((16, E), lambda i: (jnp.minimum((i + 1) * (ts // 16), S // 16 - 1), 0)), ANY],
        out_specs=pl.BlockSpec((ts, E), lambda i: (i, 0)),
        out_shape=SDS(duz.shape, BF16),
        input_output_aliases={2: 0},
        compiler_params=_cp(("parallel",), 48),
    )(dpooled, dpooled, duz)


def mm_grp(pooled, wg, b, scale, uz):
    S = pooled.shape[0]
    tm = 512

    def body(p_ref, w_ref, b_ref, s_ref, z_ref, h_ref, y_ref, yt_ref):
        for g in range(4):
            cs = slice(g * PC, (g + 1) * PC)
            h = _dot(p_ref[:, cs], w_ref[g]) + b_ref[:, cs]
            z = z_ref[:, cs]
            yp = h * s_ref[:, cs] * (z * _sigmoid(z))
            h_ref[:, cs] = h.astype(BF16)
            y_ref[:, cs] = yp.astype(BF16)
            yt_ref[cs, :] = yp.T.astype(BF16)

    row = pl.BlockSpec((tm, E), lambda i: (i, 0))
    vec = pl.BlockSpec((1, E), lambda i: (0, 0))
    return pl.pallas_call(
        body, name="mm_grp", grid=(S // tm,),
        in_specs=[row, pl.BlockSpec((4, PC, PC), lambda i: (0, 0, 0)), vec, vec,
                  pl.BlockSpec((tm, E), lambda i: (i, 1))],
        out_specs=[row, row, pl.BlockSpec((E, tm), lambda i: (0, i))],
        out_shape=[SDS((S, E), BF16), SDS((S, E), BF16), SDS((E, S), BF16)],
        compiler_params=_cp(("parallel",), 48),
    )(pooled, wg, b, scale, uz)


def mm_dyp(da1, w_out, uz, h, scale):
    S = da1.shape[0]
    tm = 512

    def body(a_ref, w_ref, z_ref, h_ref, s_ref, dh_ref, dz_ref, dsc_ref, db_ref):
        @pl.when(pl.program_id(0) == 0)
        def _():
            dsc_ref[...] = jnp.zeros_like(dsc_ref)
            db_ref[...] = jnp.zeros_like(db_ref)

        for cidx in range(E // 256):
            col = slice(cidx * 256, (cidx + 1) * 256)
            dyp = _dot_nt(a_ref[...], w_ref[col, :])
            z = z_ref[:, col]
            hh = h_ref[:, col].astype(F32)
            sc = s_ref[:, col]
            sig = _sigmoid(z)
            dhs = dyp * z * sig
            dz_ref[:, col] = (dyp * hh * sc * sig * (1.0 + z * (1.0 - sig))).astype(BF16)
            dh = dhs * sc
            dh_ref[:, col] = dh.astype(BF16)
            dsc_ref[:, col] += _fold8(dhs * hh)
            db_ref[:, col] += _fold8(dh)

    row = pl.BlockSpec((tm, E), lambda i: (i, 0))
    acc = pl.BlockSpec((8, E), lambda i: (0, 0))
    return pl.pallas_call(
        body, name="mm_dyp", grid=(S // tm,),
        in_specs=[pl.BlockSpec((tm, D), lambda i: (i, 0)), pl.BlockSpec((E, D), lambda i: (0, 0)),
                  pl.BlockSpec((tm, E), lambda i: (i, 1)), row, pl.BlockSpec((1, E), lambda i: (0, 0))],
        out_specs=[row, pl.BlockSpec((tm, E), lambda i: (i, 1)), acc, acc],
        out_shape=[SDS((S, E), BF16), SDS((S, 2 * E), BF16), SDS((8, E), F32), SDS((8, E), F32)],
        compiler_params=_cp(("arbitrary",), 48),
    )(da1, w_out, uz, h, scale)


def mm_dpooled(dh, wg):
    S = dh.shape[0]
    tm = 1024

    def body(a_ref, w_ref, o_ref):
        for g in range(4):
            cs = slice(g * PC, (g + 1) * PC)
            o_ref[:, cs] = _dot_nt(a_ref[:, cs], w_ref[g]).astype(BF16)

    row = pl.BlockSpec((tm, E), lambda i: (i, 0))
    return pl.pallas_call(
        body, name="mm_dpooled", grid=(S // tm,),
        in_specs=[row, pl.BlockSpec((4, PC, PC), lambda i: (0, 0, 0))],
        out_specs=row, out_shape=SDS((S, E), BF16),
        compiler_params=_cp(("parallel",), 48),
    )(dh, wg)


def _rope_tables(positions):
    inv_freq = 500000.0 ** (-jnp.arange(0, 32, 2, dtype=F32) / 32)
    S = positions.shape[0]
    ang = jnp.repeat(positions.astype(F32).reshape(S // 8, 8), 16, axis=1) * jnp.tile(inv_freq, 8)
    cos, sin = lax.optimization_barrier((jnp.cos(ang), jnp.sin(ang)))
    cos, sin = cos.reshape(S, 16), sin.reshape(S, 16)
    one = jnp.ones((S, HD - 32), F32)
    zero16 = jnp.zeros((S, 16), F32)
    zero = jnp.zeros((S, HD - 32), F32)
    c = jnp.concatenate([cos, cos, one], axis=1)
    s1 = jnp.concatenate([-sin, zero16, zero], axis=1)
    s2 = jnp.concatenate([zero16, sin, zero], axis=1)
    return c.astype(BF16), s1.astype(BF16), s2.astype(BF16)


def kernel(x, positions, norm_pre, norm_post, attn_w_in, attn_w_out, pool_w_in, pool_w_grp, pool_b_grp, pool_scale, pool_w_out, loss_target, m_norm_pre, m_norm_post, m_attn_w_in, m_attn_w_out, m_pool_w_in, m_pool_w_grp, m_pool_b_grp, m_pool_scale, m_pool_w_out, v_norm_pre, v_norm_post, v_attn_w_in, v_attn_w_out, v_pool_w_in, v_pool_w_grp, v_pool_b_grp, v_pool_scale, v_pool_w_out):
    S = x.shape[1]
    xi, yi, ci = _mesh_pos()
    dev = 4 * xi + 2 * yi + ci
    x2 = x[0]
    tgt = loss_target[0]

    small = jnp.concatenate([pool_b_grp[0].reshape(2, HD), pool_scale[0].reshape(2, HD),
                             jnp.zeros((4, HD), F32)], axis=0)
    w_in_l = attn_w_in[0].astype(BF16)
    hop1, hop1_token = split_start("gather_w_in_start", [w_in_l], [lax.empty((N_DEV,) + w_in_l.shape, BF16)],
                                   _hop1_plan(), 3)

    pos = positions[0]
    tabs = [_rope_tables(pos.reshape(S // d, d).T.reshape(S)) for d in DIL]
    ehot = (jnp.arange(E)[None, :] // HD == jnp.arange(HD)[:, None]).astype(BF16)
    seg_tiles = SEG // CT

    xn0, xn0_4, xn0_16, xn0t = norm_pre0(x2, norm_pre[0:1], hop1_token)
    xn0s = [xn0, xn0_4.reshape(S, D), xn0_16.reshape(S, D)]
    xn0ts = [xn0t, transpose_rows(xn0s[1], "xn0t_4"), transpose_rows(xn0s[2], "xn0t_16")]

    (w_in_l,), (w_in8,) = split_wait("gather_w_in_wait", hop1, _hop1_plan(), xn0ts[2])
    hop2, hop2_token = split_start("gather_w_in_hop2_start", [w_in8], None, _hop2_plan(D), 4)
    _, (w_in8,) = split_wait("gather_w_in_hop2_wait", hop2, _hop2_plan(D), hop2_token, inplace=True)
    hop3, hop3_token = split_start("gather_w_in_hop3_start", [w_in8], None, _hop3_plan(), 1)
    _, (w_in8,) = split_wait("gather_w_in_hop3_wait", hop3, _hop3_plan(), hop3_token, inplace=True)
    w_in8 = lax.dynamic_update_slice(w_in8, w_in_l[None], (dev, 0, 0))
    small, w_in8 = lax.optimization_barrier((small, w_in8))
    rest_l = [attn_w_out[0].astype(BF16), pool_w_in[0].astype(BF16), pool_w_grp[0].astype(BF16),
              pool_w_out[0].astype(BF16), small]
    rest_flight, rest_token = split_start(
        "gather_rest_start", rest_l, [lax.empty((N_DEV,) + a.shape, a.dtype) for a in rest_l], _peers_plan(), 7)
    Ps, os_, lses = [], [], []
    for g, d in enumerate(DIL):
        P = mm_in(xn0s[g], w_in8, g * seg_tiles, seg_tiles, tabs[g], f"mm_qkv{g}", after=rest_token)
        o, l = attn_fwd(P, g, d)
        Ps.append(P)
        os_.append(o)
        lses.append(l)
    z0 = mm_in(xn0, w_in8, 3 * seg_tiles, E // CT, None, "mm_z0")
    ys, lse3, ya, yat = combine_fwd(os_, lses, z0, ehot)

    rest_l, rest8 = split_wait("gather_rest_wait", rest_flight, _peers_plan(), ya)
    rest8 = [lax.dynamic_update_slice(r8, a[None], (dev,) + (0,) * a.ndim) for r8, a in zip(rest8, rest_l)]
    w_out8, wp_in8, wg8, wp_out8, small8 = rest8
    w_out = w_out8.reshape(E, D)
    wp_out = wp_out8.reshape(E, D)
    wp_in = wp_in8.transpose(1, 0, 2).reshape(D, 2 * E)
    wg = wg8.transpose(1, 0, 2, 3).reshape(4, PC, PC)
    b_full = small8[:, 0:2, :].reshape(N_DEV, 4, PC // N_DEV).transpose(1, 0, 2).reshape(1, E)
    scale_full = small8[:, 2:4, :].reshape(1, E)
    a0 = mm_rows(ya, w_out, "mm_out0", F32)
    h1, xn1, xn1t = post0_pre1(x2, a0, norm_post[0:1], norm_pre[1:2])

    uz = mm_rows(xn1, wp_in, "mm_uz", F32, tm=512)
    pooled, pooled_t = pool_fwd(uz)
    hgrp, yp, ypt = mm_grp(pooled, wg, b_full, scale_full, uz)
    a1 = mm_rows(yp, wp_out, "mm_out1", F32)
    dh2, da1, loss_rows, dg_post1 = post1_loss(h1, a1, tgt, norm_post[1:2])
    loss = lax.psum(0.5 / D * jnp.sum(loss_rows), ("x", "y", "c"))

    dh, duz, dscale_p, db_p = mm_dyp(da1, wp_out, uz, hgrp, scale_full)
    dpooled = mm_dpooled(dh, wg)
    duz = pool_bwd(dpooled, duz)
    g_wg = mm_dwg(pooled_t, dh)
    g_wp_out = mm_wgrad_rows(ypt, da1, "mm_dwp_out")
    g_wp_in = mm_wgrad_cols(xn1t, duz, "mm_dwp_in", shard=PC)
    dxn1 = mm_dx_full(duz, wp_in, "mm_dxn1")
    dh1, da0, dg_pre1, dg_post0 = mid_bwd(dxn1, dh2, h1, a0, norm_pre[1:2], norm_post[0:1])

    dys, dz0 = mm_dya(da0, w_out, z0, ys[0])
    g_w_out = mm_wgrad_rows(yat, da0, "mm_dw_out")
    g_w_in = mm_dw_in_part(xn0t, dz0, 3 * seg_tiles, None, "mm_dw_in_z")
    dPs = []
    for g, d in enumerate(DIL):
        dP = attn_bwd(Ps[g], dys[g], ys[g], lse3[g], tabs[g], g, d)
        g_w_in = mm_dw_in_part(xn0ts[g], dP, g * seg_tiles, g_w_in, f"mm_dw_in{g}")
        dPs.append(dP)

    cidx = ci.astype(jnp.int32).reshape(1)
    chip = (2 * xi + yi).astype(jnp.int32).reshape(1)
    fulls = [g_w_in, g_w_out.reshape(N_DEV, E // N_DEV, D), g_wp_in,
             g_wg.reshape(4, N_DEV, PC // N_DEV, PC).transpose(1, 0, 2, 3).reshape(N_DEV, 4 * PC // N_DEV, PC),
             g_wp_out.reshape(N_DEV, E // N_DEV, D)]
    pair_flight, pair_token = split_start(
        "rs_pair_start", fulls, [lax.empty((4,) + f.shape[1:], F32) for f in fulls], _pair_plan(), 4)
    dx_z = mm_dx_part(dz0, w_in8, 3 * seg_tiles, "mm_dxn0_z", after=pair_token)
    dx_0 = mm_dx_part(dPs[0], w_in8, 0, "mm_dxn0_0", after=dx_z)
    fulls, sibs = split_wait("rs_pair_wait", pair_flight, _pair_plan(), dx_0)
    parts = [pair_add(f, s, cidx, f"pair_add{k}") for k, (f, s) in enumerate(zip(fulls, sibs))]
    chips_flight, chips_token = split_start(
        "rs_chips_start", parts, [jnp.zeros(p.shape, BF16) for p in parts], _chips_plan(), 3)
    dx_1 = mm_dx_part(dPs[1], w_in8, seg_tiles, "mm_dxn0_1", after=chips_token)
    dx_2 = mm_dx_part(dPs[2], w_in8, 2 * seg_tiles, "mm_dxn0_2", after=dx_1)
    grad_x, dg_pre0 = pre0_bwd(dx_0, dx_z, dx_1, dx_2, dh1, x2, norm_pre[0:1])
    parts, recvs = split_wait("rs_chips_wait", chips_flight, _chips_plan(), grad_x)
    shards = [(attn_w_in, m_attn_w_in, v_attn_w_in), (attn_w_out, m_attn_w_out, v_attn_w_out),
              (pool_w_in, m_pool_w_in, v_pool_w_in), (pool_w_grp, m_pool_w_grp, v_pool_w_grp),
              (pool_w_out, m_pool_w_out, v_pool_w_out)]
    big = []
    for k, (recv, part, (w, m, v)) in enumerate(zip(recvs, parts, shards)):
        shp = w.shape
        r2 = recv.shape[1:]
        res = adamw_sum(recv, part, chip, w.reshape(r2), m.reshape(r2), v.reshape(r2), f"adamw{k}")
        big.append([t.reshape(shp) for t in res])

    smalls = jnp.concatenate([dg_pre0.sum(0, keepdims=True), dg_pre1.sum(0, keepdims=True),
                              dg_post0.sum(0, keepdims=True), dg_post1.sum(0, keepdims=True),
                              db_p.sum(0).reshape(2, D), dscale_p.sum(0).reshape(2, D)], axis=0)
    (smalls8,) = all_gather([smalls], "gather_small_grads")
    tot = sum_slots(smalls8, "sum_small_grads")
    g_norm_pre, g_norm_post = tot[0:2], tot[2:4]
    g_b = lax.dynamic_slice_in_dim(tot[4:6].reshape(4, PC), dev * (PC // N_DEV), PC // N_DEV, axis=1)[None]
    g_scale = lax.dynamic_slice_in_dim(tot[6:8].reshape(1, E), dev * (E // N_DEV), E // N_DEV, axis=1)
    sm = [adamw_small(g_norm_pre, norm_pre, m_norm_pre, v_norm_pre, "adamw_norm_pre"),
          adamw_small(g_norm_post, norm_post, m_norm_post, v_norm_post, "adamw_norm_post"),
          adamw_small(g_b, pool_b_grp, m_pool_b_grp, v_pool_b_grp, "adamw_b"),
          adamw_small(g_scale, pool_scale, m_pool_scale, v_pool_scale, "adamw_scale")]

    grads = [g_norm_pre, g_norm_post, big[0][0], big[1][0], big[2][0], big[3][0], g_b, g_scale, big[4][0]]

    def pick(k):
        return [sm[0][k - 1], sm[1][k - 1], big[0][k], big[1][k], big[2][k], big[3][k], sm[2][k - 1], sm[3][k - 1],
                big[4][k]]

    return (loss, grad_x[None], *grads, *pick(1), *pick(2), *pick(3))
```
